```python
import math
import jax, jax.numpy as jnp
from jax import lax
import numpy as np

D_MODEL = 1024
BATCH = 8
SEQ = 8192
DEPTH = 4

CTX_LEN = 256
GRID_W = 64
N_MIXERS = 3
MIXER_POOL = 0
MIXER_ATTN = 1
MIXER_RET = 2
POOL_WINDOWS = (2, 4, 8, 16)
POOL_GROUP = D_MODEL // 4
ATTN_HEADS = 8
ATTN_KV_HEADS = 2
ATTN_HEAD_DIM = D_MODEL // ATTN_HEADS
ATTN_GROUP = ATTN_HEADS // ATTN_KV_HEADS
Q_BLOCK = 128
ROPE_THETA = 10000.0
RET_HEADS = 4
RET_DK = D_MODEL // RET_HEADS
RET_DV = 2 * D_MODEL // RET_HEADS
RET_CHUNK = 128
D_FF = 2816
CONV_WIDTH = 3
EPS = 1e-6

kernel_name = 'hybrid_pool_gqa_retention_convffn_dit'


def _rmsnorm(x, gain):
    xf = x.astype(jnp.float32)
    y = xf * lax.rsqrt(jnp.mean(xf * xf, axis=-1, keepdims=True) + EPS)
    return (y * gain.astype(jnp.float32)).astype(x.dtype)


def _modulate(h, shift, scale):
    return h * (1.0 + scale) + shift


def _pool_mixer(h, w, b, scale):
    s = h.shape[1]
    hf = h.astype(jnp.float32)
    cs = jnp.concatenate([jnp.zeros_like(hf[:, :1]), jnp.cumsum(hf, axis=1)], axis=1)
    t = jnp.arange(s)
    outs = []
    for g, win in enumerate(POOL_WINDOWS):
        lo = jnp.clip(t - win // 2, 0, s)
        hi = jnp.clip(t + win // 2, 0, s)
        sl = slice(g * POOL_GROUP, (g + 1) * POOL_GROUP)
        csg = cs[..., sl]
        mean = (csg[:, hi] - csg[:, lo]) / (hi - lo).astype(jnp.float32)[None, :, None]
        outs.append((mean - hf[..., sl]).astype(h.dtype) @ w[g])
    return (jnp.concatenate(outs, axis=-1) + b) * scale


def _axial_angles(n_tokens):
    rows = n_tokens // GRID_W
    row = jnp.broadcast_to(jnp.arange(rows)[:, None], (rows, GRID_W)).reshape(-1).astype(jnp.float32)
    col = jnp.broadcast_to(jnp.arange(GRID_W)[None, :], (rows, GRID_W)).reshape(-1).astype(jnp.float32)
    axis_dim = ATTN_HEAD_DIM // 2
    inv = ROPE_THETA ** (-jnp.arange(0, axis_dim, 2, dtype=jnp.float32) / axis_dim)
    return row[:, None] * inv, col[:, None] * inv


def _rotate(xa, ang):
    x1, x2 = jnp.split(xa, 2, axis=-1)
    cos = jnp.cos(ang)[None, :, None, :]
    sin = jnp.sin(ang)[None, :, None, :]
    return jnp.concatenate([x1 * cos - x2 * sin, x1 * sin + x2 * cos], axis=-1)


def _apply_axial_rope(x, ang_row, ang_col):
    xr, xc = jnp.split(x.astype(jnp.float32), 2, axis=-1)
    return jnp.concatenate([_rotate(xr, ang_row), _rotate(xc, ang_col)], axis=-1).astype(x.dtype)


def _gqa(q, k, v):
    s = jnp.einsum('bqkgd,bskd->bkgqs', q, k).astype(jnp.float32) * (ATTN_HEAD_DIM ** -0.5)
    p = jax.nn.softmax(s, axis=-1).astype(v.dtype)
    return jnp.einsum('bkgqs,bskd->bqkgd', p, v)


def _attention_mixer(h_lat, h_ctx, w_qkv, q_gain, k_gain, w_o, need_ctx_out):
    def project(h):
        b, s, _ = h.shape
        nq = ATTN_HEADS * ATTN_HEAD_DIM
        nk = ATTN_KV_HEADS * ATTN_HEAD_DIM
        q, k, v = jnp.split(h @ w_qkv, [nq, nq + nk], axis=-1)
        q = _rmsnorm(q.reshape(b, s, ATTN_HEADS, ATTN_HEAD_DIM), q_gain)
        k = _rmsnorm(k.reshape(b, s, ATTN_KV_HEADS, ATTN_HEAD_DIM), k_gain)
        return q, k, v.reshape(b, s, ATTN_KV_HEADS, ATTN_HEAD_DIM)

    b, s = h_lat.shape[:2]
    qc, kc, vc = project(h_ctx)
    ql, kl, vl = project(h_lat)
    ang_r, ang_c = _axial_angles(s)
    ql = _apply_axial_rope(ql, ang_r, ang_c)
    kl = _apply_axial_rope(kl, ang_r, ang_c)
    keys = jnp.concatenate([kl, kc], axis=1)
    vals = jnp.concatenate([vl, vc], axis=1)
    q_blocks = ql.reshape(b, s // Q_BLOCK, Q_BLOCK, ATTN_KV_HEADS, ATTN_GROUP, ATTN_HEAD_DIM)
    q_blocks = q_blocks.transpose(1, 0, 2, 3, 4, 5)
    o = lax.map(lambda qb: _gqa(qb, keys, vals), q_blocks)
    o = o.transpose(1, 0, 2, 3, 4, 5).reshape(b, s, ATTN_HEADS * ATTN_HEAD_DIM)
    y_lat = o @ w_o
    y_ctx = None
    if need_ctx_out:
        l = h_ctx.shape[1]
        oc = _gqa(qc.reshape(b, l, ATTN_KV_HEADS, ATTN_GROUP, ATTN_HEAD_DIM), kc, vc)
        y_ctx = oc.reshape(b, l, ATTN_HEADS * ATTN_HEAD_DIM) @ w_o
    return y_lat, y_ctx


def _retention_chunks(q, k, v, log_gamma, state):
    b, h, s, _ = q.shape
    dv = v.shape[-1]
    n = s // RET_CHUNK

    def chunks(a):
        return jnp.moveaxis(a.reshape(b, h, n, RET_CHUNK, a.shape[-1]), 2, 0)

    idx = jnp.arange(RET_CHUNK, dtype=jnp.float32)
    diff = idx[:, None] - idx[None, :]
    intra = jnp.where(diff >= 0, jnp.exp(jnp.maximum(diff, 0.0) * log_gamma[:, None, None]), 0.0)
    q_dec = jnp.exp((idx + 1.0) * log_gamma[:, None])[None, :, :, None]
    k_dec = jnp.exp((RET_CHUNK - 1.0 - idx) * log_gamma[:, None])[None, :, :, None]
    chunk_dec = jnp.exp(RET_CHUNK * log_gamma)[None, :, None, None]

    def step(r, qkv):
        qc, kc, vc = qkv
        att = jnp.einsum('bhid,bhjd->bhij', qc, kc) * intra
        o = jnp.einsum('bhij,bhje->bhie', att, vc) + jnp.einsum('bhid,bhde->bhie', qc, r) * q_dec
        r = r * chunk_dec + jnp.einsum('bhjd,bhje->bhde', kc * k_dec, vc)
        return r, o

    r, o = lax.scan(step, state, (chunks(q), chunks(k), chunks(v)))
    return jnp.moveaxis(o, 0, 2).reshape(b, h, s, dv), r


def _final_state(k, v, log_gamma):
    l = k.shape[2]
    w = jnp.exp((l - 1.0 - jnp.arange(l, dtype=jnp.float32)) * log_gamma[:, None])
    return jnp.einsum('bhjd,bhje->bhde', k * w[None, :, :, None], v)


def _retention_mixer(h_lat, h_ctx, w_in, decay_logit, gn_w, w_out, need_ctx_out):
    log_gamma = jax.nn.log_sigmoid(decay_logit.astype(jnp.float32))
    nq = RET_HEADS * RET_DK
    nv = RET_HEADS * RET_DV

    def project(h):
        b, s, _ = h.shape
        q, k, v, g = jnp.split(h @ w_in, [nq, 2 * nq, 2 * nq + nv], axis=-1)

        def heads(a, d):
            return a.reshape(b, s, RET_HEADS, d).transpose(0, 2, 1, 3).astype(jnp.float32)
        return heads(q, RET_DK), heads(k, RET_DK) * (RET_DK ** -0.5), heads(v, RET_DV), g

    def flip(a):
        return jnp.flip(a, axis=2)

    def readout(y, g):
        b, h, s, dv = y.shape
        mu = jnp.mean(y, axis=-1, keepdims=True)
        var = jnp.mean(jnp.square(y - mu), axis=-1, keepdims=True)
        yn = (y - mu) * lax.rsqrt(var + EPS) * gn_w.astype(jnp.float32).reshape(1, h, 1, dv)
        yn = yn.transpose(0, 2, 1, 3).reshape(b, s, h * dv).astype(g.dtype)
        return (jax.nn.silu(g) * yn) @ w_out

    y_ctx = None
    if need_ctx_out:
        qc, kc, vc, gc = project(h_ctx)
        zero = jnp.zeros(qc.shape[:2] + (RET_DK, RET_DV), jnp.float32)
        oc_f, r_f = _retention_chunks(qc, kc, vc, log_gamma[0], zero)
        oc_b, r_b = _retention_chunks(flip(qc), flip(kc), flip(vc), log_gamma[1], zero)
        y_ctx = readout(oc_f + flip(oc_b), gc)
    else:
        b, l, _ = h_ctx.shape
        kv_c = (h_ctx @ w_in[:, nq:2 * nq + nv])
        kc = kv_c[..., :nq].reshape(b, l, RET_HEADS, RET_DK).transpose(0, 2, 1, 3).astype(jnp.float32) * (RET_DK ** -0.5)
        vc = kv_c[..., nq:].reshape(b, l, RET_HEADS, RET_DV).transpose(0, 2, 1, 3).astype(jnp.float32)
        r_f = _final_state(kc, vc, log_gamma[0])
        r_b = _final_state(flip(kc), flip(vc), log_gamma[1])
    ql, kl, vl, gl = project(h_lat)
    ol_f, _ = _retention_chunks(ql, kl, vl, log_gamma[0], r_f)
    ol_b, _ = _retention_chunks(flip(ql), flip(kl), flip(vl), log_gamma[1], r_b)
    y_lat = readout(ol_f + flip(ol_b), gl)
    return y_lat, y_ctx


def _conv_ffn(h, w_up, conv_w, conv_b, w_down):
    u = h @ w_up
    up = jnp.pad(u, ((0, 0), (1, 1), (0, 0)))
    u = up[:, :-2] * conv_w[0] + up[:, 1:-1] * conv_w[1] + up[:, 2:] * conv_w[2] + conv_b
    a, v = jnp.split(u, 2, axis=-1)
    return (jax.nn.silu(a) * v) @ w_down


def _layer_counts():
    kinds = [i % N_MIXERS for i in range(DEPTH)]
    return kinds.count(MIXER_POOL), kinds.count(MIXER_ATTN), kinds.count(MIXER_RET)


def _fwd_setup_inputs(seed: int = 0) -> dict:
    key = jax.random.key(seed)
    ks = jax.random.split(key, 22)
    n_pool, n_attn, n_ret = _layer_counts()
    d = D_MODEL

    def nrm(k, shape):
        return jax.random.normal(k, shape, jnp.float32)

    def w(k, shape, fan_in, gain=1.0):
        return nrm(k, shape) * (gain * fan_in ** -0.5)

    def ones_noise(k, shape):
        return 1.0 + 0.05 * nrm(k, shape)

    decay_base = jnp.asarray(np.log(2.0 ** (5 + np.arange(RET_HEADS)) - 1.0).astype(np.float32))
    return {
        'x': nrm(ks[0], (BATCH, SEQ, d)),
        'c': nrm(ks[1], (BATCH, d)),
        'ctx': nrm(ks[2], (BATCH, CTX_LEN, d)),
        'c_ctx': nrm(ks[3], (d,)),
        'ada_w': w(ks[4], (DEPTH, d, 6 * d), d, 0.5),
        'ada_b': 0.01 * nrm(ks[5], (DEPTH, 6 * d)),
        'norm_w': ones_noise(ks[6], (DEPTH, 2, d)),
        'pool_w': w(ks[7], (n_pool, 4, POOL_GROUP, POOL_GROUP), POOL_GROUP),
        'pool_b': 0.01 * nrm(ks[8], (n_pool, d)),
        'pool_scale': ones_noise(ks[9], (n_pool, d)),
        'attn_w_qkv': w(ks[10], (n_attn, d, (ATTN_HEADS + 2 * ATTN_KV_HEADS) * ATTN_HEAD_DIM), d),
        'attn_q_gain': ones_noise(ks[11], (n_attn, ATTN_HEAD_DIM)),
        'attn_k_gain': ones_noise(ks[12], (n_attn, ATTN_HEAD_DIM)),
        'attn_w_o': w(ks[13], (n_attn, ATTN_HEADS * ATTN_HEAD_DIM, d), ATTN_HEADS * ATTN_HEAD_DIM),
        'ret_w_in': w(ks[14], (n_ret, d, 2 * RET_HEADS * RET_DK + 2 * RET_HEADS * RET_DV), d),
        'ret_decay_logit': decay_base[None, None, :] + 0.1 * nrm(ks[15], (n_ret, 2, RET_HEADS)),
        'ret_gn_w': ones_noise(ks[16], (n_ret, RET_HEADS * RET_DV)),
        'ret_w_out': w(ks[17], (n_ret, RET_HEADS * RET_DV, d), RET_HEADS * RET_DV),
        'ffn_w_up': w(ks[18], (DEPTH, d, 2 * D_FF), d),
        'ffn_conv_w': w(ks[19], (DEPTH, CONV_WIDTH, 2 * D_FF), CONV_WIDTH),
        'ffn_conv_b': 0.01 * nrm(ks[20], (DEPTH, 2 * D_FF)),
        'ffn_w_down': w(ks[21], (DEPTH, D_FF, d), D_FF),
    }


def _fwd_reference(x, c, ctx, c_ctx, ada_w, ada_b, norm_w, pool_w, pool_b, pool_scale,
              attn_w_qkv, attn_q_gain, attn_k_gain, attn_w_o,
              ret_w_in, ret_decay_logit, ret_gn_w, ret_w_out,
              ffn_w_up, ffn_conv_w, ffn_conv_b, ffn_w_down):
    ctx_s = ctx
    silu_c = jax.nn.silu(c)
    silu_cc = jax.nn.silu(c_ctx)
    for i in range(DEPTH):
        kind = i % N_MIXERS
        j = i // N_MIXERS
        need_ctx_out = any(k % N_MIXERS != MIXER_POOL for k in range(i + 1, DEPTH))
        need_ctx_in = need_ctx_out or kind != MIXER_POOL

        sh1, sc1, g1, sh2, sc2, g2 = [m[:, None, :] for m in jnp.split(silu_c @ ada_w[i] + ada_b[i], 6, axis=-1)]
        h = _modulate(_rmsnorm(x, norm_w[i, 0]), sh1, sc1)
        hc = None
        if need_ctx_in:
            csh1, csc1, cg1, csh2, csc2, cg2 = jnp.split(silu_cc @ ada_w[i] + ada_b[i], 6, axis=-1)
            hc = _modulate(_rmsnorm(ctx_s, norm_w[i, 0]), csh1, csc1)

        if kind == MIXER_POOL:
            y = _pool_mixer(h, pool_w[j], pool_b[j], pool_scale[j])
            y_c = _pool_mixer(hc, pool_w[j], pool_b[j], pool_scale[j]) if need_ctx_out else None
        elif kind == MIXER_ATTN:
            y, y_c = _attention_mixer(h, hc, attn_w_qkv[j], attn_q_gain[j], attn_k_gain[j], attn_w_o[j], need_ctx_out)
        else:
            y, y_c = _retention_mixer(h, hc, ret_w_in[j], ret_decay_logit[j], ret_gn_w[j], ret_w_out[j], need_ctx_out)

        x = x + g1 * y
        x = x + g2 * _conv_ffn(_modulate(_rmsnorm(x, norm_w[i, 1]), sh2, sc2),
                               ffn_w_up[i], ffn_conv_w[i], ffn_conv_b[i], ffn_w_down[i])
        if need_ctx_out:
            ctx_s = ctx_s + cg1 * y_c
            ctx_s = ctx_s + cg2 * _conv_ffn(_modulate(_rmsnorm(ctx_s, norm_w[i, 1]), csh2, csc2),
                                            ffn_w_up[i], ffn_conv_w[i], ffn_conv_b[i], ffn_w_down[i])
    return x


import jax as _jax
import jax.numpy as _jnp

TWIN_FORMAT = 'train_step'
FWD_PARAMS = ['x', 'c', 'ctx', 'c_ctx', 'ada_w', 'ada_b', 'norm_w', 'pool_w', 'pool_b', 'pool_scale', 'attn_w_qkv', 'attn_q_gain', 'attn_k_gain', 'attn_w_o', 'ret_w_in', 'ret_decay_logit', 'ret_gn_w', 'ret_w_out', 'ffn_w_up', 'ffn_conv_w', 'ffn_conv_b', 'ffn_w_down']
TWIN_WEIGHTS = ['c_ctx', 'ada_w', 'ada_b', 'norm_w', 'pool_w', 'pool_b', 'pool_scale', 'attn_w_qkv', 'attn_q_gain', 'attn_k_gain', 'attn_w_o', 'ret_w_in', 'ret_decay_logit', 'ret_gn_w', 'ret_w_out', 'ffn_w_up', 'ffn_conv_w', 'ffn_conv_b', 'ffn_w_down']
TWIN_DIFF_INPUT = 'x'
TWIN_INPUTS = ['x', 'c', 'ctx', 'c_ctx', 'ada_w', 'ada_b', 'norm_w', 'pool_w', 'pool_b', 'pool_scale', 'attn_w_qkv', 'attn_q_gain', 'attn_k_gain', 'attn_w_o', 'ret_w_in', 'ret_decay_logit', 'ret_gn_w', 'ret_w_out', 'ffn_w_up', 'ffn_conv_w', 'ffn_conv_b', 'ffn_w_down', 'loss_target', 'm_c_ctx', 'm_ada_w', 'm_ada_b', 'm_norm_w', 'm_pool_w', 'm_pool_b', 'm_pool_scale', 'm_attn_w_qkv', 'm_attn_q_gain', 'm_attn_k_gain', 'm_attn_w_o', 'm_ret_w_in', 'm_ret_decay_logit', 'm_ret_gn_w', 'm_ret_w_out', 'm_ffn_w_up', 'm_ffn_conv_w', 'm_ffn_conv_b', 'm_ffn_w_down', 'v_c_ctx', 'v_ada_w', 'v_ada_b', 'v_norm_w', 'v_pool_w', 'v_pool_b', 'v_pool_scale', 'v_attn_w_qkv', 'v_attn_q_gain', 'v_attn_k_gain', 'v_attn_w_o', 'v_ret_w_in', 'v_ret_decay_logit', 'v_ret_gn_w', 'v_ret_w_out', 'v_ffn_w_up', 'v_ffn_conv_w', 'v_ffn_conv_b', 'v_ffn_w_down']
TWIN_OUTPUTS = ['loss', 'grad_x', 'grad_c_ctx', 'grad_ada_w', 'grad_ada_b', 'grad_norm_w', 'grad_pool_w', 'grad_pool_b', 'grad_pool_scale', 'grad_attn_w_qkv', 'grad_attn_q_gain', 'grad_attn_k_gain', 'grad_attn_w_o', 'grad_ret_w_in', 'grad_ret_decay_logit', 'grad_ret_gn_w', 'grad_ret_w_out', 'grad_ffn_w_up', 'grad_ffn_conv_w', 'grad_ffn_conv_b', 'grad_ffn_w_down', 'delta_c_ctx', 'delta_ada_w', 'delta_ada_b', 'delta_norm_w', 'delta_pool_w', 'delta_pool_b', 'delta_pool_scale', 'delta_attn_w_qkv', 'delta_attn_q_gain', 'delta_attn_k_gain', 'delta_attn_w_o', 'delta_ret_w_in', 'delta_ret_decay_logit', 'delta_ret_gn_w', 'delta_ret_w_out', 'delta_ffn_w_up', 'delta_ffn_conv_w', 'delta_ffn_conv_b', 'delta_ffn_w_down', 'new_m_c_ctx', 'new_m_ada_w', 'new_m_ada_b', 'new_m_norm_w', 'new_m_pool_w', 'new_m_pool_b', 'new_m_pool_scale', 'new_m_attn_w_qkv', 'new_m_attn_q_gain', 'new_m_attn_k_gain', 'new_m_attn_w_o', 'new_m_ret_w_in', 'new_m_ret_decay_logit', 'new_m_ret_gn_w', 'new_m_ret_w_out', 'new_m_ffn_w_up', 'new_m_ffn_conv_w', 'new_m_ffn_conv_b', 'new_m_ffn_w_down', 'new_v_c_ctx', 'new_v_ada_w', 'new_v_ada_b', 'new_v_norm_w', 'new_v_pool_w', 'new_v_pool_b', 'new_v_pool_scale', 'new_v_attn_w_qkv', 'new_v_attn_q_gain', 'new_v_attn_k_gain', 'new_v_attn_w_o', 'new_v_ret_w_in', 'new_v_ret_decay_logit', 'new_v_ret_gn_w', 'new_v_ret_w_out', 'new_v_ffn_w_up', 'new_v_ffn_conv_w', 'new_v_ffn_conv_b', 'new_v_ffn_w_down']
TWIN_LEAF_KINDS = {'loss': 'loss', 'grad_x': 'grad_x', 'grad_c_ctx': 'grad_w', 'grad_ada_w': 'grad_w', 'grad_ada_b': 'grad_w', 'grad_norm_w': 'grad_w', 'grad_pool_w': 'grad_w', 'grad_pool_b': 'grad_w', 'grad_pool_scale': 'grad_w', 'grad_attn_w_qkv': 'grad_w', 'grad_attn_q_gain': 'grad_w', 'grad_attn_k_gain': 'grad_w', 'grad_attn_w_o': 'grad_w', 'grad_ret_w_in': 'grad_w', 'grad_ret_decay_logit': 'grad_w', 'grad_ret_gn_w': 'grad_w', 'grad_ret_w_out': 'grad_w', 'grad_ffn_w_up': 'grad_w', 'grad_ffn_conv_w': 'grad_w', 'grad_ffn_conv_b': 'grad_w', 'grad_ffn_w_down': 'grad_w', 'delta_c_ctx': 'delta_w', 'delta_ada_w': 'delta_w', 'delta_ada_b': 'delta_w', 'delta_norm_w': 'delta_w', 'delta_pool_w': 'delta_w', 'delta_pool_b': 'delta_w', 'delta_pool_scale': 'delta_w', 'delta_attn_w_qkv': 'delta_w', 'delta_attn_q_gain': 'delta_w', 'delta_attn_k_gain': 'delta_w', 'delta_attn_w_o': 'delta_w', 'delta_ret_w_in': 'delta_w', 'delta_ret_decay_logit': 'delta_w', 'delta_ret_gn_w': 'delta_w', 'delta_ret_w_out': 'delta_w', 'delta_ffn_w_up': 'delta_w', 'delta_ffn_conv_w': 'delta_w', 'delta_ffn_conv_b': 'delta_w', 'delta_ffn_w_down': 'delta_w', 'new_m_c_ctx': 'new_m', 'new_m_ada_w': 'new_m', 'new_m_ada_b': 'new_m', 'new_m_norm_w': 'new_m', 'new_m_pool_w': 'new_m', 'new_m_pool_b': 'new_m', 'new_m_pool_scale': 'new_m', 'new_m_attn_w_qkv': 'new_m', 'new_m_attn_q_gain': 'new_m', 'new_m_attn_k_gain': 'new_m', 'new_m_attn_w_o': 'new_m', 'new_m_ret_w_in': 'new_m', 'new_m_ret_decay_logit': 'new_m', 'new_m_ret_gn_w': 'new_m', 'new_m_ret_w_out': 'new_m', 'new_m_ffn_w_up': 'new_m', 'new_m_ffn_conv_w': 'new_m', 'new_m_ffn_conv_b': 'new_m', 'new_m_ffn_w_down': 'new_m', 'new_v_c_ctx': 'new_v', 'new_v_ada_w': 'new_v', 'new_v_ada_b': 'new_v', 'new_v_norm_w': 'new_v', 'new_v_pool_w': 'new_v', 'new_v_pool_b': 'new_v', 'new_v_pool_scale': 'new_v', 'new_v_attn_w_qkv': 'new_v', 'new_v_attn_q_gain': 'new_v', 'new_v_attn_k_gain': 'new_v', 'new_v_attn_w_o': 'new_v', 'new_v_ret_w_in': 'new_v', 'new_v_ret_decay_logit': 'new_v', 'new_v_ret_gn_w': 'new_v', 'new_v_ret_w_out': 'new_v', 'new_v_ffn_w_up': 'new_v', 'new_v_ffn_conv_w': 'new_v', 'new_v_ffn_conv_b': 'new_v', 'new_v_ffn_w_down': 'new_v'}


def _forward(args):
    return _fwd_reference(*[args[k] for k in FWD_PARAMS])


def _output_shape():
    def fwd():
        inp = _fwd_setup_inputs(0)
        return _fwd_reference(*[inp[k] for k in FWD_PARAMS])
    out = _jax.eval_shape(fwd)
    return out.shape, out.dtype

N_MICROBATCH = 1
ADAM_LR = 0.001
ADAM_B1 = 0.9
ADAM_B2 = 0.999
ADAM_EPS = 1e-08
ADAM_WD = 0.01
ADAM_STEP = 10
PER_EXAMPLE_BATCH_AXIS = {'x': 0, 'c': 0, 'ctx': 0, 'loss_target': 0}
SHARED_INPUTS = []
_WEIGHT_DTYPES = {'c_ctx': _jnp.float32, 'ada_w': _jnp.float32, 'ada_b': _jnp.float32, 'norm_w': _jnp.float32, 'pool_w': _jnp.float32, 'pool_b': _jnp.float32, 'pool_scale': _jnp.float32, 'attn_w_qkv': _jnp.float32, 'attn_q_gain': _jnp.float32, 'attn_k_gain': _jnp.float32, 'attn_w_o': _jnp.float32, 'ret_w_in': _jnp.float32, 'ret_decay_logit': _jnp.float32, 'ret_gn_w': _jnp.float32, 'ret_w_out': _jnp.float32, 'ffn_w_up': _jnp.float32, 'ffn_conv_w': _jnp.float32, 'ffn_conv_b': _jnp.float32, 'ffn_w_down': _jnp.float32}
MOMENT_SCALE = {'c_ctx': 1.108996e-01, 'ada_w': 1.747136e+00, 'ada_b': 4.550487e+00, 'norm_w': 5.437889e+00, 'pool_w': 4.927974e-01, 'pool_b': 9.510951e-01, 'pool_scale': 5.750868e+00, 'attn_w_qkv': 3.081188e-01, 'attn_q_gain': 6.932013e-02, 'attn_k_gain': 6.941302e-02, 'attn_w_o': 3.345577e-01, 'ret_w_in': 1.152832e-01, 'ret_decay_logit': 3.682740e-01, 'ret_gn_w': 1.318478e+00, 'ret_w_out': 1.143366e-01, 'ffn_w_up': 1.634443e-01, 'ffn_conv_w': 9.868173e-01, 'ffn_conv_b': 8.046046e-01, 'ffn_w_down': 1.413949e-01}


def _to_microbatches(a, axis):
    t = _jnp.moveaxis(a, axis, 0)
    t = t.reshape((N_MICROBATCH, t.shape[0] // N_MICROBATCH) + t.shape[1:])
    return _jnp.moveaxis(t, 1, axis + 1)


def setup_inputs(seed: int = 0) -> dict:
    inp = _fwd_setup_inputs(seed)
    key = _jax.random.fold_in(_jax.random.key(seed), 7919)
    shape, _ = _output_shape()
    out = dict(inp)
    out["loss_target"] = _jax.random.normal(_jax.random.fold_in(key, 0), shape, _jnp.float32)
    for i, name in enumerate(TWIN_WEIGHTS):
        w = inp[name].astype(_jnp.float32)
        if MOMENT_SCALE is None:
            s = _jnp.sqrt(_jnp.mean(_jnp.square(w)) + 1e-30)
        else:
            s = MOMENT_SCALE[name]
        km, kv = _jax.random.split(_jax.random.fold_in(key, i + 1))
        out[name] = w
        out["m_" + name] = s * _jax.random.normal(km, w.shape, _jnp.float32)
        out["v_" + name] = (s * s) * _jax.random.uniform(kv, w.shape, _jnp.float32, 0.5, 1.5)
    if N_MICROBATCH > 1:
        for name, axis in PER_EXAMPLE_BATCH_AXIS.items():
            out[name] = _to_microbatches(out[name], axis)
    return {'x': out['x'], 'c': out['c'], 'ctx': out['ctx'], 'c_ctx': out['c_ctx'], 'ada_w': out['ada_w'], 'ada_b': out['ada_b'], 'norm_w': out['norm_w'], 'pool_w': out['pool_w'], 'pool_b': out['pool_b'], 'pool_scale': out['pool_scale'], 'attn_w_qkv': out['attn_w_qkv'], 'attn_q_gain': out['attn_q_gain'], 'attn_k_gain': out['attn_k_gain'], 'attn_w_o': out['attn_w_o'], 'ret_w_in': out['ret_w_in'], 'ret_decay_logit': out['ret_decay_logit'], 'ret_gn_w': out['ret_gn_w'], 'ret_w_out': out['ret_w_out'], 'ffn_w_up': out['ffn_w_up'], 'ffn_conv_w': out['ffn_conv_w'], 'ffn_conv_b': out['ffn_conv_b'], 'ffn_w_down': out['ffn_w_down'], 'loss_target': out['loss_target'], 'm_c_ctx': out['m_c_ctx'], 'm_ada_w': out['m_ada_w'], 'm_ada_b': out['m_ada_b'], 'm_norm_w': out['m_norm_w'], 'm_pool_w': out['m_pool_w'], 'm_pool_b': out['m_pool_b'], 'm_pool_scale': out['m_pool_scale'], 'm_attn_w_qkv': out['m_attn_w_qkv'], 'm_attn_q_gain': out['m_attn_q_gain'], 'm_attn_k_gain': out['m_attn_k_gain'], 'm_attn_w_o': out['m_attn_w_o'], 'm_ret_w_in': out['m_ret_w_in'], 'm_ret_decay_logit': out['m_ret_decay_logit'], 'm_ret_gn_w': out['m_ret_gn_w'], 'm_ret_w_out': out['m_ret_w_out'], 'm_ffn_w_up': out['m_ffn_w_up'], 'm_ffn_conv_w': out['m_ffn_conv_w'], 'm_ffn_conv_b': out['m_ffn_conv_b'], 'm_ffn_w_down': out['m_ffn_w_down'], 'v_c_ctx': out['v_c_ctx'], 'v_ada_w': out['v_ada_w'], 'v_ada_b': out['v_ada_b'], 'v_norm_w': out['v_norm_w'], 'v_pool_w': out['v_pool_w'], 'v_pool_b': out['v_pool_b'], 'v_pool_scale': out['v_pool_scale'], 'v_attn_w_qkv': out['v_attn_w_qkv'], 'v_attn_q_gain': out['v_attn_q_gain'], 'v_attn_k_gain': out['v_attn_k_gain'], 'v_attn_w_o': out['v_attn_w_o'], 'v_ret_w_in': out['v_ret_w_in'], 'v_ret_decay_logit': out['v_ret_decay_logit'], 'v_ret_gn_w': out['v_ret_gn_w'], 'v_ret_w_out': out['v_ret_w_out'], 'v_ffn_w_up': out['v_ffn_w_up'], 'v_ffn_conv_w': out['v_ffn_conv_w'], 'v_ffn_conv_b': out['v_ffn_conv_b'], 'v_ffn_w_down': out['v_ffn_w_down']}


def _loss(weights, diff, rest, loss_target):
    with _jax.named_scope("forward"):
        args = {**rest, TWIN_DIFF_INPUT: diff, **{k: w.astype(_WEIGHT_DTYPES[k]) for k, w in weights.items()}}
        y = _forward(args)
    with _jax.named_scope("loss_head"):
        err = _jnp.square(y.astype(_jnp.float32) - loss_target)
        return 0.5 * _jnp.sum(_jnp.mean(err, axis=-1)) if err.ndim else 0.5 * err


def _adamw(w, g, m, v):
    m = ADAM_B1 * m + (1.0 - ADAM_B1) * g
    v = ADAM_B2 * v + (1.0 - ADAM_B2) * _jnp.square(g)
    m_hat = m / (1.0 - ADAM_B1 ** ADAM_STEP)
    v_hat = v / (1.0 - ADAM_B2 ** ADAM_STEP)
    delta = -ADAM_LR * (m_hat / (_jnp.sqrt(v_hat) + ADAM_EPS) + ADAM_WD * w)
    return delta, m, v


def reference(x, c, ctx, c_ctx, ada_w, ada_b, norm_w, pool_w, pool_b, pool_scale, attn_w_qkv, attn_q_gain, attn_k_gain, attn_w_o, ret_w_in, ret_decay_logit, ret_gn_w, ret_w_out, ffn_w_up, ffn_conv_w, ffn_conv_b, ffn_w_down, loss_target, m_c_ctx, m_ada_w, m_ada_b, m_norm_w, m_pool_w, m_pool_b, m_pool_scale, m_attn_w_qkv, m_attn_q_gain, m_attn_k_gain, m_attn_w_o, m_ret_w_in, m_ret_decay_logit, m_ret_gn_w, m_ret_w_out, m_ffn_w_up, m_ffn_conv_w, m_ffn_conv_b, m_ffn_w_down, v_c_ctx, v_ada_w, v_ada_b, v_norm_w, v_pool_w, v_pool_b, v_pool_scale, v_attn_w_qkv, v_attn_q_gain, v_attn_k_gain, v_attn_w_o, v_ret_w_in, v_ret_decay_logit, v_ret_gn_w, v_ret_w_out, v_ffn_w_up, v_ffn_conv_w, v_ffn_conv_b, v_ffn_w_down):
    given = dict(x=x, c=c, ctx=ctx, c_ctx=c_ctx, ada_w=ada_w, ada_b=ada_b, norm_w=norm_w, pool_w=pool_w, pool_b=pool_b, pool_scale=pool_scale, attn_w_qkv=attn_w_qkv, attn_q_gain=attn_q_gain, attn_k_gain=attn_k_gain, attn_w_o=attn_w_o, ret_w_in=ret_w_in, ret_decay_logit=ret_decay_logit, ret_gn_w=ret_gn_w, ret_w_out=ret_w_out, ffn_w_up=ffn_w_up, ffn_conv_w=ffn_conv_w, ffn_conv_b=ffn_conv_b, ffn_w_down=ffn_w_down, loss_target=loss_target, m_c_ctx=m_c_ctx, m_ada_w=m_ada_w, m_ada_b=m_ada_b, m_norm_w=m_norm_w, m_pool_w=m_pool_w, m_pool_b=m_pool_b, m_pool_scale=m_pool_scale, m_attn_w_qkv=m_attn_w_qkv, m_attn_q_gain=m_attn_q_gain, m_attn_k_gain=m_attn_k_gain, m_attn_w_o=m_attn_w_o, m_ret_w_in=m_ret_w_in, m_ret_decay_logit=m_ret_decay_logit, m_ret_gn_w=m_ret_gn_w, m_ret_w_out=m_ret_w_out, m_ffn_w_up=m_ffn_w_up, m_ffn_conv_w=m_ffn_conv_w, m_ffn_conv_b=m_ffn_conv_b, m_ffn_w_down=m_ffn_w_down, v_c_ctx=v_c_ctx, v_ada_w=v_ada_w, v_ada_b=v_ada_b, v_norm_w=v_norm_w, v_pool_w=v_pool_w, v_pool_b=v_pool_b, v_pool_scale=v_pool_scale, v_attn_w_qkv=v_attn_w_qkv, v_attn_q_gain=v_attn_q_gain, v_attn_k_gain=v_attn_k_gain, v_attn_w_o=v_attn_w_o, v_ret_w_in=v_ret_w_in, v_ret_decay_logit=v_ret_decay_logit, v_ret_gn_w=v_ret_gn_w, v_ret_w_out=v_ret_w_out, v_ffn_w_up=v_ffn_w_up, v_ffn_conv_w=v_ffn_conv_w, v_ffn_conv_b=v_ffn_conv_b, v_ffn_w_down=v_ffn_w_down)
    weights = {n: given[n] for n in TWIN_WEIGHTS}
    shared = {n: given[n] for n in SHARED_INPUTS}
    per_example = {n: given[n] for n in ['x', 'c', 'ctx']}
    grad_fn = _jax.value_and_grad(_loss, argnums=(0, 1))

    def one_microbatch(ex, loss_target):
        ex = dict(ex)
        diff = ex.pop(TWIN_DIFF_INPUT)
        return grad_fn(weights, diff, {**shared, **ex}, loss_target)

    if N_MICROBATCH == 1:
        loss, (grad_w, grad_x) = one_microbatch(per_example, given["loss_target"])
    else:
        def body(carry, xs):
            loss_sum, grad_sum = carry
            l_k, (gw_k, gx_k) = one_microbatch(xs[0], xs[1])
            with _jax.named_scope("update"):
                return (loss_sum + l_k, _jax.tree.map(_jnp.add, grad_sum, gw_k)), gx_k

        init = (_jnp.zeros((), _jnp.float32), _jax.tree.map(_jnp.zeros_like, weights))
        (loss, grad_w), grad_x = _jax.lax.scan(body, init, (per_example, given["loss_target"]))
    with _jax.named_scope("update"):
        delta_w, new_m, new_v = {}, {}, {}
        for n in TWIN_WEIGHTS:
            delta_w[n], new_m[n], new_v[n] = _adamw(weights[n], grad_w[n], given["m_" + n], given["v_" + n])
    return (loss, grad_x, *[grad_w[n] for n in TWIN_WEIGHTS], *[delta_w[n] for n in TWIN_WEIGHTS],
            *[new_m[n] for n in TWIN_WEIGHTS], *[new_v[n] for n in TWIN_WEIGHTS])
```

```python
import functools
import math

import jax
import jax.numpy as jnp
from jax import lax
from jax.experimental import pallas as pl
from jax.experimental.pallas import tpu as pltpu

F32 = jnp.float32
BF16 = jnp.bfloat16
SDS = jax.ShapeDtypeStruct
MESH = pl.DeviceIdType.MESH

N_DEV = 8
EPS = 1e-6
DEPTH = 4
GRID_W = 64
POOL_WINDOWS = (2, 4, 8, 16)
N_HEADS = 8
N_KV = 2
HEAD_DIM = 128
ROPE_THETA = 10000.0
RET_HEADS = 4
RET_DK = 256
RET_DV = 512
RET_CHUNK = 128
ADAM_LR = 0.001
ADAM_B1 = 0.9
ADAM_B2 = 0.999
ADAM_EPS = 1e-08
ADAM_WD = 0.01
ADAM_STEP = 10

ROW_TILE = 256
HALO = 8
VMEM_LIMIT_V7X = 56 * 1024 * 1024


def _params(n_axes=0):
    sem = ("arbitrary",) * n_axes if n_axes else None
    return pltpu.CompilerParams(dimension_semantics=sem, vmem_limit_bytes=VMEM_LIMIT_V7X)


def _pick(n, cap, mult):
    best = None
    for d in range(mult, min(n, cap) + 1, mult):
        if n % d == 0:
            best = d
    return best if best is not None else n


def _dot(a, b):
    return jnp.dot(a, b, preferred_element_type=F32)


def _dot_nt(a, b):
    return lax.dot_general(a, b, (((1,), (1,)), ((), ())), preferred_element_type=F32)


def _dot_tn(a, b):
    return lax.dot_general(a, b, (((0,), (0,)), ((), ())), preferred_element_type=F32)


def _bf(v):
    return v.astype(BF16)


def _sigmoid(v):
    return 1.0 / (1.0 + jnp.exp(-v))


def _mm(a, b, mode, out_dtype, name):
    if mode == "nn":
        (M, K), (K2, N) = a.shape, b.shape
    elif mode == "nt":
        (M, K), (N, K2) = a.shape, b.shape
    else:
        (K, M), (K2, N) = a.shape, b.shape
    assert K == K2, (a.shape, b.shape, mode)
    if mode == "tn":
        tm, tk = _pick(M, 1408, 128), _pick(K, 768, 16)
    else:
        tm, tk = _pick(M, 768, 16), _pick(K, 2816, 128)
    tn = _pick(N, 512, 128)
    nk = K // tk
    if mode == "nn":
        a_spec = pl.BlockSpec((tm, tk), lambda i, j, k: (i, k))
        b_spec = pl.BlockSpec((tk, tn), lambda i, j, k: (k, j))
    elif mode == "nt":
        a_spec = pl.BlockSpec((tm, tk), lambda i, j, k: (i, k))
        b_spec = pl.BlockSpec((tn, tk), lambda i, j, k: (j, k))
    else:
        a_spec = pl.BlockSpec((tk, tm), lambda i, j, k: (k, i))
        b_spec = pl.BlockSpec((tk, tn), lambda i, j, k: (k, j))
    dot = {"nn": _dot, "nt": _dot_nt, "tn": _dot_tn}[mode]

    def body(a_ref, b_ref, o_ref, acc_ref):
        part = dot(_bf(a_ref[...]), _bf(b_ref[...]))
        if nk == 1:
            o_ref[...] = part.astype(out_dtype)
        else:
            k = pl.program_id(2)

            @pl.when(k == 0)
            def _():
                acc_ref[...] = part

            @pl.when(k > 0)
            def _():
                acc_ref[...] += part

            @pl.when(k == nk - 1)
            def _():
                o_ref[...] = acc_ref[...].astype(out_dtype)

    return pl.pallas_call(
        body, name=name, grid=(M // tm, N // tn, nk),
        in_specs=[a_spec, b_spec],
        out_specs=pl.BlockSpec((tm, tn), lambda i, j, k: (i, j)),
        out_shape=SDS((M, N), out_dtype),
        scratch_shapes=[pltpu.VMEM((tm, tn), F32)],
        compiler_params=_params(3),
    )(a, b)


def _seg_spec(n_lat, d):
    return pl.BlockSpec((1, 6, d), lambda i: ((i >= n_lat).astype(jnp.int32), 0, 0))


def _seg_acc_spec(n_lat, d):
    return pl.BlockSpec((1, 1, d), lambda i: ((i >= n_lat).astype(jnp.int32), 0, 0))


def _res_norm(x, y, gmod, gk, nw, nmod, nk, h_dtype, n_lat, name):
    R, D = x.shape
    has_res, has_norm = y is not None, nw is not None
    row = pl.BlockSpec((ROW_TILE, D), lambda i: (i, 0))
    vec = pl.BlockSpec((1, D), lambda i: (0, 0))
    ins, specs, outs, ospecs = [x], [row], [], []
    if has_res:
        ins += [y, gmod]
        specs += [row, _seg_spec(n_lat, D)]
        outs.append(SDS((R, D), F32))
        ospecs.append(row)
    if has_norm:
        ins += [nw.reshape(1, D), nmod]
        specs += [vec, _seg_spec(n_lat, D)]
        outs.append(SDS((R, D), h_dtype))
        ospecs.append(row)

    def body(*refs):
        refs = list(refs)
        z = refs.pop(0)[...]
        if has_res:
            y_ref, g_ref = refs.pop(0), refs.pop(0)
            z = z + g_ref[0, pl.ds(3 * gk + 2, 1), :] * y_ref[...].astype(F32)
        if has_norm:
            nw_ref, m_ref = refs.pop(0), refs.pop(0)
        if has_res:
            refs.pop(0)[...] = z
        if has_norm:
            r = lax.rsqrt(jnp.mean(z * z, axis=-1, keepdims=True) + EPS)
            h = (z * r) * nw_ref[...]
            h = h * (1.0 + m_ref[0, pl.ds(3 * nk + 1, 1), :]) + m_ref[0, pl.ds(3 * nk, 1), :]
            refs.pop(0)[...] = h.astype(h_dtype)

    res = pl.pallas_call(
        body, name=name, grid=(R // ROW_TILE,), in_specs=specs, out_specs=ospecs,
        out_shape=outs, compiler_params=_params(1),
    )(*ins)
    return res if len(res) > 1 else res[0]


def _gate_bwd(dz, y, mod, k, out_dtype, n_lat, name):
    R, D = dz.shape
    row = pl.BlockSpec((ROW_TILE, D), lambda i: (i, 0))

    def body(dz_ref, y_ref, m_ref, dy_ref, dg_ref):
        i = pl.program_id(0)
        dzv = dz_ref[...]
        dy_ref[...] = (m_ref[0, pl.ds(3 * k + 2, 1), :] * dzv).astype(out_dtype)

        @pl.when((i == 0) | (i == n_lat))
        def _():
            dg_ref[...] = jnp.zeros_like(dg_ref)

        dg_ref[0] += jnp.sum(dzv * y_ref[...].astype(F32), axis=0, keepdims=True)

    return pl.pallas_call(
        body, name=name, grid=(R // ROW_TILE,),
        in_specs=[row, row, _seg_spec(n_lat, D)],
        out_specs=[row, _seg_acc_spec(n_lat, D)],
        out_shape=[SDS((R, D), out_dtype), SDS((2, 1, D), F32)],
        compiler_params=_params(1),
    )(dz, y, mod)


def _norm_bwd(dz, dh, x, nw, mod, k, n_lat, name):
    R, D = x.shape
    row = pl.BlockSpec((ROW_TILE, D), lambda i: (i, 0))
    vec = pl.BlockSpec((1, D), lambda i: (0, 0))

    def body(dz_ref, dh_ref, x_ref, nw_ref, m_ref, dx_ref, dnw_ref, dsh_ref, dsc_ref):
        i = pl.program_id(0)
        xv = x_ref[...]
        dhv = dh_ref[...].astype(F32)
        nwv = nw_ref[...]
        sc1 = 1.0 + m_ref[0, pl.ds(3 * k + 1, 1), :]
        r = lax.rsqrt(jnp.mean(xv * xv, axis=-1, keepdims=True) + EPS)
        xhat = xv * r
        a = dhv * (nwv * sc1)
        dx_ref[...] = dz_ref[...] + r * (a - xhat * jnp.mean(a * xhat, axis=-1, keepdims=True))

        @pl.when(i == 0)
        def _():
            dnw_ref[...] = jnp.zeros_like(dnw_ref)

        @pl.when((i == 0) | (i == n_lat))
        def _():
            dsh_ref[...] = jnp.zeros_like(dsh_ref)
            dsc_ref[...] = jnp.zeros_like(dsc_ref)

        dnw_ref[...] += jnp.sum(dhv * xhat, axis=0, keepdims=True) * sc1
        dsh_ref[0] += jnp.sum(dhv, axis=0, keepdims=True)
        dsc_ref[0] += jnp.sum(dhv * xhat, axis=0, keepdims=True) * nwv

    return pl.pallas_call(
        body, name=name, grid=(R // ROW_TILE,),
        in_specs=[row, row, row, vec, _seg_spec(n_lat, D)],
        out_specs=[row, vec, _seg_acc_spec(n_lat, D), _seg_acc_spec(n_lat, D)],
        out_shape=[SDS((R, D), F32), SDS((1, D), F32), SDS((2, 1, D), F32), SDS((2, 1, D), F32)],
        compiler_params=_params(1),
    )(dz, dh, x, nw.reshape(1, D), mod)


def _loss_bwd(xf, target, n_lat):
    R, D = xf.shape
    row = pl.BlockSpec((ROW_TILE, D), lambda i: (i, 0))
    tgt = pl.BlockSpec((ROW_TILE, D), lambda i: (jnp.minimum(i, n_lat - 1), 0))

    def body(x_ref, t_ref, dx_ref, loss_ref):
        i = pl.program_id(0)
        e = jnp.where(i < n_lat, x_ref[...] - t_ref[...], 0.0)
        dx_ref[...] = e * (1.0 / D)

        @pl.when(i == 0)
        def _():
            loss_ref[...] = jnp.zeros_like(loss_ref)

        loss_ref[...] += 0.5 * jnp.sum(jnp.mean(e * e, axis=-1, keepdims=True))

    return pl.pallas_call(
        body, name="loss_bwd", grid=(R // ROW_TILE,),
        in_specs=[row, tgt],
        out_specs=[row, pl.BlockSpec((8, 128), lambda i: (0, 0))],
        out_shape=[SDS((R, D), F32), SDS((8, 128), F32)],
        compiler_params=_params(1),
    )(xf, target)


def _halo_specs(n_tiles, width, tile=ROW_TILE):
    per = tile // HALO
    prev = pl.BlockSpec((HALO, width), lambda i: (jnp.maximum(i * per - 1, 0), 0))
    nxt = pl.BlockSpec((HALO, width), lambda i: (jnp.minimum((i + 1) * per, n_tiles * per - 1), 0))
    return prev, nxt


def _edge_flags(i, n_lat, n_tiles):
    first = (i == 0) | (i == n_lat)
    last = (i == n_lat - 1) | (i == n_tiles - 1)
    return first, last


def _conv_gate_fwd(u, conv_w, conv_b, n_lat, name):
    R, F2 = u.shape
    F = F2 // 2
    n_tiles = R // ROW_TILE
    T = ROW_TILE
    cw = _pick(F, 256, 128)
    row = pl.BlockSpec((T, F2), lambda i: (i, 0))
    prev, nxt = _halo_specs(n_tiles, F2)

    def body(u_ref, p_ref, n_ref, w_ref, b_ref, o_ref):
        i = pl.program_id(0)
        first, last = _edge_flags(i, n_lat, n_tiles)
        ridx = lax.broadcasted_iota(jnp.int32, (T, 1), 0)

        def conv(c0):
            cols = pl.ds(c0, cw)
            uv = u_ref[:, cols]
            pr = jnp.where(first, 0.0, p_ref[pl.ds(HALO - 1, 1), cols])
            nx = jnp.where(last, 0.0, n_ref[pl.ds(0, 1), cols])
            up = jnp.where(ridx == 0, pr, pltpu.roll(uv, 1, 0))
            un = jnp.where(ridx == T - 1, nx, pltpu.roll(uv, T - 1, 0))
            return (up * w_ref[pl.ds(0, 1), cols] + uv * w_ref[pl.ds(1, 1), cols]
                    + un * w_ref[pl.ds(2, 1), cols] + b_ref[:, cols])

        for c0 in range(0, F, cw):
            ca, cv = conv(c0), conv(F + c0)
            o_ref[:, pl.ds(c0, cw)] = (ca * _sigmoid(ca) * cv).astype(BF16)

    return pl.pallas_call(
        body, name=name, grid=(n_tiles,),
        in_specs=[row, prev, nxt, pl.BlockSpec((3, F2), lambda i: (0, 0)),
                  pl.BlockSpec((1, F2), lambda i: (0, 0))],
        out_specs=pl.BlockSpec((T, F), lambda i: (i, 0)),
        out_shape=SDS((R, F), BF16), compiler_params=_params(1),
    )(u, u, u, conv_w, conv_b)


def _conv_gate_bwd(u, dgact, conv_w, conv_b, n_lat, name):
    R, F2 = u.shape
    F = F2 // 2
    n_tiles = R // ROW_TILE
    T, N = ROW_TILE, ROW_TILE + 2 * HALO
    cw = _pick(F, 256, 128)
    rowu = pl.BlockSpec((T, F2), lambda i: (i, 0))
    rowg = pl.BlockSpec((T, F), lambda i: (i, 0))
    pu, nu = _halo_specs(n_tiles, F2)
    pg, ng = _halo_specs(n_tiles, F)

    def body(u_ref, pu_ref, nu_ref, g_ref, pg_ref, ng_ref, w_ref, b_ref, du_ref, dw_ref, db_ref):
        i = pl.program_id(0)
        first, last = _edge_flags(i, n_lat, n_tiles)

        @pl.when(i == 0)
        def _():
            dw_ref[...] = jnp.zeros_like(dw_ref)
            db_ref[...] = jnp.zeros_like(db_ref)

        def ext(t_ref, p_ref, n_ref, cols):
            pr = jnp.where(first, 0.0, p_ref[:, cols])
            nx = jnp.where(last, 0.0, n_ref[:, cols])
            return jnp.concatenate([pr, t_ref[:, cols], nx], axis=0)

        def conv(c0):
            cols = pl.ds(c0, cw)
            e = ext(u_ref, pu_ref, nu_ref, cols)
            up, un = pltpu.roll(e, 1, 0), pltpu.roll(e, N - 1, 0)
            c = (up * w_ref[pl.ds(0, 1), cols] + e * w_ref[pl.ds(1, 1), cols]
                 + un * w_ref[pl.ds(2, 1), cols] + b_ref[:, cols])
            return c, up, e, un

        def back(c0, dc, up, e, un):
            cols = pl.ds(c0, cw)
            du = (pltpu.roll(dc, N - 1, 0) * w_ref[pl.ds(0, 1), cols] + dc * w_ref[pl.ds(1, 1), cols]
                  + pltpu.roll(dc, 1, 0) * w_ref[pl.ds(2, 1), cols])
            du_ref[:, cols] = du[HALO:HALO + T].astype(BF16)
            dct = dc[HALO:HALO + T]
            dw_ref[pl.ds(0, 1), cols] += jnp.sum(dct * up[HALO:HALO + T], axis=0, keepdims=True)
            dw_ref[pl.ds(1, 1), cols] += jnp.sum(dct * e[HALO:HALO + T], axis=0, keepdims=True)
            dw_ref[pl.ds(2, 1), cols] += jnp.sum(dct * un[HALO:HALO + T], axis=0, keepdims=True)
            db_ref[:, cols] += jnp.sum(dct, axis=0, keepdims=True)

        for c0 in range(0, F, cw):
            dg = ext(g_ref, pg_ref, ng_ref, pl.ds(c0, cw))
            ca, upa, ea, una = conv(c0)
            cv, upv, ev, unv = conv(F + c0)
            s = _sigmoid(ca)
            back(F + c0, dg * (ca * s), upv, ev, unv)
            back(c0, dg * cv * (s * (1.0 + ca * (1.0 - s))), upa, ea, una)

    return pl.pallas_call(
        body, name=name, grid=(n_tiles,),
        in_specs=[rowu, pu, nu, rowg, pg, ng, pl.BlockSpec((3, F2), lambda i: (0, 0)),
                  pl.BlockSpec((1, F2), lambda i: (0, 0))],
        out_specs=[rowu, pl.BlockSpec((3, F2), lambda i: (0, 0)), pl.BlockSpec((1, F2), lambda i: (0, 0))],
        out_shape=[SDS((R, F2), BF16), SDS((3, F2), F32), SDS((1, F2), F32)],
        compiler_params=_params(1),
    )(u, u, u, dgact, dgact, dgact, conv_w, conv_b)


def _pool_counts(i, n_lat, s_len, l_len, n_rows, offset):
    ctx = i >= n_lat
    t0 = jnp.where(ctx, i - n_lat, i) * ROW_TILE + offset
    seg = jnp.where(ctx, l_len, s_len)
    t = t0 + lax.broadcasted_iota(jnp.int32, (n_rows, 1), 0)
    out = []
    for win in POOL_WINDOWS:
        cnt = jnp.minimum(t + win // 2, seg) - jnp.maximum(t - win // 2, 0)
        out.append(jnp.maximum(cnt, 1).astype(F32))
    return out


def _window_sum(e, lo, hi, n):
    acc = None
    for j in range(lo, hi + 1):
        term = e if j == 0 else pltpu.roll(e, (-j) % n, 0)
        acc = term if acc is None else acc + term
    return acc


def _pool_fwd(h, w, b, scale, n_lat, s_len, l_len, name):
    R, D = h.shape
    G = D // 4
    n_tiles = R // ROW_TILE
    T, N = ROW_TILE, ROW_TILE + 2 * HALO
    row = pl.BlockSpec((T, D), lambda i: (i, 0))
    prev, nxt = _halo_specs(n_tiles, D)
    vec = pl.BlockSpec((1, D), lambda i: (0, 0))

    def body(h_ref, p_ref, n_ref, w_ref, b_ref, s_ref, y_ref):
        i = pl.program_id(0)
        first, last = _edge_flags(i, n_lat, n_tiles)
        cnts = _pool_counts(i, n_lat, s_len, l_len, T, 0)
        for g, win in enumerate(POOL_WINDOWS):
            cols = pl.ds(g * G, G)
            pr = jnp.where(first, 0.0, p_ref[:, cols])
            nx = jnp.where(last, 0.0, n_ref[:, cols])
            hv = h_ref[:, cols]
            e = jnp.concatenate([pr, hv, nx], axis=0)
            mean = _window_sum(e, -(win // 2), win // 2 - 1, N)[HALO:HALO + T] / cnts[g]
            yg = _dot(_bf(mean - hv), w_ref[g])
            y_ref[:, cols] = (yg + b_ref[:, cols]) * s_ref[:, cols]

    return pl.pallas_call(
        body, name=name, grid=(n_tiles,),
        in_specs=[row, prev, nxt, pl.BlockSpec((4, G, G), lambda i: (0, 0, 0)), vec, vec],
        out_specs=row, out_shape=SDS((R, D), F32), compiler_params=_params(1),
    )(h, h, h, w, b, scale)


def _pool_bwd(h, dy, w, b, scale, n_lat, s_len, l_len, name):
    R, D = h.shape
    G = D // 4
    n_tiles = R // ROW_TILE
    T, N = ROW_TILE, ROW_TILE + 2 * HALO
    row = pl.BlockSpec((T, D), lambda i: (i, 0))
    prev, nxt = _halo_specs(n_tiles, D)
    vec = pl.BlockSpec((1, D), lambda i: (0, 0))
    wspec = pl.BlockSpec((4, G, G), lambda i: (0, 0, 0))

    def body(h_ref, ph_ref, nh_ref, d_ref, pd_ref, nd_ref, w_ref, b_ref, s_ref,
             dh_ref, dw_ref, db_ref, ds_ref):
        i = pl.program_id(0)
        first, last = _edge_flags(i, n_lat, n_tiles)

        @pl.when(i == 0)
        def _():
            dw_ref[...] = jnp.zeros_like(dw_ref)
            db_ref[...] = jnp.zeros_like(db_ref)
            ds_ref[...] = jnp.zeros_like(ds_ref)

        cnts = _pool_counts(i, n_lat, s_len, l_len, T, 0)
        cnts_ext = _pool_counts(i, n_lat, s_len, l_len, N, -HALO)
        for g, win in enumerate(POOL_WINDOWS):
            cols = pl.ds(g * G, G)

            def ext(t_ref, p_ref, n_ref):
                pr = jnp.where(first, 0.0, p_ref[:, cols])
                nx = jnp.where(last, 0.0, n_ref[:, cols])
                return jnp.concatenate([pr, t_ref[:, cols], nx], axis=0)

            hv = h_ref[:, cols]
            mean = _window_sum(ext(h_ref, ph_ref, nh_ref), -(win // 2), win // 2 - 1, N)[HALO:HALO + T] / cnts[g]
            z = _bf(mean - hv)
            sc = s_ref[:, cols]
            dye = ext(d_ref, pd_ref, nd_ref)
            dt = _bf(dye * sc)
            dz = _dot_nt(dt, w_ref[g])
            dm = dz / cnts_ext[g]
            dh = _window_sum(dm, -(win // 2 - 1), win // 2, N) - dz
            dh_ref[:, cols] = dh[HALO:HALO + T]
            dyt = dye[HALO:HALO + T]
            dw_ref[g] += _dot_tn(z, dt[HALO:HALO + T])
            db_ref[:, cols] += jnp.sum(dyt * sc, axis=0, keepdims=True)
            ds_ref[:, cols] += jnp.sum(dyt * (_dot(z, w_ref[g]) + b_ref[:, cols]), axis=0, keepdims=True)

    return pl.pallas_call(
        body, name=name, grid=(n_tiles,),
        in_specs=[row, prev, nxt, row, prev, nxt, wspec, vec, vec],
        out_specs=[row, wspec, vec, vec],
        out_shape=[SDS((R, D), F32), SDS((4, G, G), F32), SDS((1, D), F32), SDS((1, D), F32)],
        compiler_params=_params(1),
    )(h, h, h, dy, dy, dy, w, b, scale)


def _rope_tables(s_len, l_len):
    t = jnp.arange(s_len)
    row = (t // GRID_W).astype(F32)
    col = (t % GRID_W).astype(F32)
    axis_dim = HEAD_DIM // 2
    inv = ROPE_THETA ** (-jnp.arange(0, axis_dim, 2, dtype=F32) / axis_dim)
    ar, ac = row[:, None] * inv, col[:, None] * inv
    cos = jnp.concatenate([jnp.cos(ar), jnp.cos(ar), jnp.cos(ac), jnp.cos(ac)], axis=-1)
    sin = jnp.concatenate([-jnp.sin(ar), jnp.sin(ar), -jnp.sin(ac), jnp.sin(ac)], axis=-1)
    cos = jnp.concatenate([cos, jnp.ones((l_len, HEAD_DIM), F32)], axis=0)
    sin = jnp.concatenate([sin, jnp.zeros((l_len, HEAD_DIM), F32)], axis=0)
    return cos, sin


def _swap_halves(v):
    lane = lax.broadcasted_iota(jnp.int32, v.shape, 1)
    return jnp.where((lane % 64) < 32, pltpu.roll(v, 96, 1), pltpu.roll(v, 32, 1))


def _qk_prep_fwd(qkv, q_gain, k_gain, cos, sin):
    R = qkv.shape[0]
    NQ, NK = N_HEADS * HEAD_DIM, N_KV * HEAD_DIM
    T = ROW_TILE
    vec = pl.BlockSpec((1, HEAD_DIM), lambda i: (0, 0))
    tab = pl.BlockSpec((T, HEAD_DIM), lambda i: (i, 0))

    def body(x_ref, qg_ref, kg_ref, c_ref, s_ref, q_ref, k_ref, v_ref):
        cosv, sinv = c_ref[...], s_ref[...]

        def prep(c0, gain):
            xh = x_ref[:, pl.ds(c0, HEAD_DIM)]
            xn = xh * lax.rsqrt(jnp.mean(xh * xh, axis=-1, keepdims=True) + EPS) * gain
            return _bf(xn * cosv + _swap_halves(xn) * sinv)

        for hd in range(N_HEADS):
            q_ref[:, pl.ds(hd * HEAD_DIM, HEAD_DIM)] = prep(hd * HEAD_DIM, qg_ref[...])
        for hd in range(N_KV):
            k_ref[:, pl.ds(hd * HEAD_DIM, HEAD_DIM)] = prep(NQ + hd * HEAD_DIM, kg_ref[...])
        v_ref[...] = _bf(x_ref[:, pl.ds(NQ + NK, NK)])

    return pl.pallas_call(
        body, name="qk_prep_fwd", grid=(R // T,),
        in_specs=[pl.BlockSpec((T, NQ + 2 * NK), lambda i: (i, 0)), vec, vec, tab, tab],
        out_specs=[pl.BlockSpec((T, NQ), lambda i: (i, 0)), pl.BlockSpec((T, NK), lambda i: (i, 0)),
                   pl.BlockSpec((T, NK), lambda i: (i, 0))],
        out_shape=[SDS((R, NQ), BF16), SDS((R, NK), BF16), SDS((R, NK), BF16)],
        compiler_params=_params(1),
    )(qkv, q_gain, k_gain, cos, sin)


def _qk_prep_bwd(qkv, dq, dk, dv, q_gain, k_gain, cos, sin):
    R = qkv.shape[0]
    NQ, NK = N_HEADS * HEAD_DIM, N_KV * HEAD_DIM
    T = ROW_TILE
    vec = pl.BlockSpec((1, HEAD_DIM), lambda i: (0, 0))
    tab = pl.BlockSpec((T, HEAD_DIM), lambda i: (i, 0))

    def body(x_ref, dq_ref, dk_ref, dv_ref, qg_ref, kg_ref, c_ref, s_ref, o_ref, dqg_ref, dkg_ref):
        i = pl.program_id(0)
        cosv, sinv = c_ref[...], s_ref[...]

        @pl.when(i == 0)
        def _():
            dqg_ref[...] = jnp.zeros_like(dqg_ref)
            dkg_ref[...] = jnp.zeros_like(dkg_ref)

        def back(c0, dout, gain, dg_ref):
            xh = x_ref[:, pl.ds(c0, HEAD_DIM)]
            r = lax.rsqrt(jnp.mean(xh * xh, axis=-1, keepdims=True) + EPS)
            xhat = xh * r
            dxn = dout * cosv + _swap_halves(dout * sinv)
            dg_ref[...] += jnp.sum(dxn * xhat, axis=0, keepdims=True)
            a = dxn * gain
            o_ref[:, pl.ds(c0, HEAD_DIM)] = _bf(r * (a - xhat * jnp.mean(a * xhat, axis=-1, keepdims=True)))

        for hd in range(N_HEADS):
            back(hd * HEAD_DIM, dq_ref[:, pl.ds(hd * HEAD_DIM, HEAD_DIM)], qg_ref[...], dqg_ref)
        for hd in range(N_KV):
            back(NQ + hd * HEAD_DIM, dk_ref[:, pl.ds(hd * HEAD_DIM, HEAD_DIM)], kg_ref[...], dkg_ref)
        o_ref[:, pl.ds(NQ + NK, NK)] = _bf(dv_ref[...])

    return pl.pallas_call(
        body, name="qk_prep_bwd", grid=(R // T,),
        in_specs=[pl.BlockSpec((T, NQ + 2 * NK), lambda i: (i, 0)), pl.BlockSpec((T, NQ), lambda i: (i, 0)),
                  pl.BlockSpec((T, NK), lambda i: (i, 0)), pl.BlockSpec((T, NK), lambda i: (i, 0)),
                  vec, vec, tab, tab],
        out_specs=[pl.BlockSpec((T, NQ + 2 * NK), lambda i: (i, 0)), vec, vec],
        out_shape=[SDS((R, NQ + 2 * NK), BF16), SDS((1, HEAD_DIM), F32), SDS((1, HEAD_DIM), F32)],
        compiler_params=_params(1),
    )(qkv, dq, dk, dv, q_gain, k_gain, cos, sin)


def _flash_fwd(q, k, v, s_len, l_len):
    R = q.shape[0]
    T = ROW_TILE
    n_lat = s_len // T
    ck = _pick(s_len, 512, 128)
    scale = HEAD_DIM ** -0.5
    group = N_HEADS // N_KV

    def body(q_ref, k_ref, v_ref, o_ref, lse_ref, m_s, l_s, acc_s):
        i = pl.program_id(1)
        qv = q_ref[...]
        m_s[...] = jnp.full_like(m_s, -jnp.inf)
        l_s[...] = jnp.zeros_like(l_s)
        acc_s[...] = jnp.zeros_like(acc_s)

        def step(rows):
            s = _dot_nt(qv, k_ref[rows, :]) * scale
            m_new = jnp.maximum(m_s[...], jnp.max(s, axis=-1, keepdims=True))
            alpha = jnp.exp(m_s[...] - m_new)
            p = jnp.exp(s - m_new)
            l_s[...] = alpha * l_s[...] + jnp.sum(p, axis=-1, keepdims=True)
            acc_s[...] = alpha * acc_s[...] + _dot(_bf(p), v_ref[rows, :])
            m_s[...] = m_new

        @pl.when(i < n_lat)
        def _():
            def loop(c, carry):
                step(pl.ds(pl.multiple_of(c * ck, ck), ck))
                return carry
            lax.fori_loop(0, s_len // ck, loop, 0)

        step(pl.ds(s_len, l_len))
        o_ref[...] = acc_s[...] / l_s[...]
        lse_ref[0] = m_s[...] + jnp.log(l_s[...])

    return pl.pallas_call(
        body, name="flash_fwd", grid=(N_HEADS, R // T),
        in_specs=[pl.BlockSpec((T, HEAD_DIM), lambda h, i: (i, h)),
                  pl.BlockSpec((R, HEAD_DIM), lambda h, i: (0, h // group)),
                  pl.BlockSpec((R, HEAD_DIM), lambda h, i: (0, h // group))],
        out_specs=[pl.BlockSpec((T, HEAD_DIM), lambda h, i: (i, h)),
                   pl.BlockSpec((1, T, 1), lambda h, i: (h, i, 0))],
        out_shape=[SDS((R, N_HEADS * HEAD_DIM), F32), SDS((N_HEADS, R, 1), F32)],
        scratch_shapes=[pltpu.VMEM((T, 1), F32), pltpu.VMEM((T, 1), F32), pltpu.VMEM((T, HEAD_DIM), F32)],
        compiler_params=_params(2),
    )(q, k, v)


def _flash_bwd(q, k, v, o, lse, do, s_len, l_len):
    R = q.shape[0]
    T = ROW_TILE
    n_lat = s_len // T
    ck = _pick(s_len, 512, 128)
    scale = HEAD_DIM ** -0.5
    group = N_HEADS // N_KV
    qspec = pl.BlockSpec((T, HEAD_DIM), lambda g, h, i: (i, g * group + h))
    kspec = pl.BlockSpec((R, HEAD_DIM), lambda g, h, i: (0, g))

    def body(q_ref, do_ref, o_ref, lse_ref, k_ref, v_ref, dq_ref, dk_ref, dv_ref, dq_s):
        h, i = pl.program_id(1), pl.program_id(2)

        @pl.when((h == 0) & (i == 0))
        def _():
            dk_ref[...] = jnp.zeros_like(dk_ref)
            dv_ref[...] = jnp.zeros_like(dv_ref)

        qv = q_ref[...]
        dov = do_ref[...]
        dob = _bf(dov)
        delta = jnp.sum(dov * o_ref[...], axis=-1, keepdims=True)
        lse_v = lse_ref[0]
        dq_s[...] = jnp.zeros_like(dq_s)

        def step(rows):
            kv, vv = k_ref[rows, :], v_ref[rows, :]
            p = jnp.exp(_dot_nt(qv, kv) * scale - lse_v)
            dv_ref[rows, :] += _dot_tn(_bf(p), dob)
            ds = _bf(p * (_dot_nt(dob, vv) - delta) * scale)
            dq_s[...] += _dot(ds, kv)
            dk_ref[rows, :] += _dot_tn(ds, qv)

        @pl.when(i < n_lat)
        def _():
            def loop(c, carry):
                step(pl.ds(pl.multiple_of(c * ck, ck), ck))
                return carry
            lax.fori_loop(0, s_len // ck, loop, 0)

        step(pl.ds(s_len, l_len))
        dq_ref[...] = dq_s[...]

    return pl.pallas_call(
        body, name="flash_bwd", grid=(N_KV, group, R // T),
        in_specs=[qspec, qspec, qspec, pl.BlockSpec((1, T, 1), lambda g, h, i: (g * group + h, i, 0)),
                  kspec, kspec],
        out_specs=[qspec, kspec, kspec],
        out_shape=[SDS((R, N_HEADS * HEAD_DIM), F32), SDS((R, N_KV * HEAD_DIM), F32),
                   SDS((R, N_KV * HEAD_DIM), F32)],
        scratch_shapes=[pltpu.VMEM((T, HEAD_DIM), F32)],
        compiler_params=_params(3),
    )(q, do, o, lse, k, v)


K_SCALE = RET_DK ** -0.5


def _log_sigmoid(v):
    return -(jnp.maximum(-v, 0.0) + jnp.log(1.0 + jnp.exp(-jnp.abs(v))))


def _ret_decays(d, lg):
    C = RET_CHUNK
    ic = lax.broadcasted_iota(jnp.int32, (C, 1), 0)
    ir = lax.broadcasted_iota(jnp.int32, (1, C), 1)
    li = jnp.where(d == 0, ic, C - 1 - ic).astype(F32)
    lj = jnp.where(d == 0, ir, C - 1 - ir).astype(F32)
    diff = li - lj
    mask = jnp.where(diff >= 0, jnp.exp(jnp.maximum(diff, 0.0) * lg), 0.0)
    qd = jnp.exp((li + 1.0) * lg)
    kd = jnp.exp((C - 1.0 - li) * lg)
    cd = jnp.exp(C * lg)
    return li, diff, mask, qd, kd, cd


def _ctx_weights(d, t, lg, l_len):
    C = RET_CHUNK
    j = (t * C + lax.broadcasted_iota(jnp.int32, (C, 1), 0)).astype(F32)
    e = jnp.where(d == 0, (l_len - 1.0) - j, j)
    return e, jnp.exp(e * lg)


def _ret_specs(n_lat_c, n_ctx_c, ctx_first):
    def blk(d, t):
        if ctx_first:
            n = jnp.maximum(t - n_ctx_c, 0)
            lat = jnp.where(d == 0, n, n_lat_c - 1 - n)
            return jnp.where(t < n_ctx_c, n_lat_c + t, lat)
        n = jnp.minimum(t, n_lat_c - 1)
        lat = jnp.where(d == 0, n_lat_c - 1 - n, n)
        return jnp.where(t >= n_lat_c, t, lat)
    return blk


def _ret_fwd(proj, lgt, s_len, l_len):
    R = proj.shape[0]
    C, H, DK, DV = RET_CHUNK, RET_HEADS, RET_DK, RET_DV
    nl, nc = s_len // C, l_len // C
    blk = _ret_specs(nl, nc, True)

    def body(q_ref, k_ref, v_ref, lg_ref, o_ref, st_ref, r_s):
        d, t = pl.program_id(0), pl.program_id(2)
        lg = jnp.max(_log_sigmoid(lg_ref[0, 0]), axis=-1, keepdims=True)

        @pl.when(t == 0)
        def _():
            r_s[...] = jnp.zeros_like(r_s)

        @pl.when(t < nc)
        def _():
            _, w = _ctx_weights(d, t, lg, l_len)
            r_s[...] += _dot_tn(_bf(k_ref[...] * K_SCALE * w), _bf(v_ref[...]))
            o_ref[0] = jnp.zeros((C, DV), F32)

        @pl.when(t >= nc)
        def _():
            _, _, mask, qd, kd, cd = _ret_decays(d, lg)
            qb, kv, vb = _bf(q_ref[...]), k_ref[...] * K_SCALE, _bf(v_ref[...])
            r = r_s[...]
            st_ref[0, 0, 0] = r
            att = _dot_nt(qb, _bf(kv)) * mask
            o_ref[0] = _dot(_bf(att), vb) + _dot(qb, _bf(r)) * qd
            r_s[...] = r * cd + _dot_tn(_bf(kv * kd), vb)

    return pl.pallas_call(
        body, name="ret_fwd", grid=(2, H, nc + nl),
        in_specs=[pl.BlockSpec((C, DK), lambda d, h, t: (blk(d, t), h)),
                  pl.BlockSpec((C, DK), lambda d, h, t: (blk(d, t), H + h)),
                  pl.BlockSpec((C, DV), lambda d, h, t: (blk(d, t), H + h)),
                  pl.BlockSpec((1, 1, 1, 128), lambda d, h, t: (d, h, 0, 0))],
        out_specs=[pl.BlockSpec((1, C, DV), lambda d, h, t: (d, blk(d, t), h)),
                   pl.BlockSpec((1, 1, 1, DK, DV), lambda d, h, t: (d, h, jnp.maximum(t - nc, 0), 0, 0))],
        out_shape=[SDS((2, R, H * DV), F32), SDS((2, H, nl, DK, DV), F32)],
        scratch_shapes=[pltpu.VMEM((DK, DV), F32)],
        compiler_params=_params(3),
    )(proj, proj, proj, lgt)


def _ret_bwd(proj, lgt, states, do, s_len, l_len):
    R = proj.shape[0]
    C, H, DK, DV = RET_CHUNK, RET_HEADS, RET_DK, RET_DV
    nl, nc = s_len // C, l_len // C
    blk = _ret_specs(nl, nc, False)
    last = nl + nc - 1

    def body(q_ref, k_ref, v_ref, lg_ref, st_ref, do_ref, dq_ref, dk_ref, dv_ref, dlg_ref, dr_s, dl_s):
        d, t = pl.program_id(0), pl.program_id(2)
        x = lg_ref[0, 0]
        lg = jnp.max(_log_sigmoid(x), axis=-1, keepdims=True)

        @pl.when(t == 0)
        def _():
            dr_s[...] = jnp.zeros_like(dr_s)
            dl_s[...] = jnp.zeros_like(dl_s)

        @pl.when(t < nl)
        def _():
            li, diff, mask, qd, kd, cd = _ret_decays(d, lg)
            qv, kv, vv, dov = q_ref[...], k_ref[...] * K_SCALE, v_ref[...], do_ref[...]
            qb, kb, vb, dob = _bf(qv), _bf(kv), _bf(vv), _bf(dov)
            r, drn = st_ref[0, 0, 0], dr_s[...]
            rb, drb = _bf(r), _bf(drn)
            p = _dot_nt(qb, kb)
            dp = _dot_nt(dob, vb) * mask
            dpb = _bf(dp)
            doq = _bf(dov * qd)
            dq_inter = _dot_nt(doq, rb)
            dk_state = kd * _dot_nt(vb, drb)
            dq_ref[0] = _dot(dpb, kb) + dq_inter
            dk_ref[0] = (_dot_tn(dpb, qb) + dk_state) * K_SCALE
            dv_ref[0] = _dot_tn(_bf(p * mask), dob) + _dot(_bf(kv * kd), drb)
            dr_s[...] = cd * drn + _dot_tn(qb, doq)
            dl_s[...] += (jnp.sum(dp * p * diff) + jnp.sum((li + 1.0) * qv * dq_inter)
                          + jnp.sum((C - 1.0 - li) * kv * dk_state) + C * jnp.sum(cd * r * drn))

        @pl.when(t >= nl)
        def _():
            e, w = _ctx_weights(d, t - nl, lg, l_len)
            kv, vb, drb = k_ref[...] * K_SCALE, _bf(v_ref[...]), _bf(dr_s[...])
            dkc = w * _dot_nt(vb, drb)
            dq_ref[0] = jnp.zeros((C, DK), F32)
            dk_ref[0] = dkc * K_SCALE
            dv_ref[0] = _dot(_bf(kv * w), drb)
            dl_s[...] += jnp.sum(e * kv * dkc)

        @pl.when(t == last)
        def _():
            dlg_ref[0, 0] = dl_s[...] * (1.0 / (1.0 + jnp.exp(x)))

    return pl.pallas_call(
        body, name="ret_bwd", grid=(2, H, nl + nc),
        in_specs=[pl.BlockSpec((C, DK), lambda d, h, t: (blk(d, t), h)),
                  pl.BlockSpec((C, DK), lambda d, h, t: (blk(d, t), H + h)),
                  pl.BlockSpec((C, DV), lambda d, h, t: (blk(d, t), H + h)),
                  pl.BlockSpec((1, 1, 1, 128), lambda d, h, t: (d, h, 0, 0)),
                  pl.BlockSpec((1, 1, 1, DK, DV), lambda d, h, t: (d, h, jnp.maximum(nl - 1 - t, 0), 0, 0)),
                  pl.BlockSpec((C, DV), lambda d, h, t: (blk(d, t), h))],
        out_specs=[pl.BlockSpec((1, C, DK), lambda d, h, t: (d, blk(d, t), h)),
                   pl.BlockSpec((1, C, DK), lambda d, h, t: (d, blk(d, t), h)),
                   pl.BlockSpec((1, C, DV), lambda d, h, t: (d, blk(d, t), h)),
                   pl.BlockSpec((1, 1, 1, 128), lambda d, h, t: (d, h, 0, 0))],
        out_shape=[SDS((2, R, H * DK), F32), SDS((2, R, H * DK), F32), SDS((2, R, H * DV), F32),
                   SDS((2, H, 1, 128), F32)],
        scratch_shapes=[pltpu.VMEM((DK, DV), F32), pltpu.VMEM((1, 128), F32)],
        compiler_params=_params(3),
    )(proj, proj, proj, lgt, states, do)


def _readout_fwd(o2, proj, gn_w):
    R = proj.shape[0]
    H, DV = RET_HEADS, RET_DV
    W = H * DV
    T = ROW_TILE

    def body(o_ref, g_ref, w_ref, out_ref):
        for hh in range(H):
            cols = pl.ds(hh * DV, DV)
            y = o_ref[0, :, cols] + o_ref[1, :, cols]
            yc = y - jnp.mean(y, axis=-1, keepdims=True)
            yn = yc * lax.rsqrt(jnp.mean(yc * yc, axis=-1, keepdims=True) + EPS) * w_ref[:, cols]
            g = g_ref[:, cols]
            out_ref[:, cols] = _bf(g * _sigmoid(g) * yn)

    return pl.pallas_call(
        body, name="readout_fwd", grid=(R // T,),
        in_specs=[pl.BlockSpec((2, T, W), lambda i: (0, i, 0)), pl.BlockSpec((T, W), lambda i: (i, 2)),
                  pl.BlockSpec((1, W), lambda i: (0, 0))],
        out_specs=pl.BlockSpec((T, W), lambda i: (i, 0)),
        out_shape=SDS((R, W), BF16), compiler_params=_params(1),
    )(o2, proj, gn_w)


def _readout_bwd(o2, proj, gn_w, dgated):
    R = proj.shape[0]
    H, DV = RET_HEADS, RET_DV
    W = H * DV
    T = ROW_TILE

    def body(o_ref, g_ref, w_ref, d_ref, do_ref, dg_ref, dw_ref):
        i = pl.program_id(0)

        @pl.when(i == 0)
        def _():
            dw_ref[...] = jnp.zeros_like(dw_ref)

        for hh in range(H):
            cols = pl.ds(hh * DV, DV)
            y = o_ref[0, :, cols] + o_ref[1, :, cols]
            yc = y - jnp.mean(y, axis=-1, keepdims=True)
            rstd = lax.rsqrt(jnp.mean(yc * yc, axis=-1, keepdims=True) + EPS)
            yn0 = yc * rstd
            wv = w_ref[:, cols]
            g = g_ref[:, cols]
            s = _sigmoid(g)
            dgt = d_ref[:, cols]
            dyn = dgt * (g * s)
            dg_ref[:, cols] = _bf(dgt * (yn0 * wv) * (s * (1.0 + g * (1.0 - s))))
            dw_ref[:, cols] += jnp.sum(dyn * yn0, axis=0, keepdims=True)
            a = dyn * wv
            do_ref[:, cols] = rstd * (a - jnp.mean(a, axis=-1, keepdims=True)
                                      - yn0 * jnp.mean(a * yn0, axis=-1, keepdims=True))

    return pl.pallas_call(
        body, name="readout_bwd", grid=(R // T,),
        in_specs=[pl.BlockSpec((2, T, W), lambda i: (0, i, 0)), pl.BlockSpec((T, W), lambda i: (i, 2)),
                  pl.BlockSpec((1, W), lambda i: (0, 0)), pl.BlockSpec((T, W), lambda i: (i, 0))],
        out_specs=[pl.BlockSpec((T, W), lambda i: (i, 0)), pl.BlockSpec((T, W), lambda i: (i, 0)),
                   pl.BlockSpec((1, W), lambda i: (0, 0))],
        out_shape=[SDS((R, W), F32), SDS((R, W), BF16), SDS((1, W), F32)],
        compiler_params=_params(1),
    )(o2, proj, gn_w, dgated)


def _ret_dproj(dq2, dk2, dv2, dg):
    R = dg.shape[0]
    NQ, NV = RET_HEADS * RET_DK, RET_HEADS * RET_DV
    T = ROW_TILE

    def body(dq_ref, dk_ref, dv_ref, dg_ref, o_ref):
        o_ref[:, pl.ds(0, NQ)] = _bf(dq_ref[0] + dq_ref[1])
        o_ref[:, pl.ds(NQ, NQ)] = _bf(dk_ref[0] + dk_ref[1])
        o_ref[:, pl.ds(2 * NQ, NV)] = _bf(dv_ref[0] + dv_ref[1])
        o_ref[:, pl.ds(2 * NQ + NV, NV)] = dg_ref[...]

    return pl.pallas_call(
        body, name="ret_dproj", grid=(R // T,),
        in_specs=[pl.BlockSpec((2, T, NQ), lambda i: (0, i, 0)), pl.BlockSpec((2, T, NQ), lambda i: (0, i, 0)),
                  pl.BlockSpec((2, T, NV), lambda i: (0, i, 0)), pl.BlockSpec((T, NV), lambda i: (i, 0))],
        out_specs=pl.BlockSpec((T, 2 * NQ + 2 * NV), lambda i: (i, 0)),
        out_shape=SDS((R, 2 * NQ + 2 * NV), BF16), compiler_params=_params(1),
    )(dq2, dk2, dv2, dg)


def _silu(v):
    return v * _sigmoid(v)


def _ada_fwd(c_rows, ada_w, ada_b_shard):
    depth, D, cols = ada_w.shape

    def body(c_ref, w_ref, b_ref, o_ref):
        o_ref[0] = _dot(_bf(_silu(c_ref[...])), _bf(w_ref[0])) + b_ref[0]

    return pl.pallas_call(
        body, name="ada_fwd", grid=(depth,),
        in_specs=[pl.BlockSpec((16, D), lambda i: (0, 0)), pl.BlockSpec((1, D, cols), lambda i: (i, 0, 0)),
                  pl.BlockSpec((1, 1, cols), lambda i: (i, 0, 0))],
        out_specs=pl.BlockSpec((1, 16, cols), lambda i: (i, 0, 0)),
        out_shape=SDS((depth, 16, cols), F32), compiler_params=_params(1),
    )(c_rows, ada_w, ada_b_shard)


def _ada_bwd(c_rows, ada_w, d_lat, d_ctx):
    depth, D, cols = ada_w.shape

    def body(c_ref, w_ref, dl_ref, dc_ref, dw_ref, pc_ref):
        i = pl.program_id(0)
        cv = c_ref[...]
        a = _silu(cv)
        dcs = jnp.broadcast_to(jnp.sum(dc_ref[0], axis=0, keepdims=True), (8, cols))
        dw_ref[0] = _dot_tn(_bf(a[0:8]), _bf(dl_ref[0])) + _dot_tn(_bf(a[8:16]), _bf(dcs))

        @pl.when(i == 0)
        def _():
            pc_ref[...] = jnp.zeros_like(pc_ref)

        pc_ref[...] += _dot_nt(_bf(dcs), _bf(w_ref[0]))

        @pl.when(i == depth - 1)
        def _():
            cc = c_ref[pl.ds(8, 1), :]
            s = _sigmoid(cc)
            pc_ref[...] = pc_ref[...] * (s * (1.0 + cc * (1.0 - s)))

    return pl.pallas_call(
        body, name="ada_bwd", grid=(depth,),
        in_specs=[pl.BlockSpec((16, D), lambda i: (0, 0)), pl.BlockSpec((1, D, cols), lambda i: (i, 0, 0)),
                  pl.BlockSpec((1, 8, cols), lambda i: (i, 0, 0)), pl.BlockSpec((1, 8, cols), lambda i: (i, 0, 0))],
        out_specs=[pl.BlockSpec((1, D, cols), lambda i: (i, 0, 0)), pl.BlockSpec((8, D), lambda i: (0, 0))],
        out_shape=[SDS((depth, D, cols), F32), SDS((8, D), F32)], compiler_params=_params(1),
    )(c_rows, ada_w, d_lat, d_ctx)


def _adamw(w, g, m, v, name):
    shape = w.shape
    n = g.shape[0]
    cols = shape[-1]
    rows = w.size // cols
    tr = _pick(rows, 512, 8) if rows * cols * 4 > (1 << 20) else rows
    spec = pl.BlockSpec((tr, cols), lambda i: (i, 0))

    def body(w_ref, g_ref, m_ref, v_ref, go_ref, d_ref, mo_ref, vo_ref):
        gs = g_ref[0].astype(F32)
        for k in range(1, n):
            gs = gs + g_ref[k].astype(F32)
        mn = ADAM_B1 * m_ref[...] + (1.0 - ADAM_B1) * gs
        vn = ADAM_B2 * v_ref[...] + (1.0 - ADAM_B2) * jnp.square(gs)
        m_hat = mn / (1.0 - ADAM_B1 ** ADAM_STEP)
        v_hat = vn / (1.0 - ADAM_B2 ** ADAM_STEP)
        go_ref[...] = gs
        d_ref[...] = -ADAM_LR * (m_hat / (jnp.sqrt(v_hat) + ADAM_EPS) + ADAM_WD * w_ref[...])
        mo_ref[...] = mn
        vo_ref[...] = vn

    outs = pl.pallas_call(
        body, name=name, grid=(rows // tr,),
        in_specs=[spec, pl.BlockSpec((n, tr, cols), lambda i: (0, i, 0)), spec, spec],
        out_specs=[spec] * 4, out_shape=[SDS((rows, cols), F32)] * 4, compiler_params=_params(1),
    )(w.reshape(rows, cols), g.reshape(n, rows, cols), m.reshape(rows, cols), v.reshape(rows, cols))
    return tuple(o.reshape(shape) for o in outs)


def _sum_slots(recv):
    n, rows, cols = recv.shape
    tr = _pick(rows, 256, 16)

    def body(r_ref, o_ref):
        acc = r_ref[0].astype(F32)
        for k in range(1, n):
            acc = acc + r_ref[k].astype(F32)
        o_ref[...] = acc

    return pl.pallas_call(
        body, name="sum_slots", grid=(rows // tr,),
        in_specs=[pl.BlockSpec((n, tr, cols), lambda i: (0, i, 0))],
        out_specs=pl.BlockSpec((tr, cols), lambda i: (i, 0)),
        out_shape=SDS((rows, cols), F32), compiler_params=_params(1),
    )(recv)


def _position():
    return lax.axis_index("x"), lax.axis_index("y"), lax.axis_index("c")


def _peer(k, x, y, c):
    return (1 - x if k & 4 else x, 1 - y if k & 2 else y, 1 - c if k & 1 else c)


def _index(pos):
    return 4 * pos[0] + 2 * pos[1] + pos[2]


def _gather_small(v, name):
    rows, lanes = v.shape

    def body(x_ref, out_ref, send_sems, recv_sems, local_sem):
        me = _position()
        mine = pltpu.make_async_copy(x_ref, out_ref.at[_index(me)], local_sem)
        mine.start()

        def copy(k, slot):
            return pltpu.make_async_remote_copy(
                src_ref=x_ref, dst_ref=out_ref.at[slot], send_sem=send_sems.at[k - 1],
                recv_sem=recv_sems.at[k - 1], device_id=_peer(k, *me), device_id_type=MESH)

        sends = [copy(k, _index(me)) for k in range(1, N_DEV)]
        for cp in sends:
            cp.start()
        for k in range(1, N_DEV):
            copy(k, _index(_peer(k, *me))).wait_recv()
        for cp in sends:
            cp.wait_send()
        mine.wait()

    return pl.pallas_call(
        body, name=name, out_shape=SDS((N_DEV, rows, lanes), v.dtype),
        in_specs=[pl.BlockSpec(memory_space=pltpu.VMEM)],
        out_specs=pl.BlockSpec(memory_space=pltpu.VMEM),
        scratch_shapes=[pltpu.SemaphoreType.DMA((N_DEV - 1,)), pltpu.SemaphoreType.DMA((N_DEV - 1,)),
                        pltpu.SemaphoreType.DMA],
        compiler_params=pltpu.CompilerParams(vmem_limit_bytes=VMEM_LIMIT_V7X),
    )(v)


def _gather_big(v, name):
    rows, cols = v.shape

    def body(x_ref, out_ref, send_sems, recv_sems, local_sem):
        x, y, c = _position()
        me, sibling = (x, y, c), (x, y, 1 - c)
        chips = [(1 - x, y), (x, 1 - y), (1 - x, 1 - y)]

        def copy(k, block, to, src=None):
            slot = out_ref.at[_index(block)]
            return pltpu.make_async_remote_copy(
                src_ref=slot if src is None else src, dst_ref=slot, send_sem=send_sems.at[k],
                recv_sem=recv_sems.at[k], device_id=to, device_id_type=MESH)

        mine = pltpu.make_async_copy(x_ref, out_ref.at[_index(me)], local_sem)
        mine.start()
        first = [copy(0, me, sibling, src=x_ref)]
        first += [copy(1 + j, me, (*chip, c), src=x_ref) for j, chip in enumerate(chips)]
        for cp in first:
            cp.start()
        passed = [copy(4 + j, (*chip, c), sibling) for j, chip in enumerate(chips)]
        for j, chip in enumerate(chips):
            copy(1 + j, (*chip, c), me).wait_recv()
            passed[j].start()
        copy(0, sibling, me).wait_recv()
        for j, chip in enumerate(chips):
            copy(4 + j, (*chip, 1 - c), me).wait_recv()
        for cp in first + passed:
            cp.wait_send()
        mine.wait()

    return pl.pallas_call(
        body, name=name, out_shape=SDS((N_DEV, rows, cols), v.dtype),
        in_specs=[pl.BlockSpec(memory_space=pl.ANY)],
        out_specs=pl.BlockSpec(memory_space=pl.ANY),
        scratch_shapes=[pltpu.SemaphoreType.DMA((N_DEV - 1,)), pltpu.SemaphoreType.DMA((N_DEV - 1,)),
                        pltpu.SemaphoreType.DMA],
    )(v)


def _exchange(p, name):
    n, rows, cols = p.shape

    def body(p_ref, out_ref, send_sems, recv_sems, local_sem):
        me = _position()
        mine = pltpu.make_async_copy(p_ref.at[_index(me)], out_ref.at[_index(me)], local_sem)
        mine.start()

        def copy(k, src_slot, dst_slot):
            return pltpu.make_async_remote_copy(
                src_ref=p_ref.at[src_slot], dst_ref=out_ref.at[dst_slot], send_sem=send_sems.at[k - 1],
                recv_sem=recv_sems.at[k - 1], device_id=_peer(k, *me), device_id_type=MESH)

        sends = [copy(k, _index(_peer(k, *me)), _index(me)) for k in range(1, N_DEV)]
        for cp in sends:
            cp.start()
        for k in range(1, N_DEV):
            copy(k, _index(me), _index(_peer(k, *me))).wait_recv()
        for cp in sends:
            cp.wait_send()
        mine.wait()

    return pl.pallas_call(
        body, name=name, out_shape=SDS((n, rows, cols), p.dtype),
        in_specs=[pl.BlockSpec(memory_space=pl.ANY)],
        out_specs=pl.BlockSpec(memory_space=pl.ANY),
        scratch_shapes=[pltpu.SemaphoreType.DMA((N_DEV - 1,)), pltpu.SemaphoreType.DMA((N_DEV - 1,)),
                        pltpu.SemaphoreType.DMA],
    )(p)


def _pack_rows(arrays, lanes, dtype):
    flat = jnp.concatenate([a.astype(dtype).reshape(-1) for a in arrays])
    pad = (-flat.size) % (16 * lanes)
    if pad:
        flat = jnp.concatenate([flat, jnp.zeros((pad,), dtype)])
    return flat.reshape(-1, lanes)


def _unpack_rows(packed, shapes):
    n = packed.shape[0]
    flat = packed.reshape(n, -1)
    out, off = [], 0
    for shp in shapes:
        size = math.prod(shp)
        out.append(flat[:, off:off + size].reshape((n,) + tuple(shp)))
        off += size
    return out


def _unshard(g8, axis):
    moved = jnp.moveaxis(g8, 0, axis)
    shp = list(moved.shape)
    shp[axis:axis + 2] = [shp[axis] * shp[axis + 1]]
    return moved.reshape(shp)


def _split8(full, axis):
    shp = list(full.shape)
    shp[axis:axis + 1] = [N_DEV, shp[axis] // N_DEV]
    return jnp.moveaxis(full.reshape(shp), axis, 0)


def _my_shard(g, axis, me):
    size = g.shape[axis + 1] // N_DEV
    return lax.dynamic_slice_in_dim(g, me * size, size, axis=axis + 1)


BIG_WEIGHTS = (("ffn_w_up", 2), ("ffn_w_down", 1), ("attn_w_qkv", 2), ("attn_w_o", 1),
               ("ret_w_in", 2), ("ret_w_out", 1), ("pool_w", 2))
SMALL_SHARDED = (("norm_w", 2), ("pool_b", 1), ("pool_scale", 1), ("ret_gn_w", 1), ("ffn_conv_w", 2))
REPLICATED = ("ada_b", "attn_q_gain", "attn_k_gain", "ret_decay_logit", "ffn_conv_b")
WEIGHT_ORDER = ("c_ctx", "ada_w", "ada_b", "norm_w", "pool_w", "pool_b", "pool_scale", "attn_w_qkv",
                "attn_q_gain", "attn_k_gain", "attn_w_o", "ret_w_in", "ret_decay_logit", "ret_gn_w",
                "ret_w_out", "ffn_w_up", "ffn_conv_w", "ffn_conv_b", "ffn_w_down")


def _local_step(x0, target, mods, P, s_len, l_len):
    n_lat = s_len // ROW_TILE
    nw = P["norm_w"]
    lgt = jnp.broadcast_to(P["ret_decay_logit"][0][:, :, None, None], (2, RET_HEADS, 1, 128))
    cos, sin = _rope_tables(s_len, l_len)
    h_dtype = [F32 if i % 3 == 0 else BF16 for i in range(DEPTH)]
    saved = []
    X = x0
    h = _res_norm(X, None, None, 0, nw[0, 0], mods[0], 0, h_dtype[0], n_lat, "norm_first")
    for i in range(DEPTH):
        kind, j, mod = i % 3, i // 3, mods[i]
        sv = {"X": X, "h": h}
        if kind == 0:
            y = _pool_fwd(h, P["pool_w"][j], P["pool_b"][j:j + 1], P["pool_scale"][j:j + 1],
                          n_lat, s_len, l_len, f"pool_fwd{i}")
        elif kind == 1:
            qkv = _mm(h, P["attn_w_qkv"][j], "nn", F32, f"qkv{i}")
            q, k, v = _qk_prep_fwd(qkv, P["attn_q_gain"][j:j + 1], P["attn_k_gain"][j:j + 1], cos, sin)
            o, lse = _flash_fwd(q, k, v, s_len, l_len)
            y = _mm(o, P["attn_w_o"][j], "nn", F32, f"attn_out{i}")
            sv.update(qkv=qkv, q=q, k=k, v=v, o=o, lse=lse)
        else:
            proj = _mm(h, P["ret_w_in"][j], "nn", F32, f"ret_in{i}")
            o2, states = _ret_fwd(proj, lgt, s_len, l_len)
            gated = _readout_fwd(o2, proj, P["ret_gn_w"][j:j + 1])
            y = _mm(gated, P["ret_w_out"][j], "nn", F32, f"ret_out{i}")
            sv.update(proj=proj, o2=o2, states=states, gated=gated)
        X1, h2 = _res_norm(X, y, mod, 0, nw[i, 1], mod, 1, BF16, n_lat, f"res_norm_mid{i}")
        u = _mm(h2, P["ffn_w_up"][i], "nn", F32, f"ffn_up{i}")
        gact = _conv_gate_fwd(u, P["ffn_conv_w"][i], P["ffn_conv_b"][i:i + 1], n_lat, f"conv_gate_fwd{i}")
        f = _mm(gact, P["ffn_w_down"][i], "nn", F32, f"ffn_down{i}")
        sv.update(y=y, X1=X1, h2=h2, u=u, gact=gact, f=f)
        saved.append(sv)
        if i + 1 < DEPTH:
            X, h = _res_norm(X1, f, mod, 1, nw[i + 1, 0], mods[i + 1], 0, h_dtype[i + 1], n_lat,
                             f"res_norm_end{i}")
        else:
            X = _res_norm(X1, f, mod, 1, None, None, 0, None, n_lat, "res_last")

    dX, loss = _loss_bwd(X, target, n_lat)
    G = {name: [None] * P[name].shape[0] for name in
         ("pool_w", "pool_b", "pool_scale", "attn_w_qkv", "attn_q_gain", "attn_k_gain", "attn_w_o", "ret_w_in",
          "ret_decay_logit", "ret_gn_w", "ret_w_out", "ffn_w_up", "ffn_conv_w", "ffn_conv_b", "ffn_w_down")}
    dnw = [[None, None] for _ in range(DEPTH)]
    dmods = [None] * DEPTH
    for i in reversed(range(DEPTH)):
        kind, j, mod, sv = i % 3, i // 3, mods[i], saved[i]
        df, dg2 = _gate_bwd(dX, sv["f"], mod, 1, BF16, n_lat, f"gate_bwd_ffn{i}")
        dgact = _mm(df, P["ffn_w_down"][i], "nt", F32, f"ffn_down_dx{i}")
        G["ffn_w_down"][i] = _mm(sv["gact"], df, "tn", F32, f"ffn_down_dw{i}")
        du, dcw, dcb = _conv_gate_bwd(sv["u"], dgact, P["ffn_conv_w"][i], P["ffn_conv_b"][i:i + 1], n_lat,
                                      f"conv_gate_bwd{i}")
        G["ffn_conv_w"][i], G["ffn_conv_b"][i] = dcw, dcb[0]
        dh2 = _mm(du, P["ffn_w_up"][i], "nt", F32, f"ffn_up_dx{i}")
        G["ffn_w_up"][i] = _mm(sv["h2"], du, "tn", F32, f"ffn_up_dw{i}")
        dX1, dnw[i][1], dsh2, dsc2 = _norm_bwd(dX, dh2, sv["X1"], nw[i, 1], mod, 1, n_lat, f"norm_bwd_ffn{i}")
        dy, dg1 = _gate_bwd(dX1, sv["y"], mod, 0, F32 if kind == 0 else BF16, n_lat, f"gate_bwd_mix{i}")
        h = sv["h"]
        if kind == 0:
            dh, dpw, dpb, dps = _pool_bwd(h, dy, P["pool_w"][j], P["pool_b"][j:j + 1], P["pool_scale"][j:j + 1],
                                          n_lat, s_len, l_len, f"pool_bwd{i}")
            G["pool_w"][j], G["pool_b"][j], G["pool_scale"][j] = dpw, dpb[0], dps[0]
        elif kind == 1:
            do = _mm(dy, P["attn_w_o"][j], "nt", F32, f"attn_out_dx{i}")
            G["attn_w_o"][j] = _mm(sv["o"], dy, "tn", F32, f"attn_out_dw{i}")
            dq, dk, dv = _flash_bwd(sv["q"], sv["k"], sv["v"], sv["o"], sv["lse"], do, s_len, l_len)
            dqkv, dqg, dkg = _qk_prep_bwd(sv["qkv"], dq, dk, dv, P["attn_q_gain"][j:j + 1],
                                          P["attn_k_gain"][j:j + 1], cos, sin)
            G["attn_q_gain"][j], G["attn_k_gain"][j] = dqg[0], dkg[0]
            dh = _mm(dqkv, P["attn_w_qkv"][j], "nt", F32, f"qkv_dx{i}")
            G["attn_w_qkv"][j] = _mm(h, dqkv, "tn", F32, f"qkv_dw{i}")
        else:
            dgated = _mm(dy, P["ret_w_out"][j], "nt", F32, f"ret_out_dx{i}")
            G["ret_w_out"][j] = _mm(sv["gated"], dy, "tn", F32, f"ret_out_dw{i}")
            do, dg, dgn = _readout_bwd(sv["o2"], sv["proj"], P["ret_gn_w"][j:j + 1], dgated)
            dq2, dk2, dv2, dlg = _ret_bwd(sv["proj"], lgt, sv["states"], do, s_len, l_len)
            dproj = _ret_dproj(dq2, dk2, dv2, dg)
            G["ret_gn_w"][j], G["ret_decay_logit"][j] = dgn[0], dlg[:, :, 0, 0]
            dh = _mm(dproj, P["ret_w_in"][j], "nt", F32, f"ret_in_dx{i}")
            G["ret_w_in"][j] = _mm(h, dproj, "tn", F32, f"ret_in_dw{i}")
        dX, dnw[i][0], dsh1, dsc1 = _norm_bwd(dX1, dh, sv["X"], nw[i, 0], mod, 0, n_lat, f"norm_bwd_mix{i}")
        dmods[i] = jnp.concatenate([dsh1, dsc1, dg1, dsh2, dsc2, dg2], axis=1)
    grads = {name: jnp.stack(parts) for name, parts in G.items()}
    grads["norm_w"] = jnp.stack([jnp.concatenate(pair, axis=0) for pair in dnw])
    return loss, dX, grads, jnp.stack(dmods)


def kernel(x, c, ctx, c_ctx, ada_w, ada_b, norm_w, pool_w, pool_b, pool_scale, attn_w_qkv, attn_q_gain,
           attn_k_gain, attn_w_o, ret_w_in, ret_decay_logit, ret_gn_w, ret_w_out, ffn_w_up, ffn_conv_w,
           ffn_conv_b, ffn_w_down, loss_target, m_c_ctx, m_ada_w, m_ada_b, m_norm_w, m_pool_w, m_pool_b,
           m_pool_scale, m_attn_w_qkv, m_attn_q_gain, m_attn_k_gain, m_attn_w_o, m_ret_w_in,
           m_ret_decay_logit, m_ret_gn_w, m_ret_w_out, m_ffn_w_up, m_ffn_conv_w, m_ffn_conv_b, m_ffn_w_down,
           v_c_ctx, v_ada_w, v_ada_b, v_norm_w, v_pool_w, v_pool_b, v_pool_scale, v_attn_w_qkv, v_attn_q_gain,
           v_attn_k_gain, v_attn_w_o, v_ret_w_in, v_ret_decay_logit, v_ret_gn_w, v_ret_w_out, v_ffn_w_up,
           v_ffn_conv_w, v_ffn_conv_b, v_ffn_w_down):
    A = dict(locals())
    me = _index(_position())
    s_len, D = x.shape[1], x.shape[2]
    l_len = ctx.shape[1]
    assert s_len % ROW_TILE == 0 and l_len % ROW_TILE == 0 and s_len % GRID_W == 0

    small = [A[n] for n, _ in SMALL_SHARDED]
    got = _gather_small(_pack_rows([c] + small, 128, F32), "gather_c_small")
    parts = _unpack_rows(got, [c.shape] + [a.shape for a in small])
    c_all = parts[0].reshape(N_DEV, D)
    P = {n: _unshard(g8, ax) for (n, ax), g8 in zip(SMALL_SHARDED, parts[1:])}

    c_rows = jnp.concatenate([c_all, c_ctx.reshape(1, D), jnp.zeros((7, D), F32)], axis=0)
    cols = ada_w.shape[2]
    ada_b_shard = lax.dynamic_slice_in_dim(ada_b, me * cols, cols, axis=1).reshape(DEPTH, 1, cols)
    mod_shard = _ada_fwd(c_rows, ada_w, ada_b_shard)
    got = _gather_small(mod_shard.reshape(-1, 128), "gather_mod").reshape(N_DEV, DEPTH, 16, cols)
    mod_lat = lax.dynamic_index_in_dim(got, me, axis=2, keepdims=False)
    mod_ctx = got[:, :, 8, :]
    mods = jnp.stack([jnp.moveaxis(mod_lat, 0, 1).reshape(DEPTH, 6, D),
                      jnp.moveaxis(mod_ctx, 0, 1).reshape(DEPTH, 6, D)], axis=1)

    big = [A[n].astype(BF16) for n, _ in BIG_WEIGHTS]
    rows = [b.size // D for b in big]
    got = _gather_big(jnp.concatenate([b.reshape(-1, D) for b in big], axis=0), "gather_weights")
    off = 0
    for (n, ax), b, r in zip(BIG_WEIGHTS, big, rows):
        P[n] = _unshard(got[:, off:off + r].reshape((N_DEV,) + b.shape), ax)
        off += r
    for n in REPLICATED:
        P[n] = A[n]

    x0 = jnp.concatenate([x[0], ctx[0]], axis=0)
    loss8, dx0, G, dmods = _local_step(x0, loss_target[0], mods, P, s_len, l_len)
    loss = lax.psum(loss8[0, 0], ("x", "y", "c"))
    grad_x = dx0[:s_len].reshape(x.shape)

    small_names = ["dmods"] + list(REPLICATED[1:]) + [n for n, _ in SMALL_SHARDED]
    small_parts = [dmods] + [G[n] for n in small_names[1:]]
    got = _gather_small(_pack_rows(small_parts, 128, F32), "gather_small_grads")
    S8 = dict(zip(small_names, _unpack_rows(got, [a.shape for a in small_parts])))

    dm = S8["dmods"].reshape(N_DEV, DEPTH, 2, 6 * D)
    dm_mine = lax.dynamic_slice_in_dim(dm, me * cols, cols, axis=3)
    g_ada_w, pc = _ada_bwd(c_rows, ada_w, jnp.moveaxis(dm_mine[:, :, 0], 0, 1), jnp.moveaxis(dm_mine[:, :, 1], 0, 1))
    pc8 = _gather_small(pc.reshape(-1, 128), "gather_c_ctx_grad").reshape(N_DEV, 8, D)

    send = jnp.concatenate([_split8(G[n], ax).astype(BF16).reshape(N_DEV, -1, D) for n, ax in BIG_WEIGHTS], axis=1)
    gsum = _sum_slots(_exchange(send, "exchange_grads"))

    g_in = {"c_ctx": pc8[:, 0, :], "ada_w": g_ada_w[None],
            "ada_b": jnp.moveaxis(dm, 2, 1).reshape(2 * N_DEV, DEPTH, 6 * D)}
    for n in REPLICATED[1:]:
        g_in[n] = S8[n]
    for n, ax in SMALL_SHARDED:
        g_in[n] = _my_shard(S8[n], ax, me)
    off = 0
    for (n, ax), b, r in zip(BIG_WEIGHTS, big, rows):
        g_in[n] = gsum[off:off + r].reshape((1,) + b.shape)
        off += r
    res = {n: _adamw(A[n], g_in[n], A["m_" + n], A["v_" + n], "adamw_" + n) for n in WEIGHT_ORDER}
    outs = [loss, grad_x]
    for slot in range(4):
        outs += [res[n][slot] for n in WEIGHT_ORDER]
    return tuple(outs)
```

```python
import functools
import math

import jax
import jax.numpy as jnp
from jax import lax
from jax.experimental import pallas as pl
from jax.experimental.pallas import tpu as pltpu

F32 = jnp.float32
BF16 = jnp.bfloat16
SDS = jax.ShapeDtypeStruct
MESH = pl.DeviceIdType.MESH

N_DEV = 8
EPS = 1e-6
DEPTH = 4
GRID_W = 64
POOL_WINDOWS = (2, 4, 8, 16)
N_HEADS = 8
N_KV = 2
HEAD_DIM = 128
ROPE_THETA = 10000.0
RET_HEADS = 4
RET_DK = 256
RET_DV = 512
RET_CHUNK = 128
ADAM_LR = 0.001
ADAM_B1 = 0.9
ADAM_B2 = 0.999
ADAM_EPS = 1e-08
ADAM_WD = 0.01
ADAM_STEP = 10

ROW_TILE = 256
HALO = 8
VMEM_LIMIT_V7X = 56 * 1024 * 1024


def _params(n_axes=0):
    sem = ("arbitrary",) * n_axes if n_axes else None
    return pltpu.CompilerParams(dimension_semantics=sem, vmem_limit_bytes=VMEM_LIMIT_V7X)


def _pick(n, cap, mult):
    best = None
    for d in range(mult, min(n, cap) + 1, mult):
        if n % d == 0:
            best = d
    return best if best is not None else n


def _dot(a, b):
    return jnp.dot(a, b, preferred_element_type=F32)


def _dot_nt(a, b):
    return lax.dot_general(a, b, (((1,), (1,)), ((), ())), preferred_element_type=F32)


def _dot_tn(a, b):
    return lax.dot_general(a, b, (((0,), (0,)), ((), ())), preferred_element_type=F32)


def _bf(v):
    return v.astype(BF16)


def _sigmoid(v):
    return 1.0 / (1.0 + jnp.exp(-v))


def _mm(a, b, mode, out_dtype, name):
    if mode == "nn":
        (M, K), (K2, N) = a.shape, b.shape
    elif mode == "nt":
        (M, K), (N, K2) = a.shape, b.shape
    else:
        (K, M), (K2, N) = a.shape, b.shape
    assert K == K2, (a.shape, b.shape, mode)
    if mode == "tn":
        tm, tk = _pick(M, 1408, 128), _pick(K, 768, 16)
    else:
        tm, tk = _pick(M, 768, 16), _pick(K, 2816, 128)
    tn = _pick(N, 512, 128)
    nk = K // tk
    if mode == "nn":
        a_spec = pl.BlockSpec((tm, tk), lambda i, j, k: (i, k))
        b_spec = pl.BlockSpec((tk, tn), lambda i, j, k: (k, j))
    elif mode == "nt":
        a_spec = pl.BlockSpec((tm, tk), lambda i, j, k: (i, k))
        b_spec = pl.BlockSpec((tn, tk), lambda i, j, k: (j, k))
    else:
        a_spec = pl.BlockSpec((tk, tm), lambda i, j, k: (k, i))
        b_spec = pl.BlockSpec((tk, tn), lambda i, j, k: (k, j))
    dot = {"nn": _dot, "nt": _dot_nt, "tn": _dot_tn}[mode]

    def body(a_ref, b_ref, o_ref, acc_ref):
        part = dot(_bf(a_ref[...]), _bf(b_ref[...]))
        if nk == 1:
            o_ref[...] = part.astype(out_dtype)
        else:
            k = pl.program_id(2)

            @pl.when(k == 0)
            def _():
                acc_ref[...] = part

            @pl.when(k > 0)
            def _():
                acc_ref[...] += part

            @pl.when(k == nk - 1)
            def _():
                o_ref[...] = acc_ref[...].astype(out_dtype)

    return pl.pallas_call(
        body, name=name, grid=(M // tm, N // tn, nk),
        in_specs=[a_spec, b_spec],
        out_specs=pl.BlockSpec((tm, tn), lambda i, j, k: (i, j)),
        out_shape=SDS((M, N), out_dtype),
        scratch_shapes=[pltpu.VMEM((tm, tn), F32)],
        compiler_params=_params(3),
    )(a, b)


def _seg_spec(n_lat, d):
    return pl.BlockSpec((1, 6, d), lambda i: ((i >= n_lat).astype(jnp.int32), 0, 0))


def _seg_acc_spec(n_lat, d):
    return pl.BlockSpec((1, 1, d), lambda i: ((i >= n_lat).astype(jnp.int32), 0, 0))


def _res_norm(x, y, gmod, gk, nw, nmod, nk, h_dtype, n_lat, name):
    R, D = x.shape
    has_res, has_norm = y is not None, nw is not None
    row = pl.BlockSpec((ROW_TILE, D), lambda i: (i, 0))
    vec = pl.BlockSpec((1, D), lambda i: (0, 0))
    ins, specs, outs, ospecs = [x], [row], [], []
    if has_res:
        ins += [y, gmod]
        specs += [row, _seg_spec(n_lat, D)]
        outs.append(SDS((R, D), F32))
        ospecs.append(row)
    if has_norm:
        ins += [nw.reshape(1, D), nmod]
        specs += [vec, _seg_spec(n_lat, D)]
        outs.append(SDS((R, D), h_dtype))
        ospecs.append(row)

    def body(*refs):
        refs = list(refs)
        z = refs.pop(0)[...]
        if has_res:
            y_ref, g_ref = refs.pop(0), refs.pop(0)
            z = z + g_ref[0, pl.ds(3 * gk + 2, 1), :] * y_ref[...].astype(F32)
        if has_norm:
            nw_ref, m_ref = refs.pop(0), refs.pop(0)
        if has_res:
            refs.pop(0)[...] = z
        if has_norm:
            r = lax.rsqrt(jnp.mean(z * z, axis=-1, keepdims=True) + EPS)
            h = (z * r) * nw_ref[...]
            h = h * (1.0 + m_ref[0, pl.ds(3 * nk + 1, 1), :]) + m_ref[0, pl.ds(3 * nk, 1), :]
            refs.pop(0)[...] = h.astype(h_dtype)

    res = pl.pallas_call(
        body, name=name, grid=(R // ROW_TILE,), in_specs=specs, out_specs=ospecs,
        out_shape=outs, compiler_params=_params(1),
    )(*ins)
    return res if len(res) > 1 else res[0]


def _gate_bwd(dz, y, mod, k, out_dtype, n_lat, name):
    R, D = dz.shape
    row = pl.BlockSpec((ROW_TILE, D), lambda i: (i, 0))

    def body(dz_ref, y_ref, m_ref, dy_ref, dg_ref):
        i = pl.program_id(0)
        dzv = dz_ref[...]
        dy_ref[...] = (m_ref[0, pl.ds(3 * k + 2, 1), :] * dzv).astype(out_dtype)

        @pl.when((i == 0) | (i == n_lat))
        def _():
            dg_ref[...] = jnp.zeros_like(dg_ref)

        dg_ref[0] += jnp.sum(dzv * y_ref[...].astype(F32), axis=0, keepdims=True)

    return pl.pallas_call(
        body, name=name, grid=(R // ROW_TILE,),
        in_specs=[row, row, _seg_spec(n_lat, D)],
        out_specs=[row, _seg_acc_spec(n_lat, D)],
        out_shape=[SDS((R, D), out_dtype), SDS((2, 1, D), F32)],
        compiler_params=_params(1),
    )(dz, y, mod)


def _norm_bwd(dz, dh, x, nw, mod, k, n_lat, name):
    R, D = x.shape
    row = pl.BlockSpec((ROW_TILE, D), lambda i: (i, 0))
    vec = pl.BlockSpec((1, D), lambda i: (0, 0))

    def body(dz_ref, dh_ref, x_ref, nw_ref, m_ref, dx_ref, dnw_ref, dsh_ref, dsc_ref):
        i = pl.program_id(0)
        xv = x_ref[...]
        dhv = dh_ref[...].astype(F32)
        nwv = nw_ref[...]
        sc1 = 1.0 + m_ref[0, pl.ds(3 * k + 1, 1), :]
        r = lax.rsqrt(jnp.mean(xv * xv, axis=-1, keepdims=True) + EPS)
        xhat = xv * r
        a = dhv * (nwv * sc1)
        dx_ref[...] = dz_ref[...] + r * (a - xhat * jnp.mean(a * xhat, axis=-1, keepdims=True))

        @pl.when(i == 0)
        def _():
            dnw_ref[...] = jnp.zeros_like(dnw_ref)

        @pl.when((i == 0) | (i == n_lat))
        def _():
            dsh_ref[...] = jnp.zeros_like(dsh_ref)
            dsc_ref[...] = jnp.zeros_like(dsc_ref)

        dnw_ref[...] += jnp.sum(dhv * xhat, axis=0, keepdims=True) * sc1
        dsh_ref[0] += jnp.sum(dhv, axis=0, keepdims=True)
        dsc_ref[0] += jnp.sum(dhv * xhat, axis=0, keepdims=True) * nwv

    return pl.pallas_call(
        body, name=name, grid=(R // ROW_TILE,),
        in_specs=[row, row, row, vec, _seg_spec(n_lat, D)],
        out_specs=[row, vec, _seg_acc_spec(n_lat, D), _seg_acc_spec(n_lat, D)],
        out_shape=[SDS((R, D), F32), SDS((1, D), F32), SDS((2, 1, D), F32), SDS((2, 1, D), F32)],
        compiler_params=_params(1),
    )(dz, dh, x, nw.reshape(1, D), mod)


def _loss_bwd(xf, target, n_lat):
    R, D = xf.shape
    row = pl.BlockSpec((ROW_TILE, D), lambda i: (i, 0))
    tgt = pl.BlockSpec((ROW_TILE, D), lambda i: (jnp.minimum(i, n_lat - 1), 0))

    def body(x_ref, t_ref, dx_ref, loss_ref):
        i = pl.program_id(0)
        e = jnp.where(i < n_lat, x_ref[...] - t_ref[...], 0.0)
        dx_ref[...] = e * (1.0 / D)

        @pl.when(i == 0)
        def _():
            loss_ref[...] = jnp.zeros_like(loss_ref)

        loss_ref[...] += 0.5 * jnp.sum(jnp.mean(e * e, axis=-1, keepdims=True))

    return pl.pallas_call(
        body, name="loss_bwd", grid=(R // ROW_TILE,),
        in_specs=[row, tgt],
        out_specs=[row, pl.BlockSpec((8, 128), lambda i: (0, 0))],
        out_shape=[SDS((R, D), F32), SDS((8, 128), F32)],
        compiler_params=_params(1),
    )(xf, target)


def _halo_specs(n_tiles, width, tile=ROW_TILE):
    per = tile // HALO
    prev = pl.BlockSpec((HALO, width), lambda i: (jnp.maximum(i * per - 1, 0), 0))
    nxt = pl.BlockSpec((HALO, width), lambda i: (jnp.minimum((i + 1) * per, n_tiles * per - 1), 0))
    return prev, nxt


def _edge_flags(i, n_lat, n_tiles):
    first = (i == 0) | (i == n_lat)
    last = (i == n_lat - 1) | (i == n_tiles - 1)
    return first, last


def _conv_gate_fwd(u, conv_w, conv_b, n_lat, name):
    R, F2 = u.shape
    F = F2 // 2
    n_tiles = R // ROW_TILE
    T = ROW_TILE
    cw = _pick(F, 256, 128)
    row = pl.BlockSpec((T, F2), lambda i: (i, 0))
    prev, nxt = _halo_specs(n_tiles, F2)

    def body(u_ref, p_ref, n_ref, w_ref, b_ref, o_ref):
        i = pl.program_id(0)
        first, last = _edge_flags(i, n_lat, n_tiles)
        ridx = lax.broadcasted_iota(jnp.int32, (T, 1), 0)

        def conv(c0):
            cols = pl.ds(c0, cw)
            uv = u_ref[:, cols]
            pr = jnp.where(first, 0.0, p_ref[pl.ds(HALO - 1, 1), cols])
            nx = jnp.where(last, 0.0, n_ref[pl.ds(0, 1), cols])
            up = jnp.where(ridx == 0, pr, pltpu.roll(uv, 1, 0))
            un = jnp.where(ridx == T - 1, nx, pltpu.roll(uv, T - 1, 0))
            return (up * w_ref[pl.ds(0, 1), cols] + uv * w_ref[pl.ds(1, 1), cols]
                    + un * w_ref[pl.ds(2, 1), cols] + b_ref[:, cols])

        for c0 in range(0, F, cw):
            ca, cv = conv(c0), conv(F + c0)
            o_ref[:, pl.ds(c0, cw)] = (ca * _sigmoid(ca) * cv).astype(BF16)

    return pl.pallas_call(
        body, name=name, grid=(n_tiles,),
        in_specs=[row, prev, nxt, pl.BlockSpec((3, F2), lambda i: (0, 0)),
                  pl.BlockSpec((1, F2), lambda i: (0, 0))],
        out_specs=pl.BlockSpec((T, F), lambda i: (i, 0)),
        out_shape=SDS((R, F), BF16), compiler_params=_params(1),
    )(u, u, u, conv_w, conv_b)


def _conv_gate_bwd(u, dgact, conv_w, conv_b, n_lat, name):
    R, F2 = u.shape
    F = F2 // 2
    n_tiles = R // ROW_TILE
    T, N = ROW_TILE, ROW_TILE + 2 * HALO
    cw = _pick(F, 256, 128)
    rowu = pl.BlockSpec((T, F2), lambda i: (i, 0))
    rowg = pl.BlockSpec((T, F), lambda i: (i, 0))
    pu, nu = _halo_specs(n_tiles, F2)
    pg, ng = _halo_specs(n_tiles, F)

    def body(u_ref, pu_ref, nu_ref, g_ref, pg_ref, ng_ref, w_ref, b_ref, du_ref, dw_ref, db_ref):
        i = pl.program_id(0)
        first, last = _edge_flags(i, n_lat, n_tiles)

        @pl.when(i == 0)
        def _():
            dw_ref[...] = jnp.zeros_like(dw_ref)
            db_ref[...] = jnp.zeros_like(db_ref)

        def ext(t_ref, p_ref, n_ref, cols):
            pr = jnp.where(first, 0.0, p_ref[:, cols])
            nx = jnp.where(last, 0.0, n_ref[:, cols])
            return jnp.concatenate([pr, t_ref[:, cols], nx], axis=0)

        def conv(c0):
            cols = pl.ds(c0, cw)
            e = ext(u_ref, pu_ref, nu_ref, cols)
            up, un = pltpu.roll(e, 1, 0), pltpu.roll(e, N - 1, 0)
            c = (up * w_ref[pl.ds(0, 1), cols] + e * w_ref[pl.ds(1, 1), cols]
                 + un * w_ref[pl.ds(2, 1), cols] + b_ref[:, cols])
            return c, up, e, un

        def back(c0, dc, up, e, un):
            cols = pl.ds(c0, cw)
            du = (pltpu.roll(dc, N - 1, 0) * w_ref[pl.ds(0, 1), cols] + dc * w_ref[pl.ds(1, 1), cols]
                  + pltpu.roll(dc, 1, 0) * w_ref[pl.ds(2, 1), cols])
            du_ref[:, cols] = du[HALO:HALO + T].astype(BF16)
            dct = dc[HALO:HALO + T]
            dw_ref[pl.ds(0, 1), cols] += jnp.sum(dct * up[HALO:HALO + T], axis=0, keepdims=True)
            dw_ref[pl.ds(1, 1), cols] += jnp.sum(dct * e[HALO:HALO + T], axis=0, keepdims=True)
            dw_ref[pl.ds(2, 1), cols] += jnp.sum(dct * un[HALO:HALO + T], axis=0, keepdims=True)
            db_ref[:, cols] += jnp.sum(dct, axis=0, keepdims=True)

        for c0 in range(0, F, cw):
            dg = ext(g_ref, pg_ref, ng_ref, pl.ds(c0, cw))
            ca, upa, ea, una = conv(c0)
            cv, upv, ev, unv = conv(F + c0)
            s = _sigmoid(ca)
            back(F + c0, dg * (ca * s), upv, ev, unv)
            back(c0, dg * cv * (s * (1.0 + ca * (1.0 - s))), upa, ea, una)

    return pl.pallas_call(
        body, name=name, grid=(n_tiles,),
        in_specs=[rowu, pu, nu, rowg, pg, ng, pl.BlockSpec((3, F2), lambda i: (0, 0)),
                  pl.BlockSpec((1, F2), lambda i: (0, 0))],
        out_specs=[rowu, pl.BlockSpec((3, F2), lambda i: (0, 0)), pl.BlockSpec((1, F2), lambda i: (0, 0))],
        out_shape=[SDS((R, F2), BF16), SDS((3, F2), F32), SDS((1, F2), F32)],
        compiler_params=_params(1),
    )(u, u, u, dgact, dgact, dgact, conv_w, conv_b)


def _pool_counts(i, n_lat, s_len, l_len, n_rows, offset):
    ctx = i >= n_lat
    t0 = jnp.where(ctx, i - n_lat, i) * ROW_TILE + offset
    seg = jnp.where(ctx, l_len, s_len)
    t = t0 + lax.broadcasted_iota(jnp.int32, (n_rows, 1), 0)
    out = []
    for win in POOL_WINDOWS:
        cnt = jnp.minimum(t + win // 2, seg) - jnp.maximum(t - win // 2, 0)
        out.append(jnp.maximum(cnt, 1).astype(F32))
    return out


def _window_sum(e, lo, hi, n):
    acc = None
    for j in range(lo, hi + 1):
        term = e if j == 0 else pltpu.roll(e, (-j) % n, 0)
        acc = term if acc is None else acc + term
    return acc


def _pool_fwd(h, w, b, scale, n_lat, s_len, l_len, name):
    R, D = h.shape
    G = D // 4
    n_tiles = R // ROW_TILE
    T, N = ROW_TILE, ROW_TILE + 2 * HALO
    row = pl.BlockSpec((T, D), lambda i: (i, 0))
    prev, nxt = _halo_specs(n_tiles, D)
    vec = pl.BlockSpec((1, D), lambda i: (0, 0))

    def body(h_ref, p_ref, n_ref, w_ref, b_ref, s_ref, y_ref):
        i = pl.program_id(0)
        first, last = _edge_flags(i, n_lat, n_tiles)
        cnts = _pool_counts(i, n_lat, s_len, l_len, T, 0)
        for g, win in enumerate(POOL_WINDOWS):
            cols = pl.ds(g * G, G)
            pr = jnp.where(first, 0.0, p_ref[:, cols])
            nx = jnp.where(last, 0.0, n_ref[:, cols])
            hv = h_ref[:, cols]
            e = jnp.concatenate([pr, hv, nx], axis=0)
            mean = _window_sum(e, -(win // 2), win // 2 - 1, N)[HALO:HALO + T] / cnts[g]
            yg = _dot(_bf(mean - hv), w_ref[g])
            y_ref[:, cols] = (yg + b_ref[:, cols]) * s_ref[:, cols]

    return pl.pallas_call(
        body, name=name, grid=(n_tiles,),
        in_specs=[row, prev, nxt, pl.BlockSpec((4, G, G), lambda i: (0, 0, 0)), vec, vec],
        out_specs=row, out_shape=SDS((R, D), F32), compiler_params=_params(1),
    )(h, h, h, w, b, scale)


def _pool_bwd(h, dy, w, b, scale, n_lat, s_len, l_len, name):
    R, D = h.shape
    G = D // 4
    n_tiles = R // ROW_TILE
    T, N = ROW_TILE, ROW_TILE + 2 * HALO
    row = pl.BlockSpec((T, D), lambda i: (i, 0))
    prev, nxt = _halo_specs(n_tiles, D)
    vec = pl.BlockSpec((1, D), lambda i: (0, 0))
    wspec = pl.BlockSpec((4, G, G), lambda i: (0, 0, 0))

    def body(h_ref, ph_ref, nh_ref, d_ref, pd_ref, nd_ref, w_ref, b_ref, s_ref,
             dh_ref, dw_ref, db_ref, ds_ref):
        i = pl.program_id(0)
        first, last = _edge_flags(i, n_lat, n_tiles)

        @pl.when(i == 0)
        def _():
            dw_ref[...] = jnp.zeros_like(dw_ref)
            db_ref[...] = jnp.zeros_like(db_ref)
            ds_ref[...] = jnp.zeros_like(ds_ref)

        cnts = _pool_counts(i, n_lat, s_len, l_len, T, 0)
        cnts_ext = _pool_counts(i, n_lat, s_len, l_len, N, -HALO)
        for g, win in enumerate(POOL_WINDOWS):
            cols = pl.ds(g * G, G)

            def ext(t_ref, p_ref, n_ref):
                pr = jnp.where(first, 0.0, p_ref[:, cols])
                nx = jnp.where(last, 0.0, n_ref[:, cols])
                return jnp.concatenate([pr, t_ref[:, cols], nx], axis=0)

            hv = h_ref[:, cols]
            mean = _window_sum(ext(h_ref, ph_ref, nh_ref), -(win // 2), win // 2 - 1, N)[HALO:HALO + T] / cnts[g]
            z = _bf(mean - hv)
            sc = s_ref[:, cols]
            dye = ext(d_ref, pd_ref, nd_ref)
            dt = _bf(dye * sc)
            dz = _dot_nt(dt, w_ref[g])
            dm = dz / cnts_ext[g]
            dh = _window_sum(dm, -(win // 2 - 1), win // 2, N) - dz
            dh_ref[:, cols] = dh[HALO:HALO + T]
            dyt = dye[HALO:HALO + T]
            dw_ref[g] += _dot_tn(z, dt[HALO:HALO + T])
            db_ref[:, cols] += jnp.sum(dyt * sc, axis=0, keepdims=True)
            ds_ref[:, cols] += jnp.sum(dyt * (_dot(z, w_ref[g]) + b_ref[:, cols]), axis=0, keepdims=True)

    return pl.pallas_call(
        body, name=name, grid=(n_tiles,),
        in_specs=[row, prev, nxt, row, prev, nxt, wspec, vec, vec],
        out_specs=[row, wspec, vec, vec],
        out_shape=[SDS((R, D), F32), SDS((4, G, G), F32), SDS((1, D), F32), SDS((1, D), F32)],
        compiler_params=_params(1),
    )(h, h, h, dy, dy, dy, w, b, scale)


def _rope_tables(s_len, l_len):
    t = jnp.arange(s_len)
    row = (t // GRID_W).astype(F32)
    col = (t % GRID_W).astype(F32)
    axis_dim = HEAD_DIM // 2
    inv = ROPE_THETA ** (-jnp.arange(0, axis_dim, 2, dtype=F32) / axis_dim)
    ar, ac = row[:, None] * inv, col[:, None] * inv
    cos = jnp.concatenate([jnp.cos(ar), jnp.cos(ar), jnp.cos(ac), jnp.cos(ac)], axis=-1)
    sin = jnp.concatenate([-jnp.sin(ar), jnp.sin(ar), -jnp.sin(ac), jnp.sin(ac)], axis=-1)
    cos = jnp.concatenate([cos, jnp.ones((l_len, HEAD_DIM), F32)], axis=0)
    sin = jnp.concatenate([sin, jnp.zeros((l_len, HEAD_DIM), F32)], axis=0)
    return cos, sin


def _swap_halves(v):
    lane = lax.broadcasted_iota(jnp.int32, v.shape, 1)
    return jnp.where((lane % 64) < 32, pltpu.roll(v, 96, 1), pltpu.roll(v, 32, 1))


def _qk_prep_fwd(qkv, q_gain, k_gain, cos, sin):
    R = qkv.shape[0]
    NQ, NK = N_HEADS * HEAD_DIM, N_KV * HEAD_DIM
    T = ROW_TILE
    vec = pl.BlockSpec((1, HEAD_DIM), lambda i: (0, 0))
    tab = pl.BlockSpec((T, HEAD_DIM), lambda i: (i, 0))

    def body(x_ref, qg_ref, kg_ref, c_ref, s_ref, q_ref, k_ref, v_ref):
        cosv, sinv = c_ref[...], s_ref[...]

        def prep(c0, gain):
            xh = x_ref[:, pl.ds(c0, HEAD_DIM)]
            xn = xh * lax.rsqrt(jnp.mean(xh * xh, axis=-1, keepdims=True) + EPS) * gain
            return _bf(xn * cosv + _swap_halves(xn) * sinv)

        for hd in range(N_HEADS):
            q_ref[:, pl.ds(hd * HEAD_DIM, HEAD_DIM)] = prep(hd * HEAD_DIM, qg_ref[...])
        for hd in range(N_KV):
            k_ref[:, pl.ds(hd * HEAD_DIM, HEAD_DIM)] = prep(NQ + hd * HEAD_DIM, kg_ref[...])
            v_ref[:, pl.ds(2 * hd * HEAD_DIM, HEAD_DIM)] = _bf(x_ref[:, pl.ds(NQ + NK + hd * HEAD_DIM, HEAD_DIM)])
            v_ref[:, pl.ds((2 * hd + 1) * HEAD_DIM, HEAD_DIM)] = jnp.ones((T, HEAD_DIM), BF16)

    return pl.pallas_call(
        body, name="qk_prep_fwd", grid=(R // T,),
        in_specs=[pl.BlockSpec((T, NQ + 2 * NK), lambda i: (i, 0)), vec, vec, tab, tab],
        out_specs=[pl.BlockSpec((T, NQ), lambda i: (i, 0)), pl.BlockSpec((T, NK), lambda i: (i, 0)),
                   pl.BlockSpec((T, 2 * NK), lambda i: (i, 0))],
        out_shape=[SDS((R, NQ), BF16), SDS((R, NK), BF16), SDS((R, 2 * NK), BF16)],
        compiler_params=_params(1),
    )(qkv, q_gain, k_gain, cos, sin)


def _qk_prep_bwd(qkv, dq, dk, dv, q_gain, k_gain, cos, sin):
    R = qkv.shape[0]
    NQ, NK = N_HEADS * HEAD_DIM, N_KV * HEAD_DIM
    T = ROW_TILE
    vec = pl.BlockSpec((1, HEAD_DIM), lambda i: (0, 0))
    tab = pl.BlockSpec((T, HEAD_DIM), lambda i: (i, 0))

    def body(x_ref, dq_ref, dk_ref, dv_ref, qg_ref, kg_ref, c_ref, s_ref, o_ref, dqg_ref, dkg_ref):
        i = pl.program_id(0)
        cosv, sinv = c_ref[...], s_ref[...]

        @pl.when(i == 0)
        def _():
            dqg_ref[...] = jnp.zeros_like(dqg_ref)
            dkg_ref[...] = jnp.zeros_like(dkg_ref)

        def back(c0, dout, gain, dg_ref):
            xh = x_ref[:, pl.ds(c0, HEAD_DIM)]
            r = lax.rsqrt(jnp.mean(xh * xh, axis=-1, keepdims=True) + EPS)
            xhat = xh * r
            dxn = dout * cosv + _swap_halves(dout * sinv)
            dg_ref[...] += jnp.sum(dxn * xhat, axis=0, keepdims=True)
            a = dxn * gain
            o_ref[:, pl.ds(c0, HEAD_DIM)] = _bf(r * (a - xhat * jnp.mean(a * xhat, axis=-1, keepdims=True)))

        for hd in range(N_HEADS):
            back(hd * HEAD_DIM, dq_ref[:, pl.ds(hd * HEAD_DIM, HEAD_DIM)], qg_ref[...], dqg_ref)
        for hd in range(N_KV):
            back(NQ + hd * HEAD_DIM, dk_ref[:, pl.ds(hd * HEAD_DIM, HEAD_DIM)], kg_ref[...], dkg_ref)
        o_ref[:, pl.ds(NQ + NK, NK)] = _bf(dv_ref[...])

    return pl.pallas_call(
        body, name="qk_prep_bwd", grid=(R // T,),
        in_specs=[pl.BlockSpec((T, NQ + 2 * NK), lambda i: (i, 0)), pl.BlockSpec((T, NQ), lambda i: (i, 0)),
                  pl.BlockSpec((T, NK), lambda i: (i, 0)), pl.BlockSpec((T, NK), lambda i: (i, 0)),
                  vec, vec, tab, tab],
        out_specs=[pl.BlockSpec((T, NQ + 2 * NK), lambda i: (i, 0)), vec, vec],
        out_shape=[SDS((R, NQ + 2 * NK), BF16), SDS((1, HEAD_DIM), F32), SDS((1, HEAD_DIM), F32)],
        compiler_params=_params(1),
    )(qkv, dq, dk, dv, q_gain, k_gain, cos, sin)


def _flash_fwd(q, k, v, s_len, l_len):
    R = q.shape[0]
    T = ROW_TILE
    n_lat = s_len // T
    ck = _pick(s_len, 512, 128)
    scale = HEAD_DIM ** -0.5
    group = N_HEADS // N_KV
    GW = group * HEAD_DIM

    M = group * T
    to_log2 = scale * math.log2(math.e)

    def body(q_ref, k_ref, v_ref, o_ref, lse_ref, m8_s, mb_s, acc_s):
        i = pl.program_id(1)
        qv = jnp.concatenate([q_ref[:, pl.ds(hh * HEAD_DIM, HEAD_DIM)] for hh in range(group)], axis=0)
        chunks = s_len // ck

        m8_s[...] = jnp.full_like(m8_s, -jnp.inf)

        def stat(rows, n):
            st = _dot_nt(k_ref[rows, :], qv)
            m8_s[...] = jnp.maximum(m8_s[...], jnp.max(st.reshape(n // 8, 8, M), axis=0))

        @pl.when(i < n_lat)
        def _():
            def loop(c, carry):
                stat(pl.ds(pl.multiple_of(c * ck, ck), ck), ck)
                return carry
            lax.fori_loop(0, chunks, loop, 0)

        stat(pl.ds(s_len, l_len), l_len)
        m_row = jnp.max(m8_s[...], axis=0, keepdims=True) * to_log2
        mb = jnp.broadcast_to(m_row, (HEAD_DIM, M)).T
        mb_s[...] = jnp.concatenate([mb] * (ck // HEAD_DIM), axis=1)

        acc_s[...] = jnp.zeros_like(acc_s)

        def step(rows, n):
            s2 = _dot_nt(qv, k_ref[rows, :]) * to_log2
            p = jnp.exp2(s2 - mb_s[:, pl.ds(0, n)])
            acc_s[...] += _dot(_bf(p), v_ref[rows, :])

        @pl.when(i < n_lat)
        def _():
            def loop(c, carry):
                step(pl.ds(pl.multiple_of(c * ck, ck), ck), ck)
                return carry
            lax.fori_loop(0, chunks, loop, 0)

        step(pl.ds(s_len, l_len), l_len)
        l_rep = acc_s[:, pl.ds(HEAD_DIM, HEAD_DIM)]
        o = acc_s[:, pl.ds(0, HEAD_DIM)] / l_rep
        for hh in range(group):
            o_ref[:, pl.ds(hh * HEAD_DIM, HEAD_DIM)] = o[hh * T:(hh + 1) * T]
        lse = (mb_s[:, pl.ds(0, HEAD_DIM)] + jnp.log2(l_rep)) * math.log(2.0)
        lse_ref[...] = jnp.max(lse, axis=-1, keepdims=True).reshape(group, T, 1)

    return pl.pallas_call(
        body, name="flash_fwd", grid=(N_KV, R // T),
        in_specs=[pl.BlockSpec((T, GW), lambda g, i: (i, g)),
                  pl.BlockSpec((R, HEAD_DIM), lambda g, i: (0, g)),
                  pl.BlockSpec((R, 2 * HEAD_DIM), lambda g, i: (0, g))],
        out_specs=[pl.BlockSpec((T, GW), lambda g, i: (i, g)),
                   pl.BlockSpec((group, T, 1), lambda g, i: (g, i, 0))],
        out_shape=[SDS((R, N_HEADS * HEAD_DIM), F32), SDS((N_HEADS, R, 1), F32)],
        scratch_shapes=[pltpu.VMEM((8, M), F32), pltpu.VMEM((M, ck), F32), pltpu.VMEM((M, 2 * HEAD_DIM), F32)],
        compiler_params=_params(2),
    )(q, k, v)


def _flash_bwd(q, k, v, o, lse, do, s_len, l_len):
    R = q.shape[0]
    T = ROW_TILE
    n_lat = s_len // T
    ck = _pick(s_len, 512, 128)
    scale = HEAD_DIM ** -0.5
    group = N_HEADS // N_KV
    GW = group * HEAD_DIM
    qspec = pl.BlockSpec((T, GW), lambda g, i: (i, g))
    kspec = pl.BlockSpec((R, HEAD_DIM), lambda g, i: (0, g))

    M = group * T
    log2e = math.log2(math.e)

    def body(q_ref, do_ref, o_ref, lse_ref, k_ref, v_ref, dq_ref, dk_ref, dv_ref, dq_s, lse_s, delta_s):
        i = pl.program_id(1)

        @pl.when(i == 0)
        def _():
            dk_ref[...] = jnp.zeros_like(dk_ref)
            dv_ref[...] = jnp.zeros_like(dv_ref)

        def stacked(ref):
            return jnp.concatenate([ref[:, pl.ds(hh * HEAD_DIM, HEAD_DIM)] for hh in range(group)], axis=0)

        qv = stacked(q_ref)
        dov = stacked(do_ref)
        dob = _bf(dov)
        delta_s[...] = jnp.broadcast_to(jnp.sum(dov * stacked(o_ref), axis=-1, keepdims=True), (M, ck))
        lse_s[...] = jnp.broadcast_to(lse_ref[...].reshape(M, 1) * log2e, (M, ck))
        dq_s[...] = jnp.zeros_like(dq_s)

        def step(rows, n):
            kv, vv = k_ref[rows, :], v_ref[rows, :]
            p = jnp.exp2(_dot_nt(qv, kv) * (scale * log2e) - lse_s[:, pl.ds(0, n)])
            dv_ref[rows, :] += _dot_tn(_bf(p), dob)
            ds = _bf(p * (_dot_nt(dob, vv) - delta_s[:, pl.ds(0, n)]) * scale)
            dq_s[...] += _dot(ds, kv)
            dk_ref[rows, :] += _dot_tn(ds, qv)

        @pl.when(i < n_lat)
        def _():
            def loop(c, carry):
                step(pl.ds(pl.multiple_of(c * ck, ck), ck), ck)
                return carry
            lax.fori_loop(0, s_len // ck, loop, 0)

        step(pl.ds(s_len, l_len), l_len)
        for hh in range(group):
            dq_ref[:, pl.ds(hh * HEAD_DIM, HEAD_DIM)] = dq_s[pl.ds(hh * T, T), :]

    return pl.pallas_call(
        body, name="flash_bwd", grid=(N_KV, R // T),
        in_specs=[qspec, qspec, qspec, pl.BlockSpec((group, T, 1), lambda g, i: (g, i, 0)), kspec,
                  pl.BlockSpec((R, HEAD_DIM), lambda g, i: (0, 2 * g))],
        out_specs=[qspec, kspec, kspec],
        out_shape=[SDS((R, N_HEADS * HEAD_DIM), F32), SDS((R, N_KV * HEAD_DIM), F32),
                   SDS((R, N_KV * HEAD_DIM), F32)],
        scratch_shapes=[pltpu.VMEM((M, HEAD_DIM), F32), pltpu.VMEM((M, ck), F32), pltpu.VMEM((M, ck), F32)],
        compiler_params=_params(2),
    )(q, do, o, lse, k, v)


K_SCALE = RET_DK ** -0.5


def _log_sigmoid(v):
    return -(jnp.maximum(-v, 0.0) + jnp.log(1.0 + jnp.exp(-jnp.abs(v))))


def _ret_decays(d, lg):
    C = RET_CHUNK
    ic = lax.broadcasted_iota(jnp.int32, (C, 1), 0)
    ir = lax.broadcasted_iota(jnp.int32, (1, C), 1)
    li = jnp.where(d == 0, ic, C - 1 - ic).astype(F32)
    lj = jnp.where(d == 0, ir, C - 1 - ir).astype(F32)
    diff = li - lj
    mask = jnp.where(diff >= 0, jnp.exp(jnp.maximum(diff, 0.0) * lg), 0.0)
    qd = jnp.exp((li + 1.0) * lg)
    kd = jnp.exp((C - 1.0 - li) * lg)
    cd = jnp.exp(C * lg)
    return li, diff, mask, qd, kd, cd


def _ctx_weights(d, t, lg, l_len):
    C = RET_CHUNK
    j = (t * C + lax.broadcasted_iota(jnp.int32, (C, 1), 0)).astype(F32)
    e = jnp.where(d == 0, (l_len - 1.0) - j, j)
    return e, jnp.exp(e * lg)


def _ret_specs(n_lat_c, n_ctx_c, ctx_first):
    def blk(d, t):
        if ctx_first:
            n = jnp.maximum(t - n_ctx_c, 0)
            lat = jnp.where(d == 0, n, n_lat_c - 1 - n)
            return jnp.where(t < n_ctx_c, n_lat_c + t, lat)
        n = jnp.minimum(t, n_lat_c - 1)
        lat = jnp.where(d == 0, n_lat_c - 1 - n, n)
        return jnp.where(t >= n_lat_c, t, lat)
    return blk


def _ret_fwd(proj, lgt, s_len, l_len):
    R = proj.shape[0]
    C, H, DK, DV = RET_CHUNK, RET_HEADS, RET_DK, RET_DV
    nl, nc = s_len // C, l_len // C
    blk = _ret_specs(nl, nc, True)

    def body(q_ref, k_ref, v_ref, lg_ref, o_ref, st_ref, r_s):
        d, t = pl.program_id(0), pl.program_id(2)
        lg = jnp.max(_log_sigmoid(lg_ref[0, 0]), axis=-1, keepdims=True)

        @pl.when(t == 0)
        def _():
            r_s[...] = jnp.zeros_like(r_s)

        @pl.when(t < nc)
        def _():
            _, w = _ctx_weights(d, t, lg, l_len)
            r_s[...] += _dot_tn(_bf(k_ref[...] * K_SCALE * w), _bf(v_ref[...]))
            o_ref[0] = jnp.zeros((C, DV), F32)

        @pl.when(t >= nc)
        def _():
            _, _, mask, qd, kd, cd = _ret_decays(d, lg)
            qb, kv, vb = _bf(q_ref[...]), k_ref[...] * K_SCALE, _bf(v_ref[...])
            r = r_s[...]
            st_ref[0, 0, 0] = r
            att = _dot_nt(qb, _bf(kv)) * mask
            o_ref[0] = _dot(_bf(att), vb) + _dot(qb, _bf(r)) * qd
            r_s[...] = r * cd + _dot_tn(_bf(kv * kd), vb)

    return pl.pallas_call(
        body, name="ret_fwd", grid=(2, H, nc + nl),
        in_specs=[pl.BlockSpec((C, DK), lambda d, h, t: (blk(d, t), h)),
                  pl.BlockSpec((C, DK), lambda d, h, t: (blk(d, t), H + h)),
                  pl.BlockSpec((C, DV), lambda d, h, t: (blk(d, t), H + h)),
                  pl.BlockSpec((1, 1, 1, 128), lambda d, h, t: (d, h, 0, 0))],
        out_specs=[pl.BlockSpec((1, C, DV), lambda d, h, t: (d, blk(d, t), h)),
                   pl.BlockSpec((1, 1, 1, DK, DV), lambda d, h, t: (d, h, jnp.maximum(t - nc, 0), 0, 0))],
        out_shape=[SDS((2, R, H * DV), F32), SDS((2, H, nl, DK, DV), F32)],
        scratch_shapes=[pltpu.VMEM((DK, DV), F32)],
        compiler_params=_params(3),
    )(proj, proj, proj, lgt)


def _ret_bwd(proj, lgt, states, do, s_len, l_len):
    R = proj.shape[0]
    C, H, DK, DV = RET_CHUNK, RET_HEADS, RET_DK, RET_DV
    nl, nc = s_len // C, l_len // C
    blk = _ret_specs(nl, nc, False)
    last = nl + nc - 1

    def body(q_ref, k_ref, v_ref, lg_ref, st_ref, do_ref, dq_ref, dk_ref, dv_ref, dlg_ref, dr_s, dl_s):
        d, t = pl.program_id(0), pl.program_id(2)
        x = lg_ref[0, 0]
        lg = jnp.max(_log_sigmoid(x), axis=-1, keepdims=True)

        @pl.when(t == 0)
        def _():
            dr_s[...] = jnp.zeros_like(dr_s)
            dl_s[...] = jnp.zeros_like(dl_s)

        @pl.when(t < nl)
        def _():
            li, diff, mask, qd, kd, cd = _ret_decays(d, lg)
            qv, kv, vv, dov = q_ref[...], k_ref[...] * K_SCALE, v_ref[...], do_ref[...]
            qb, kb, vb, dob = _bf(qv), _bf(kv), _bf(vv), _bf(dov)
            r, drn = st_ref[0, 0, 0], dr_s[...]
            rb, drb = _bf(r), _bf(drn)
            p = _dot_nt(qb, kb)
            dp = _dot_nt(dob, vb) * mask
            dpb = _bf(dp)
            doq = _bf(dov * qd)
            dq_inter = _dot_nt(doq, rb)
            dk_state = kd * _dot_nt(vb, drb)
            dq_ref[0] = _dot(dpb, kb) + dq_inter
            dk_ref[0] = (_dot_tn(dpb, qb) + dk_state) * K_SCALE
            dv_ref[0] = _dot_tn(_bf(p * mask), dob) + _dot(_bf(kv * kd), drb)
            dr_s[...] = cd * drn + _dot_tn(qb, doq)
            dl_s[...] += (jnp.sum(dp * p * diff) + jnp.sum((li + 1.0) * qv * dq_inter)
                          + jnp.sum((C - 1.0 - li) * kv * dk_state) + C * jnp.sum(cd * r * drn))

        @pl.when(t >= nl)
        def _():
            e, w = _ctx_weights(d, t - nl, lg, l_len)
            kv, vb, drb = k_ref[...] * K_SCALE, _bf(v_ref[...]), _bf(dr_s[...])
            dkc = w * _dot_nt(vb, drb)
            dq_ref[0] = jnp.zeros((C, DK), F32)
            dk_ref[0] = dkc * K_SCALE
            dv_ref[0] = _dot(_bf(kv * w), drb)
            dl_s[...] += jnp.sum(e * kv * dkc)

        @pl.when(t == last)
        def _():
            dlg_ref[0, 0] = dl_s[...] * (1.0 / (1.0 + jnp.exp(x)))

    return pl.pallas_call(
        body, name="ret_bwd", grid=(2, H, nl + nc),
        in_specs=[pl.BlockSpec((C, DK), lambda d, h, t: (blk(d, t), h)),
                  pl.BlockSpec((C, DK), lambda d, h, t: (blk(d, t), H + h)),
                  pl.BlockSpec((C, DV), lambda d, h, t: (blk(d, t), H + h)),
                  pl.BlockSpec((1, 1, 1, 128), lambda d, h, t: (d, h, 0, 0)),
                  pl.BlockSpec((1, 1, 1, DK, DV), lambda d, h, t: (d, h, jnp.maximum(nl - 1 - t, 0), 0, 0)),
                  pl.BlockSpec((C, DV), lambda d, h, t: (blk(d, t), h))],
        out_specs=[pl.BlockSpec((1, C, DK), lambda d, h, t: (d, blk(d, t), h)),
                   pl.BlockSpec((1, C, DK), lambda d, h, t: (d, blk(d, t), h)),
                   pl.BlockSpec((1, C, DV), lambda d, h, t: (d, blk(d, t), h)),
                   pl.BlockSpec((1, 1, 1, 128), lambda d, h, t: (d, h, 0, 0))],
        out_shape=[SDS((2, R, H * DK), F32), SDS((2, R, H * DK), F32), SDS((2, R, H * DV), F32),
                   SDS((2, H, 1, 128), F32)],
        scratch_shapes=[pltpu.VMEM((DK, DV), F32), pltpu.VMEM((1, 128), F32)],
        compiler_params=_params(3),
    )(proj, proj, proj, lgt, states, do)


def _readout_fwd(o2, proj, gn_w):
    R = proj.shape[0]
    H, DV = RET_HEADS, RET_DV
    W = H * DV
    T = ROW_TILE

    def body(o_ref, g_ref, w_ref, out_ref):
        for hh in range(H):
            cols = pl.ds(hh * DV, DV)
            y = o_ref[0, :, cols] + o_ref[1, :, cols]
            yc = y - jnp.mean(y, axis=-1, keepdims=True)
            yn = yc * lax.rsqrt(jnp.mean(yc * yc, axis=-1, keepdims=True) + EPS) * w_ref[:, cols]
            g = g_ref[:, cols]
            out_ref[:, cols] = _bf(g * _sigmoid(g) * yn)

    return pl.pallas_call(
        body, name="readout_fwd", grid=(R // T,),
        in_specs=[pl.BlockSpec((2, T, W), lambda i: (0, i, 0)), pl.BlockSpec((T, W), lambda i: (i, 2)),
                  pl.BlockSpec((1, W), lambda i: (0, 0))],
        out_specs=pl.BlockSpec((T, W), lambda i: (i, 0)),
        out_shape=SDS((R, W), BF16), compiler_params=_params(1),
    )(o2, proj, gn_w)


def _readout_bwd(o2, proj, gn_w, dgated):
    R = proj.shape[0]
    H, DV = RET_HEADS, RET_DV
    W = H * DV
    T = ROW_TILE

    def body(o_ref, g_ref, w_ref, d_ref, do_ref, dg_ref, dw_ref):
        i = pl.program_id(0)

        @pl.when(i == 0)
        def _():
            dw_ref[...] = jnp.zeros_like(dw_ref)

        for hh in range(H):
            cols = pl.ds(hh * DV, DV)
            y = o_ref[0, :, cols] + o_ref[1, :, cols]
            yc = y - jnp.mean(y, axis=-1, keepdims=True)
            rstd = lax.rsqrt(jnp.mean(yc * yc, axis=-1, keepdims=True) + EPS)
            yn0 = yc * rstd
            wv = w_ref[:, cols]
            g = g_ref[:, cols]
            s = _sigmoid(g)
            dgt = d_ref[:, cols]
            dyn = dgt * (g * s)
            dg_ref[:, cols] = _bf(dgt * (yn0 * wv) * (s * (1.0 + g * (1.0 - s))))
            dw_ref[:, cols] += jnp.sum(dyn * yn0, axis=0, keepdims=True)
            a = dyn * wv
            do_ref[:, cols] = rstd * (a - jnp.mean(a, axis=-1, keepdims=True)
                                      - yn0 * jnp.mean(a * yn0, axis=-1, keepdims=True))

    return pl.pallas_call(
        body, name="readout_bwd", grid=(R // T,),
        in_specs=[pl.BlockSpec((2, T, W), lambda i: (0, i, 0)), pl.BlockSpec((T, W), lambda i: (i, 2)),
                  pl.BlockSpec((1, W), lambda i: (0, 0)), pl.BlockSpec((T, W), lambda i: (i, 0))],
        out_specs=[pl.BlockSpec((T, W), lambda i: (i, 0)), pl.BlockSpec((T, W), lambda i: (i, 0)),
                   pl.BlockSpec((1, W), lambda i: (0, 0))],
        out_shape=[SDS((R, W), F32), SDS((R, W), BF16), SDS((1, W), F32)],
        compiler_params=_params(1),
    )(o2, proj, gn_w, dgated)


def _ret_dproj(dq2, dk2, dv2, dg):
    R = dg.shape[0]
    NQ, NV = RET_HEADS * RET_DK, RET_HEADS * RET_DV
    T = ROW_TILE

    def body(dq_ref, dk_ref, dv_ref, dg_ref, o_ref):
        o_ref[:, pl.ds(0, NQ)] = _bf(dq_ref[0] + dq_ref[1])
        o_ref[:, pl.ds(NQ, NQ)] = _bf(dk_ref[0] + dk_ref[1])
        o_ref[:, pl.ds(2 * NQ, NV)] = _bf(dv_ref[0] + dv_ref[1])
        o_ref[:, pl.ds(2 * NQ + NV, NV)] = dg_ref[...]

    return pl.pallas_call(
        body, name="ret_dproj", grid=(R // T,),
        in_specs=[pl.BlockSpec((2, T, NQ), lambda i: (0, i, 0)), pl.BlockSpec((2, T, NQ), lambda i: (0, i, 0)),
                  pl.BlockSpec((2, T, NV), lambda i: (0, i, 0)), pl.BlockSpec((T, NV), lambda i: (i, 0))],
        out_specs=pl.BlockSpec((T, 2 * NQ + 2 * NV), lambda i: (i, 0)),
        out_shape=SDS((R, 2 * NQ + 2 * NV), BF16), compiler_params=_params(1),
    )(dq2, dk2, dv2, dg)


def _silu(v):
    return v * _sigmoid(v)


def _ada_fwd(c_rows, ada_w, ada_b_shard):
    depth, D, cols = ada_w.shape

    def body(c_ref, w_ref, b_ref, o_ref):
        o_ref[0] = _dot(_bf(_silu(c_ref[...])), _bf(w_ref[0])) + b_ref[0]

    return pl.pallas_call(
        body, name="ada_fwd", grid=(depth,),
        in_specs=[pl.BlockSpec((16, D), lambda i: (0, 0)), pl.BlockSpec((1, D, cols), lambda i: (i, 0, 0)),
                  pl.BlockSpec((1, 1, cols), lambda i: (i, 0, 0))],
        out_specs=pl.BlockSpec((1, 16, cols), lambda i: (i, 0, 0)),
        out_shape=SDS((depth, 16, cols), F32), compiler_params=_params(1),
    )(c_rows, ada_w, ada_b_shard)


def _ada_bwd(c_rows, ada_w, d_lat, d_ctx):
    depth, D, cols = ada_w.shape

    def body(c_ref, w_ref, dl_ref, dc_ref, dw_ref, pc_ref):
        i = pl.program_id(0)
        cv = c_ref[...]
        a = _silu(cv)
        dcs = jnp.broadcast_to(jnp.sum(dc_ref[0], axis=0, keepdims=True), (8, cols))
        dw_ref[0] = _dot_tn(_bf(a[0:8]), _bf(dl_ref[0])) + _dot_tn(_bf(a[8:16]), _bf(dcs))

        @pl.when(i == 0)
        def _():
            pc_ref[...] = jnp.zeros_like(pc_ref)

        pc_ref[...] += _dot_nt(_bf(dcs), _bf(w_ref[0]))

        @pl.when(i == depth - 1)
        def _():
            cc = c_ref[pl.ds(8, 1), :]
            s = _sigmoid(cc)
            pc_ref[...] = pc_ref[...] * (s * (1.0 + cc * (1.0 - s)))

    return pl.pallas_call(
        body, name="ada_bwd", grid=(depth,),
        in_specs=[pl.BlockSpec((16, D), lambda i: (0, 0)), pl.BlockSpec((1, D, cols), lambda i: (i, 0, 0)),
                  pl.BlockSpec((1, 8, cols), lambda i: (i, 0, 0)), pl.BlockSpec((1, 8, cols), lambda i: (i, 0, 0))],
        out_specs=[pl.BlockSpec((1, D, cols), lambda i: (i, 0, 0)), pl.BlockSpec((8, D), lambda i: (0, 0))],
        out_shape=[SDS((depth, D, cols), F32), SDS((8, D), F32)], compiler_params=_params(1),
    )(c_rows, ada_w, d_lat, d_ctx)


def _adamw(w, g, m, v, name):
    shape = w.shape
    n = g.shape[0]
    cols = shape[-1]
    rows = w.size // cols
    tr = _pick(rows, 512, 8) if rows * cols * 4 > (1 << 20) else rows
    spec = pl.BlockSpec((tr, cols), lambda i: (i, 0))

    def body(w_ref, g_ref, m_ref, v_ref, go_ref, d_ref, mo_ref, vo_ref):
        gs = g_ref[0].astype(F32)
        for k in range(1, n):
            gs = gs + g_ref[k].astype(F32)
        mn = ADAM_B1 * m_ref[...] + (1.0 - ADAM_B1) * gs
        vn = ADAM_B2 * v_ref[...] + (1.0 - ADAM_B2) * jnp.square(gs)
        m_hat = mn / (1.0 - ADAM_B1 ** ADAM_STEP)
        v_hat = vn / (1.0 - ADAM_B2 ** ADAM_STEP)
        go_ref[...] = gs
        d_ref[...] = -ADAM_LR * (m_hat / (jnp.sqrt(v_hat) + ADAM_EPS) + ADAM_WD * w_ref[...])
        mo_ref[...] = mn
        vo_ref[...] = vn

    outs = pl.pallas_call(
        body, name=name, grid=(rows // tr,),
        in_specs=[spec, pl.BlockSpec((n, tr, cols), lambda i: (0, i, 0)), spec, spec],
        out_specs=[spec] * 4, out_shape=[SDS((rows, cols), F32)] * 4, compiler_params=_params(1),
    )(w.reshape(rows, cols), g.reshape(n, rows, cols), m.reshape(rows, cols), v.reshape(rows, cols))
    return tuple(o.reshape(shape) for o in outs)


def _sum_slots(recv):
    n, rows, cols = recv.shape
    tr = _pick(rows, 256, 16)

    def body(r_ref, o_ref):
        acc = r_ref[0].astype(F32)
        for k in range(1, n):
            acc = acc + r_ref[k].astype(F32)
        o_ref[...] = acc

    return pl.pallas_call(
        body, name="sum_slots", grid=(rows // tr,),
        in_specs=[pl.BlockSpec((n, tr, cols), lambda i: (0, i, 0))],
        out_specs=pl.BlockSpec((tr, cols), lambda i: (i, 0)),
        out_shape=SDS((rows, cols), F32), compiler_params=_params(1),
    )(recv)


def _position():
    return lax.axis_index("x"), lax.axis_index("y"), lax.axis_index("c")


def _peer(k, x, y, c):
    return (1 - x if k & 4 else x, 1 - y if k & 2 else y, 1 - c if k & 1 else c)


def _index(pos):
    return 4 * pos[0] + 2 * pos[1] + pos[2]


def _gather_small(v, name):
    rows, lanes = v.shape

    def body(x_ref, out_ref, send_sems, recv_sems, local_sem):
        me = _position()
        mine = pltpu.make_async_copy(x_ref, out_ref.at[_index(me)], local_sem)
        mine.start()

        def copy(k, slot):
            return pltpu.make_async_remote_copy(
                src_ref=x_ref, dst_ref=out_ref.at[slot], send_sem=send_sems.at[k - 1],
                recv_sem=recv_sems.at[k - 1], device_id=_peer(k, *me), device_id_type=MESH)

        sends = [copy(k, _index(me)) for k in range(1, N_DEV)]
        for cp in sends:
            cp.start()
        for k in range(1, N_DEV):
            copy(k, _index(_peer(k, *me))).wait_recv()
        for cp in sends:
            cp.wait_send()
        mine.wait()

    return pl.pallas_call(
        body, name=name, out_shape=SDS((N_DEV, rows, lanes), v.dtype),
        in_specs=[pl.BlockSpec(memory_space=pltpu.VMEM)],
        out_specs=pl.BlockSpec(memory_space=pltpu.VMEM),
        scratch_shapes=[pltpu.SemaphoreType.DMA((N_DEV - 1,)), pltpu.SemaphoreType.DMA((N_DEV - 1,)),
                        pltpu.SemaphoreType.DMA],
        compiler_params=pltpu.CompilerParams(vmem_limit_bytes=VMEM_LIMIT_V7X),
    )(v)


def _gather_big(v, name):
    rows, cols = v.shape

    def body(x_ref, out_ref, send_sems, recv_sems, local_sem):
        x, y, c = _position()
        me, sibling = (x, y, c), (x, y, 1 - c)
        chips = [(1 - x, y), (x, 1 - y), (1 - x, 1 - y)]

        def copy(k, block, to, src=None):
            slot = out_ref.at[_index(block)]
            return pltpu.make_async_remote_copy(
                src_ref=slot if src is None else src, dst_ref=slot, send_sem=send_sems.at[k],
                recv_sem=recv_sems.at[k], device_id=to, device_id_type=MESH)

        mine = pltpu.make_async_copy(x_ref, out_ref.at[_index(me)], local_sem)
        mine.start()
        first = [copy(0, me, sibling, src=x_ref)]
        first += [copy(1 + j, me, (*chip, c), src=x_ref) for j, chip in enumerate(chips)]
        for cp in first:
            cp.start()
        passed = [copy(4 + j, (*chip, c), sibling) for j, chip in enumerate(chips)]
        for j, chip in enumerate(chips):
            copy(1 + j, (*chip, c), me).wait_recv()
            passed[j].start()
        copy(0, sibling, me).wait_recv()
        for j, chip in enumerate(chips):
            copy(4 + j, (*chip, 1 - c), me).wait_recv()
        for cp in first + passed:
            cp.wait_send()
        mine.wait()

    return pl.pallas_call(
        body, name=name, out_shape=SDS((N_DEV, rows, cols), v.dtype),
        in_specs=[pl.BlockSpec(memory_space=pl.ANY)],
        out_specs=pl.BlockSpec(memory_space=pl.ANY),
        scratch_shapes=[pltpu.SemaphoreType.DMA((N_DEV - 1,)), pltpu.SemaphoreType.DMA((N_DEV - 1,)),
                        pltpu.SemaphoreType.DMA],
    )(v)


def _exchange(p, name):
    n, rows, cols = p.shape

    def body(p_ref, out_ref, send_sems, recv_sems, local_sem):
        me = _position()
        mine = pltpu.make_async_copy(p_ref.at[_index(me)], out_ref.at[_index(me)], local_sem)
        mine.start()

        def copy(k, src_slot, dst_slot):
            return pltpu.make_async_remote_copy(
                src_ref=p_ref.at[src_slot], dst_ref=out_ref.at[dst_slot], send_sem=send_sems.at[k - 1],
                recv_sem=recv_sems.at[k - 1], device_id=_peer(k, *me), device_id_type=MESH)

        sends = [copy(k, _index(_peer(k, *me)), _index(me)) for k in range(1, N_DEV)]
        for cp in sends:
            cp.start()
        for k in range(1, N_DEV):
            copy(k, _index(me), _index(_peer(k, *me))).wait_recv()
        for cp in sends:
            cp.wait_send()
        mine.wait()

    return pl.pallas_call(
        body, name=name, out_shape=SDS((n, rows, cols), p.dtype),
        in_specs=[pl.BlockSpec(memory_space=pl.ANY)],
        out_specs=pl.BlockSpec(memory_space=pl.ANY),
        scratch_shapes=[pltpu.SemaphoreType.DMA((N_DEV - 1,)), pltpu.SemaphoreType.DMA((N_DEV - 1,)),
                        pltpu.SemaphoreType.DMA],
    )(p)


def _pack_rows(arrays, lanes, dtype):
    flat = jnp.concatenate([a.astype(dtype).reshape(-1) for a in arrays])
    pad = (-flat.size) % (16 * lanes)
    if pad:
        flat = jnp.concatenate([flat, jnp.zeros((pad,), dtype)])
    return flat.reshape(-1, lanes)


def _unpack_rows(packed, shapes):
    n = packed.shape[0]
    flat = packed.reshape(n, -1)
    out, off = [], 0
    for shp in shapes:
        size = math.prod(shp)
        out.append(flat[:, off:off + size].reshape((n,) + tuple(shp)))
        off += size
    return out


def _unshard(g8, axis):
    moved = jnp.moveaxis(g8, 0, axis)
    shp = list(moved.shape)
    shp[axis:axis + 2] = [shp[axis] * shp[axis + 1]]
    return moved.reshape(shp)


def _split8(full, axis):
    shp = list(full.shape)
    shp[axis:axis + 1] = [N_DEV, shp[axis] // N_DEV]
    return jnp.moveaxis(full.reshape(shp), axis, 0)


def _my_shard(g, axis, me):
    size = g.shape[axis + 1] // N_DEV
    return lax.dynamic_slice_in_dim(g, me * size, size, axis=axis + 1)


BIG_WEIGHTS = (("ffn_w_up", 2), ("ffn_w_down", 1), ("attn_w_qkv", 2), ("attn_w_o", 1),
               ("ret_w_in", 2), ("ret_w_out", 1), ("pool_w", 2))
SMALL_SHARDED = (("norm_w", 2), ("pool_b", 1), ("pool_scale", 1), ("ret_gn_w", 1), ("ffn_conv_w", 2))
REPLICATED = ("ada_b", "attn_q_gain", "attn_k_gain", "ret_decay_logit", "ffn_conv_b")
WEIGHT_ORDER = ("c_ctx", "ada_w", "ada_b", "norm_w", "pool_w", "pool_b", "pool_scale", "attn_w_qkv",
                "attn_q_gain", "attn_k_gain", "attn_w_o", "ret_w_in", "ret_decay_logit", "ret_gn_w",
                "ret_w_out", "ffn_w_up", "ffn_conv_w", "ffn_conv_b", "ffn_w_down")


def _local_step(x0, target, mods, P, s_len, l_len):
    n_lat = s_len // ROW_TILE
    nw = P["norm_w"]
    lgt = jnp.broadcast_to(P["ret_decay_logit"][0][:, :, None, None], (2, RET_HEADS, 1, 128))
    cos, sin = _rope_tables(s_len, l_len)
    h_dtype = [F32 if i % 3 == 0 else BF16 for i in range(DEPTH)]
    saved = []
    X = x0
    h = _res_norm(X, None, None, 0, nw[0, 0], mods[0], 0, h_dtype[0], n_lat, "norm_first")
    for i in range(DEPTH):
        kind, j, mod = i % 3, i // 3, mods[i]
        sv = {"X": X, "h": h}
        if kind == 0:
            y = _pool_fwd(h, P["pool_w"][j], P["pool_b"][j:j + 1], P["pool_scale"][j:j + 1],
                          n_lat, s_len, l_len, f"pool_fwd{i}")
        elif kind == 1:
            qkv = _mm(h, P["attn_w_qkv"][j], "nn", F32, f"qkv{i}")
            q, k, v = _qk_prep_fwd(qkv, P["attn_q_gain"][j:j + 1], P["attn_k_gain"][j:j + 1], cos, sin)
            o, lse = _flash_fwd(q, k, v, s_len, l_len)
            y = _mm(o, P["attn_w_o"][j], "nn", F32, f"attn_out{i}")
            sv.update(qkv=qkv, q=q, k=k, v=v, o=o, lse=lse)
        else:
            proj = _mm(h, P["ret_w_in"][j], "nn", F32, f"ret_in{i}")
            o2, states = _ret_fwd(proj, lgt, s_len, l_len)
            gated = _readout_fwd(o2, proj, P["ret_gn_w"][j:j + 1])
            y = _mm(gated, P["ret_w_out"][j], "nn", F32, f"ret_out{i}")
            sv.update(proj=proj, o2=o2, states=states, gated=gated)
        X1, h2 = _res_norm(X, y, mod, 0, nw[i, 1], mod, 1, BF16, n_lat, f"res_norm_mid{i}")
        u = _mm(h2, P["ffn_w_up"][i], "nn", F32, f"ffn_up{i}")
        gact = _conv_gate_fwd(u, P["ffn_conv_w"][i], P["ffn_conv_b"][i:i + 1], n_lat, f"conv_gate_fwd{i}")
        f = _mm(gact, P["ffn_w_down"][i], "nn", F32, f"ffn_down{i}")
        sv.update(y=y, X1=X1, h2=h2, u=u, gact=gact, f=f)
        saved.append(sv)
        if i + 1 < DEPTH:
            X, h = _res_norm(X1, f, mod, 1, nw[i + 1, 0], mods[i + 1], 0, h_dtype[i + 1], n_lat,
                             f"res_norm_end{i}")
        else:
            X = _res_norm(X1, f, mod, 1, None, None, 0, None, n_lat, "res_last")

    dX, loss = _loss_bwd(X, target, n_lat)
    G = {name: [None] * P[name].shape[0] for name in
         ("pool_w", "pool_b", "pool_scale", "attn_w_qkv", "attn_q_gain", "attn_k_gain", "attn_w_o", "ret_w_in",
          "ret_decay_logit", "ret_gn_w", "ret_w_out", "ffn_w_up", "ffn_conv_w", "ffn_conv_b", "ffn_w_down")}
    dnw = [[None, None] for _ in range(DEPTH)]
    dmods = [None] * DEPTH
    for i in reversed(range(DEPTH)):
        kind, j, mod, sv = i % 3, i // 3, mods[i], saved[i]
        df, dg2 = _gate_bwd(dX, sv["f"], mod, 1, BF16, n_lat, f"gate_bwd_ffn{i}")
        dgact = _mm(df, P["ffn_w_down"][i], "nt", F32, f"ffn_down_dx{i}")
        G["ffn_w_down"][i] = _mm(sv["gact"], df, "tn", F32, f"ffn_down_dw{i}")
        du, dcw, dcb = _conv_gate_bwd(sv["u"], dgact, P["ffn_conv_w"][i], P["ffn_conv_b"][i:i + 1], n_lat,
                                      f"conv_gate_bwd{i}")
        G["ffn_conv_w"][i], G["ffn_conv_b"][i] = dcw, dcb[0]
        dh2 = _mm(du, P["ffn_w_up"][i], "nt", F32, f"ffn_up_dx{i}")
        G["ffn_w_up"][i] = _mm(sv["h2"], du, "tn", F32, f"ffn_up_dw{i}")
        dX1, dnw[i][1], dsh2, dsc2 = _norm_bwd(dX, dh2, sv["X1"], nw[i, 1], mod, 1, n_lat, f"norm_bwd_ffn{i}")
        dy, dg1 = _gate_bwd(dX1, sv["y"], mod, 0, F32 if kind == 0 else BF16, n_lat, f"gate_bwd_mix{i}")
        h = sv["h"]
        if kind == 0:
            dh, dpw, dpb, dps = _pool_bwd(h, dy, P["pool_w"][j], P["pool_b"][j:j + 1], P["pool_scale"][j:j + 1],
                                          n_lat, s_len, l_len, f"pool_bwd{i}")
            G["pool_w"][j], G["pool_b"][j], G["pool_scale"][j] = dpw, dpb[0], dps[0]
        elif kind == 1:
            do = _mm(dy, P["attn_w_o"][j], "nt", F32, f"attn_out_dx{i}")
            G["attn_w_o"][j] = _mm(sv["o"], dy, "tn", F32, f"attn_out_dw{i}")
            dq, dk, dv = _flash_bwd(sv["q"], sv["k"], sv["v"], sv["o"], sv["lse"], do, s_len, l_len)
            dqkv, dqg, dkg = _qk_prep_bwd(sv["qkv"], dq, dk, dv, P["attn_q_gain"][j:j + 1],
                                          P["attn_k_gain"][j:j + 1], cos, sin)
            G["attn_q_gain"][j], G["attn_k_gain"][j] = dqg[0], dkg[0]
            dh = _mm(dqkv, P["attn_w_qkv"][j], "nt", F32, f"qkv_dx{i}")
            G["attn_w_qkv"][j] = _mm(h, dqkv, "tn", F32, f"qkv_dw{i}")
        else:
            dgated = _mm(dy, P["ret_w_out"][j], "nt", F32, f"ret_out_dx{i}")
            G["ret_w_out"][j] = _mm(sv["gated"], dy, "tn", F32, f"ret_out_dw{i}")
            do, dg, dgn = _readout_bwd(sv["o2"], sv["proj"], P["ret_gn_w"][j:j + 1], dgated)
            dq2, dk2, dv2, dlg = _ret_bwd(sv["proj"], lgt, sv["states"], do, s_len, l_len)
            dproj = _ret_dproj(dq2, dk2, dv2, dg)
            G["ret_gn_w"][j], G["ret_decay_logit"][j] = dgn[0], dlg[:, :, 0, 0]
            dh = _mm(dproj, P["ret_w_in"][j], "nt", F32, f"ret_in_dx{i}")
            G["ret_w_in"][j] = _mm(h, dproj, "tn", F32, f"ret_in_dw{i}")
        dX, dnw[i][0], dsh1, dsc1 = _norm_bwd(dX1, dh, sv["X"], nw[i, 0], mod, 0, n_lat, f"norm_bwd_mix{i}")
        dmods[i] = jnp.concatenate([dsh1, dsc1, dg1, dsh2, dsc2, dg2], axis=1)
    grads = {name: jnp.stack(parts) for name, parts in G.items()}
    grads["norm_w"] = jnp.stack([jnp.concatenate(pair, axis=0) for pair in dnw])
    return loss, dX, grads, jnp.stack(dmods)


def kernel(x, c, ctx, c_ctx, ada_w, ada_b, norm_w, pool_w, pool_b, pool_scale, attn_w_qkv, attn_q_gain,
           attn_k_gain, attn_w_o, ret_w_in, ret_decay_logit, ret_gn_w, ret_w_out, ffn_w_up, ffn_conv_w,
           ffn_conv_b, ffn_w_down, loss_target, m_c_ctx, m_ada_w, m_ada_b, m_norm_w, m_pool_w, m_pool_b,
           m_pool_scale, m_attn_w_qkv, m_attn_q_gain, m_attn_k_gain, m_attn_w_o, m_ret_w_in,
           m_ret_decay_logit, m_ret_gn_w, m_ret_w_out, m_ffn_w_up, m_ffn_conv_w, m_ffn_conv_b, m_ffn_w_down,
           v_c_ctx, v_ada_w, v_ada_b, v_norm_w, v_pool_w, v_pool_b, v_pool_scale, v_attn_w_qkv, v_attn_q_gain,
           v_attn_k_gain, v_attn_w_o, v_ret_w_in, v_ret_decay_logit, v_ret_gn_w, v_ret_w_out, v_ffn_w_up,
           v_ffn_conv_w, v_ffn_conv_b, v_ffn_w_down):
    A = dict(locals())
    me = _index(_position())
    s_len, D = x.shape[1], x.shape[2]
    l_len = ctx.shape[1]
    assert s_len % ROW_TILE == 0 and l_len % ROW_TILE == 0 and s_len % GRID_W == 0

    small = [A[n] for n, _ in SMALL_SHARDED]
    got = _gather_small(_pack_rows([c] + small, 128, F32), "gather_c_small")
    parts = _unpack_rows(got, [c.shape] + [a.shape for a in small])
    c_all = parts[0].reshape(N_DEV, D)
    P = {n: _unshard(g8, ax) for (n, ax), g8 in zip(SMALL_SHARDED, parts[1:])}

    c_rows = jnp.concatenate([c_all, c_ctx.reshape(1, D), jnp.zeros((7, D), F32)], axis=0)
    cols = ada_w.shape[2]
    ada_b_shard = lax.dynamic_slice_in_dim(ada_b, me * cols, cols, axis=1).reshape(DEPTH, 1, cols)
    mod_shard = _ada_fwd(c_rows, ada_w, ada_b_shard)
    got = _gather_small(mod_shard.reshape(-1, 128), "gather_mod").reshape(N_DEV, DEPTH, 16, cols)
    mod_lat = lax.dynamic_index_in_dim(got, me, axis=2, keepdims=False)
    mod_ctx = got[:, :, 8, :]
    mods = jnp.stack([jnp.moveaxis(mod_lat, 0, 1).reshape(DEPTH, 6, D),
                      jnp.moveaxis(mod_ctx, 0, 1).reshape(DEPTH, 6, D)], axis=1)

    big = [A[n].astype(BF16) for n, _ in BIG_WEIGHTS]
    rows = [b.size // D for b in big]
    got = _gather_big(jnp.concatenate([b.reshape(-1, D) for b in big], axis=0), "gather_weights")
    off = 0
    for (n, ax), b, r in zip(BIG_WEIGHTS, big, rows):
        P[n] = _unshard(got[:, off:off + r].reshape((N_DEV,) + b.shape), ax)
        off += r
    for n in REPLICATED:
        P[n] = A[n]

    x0 = jnp.concatenate([x[0], ctx[0]], axis=0)
    loss8, dx0, G, dmods = _local_step(x0, loss_target[0], mods, P, s_len, l_len)
    loss = lax.psum(loss8[0, 0], ("x", "y", "c"))
    grad_x = dx0[:s_len].reshape(x.shape)

    small_names = ["dmods"] + list(REPLICATED[1:]) + [n for n, _ in SMALL_SHARDED]
    small_parts = [dmods] + [G[n] for n in small_names[1:]]
    got = _gather_small(_pack_rows(small_parts, 128, F32), "gather_small_grads")
    S8 = dict(zip(small_names, _unpack_rows(got, [a.shape for a in small_parts])))

    dm = S8["dmods"].reshape(N_DEV, DEPTH, 2, 6 * D)
    dm_mine = lax.dynamic_slice_in_dim(dm, me * cols, cols, axis=3)
    g_ada_w, pc = _ada_bwd(c_rows, ada_w, jnp.moveaxis(dm_mine[:, :, 0], 0, 1), jnp.moveaxis(dm_mine[:, :, 1], 0, 1))
    pc8 = _gather_small(pc.reshape(-1, 128), "gather_c_ctx_grad").reshape(N_DEV, 8, D)

    send = jnp.concatenate([_split8(G[n], ax).astype(BF16).reshape(N_DEV, -1, D) for n, ax in BIG_WEIGHTS], axis=1)
    gsum = _sum_slots(_exchange(send, "exchange_grads"))

    g_in = {"c_ctx": pc8[:, 0, :], "ada_w": g_ada_w[None],
            "ada_b": jnp.moveaxis(dm, 2, 1).reshape(2 * N_DEV, DEPTH, 6 * D)}
    for n in REPLICATED[1:]:
        g_in[n] = S8[n]
    for n, ax in SMALL_SHARDED:
        g_in[n] = _my_shard(S8[n], ax, me)
    off = 0
    for (n, ax), b, r in zip(BIG_WEIGHTS, big, rows):
        g_in[n] = gsum[off:off + r].reshape((1,) + b.shape)
        off += r
    res = {n: _adamw(A[n], g_in[n], A["m_" + n], A["v_" + n], "adamw_" + n) for n in WEIGHT_ORDER}
    outs = [loss, grad_x]
    for slot in range(4):
        outs += [res[n][slot] for n in WEIGHT_ORDER]
    return tuple(outs)
```

```python
import functools
import math

import jax
import jax.numpy as jnp
from jax import lax
from jax.experimental import pallas as pl
from jax.experimental.pallas import tpu as pltpu

F32 = jnp.float32
BF16 = jnp.bfloat16
SDS = jax.ShapeDtypeStruct
MESH = pl.DeviceIdType.MESH

N_DEV = 8
EPS = 1e-6
DEPTH = 4
GRID_W = 64
POOL_WINDOWS = (2, 4, 8, 16)
N_HEADS = 8
N_KV = 2
HEAD_DIM = 128
ROPE_THETA = 10000.0
RET_HEADS = 4
RET_DK = 256
RET_DV = 512
RET_CHUNK = 128
ADAM_LR = 0.001
ADAM_B1 = 0.9
ADAM_B2 = 0.999
ADAM_EPS = 1e-08
ADAM_WD = 0.01
ADAM_STEP = 10

ROW_TILE = 256
HALO = 8
VMEM_LIMIT_V7X = 56 * 1024 * 1024


def _params(n_axes=0):
    sem = ("arbitrary",) * n_axes if n_axes else None
    return pltpu.CompilerParams(dimension_semantics=sem, vmem_limit_bytes=VMEM_LIMIT_V7X)


def _pick(n, cap, mult):
    best = None
    for d in range(mult, min(n, cap) + 1, mult):
        if n % d == 0:
            best = d
    return best if best is not None else n


def _dot(a, b):
    return jnp.dot(a, b, preferred_element_type=F32)


def _dot_nt(a, b):
    return lax.dot_general(a, b, (((1,), (1,)), ((), ())), preferred_element_type=F32)


def _dot_tn(a, b):
    return lax.dot_general(a, b, (((0,), (0,)), ((), ())), preferred_element_type=F32)


def _bf(v):
    return v.astype(BF16)


def _sigmoid(v):
    return 0.5 * jnp.tanh(0.5 * v) + 0.5


MM_VMEM_BUDGET = 40 * 1024 * 1024
MM_STEP_BYTES = 1 << 20


def _divisors(n, mult, cap):
    return [d for d in range(mult, min(n, cap) + 1, mult) if n % d == 0] or [n]


def _mm_tiles(mode, M, N, K, a_item, b_item, o_item):
    best = None
    for tm in _divisors(M, 128 if mode == "tn" else 16, 2816):
        for tn in _divisors(N, 128, 2048):
            for tk in _divisors(K, 16 if mode == "tn" else 128, 2816):
                ni, nj, nk = M // tm, N // tn, K // tk
                vmem = 2 * (tm * tk * a_item + tk * tn * b_item + tm * tn * o_item) + tm * tn * 4
                if vmem > MM_VMEM_BUDGET:
                    continue
                a_reads = 1 if nk == 1 else nj
                b_reads = 1 if (nk == 1 and nj == 1) else ni
                cost = (M * K * a_item * a_reads + K * N * b_item * b_reads + M * N * o_item
                        + ni * nj * nk * MM_STEP_BYTES + (nk - 1) * M * N)
                if best is None or cost < best[0]:
                    best = (cost, tm, tn, tk)
    return best[1:]


def _mm(a, b, mode, out_dtype, name):
    if mode == "nn":
        (M, K), (K2, N) = a.shape, b.shape
    elif mode == "nt":
        (M, K), (N, K2) = a.shape, b.shape
    else:
        (K, M), (K2, N) = a.shape, b.shape
    assert K == K2, (a.shape, b.shape, mode)
    tm, tn, tk = _mm_tiles(mode, M, N, K, a.dtype.itemsize, b.dtype.itemsize, jnp.dtype(out_dtype).itemsize)
    nk = K // tk
    if mode == "nn":
        a_spec = pl.BlockSpec((tm, tk), lambda i, j, k: (i, k))
        b_spec = pl.BlockSpec((tk, tn), lambda i, j, k: (k, j))
    elif mode == "nt":
        a_spec = pl.BlockSpec((tm, tk), lambda i, j, k: (i, k))
        b_spec = pl.BlockSpec((tn, tk), lambda i, j, k: (j, k))
    else:
        a_spec = pl.BlockSpec((tk, tm), lambda i, j, k: (k, i))
        b_spec = pl.BlockSpec((tk, tn), lambda i, j, k: (k, j))
    dot = {"nn": _dot, "nt": _dot_nt, "tn": _dot_tn}[mode]

    def body(a_ref, b_ref, o_ref, acc_ref):
        part = dot(_bf(a_ref[...]), _bf(b_ref[...]))
        if nk == 1:
            o_ref[...] = part.astype(out_dtype)
        else:
            k = pl.program_id(2)

            @pl.when(k == 0)
            def _():
                acc_ref[...] = part

            @pl.when(k > 0)
            def _():
                acc_ref[...] += part

            @pl.when(k == nk - 1)
            def _():
                o_ref[...] = acc_ref[...].astype(out_dtype)

    return pl.pallas_call(
        body, name=name, grid=(M // tm, N // tn, nk),
        in_specs=[a_spec, b_spec],
        out_specs=pl.BlockSpec((tm, tn), lambda i, j, k: (i, j)),
        out_shape=SDS((M, N), out_dtype),
        scratch_shapes=[pltpu.VMEM((tm, tn), F32)],
        compiler_params=_params(3),
    )(a, b)


def _seg_spec(n_lat, d):
    return pl.BlockSpec((1, 6, d), lambda i: ((i >= n_lat).astype(jnp.int32), 0, 0))


def _seg_acc_spec(n_lat, d):
    return pl.BlockSpec((1, 1, d), lambda i: ((i >= n_lat).astype(jnp.int32), 0, 0))


def _res_norm(x, y, gmod, gk, nw, nmod, nk, h_dtype, n_lat, name):
    R, D = x.shape
    has_res, has_norm = y is not None, nw is not None
    row = pl.BlockSpec((ROW_TILE, D), lambda i: (i, 0))
    vec = pl.BlockSpec((1, D), lambda i: (0, 0))
    ins, specs, outs, ospecs = [x], [row], [], []
    if has_res:
        ins += [y, gmod]
        specs += [row, _seg_spec(n_lat, D)]
        outs.append(SDS((R, D), F32))
        ospecs.append(row)
    if has_norm:
        ins += [nw.reshape(1, D), nmod]
        specs += [vec, _seg_spec(n_lat, D)]
        outs.append(SDS((R, D), h_dtype))
        ospecs.append(row)

    def body(*refs):
        refs = list(refs)
        z = refs.pop(0)[...]
        if has_res:
            y_ref, g_ref = refs.pop(0), refs.pop(0)
            z = z + g_ref[0, pl.ds(3 * gk + 2, 1), :] * y_ref[...].astype(F32)
        if has_norm:
            nw_ref, m_ref = refs.pop(0), refs.pop(0)
        if has_res:
            refs.pop(0)[...] = z
        if has_norm:
            r = lax.rsqrt(jnp.mean(z * z, axis=-1, keepdims=True) + EPS)
            h = (z * r) * nw_ref[...]
            h = h * (1.0 + m_ref[0, pl.ds(3 * nk + 1, 1), :]) + m_ref[0, pl.ds(3 * nk, 1), :]
            refs.pop(0)[...] = h.astype(h_dtype)

    res = pl.pallas_call(
        body, name=name, grid=(R // ROW_TILE,), in_specs=specs, out_specs=ospecs,
        out_shape=outs, compiler_params=_params(1),
    )(*ins)
    return res if len(res) > 1 else res[0]


def _gate_bwd(dz, y, mod, k, out_dtype, n_lat, name):
    R, D = dz.shape
    row = pl.BlockSpec((ROW_TILE, D), lambda i: (i, 0))

    def body(dz_ref, y_ref, m_ref, dy_ref, dg_ref):
        i = pl.program_id(0)
        dzv = dz_ref[...]
        dy_ref[...] = (m_ref[0, pl.ds(3 * k + 2, 1), :] * dzv).astype(out_dtype)

        @pl.when((i == 0) | (i == n_lat))
        def _():
            dg_ref[...] = jnp.zeros_like(dg_ref)

        dg_ref[0] += jnp.sum(dzv * y_ref[...].astype(F32), axis=0, keepdims=True)

    return pl.pallas_call(
        body, name=name, grid=(R // ROW_TILE,),
        in_specs=[row, row, _seg_spec(n_lat, D)],
        out_specs=[row, _seg_acc_spec(n_lat, D)],
        out_shape=[SDS((R, D), out_dtype), SDS((2, 1, D), F32)],
        compiler_params=_params(1),
    )(dz, y, mod)


def _norm_bwd(dz, dh, x, nw, mod, k, n_lat, name):
    R, D = x.shape
    row = pl.BlockSpec((ROW_TILE, D), lambda i: (i, 0))
    vec = pl.BlockSpec((1, D), lambda i: (0, 0))

    def body(dz_ref, dh_ref, x_ref, nw_ref, m_ref, dx_ref, dnw_ref, dsh_ref, dsc_ref):
        i = pl.program_id(0)
        xv = x_ref[...]
        dhv = dh_ref[...].astype(F32)
        nwv = nw_ref[...]
        sc1 = 1.0 + m_ref[0, pl.ds(3 * k + 1, 1), :]
        r = lax.rsqrt(jnp.mean(xv * xv, axis=-1, keepdims=True) + EPS)
        xhat = xv * r
        a = dhv * (nwv * sc1)
        dx_ref[...] = dz_ref[...] + r * (a - xhat * jnp.mean(a * xhat, axis=-1, keepdims=True))

        @pl.when(i == 0)
        def _():
            dnw_ref[...] = jnp.zeros_like(dnw_ref)

        @pl.when((i == 0) | (i == n_lat))
        def _():
            dsh_ref[...] = jnp.zeros_like(dsh_ref)
            dsc_ref[...] = jnp.zeros_like(dsc_ref)

        dnw_ref[...] += jnp.sum(dhv * xhat, axis=0, keepdims=True) * sc1
        dsh_ref[0] += jnp.sum(dhv, axis=0, keepdims=True)
        dsc_ref[0] += jnp.sum(dhv * xhat, axis=0, keepdims=True) * nwv

    return pl.pallas_call(
        body, name=name, grid=(R // ROW_TILE,),
        in_specs=[row, row, row, vec, _seg_spec(n_lat, D)],
        out_specs=[row, vec, _seg_acc_spec(n_lat, D), _seg_acc_spec(n_lat, D)],
        out_shape=[SDS((R, D), F32), SDS((1, D), F32), SDS((2, 1, D), F32), SDS((2, 1, D), F32)],
        compiler_params=_params(1),
    )(dz, dh, x, nw.reshape(1, D), mod)


def _loss_bwd(xf, target, n_lat):
    R, D = xf.shape
    row = pl.BlockSpec((ROW_TILE, D), lambda i: (i, 0))
    tgt = pl.BlockSpec((ROW_TILE, D), lambda i: (jnp.minimum(i, n_lat - 1), 0))

    def body(x_ref, t_ref, dx_ref, loss_ref):
        i = pl.program_id(0)
        e = jnp.where(i < n_lat, x_ref[...] - t_ref[...], 0.0)
        dx_ref[...] = e * (1.0 / D)

        @pl.when(i == 0)
        def _():
            loss_ref[...] = jnp.zeros_like(loss_ref)

        loss_ref[...] += 0.5 * jnp.sum(jnp.mean(e * e, axis=-1, keepdims=True))

    return pl.pallas_call(
        body, name="loss_bwd", grid=(R // ROW_TILE,),
        in_specs=[row, tgt],
        out_specs=[row, pl.BlockSpec((8, 128), lambda i: (0, 0))],
        out_shape=[SDS((R, D), F32), SDS((8, 128), F32)],
        compiler_params=_params(1),
    )(xf, target)


def _halo_specs(n_tiles, width, tile=ROW_TILE):
    per = tile // HALO
    prev = pl.BlockSpec((HALO, width), lambda i: (jnp.maximum(i * per - 1, 0), 0))
    nxt = pl.BlockSpec((HALO, width), lambda i: (jnp.minimum((i + 1) * per, n_tiles * per - 1), 0))
    return prev, nxt


def _edge_flags(i, n_lat, n_tiles):
    first = (i == 0) | (i == n_lat)
    last = (i == n_lat - 1) | (i == n_tiles - 1)
    return first, last


def _conv_gate_fwd(u, conv_w, conv_b, n_lat, name):
    R, F2 = u.shape
    F = F2 // 2
    n_tiles = R // ROW_TILE
    T = ROW_TILE
    cw = _pick(F, 256, 128)
    row = pl.BlockSpec((T, F2), lambda i: (i, 0))
    prev, nxt = _halo_specs(n_tiles, F2)

    def body(u_ref, p_ref, n_ref, w_ref, b_ref, o_ref):
        i = pl.program_id(0)
        first, last = _edge_flags(i, n_lat, n_tiles)
        ridx = lax.broadcasted_iota(jnp.int32, (T, 1), 0)

        def conv(c0):
            cols = pl.ds(c0, cw)
            uv = u_ref[:, cols]
            pr = jnp.where(first, 0.0, p_ref[pl.ds(HALO - 1, 1), cols])
            nx = jnp.where(last, 0.0, n_ref[pl.ds(0, 1), cols])
            up = jnp.where(ridx == 0, pr, pltpu.roll(uv, 1, 0))
            un = jnp.where(ridx == T - 1, nx, pltpu.roll(uv, T - 1, 0))
            return (up * w_ref[pl.ds(0, 1), cols] + uv * w_ref[pl.ds(1, 1), cols]
                    + un * w_ref[pl.ds(2, 1), cols] + b_ref[:, cols])

        for c0 in range(0, F, cw):
            ca, cv = conv(c0), conv(F + c0)
            o_ref[:, pl.ds(c0, cw)] = (ca * _sigmoid(ca) * cv).astype(BF16)

    return pl.pallas_call(
        body, name=name, grid=(n_tiles,),
        in_specs=[row, prev, nxt, pl.BlockSpec((3, F2), lambda i: (0, 0)),
                  pl.BlockSpec((1, F2), lambda i: (0, 0))],
        out_specs=pl.BlockSpec((T, F), lambda i: (i, 0)),
        out_shape=SDS((R, F), BF16), compiler_params=_params(1),
    )(u, u, u, conv_w, conv_b)


def _conv_gate_bwd(u, dgact, conv_w, conv_b, n_lat, name):
    R, F2 = u.shape
    F = F2 // 2
    n_tiles = R // ROW_TILE
    T, N = ROW_TILE, ROW_TILE + 2 * HALO
    cw = _pick(F, 256, 128)
    rowu = pl.BlockSpec((T, F2), lambda i: (i, 0))
    rowg = pl.BlockSpec((T, F), lambda i: (i, 0))
    pu, nu = _halo_specs(n_tiles, F2)
    pg, ng = _halo_specs(n_tiles, F)

    def body(u_ref, pu_ref, nu_ref, g_ref, pg_ref, ng_ref, w_ref, b_ref, du_ref, dw_ref, db_ref):
        i = pl.program_id(0)
        first, last = _edge_flags(i, n_lat, n_tiles)

        @pl.when(i == 0)
        def _():
            dw_ref[...] = jnp.zeros_like(dw_ref)
            db_ref[...] = jnp.zeros_like(db_ref)

        def ext(t_ref, p_ref, n_ref, cols):
            pr = jnp.where(first, 0.0, p_ref[:, cols])
            nx = jnp.where(last, 0.0, n_ref[:, cols])
            return jnp.concatenate([pr, t_ref[:, cols], nx], axis=0)

        def conv(c0):
            cols = pl.ds(c0, cw)
            e = ext(u_ref, pu_ref, nu_ref, cols)
            up, un = pltpu.roll(e, 1, 0), pltpu.roll(e, N - 1, 0)
            c = (up * w_ref[pl.ds(0, 1), cols] + e * w_ref[pl.ds(1, 1), cols]
                 + un * w_ref[pl.ds(2, 1), cols] + b_ref[:, cols])
            return c, up, e, un

        def back(c0, dc, up, e, un):
            cols = pl.ds(c0, cw)
            du = (pltpu.roll(dc, N - 1, 0) * w_ref[pl.ds(0, 1), cols] + dc * w_ref[pl.ds(1, 1), cols]
                  + pltpu.roll(dc, 1, 0) * w_ref[pl.ds(2, 1), cols])
            du_ref[:, cols] = du[HALO:HALO + T].astype(BF16)
            dct = dc[HALO:HALO + T]
            dw_ref[pl.ds(0, 1), cols] += jnp.sum(dct * up[HALO:HALO + T], axis=0, keepdims=True)
            dw_ref[pl.ds(1, 1), cols] += jnp.sum(dct * e[HALO:HALO + T], axis=0, keepdims=True)
            dw_ref[pl.ds(2, 1), cols] += jnp.sum(dct * un[HALO:HALO + T], axis=0, keepdims=True)
            db_ref[:, cols] += jnp.sum(dct, axis=0, keepdims=True)

        for c0 in range(0, F, cw):
            dg = ext(g_ref, pg_ref, ng_ref, pl.ds(c0, cw))
            ca, upa, ea, una = conv(c0)
            cv, upv, ev, unv = conv(F + c0)
            s = _sigmoid(ca)
            back(F + c0, dg * (ca * s), upv, ev, unv)
            back(c0, dg * cv * (s * (1.0 + ca * (1.0 - s))), upa, ea, una)

    return pl.pallas_call(
        body, name=name, grid=(n_tiles,),
        in_specs=[rowu, pu, nu, rowg, pg, ng, pl.BlockSpec((3, F2), lambda i: (0, 0)),
                  pl.BlockSpec((1, F2), lambda i: (0, 0))],
        out_specs=[rowu, pl.BlockSpec((3, F2), lambda i: (0, 0)), pl.BlockSpec((1, F2), lambda i: (0, 0))],
        out_shape=[SDS((R, F2), BF16), SDS((3, F2), F32), SDS((1, F2), F32)],
        compiler_params=_params(1),
    )(u, u, u, dgact, dgact, dgact, conv_w, conv_b)


def _pool_counts(i, n_lat, s_len, l_len, n_rows, offset):
    ctx = i >= n_lat
    t0 = jnp.where(ctx, i - n_lat, i) * ROW_TILE + offset
    seg = jnp.where(ctx, l_len, s_len)
    t = t0 + lax.broadcasted_iota(jnp.int32, (n_rows, 1), 0)
    out = []
    for win in POOL_WINDOWS:
        cnt = jnp.minimum(t + win // 2, seg) - jnp.maximum(t - win // 2, 0)
        out.append(jnp.maximum(cnt, 1).astype(F32))
    return out


def _window_sum(e, lo, hi, n):
    acc = None
    for j in range(lo, hi + 1):
        term = e if j == 0 else pltpu.roll(e, (-j) % n, 0)
        acc = term if acc is None else acc + term
    return acc


def _pool_fwd(h, w, b, scale, n_lat, s_len, l_len, name):
    R, D = h.shape
    G = D // 4
    n_tiles = R // ROW_TILE
    T, N = ROW_TILE, ROW_TILE + 2 * HALO
    row = pl.BlockSpec((T, D), lambda i: (i, 0))
    prev, nxt = _halo_specs(n_tiles, D)
    vec = pl.BlockSpec((1, D), lambda i: (0, 0))

    def body(h_ref, p_ref, n_ref, w_ref, b_ref, s_ref, y_ref):
        i = pl.program_id(0)
        first, last = _edge_flags(i, n_lat, n_tiles)
        cnts = _pool_counts(i, n_lat, s_len, l_len, T, 0)
        for g, win in enumerate(POOL_WINDOWS):
            cols = pl.ds(g * G, G)
            pr = jnp.where(first, 0.0, p_ref[:, cols])
            nx = jnp.where(last, 0.0, n_ref[:, cols])
            hv = h_ref[:, cols]
            e = jnp.concatenate([pr, hv, nx], axis=0)
            mean = _window_sum(e, -(win // 2), win // 2 - 1, N)[HALO:HALO + T] / cnts[g]
            yg = _dot(_bf(mean - hv), w_ref[g])
            y_ref[:, cols] = (yg + b_ref[:, cols]) * s_ref[:, cols]

    return pl.pallas_call(
        body, name=name, grid=(n_tiles,),
        in_specs=[row, prev, nxt, pl.BlockSpec((4, G, G), lambda i: (0, 0, 0)), vec, vec],
        out_specs=row, out_shape=SDS((R, D), F32), compiler_params=_params(1),
    )(h, h, h, w, b, scale)


def _pool_bwd(h, dy, w, b, scale, n_lat, s_len, l_len, name):
    R, D = h.shape
    G = D // 4
    n_tiles = R // ROW_TILE
    T, N = ROW_TILE, ROW_TILE + 2 * HALO
    row = pl.BlockSpec((T, D), lambda i: (i, 0))
    prev, nxt = _halo_specs(n_tiles, D)
    vec = pl.BlockSpec((1, D), lambda i: (0, 0))
    wspec = pl.BlockSpec((4, G, G), lambda i: (0, 0, 0))

    def body(h_ref, ph_ref, nh_ref, d_ref, pd_ref, nd_ref, w_ref, b_ref, s_ref,
             dh_ref, dw_ref, db_ref, ds_ref):
        i = pl.program_id(0)
        first, last = _edge_flags(i, n_lat, n_tiles)

        @pl.when(i == 0)
        def _():
            dw_ref[...] = jnp.zeros_like(dw_ref)
            db_ref[...] = jnp.zeros_like(db_ref)
            ds_ref[...] = jnp.zeros_like(ds_ref)

        cnts = _pool_counts(i, n_lat, s_len, l_len, T, 0)
        cnts_ext = _pool_counts(i, n_lat, s_len, l_len, N, -HALO)
        for g, win in enumerate(POOL_WINDOWS):
            cols = pl.ds(g * G, G)

            def ext(t_ref, p_ref, n_ref):
                pr = jnp.where(first, 0.0, p_ref[:, cols])
                nx = jnp.where(last, 0.0, n_ref[:, cols])
                return jnp.concatenate([pr, t_ref[:, cols], nx], axis=0)

            hv = h_ref[:, cols]
            mean = _window_sum(ext(h_ref, ph_ref, nh_ref), -(win // 2), win // 2 - 1, N)[HALO:HALO + T] / cnts[g]
            z = _bf(mean - hv)
            sc = s_ref[:, cols]
            dye = ext(d_ref, pd_ref, nd_ref)
            dt = _bf(dye * sc)
            dz = _dot_nt(dt, w_ref[g])
            dm = dz / cnts_ext[g]
            dh = _window_sum(dm, -(win // 2 - 1), win // 2, N) - dz
            dh_ref[:, cols] = dh[HALO:HALO + T]
            dyt = dye[HALO:HALO + T]
            dw_ref[g] += _dot_tn(z, dt[HALO:HALO + T])
            db_ref[:, cols] += jnp.sum(dyt * sc, axis=0, keepdims=True)
            ds_ref[:, cols] += jnp.sum(dyt * (_dot(z, w_ref[g]) + b_ref[:, cols]), axis=0, keepdims=True)

    return pl.pallas_call(
        body, name=name, grid=(n_tiles,),
        in_specs=[row, prev, nxt, row, prev, nxt, wspec, vec, vec],
        out_specs=[row, wspec, vec, vec],
        out_shape=[SDS((R, D), F32), SDS((4, G, G), F32), SDS((1, D), F32), SDS((1, D), F32)],
        compiler_params=_params(1),
    )(h, h, h, dy, dy, dy, w, b, scale)


def _rope_tables(s_len, l_len):
    t = jnp.arange(s_len)
    row = (t // GRID_W).astype(F32)
    col = (t % GRID_W).astype(F32)
    axis_dim = HEAD_DIM // 2
    inv = ROPE_THETA ** (-jnp.arange(0, axis_dim, 2, dtype=F32) / axis_dim)
    ar, ac = row[:, None] * inv, col[:, None] * inv
    cos = jnp.concatenate([jnp.cos(ar), jnp.cos(ar), jnp.cos(ac), jnp.cos(ac)], axis=-1)
    sin = jnp.concatenate([-jnp.sin(ar), jnp.sin(ar), -jnp.sin(ac), jnp.sin(ac)], axis=-1)
    cos = jnp.concatenate([cos, jnp.ones((l_len, HEAD_DIM), F32)], axis=0)
    sin = jnp.concatenate([sin, jnp.zeros((l_len, HEAD_DIM), F32)], axis=0)
    return cos, sin


def _swap_halves(v):
    lane = lax.broadcasted_iota(jnp.int32, v.shape, 1)
    return jnp.where((lane % 64) < 32, pltpu.roll(v, 96, 1), pltpu.roll(v, 32, 1))


def _qk_prep_fwd(qkv, q_gain, k_gain, cos, sin):
    R = qkv.shape[0]
    NQ, NK = N_HEADS * HEAD_DIM, N_KV * HEAD_DIM
    T = ROW_TILE
    vec = pl.BlockSpec((1, HEAD_DIM), lambda i: (0, 0))
    tab = pl.BlockSpec((T, HEAD_DIM), lambda i: (i, 0))

    def body(x_ref, qg_ref, kg_ref, c_ref, s_ref, q_ref, k_ref, v_ref):
        cosv, sinv = c_ref[...], s_ref[...]

        def prep(c0, gain):
            xh = x_ref[:, pl.ds(c0, HEAD_DIM)]
            xn = xh * lax.rsqrt(jnp.mean(xh * xh, axis=-1, keepdims=True) + EPS) * gain
            return _bf(xn * cosv + _swap_halves(xn) * sinv)

        for hd in range(N_HEADS):
            q_ref[:, pl.ds(hd * HEAD_DIM, HEAD_DIM)] = prep(hd * HEAD_DIM, qg_ref[...])
        for hd in range(N_KV):
            k_ref[:, pl.ds(hd * HEAD_DIM, HEAD_DIM)] = prep(NQ + hd * HEAD_DIM, kg_ref[...])
            v_ref[:, pl.ds(2 * hd * HEAD_DIM, HEAD_DIM)] = _bf(x_ref[:, pl.ds(NQ + NK + hd * HEAD_DIM, HEAD_DIM)])
            v_ref[:, pl.ds((2 * hd + 1) * HEAD_DIM, HEAD_DIM)] = jnp.ones((T, HEAD_DIM), BF16)

    return pl.pallas_call(
        body, name="qk_prep_fwd", grid=(R // T,),
        in_specs=[pl.BlockSpec((T, NQ + 2 * NK), lambda i: (i, 0)), vec, vec, tab, tab],
        out_specs=[pl.BlockSpec((T, NQ), lambda i: (i, 0)), pl.BlockSpec((T, NK), lambda i: (i, 0)),
                   pl.BlockSpec((T, 2 * NK), lambda i: (i, 0))],
        out_shape=[SDS((R, NQ), BF16), SDS((R, NK), BF16), SDS((R, 2 * NK), BF16)],
        compiler_params=_params(1),
    )(qkv, q_gain, k_gain, cos, sin)


def _qk_prep_bwd(qkv, dq, dk, dv, q_gain, k_gain, cos, sin):
    R = qkv.shape[0]
    NQ, NK = N_HEADS * HEAD_DIM, N_KV * HEAD_DIM
    T = ROW_TILE
    vec = pl.BlockSpec((1, HEAD_DIM), lambda i: (0, 0))
    tab = pl.BlockSpec((T, HEAD_DIM), lambda i: (i, 0))

    def body(x_ref, dq_ref, dk_ref, dv_ref, qg_ref, kg_ref, c_ref, s_ref, o_ref, dqg_ref, dkg_ref):
        i = pl.program_id(0)
        cosv, sinv = c_ref[...], s_ref[...]

        @pl.when(i == 0)
        def _():
            dqg_ref[...] = jnp.zeros_like(dqg_ref)
            dkg_ref[...] = jnp.zeros_like(dkg_ref)

        def back(c0, dout, gain, dg_ref):
            xh = x_ref[:, pl.ds(c0, HEAD_DIM)]
            r = lax.rsqrt(jnp.mean(xh * xh, axis=-1, keepdims=True) + EPS)
            xhat = xh * r
            dxn = dout * cosv + _swap_halves(dout * sinv)
            dg_ref[...] += jnp.sum(dxn * xhat, axis=0, keepdims=True)
            a = dxn * gain
            o_ref[:, pl.ds(c0, HEAD_DIM)] = _bf(r * (a - xhat * jnp.mean(a * xhat, axis=-1, keepdims=True)))

        for hd in range(N_HEADS):
            back(hd * HEAD_DIM, dq_ref[:, pl.ds(hd * HEAD_DIM, HEAD_DIM)], qg_ref[...], dqg_ref)
        for hd in range(N_KV):
            back(NQ + hd * HEAD_DIM, dk_ref[:, pl.ds(hd * HEAD_DIM, HEAD_DIM)], kg_ref[...], dkg_ref)
        o_ref[:, pl.ds(NQ + NK, NK)] = _bf(dv_ref[...])

    return pl.pallas_call(
        body, name="qk_prep_bwd", grid=(R // T,),
        in_specs=[pl.BlockSpec((T, NQ + 2 * NK), lambda i: (i, 0)), pl.BlockSpec((T, NQ), lambda i: (i, 0)),
                  pl.BlockSpec((T, NK), lambda i: (i, 0)), pl.BlockSpec((T, NK), lambda i: (i, 0)),
                  vec, vec, tab, tab],
        out_specs=[pl.BlockSpec((T, NQ + 2 * NK), lambda i: (i, 0)), vec, vec],
        out_shape=[SDS((R, NQ + 2 * NK), BF16), SDS((1, HEAD_DIM), F32), SDS((1, HEAD_DIM), F32)],
        compiler_params=_params(1),
    )(qkv, dq, dk, dv, q_gain, k_gain, cos, sin)


def _flash_fwd(q, k, v, s_len, l_len):
    R = q.shape[0]
    T = ROW_TILE
    n_lat = s_len // T
    ck = _pick(s_len, 512, 128)
    scale = HEAD_DIM ** -0.5
    group = N_HEADS // N_KV
    GW = group * HEAD_DIM

    M = group * T
    to_log2 = scale * math.log2(math.e)

    def body(q_ref, k_ref, v_ref, o_ref, lse_ref, m8_s, mb_s, acc_s):
        i = pl.program_id(1)
        qv = jnp.concatenate([q_ref[:, pl.ds(hh * HEAD_DIM, HEAD_DIM)] for hh in range(group)], axis=0)
        chunks = s_len // ck

        m8_s[...] = jnp.full_like(m8_s, -jnp.inf)

        def stat(rows, n):
            st = _dot_nt(k_ref[rows, :], qv)
            m8_s[...] = jnp.maximum(m8_s[...], jnp.max(st.reshape(n // 8, 8, M), axis=0))

        @pl.when(i < n_lat)
        def _():
            def loop(c, carry):
                stat(pl.ds(pl.multiple_of(c * ck, ck), ck), ck)
                return carry
            lax.fori_loop(0, chunks, loop, 0)

        stat(pl.ds(s_len, l_len), l_len)
        m_row = jnp.max(m8_s[...], axis=0, keepdims=True) * to_log2
        mb = jnp.broadcast_to(m_row, (HEAD_DIM, M)).T
        mb_s[...] = jnp.concatenate([mb] * (ck // HEAD_DIM), axis=1)

        acc_s[...] = jnp.zeros_like(acc_s)

        def step(rows, n):
            s2 = _dot_nt(qv, k_ref[rows, :]) * to_log2
            p = jnp.exp2(s2 - mb_s[:, pl.ds(0, n)])
            acc_s[...] += _dot(_bf(p), v_ref[rows, :])

        @pl.when(i < n_lat)
        def _():
            def loop(c, carry):
                step(pl.ds(pl.multiple_of(c * ck, ck), ck), ck)
                return carry
            lax.fori_loop(0, chunks, loop, 0)

        step(pl.ds(s_len, l_len), l_len)
        l_rep = acc_s[:, pl.ds(HEAD_DIM, HEAD_DIM)]
        o = acc_s[:, pl.ds(0, HEAD_DIM)] / l_rep
        for hh in range(group):
            o_ref[:, pl.ds(hh * HEAD_DIM, HEAD_DIM)] = o[hh * T:(hh + 1) * T]
        lse = (mb_s[:, pl.ds(0, HEAD_DIM)] + jnp.log2(l_rep)) * math.log(2.0)
        lse_ref[...] = jnp.max(lse, axis=-1, keepdims=True).reshape(group, T, 1)

    return pl.pallas_call(
        body, name="flash_fwd", grid=(N_KV, R // T),
        in_specs=[pl.BlockSpec((T, GW), lambda g, i: (i, g)),
                  pl.BlockSpec((R, HEAD_DIM), lambda g, i: (0, g)),
                  pl.BlockSpec((R, 2 * HEAD_DIM), lambda g, i: (0, g))],
        out_specs=[pl.BlockSpec((T, GW), lambda g, i: (i, g)),
                   pl.BlockSpec((group, T, 1), lambda g, i: (g, i, 0))],
        out_shape=[SDS((R, N_HEADS * HEAD_DIM), F32), SDS((N_HEADS, R, 1), F32)],
        scratch_shapes=[pltpu.VMEM((8, M), F32), pltpu.VMEM((M, ck), F32), pltpu.VMEM((M, 2 * HEAD_DIM), F32)],
        compiler_params=_params(2),
    )(q, k, v)


def _flash_bwd(q, k, v, o, lse, do, s_len, l_len):
    R = q.shape[0]
    T = ROW_TILE
    n_lat = s_len // T
    ck = _pick(s_len, 512, 128)
    scale = HEAD_DIM ** -0.5
    group = N_HEADS // N_KV
    GW = group * HEAD_DIM
    qspec = pl.BlockSpec((T, GW), lambda g, i: (i, g))
    kspec = pl.BlockSpec((R, HEAD_DIM), lambda g, i: (0, g))

    M = group * T
    log2e = math.log2(math.e)

    def body(q_ref, do_ref, o_ref, lse_ref, k_ref, v_ref, dq_ref, dk_ref, dv_ref, dq_s, lse_s, delta_s):
        i = pl.program_id(1)

        @pl.when(i == 0)
        def _():
            dk_ref[...] = jnp.zeros_like(dk_ref)
            dv_ref[...] = jnp.zeros_like(dv_ref)

        def stacked(ref):
            return jnp.concatenate([ref[:, pl.ds(hh * HEAD_DIM, HEAD_DIM)] for hh in range(group)], axis=0)

        qv = stacked(q_ref)
        dov = stacked(do_ref)
        dob = _bf(dov)
        delta_s[...] = jnp.broadcast_to(jnp.sum(dov * stacked(o_ref), axis=-1, keepdims=True), (M, ck))
        lse_s[...] = jnp.broadcast_to(lse_ref[...].reshape(M, 1) * log2e, (M, ck))
        dq_s[...] = jnp.zeros_like(dq_s)

        def step(rows, n):
            kv, vv = k_ref[rows, :], v_ref[rows, :]
            p = jnp.exp2(_dot_nt(qv, kv) * (scale * log2e) - lse_s[:, pl.ds(0, n)])
            dv_ref[rows, :] += _dot_tn(_bf(p), dob)
            ds = _bf(p * (_dot_nt(dob, vv) - delta_s[:, pl.ds(0, n)]) * scale)
            dq_s[...] += _dot(ds, kv)
            dk_ref[rows, :] += _dot_tn(ds, qv)

        @pl.when(i < n_lat)
        def _():
            def loop(c, carry):
                step(pl.ds(pl.multiple_of(c * ck, ck), ck), ck)
                return carry
            lax.fori_loop(0, s_len // ck, loop, 0)

        step(pl.ds(s_len, l_len), l_len)
        for hh in range(group):
            dq_ref[:, pl.ds(hh * HEAD_DIM, HEAD_DIM)] = dq_s[pl.ds(hh * T, T), :]

    return pl.pallas_call(
        body, name="flash_bwd", grid=(N_KV, R // T),
        in_specs=[qspec, qspec, qspec, pl.BlockSpec((group, T, 1), lambda g, i: (g, i, 0)), kspec,
                  pl.BlockSpec((R, HEAD_DIM), lambda g, i: (0, 2 * g))],
        out_specs=[qspec, kspec, kspec],
        out_shape=[SDS((R, N_HEADS * HEAD_DIM), F32), SDS((R, N_KV * HEAD_DIM), F32),
                   SDS((R, N_KV * HEAD_DIM), F32)],
        scratch_shapes=[pltpu.VMEM((M, HEAD_DIM), F32), pltpu.VMEM((M, ck), F32), pltpu.VMEM((M, ck), F32)],
        compiler_params=_params(2),
    )(q, do, o, lse, k, v)


K_SCALE = RET_DK ** -0.5


def _log_sigmoid(v):
    return -(jnp.maximum(-v, 0.0) + jnp.log(1.0 + jnp.exp(-jnp.abs(v))))


def _ret_decays(d, lg):
    C = RET_CHUNK
    ic = lax.broadcasted_iota(jnp.int32, (C, 1), 0)
    ir = lax.broadcasted_iota(jnp.int32, (1, C), 1)
    li = jnp.where(d == 0, ic, C - 1 - ic).astype(F32)
    lj = jnp.where(d == 0, ir, C - 1 - ir).astype(F32)
    diff = li - lj
    mask = jnp.where(diff >= 0, jnp.exp(jnp.maximum(diff, 0.0) * lg), 0.0)
    qd = jnp.exp((li + 1.0) * lg)
    kd = jnp.exp((C - 1.0 - li) * lg)
    cd = jnp.exp(C * lg)
    return li, diff, mask, qd, kd, cd


def _ctx_weights(d, t, lg, l_len):
    C = RET_CHUNK
    j = (t * C + lax.broadcasted_iota(jnp.int32, (C, 1), 0)).astype(F32)
    e = jnp.where(d == 0, (l_len - 1.0) - j, j)
    return e, jnp.exp(e * lg)


def _ret_specs(n_lat_c, n_ctx_c, ctx_first):
    def blk(d, t):
        if ctx_first:
            n = jnp.maximum(t - n_ctx_c, 0)
            lat = jnp.where(d == 0, n, n_lat_c - 1 - n)
            return jnp.where(t < n_ctx_c, n_lat_c + t, lat)
        n = jnp.minimum(t, n_lat_c - 1)
        lat = jnp.where(d == 0, n_lat_c - 1 - n, n)
        return jnp.where(t >= n_lat_c, t, lat)
    return blk


def _ret_fwd(proj, lgt, s_len, l_len):
    R = proj.shape[0]
    C, H, DK, DV = RET_CHUNK, RET_HEADS, RET_DK, RET_DV
    nl, nc = s_len // C, l_len // C
    blk = _ret_specs(nl, nc, True)

    def body(q_ref, k_ref, v_ref, lg_ref, o_ref, st_ref, r_s):
        d, t = pl.program_id(0), pl.program_id(2)
        lg = jnp.max(_log_sigmoid(lg_ref[0, 0]), axis=-1, keepdims=True)

        @pl.when(t == 0)
        def _():
            r_s[...] = jnp.zeros_like(r_s)

        @pl.when(t < nc)
        def _():
            _, w = _ctx_weights(d, t, lg, l_len)
            r_s[...] += _dot_tn(_bf(k_ref[...] * K_SCALE * w), _bf(v_ref[...]))
            o_ref[0] = jnp.zeros((C, DV), F32)

        @pl.when(t >= nc)
        def _():
            _, _, mask, qd, kd, cd = _ret_decays(d, lg)
            qb, kv, vb = _bf(q_ref[...]), k_ref[...] * K_SCALE, _bf(v_ref[...])
            r = r_s[...]
            st_ref[0, 0, 0] = r
            att = _dot_nt(qb, _bf(kv)) * mask
            o_ref[0] = _dot(_bf(att), vb) + _dot(qb, _bf(r)) * qd
            r_s[...] = r * cd + _dot_tn(_bf(kv * kd), vb)

    return pl.pallas_call(
        body, name="ret_fwd", grid=(2, H, nc + nl),
        in_specs=[pl.BlockSpec((C, DK), lambda d, h, t: (blk(d, t), h)),
                  pl.BlockSpec((C, DK), lambda d, h, t: (blk(d, t), H + h)),
                  pl.BlockSpec((C, DV), lambda d, h, t: (blk(d, t), H + h)),
                  pl.BlockSpec((1, 1, 1, 128), lambda d, h, t: (d, h, 0, 0))],
        out_specs=[pl.BlockSpec((1, C, DV), lambda d, h, t: (d, blk(d, t), h)),
                   pl.BlockSpec((1, 1, 1, DK, DV), lambda d, h, t: (d, h, jnp.maximum(t - nc, 0), 0, 0))],
        out_shape=[SDS((2, R, H * DV), F32), SDS((2, H, nl, DK, DV), F32)],
        scratch_shapes=[pltpu.VMEM((DK, DV), F32)],
        compiler_params=_params(3),
    )(proj, proj, proj, lgt)


def _ret_bwd(proj, lgt, states, do, s_len, l_len):
    R = proj.shape[0]
    C, H, DK, DV = RET_CHUNK, RET_HEADS, RET_DK, RET_DV
    nl, nc = s_len // C, l_len // C
    blk = _ret_specs(nl, nc, False)
    last = nl + nc - 1

    def body(q_ref, k_ref, v_ref, lg_ref, st_ref, do_ref, dq_ref, dk_ref, dv_ref, dlg_ref, dr_s, dl_s):
        d, t = pl.program_id(0), pl.program_id(2)
        x = lg_ref[0, 0]
        lg = jnp.max(_log_sigmoid(x), axis=-1, keepdims=True)

        @pl.when(t == 0)
        def _():
            dr_s[...] = jnp.zeros_like(dr_s)
            dl_s[...] = jnp.zeros_like(dl_s)

        @pl.when(t < nl)
        def _():
            li, diff, mask, qd, kd, cd = _ret_decays(d, lg)
            qv, kv, vv, dov = q_ref[...], k_ref[...] * K_SCALE, v_ref[...], do_ref[...]
            qb, kb, vb, dob = _bf(qv), _bf(kv), _bf(vv), _bf(dov)
            r, drn = st_ref[0, 0, 0], dr_s[...]
            rb, drb = _bf(r), _bf(drn)
            p = _dot_nt(qb, kb)
            dp = _dot_nt(dob, vb) * mask
            dpb = _bf(dp)
            doq = _bf(dov * qd)
            dq_inter = _dot_nt(doq, rb)
            dk_state = kd * _dot_nt(vb, drb)
            dq_ref[0] = _dot(dpb, kb) + dq_inter
            dk_ref[0] = (_dot_tn(dpb, qb) + dk_state) * K_SCALE
            dv_ref[0] = _dot_tn(_bf(p * mask), dob) + _dot(_bf(kv * kd), drb)
            dr_s[...] = cd * drn + _dot_tn(qb, doq)
            dl_s[...] += (jnp.sum(dp * p * diff) + jnp.sum((li + 1.0) * qv * dq_inter)
                          + jnp.sum((C - 1.0 - li) * kv * dk_state) + C * jnp.sum(cd * r * drn))

        @pl.when(t >= nl)
        def _():
            e, w = _ctx_weights(d, t - nl, lg, l_len)
            kv, vb, drb = k_ref[...] * K_SCALE, _bf(v_ref[...]), _bf(dr_s[...])
            dkc = w * _dot_nt(vb, drb)
            dq_ref[0] = jnp.zeros((C, DK), F32)
            dk_ref[0] = dkc * K_SCALE
            dv_ref[0] = _dot(_bf(kv * w), drb)
            dl_s[...] += jnp.sum(e * kv * dkc)

        @pl.when(t == last)
        def _():
            dlg_ref[0, 0] = dl_s[...] * (1.0 / (1.0 + jnp.exp(x)))

    return pl.pallas_call(
        body, name="ret_bwd", grid=(2, H, nl + nc),
        in_specs=[pl.BlockSpec((C, DK), lambda d, h, t: (blk(d, t), h)),
                  pl.BlockSpec((C, DK), lambda d, h, t: (blk(d, t), H + h)),
                  pl.BlockSpec((C, DV), lambda d, h, t: (blk(d, t), H + h)),
                  pl.BlockSpec((1, 1, 1, 128), lambda d, h, t: (d, h, 0, 0)),
                  pl.BlockSpec((1, 1, 1, DK, DV), lambda d, h, t: (d, h, jnp.maximum(nl - 1 - t, 0), 0, 0)),
                  pl.BlockSpec((C, DV), lambda d, h, t: (blk(d, t), h))],
        out_specs=[pl.BlockSpec((1, C, DK), lambda d, h, t: (d, blk(d, t), h)),
                   pl.BlockSpec((1, C, DK), lambda d, h, t: (d, blk(d, t), h)),
                   pl.BlockSpec((1, C, DV), lambda d, h, t: (d, blk(d, t), h)),
                   pl.BlockSpec((1, 1, 1, 128), lambda d, h, t: (d, h, 0, 0))],
        out_shape=[SDS((2, R, H * DK), F32), SDS((2, R, H * DK), F32), SDS((2, R, H * DV), F32),
                   SDS((2, H, 1, 128), F32)],
        scratch_shapes=[pltpu.VMEM((DK, DV), F32), pltpu.VMEM((1, 128), F32)],
        compiler_params=_params(3),
    )(proj, proj, proj, lgt, states, do)


def _readout_fwd(o2, proj, gn_w):
    R = proj.shape[0]
    H, DV = RET_HEADS, RET_DV
    W = H * DV
    T = ROW_TILE

    def body(o_ref, g_ref, w_ref, out_ref):
        for hh in range(H):
            cols = pl.ds(hh * DV, DV)
            y = o_ref[0, :, cols] + o_ref[1, :, cols]
            yc = y - jnp.mean(y, axis=-1, keepdims=True)
            yn = yc * lax.rsqrt(jnp.mean(yc * yc, axis=-1, keepdims=True) + EPS) * w_ref[:, cols]
            g = g_ref[:, cols]
            out_ref[:, cols] = _bf(g * _sigmoid(g) * yn)

    return pl.pallas_call(
        body, name="readout_fwd", grid=(R // T,),
        in_specs=[pl.BlockSpec((2, T, W), lambda i: (0, i, 0)), pl.BlockSpec((T, W), lambda i: (i, 2)),
                  pl.BlockSpec((1, W), lambda i: (0, 0))],
        out_specs=pl.BlockSpec((T, W), lambda i: (i, 0)),
        out_shape=SDS((R, W), BF16), compiler_params=_params(1),
    )(o2, proj, gn_w)


def _readout_bwd(o2, proj, gn_w, dgated):
    R = proj.shape[0]
    H, DV = RET_HEADS, RET_DV
    W = H * DV
    T = ROW_TILE

    def body(o_ref, g_ref, w_ref, d_ref, do_ref, dg_ref, dw_ref):
        i = pl.program_id(0)

        @pl.when(i == 0)
        def _():
            dw_ref[...] = jnp.zeros_like(dw_ref)

        for hh in range(H):
            cols = pl.ds(hh * DV, DV)
            y = o_ref[0, :, cols] + o_ref[1, :, cols]
            yc = y - jnp.mean(y, axis=-1, keepdims=True)
            rstd = lax.rsqrt(jnp.mean(yc * yc, axis=-1, keepdims=True) + EPS)
            yn0 = yc * rstd
            wv = w_ref[:, cols]
            g = g_ref[:, cols]
            s = _sigmoid(g)
            dgt = d_ref[:, cols]
            dyn = dgt * (g * s)
            dg_ref[:, cols] = _bf(dgt * (yn0 * wv) * (s * (1.0 + g * (1.0 - s))))
            dw_ref[:, cols] += jnp.sum(dyn * yn0, axis=0, keepdims=True)
            a = dyn * wv
            do_ref[:, cols] = rstd * (a - jnp.mean(a, axis=-1, keepdims=True)
                                      - yn0 * jnp.mean(a * yn0, axis=-1, keepdims=True))

    return pl.pallas_call(
        body, name="readout_bwd", grid=(R // T,),
        in_specs=[pl.BlockSpec((2, T, W), lambda i: (0, i, 0)), pl.BlockSpec((T, W), lambda i: (i, 2)),
                  pl.BlockSpec((1, W), lambda i: (0, 0)), pl.BlockSpec((T, W), lambda i: (i, 0))],
        out_specs=[pl.BlockSpec((T, W), lambda i: (i, 0)), pl.BlockSpec((T, W), lambda i: (i, 0)),
                   pl.BlockSpec((1, W), lambda i: (0, 0))],
        out_shape=[SDS((R, W), F32), SDS((R, W), BF16), SDS((1, W), F32)],
        compiler_params=_params(1),
    )(o2, proj, gn_w, dgated)


def _ret_dproj(dq2, dk2, dv2, dg):
    R = dg.shape[0]
    NQ, NV = RET_HEADS * RET_DK, RET_HEADS * RET_DV
    T = ROW_TILE

    def body(dq_ref, dk_ref, dv_ref, dg_ref, o_ref):
        o_ref[:, pl.ds(0, NQ)] = _bf(dq_ref[0] + dq_ref[1])
        o_ref[:, pl.ds(NQ, NQ)] = _bf(dk_ref[0] + dk_ref[1])
        o_ref[:, pl.ds(2 * NQ, NV)] = _bf(dv_ref[0] + dv_ref[1])
        o_ref[:, pl.ds(2 * NQ + NV, NV)] = dg_ref[...]

    return pl.pallas_call(
        body, name="ret_dproj", grid=(R // T,),
        in_specs=[pl.BlockSpec((2, T, NQ), lambda i: (0, i, 0)), pl.BlockSpec((2, T, NQ), lambda i: (0, i, 0)),
                  pl.BlockSpec((2, T, NV), lambda i: (0, i, 0)), pl.BlockSpec((T, NV), lambda i: (i, 0))],
        out_specs=pl.BlockSpec((T, 2 * NQ + 2 * NV), lambda i: (i, 0)),
        out_shape=SDS((R, 2 * NQ + 2 * NV), BF16), compiler_params=_params(1),
    )(dq2, dk2, dv2, dg)


def _silu(v):
    return v * _sigmoid(v)


def _ada_fwd(c_rows, ada_w, ada_b_shard):
    depth, D, cols = ada_w.shape

    def body(c_ref, w_ref, b_ref, o_ref):
        o_ref[0] = _dot(_bf(_silu(c_ref[...])), _bf(w_ref[0])) + b_ref[0]

    return pl.pallas_call(
        body, name="ada_fwd", grid=(depth,),
        in_specs=[pl.BlockSpec((16, D), lambda i: (0, 0)), pl.BlockSpec((1, D, cols), lambda i: (i, 0, 0)),
                  pl.BlockSpec((1, 1, cols), lambda i: (i, 0, 0))],
        out_specs=pl.BlockSpec((1, 16, cols), lambda i: (i, 0, 0)),
        out_shape=SDS((depth, 16, cols), F32), compiler_params=_params(1),
    )(c_rows, ada_w, ada_b_shard)


def _ada_bwd(c_rows, ada_w, d_lat, d_ctx):
    depth, D, cols = ada_w.shape

    def body(c_ref, w_ref, dl_ref, dc_ref, dw_ref, pc_ref):
        i = pl.program_id(0)
        cv = c_ref[...]
        a = _silu(cv)
        dcs = jnp.broadcast_to(jnp.sum(dc_ref[0], axis=0, keepdims=True), (8, cols))
        dw_ref[0] = _dot_tn(_bf(a[0:8]), _bf(dl_ref[0])) + _dot_tn(_bf(a[8:16]), _bf(dcs))

        @pl.when(i == 0)
        def _():
            pc_ref[...] = jnp.zeros_like(pc_ref)

        pc_ref[...] += _dot_nt(_bf(dcs), _bf(w_ref[0]))

        @pl.when(i == depth - 1)
        def _():
            cc = c_ref[pl.ds(8, 1), :]
            s = _sigmoid(cc)
            pc_ref[...] = pc_ref[...] * (s * (1.0 + cc * (1.0 - s)))

    return pl.pallas_call(
        body, name="ada_bwd", grid=(depth,),
        in_specs=[pl.BlockSpec((16, D), lambda i: (0, 0)), pl.BlockSpec((1, D, cols), lambda i: (i, 0, 0)),
                  pl.BlockSpec((1, 8, cols), lambda i: (i, 0, 0)), pl.BlockSpec((1, 8, cols), lambda i: (i, 0, 0))],
        out_specs=[pl.BlockSpec((1, D, cols), lambda i: (i, 0, 0)), pl.BlockSpec((8, D), lambda i: (0, 0))],
        out_shape=[SDS((depth, D, cols), F32), SDS((8, D), F32)], compiler_params=_params(1),
    )(c_rows, ada_w, d_lat, d_ctx)


def _adamw(w, g, m, v, name):
    shape = w.shape
    n = g.shape[0]
    cols = shape[-1]
    rows = w.size // cols
    tr = _pick(rows, 512, 8) if rows * cols * 4 > (1 << 20) else rows
    spec = pl.BlockSpec((tr, cols), lambda i: (i, 0))

    def body(w_ref, g_ref, m_ref, v_ref, go_ref, d_ref, mo_ref, vo_ref):
        gs = g_ref[0].astype(F32)
        for k in range(1, n):
            gs = gs + g_ref[k].astype(F32)
        mn = ADAM_B1 * m_ref[...] + (1.0 - ADAM_B1) * gs
        vn = ADAM_B2 * v_ref[...] + (1.0 - ADAM_B2) * jnp.square(gs)
        m_hat = mn / (1.0 - ADAM_B1 ** ADAM_STEP)
        v_hat = vn / (1.0 - ADAM_B2 ** ADAM_STEP)
        go_ref[...] = gs
        d_ref[...] = -ADAM_LR * (m_hat / (jnp.sqrt(v_hat) + ADAM_EPS) + ADAM_WD * w_ref[...])
        mo_ref[...] = mn
        vo_ref[...] = vn

    outs = pl.pallas_call(
        body, name=name, grid=(rows // tr,),
        in_specs=[spec, pl.BlockSpec((n, tr, cols), lambda i: (0, i, 0)), spec, spec],
        out_specs=[spec] * 4, out_shape=[SDS((rows, cols), F32)] * 4, compiler_params=_params(1),
    )(w.reshape(rows, cols), g.reshape(n, rows, cols), m.reshape(rows, cols), v.reshape(rows, cols))
    return tuple(o.reshape(shape) for o in outs)


def _sum_slots(recv):
    n, rows, cols = recv.shape
    tr = _pick(rows, 256, 16)

    def body(r_ref, o_ref):
        acc = r_ref[0].astype(F32)
        for k in range(1, n):
            acc = acc + r_ref[k].astype(F32)
        o_ref[...] = acc

    return pl.pallas_call(
        body, name="sum_slots", grid=(rows // tr,),
        in_specs=[pl.BlockSpec((n, tr, cols), lambda i: (0, i, 0))],
        out_specs=pl.BlockSpec((tr, cols), lambda i: (i, 0)),
        out_shape=SDS((rows, cols), F32), compiler_params=_params(1),
    )(recv)


def _position():
    return lax.axis_index("x"), lax.axis_index("y"), lax.axis_index("c")


def _peer(k, x, y, c):
    return (1 - x if k & 4 else x, 1 - y if k & 2 else y, 1 - c if k & 1 else c)


def _index(pos):
    return 4 * pos[0] + 2 * pos[1] + pos[2]


def _gather_small(v, name):
    rows, lanes = v.shape

    def body(x_ref, out_ref, send_sems, recv_sems, local_sem):
        me = _position()
        mine = pltpu.make_async_copy(x_ref, out_ref.at[_index(me)], local_sem)
        mine.start()

        def copy(k, slot):
            return pltpu.make_async_remote_copy(
                src_ref=x_ref, dst_ref=out_ref.at[slot], send_sem=send_sems.at[k - 1],
                recv_sem=recv_sems.at[k - 1], device_id=_peer(k, *me), device_id_type=MESH)

        sends = [copy(k, _index(me)) for k in range(1, N_DEV)]
        for cp in sends:
            cp.start()
        for k in range(1, N_DEV):
            copy(k, _index(_peer(k, *me))).wait_recv()
        for cp in sends:
            cp.wait_send()
        mine.wait()

    return pl.pallas_call(
        body, name=name, out_shape=SDS((N_DEV, rows, lanes), v.dtype),
        in_specs=[pl.BlockSpec(memory_space=pltpu.VMEM)],
        out_specs=pl.BlockSpec(memory_space=pltpu.VMEM),
        scratch_shapes=[pltpu.SemaphoreType.DMA((N_DEV - 1,)), pltpu.SemaphoreType.DMA((N_DEV - 1,)),
                        pltpu.SemaphoreType.DMA],
        compiler_params=pltpu.CompilerParams(vmem_limit_bytes=VMEM_LIMIT_V7X),
    )(v)


def _gather_big(v, name):
    rows, cols = v.shape

    def body(x_ref, out_ref, send_sems, recv_sems, local_sem):
        x, y, c = _position()
        me, sibling = (x, y, c), (x, y, 1 - c)
        chips = [(1 - x, y), (x, 1 - y), (1 - x, 1 - y)]

        def copy(k, block, to, src=None):
            slot = out_ref.at[_index(block)]
            return pltpu.make_async_remote_copy(
                src_ref=slot if src is None else src, dst_ref=slot, send_sem=send_sems.at[k],
                recv_sem=recv_sems.at[k], device_id=to, device_id_type=MESH)

        mine = pltpu.make_async_copy(x_ref, out_ref.at[_index(me)], local_sem)
        mine.start()
        first = [copy(0, me, sibling, src=x_ref)]
        first += [copy(1 + j, me, (*chip, c), src=x_ref) for j, chip in enumerate(chips)]
        for cp in first:
            cp.start()
        passed = [copy(4 + j, (*chip, c), sibling) for j, chip in enumerate(chips)]
        for j, chip in enumerate(chips):
            copy(1 + j, (*chip, c), me).wait_recv()
            passed[j].start()
        copy(0, sibling, me).wait_recv()
        for j, chip in enumerate(chips):
            copy(4 + j, (*chip, 1 - c), me).wait_recv()
        for cp in first + passed:
            cp.wait_send()
        mine.wait()

    return pl.pallas_call(
        body, name=name, out_shape=SDS((N_DEV, rows, cols), v.dtype),
        in_specs=[pl.BlockSpec(memory_space=pl.ANY)],
        out_specs=pl.BlockSpec(memory_space=pl.ANY),
        scratch_shapes=[pltpu.SemaphoreType.DMA((N_DEV - 1,)), pltpu.SemaphoreType.DMA((N_DEV - 1,)),
                        pltpu.SemaphoreType.DMA],
    )(v)


def _exchange(p, name):
    n, rows, cols = p.shape

    def body(p_ref, out_ref, send_sems, recv_sems, local_sem):
        me = _position()
        mine = pltpu.make_async_copy(p_ref.at[_index(me)], out_ref.at[_index(me)], local_sem)
        mine.start()

        def copy(k, src_slot, dst_slot):
            return pltpu.make_async_remote_copy(
                src_ref=p_ref.at[src_slot], dst_ref=out_ref.at[dst_slot], send_sem=send_sems.at[k - 1],
                recv_sem=recv_sems.at[k - 1], device_id=_peer(k, *me), device_id_type=MESH)

        sends = [copy(k, _index(_peer(k, *me)), _index(me)) for k in range(1, N_DEV)]
        for cp in sends:
            cp.start()
        for k in range(1, N_DEV):
            copy(k, _index(me), _index(_peer(k, *me))).wait_recv()
        for cp in sends:
            cp.wait_send()
        mine.wait()

    return pl.pallas_call(
        body, name=name, out_shape=SDS((n, rows, cols), p.dtype),
        in_specs=[pl.BlockSpec(memory_space=pl.ANY)],
        out_specs=pl.BlockSpec(memory_space=pl.ANY),
        scratch_shapes=[pltpu.SemaphoreType.DMA((N_DEV - 1,)), pltpu.SemaphoreType.DMA((N_DEV - 1,)),
                        pltpu.SemaphoreType.DMA],
    )(p)


def _pack_rows(arrays, lanes, dtype):
    flat = jnp.concatenate([a.astype(dtype).reshape(-1) for a in arrays])
    pad = (-flat.size) % (16 * lanes)
    if pad:
        flat = jnp.concatenate([flat, jnp.zeros((pad,), dtype)])
    return flat.reshape(-1, lanes)


def _unpack_rows(packed, shapes):
    n = packed.shape[0]
    flat = packed.reshape(n, -1)
    out, off = [], 0
    for shp in shapes:
        size = math.prod(shp)
        out.append(flat[:, off:off + size].reshape((n,) + tuple(shp)))
        off += size
    return out


def _unshard(g8, axis):
    moved = jnp.moveaxis(g8, 0, axis)
    shp = list(moved.shape)
    shp[axis:axis + 2] = [shp[axis] * shp[axis + 1]]
    return moved.reshape(shp)


def _split8(full, axis):
    shp = list(full.shape)
    shp[axis:axis + 1] = [N_DEV, shp[axis] // N_DEV]
    return jnp.moveaxis(full.reshape(shp), axis, 0)


def _my_shard(g, axis, me):
    size = g.shape[axis + 1] // N_DEV
    return lax.dynamic_slice_in_dim(g, me * size, size, axis=axis + 1)


BIG_WEIGHTS = (("ffn_w_up", 2), ("ffn_w_down", 1), ("attn_w_qkv", 2), ("attn_w_o", 1),
               ("ret_w_in", 2), ("ret_w_out", 1), ("pool_w", 2))
SMALL_SHARDED = (("norm_w", 2), ("pool_b", 1), ("pool_scale", 1), ("ret_gn_w", 1), ("ffn_conv_w", 2))
REPLICATED = ("ada_b", "attn_q_gain", "attn_k_gain", "ret_decay_logit", "ffn_conv_b")
WEIGHT_ORDER = ("c_ctx", "ada_w", "ada_b", "norm_w", "pool_w", "pool_b", "pool_scale", "attn_w_qkv",
                "attn_q_gain", "attn_k_gain", "attn_w_o", "ret_w_in", "ret_decay_logit", "ret_gn_w",
                "ret_w_out", "ffn_w_up", "ffn_conv_w", "ffn_conv_b", "ffn_w_down")


def _local_step(x0, target, mods, P, s_len, l_len):
    n_lat = s_len // ROW_TILE
    nw = P["norm_w"]
    lgt = jnp.broadcast_to(P["ret_decay_logit"][0][:, :, None, None], (2, RET_HEADS, 1, 128))
    cos, sin = _rope_tables(s_len, l_len)
    h_dtype = [F32 if i % 3 == 0 else BF16 for i in range(DEPTH)]
    saved = []
    X = x0
    h = _res_norm(X, None, None, 0, nw[0, 0], mods[0], 0, h_dtype[0], n_lat, "norm_first")
    for i in range(DEPTH):
        kind, j, mod = i % 3, i // 3, mods[i]
        sv = {"X": X, "h": h}
        if kind == 0:
            y = _pool_fwd(h, P["pool_w"][j], P["pool_b"][j:j + 1], P["pool_scale"][j:j + 1],
                          n_lat, s_len, l_len, f"pool_fwd{i}")
        elif kind == 1:
            qkv = _mm(h, P["attn_w_qkv"][j], "nn", F32, f"qkv{i}")
            q, k, v = _qk_prep_fwd(qkv, P["attn_q_gain"][j:j + 1], P["attn_k_gain"][j:j + 1], cos, sin)
            o, lse = _flash_fwd(q, k, v, s_len, l_len)
            y = _mm(o, P["attn_w_o"][j], "nn", F32, f"attn_out{i}")
            sv.update(qkv=qkv, q=q, k=k, v=v, o=o, lse=lse)
        else:
            proj = _mm(h, P["ret_w_in"][j], "nn", F32, f"ret_in{i}")
            o2, states = _ret_fwd(proj, lgt, s_len, l_len)
            gated = _readout_fwd(o2, proj, P["ret_gn_w"][j:j + 1])
            y = _mm(gated, P["ret_w_out"][j], "nn", F32, f"ret_out{i}")
            sv.update(proj=proj, o2=o2, states=states, gated=gated)
        X1, h2 = _res_norm(X, y, mod, 0, nw[i, 1], mod, 1, BF16, n_lat, f"res_norm_mid{i}")
        u = _mm(h2, P["ffn_w_up"][i], "nn", F32, f"ffn_up{i}")
        gact = _conv_gate_fwd(u, P["ffn_conv_w"][i], P["ffn_conv_b"][i:i + 1], n_lat, f"conv_gate_fwd{i}")
        f = _mm(gact, P["ffn_w_down"][i], "nn", F32, f"ffn_down{i}")
        sv.update(y=y, X1=X1, h2=h2, u=u, gact=gact, f=f)
        saved.append(sv)
        if i + 1 < DEPTH:
            X, h = _res_norm(X1, f, mod, 1, nw[i + 1, 0], mods[i + 1], 0, h_dtype[i + 1], n_lat,
                             f"res_norm_end{i}")
        else:
            X = _res_norm(X1, f, mod, 1, None, None, 0, None, n_lat, "res_last")

    dX, loss = _loss_bwd(X, target, n_lat)
    G = {name: [None] * P[name].shape[0] for name in
         ("pool_w", "pool_b", "pool_scale", "attn_w_qkv", "attn_q_gain", "attn_k_gain", "attn_w_o", "ret_w_in",
          "ret_decay_logit", "ret_gn_w", "ret_w_out", "ffn_w_up", "ffn_conv_w", "ffn_conv_b", "ffn_w_down")}
    dnw = [[None, None] for _ in range(DEPTH)]
    dmods = [None] * DEPTH
    for i in reversed(range(DEPTH)):
        kind, j, mod, sv = i % 3, i // 3, mods[i], saved[i]
        df, dg2 = _gate_bwd(dX, sv["f"], mod, 1, BF16, n_lat, f"gate_bwd_ffn{i}")
        dgact = _mm(df, P["ffn_w_down"][i], "nt", F32, f"ffn_down_dx{i}")
        G["ffn_w_down"][i] = _mm(sv["gact"], df, "tn", F32, f"ffn_down_dw{i}")
        du, dcw, dcb = _conv_gate_bwd(sv["u"], dgact, P["ffn_conv_w"][i], P["ffn_conv_b"][i:i + 1], n_lat,
                                      f"conv_gate_bwd{i}")
        G["ffn_conv_w"][i], G["ffn_conv_b"][i] = dcw, dcb[0]
        dh2 = _mm(du, P["ffn_w_up"][i], "nt", F32, f"ffn_up_dx{i}")
        G["ffn_w_up"][i] = _mm(sv["h2"], du, "tn", F32, f"ffn_up_dw{i}")
        dX1, dnw[i][1], dsh2, dsc2 = _norm_bwd(dX, dh2, sv["X1"], nw[i, 1], mod, 1, n_lat, f"norm_bwd_ffn{i}")
        dy, dg1 = _gate_bwd(dX1, sv["y"], mod, 0, F32 if kind == 0 else BF16, n_lat, f"gate_bwd_mix{i}")
        h = sv["h"]
        if kind == 0:
            dh, dpw, dpb, dps = _pool_bwd(h, dy, P["pool_w"][j], P["pool_b"][j:j + 1], P["pool_scale"][j:j + 1],
                                          n_lat, s_len, l_len, f"pool_bwd{i}")
            G["pool_w"][j], G["pool_b"][j], G["pool_scale"][j] = dpw, dpb[0], dps[0]
        elif kind == 1:
            do = _mm(dy, P["attn_w_o"][j], "nt", F32, f"attn_out_dx{i}")
            G["attn_w_o"][j] = _mm(sv["o"], dy, "tn", F32, f"attn_out_dw{i}")
            dq, dk, dv = _flash_bwd(sv["q"], sv["k"], sv["v"], sv["o"], sv["lse"], do, s_len, l_len)
            dqkv, dqg, dkg = _qk_prep_bwd(sv["qkv"], dq, dk, dv, P["attn_q_gain"][j:j + 1],
                                          P["attn_k_gain"][j:j + 1], cos, sin)
            G["attn_q_gain"][j], G["attn_k_gain"][j] = dqg[0], dkg[0]
            dh = _mm(dqkv, P["attn_w_qkv"][j], "nt", F32, f"qkv_dx{i}")
            G["attn_w_qkv"][j] = _mm(h, dqkv, "tn", F32, f"qkv_dw{i}")
        else:
            dgated = _mm(dy, P["ret_w_out"][j], "nt", F32, f"ret_out_dx{i}")
            G["ret_w_out"][j] = _mm(sv["gated"], dy, "tn", F32, f"ret_out_dw{i}")
            do, dg, dgn = _readout_bwd(sv["o2"], sv["proj"], P["ret_gn_w"][j:j + 1], dgated)
            dq2, dk2, dv2, dlg = _ret_bwd(sv["proj"], lgt, sv["states"], do, s_len, l_len)
            dproj = _ret_dproj(dq2, dk2, dv2, dg)
            G["ret_gn_w"][j], G["ret_decay_logit"][j] = dgn[0], dlg[:, :, 0, 0]
            dh = _mm(dproj, P["ret_w_in"][j], "nt", F32, f"ret_in_dx{i}")
            G["ret_w_in"][j] = _mm(h, dproj, "tn", F32, f"ret_in_dw{i}")
        dX, dnw[i][0], dsh1, dsc1 = _norm_bwd(dX1, dh, sv["X"], nw[i, 0], mod, 0, n_lat, f"norm_bwd_mix{i}")
        dmods[i] = jnp.concatenate([dsh1, dsc1, dg1, dsh2, dsc2, dg2], axis=1)
    grads = {name: jnp.stack(parts) for name, parts in G.items()}
    grads["norm_w"] = jnp.stack([jnp.concatenate(pair, axis=0) for pair in dnw])
    return loss, dX, grads, jnp.stack(dmods)


def kernel(x, c, ctx, c_ctx, ada_w, ada_b, norm_w, pool_w, pool_b, pool_scale, attn_w_qkv, attn_q_gain,
           attn_k_gain, attn_w_o, ret_w_in, ret_decay_logit, ret_gn_w, ret_w_out, ffn_w_up, ffn_conv_w,
           ffn_conv_b, ffn_w_down, loss_target, m_c_ctx, m_ada_w, m_ada_b, m_norm_w, m_pool_w, m_pool_b,
           m_pool_scale, m_attn_w_qkv, m_attn_q_gain, m_attn_k_gain, m_attn_w_o, m_ret_w_in,
           m_ret_decay_logit, m_ret_gn_w, m_ret_w_out, m_ffn_w_up, m_ffn_conv_w, m_ffn_conv_b, m_ffn_w_down,
           v_c_ctx, v_ada_w, v_ada_b, v_norm_w, v_pool_w, v_pool_b, v_pool_scale, v_attn_w_qkv, v_attn_q_gain,
           v_attn_k_gain, v_attn_w_o, v_ret_w_in, v_ret_decay_logit, v_ret_gn_w, v_ret_w_out, v_ffn_w_up,
           v_ffn_conv_w, v_ffn_conv_b, v_ffn_w_down):
    A = dict(locals())
    me = _index(_position())
    s_len, D = x.shape[1], x.shape[2]
    l_len = ctx.shape[1]
    assert s_len % ROW_TILE == 0 and l_len % ROW_TILE == 0 and s_len % GRID_W == 0

    small = [A[n] for n, _ in SMALL_SHARDED]
    got = _gather_small(_pack_rows([c] + small, 128, F32), "gather_c_small")
    parts = _unpack_rows(got, [c.shape] + [a.shape for a in small])
    c_all = parts[0].reshape(N_DEV, D)
    P = {n: _unshard(g8, ax) for (n, ax), g8 in zip(SMALL_SHARDED, parts[1:])}

    c_rows = jnp.concatenate([c_all, c_ctx.reshape(1, D), jnp.zeros((7, D), F32)], axis=0)
    cols = ada_w.shape[2]
    ada_b_shard = lax.dynamic_slice_in_dim(ada_b, me * cols, cols, axis=1).reshape(DEPTH, 1, cols)
    mod_shard = _ada_fwd(c_rows, ada_w, ada_b_shard)
    got = _gather_small(mod_shard.reshape(-1, 128), "gather_mod").reshape(N_DEV, DEPTH, 16, cols)
    mod_lat = lax.dynamic_index_in_dim(got, me, axis=2, keepdims=False)
    mod_ctx = got[:, :, 8, :]
    mods = jnp.stack([jnp.moveaxis(mod_lat, 0, 1).reshape(DEPTH, 6, D),
                      jnp.moveaxis(mod_ctx, 0, 1).reshape(DEPTH, 6, D)], axis=1)

    big = [A[n].astype(BF16) for n, _ in BIG_WEIGHTS]
    rows = [b.size // D for b in big]
    got = _gather_big(jnp.concatenate([b.reshape(-1, D) for b in big], axis=0), "gather_weights")
    off = 0
    for (n, ax), b, r in zip(BIG_WEIGHTS, big, rows):
        P[n] = _unshard(got[:, off:off + r].reshape((N_DEV,) + b.shape), ax)
        off += r
    for n in REPLICATED:
        P[n] = A[n]

    x0 = jnp.concatenate([x[0], ctx[0]], axis=0)
    loss8, dx0, G, dmods = _local_step(x0, loss_target[0], mods, P, s_len, l_len)
    loss = lax.psum(loss8[0, 0], ("x", "y", "c"))
    grad_x = dx0[:s_len].reshape(x.shape)

    small_names = ["dmods"] + list(REPLICATED[1:]) + [n for n, _ in SMALL_SHARDED]
    small_parts = [dmods] + [G[n] for n in small_names[1:]]
    got = _gather_small(_pack_rows(small_parts, 128, F32), "gather_small_grads")
    S8 = dict(zip(small_names, _unpack_rows(got, [a.shape for a in small_parts])))

    dm = S8["dmods"].reshape(N_DEV, DEPTH, 2, 6 * D)
    dm_mine = lax.dynamic_slice_in_dim(dm, me * cols, cols, axis=3)
    g_ada_w, pc = _ada_bwd(c_rows, ada_w, jnp.moveaxis(dm_mine[:, :, 0], 0, 1), jnp.moveaxis(dm_mine[:, :, 1], 0, 1))
    pc8 = _gather_small(pc.reshape(-1, 128), "gather_c_ctx_grad").reshape(N_DEV, 8, D)

    send = jnp.concatenate([_split8(G[n], ax).astype(BF16).reshape(N_DEV, -1, D) for n, ax in BIG_WEIGHTS], axis=1)
    gsum = _sum_slots(_exchange(send, "exchange_grads"))

    g_in = {"c_ctx": pc8[:, 0, :], "ada_w": g_ada_w[None],
            "ada_b": jnp.moveaxis(dm, 2, 1).reshape(2 * N_DEV, DEPTH, 6 * D)}
    for n in REPLICATED[1:]:
        g_in[n] = S8[n]
    for n, ax in SMALL_SHARDED:
        g_in[n] = _my_shard(S8[n], ax, me)
    off = 0
    for (n, ax), b, r in zip(BIG_WEIGHTS, big, rows):
        g_in[n] = gsum[off:off + r].reshape((1,) + b.shape)
        off += r
    res = {n: _adamw(A[n], g_in[n], A["m_" + n], A["v_" + n], "adamw_" + n) for n in WEIGHT_ORDER}
    outs = [loss, grad_x]
    for slot in range(4):
        outs += [res[n][slot] for n in WEIGHT_ORDER]
    return tuple(outs)
```

```python
import functools
import math

import jax
import jax.numpy as jnp
from jax import lax
from jax.experimental import pallas as pl
from jax.experimental.pallas import tpu as pltpu

F32 = jnp.float32
BF16 = jnp.bfloat16
SDS = jax.ShapeDtypeStruct
MESH = pl.DeviceIdType.MESH

N_DEV = 8
EPS = 1e-6
DEPTH = 4
GRID_W = 64
POOL_WINDOWS = (2, 4, 8, 16)
N_HEADS = 8
N_KV = 2
HEAD_DIM = 128
ROPE_THETA = 10000.0
RET_HEADS = 4
RET_DK = 256
RET_DV = 512
RET_CHUNK = 128
ADAM_LR = 0.001
ADAM_B1 = 0.9
ADAM_B2 = 0.999
ADAM_EPS = 1e-08
ADAM_WD = 0.01
ADAM_STEP = 10

ROW_TILE = 256
HALO = 8
VMEM_LIMIT_V7X = 56 * 1024 * 1024


def _params(n_axes=0):
    sem = ("arbitrary",) * n_axes if n_axes else None
    return pltpu.CompilerParams(dimension_semantics=sem, vmem_limit_bytes=VMEM_LIMIT_V7X)


def _pick(n, cap, mult):
    best = None
    for d in range(mult, min(n, cap) + 1, mult):
        if n % d == 0:
            best = d
    return best if best is not None else n


def _dot(a, b):
    return jnp.dot(a, b, preferred_element_type=F32)


def _dot_nt(a, b):
    return lax.dot_general(a, b, (((1,), (1,)), ((), ())), preferred_element_type=F32)


def _dot_tn(a, b):
    return lax.dot_general(a, b, (((0,), (0,)), ((), ())), preferred_element_type=F32)


def _bf(v):
    return v.astype(BF16)


def _sigmoid(v):
    return 0.5 * jnp.tanh(0.5 * v) + 0.5


MM_VMEM_BUDGET = 40 * 1024 * 1024
MM_STEP_BYTES = 1 << 20


def _divisors(n, mult, cap):
    return [d for d in range(mult, min(n, cap) + 1, mult) if n % d == 0] or [n]


def _mm_tiles(mode, M, N, K, a_item, b_item, o_item):
    best = None
    for tm in _divisors(M, 128 if mode == "tn" else 16, 2816):
        for tn in _divisors(N, 128, 2048):
            for tk in _divisors(K, 16 if mode == "tn" else 128, 2816):
                ni, nj, nk = M // tm, N // tn, K // tk
                vmem = 2 * (tm * tk * a_item + tk * tn * b_item + tm * tn * o_item) + tm * tn * 4
                if vmem > MM_VMEM_BUDGET:
                    continue
                a_reads = 1 if nk == 1 else nj
                b_reads = 1 if (nk == 1 and nj == 1) else ni
                cost = (M * K * a_item * a_reads + K * N * b_item * b_reads + M * N * o_item
                        + ni * nj * nk * MM_STEP_BYTES + (nk - 1) * M * N)
                if best is None or cost < best[0]:
                    best = (cost, tm, tn, tk)
    return best[1:]


def _mm(a, b, mode, out_dtype, name):
    if mode == "nn":
        (M, K), (K2, N) = a.shape, b.shape
    elif mode == "nt":
        (M, K), (N, K2) = a.shape, b.shape
    else:
        (K, M), (K2, N) = a.shape, b.shape
    assert K == K2, (a.shape, b.shape, mode)
    tm, tn, tk = _mm_tiles(mode, M, N, K, a.dtype.itemsize, b.dtype.itemsize, jnp.dtype(out_dtype).itemsize)
    nk = K // tk
    if mode == "nn":
        a_spec = pl.BlockSpec((tm, tk), lambda i, j, k: (i, k))
        b_spec = pl.BlockSpec((tk, tn), lambda i, j, k: (k, j))
    elif mode == "nt":
        a_spec = pl.BlockSpec((tm, tk), lambda i, j, k: (i, k))
        b_spec = pl.BlockSpec((tn, tk), lambda i, j, k: (j, k))
    else:
        a_spec = pl.BlockSpec((tk, tm), lambda i, j, k: (k, i))
        b_spec = pl.BlockSpec((tk, tn), lambda i, j, k: (k, j))
    dot = {"nn": _dot, "nt": _dot_nt, "tn": _dot_tn}[mode]

    def body(a_ref, b_ref, o_ref, acc_ref):
        part = dot(_bf(a_ref[...]), _bf(b_ref[...]))
        if nk == 1:
            o_ref[...] = part.astype(out_dtype)
        else:
            k = pl.program_id(2)

            @pl.when(k == 0)
            def _():
                acc_ref[...] = part

            @pl.when(k > 0)
            def _():
                acc_ref[...] += part

            @pl.when(k == nk - 1)
            def _():
                o_ref[...] = acc_ref[...].astype(out_dtype)

    return pl.pallas_call(
        body, name=name, grid=(M // tm, N // tn, nk),
        in_specs=[a_spec, b_spec],
        out_specs=pl.BlockSpec((tm, tn), lambda i, j, k: (i, j)),
        out_shape=SDS((M, N), out_dtype),
        scratch_shapes=[pltpu.VMEM((tm, tn), F32)],
        compiler_params=_params(3),
    )(a, b)


def _seg_spec(n_lat, d):
    return pl.BlockSpec((1, 6, d), lambda i: ((i >= n_lat).astype(jnp.int32), 0, 0))


def _seg_acc_spec(n_lat, d):
    return pl.BlockSpec((1, 1, d), lambda i: ((i >= n_lat).astype(jnp.int32), 0, 0))


def _res_norm(x, y, gmod, gk, nw, nmod, nk, h_dtype, n_lat, name):
    R, D = x.shape
    has_res, has_norm = y is not None, nw is not None
    row = pl.BlockSpec((ROW_TILE, D), lambda i: (i, 0))
    vec = pl.BlockSpec((1, D), lambda i: (0, 0))
    ins, specs, outs, ospecs = [x], [row], [], []
    if has_res:
        ins += [y, gmod]
        specs += [row, _seg_spec(n_lat, D)]
        outs.append(SDS((R, D), F32))
        ospecs.append(row)
    if has_norm:
        ins += [nw.reshape(1, D), nmod]
        specs += [vec, _seg_spec(n_lat, D)]
        outs.append(SDS((R, D), h_dtype))
        ospecs.append(row)

    def body(*refs):
        refs = list(refs)
        z = refs.pop(0)[...]
        if has_res:
            y_ref, g_ref = refs.pop(0), refs.pop(0)
            z = z + g_ref[0, pl.ds(3 * gk + 2, 1), :] * y_ref[...].astype(F32)
        if has_norm:
            nw_ref, m_ref = refs.pop(0), refs.pop(0)
        if has_res:
            refs.pop(0)[...] = z
        if has_norm:
            r = lax.rsqrt(jnp.mean(z * z, axis=-1, keepdims=True) + EPS)
            h = (z * r) * nw_ref[...]
            h = h * (1.0 + m_ref[0, pl.ds(3 * nk + 1, 1), :]) + m_ref[0, pl.ds(3 * nk, 1), :]
            refs.pop(0)[...] = h.astype(h_dtype)

    res = pl.pallas_call(
        body, name=name, grid=(R // ROW_TILE,), in_specs=specs, out_specs=ospecs,
        out_shape=outs, compiler_params=_params(1),
    )(*ins)
    return res if len(res) > 1 else res[0]


def _gate_bwd(dz, y, mod, k, out_dtype, n_lat, name):
    R, D = dz.shape
    row = pl.BlockSpec((ROW_TILE, D), lambda i: (i, 0))

    def body(dz_ref, y_ref, m_ref, dy_ref, dg_ref):
        i = pl.program_id(0)
        dzv = dz_ref[...]
        dy_ref[...] = (m_ref[0, pl.ds(3 * k + 2, 1), :] * dzv).astype(out_dtype)

        @pl.when((i == 0) | (i == n_lat))
        def _():
            dg_ref[...] = jnp.zeros_like(dg_ref)

        dg_ref[0] += jnp.sum(dzv * y_ref[...].astype(F32), axis=0, keepdims=True)

    return pl.pallas_call(
        body, name=name, grid=(R // ROW_TILE,),
        in_specs=[row, row, _seg_spec(n_lat, D)],
        out_specs=[row, _seg_acc_spec(n_lat, D)],
        out_shape=[SDS((R, D), out_dtype), SDS((2, 1, D), F32)],
        compiler_params=_params(1),
    )(dz, y, mod)


def _norm_bwd(dz, dh, x, nw, mod, k, n_lat, name):
    R, D = x.shape
    row = pl.BlockSpec((ROW_TILE, D), lambda i: (i, 0))
    vec = pl.BlockSpec((1, D), lambda i: (0, 0))

    def body(dz_ref, dh_ref, x_ref, nw_ref, m_ref, dx_ref, dnw_ref, dsh_ref, dsc_ref):
        i = pl.program_id(0)
        xv = x_ref[...]
        dhv = dh_ref[...].astype(F32)
        nwv = nw_ref[...]
        sc1 = 1.0 + m_ref[0, pl.ds(3 * k + 1, 1), :]
        r = lax.rsqrt(jnp.mean(xv * xv, axis=-1, keepdims=True) + EPS)
        xhat = xv * r
        a = dhv * (nwv * sc1)
        dx_ref[...] = dz_ref[...] + r * (a - xhat * jnp.mean(a * xhat, axis=-1, keepdims=True))

        @pl.when(i == 0)
        def _():
            dnw_ref[...] = jnp.zeros_like(dnw_ref)

        @pl.when((i == 0) | (i == n_lat))
        def _():
            dsh_ref[...] = jnp.zeros_like(dsh_ref)
            dsc_ref[...] = jnp.zeros_like(dsc_ref)

        dnw_ref[...] += jnp.sum(dhv * xhat, axis=0, keepdims=True) * sc1
        dsh_ref[0] += jnp.sum(dhv, axis=0, keepdims=True)
        dsc_ref[0] += jnp.sum(dhv * xhat, axis=0, keepdims=True) * nwv

    return pl.pallas_call(
        body, name=name, grid=(R // ROW_TILE,),
        in_specs=[row, row, row, vec, _seg_spec(n_lat, D)],
        out_specs=[row, vec, _seg_acc_spec(n_lat, D), _seg_acc_spec(n_lat, D)],
        out_shape=[SDS((R, D), F32), SDS((1, D), F32), SDS((2, 1, D), F32), SDS((2, 1, D), F32)],
        compiler_params=_params(1),
    )(dz, dh, x, nw.reshape(1, D), mod)


def _loss_bwd(xf, target, n_lat):
    R, D = xf.shape
    row = pl.BlockSpec((ROW_TILE, D), lambda i: (i, 0))
    tgt = pl.BlockSpec((ROW_TILE, D), lambda i: (jnp.minimum(i, n_lat - 1), 0))

    def body(x_ref, t_ref, dx_ref, loss_ref):
        i = pl.program_id(0)
        e = jnp.where(i < n_lat, x_ref[...] - t_ref[...], 0.0)
        dx_ref[...] = e * (1.0 / D)

        @pl.when(i == 0)
        def _():
            loss_ref[...] = jnp.zeros_like(loss_ref)

        loss_ref[...] += 0.5 * jnp.sum(jnp.mean(e * e, axis=-1, keepdims=True))

    return pl.pallas_call(
        body, name="loss_bwd", grid=(R // ROW_TILE,),
        in_specs=[row, tgt],
        out_specs=[row, pl.BlockSpec((8, 128), lambda i: (0, 0))],
        out_shape=[SDS((R, D), F32), SDS((8, 128), F32)],
        compiler_params=_params(1),
    )(xf, target)


def _halo_specs(n_tiles, width, tile=ROW_TILE):
    per = tile // HALO
    prev = pl.BlockSpec((HALO, width), lambda i: (jnp.maximum(i * per - 1, 0), 0))
    nxt = pl.BlockSpec((HALO, width), lambda i: (jnp.minimum((i + 1) * per, n_tiles * per - 1), 0))
    return prev, nxt


def _edge_flags(i, n_lat, n_tiles):
    first = (i == 0) | (i == n_lat)
    last = (i == n_lat - 1) | (i == n_tiles - 1)
    return first, last


def _conv_gate_fwd(u, conv_w, conv_b, n_lat, name):
    R, F2 = u.shape
    F = F2 // 2
    n_tiles = R // ROW_TILE
    T = ROW_TILE
    cw = _pick(F, 256, 128)
    row = pl.BlockSpec((T, F2), lambda i: (i, 0))
    prev, nxt = _halo_specs(n_tiles, F2)

    def body(u_ref, p_ref, n_ref, w_ref, b_ref, o_ref):
        i = pl.program_id(0)
        first, last = _edge_flags(i, n_lat, n_tiles)
        ridx = lax.broadcasted_iota(jnp.int32, (T, 1), 0)

        def conv(c0):
            cols = pl.ds(c0, cw)
            uv = u_ref[:, cols]
            pr = jnp.where(first, 0.0, p_ref[pl.ds(HALO - 1, 1), cols])
            nx = jnp.where(last, 0.0, n_ref[pl.ds(0, 1), cols])
            up = jnp.where(ridx == 0, pr, pltpu.roll(uv, 1, 0))
            un = jnp.where(ridx == T - 1, nx, pltpu.roll(uv, T - 1, 0))
            return (up * w_ref[pl.ds(0, 1), cols] + uv * w_ref[pl.ds(1, 1), cols]
                    + un * w_ref[pl.ds(2, 1), cols] + b_ref[:, cols])

        for c0 in range(0, F, cw):
            ca, cv = conv(c0), conv(F + c0)
            o_ref[:, pl.ds(c0, cw)] = (ca * _sigmoid(ca) * cv).astype(BF16)

    return pl.pallas_call(
        body, name=name, grid=(n_tiles,),
        in_specs=[row, prev, nxt, pl.BlockSpec((3, F2), lambda i: (0, 0)),
                  pl.BlockSpec((1, F2), lambda i: (0, 0))],
        out_specs=pl.BlockSpec((T, F), lambda i: (i, 0)),
        out_shape=SDS((R, F), BF16), compiler_params=_params(1),
    )(u, u, u, conv_w, conv_b)


def _conv_gate_bwd(u, dgact, conv_w, conv_b, n_lat, name):
    R, F2 = u.shape
    F = F2 // 2
    n_tiles = R // ROW_TILE
    T, N = ROW_TILE, ROW_TILE + 2 * HALO
    cw = _pick(F, 256, 128)
    rowu = pl.BlockSpec((T, F2), lambda i: (i, 0))
    rowg = pl.BlockSpec((T, F), lambda i: (i, 0))
    pu, nu = _halo_specs(n_tiles, F2)
    pg, ng = _halo_specs(n_tiles, F)

    def body(u_ref, pu_ref, nu_ref, g_ref, pg_ref, ng_ref, w_ref, b_ref, du_ref, dw_ref, db_ref):
        i = pl.program_id(0)
        first, last = _edge_flags(i, n_lat, n_tiles)

        @pl.when(i == 0)
        def _():
            dw_ref[...] = jnp.zeros_like(dw_ref)
            db_ref[...] = jnp.zeros_like(db_ref)

        def ext(t_ref, p_ref, n_ref, cols):
            pr = jnp.where(first, 0.0, p_ref[:, cols])
            nx = jnp.where(last, 0.0, n_ref[:, cols])
            return jnp.concatenate([pr, t_ref[:, cols], nx], axis=0)

        def conv(c0):
            cols = pl.ds(c0, cw)
            e = ext(u_ref, pu_ref, nu_ref, cols)
            up, un = pltpu.roll(e, 1, 0), pltpu.roll(e, N - 1, 0)
            c = (up * w_ref[pl.ds(0, 1), cols] + e * w_ref[pl.ds(1, 1), cols]
                 + un * w_ref[pl.ds(2, 1), cols] + b_ref[:, cols])
            return c, up, e, un

        def back(c0, dc, up, e, un):
            cols = pl.ds(c0, cw)
            du = (pltpu.roll(dc, N - 1, 0) * w_ref[pl.ds(0, 1), cols] + dc * w_ref[pl.ds(1, 1), cols]
                  + pltpu.roll(dc, 1, 0) * w_ref[pl.ds(2, 1), cols])
            du_ref[:, cols] = du[HALO:HALO + T].astype(BF16)
            dct = dc[HALO:HALO + T]
            dw_ref[pl.ds(0, 1), cols] += jnp.sum(dct * up[HALO:HALO + T], axis=0, keepdims=True)
            dw_ref[pl.ds(1, 1), cols] += jnp.sum(dct * e[HALO:HALO + T], axis=0, keepdims=True)
            dw_ref[pl.ds(2, 1), cols] += jnp.sum(dct * un[HALO:HALO + T], axis=0, keepdims=True)
            db_ref[:, cols] += jnp.sum(dct, axis=0, keepdims=True)

        for c0 in range(0, F, cw):
            dg = ext(g_ref, pg_ref, ng_ref, pl.ds(c0, cw))
            ca, upa, ea, una = conv(c0)
            cv, upv, ev, unv = conv(F + c0)
            s = _sigmoid(ca)
            back(F + c0, dg * (ca * s), upv, ev, unv)
            back(c0, dg * cv * (s * (1.0 + ca * (1.0 - s))), upa, ea, una)

    return pl.pallas_call(
        body, name=name, grid=(n_tiles,),
        in_specs=[rowu, pu, nu, rowg, pg, ng, pl.BlockSpec((3, F2), lambda i: (0, 0)),
                  pl.BlockSpec((1, F2), lambda i: (0, 0))],
        out_specs=[rowu, pl.BlockSpec((3, F2), lambda i: (0, 0)), pl.BlockSpec((1, F2), lambda i: (0, 0))],
        out_shape=[SDS((R, F2), BF16), SDS((3, F2), F32), SDS((1, F2), F32)],
        compiler_params=_params(1),
    )(u, u, u, dgact, dgact, dgact, conv_w, conv_b)


def _pool_counts(i, n_lat, s_len, l_len, n_rows, offset):
    ctx = i >= n_lat
    t0 = jnp.where(ctx, i - n_lat, i) * ROW_TILE + offset
    seg = jnp.where(ctx, l_len, s_len)
    t = t0 + lax.broadcasted_iota(jnp.int32, (n_rows, 1), 0)
    out = []
    for win in POOL_WINDOWS:
        cnt = jnp.minimum(t + win // 2, seg) - jnp.maximum(t - win // 2, 0)
        out.append(jnp.maximum(cnt, 1).astype(F32))
    return out


def _window_sum(e, lo, hi, n):
    acc = None
    for j in range(lo, hi + 1):
        term = e if j == 0 else pltpu.roll(e, (-j) % n, 0)
        acc = term if acc is None else acc + term
    return acc


def _pool_fwd(h, w, b, scale, n_lat, s_len, l_len, name):
    R, D = h.shape
    G = D // 4
    n_tiles = R // ROW_TILE
    T, N = ROW_TILE, ROW_TILE + 2 * HALO
    row = pl.BlockSpec((T, D), lambda i: (i, 0))
    prev, nxt = _halo_specs(n_tiles, D)
    vec = pl.BlockSpec((1, D), lambda i: (0, 0))

    def body(h_ref, p_ref, n_ref, w_ref, b_ref, s_ref, y_ref):
        i = pl.program_id(0)
        first, last = _edge_flags(i, n_lat, n_tiles)
        cnts = _pool_counts(i, n_lat, s_len, l_len, T, 0)
        for g, win in enumerate(POOL_WINDOWS):
            cols = pl.ds(g * G, G)
            pr = jnp.where(first, 0.0, p_ref[:, cols])
            nx = jnp.where(last, 0.0, n_ref[:, cols])
            hv = h_ref[:, cols]
            e = jnp.concatenate([pr, hv, nx], axis=0)
            mean = _window_sum(e, -(win // 2), win // 2 - 1, N)[HALO:HALO + T] / cnts[g]
            yg = _dot(_bf(mean - hv), w_ref[g])
            y_ref[:, cols] = (yg + b_ref[:, cols]) * s_ref[:, cols]

    return pl.pallas_call(
        body, name=name, grid=(n_tiles,),
        in_specs=[row, prev, nxt, pl.BlockSpec((4, G, G), lambda i: (0, 0, 0)), vec, vec],
        out_specs=row, out_shape=SDS((R, D), F32), compiler_params=_params(1),
    )(h, h, h, w, b, scale)


def _pool_bwd(h, dy, w, b, scale, n_lat, s_len, l_len, name):
    R, D = h.shape
    G = D // 4
    n_tiles = R // ROW_TILE
    T, N = ROW_TILE, ROW_TILE + 2 * HALO
    row = pl.BlockSpec((T, D), lambda i: (i, 0))
    prev, nxt = _halo_specs(n_tiles, D)
    vec = pl.BlockSpec((1, D), lambda i: (0, 0))
    wspec = pl.BlockSpec((4, G, G), lambda i: (0, 0, 0))

    def body(h_ref, ph_ref, nh_ref, d_ref, pd_ref, nd_ref, w_ref, b_ref, s_ref,
             dh_ref, dw_ref, db_ref, ds_ref):
        i = pl.program_id(0)
        first, last = _edge_flags(i, n_lat, n_tiles)

        @pl.when(i == 0)
        def _():
            dw_ref[...] = jnp.zeros_like(dw_ref)
            db_ref[...] = jnp.zeros_like(db_ref)
            ds_ref[...] = jnp.zeros_like(ds_ref)

        cnts = _pool_counts(i, n_lat, s_len, l_len, T, 0)
        cnts_ext = _pool_counts(i, n_lat, s_len, l_len, N, -HALO)
        for g, win in enumerate(POOL_WINDOWS):
            cols = pl.ds(g * G, G)

            def ext(t_ref, p_ref, n_ref):
                pr = jnp.where(first, 0.0, p_ref[:, cols])
                nx = jnp.where(last, 0.0, n_ref[:, cols])
                return jnp.concatenate([pr, t_ref[:, cols], nx], axis=0)

            hv = h_ref[:, cols]
            mean = _window_sum(ext(h_ref, ph_ref, nh_ref), -(win // 2), win // 2 - 1, N)[HALO:HALO + T] / cnts[g]
            z = _bf(mean - hv)
            sc = s_ref[:, cols]
            dye = ext(d_ref, pd_ref, nd_ref)
            dt = _bf(dye * sc)
            dz = _dot_nt(dt, w_ref[g])
            dm = dz / cnts_ext[g]
            dh = _window_sum(dm, -(win // 2 - 1), win // 2, N) - dz
            dh_ref[:, cols] = dh[HALO:HALO + T]
            dyt = dye[HALO:HALO + T]
            dw_ref[g] += _dot_tn(z, dt[HALO:HALO + T])
            db_ref[:, cols] += jnp.sum(dyt * sc, axis=0, keepdims=True)
            ds_ref[:, cols] += jnp.sum(dyt * (_dot(z, w_ref[g]) + b_ref[:, cols]), axis=0, keepdims=True)

    return pl.pallas_call(
        body, name=name, grid=(n_tiles,),
        in_specs=[row, prev, nxt, row, prev, nxt, wspec, vec, vec],
        out_specs=[row, wspec, vec, vec],
        out_shape=[SDS((R, D), F32), SDS((4, G, G), F32), SDS((1, D), F32), SDS((1, D), F32)],
        compiler_params=_params(1),
    )(h, h, h, dy, dy, dy, w, b, scale)


def _rope_tables(s_len, l_len):
    t = jnp.arange(s_len)
    row = (t // GRID_W).astype(F32)
    col = (t % GRID_W).astype(F32)
    axis_dim = HEAD_DIM // 2
    inv = ROPE_THETA ** (-jnp.arange(0, axis_dim, 2, dtype=F32) / axis_dim)
    ar, ac = row[:, None] * inv, col[:, None] * inv
    cos = jnp.concatenate([jnp.cos(ar), jnp.cos(ar), jnp.cos(ac), jnp.cos(ac)], axis=-1)
    sin = jnp.concatenate([-jnp.sin(ar), jnp.sin(ar), -jnp.sin(ac), jnp.sin(ac)], axis=-1)
    cos = jnp.concatenate([cos, jnp.ones((l_len, HEAD_DIM), F32)], axis=0)
    sin = jnp.concatenate([sin, jnp.zeros((l_len, HEAD_DIM), F32)], axis=0)
    return cos, sin


def _swap_halves(v):
    lane = lax.broadcasted_iota(jnp.int32, v.shape, 1)
    return jnp.where((lane % 64) < 32, pltpu.roll(v, 96, 1), pltpu.roll(v, 32, 1))


def _qk_prep_fwd(qkv, q_gain, k_gain, cos, sin):
    R = qkv.shape[0]
    NQ, NK = N_HEADS * HEAD_DIM, N_KV * HEAD_DIM
    T = ROW_TILE
    vec = pl.BlockSpec((1, HEAD_DIM), lambda i: (0, 0))
    tab = pl.BlockSpec((T, HEAD_DIM), lambda i: (i, 0))

    def body(x_ref, qg_ref, kg_ref, c_ref, s_ref, q_ref, k_ref, v_ref):
        cosv, sinv = c_ref[...], s_ref[...]

        def prep(c0, gain):
            xh = x_ref[:, pl.ds(c0, HEAD_DIM)]
            xn = xh * lax.rsqrt(jnp.mean(xh * xh, axis=-1, keepdims=True) + EPS) * gain
            return _bf(xn * cosv + _swap_halves(xn) * sinv)

        for hd in range(N_HEADS):
            q_ref[:, pl.ds(hd * HEAD_DIM, HEAD_DIM)] = prep(hd * HEAD_DIM, qg_ref[...])
        for hd in range(N_KV):
            k_ref[:, pl.ds(hd * HEAD_DIM, HEAD_DIM)] = prep(NQ + hd * HEAD_DIM, kg_ref[...])
            v_ref[:, pl.ds(2 * hd * HEAD_DIM, HEAD_DIM)] = _bf(x_ref[:, pl.ds(NQ + NK + hd * HEAD_DIM, HEAD_DIM)])
            v_ref[:, pl.ds((2 * hd + 1) * HEAD_DIM, HEAD_DIM)] = jnp.ones((T, HEAD_DIM), BF16)

    return pl.pallas_call(
        body, name="qk_prep_fwd", grid=(R // T,),
        in_specs=[pl.BlockSpec((T, NQ + 2 * NK), lambda i: (i, 0)), vec, vec, tab, tab],
        out_specs=[pl.BlockSpec((T, NQ), lambda i: (i, 0)), pl.BlockSpec((T, NK), lambda i: (i, 0)),
                   pl.BlockSpec((T, 2 * NK), lambda i: (i, 0))],
        out_shape=[SDS((R, NQ), BF16), SDS((R, NK), BF16), SDS((R, 2 * NK), BF16)],
        compiler_params=_params(1),
    )(qkv, q_gain, k_gain, cos, sin)


def _qk_prep_bwd(qkv, dq, dk, dv, q_gain, k_gain, cos, sin):
    R = qkv.shape[0]
    NQ, NK = N_HEADS * HEAD_DIM, N_KV * HEAD_DIM
    T = ROW_TILE
    vec = pl.BlockSpec((1, HEAD_DIM), lambda i: (0, 0))
    tab = pl.BlockSpec((T, HEAD_DIM), lambda i: (i, 0))

    def body(x_ref, dq_ref, dk_ref, dv_ref, qg_ref, kg_ref, c_ref, s_ref, o_ref, dqg_ref, dkg_ref):
        i = pl.program_id(0)
        cosv, sinv = c_ref[...], s_ref[...]

        @pl.when(i == 0)
        def _():
            dqg_ref[...] = jnp.zeros_like(dqg_ref)
            dkg_ref[...] = jnp.zeros_like(dkg_ref)

        def back(c0, dout, gain, dg_ref):
            xh = x_ref[:, pl.ds(c0, HEAD_DIM)]
            r = lax.rsqrt(jnp.mean(xh * xh, axis=-1, keepdims=True) + EPS)
            xhat = xh * r
            dxn = dout * cosv + _swap_halves(dout * sinv)
            dg_ref[...] += jnp.sum(dxn * xhat, axis=0, keepdims=True)
            a = dxn * gain
            o_ref[:, pl.ds(c0, HEAD_DIM)] = _bf(r * (a - xhat * jnp.mean(a * xhat, axis=-1, keepdims=True)))

        for hd in range(N_HEADS):
            back(hd * HEAD_DIM, dq_ref[:, pl.ds(hd * HEAD_DIM, HEAD_DIM)], qg_ref[...], dqg_ref)
        for hd in range(N_KV):
            back(NQ + hd * HEAD_DIM, dk_ref[:, pl.ds(hd * HEAD_DIM, HEAD_DIM)], kg_ref[...], dkg_ref)
        o_ref[:, pl.ds(NQ + NK, NK)] = _bf(dv_ref[...])

    return pl.pallas_call(
        body, name="qk_prep_bwd", grid=(R // T,),
        in_specs=[pl.BlockSpec((T, NQ + 2 * NK), lambda i: (i, 0)), pl.BlockSpec((T, NQ), lambda i: (i, 0)),
                  pl.BlockSpec((T, NK), lambda i: (i, 0)), pl.BlockSpec((T, NK), lambda i: (i, 0)),
                  vec, vec, tab, tab],
        out_specs=[pl.BlockSpec((T, NQ + 2 * NK), lambda i: (i, 0)), vec, vec],
        out_shape=[SDS((R, NQ + 2 * NK), BF16), SDS((1, HEAD_DIM), F32), SDS((1, HEAD_DIM), F32)],
        compiler_params=_params(1),
    )(qkv, dq, dk, dv, q_gain, k_gain, cos, sin)


def _flash_fwd(q, k, v, s_len, l_len):
    R = q.shape[0]
    T = ROW_TILE
    n_lat = s_len // T
    ck = _pick(s_len, 512, 128)
    scale = HEAD_DIM ** -0.5
    group = N_HEADS // N_KV
    GW = group * HEAD_DIM

    M = group * T
    to_log2 = scale * math.log2(math.e)

    def body(q_ref, k_ref, v_ref, o_ref, lse_ref, m8_s, mb_s, acc_s):
        i = pl.program_id(1)
        qv = jnp.concatenate([q_ref[:, pl.ds(hh * HEAD_DIM, HEAD_DIM)] for hh in range(group)], axis=0)
        chunks = s_len // ck

        m8_s[...] = jnp.full_like(m8_s, -jnp.inf)

        def stat(rows, n):
            st = _dot_nt(k_ref[rows, :], qv)
            m8_s[...] = jnp.maximum(m8_s[...], jnp.max(st.reshape(n // 8, 8, M), axis=0))

        @pl.when(i < n_lat)
        def _():
            def loop(c, carry):
                stat(pl.ds(pl.multiple_of(c * ck, ck), ck), ck)
                return carry
            lax.fori_loop(0, chunks, loop, 0)

        stat(pl.ds(s_len, l_len), l_len)
        m_row = jnp.max(m8_s[...], axis=0, keepdims=True) * to_log2
        mb = jnp.broadcast_to(m_row, (HEAD_DIM, M)).T
        mb_s[...] = jnp.concatenate([mb] * (ck // HEAD_DIM), axis=1)

        acc_s[...] = jnp.zeros_like(acc_s)

        def step(rows, n):
            s2 = _dot_nt(qv, k_ref[rows, :]) * to_log2
            p = jnp.exp2(s2 - mb_s[:, pl.ds(0, n)])
            acc_s[...] += _dot(_bf(p), v_ref[rows, :])

        @pl.when(i < n_lat)
        def _():
            def loop(c, carry):
                step(pl.ds(pl.multiple_of(c * ck, ck), ck), ck)
                return carry
            lax.fori_loop(0, chunks, loop, 0)

        step(pl.ds(s_len, l_len), l_len)
        l_rep = acc_s[:, pl.ds(HEAD_DIM, HEAD_DIM)]
        o = acc_s[:, pl.ds(0, HEAD_DIM)] / l_rep
        for hh in range(group):
            o_ref[:, pl.ds(hh * HEAD_DIM, HEAD_DIM)] = o[hh * T:(hh + 1) * T]
        lse = (mb_s[:, pl.ds(0, HEAD_DIM)] + jnp.log2(l_rep)) * math.log(2.0)
        lse_ref[...] = jnp.max(lse, axis=-1, keepdims=True).reshape(group, T, 1)

    return pl.pallas_call(
        body, name="flash_fwd", grid=(N_KV, R // T),
        in_specs=[pl.BlockSpec((T, GW), lambda g, i: (i, g)),
                  pl.BlockSpec((R, HEAD_DIM), lambda g, i: (0, g)),
                  pl.BlockSpec((R, 2 * HEAD_DIM), lambda g, i: (0, g))],
        out_specs=[pl.BlockSpec((T, GW), lambda g, i: (i, g)),
                   pl.BlockSpec((group, T, 1), lambda g, i: (g, i, 0))],
        out_shape=[SDS((R, N_HEADS * HEAD_DIM), F32), SDS((N_HEADS, R, 1), F32)],
        scratch_shapes=[pltpu.VMEM((8, M), F32), pltpu.VMEM((M, ck), F32), pltpu.VMEM((M, 2 * HEAD_DIM), F32)],
        compiler_params=_params(2),
    )(q, k, v)


def _flash_bwd(q, k, v, o, lse, do, s_len, l_len):
    R = q.shape[0]
    T = ROW_TILE
    n_lat = s_len // T
    ck = _pick(s_len, 512, 128)
    scale = HEAD_DIM ** -0.5
    group = N_HEADS // N_KV
    GW = group * HEAD_DIM
    qspec = pl.BlockSpec((T, GW), lambda g, i: (i, g))
    kspec = pl.BlockSpec((R, HEAD_DIM), lambda g, i: (0, g))

    M = group * T
    log2e = math.log2(math.e)

    def body(q_ref, do_ref, o_ref, lse_ref, k_ref, v_ref, dq_ref, dk_ref, dv_ref, dq_s, lse_s, delta_s):
        i = pl.program_id(1)

        @pl.when(i == 0)
        def _():
            dk_ref[...] = jnp.zeros_like(dk_ref)
            dv_ref[...] = jnp.zeros_like(dv_ref)

        def stacked(ref):
            return jnp.concatenate([ref[:, pl.ds(hh * HEAD_DIM, HEAD_DIM)] for hh in range(group)], axis=0)

        qv = stacked(q_ref)
        dov = stacked(do_ref)
        dob = _bf(dov)
        delta_s[...] = jnp.broadcast_to(jnp.sum(dov * stacked(o_ref), axis=-1, keepdims=True), (M, ck))
        lse_s[...] = jnp.broadcast_to(lse_ref[...].reshape(M, 1) * log2e, (M, ck))
        dq_s[...] = jnp.zeros_like(dq_s)

        def step(rows, n):
            kv, vv = k_ref[rows, :], v_ref[rows, :]
            p = jnp.exp2(_dot_nt(qv, kv) * (scale * log2e) - lse_s[:, pl.ds(0, n)])
            dv_ref[rows, :] += _dot_tn(_bf(p), dob)
            ds = _bf(p * (_dot_nt(dob, vv) - delta_s[:, pl.ds(0, n)]) * scale)
            dq_s[...] += _dot(ds, kv)
            dk_ref[rows, :] += _dot_tn(ds, qv)

        @pl.when(i < n_lat)
        def _():
            def loop(c, carry):
                step(pl.ds(pl.multiple_of(c * ck, ck), ck), ck)
                return carry
            lax.fori_loop(0, s_len // ck, loop, 0)

        step(pl.ds(s_len, l_len), l_len)
        for hh in range(group):
            dq_ref[:, pl.ds(hh * HEAD_DIM, HEAD_DIM)] = dq_s[pl.ds(hh * T, T), :]

    return pl.pallas_call(
        body, name="flash_bwd", grid=(N_KV, R // T),
        in_specs=[qspec, qspec, qspec, pl.BlockSpec((group, T, 1), lambda g, i: (g, i, 0)), kspec,
                  pl.BlockSpec((R, HEAD_DIM), lambda g, i: (0, 2 * g))],
        out_specs=[qspec, kspec, kspec],
        out_shape=[SDS((R, N_HEADS * HEAD_DIM), F32), SDS((R, N_KV * HEAD_DIM), F32),
                   SDS((R, N_KV * HEAD_DIM), F32)],
        scratch_shapes=[pltpu.VMEM((M, HEAD_DIM), F32), pltpu.VMEM((M, ck), F32), pltpu.VMEM((M, ck), F32)],
        compiler_params=_params(2),
    )(q, do, o, lse, k, v)


K_SCALE = RET_DK ** -0.5


def _log_sigmoid(v):
    return -(jnp.maximum(-v, 0.0) + jnp.log(1.0 + jnp.exp(-jnp.abs(v))))


def _ret_decays(d, lg):
    C = RET_CHUNK
    ic = lax.broadcasted_iota(jnp.int32, (C, 1), 0)
    ir = lax.broadcasted_iota(jnp.int32, (1, C), 1)
    li = jnp.where(d == 0, ic, C - 1 - ic).astype(F32)
    lj = jnp.where(d == 0, ir, C - 1 - ir).astype(F32)
    diff = li - lj
    mask = jnp.where(diff >= 0, jnp.exp(jnp.maximum(diff, 0.0) * lg), 0.0)
    qd = jnp.exp((li + 1.0) * lg)
    kd = jnp.exp((C - 1.0 - li) * lg)
    cd = jnp.exp(C * lg)
    return li, diff, mask, qd, kd, cd


def _ctx_weights(d, t, lg, l_len):
    C = RET_CHUNK
    j = (t * C + lax.broadcasted_iota(jnp.int32, (C, 1), 0)).astype(F32)
    e = jnp.where(d == 0, (l_len - 1.0) - j, j)
    return e, jnp.exp(e * lg)


def _ret_specs(n_lat_c, n_ctx_c, ctx_first):
    def blk(d, t):
        if ctx_first:
            n = jnp.maximum(t - n_ctx_c, 0)
            lat = jnp.where(d == 0, n, n_lat_c - 1 - n)
            return jnp.where(t < n_ctx_c, n_lat_c + t, lat)
        n = jnp.minimum(t, n_lat_c - 1)
        lat = jnp.where(d == 0, n_lat_c - 1 - n, n)
        return jnp.where(t >= n_lat_c, t, lat)
    return blk


def _ret_fwd(proj, lgt, s_len, l_len):
    R = proj.shape[0]
    C, H, DK, DV = RET_CHUNK, RET_HEADS, RET_DK, RET_DV
    nl, nc = s_len // C, l_len // C
    blk = _ret_specs(nl, nc, True)

    def body(q_ref, k_ref, v_ref, lg_ref, o_ref, st_ref, r_s):
        d, t = pl.program_id(0), pl.program_id(2)
        lg = jnp.max(_log_sigmoid(lg_ref[0, 0]), axis=-1, keepdims=True)

        @pl.when(t == 0)
        def _():
            r_s[...] = jnp.zeros_like(r_s)

        @pl.when(t < nc)
        def _():
            _, w = _ctx_weights(d, t, lg, l_len)
            r_s[...] += _dot_tn(_bf(k_ref[...] * K_SCALE * w), _bf(v_ref[...]))
            o_ref[0] = jnp.zeros((C, DV), F32)

        @pl.when(t >= nc)
        def _():
            _, _, mask, qd, kd, cd = _ret_decays(d, lg)
            qb, kv, vb = _bf(q_ref[...]), k_ref[...] * K_SCALE, _bf(v_ref[...])
            r = r_s[...]
            st_ref[0, 0, 0] = r
            att = _dot_nt(qb, _bf(kv)) * mask
            o_ref[0] = _dot(_bf(att), vb) + _dot(qb, _bf(r)) * qd
            r_s[...] = r * cd + _dot_tn(_bf(kv * kd), vb)

    return pl.pallas_call(
        body, name="ret_fwd", grid=(2, H, nc + nl),
        in_specs=[pl.BlockSpec((C, DK), lambda d, h, t: (blk(d, t), h)),
                  pl.BlockSpec((C, DK), lambda d, h, t: (blk(d, t), H + h)),
                  pl.BlockSpec((C, DV), lambda d, h, t: (blk(d, t), H + h)),
                  pl.BlockSpec((1, 1, 1, 128), lambda d, h, t: (d, h, 0, 0))],
        out_specs=[pl.BlockSpec((1, C, DV), lambda d, h, t: (d, blk(d, t), h)),
                   pl.BlockSpec((1, 1, 1, DK, DV), lambda d, h, t: (d, h, jnp.maximum(t - nc, 0), 0, 0))],
        out_shape=[SDS((2, R, H * DV), F32), SDS((2, H, nl, DK, DV), F32)],
        scratch_shapes=[pltpu.VMEM((DK, DV), F32)],
        compiler_params=_params(3),
    )(proj, proj, proj, lgt)


def _ret_bwd(proj, lgt, states, do, s_len, l_len):
    R = proj.shape[0]
    C, H, DK, DV = RET_CHUNK, RET_HEADS, RET_DK, RET_DV
    nl, nc = s_len // C, l_len // C
    blk = _ret_specs(nl, nc, False)
    last = nl + nc - 1

    def body(q_ref, k_ref, v_ref, lg_ref, st_ref, do_ref, dq_ref, dk_ref, dv_ref, dlg_ref, dr_s, dl_s):
        d, t = pl.program_id(0), pl.program_id(2)
        x = lg_ref[0, 0]
        lg = jnp.max(_log_sigmoid(x), axis=-1, keepdims=True)

        @pl.when(t == 0)
        def _():
            dr_s[...] = jnp.zeros_like(dr_s)
            dl_s[...] = jnp.zeros_like(dl_s)

        @pl.when(t < nl)
        def _():
            li, diff, mask, qd, kd, cd = _ret_decays(d, lg)
            qv, kv, vv, dov = q_ref[...], k_ref[...] * K_SCALE, v_ref[...], do_ref[...]
            qb, kb, vb, dob = _bf(qv), _bf(kv), _bf(vv), _bf(dov)
            r, drn = st_ref[0, 0, 0], dr_s[...]
            rb, drb = _bf(r), _bf(drn)
            p = _dot_nt(qb, kb)
            dp = _dot_nt(dob, vb) * mask
            dpb = _bf(dp)
            doq = _bf(dov * qd)
            dq_inter = _dot_nt(doq, rb)
            dk_state = kd * _dot_nt(vb, drb)
            dq_ref[0] = _dot(dpb, kb) + dq_inter
            dk_ref[0] = (_dot_tn(dpb, qb) + dk_state) * K_SCALE
            dv_ref[0] = _dot_tn(_bf(p * mask), dob) + _dot(_bf(kv * kd), drb)
            dr_s[...] = cd * drn + _dot_tn(qb, doq)
            dl_s[...] += (jnp.sum(dp * p * diff) + jnp.sum((li + 1.0) * qv * dq_inter)
                          + jnp.sum((C - 1.0 - li) * kv * dk_state) + C * jnp.sum(cd * r * drn))

        @pl.when(t >= nl)
        def _():
            e, w = _ctx_weights(d, t - nl, lg, l_len)
            kv, vb, drb = k_ref[...] * K_SCALE, _bf(v_ref[...]), _bf(dr_s[...])
            dkc = w * _dot_nt(vb, drb)
            dq_ref[0] = jnp.zeros((C, DK), F32)
            dk_ref[0] = dkc * K_SCALE
            dv_ref[0] = _dot(_bf(kv * w), drb)
            dl_s[...] += jnp.sum(e * kv * dkc)

        @pl.when(t == last)
        def _():
            dlg_ref[0, 0] = dl_s[...] * (1.0 / (1.0 + jnp.exp(x)))

    return pl.pallas_call(
        body, name="ret_bwd", grid=(2, H, nl + nc),
        in_specs=[pl.BlockSpec((C, DK), lambda d, h, t: (blk(d, t), h)),
                  pl.BlockSpec((C, DK), lambda d, h, t: (blk(d, t), H + h)),
                  pl.BlockSpec((C, DV), lambda d, h, t: (blk(d, t), H + h)),
                  pl.BlockSpec((1, 1, 1, 128), lambda d, h, t: (d, h, 0, 0)),
                  pl.BlockSpec((1, 1, 1, DK, DV), lambda d, h, t: (d, h, jnp.maximum(nl - 1 - t, 0), 0, 0)),
                  pl.BlockSpec((C, DV), lambda d, h, t: (blk(d, t), h))],
        out_specs=[pl.BlockSpec((1, C, DK), lambda d, h, t: (d, blk(d, t), h)),
                   pl.BlockSpec((1, C, DK), lambda d, h, t: (d, blk(d, t), h)),
                   pl.BlockSpec((1, C, DV), lambda d, h, t: (d, blk(d, t), h)),
                   pl.BlockSpec((1, 1, 1, 128), lambda d, h, t: (d, h, 0, 0))],
        out_shape=[SDS((2, R, H * DK), F32), SDS((2, R, H * DK), F32), SDS((2, R, H * DV), F32),
                   SDS((2, H, 1, 128), F32)],
        scratch_shapes=[pltpu.VMEM((DK, DV), F32), pltpu.VMEM((1, 128), F32)],
        compiler_params=_params(3),
    )(proj, proj, proj, lgt, states, do)


def _readout_fwd(o2, proj, gn_w):
    R = proj.shape[0]
    H, DV = RET_HEADS, RET_DV
    W = H * DV
    T = ROW_TILE

    def body(o_ref, g_ref, w_ref, out_ref):
        for hh in range(H):
            cols = pl.ds(hh * DV, DV)
            y = o_ref[0, :, cols] + o_ref[1, :, cols]
            yc = y - jnp.mean(y, axis=-1, keepdims=True)
            yn = yc * lax.rsqrt(jnp.mean(yc * yc, axis=-1, keepdims=True) + EPS) * w_ref[:, cols]
            g = g_ref[:, cols]
            out_ref[:, cols] = _bf(g * _sigmoid(g) * yn)

    return pl.pallas_call(
        body, name="readout_fwd", grid=(R // T,),
        in_specs=[pl.BlockSpec((2, T, W), lambda i: (0, i, 0)), pl.BlockSpec((T, W), lambda i: (i, 2)),
                  pl.BlockSpec((1, W), lambda i: (0, 0))],
        out_specs=pl.BlockSpec((T, W), lambda i: (i, 0)),
        out_shape=SDS((R, W), BF16), compiler_params=_params(1),
    )(o2, proj, gn_w)


def _readout_bwd(o2, proj, gn_w, dgated):
    R = proj.shape[0]
    H, DV = RET_HEADS, RET_DV
    W = H * DV
    T = ROW_TILE

    def body(o_ref, g_ref, w_ref, d_ref, do_ref, dg_ref, dw_ref):
        i = pl.program_id(0)

        @pl.when(i == 0)
        def _():
            dw_ref[...] = jnp.zeros_like(dw_ref)

        for hh in range(H):
            cols = pl.ds(hh * DV, DV)
            y = o_ref[0, :, cols] + o_ref[1, :, cols]
            yc = y - jnp.mean(y, axis=-1, keepdims=True)
            rstd = lax.rsqrt(jnp.mean(yc * yc, axis=-1, keepdims=True) + EPS)
            yn0 = yc * rstd
            wv = w_ref[:, cols]
            g = g_ref[:, cols]
            s = _sigmoid(g)
            dgt = d_ref[:, cols]
            dyn = dgt * (g * s)
            dg_ref[:, cols] = _bf(dgt * (yn0 * wv) * (s * (1.0 + g * (1.0 - s))))
            dw_ref[:, cols] += jnp.sum(dyn * yn0, axis=0, keepdims=True)
            a = dyn * wv
            do_ref[:, cols] = rstd * (a - jnp.mean(a, axis=-1, keepdims=True)
                                      - yn0 * jnp.mean(a * yn0, axis=-1, keepdims=True))

    return pl.pallas_call(
        body, name="readout_bwd", grid=(R // T,),
        in_specs=[pl.BlockSpec((2, T, W), lambda i: (0, i, 0)), pl.BlockSpec((T, W), lambda i: (i, 2)),
                  pl.BlockSpec((1, W), lambda i: (0, 0)), pl.BlockSpec((T, W), lambda i: (i, 0))],
        out_specs=[pl.BlockSpec((T, W), lambda i: (i, 0)), pl.BlockSpec((T, W), lambda i: (i, 0)),
                   pl.BlockSpec((1, W), lambda i: (0, 0))],
        out_shape=[SDS((R, W), F32), SDS((R, W), BF16), SDS((1, W), F32)],
        compiler_params=_params(1),
    )(o2, proj, gn_w, dgated)


def _ret_dproj(dq2, dk2, dv2, dg):
    R = dg.shape[0]
    NQ, NV = RET_HEADS * RET_DK, RET_HEADS * RET_DV
    T = ROW_TILE

    def body(dq_ref, dk_ref, dv_ref, dg_ref, o_ref):
        o_ref[:, pl.ds(0, NQ)] = _bf(dq_ref[0] + dq_ref[1])
        o_ref[:, pl.ds(NQ, NQ)] = _bf(dk_ref[0] + dk_ref[1])
        o_ref[:, pl.ds(2 * NQ, NV)] = _bf(dv_ref[0] + dv_ref[1])
        o_ref[:, pl.ds(2 * NQ + NV, NV)] = dg_ref[...]

    return pl.pallas_call(
        body, name="ret_dproj", grid=(R // T,),
        in_specs=[pl.BlockSpec((2, T, NQ), lambda i: (0, i, 0)), pl.BlockSpec((2, T, NQ), lambda i: (0, i, 0)),
                  pl.BlockSpec((2, T, NV), lambda i: (0, i, 0)), pl.BlockSpec((T, NV), lambda i: (i, 0))],
        out_specs=pl.BlockSpec((T, 2 * NQ + 2 * NV), lambda i: (i, 0)),
        out_shape=SDS((R, 2 * NQ + 2 * NV), BF16), compiler_params=_params(1),
    )(dq2, dk2, dv2, dg)


def _silu(v):
    return v * _sigmoid(v)


def _ada_fwd(c_rows, ada_w, ada_b_shard):
    depth, D, cols = ada_w.shape

    def body(c_ref, w_ref, b_ref, o_ref):
        o_ref[0] = _dot(_bf(_silu(c_ref[...])), _bf(w_ref[0])) + b_ref[0]

    return pl.pallas_call(
        body, name="ada_fwd", grid=(depth,),
        in_specs=[pl.BlockSpec((16, D), lambda i: (0, 0)), pl.BlockSpec((1, D, cols), lambda i: (i, 0, 0)),
                  pl.BlockSpec((1, 1, cols), lambda i: (i, 0, 0))],
        out_specs=pl.BlockSpec((1, 16, cols), lambda i: (i, 0, 0)),
        out_shape=SDS((depth, 16, cols), F32), compiler_params=_params(1),
    )(c_rows, ada_w, ada_b_shard)


def _ada_bwd(c_rows, ada_w, d_lat, d_ctx):
    depth, D, cols = ada_w.shape

    def body(c_ref, w_ref, dl_ref, dc_ref, dw_ref, pc_ref):
        i = pl.program_id(0)
        cv = c_ref[...]
        a = _silu(cv)
        dcs = jnp.broadcast_to(jnp.sum(dc_ref[0], axis=0, keepdims=True), (8, cols))
        dw_ref[0] = _dot_tn(_bf(a[0:8]), _bf(dl_ref[0])) + _dot_tn(_bf(a[8:16]), _bf(dcs))

        @pl.when(i == 0)
        def _():
            pc_ref[...] = jnp.zeros_like(pc_ref)

        pc_ref[...] += _dot_nt(_bf(dcs), _bf(w_ref[0]))

        @pl.when(i == depth - 1)
        def _():
            cc = c_ref[pl.ds(8, 1), :]
            s = _sigmoid(cc)
            pc_ref[...] = pc_ref[...] * (s * (1.0 + cc * (1.0 - s)))

    return pl.pallas_call(
        body, name="ada_bwd", grid=(depth,),
        in_specs=[pl.BlockSpec((16, D), lambda i: (0, 0)), pl.BlockSpec((1, D, cols), lambda i: (i, 0, 0)),
                  pl.BlockSpec((1, 8, cols), lambda i: (i, 0, 0)), pl.BlockSpec((1, 8, cols), lambda i: (i, 0, 0))],
        out_specs=[pl.BlockSpec((1, D, cols), lambda i: (i, 0, 0)), pl.BlockSpec((8, D), lambda i: (0, 0))],
        out_shape=[SDS((depth, D, cols), F32), SDS((8, D), F32)], compiler_params=_params(1),
    )(c_rows, ada_w, d_lat, d_ctx)


def _adamw(w, g, m, v, name):
    shape = w.shape
    n = g.shape[0]
    cols = shape[-1]
    rows = w.size // cols
    tr = _pick(rows, 512, 8) if rows * cols * 4 > (1 << 20) else rows
    spec = pl.BlockSpec((tr, cols), lambda i: (i, 0))

    def body(w_ref, g_ref, m_ref, v_ref, go_ref, d_ref, mo_ref, vo_ref):
        gs = g_ref[0].astype(F32)
        for k in range(1, n):
            gs = gs + g_ref[k].astype(F32)
        mn = ADAM_B1 * m_ref[...] + (1.0 - ADAM_B1) * gs
        vn = ADAM_B2 * v_ref[...] + (1.0 - ADAM_B2) * jnp.square(gs)
        m_hat = mn / (1.0 - ADAM_B1 ** ADAM_STEP)
        v_hat = vn / (1.0 - ADAM_B2 ** ADAM_STEP)
        go_ref[...] = gs
        d_ref[...] = -ADAM_LR * (m_hat / (jnp.sqrt(v_hat) + ADAM_EPS) + ADAM_WD * w_ref[...])
        mo_ref[...] = mn
        vo_ref[...] = vn

    outs = pl.pallas_call(
        body, name=name, grid=(rows // tr,),
        in_specs=[spec, pl.BlockSpec((n, tr, cols), lambda i: (0, i, 0)), spec, spec],
        out_specs=[spec] * 4, out_shape=[SDS((rows, cols), F32)] * 4, compiler_params=_params(1),
    )(w.reshape(rows, cols), g.reshape(n, rows, cols), m.reshape(rows, cols), v.reshape(rows, cols))
    return tuple(o.reshape(shape) for o in outs)


def _sum_slots(own, recv, name):
    n, rows, cols = recv.shape
    tr = _pick(rows, 512, 16)

    def body(own_ref, r_ref, o_ref):
        acc = own_ref[...].astype(F32)
        for k in range(n):
            acc = acc + r_ref[k].astype(F32)
        o_ref[...] = acc

    return pl.pallas_call(
        body, name=name, grid=(rows // tr,),
        in_specs=[pl.BlockSpec((tr, cols), lambda i: (i, 0)), pl.BlockSpec((n, tr, cols), lambda i: (0, i, 0))],
        out_specs=pl.BlockSpec((tr, cols), lambda i: (i, 0)),
        out_shape=SDS((rows, cols), F32), compiler_params=_params(1),
    )(own, recv)


def _position():
    return lax.axis_index("x"), lax.axis_index("y"), lax.axis_index("c")


def _peer(k, x, y, c):
    return (1 - x if k & 4 else x, 1 - y if k & 2 else y, 1 - c if k & 1 else c)


def _index(pos):
    return 4 * pos[0] + 2 * pos[1] + pos[2]


def _gather_small(v, name):
    rows, lanes = v.shape

    def body(x_ref, out_ref, send_sems, recv_sems, local_sem):
        me = _position()
        mine = pltpu.make_async_copy(x_ref, out_ref.at[_index(me)], local_sem)
        mine.start()

        def copy(k, slot):
            return pltpu.make_async_remote_copy(
                src_ref=x_ref, dst_ref=out_ref.at[slot], send_sem=send_sems.at[k - 1],
                recv_sem=recv_sems.at[k - 1], device_id=_peer(k, *me), device_id_type=MESH)

        sends = [copy(k, _index(me)) for k in range(1, N_DEV)]
        for cp in sends:
            cp.start()
        for k in range(1, N_DEV):
            copy(k, _index(_peer(k, *me))).wait_recv()
        for cp in sends:
            cp.wait_send()
        mine.wait()

    return pl.pallas_call(
        body, name=name, out_shape=SDS((N_DEV, rows, lanes), v.dtype),
        in_specs=[pl.BlockSpec(memory_space=pltpu.VMEM)],
        out_specs=pl.BlockSpec(memory_space=pltpu.VMEM),
        scratch_shapes=[pltpu.SemaphoreType.DMA((N_DEV - 1,)), pltpu.SemaphoreType.DMA((N_DEV - 1,)),
                        pltpu.SemaphoreType.DMA],
        compiler_params=pltpu.CompilerParams(vmem_limit_bytes=VMEM_LIMIT_V7X),
    )(v)


def _gather_big(v, name):
    rows, cols = v.shape

    def body(x_ref, out_ref, send_sems, recv_sems, local_sem):
        x, y, c = _position()
        me, sibling = (x, y, c), (x, y, 1 - c)
        chips = [(1 - x, y), (x, 1 - y), (1 - x, 1 - y)]

        def copy(k, block, to, src=None):
            slot = out_ref.at[_index(block)]
            return pltpu.make_async_remote_copy(
                src_ref=slot if src is None else src, dst_ref=slot, send_sem=send_sems.at[k],
                recv_sem=recv_sems.at[k], device_id=to, device_id_type=MESH)

        mine = pltpu.make_async_copy(x_ref, out_ref.at[_index(me)], local_sem)
        mine.start()
        first = [copy(0, me, sibling, src=x_ref)]
        first += [copy(1 + j, me, (*chip, c), src=x_ref) for j, chip in enumerate(chips)]
        for cp in first:
            cp.start()
        passed = [copy(4 + j, (*chip, c), sibling) for j, chip in enumerate(chips)]
        for j, chip in enumerate(chips):
            copy(1 + j, (*chip, c), me).wait_recv()
            passed[j].start()
        copy(0, sibling, me).wait_recv()
        for j, chip in enumerate(chips):
            copy(4 + j, (*chip, 1 - c), me).wait_recv()
        for cp in first + passed:
            cp.wait_send()
        mine.wait()

    return pl.pallas_call(
        body, name=name, out_shape=SDS((N_DEV, rows, cols), v.dtype),
        in_specs=[pl.BlockSpec(memory_space=pl.ANY)],
        out_specs=pl.BlockSpec(memory_space=pl.ANY),
        scratch_shapes=[pltpu.SemaphoreType.DMA((N_DEV - 1,)), pltpu.SemaphoreType.DMA((N_DEV - 1,)),
                        pltpu.SemaphoreType.DMA],
    )(v)


HBM_SPEC = pl.BlockSpec(memory_space=pltpu.HBM)
SEM_SPEC = pl.BlockSpec(memory_space=pltpu.SEMAPHORE)
SPLIT_EFFECT = pltpu.SideEffectType.DATAFLOW_SIDE_EFFECTING


def _split_start(src, land, gather, name):
    def body(src_ref, land_ref, send_sem, recv_sem, src_thru, land_thru, token):
        me = _position()
        for k in range(1, N_DEV):
            peer = _peer(k, *me)
            pltpu.make_async_remote_copy(
                src_ref=src_ref if gather else src_ref.at[_index(peer)], dst_ref=land_ref.at[_index(me)],
                send_sem=send_sem, recv_sem=recv_sem, device_id=peer, device_id_type=MESH).start()
        token[...] = jnp.zeros_like(token)

    return pl.pallas_call(
        body, name=name,
        out_shape=(pltpu.SemaphoreType.DMA(()), pltpu.SemaphoreType.DMA(()), pltpu.HBM(src.shape, src.dtype),
                   pltpu.HBM(land.shape, land.dtype), SDS((8, 128), F32)),
        in_specs=(HBM_SPEC, HBM_SPEC),
        out_specs=(SEM_SPEC, SEM_SPEC, HBM_SPEC, HBM_SPEC, pl.BlockSpec(memory_space=pltpu.VMEM)),
        input_output_aliases={0: 2, 1: 3},
        compiler_params=pltpu.CompilerParams(has_side_effects=SPLIT_EFFECT),
    )(pltpu.with_memory_space_constraint(src, pltpu.HBM), pltpu.with_memory_space_constraint(land, pltpu.HBM))


def _split_wait(send_sem, recv_sem, src_thru, land_thru, after, name):
    def body(src_ref, land_ref, send_sem, recv_sem, after_ref, src_out, land_out):
        me = _position()
        seven = land_ref.at[pl.ds(0, N_DEV - 1)]
        copies = pltpu.make_async_remote_copy(
            src_ref=seven, dst_ref=seven, send_sem=send_sem, recv_sem=recv_sem,
            device_id=_peer(1, *me), device_id_type=MESH)
        copies.wait_send()
        copies.wait_recv()

    return pl.pallas_call(
        body, name=name,
        out_shape=(pltpu.HBM(src_thru.shape, src_thru.dtype), pltpu.HBM(land_thru.shape, land_thru.dtype)),
        in_specs=(HBM_SPEC, HBM_SPEC, SEM_SPEC, SEM_SPEC, pl.BlockSpec(memory_space=pl.ANY)),
        out_specs=(HBM_SPEC, HBM_SPEC), input_output_aliases={0: 0, 1: 1},
        compiler_params=pltpu.CompilerParams(has_side_effects=SPLIT_EFFECT),
    )(src_thru, land_thru, send_sem, recv_sem, after)


def _pack_rows(arrays, lanes, dtype):
    flat = jnp.concatenate([a.astype(dtype).reshape(-1) for a in arrays])
    pad = (-flat.size) % (16 * lanes)
    if pad:
        flat = jnp.concatenate([flat, jnp.zeros((pad,), dtype)])
    return flat.reshape(-1, lanes)


def _unpack_rows(packed, shapes):
    n = packed.shape[0]
    flat = packed.reshape(n, -1)
    out, off = [], 0
    for shp in shapes:
        size = math.prod(shp)
        out.append(flat[:, off:off + size].reshape((n,) + tuple(shp)))
        off += size
    return out


def _unshard(g8, axis):
    moved = jnp.moveaxis(g8, 0, axis)
    shp = list(moved.shape)
    shp[axis:axis + 2] = [shp[axis] * shp[axis + 1]]
    return moved.reshape(shp)


def _split8(full, axis):
    shp = list(full.shape)
    shp[axis:axis + 1] = [N_DEV, shp[axis] // N_DEV]
    return jnp.moveaxis(full.reshape(shp), axis, 0)


def _my_shard(g, axis, me):
    size = g.shape[axis + 1] // N_DEV
    return lax.dynamic_slice_in_dim(g, me * size, size, axis=axis + 1)


BIG_WEIGHTS = ("ffn_w_up", "ffn_w_down", "attn_w_qkv", "attn_w_o", "ret_w_in", "ret_w_out", "pool_w")
LAYER_WEIGHTS = (
    (("ffn_w_up", 0, 1), ("ffn_w_down", 0, 0), ("pool_w", 0, 1)),
    (("ffn_w_up", 1, 1), ("ffn_w_down", 1, 0), ("attn_w_qkv", 0, 1), ("attn_w_o", 0, 0)),
    (("ffn_w_up", 2, 1), ("ffn_w_down", 2, 0), ("ret_w_in", 0, 1), ("ret_w_out", 0, 0)),
    (("ffn_w_up", 3, 1), ("ffn_w_down", 3, 0), ("pool_w", 1, 1)),
)
SMALL_SHARDED = (("norm_w", 2), ("pool_b", 1), ("pool_scale", 1), ("ret_gn_w", 1), ("ffn_conv_w", 2))
REPLICATED = ("ada_b", "attn_q_gain", "attn_k_gain", "ret_decay_logit", "ffn_conv_b")
WEIGHT_ORDER = ("c_ctx", "ada_w", "ada_b", "norm_w", "pool_w", "pool_b", "pool_scale", "attn_w_qkv",
                "attn_q_gain", "attn_k_gain", "attn_w_o", "ret_w_in", "ret_decay_logit", "ret_gn_w",
                "ret_w_out", "ffn_w_up", "ffn_conv_w", "ffn_conv_b", "ffn_w_down")


def _local_step(x0, target, mods, P, get_weights, put_grads, s_len, l_len):
    n_lat = s_len // ROW_TILE
    nw = P["norm_w"]
    lgt = jnp.broadcast_to(P["ret_decay_logit"][0][:, :, None, None], (2, RET_HEADS, 1, 128))
    cos, sin = _rope_tables(s_len, l_len)
    h_dtype = [F32 if i % 3 == 0 else BF16 for i in range(DEPTH)]
    saved = []
    mods = list(mods)
    X = x0
    h = _res_norm(X, None, None, 0, nw[0, 0], mods[0], 0, h_dtype[0], n_lat, "norm_first")
    for i in range(DEPTH):
        kind, j, mod = i % 3, i // 3, mods[i]
        W = get_weights(i, X)
        sv = {"X": X, "h": h, "W": W}
        if kind == 0:
            y = _pool_fwd(h, W["pool_w"], P["pool_b"][j:j + 1], P["pool_scale"][j:j + 1],
                          n_lat, s_len, l_len, f"pool_fwd{i}")
        elif kind == 1:
            qkv = _mm(h, W["attn_w_qkv"], "nn", F32, f"qkv{i}")
            q, k, v = _qk_prep_fwd(qkv, P["attn_q_gain"][j:j + 1], P["attn_k_gain"][j:j + 1], cos, sin)
            o, lse = _flash_fwd(q, k, v, s_len, l_len)
            y = _mm(o, W["attn_w_o"], "nn", F32, f"attn_out{i}")
            sv.update(qkv=qkv, q=q, k=k, v=v, o=o, lse=lse)
        else:
            proj = _mm(h, W["ret_w_in"], "nn", F32, f"ret_in{i}")
            o2, states = _ret_fwd(proj, lgt, s_len, l_len)
            gated = _readout_fwd(o2, proj, P["ret_gn_w"][j:j + 1])
            y = _mm(gated, W["ret_w_out"], "nn", F32, f"ret_out{i}")
            sv.update(proj=proj, o2=o2, states=states, gated=gated)
        X1, h2 = _res_norm(X, y, mod, 0, nw[i, 1], mod, 1, BF16, n_lat, f"res_norm_mid{i}")
        u = _mm(h2, W["ffn_w_up"], "nn", F32, f"ffn_up{i}")
        gact = _conv_gate_fwd(u, P["ffn_conv_w"][i], P["ffn_conv_b"][i:i + 1], n_lat, f"conv_gate_fwd{i}")
        f = _mm(gact, W["ffn_w_down"], "nn", F32, f"ffn_down{i}")
        sv.update(y=y, X1=X1, h2=h2, u=u, gact=gact, f=f)
        saved.append(sv)
        if i + 1 < DEPTH:
            X, h = _res_norm(X1, f, mod, 1, nw[i + 1, 0], mods[i + 1], 0, h_dtype[i + 1], n_lat,
                             f"res_norm_end{i}")
        else:
            X = _res_norm(X1, f, mod, 1, None, None, 0, None, n_lat, "res_last")

    dX, loss = _loss_bwd(X, target, n_lat)
    G = {name: [None] * P[name].shape[0] for name in
         ("pool_b", "pool_scale", "attn_q_gain", "attn_k_gain", "ret_decay_logit", "ret_gn_w", "ffn_conv_w",
          "ffn_conv_b")}
    dnw = [[None, None] for _ in range(DEPTH)]
    dmods = [None] * DEPTH
    for i in reversed(range(DEPTH)):
        kind, j, mod, sv = i % 3, i // 3, mods[i], saved[i]
        W, gl = sv["W"], {}
        df, dg2 = _gate_bwd(dX, sv["f"], mod, 1, BF16, n_lat, f"gate_bwd_ffn{i}")
        dgact = _mm(df, W["ffn_w_down"], "nt", F32, f"ffn_down_dx{i}")
        gl["ffn_w_down"] = _mm(sv["gact"], df, "tn", F32, f"ffn_down_dw{i}")
        du, dcw, dcb = _conv_gate_bwd(sv["u"], dgact, P["ffn_conv_w"][i], P["ffn_conv_b"][i:i + 1], n_lat,
                                      f"conv_gate_bwd{i}")
        G["ffn_conv_w"][i], G["ffn_conv_b"][i] = dcw, dcb[0]
        dh2 = _mm(du, W["ffn_w_up"], "nt", F32, f"ffn_up_dx{i}")
        gl["ffn_w_up"] = _mm(sv["h2"], du, "tn", F32, f"ffn_up_dw{i}")
        dX1, dnw[i][1], dsh2, dsc2 = _norm_bwd(dX, dh2, sv["X1"], nw[i, 1], mod, 1, n_lat, f"norm_bwd_ffn{i}")
        dy, dg1 = _gate_bwd(dX1, sv["y"], mod, 0, F32 if kind == 0 else BF16, n_lat, f"gate_bwd_mix{i}")
        h = sv["h"]
        if kind == 0:
            dh, dpw, dpb, dps = _pool_bwd(h, dy, W["pool_w"], P["pool_b"][j:j + 1], P["pool_scale"][j:j + 1],
                                          n_lat, s_len, l_len, f"pool_bwd{i}")
            gl["pool_w"], G["pool_b"][j], G["pool_scale"][j] = dpw, dpb[0], dps[0]
        elif kind == 1:
            do = _mm(dy, W["attn_w_o"], "nt", F32, f"attn_out_dx{i}")
            gl["attn_w_o"] = _mm(sv["o"], dy, "tn", F32, f"attn_out_dw{i}")
            dq, dk, dv = _flash_bwd(sv["q"], sv["k"], sv["v"], sv["o"], sv["lse"], do, s_len, l_len)
            dqkv, dqg, dkg = _qk_prep_bwd(sv["qkv"], dq, dk, dv, P["attn_q_gain"][j:j + 1],
                                          P["attn_k_gain"][j:j + 1], cos, sin)
            G["attn_q_gain"][j], G["attn_k_gain"][j] = dqg[0], dkg[0]
            dh = _mm(dqkv, W["attn_w_qkv"], "nt", F32, f"qkv_dx{i}")
            gl["attn_w_qkv"] = _mm(h, dqkv, "tn", F32, f"qkv_dw{i}")
        else:
            dgated = _mm(dy, W["ret_w_out"], "nt", F32, f"ret_out_dx{i}")
            gl["ret_w_out"] = _mm(sv["gated"], dy, "tn", F32, f"ret_out_dw{i}")
            do, dg, dgn = _readout_bwd(sv["o2"], sv["proj"], P["ret_gn_w"][j:j + 1], dgated)
            dq2, dk2, dv2, dlg = _ret_bwd(sv["proj"], lgt, sv["states"], do, s_len, l_len)
            dproj = _ret_dproj(dq2, dk2, dv2, dg)
            G["ret_gn_w"][j], G["ret_decay_logit"][j] = dgn[0], dlg[:, :, 0, 0]
            dh = _mm(dproj, W["ret_w_in"], "nt", F32, f"ret_in_dx{i}")
            gl["ret_w_in"] = _mm(h, dproj, "tn", F32, f"ret_in_dw{i}")
        dX, dnw[i][0], dsh1, dsc1 = _norm_bwd(dX1, dh, sv["X"], nw[i, 0], mod, 0, n_lat, f"norm_bwd_mix{i}")
        dmods[i] = jnp.concatenate([dsh1, dsc1, dg1, dsh2, dsc2, dg2], axis=1)
        zero = put_grads(i, gl)
        if i > 0:
            mods[i - 1] = mods[i - 1] + zero
    grads = {name: jnp.stack(parts) for name, parts in G.items()}
    grads["norm_w"] = jnp.stack([jnp.concatenate(pair, axis=0) for pair in dnw])
    return loss, dX, grads, jnp.stack(dmods)


def kernel(x, c, ctx, c_ctx, ada_w, ada_b, norm_w, pool_w, pool_b, pool_scale, attn_w_qkv, attn_q_gain,
           attn_k_gain, attn_w_o, ret_w_in, ret_decay_logit, ret_gn_w, ret_w_out, ffn_w_up, ffn_conv_w,
           ffn_conv_b, ffn_w_down, loss_target, m_c_ctx, m_ada_w, m_ada_b, m_norm_w, m_pool_w, m_pool_b,
           m_pool_scale, m_attn_w_qkv, m_attn_q_gain, m_attn_k_gain, m_attn_w_o, m_ret_w_in,
           m_ret_decay_logit, m_ret_gn_w, m_ret_w_out, m_ffn_w_up, m_ffn_conv_w, m_ffn_conv_b, m_ffn_w_down,
           v_c_ctx, v_ada_w, v_ada_b, v_norm_w, v_pool_w, v_pool_b, v_pool_scale, v_attn_w_qkv, v_attn_q_gain,
           v_attn_k_gain, v_attn_w_o, v_ret_w_in, v_ret_decay_logit, v_ret_gn_w, v_ret_w_out, v_ffn_w_up,
           v_ffn_conv_w, v_ffn_conv_b, v_ffn_w_down):
    A = dict(locals())
    me = _index(_position())
    s_len, D = x.shape[1], x.shape[2]
    l_len = ctx.shape[1]
    assert s_len % ROW_TILE == 0 and l_len % ROW_TILE == 0 and s_len % GRID_W == 0

    small = [A[n] for n, _ in SMALL_SHARDED]
    got = _gather_small(_pack_rows([c] + small, 128, F32), "gather_c_small")
    parts = _unpack_rows(got, [c.shape] + [a.shape for a in small])
    c_all = parts[0].reshape(N_DEV, D)
    P = {n: _unshard(g8, ax) for (n, ax), g8 in zip(SMALL_SHARDED, parts[1:])}

    c_rows = jnp.concatenate([c_all, c_ctx.reshape(1, D), jnp.zeros((7, D), F32)], axis=0)
    cols = ada_w.shape[2]
    ada_b_shard = lax.dynamic_slice_in_dim(ada_b, me * cols, cols, axis=1).reshape(DEPTH, 1, cols)
    mod_shard = _ada_fwd(c_rows, ada_w, ada_b_shard)
    got = _gather_small(mod_shard.reshape(-1, 128), "gather_mod").reshape(N_DEV, DEPTH, 16, cols)
    mod_lat = lax.dynamic_index_in_dim(got, me, axis=2, keepdims=False)
    mod_ctx = got[:, :, 8, :]
    mods = jnp.stack([jnp.moveaxis(mod_lat, 0, 1).reshape(DEPTH, 6, D),
                      jnp.moveaxis(mod_ctx, 0, 1).reshape(DEPTH, 6, D)], axis=1)

    shards = [[A[n][j].astype(BF16) for n, j, _ in lw] for lw in LAYER_WEIGHTS]
    packs = [jnp.concatenate([s.reshape(-1, D) for s in layer], axis=0) for layer in shards]

    def unpack_layer(i, piece):
        out, off = {}, 0
        for (n, j, ax), shard in zip(LAYER_WEIGHTS[i], shards[i]):
            r = shard.size // D
            out[(n, j)] = piece(off, r, shard.shape, ax)
            off += r
        return out

    def full_weights(i, got):
        w = unpack_layer(i, lambda off, r, shp, ax: _unshard(got[:, off:off + r].reshape((N_DEV,) + shp), ax))
        return {n: a for (n, j), a in w.items()}

    first = full_weights(0, _gather_big(packs[0], "gather_weights0"))
    flights, zero = {}, jnp.zeros((), F32)
    for i in range(1, DEPTH):
        flights[i] = _split_start(packs[i], jnp.zeros((N_DEV,) + packs[i].shape, BF16), True, f"gather_start{i}")
        zero = zero + flights[i][4][0, 0]
    mods = [mods[i] for i in range(DEPTH)]
    mods[0] = mods[0] + zero
    for n in REPLICATED:
        P[n] = A[n]

    def get_weights(i, x_now):
        if i == 0:
            return first
        send_sem, recv_sem, src_thru, land_thru, _ = flights[i]
        own, land = _split_wait(send_sem, recv_sem, src_thru, land_thru, x_now, f"gather_wait{i}")
        return full_weights(i, lax.dynamic_update_index_in_dim(land, own, me, axis=0))

    sent = {}

    def put_grads(i, gl):
        send = jnp.concatenate([_split8(gl[n], ax).astype(BF16).reshape(N_DEV, -1, D)
                                for n, j, ax in LAYER_WEIGHTS[i]], axis=1)
        sent[i] = _split_start(send, jnp.zeros(send.shape, BF16), False, f"exchange_start{i}")
        return sent[i][4][0, 0]

    x0 = jnp.concatenate([x[0], ctx[0]], axis=0)
    loss8, dx0, G, dmods = _local_step(x0, loss_target[0], mods, P, get_weights, put_grads, s_len, l_len)
    loss = lax.psum(loss8[0, 0], ("x", "y", "c"))
    grad_x = dx0[:s_len].reshape(x.shape)

    small_names = ["dmods"] + list(REPLICATED[1:]) + [n for n, _ in SMALL_SHARDED]
    small_parts = [dmods] + [G[n] for n in small_names[1:]]
    got = _gather_small(_pack_rows(small_parts, 128, F32), "gather_small_grads")
    S8 = dict(zip(small_names, _unpack_rows(got, [a.shape for a in small_parts])))

    dm = S8["dmods"].reshape(N_DEV, DEPTH, 2, 6 * D)
    dm_mine = lax.dynamic_slice_in_dim(dm, me * cols, cols, axis=3)
    g_ada_w, pc = _ada_bwd(c_rows, ada_w, jnp.moveaxis(dm_mine[:, :, 0], 0, 1), jnp.moveaxis(dm_mine[:, :, 1], 0, 1))
    pc8 = _gather_small(pc.reshape(-1, 128), "gather_c_ctx_grad").reshape(N_DEV, 8, D)

    shard_grads = {}
    for i in range(DEPTH):
        send_sem, recv_sem, src_thru, land_thru, _ = sent[i]
        send, land = _split_wait(send_sem, recv_sem, src_thru, land_thru, pc8, f"exchange_wait{i}")
        gsum = _sum_slots(lax.dynamic_index_in_dim(send, me, axis=0, keepdims=False), land, f"sum_slots{i}")
        shard_grads.update(unpack_layer(i, lambda off, r, shp, ax: gsum[off:off + r].reshape(shp)))

    g_in = {"c_ctx": pc8[:, 0, :], "ada_w": g_ada_w[None],
            "ada_b": jnp.moveaxis(dm, 2, 1).reshape(2 * N_DEV, DEPTH, 6 * D)}
    for n in REPLICATED[1:]:
        g_in[n] = S8[n]
    for n, ax in SMALL_SHARDED:
        g_in[n] = _my_shard(S8[n], ax, me)
    for n in BIG_WEIGHTS:
        g_in[n] = jnp.stack([shard_grads[(n, j)] for j in range(A[n].shape[0])])[None]
    res = {n: _adamw(A[n], g_in[n], A["m_" + n], A["v_" + n], "adamw_" + n) for n in WEIGHT_ORDER}
    outs = [loss, grad_x]
    for slot in range(4):
        outs += [res[n][slot] for n in WEIGHT_ORDER]
    return tuple(outs)
```

```python
import functools
import math

import jax
import jax.numpy as jnp
from jax import lax
from jax.experimental import pallas as pl
from jax.experimental.pallas import tpu as pltpu

F32 = jnp.float32
BF16 = jnp.bfloat16
SDS = jax.ShapeDtypeStruct
MESH = pl.DeviceIdType.MESH

N_DEV = 8
EPS = 1e-6
DEPTH = 4
GRID_W = 64
POOL_WINDOWS = (2, 4, 8, 16)
N_HEADS = 8
N_KV = 2
HEAD_DIM = 128
ROPE_THETA = 10000.0
RET_HEADS = 4
RET_DK = 256
RET_DV = 512
RET_CHUNK = 128
ADAM_LR = 0.001
ADAM_B1 = 0.9
ADAM_B2 = 0.999
ADAM_EPS = 1e-08
ADAM_WD = 0.01
ADAM_STEP = 10

ROW_TILE = 256
HALO = 8
VMEM_LIMIT_V7X = 56 * 1024 * 1024


def _params(n_axes=0):
    sem = ("arbitrary",) * n_axes if n_axes else None
    return pltpu.CompilerParams(dimension_semantics=sem, vmem_limit_bytes=VMEM_LIMIT_V7X)


def _pick(n, cap, mult):
    best = None
    for d in range(mult, min(n, cap) + 1, mult):
        if n % d == 0:
            best = d
    return best if best is not None else n


def _dot(a, b):
    return jnp.dot(a, b, preferred_element_type=F32)


def _dot_nt(a, b):
    return lax.dot_general(a, b, (((1,), (1,)), ((), ())), preferred_element_type=F32)


def _dot_tn(a, b):
    return lax.dot_general(a, b, (((0,), (0,)), ((), ())), preferred_element_type=F32)


def _bf(v):
    return v.astype(BF16)


def _sigmoid(v):
    return 0.5 * jnp.tanh(0.5 * v) + 0.5


MM_VMEM_BUDGET = 40 * 1024 * 1024
MM_STEP_BYTES = 1 << 20


def _divisors(n, mult, cap):
    return [d for d in range(mult, min(n, cap) + 1, mult) if n % d == 0] or [n]


def _mm_tiles(mode, M, N, K, a_item, b_item, o_item):
    best = None
    for tm in _divisors(M, 128 if mode == "tn" else 16, 2816):
        for tn in _divisors(N, 128, 2048):
            for tk in _divisors(K, 16 if mode == "tn" else 128, 2816):
                ni, nj, nk = M // tm, N // tn, K // tk
                vmem = 2 * (tm * tk * a_item + tk * tn * b_item + tm * tn * o_item) + tm * tn * 4
                if vmem > MM_VMEM_BUDGET:
                    continue
                a_reads = 1 if nk == 1 else nj
                b_reads = 1 if (nk == 1 and nj == 1) else ni
                cost = (M * K * a_item * a_reads + K * N * b_item * b_reads + M * N * o_item
                        + ni * nj * nk * MM_STEP_BYTES + (nk - 1) * M * N)
                if best is None or cost < best[0]:
                    best = (cost, tm, tn, tk)
    return best[1:]


def _mm(a, b, mode, out_dtype, name):
    if mode == "nn":
        (M, K), (K2, N) = a.shape, b.shape
    elif mode == "nt":
        (M, K), (N, K2) = a.shape, b.shape
    else:
        (K, M), (K2, N) = a.shape, b.shape
    assert K == K2, (a.shape, b.shape, mode)
    tm, tn, tk = _mm_tiles(mode, M, N, K, a.dtype.itemsize, b.dtype.itemsize, jnp.dtype(out_dtype).itemsize)
    nk = K // tk
    if mode == "nn":
        a_spec = pl.BlockSpec((tm, tk), lambda i, j, k: (i, k))
        b_spec = pl.BlockSpec((tk, tn), lambda i, j, k: (k, j))
    elif mode == "nt":
        a_spec = pl.BlockSpec((tm, tk), lambda i, j, k: (i, k))
        b_spec = pl.BlockSpec((tn, tk), lambda i, j, k: (j, k))
    else:
        a_spec = pl.BlockSpec((tk, tm), lambda i, j, k: (k, i))
        b_spec = pl.BlockSpec((tk, tn), lambda i, j, k: (k, j))
    dot = {"nn": _dot, "nt": _dot_nt, "tn": _dot_tn}[mode]

    def body(a_ref, b_ref, o_ref, acc_ref):
        part = dot(_bf(a_ref[...]), _bf(b_ref[...]))
        if nk == 1:
            o_ref[...] = part.astype(out_dtype)
        else:
            k = pl.program_id(2)

            @pl.when(k == 0)
            def _():
                acc_ref[...] = part

            @pl.when(k > 0)
            def _():
                acc_ref[...] += part

            @pl.when(k == nk - 1)
            def _():
                o_ref[...] = acc_ref[...].astype(out_dtype)

    return pl.pallas_call(
        body, name=name, grid=(M // tm, N // tn, nk),
        in_specs=[a_spec, b_spec],
        out_specs=pl.BlockSpec((tm, tn), lambda i, j, k: (i, j)),
        out_shape=SDS((M, N), out_dtype),
        scratch_shapes=[pltpu.VMEM((tm, tn), F32)],
        compiler_params=_params(3),
    )(a, b)


def _seg_spec(n_lat, d):
    return pl.BlockSpec((1, 6, d), lambda i: ((i >= n_lat).astype(jnp.int32), 0, 0))


def _seg_acc_spec(n_lat, d):
    return pl.BlockSpec((1, 1, d), lambda i: ((i >= n_lat).astype(jnp.int32), 0, 0))


def _res_norm(x, y, gmod, gk, nw, nmod, nk, h_dtype, n_lat, name):
    R, D = x.shape
    has_res, has_norm = y is not None, nw is not None
    row = pl.BlockSpec((ROW_TILE, D), lambda i: (i, 0))
    vec = pl.BlockSpec((1, D), lambda i: (0, 0))
    ins, specs, outs, ospecs = [x], [row], [], []
    if has_res:
        ins += [y, gmod]
        specs += [row, _seg_spec(n_lat, D)]
        outs.append(SDS((R, D), F32))
        ospecs.append(row)
    if has_norm:
        ins += [nw.reshape(1, D), nmod]
        specs += [vec, _seg_spec(n_lat, D)]
        outs.append(SDS((R, D), h_dtype))
        ospecs.append(row)

    def body(*refs):
        refs = list(refs)
        z = refs.pop(0)[...]
        if has_res:
            y_ref, g_ref = refs.pop(0), refs.pop(0)
            z = z + g_ref[0, pl.ds(3 * gk + 2, 1), :] * y_ref[...].astype(F32)
        if has_norm:
            nw_ref, m_ref = refs.pop(0), refs.pop(0)
        if has_res:
            refs.pop(0)[...] = z
        if has_norm:
            r = lax.rsqrt(jnp.mean(z * z, axis=-1, keepdims=True) + EPS)
            h = (z * r) * nw_ref[...]
            h = h * (1.0 + m_ref[0, pl.ds(3 * nk + 1, 1), :]) + m_ref[0, pl.ds(3 * nk, 1), :]
            refs.pop(0)[...] = h.astype(h_dtype)

    res = pl.pallas_call(
        body, name=name, grid=(R // ROW_TILE,), in_specs=specs, out_specs=ospecs,
        out_shape=outs, compiler_params=_params(1),
    )(*ins)
    return res if len(res) > 1 else res[0]


def _gate_bwd(dz, y, mod, k, out_dtype, n_lat, name):
    R, D = dz.shape
    row = pl.BlockSpec((ROW_TILE, D), lambda i: (i, 0))

    def body(dz_ref, y_ref, m_ref, dy_ref, dg_ref):
        i = pl.program_id(0)
        dzv = dz_ref[...]
        dy_ref[...] = (m_ref[0, pl.ds(3 * k + 2, 1), :] * dzv).astype(out_dtype)

        @pl.when((i == 0) | (i == n_lat))
        def _():
            dg_ref[...] = jnp.zeros_like(dg_ref)

        dg_ref[0] += jnp.sum(dzv * y_ref[...].astype(F32), axis=0, keepdims=True)

    return pl.pallas_call(
        body, name=name, grid=(R // ROW_TILE,),
        in_specs=[row, row, _seg_spec(n_lat, D)],
        out_specs=[row, _seg_acc_spec(n_lat, D)],
        out_shape=[SDS((R, D), out_dtype), SDS((2, 1, D), F32)],
        compiler_params=_params(1),
    )(dz, y, mod)


def _norm_bwd(dz, dh, x, nw, mod, k, n_lat, name):
    R, D = x.shape
    row = pl.BlockSpec((ROW_TILE, D), lambda i: (i, 0))
    vec = pl.BlockSpec((1, D), lambda i: (0, 0))

    def body(dz_ref, dh_ref, x_ref, nw_ref, m_ref, dx_ref, dnw_ref, dsh_ref, dsc_ref):
        i = pl.program_id(0)
        xv = x_ref[...]
        dhv = dh_ref[...].astype(F32)
        nwv = nw_ref[...]
        sc1 = 1.0 + m_ref[0, pl.ds(3 * k + 1, 1), :]
        r = lax.rsqrt(jnp.mean(xv * xv, axis=-1, keepdims=True) + EPS)
        xhat = xv * r
        a = dhv * (nwv * sc1)
        dx_ref[...] = dz_ref[...] + r * (a - xhat * jnp.mean(a * xhat, axis=-1, keepdims=True))

        @pl.when(i == 0)
        def _():
            dnw_ref[...] = jnp.zeros_like(dnw_ref)

        @pl.when((i == 0) | (i == n_lat))
        def _():
            dsh_ref[...] = jnp.zeros_like(dsh_ref)
            dsc_ref[...] = jnp.zeros_like(dsc_ref)

        dnw_ref[...] += jnp.sum(dhv * xhat, axis=0, keepdims=True) * sc1
        dsh_ref[0] += jnp.sum(dhv, axis=0, keepdims=True)
        dsc_ref[0] += jnp.sum(dhv * xhat, axis=0, keepdims=True) * nwv

    return pl.pallas_call(
        body, name=name, grid=(R // ROW_TILE,),
        in_specs=[row, row, row, vec, _seg_spec(n_lat, D)],
        out_specs=[row, vec, _seg_acc_spec(n_lat, D), _seg_acc_spec(n_lat, D)],
        out_shape=[SDS((R, D), F32), SDS((1, D), F32), SDS((2, 1, D), F32), SDS((2, 1, D), F32)],
        compiler_params=_params(1),
    )(dz, dh, x, nw.reshape(1, D), mod)


def _loss_bwd(xf, target, n_lat):
    R, D = xf.shape
    row = pl.BlockSpec((ROW_TILE, D), lambda i: (i, 0))
    tgt = pl.BlockSpec((ROW_TILE, D), lambda i: (jnp.minimum(i, n_lat - 1), 0))

    def body(x_ref, t_ref, dx_ref, loss_ref):
        i = pl.program_id(0)
        e = jnp.where(i < n_lat, x_ref[...] - t_ref[...], 0.0)
        dx_ref[...] = e * (1.0 / D)

        @pl.when(i == 0)
        def _():
            loss_ref[...] = jnp.zeros_like(loss_ref)

        loss_ref[...] += 0.5 * jnp.sum(jnp.mean(e * e, axis=-1, keepdims=True))

    return pl.pallas_call(
        body, name="loss_bwd", grid=(R // ROW_TILE,),
        in_specs=[row, tgt],
        out_specs=[row, pl.BlockSpec((8, 128), lambda i: (0, 0))],
        out_shape=[SDS((R, D), F32), SDS((8, 128), F32)],
        compiler_params=_params(1),
    )(xf, target)


def _halo_specs(n_tiles, width, tile=ROW_TILE):
    per = tile // HALO
    prev = pl.BlockSpec((HALO, width), lambda i: (jnp.maximum(i * per - 1, 0), 0))
    nxt = pl.BlockSpec((HALO, width), lambda i: (jnp.minimum((i + 1) * per, n_tiles * per - 1), 0))
    return prev, nxt


def _edge_flags(i, n_lat, n_tiles):
    first = (i == 0) | (i == n_lat)
    last = (i == n_lat - 1) | (i == n_tiles - 1)
    return first, last


def _conv_gate_fwd(u, conv_w, conv_b, n_lat, name):
    R, F2 = u.shape
    F = F2 // 2
    n_tiles = R // ROW_TILE
    T = ROW_TILE
    cw = _pick(F, 256, 128)
    row = pl.BlockSpec((T, F2), lambda i: (i, 0))
    prev, nxt = _halo_specs(n_tiles, F2)

    def body(u_ref, p_ref, n_ref, w_ref, b_ref, o_ref):
        i = pl.program_id(0)
        first, last = _edge_flags(i, n_lat, n_tiles)
        ridx = lax.broadcasted_iota(jnp.int32, (T, 1), 0)

        def conv(c0):
            cols = pl.ds(c0, cw)
            uv = u_ref[:, cols]
            pr = jnp.where(first, 0.0, p_ref[pl.ds(HALO - 1, 1), cols])
            nx = jnp.where(last, 0.0, n_ref[pl.ds(0, 1), cols])
            up = jnp.where(ridx == 0, pr, pltpu.roll(uv, 1, 0))
            un = jnp.where(ridx == T - 1, nx, pltpu.roll(uv, T - 1, 0))
            return (up * w_ref[pl.ds(0, 1), cols] + uv * w_ref[pl.ds(1, 1), cols]
                    + un * w_ref[pl.ds(2, 1), cols] + b_ref[:, cols])

        for c0 in range(0, F, cw):
            ca, cv = conv(c0), conv(F + c0)
            o_ref[:, pl.ds(c0, cw)] = (ca * _sigmoid(ca) * cv).astype(BF16)

    return pl.pallas_call(
        body, name=name, grid=(n_tiles,),
        in_specs=[row, prev, nxt, pl.BlockSpec((3, F2), lambda i: (0, 0)),
                  pl.BlockSpec((1, F2), lambda i: (0, 0))],
        out_specs=pl.BlockSpec((T, F), lambda i: (i, 0)),
        out_shape=SDS((R, F), BF16), compiler_params=_params(1),
    )(u, u, u, conv_w, conv_b)


def _conv_gate_bwd(u, dgact, conv_w, conv_b, n_lat, name):
    R, F2 = u.shape
    F = F2 // 2
    n_tiles = R // ROW_TILE
    T, N = ROW_TILE, ROW_TILE + 2 * HALO
    cw = _pick(F, 256, 128)
    rowu = pl.BlockSpec((T, F2), lambda i: (i, 0))
    rowg = pl.BlockSpec((T, F), lambda i: (i, 0))
    pu, nu = _halo_specs(n_tiles, F2)
    pg, ng = _halo_specs(n_tiles, F)

    def body(u_ref, pu_ref, nu_ref, g_ref, pg_ref, ng_ref, w_ref, b_ref, du_ref, dw_ref, db_ref):
        i = pl.program_id(0)
        first, last = _edge_flags(i, n_lat, n_tiles)

        @pl.when(i == 0)
        def _():
            dw_ref[...] = jnp.zeros_like(dw_ref)
            db_ref[...] = jnp.zeros_like(db_ref)

        def ext(t_ref, p_ref, n_ref, cols):
            pr = jnp.where(first, 0.0, p_ref[:, cols])
            nx = jnp.where(last, 0.0, n_ref[:, cols])
            return jnp.concatenate([pr, t_ref[:, cols], nx], axis=0)

        def conv(c0):
            cols = pl.ds(c0, cw)
            e = ext(u_ref, pu_ref, nu_ref, cols)
            up, un = pltpu.roll(e, 1, 0), pltpu.roll(e, N - 1, 0)
            c = (up * w_ref[pl.ds(0, 1), cols] + e * w_ref[pl.ds(1, 1), cols]
                 + un * w_ref[pl.ds(2, 1), cols] + b_ref[:, cols])
            return c, up, e, un

        def back(c0, dc, up, e, un):
            cols = pl.ds(c0, cw)
            du = (pltpu.roll(dc, N - 1, 0) * w_ref[pl.ds(0, 1), cols] + dc * w_ref[pl.ds(1, 1), cols]
                  + pltpu.roll(dc, 1, 0) * w_ref[pl.ds(2, 1), cols])
            du_ref[:, cols] = du[HALO:HALO + T].astype(BF16)
            dct = dc[HALO:HALO + T]
            dw_ref[pl.ds(0, 1), cols] += jnp.sum(dct * up[HALO:HALO + T], axis=0, keepdims=True)
            dw_ref[pl.ds(1, 1), cols] += jnp.sum(dct * e[HALO:HALO + T], axis=0, keepdims=True)
            dw_ref[pl.ds(2, 1), cols] += jnp.sum(dct * un[HALO:HALO + T], axis=0, keepdims=True)
            db_ref[:, cols] += jnp.sum(dct, axis=0, keepdims=True)

        for c0 in range(0, F, cw):
            dg = ext(g_ref, pg_ref, ng_ref, pl.ds(c0, cw))
            ca, upa, ea, una = conv(c0)
            cv, upv, ev, unv = conv(F + c0)
            s = _sigmoid(ca)
            back(F + c0, dg * (ca * s), upv, ev, unv)
            back(c0, dg * cv * (s * (1.0 + ca * (1.0 - s))), upa, ea, una)

    return pl.pallas_call(
        body, name=name, grid=(n_tiles,),
        in_specs=[rowu, pu, nu, rowg, pg, ng, pl.BlockSpec((3, F2), lambda i: (0, 0)),
                  pl.BlockSpec((1, F2), lambda i: (0, 0))],
        out_specs=[rowu, pl.BlockSpec((3, F2), lambda i: (0, 0)), pl.BlockSpec((1, F2), lambda i: (0, 0))],
        out_shape=[SDS((R, F2), BF16), SDS((3, F2), F32), SDS((1, F2), F32)],
        compiler_params=_params(1),
    )(u, u, u, dgact, dgact, dgact, conv_w, conv_b)


def _pool_counts(i, n_lat, s_len, l_len, n_rows, offset):
    ctx = i >= n_lat
    t0 = jnp.where(ctx, i - n_lat, i) * ROW_TILE + offset
    seg = jnp.where(ctx, l_len, s_len)
    t = t0 + lax.broadcasted_iota(jnp.int32, (n_rows, 1), 0)
    out = []
    for win in POOL_WINDOWS:
        cnt = jnp.minimum(t + win // 2, seg) - jnp.maximum(t - win // 2, 0)
        out.append(jnp.maximum(cnt, 1).astype(F32))
    return out


def _window_sum(e, lo, hi, n):
    acc = None
    for j in range(lo, hi + 1):
        term = e if j == 0 else pltpu.roll(e, (-j) % n, 0)
        acc = term if acc is None else acc + term
    return acc


def _pool_fwd(h, w, b, scale, n_lat, s_len, l_len, name):
    R, D = h.shape
    G = D // 4
    n_tiles = R // ROW_TILE
    T, N = ROW_TILE, ROW_TILE + 2 * HALO
    row = pl.BlockSpec((T, D), lambda i: (i, 0))
    prev, nxt = _halo_specs(n_tiles, D)
    vec = pl.BlockSpec((1, D), lambda i: (0, 0))

    def body(h_ref, p_ref, n_ref, w_ref, b_ref, s_ref, y_ref):
        i = pl.program_id(0)
        first, last = _edge_flags(i, n_lat, n_tiles)
        cnts = _pool_counts(i, n_lat, s_len, l_len, T, 0)
        for g, win in enumerate(POOL_WINDOWS):
            cols = pl.ds(g * G, G)
            pr = jnp.where(first, 0.0, p_ref[:, cols])
            nx = jnp.where(last, 0.0, n_ref[:, cols])
            hv = h_ref[:, cols]
            e = jnp.concatenate([pr, hv, nx], axis=0)
            mean = _window_sum(e, -(win // 2), win // 2 - 1, N)[HALO:HALO + T] / cnts[g]
            yg = _dot(_bf(mean - hv), w_ref[g])
            y_ref[:, cols] = (yg + b_ref[:, cols]) * s_ref[:, cols]

    return pl.pallas_call(
        body, name=name, grid=(n_tiles,),
        in_specs=[row, prev, nxt, pl.BlockSpec((4, G, G), lambda i: (0, 0, 0)), vec, vec],
        out_specs=row, out_shape=SDS((R, D), F32), compiler_params=_params(1),
    )(h, h, h, w, b, scale)


def _pool_bwd(h, dy, w, b, scale, n_lat, s_len, l_len, name):
    R, D = h.shape
    G = D // 4
    n_tiles = R // ROW_TILE
    T, N = ROW_TILE, ROW_TILE + 2 * HALO
    row = pl.BlockSpec((T, D), lambda i: (i, 0))
    prev, nxt = _halo_specs(n_tiles, D)
    vec = pl.BlockSpec((1, D), lambda i: (0, 0))
    wspec = pl.BlockSpec((4, G, G), lambda i: (0, 0, 0))

    def body(h_ref, ph_ref, nh_ref, d_ref, pd_ref, nd_ref, w_ref, b_ref, s_ref,
             dh_ref, dw_ref, db_ref, ds_ref):
        i = pl.program_id(0)
        first, last = _edge_flags(i, n_lat, n_tiles)

        @pl.when(i == 0)
        def _():
            dw_ref[...] = jnp.zeros_like(dw_ref)
            db_ref[...] = jnp.zeros_like(db_ref)
            ds_ref[...] = jnp.zeros_like(ds_ref)

        cnts = _pool_counts(i, n_lat, s_len, l_len, T, 0)
        cnts_ext = _pool_counts(i, n_lat, s_len, l_len, N, -HALO)
        for g, win in enumerate(POOL_WINDOWS):
            cols = pl.ds(g * G, G)

            def ext(t_ref, p_ref, n_ref):
                pr = jnp.where(first, 0.0, p_ref[:, cols])
                nx = jnp.where(last, 0.0, n_ref[:, cols])
                return jnp.concatenate([pr, t_ref[:, cols], nx], axis=0)

            hv = h_ref[:, cols]
            mean = _window_sum(ext(h_ref, ph_ref, nh_ref), -(win // 2), win // 2 - 1, N)[HALO:HALO + T] / cnts[g]
            z = _bf(mean - hv)
            sc = s_ref[:, cols]
            dye = ext(d_ref, pd_ref, nd_ref)
            dt = _bf(dye * sc)
            dz = _dot_nt(dt, w_ref[g])
            dm = dz / cnts_ext[g]
            dh = _window_sum(dm, -(win // 2 - 1), win // 2, N) - dz
            dh_ref[:, cols] = dh[HALO:HALO + T]
            dyt = dye[HALO:HALO + T]
            dw_ref[g] += _dot_tn(z, dt[HALO:HALO + T])
            db_ref[:, cols] += jnp.sum(dyt * sc, axis=0, keepdims=True)
            ds_ref[:, cols] += jnp.sum(dyt * (_dot(z, w_ref[g]) + b_ref[:, cols]), axis=0, keepdims=True)

    return pl.pallas_call(
        body, name=name, grid=(n_tiles,),
        in_specs=[row, prev, nxt, row, prev, nxt, wspec, vec, vec],
        out_specs=[row, wspec, vec, vec],
        out_shape=[SDS((R, D), F32), SDS((4, G, G), F32), SDS((1, D), F32), SDS((1, D), F32)],
        compiler_params=_params(1),
    )(h, h, h, dy, dy, dy, w, b, scale)


def _rope_tables(s_len, l_len):
    t = jnp.arange(s_len)
    row = (t // GRID_W).astype(F32)
    col = (t % GRID_W).astype(F32)
    axis_dim = HEAD_DIM // 2
    inv = ROPE_THETA ** (-jnp.arange(0, axis_dim, 2, dtype=F32) / axis_dim)
    ar, ac = row[:, None] * inv, col[:, None] * inv
    cos = jnp.concatenate([jnp.cos(ar), jnp.cos(ar), jnp.cos(ac), jnp.cos(ac)], axis=-1)
    sin = jnp.concatenate([-jnp.sin(ar), jnp.sin(ar), -jnp.sin(ac), jnp.sin(ac)], axis=-1)
    cos = jnp.concatenate([cos, jnp.ones((l_len, HEAD_DIM), F32)], axis=0)
    sin = jnp.concatenate([sin, jnp.zeros((l_len, HEAD_DIM), F32)], axis=0)
    return cos, sin


def _swap_halves(v):
    lane = lax.broadcasted_iota(jnp.int32, v.shape, 1)
    return jnp.where((lane % 64) < 32, pltpu.roll(v, 96, 1), pltpu.roll(v, 32, 1))


def _qk_prep_fwd(qkv, q_gain, k_gain, cos, sin):
    R = qkv.shape[0]
    NQ, NK = N_HEADS * HEAD_DIM, N_KV * HEAD_DIM
    T = ROW_TILE
    vec = pl.BlockSpec((1, HEAD_DIM), lambda i: (0, 0))
    tab = pl.BlockSpec((T, HEAD_DIM), lambda i: (i, 0))

    def body(x_ref, qg_ref, kg_ref, c_ref, s_ref, q_ref, k_ref, v_ref):
        cosv, sinv = c_ref[...], s_ref[...]

        def prep(c0, gain):
            xh = x_ref[:, pl.ds(c0, HEAD_DIM)]
            xn = xh * lax.rsqrt(jnp.mean(xh * xh, axis=-1, keepdims=True) + EPS) * gain
            return _bf(xn * cosv + _swap_halves(xn) * sinv)

        for hd in range(N_HEADS):
            q_ref[:, pl.ds(hd * HEAD_DIM, HEAD_DIM)] = prep(hd * HEAD_DIM, qg_ref[...])
        for hd in range(N_KV):
            k_ref[:, pl.ds(hd * HEAD_DIM, HEAD_DIM)] = prep(NQ + hd * HEAD_DIM, kg_ref[...])
            v_ref[:, pl.ds(2 * hd * HEAD_DIM, HEAD_DIM)] = _bf(x_ref[:, pl.ds(NQ + NK + hd * HEAD_DIM, HEAD_DIM)])
            v_ref[:, pl.ds((2 * hd + 1) * HEAD_DIM, HEAD_DIM)] = jnp.ones((T, HEAD_DIM), BF16)

    return pl.pallas_call(
        body, name="qk_prep_fwd", grid=(R // T,),
        in_specs=[pl.BlockSpec((T, NQ + 2 * NK), lambda i: (i, 0)), vec, vec, tab, tab],
        out_specs=[pl.BlockSpec((T, NQ), lambda i: (i, 0)), pl.BlockSpec((T, NK), lambda i: (i, 0)),
                   pl.BlockSpec((T, 2 * NK), lambda i: (i, 0))],
        out_shape=[SDS((R, NQ), BF16), SDS((R, NK), BF16), SDS((R, 2 * NK), BF16)],
        compiler_params=_params(1),
    )(qkv, q_gain, k_gain, cos, sin)


def _qk_prep_bwd(qkv, dq, dk, dv, q_gain, k_gain, cos, sin):
    R = qkv.shape[0]
    NQ, NK = N_HEADS * HEAD_DIM, N_KV * HEAD_DIM
    T = ROW_TILE
    vec = pl.BlockSpec((1, HEAD_DIM), lambda i: (0, 0))
    tab = pl.BlockSpec((T, HEAD_DIM), lambda i: (i, 0))

    def body(x_ref, dq_ref, dk_ref, dv_ref, qg_ref, kg_ref, c_ref, s_ref, o_ref, dqg_ref, dkg_ref):
        i = pl.program_id(0)
        cosv, sinv = c_ref[...], s_ref[...]

        @pl.when(i == 0)
        def _():
            dqg_ref[...] = jnp.zeros_like(dqg_ref)
            dkg_ref[...] = jnp.zeros_like(dkg_ref)

        def back(c0, dout, gain, dg_ref):
            xh = x_ref[:, pl.ds(c0, HEAD_DIM)]
            r = lax.rsqrt(jnp.mean(xh * xh, axis=-1, keepdims=True) + EPS)
            xhat = xh * r
            dxn = dout * cosv + _swap_halves(dout * sinv)
            dg_ref[...] += jnp.sum(dxn * xhat, axis=0, keepdims=True)
            a = dxn * gain
            o_ref[:, pl.ds(c0, HEAD_DIM)] = _bf(r * (a - xhat * jnp.mean(a * xhat, axis=-1, keepdims=True)))

        for hd in range(N_HEADS):
            back(hd * HEAD_DIM, dq_ref[:, pl.ds(hd * HEAD_DIM, HEAD_DIM)], qg_ref[...], dqg_ref)
        for hd in range(N_KV):
            back(NQ + hd * HEAD_DIM, dk_ref[:, pl.ds(hd * HEAD_DIM, HEAD_DIM)], kg_ref[...], dkg_ref)
        o_ref[:, pl.ds(NQ + NK, NK)] = _bf(dv_ref[...])

    return pl.pallas_call(
        body, name="qk_prep_bwd", grid=(R // T,),
        in_specs=[pl.BlockSpec((T, NQ + 2 * NK), lambda i: (i, 0)), pl.BlockSpec((T, NQ), lambda i: (i, 0)),
                  pl.BlockSpec((T, NK), lambda i: (i, 0)), pl.BlockSpec((T, NK), lambda i: (i, 0)),
                  vec, vec, tab, tab],
        out_specs=[pl.BlockSpec((T, NQ + 2 * NK), lambda i: (i, 0)), vec, vec],
        out_shape=[SDS((R, NQ + 2 * NK), BF16), SDS((1, HEAD_DIM), F32), SDS((1, HEAD_DIM), F32)],
        compiler_params=_params(1),
    )(qkv, dq, dk, dv, q_gain, k_gain, cos, sin)


def _flash_fwd(q, k, v, s_len, l_len):
    R = q.shape[0]
    T = ROW_TILE
    n_lat = s_len // T
    ck = _pick(s_len, 512, 128)
    scale = HEAD_DIM ** -0.5
    group = N_HEADS // N_KV
    GW = group * HEAD_DIM

    M = group * T
    to_log2 = scale * math.log2(math.e)

    def body(q_ref, k_ref, v_ref, o_ref, lse_ref, m8_s, mb_s, acc_s):
        i = pl.program_id(1)
        qv = jnp.concatenate([q_ref[:, pl.ds(hh * HEAD_DIM, HEAD_DIM)] for hh in range(group)], axis=0)
        chunks = s_len // ck

        m8_s[...] = jnp.full_like(m8_s, -jnp.inf)

        def stat(rows, n):
            st = _dot_nt(k_ref[rows, :], qv)
            m8_s[...] = jnp.maximum(m8_s[...], jnp.max(st.reshape(n // 8, 8, M), axis=0))

        @pl.when(i < n_lat)
        def _():
            def loop(c, carry):
                stat(pl.ds(pl.multiple_of(c * ck, ck), ck), ck)
                return carry
            lax.fori_loop(0, chunks, loop, 0)

        stat(pl.ds(s_len, l_len), l_len)
        m_row = jnp.max(m8_s[...], axis=0, keepdims=True) * to_log2
        mb = jnp.broadcast_to(m_row, (HEAD_DIM, M)).T
        mb_s[...] = jnp.concatenate([mb] * (ck // HEAD_DIM), axis=1)

        acc_s[...] = jnp.zeros_like(acc_s)

        def step(rows, n):
            s2 = _dot_nt(qv, k_ref[rows, :]) * to_log2
            p = jnp.exp2(s2 - mb_s[:, pl.ds(0, n)])
            acc_s[...] += _dot(_bf(p), v_ref[rows, :])

        @pl.when(i < n_lat)
        def _():
            def loop(c, carry):
                step(pl.ds(pl.multiple_of(c * ck, ck), ck), ck)
                return carry
            lax.fori_loop(0, chunks, loop, 0)

        step(pl.ds(s_len, l_len), l_len)
        l_rep = acc_s[:, pl.ds(HEAD_DIM, HEAD_DIM)]
        o = acc_s[:, pl.ds(0, HEAD_DIM)] / l_rep
        for hh in range(group):
            o_ref[:, pl.ds(hh * HEAD_DIM, HEAD_DIM)] = o[hh * T:(hh + 1) * T]
        lse = (mb_s[:, pl.ds(0, HEAD_DIM)] + jnp.log2(l_rep)) * math.log(2.0)
        lse_ref[...] = jnp.max(lse, axis=-1, keepdims=True).reshape(group, T, 1)

    return pl.pallas_call(
        body, name="flash_fwd", grid=(N_KV, R // T),
        in_specs=[pl.BlockSpec((T, GW), lambda g, i: (i, g)),
                  pl.BlockSpec((R, HEAD_DIM), lambda g, i: (0, g)),
                  pl.BlockSpec((R, 2 * HEAD_DIM), lambda g, i: (0, g))],
        out_specs=[pl.BlockSpec((T, GW), lambda g, i: (i, g)),
                   pl.BlockSpec((group, T, 1), lambda g, i: (g, i, 0))],
        out_shape=[SDS((R, N_HEADS * HEAD_DIM), F32), SDS((N_HEADS, R, 1), F32)],
        scratch_shapes=[pltpu.VMEM((8, M), F32), pltpu.VMEM((M, ck), F32), pltpu.VMEM((M, 2 * HEAD_DIM), F32)],
        compiler_params=_params(2),
    )(q, k, v)


def _flash_bwd(q, k, v, o, lse, do, s_len, l_len):
    R = q.shape[0]
    T = ROW_TILE
    n_lat = s_len // T
    ck = _pick(s_len, 512, 128)
    scale = HEAD_DIM ** -0.5
    group = N_HEADS // N_KV
    GW = group * HEAD_DIM
    qspec = pl.BlockSpec((T, GW), lambda g, i: (i, g))
    kspec = pl.BlockSpec((R, HEAD_DIM), lambda g, i: (0, g))

    M = group * T
    log2e = math.log2(math.e)

    def body(q_ref, do_ref, o_ref, lse_ref, k_ref, v_ref, dq_ref, dk_ref, dv_ref, dq_s, lse_s, delta_s):
        i = pl.program_id(1)

        @pl.when(i == 0)
        def _():
            dk_ref[...] = jnp.zeros_like(dk_ref)
            dv_ref[...] = jnp.zeros_like(dv_ref)

        def stacked(ref):
            return jnp.concatenate([ref[:, pl.ds(hh * HEAD_DIM, HEAD_DIM)] for hh in range(group)], axis=0)

        qv = stacked(q_ref)
        dov = stacked(do_ref)
        dob = _bf(dov)
        delta_s[...] = jnp.broadcast_to(jnp.sum(dov * stacked(o_ref), axis=-1, keepdims=True), (M, ck))
        lse_s[...] = jnp.broadcast_to(lse_ref[...].reshape(M, 1) * log2e, (M, ck))
        dq_s[...] = jnp.zeros_like(dq_s)

        def step(rows, n):
            kv, vv = k_ref[rows, :], v_ref[rows, :]
            p = jnp.exp2(_dot_nt(qv, kv) * (scale * log2e) - lse_s[:, pl.ds(0, n)])
            dv_ref[rows, :] += _dot_tn(_bf(p), dob)
            ds = _bf(p * (_dot_nt(dob, vv) - delta_s[:, pl.ds(0, n)]) * scale)
            dq_s[...] += _dot(ds, kv)
            dk_ref[rows, :] += _dot_tn(ds, qv)

        @pl.when(i < n_lat)
        def _():
            def loop(c, carry):
                step(pl.ds(pl.multiple_of(c * ck, ck), ck), ck)
                return carry
            lax.fori_loop(0, s_len // ck, loop, 0)

        step(pl.ds(s_len, l_len), l_len)
        for hh in range(group):
            dq_ref[:, pl.ds(hh * HEAD_DIM, HEAD_DIM)] = dq_s[pl.ds(hh * T, T), :]

    return pl.pallas_call(
        body, name="flash_bwd", grid=(N_KV, R // T),
        in_specs=[qspec, qspec, qspec, pl.BlockSpec((group, T, 1), lambda g, i: (g, i, 0)), kspec,
                  pl.BlockSpec((R, HEAD_DIM), lambda g, i: (0, 2 * g))],
        out_specs=[qspec, kspec, kspec],
        out_shape=[SDS((R, N_HEADS * HEAD_DIM), F32), SDS((R, N_KV * HEAD_DIM), F32),
                   SDS((R, N_KV * HEAD_DIM), F32)],
        scratch_shapes=[pltpu.VMEM((M, HEAD_DIM), F32), pltpu.VMEM((M, ck), F32), pltpu.VMEM((M, ck), F32)],
        compiler_params=_params(2),
    )(q, do, o, lse, k, v)


K_SCALE = RET_DK ** -0.5


def _log_sigmoid(v):
    return -(jnp.maximum(-v, 0.0) + jnp.log(1.0 + jnp.exp(-jnp.abs(v))))


def _ret_decays(d, lg):
    C = RET_CHUNK
    ic = lax.broadcasted_iota(jnp.int32, (C, 1), 0)
    ir = lax.broadcasted_iota(jnp.int32, (1, C), 1)
    li = jnp.where(d == 0, ic, C - 1 - ic).astype(F32)
    lj = jnp.where(d == 0, ir, C - 1 - ir).astype(F32)
    diff = li - lj
    mask = jnp.where(diff >= 0, jnp.exp(jnp.maximum(diff, 0.0) * lg), 0.0)
    qd = jnp.exp((li + 1.0) * lg)
    kd = jnp.exp((C - 1.0 - li) * lg)
    cd = jnp.exp(C * lg)
    return li, diff, mask, qd, kd, cd


def _ctx_weights(d, t, lg, l_len):
    C = RET_CHUNK
    j = (t * C + lax.broadcasted_iota(jnp.int32, (C, 1), 0)).astype(F32)
    e = jnp.where(d == 0, (l_len - 1.0) - j, j)
    return e, jnp.exp(e * lg)


def _ret_specs(n_lat_c, n_ctx_c, ctx_first):
    def blk(d, t):
        if ctx_first:
            n = jnp.maximum(t - n_ctx_c, 0)
            lat = jnp.where(d == 0, n, n_lat_c - 1 - n)
            return jnp.where(t < n_ctx_c, n_lat_c + t, lat)
        n = jnp.minimum(t, n_lat_c - 1)
        lat = jnp.where(d == 0, n_lat_c - 1 - n, n)
        return jnp.where(t >= n_lat_c, t, lat)
    return blk


def _ret_fwd(proj, lgt, s_len, l_len):
    R = proj.shape[0]
    C, H, DK, DV = RET_CHUNK, RET_HEADS, RET_DK, RET_DV
    nl, nc = s_len // C, l_len // C
    blk = _ret_specs(nl, nc, True)

    def body(q_ref, k_ref, v_ref, lg_ref, o_ref, st_ref, r_s):
        d, t = pl.program_id(0), pl.program_id(1)

        @pl.when(t == 0)
        def _():
            r_s[...] = jnp.zeros_like(r_s)

        def log_gamma(hh):
            return jnp.max(_log_sigmoid(lg_ref[0, hh]), axis=-1, keepdims=True)

        @pl.when(t < nc)
        def _():
            for hh in range(H):
                qc, vc = pl.ds(hh * DK, DK), pl.ds(hh * DV, DV)
                _, w = _ctx_weights(d, t, log_gamma(hh), l_len)
                r_s[hh] += _dot_tn(_bf(k_ref[:, qc] * K_SCALE * w), _bf(v_ref[:, vc]))
                o_ref[0, :, vc] = jnp.zeros((C, DV), F32)

        @pl.when(t >= nc)
        def _():
            for hh in range(H):
                qc, vc = pl.ds(hh * DK, DK), pl.ds(hh * DV, DV)
                _, _, mask, qd, kd, cd = _ret_decays(d, log_gamma(hh))
                qb, kv, vb = _bf(q_ref[:, qc]), k_ref[:, qc] * K_SCALE, _bf(v_ref[:, vc])
                r = r_s[hh]
                st_ref[0, hh, 0] = r
                att = _dot_nt(qb, _bf(kv)) * mask
                o_ref[0, :, vc] = _dot(_bf(att), vb) + _dot(qb, _bf(r)) * qd
                r_s[hh] = r * cd + _dot_tn(_bf(kv * kd), vb)

    return pl.pallas_call(
        body, name="ret_fwd", grid=(2, nc + nl),
        in_specs=[pl.BlockSpec((C, H * DK), lambda d, t: (blk(d, t), 0)),
                  pl.BlockSpec((C, H * DK), lambda d, t: (blk(d, t), 1)),
                  pl.BlockSpec((C, H * DV), lambda d, t: (blk(d, t), 1)),
                  pl.BlockSpec((1, H, 1, 128), lambda d, t: (d, 0, 0, 0))],
        out_specs=[pl.BlockSpec((1, C, H * DV), lambda d, t: (d, blk(d, t), 0)),
                   pl.BlockSpec((1, H, 1, DK, DV), lambda d, t: (d, 0, jnp.maximum(t - nc, 0), 0, 0))],
        out_shape=[SDS((2, R, H * DV), F32), SDS((2, H, nl, DK, DV), F32)],
        scratch_shapes=[pltpu.VMEM((H, DK, DV), F32)],
        compiler_params=_params(2),
    )(proj, proj, proj, lgt)


def _ret_bwd(proj, lgt, states, do, s_len, l_len):
    R = proj.shape[0]
    C, H, DK, DV = RET_CHUNK, RET_HEADS, RET_DK, RET_DV
    nl, nc = s_len // C, l_len // C
    blk = _ret_specs(nl, nc, False)
    last = nl + nc - 1

    def body(q_ref, k_ref, v_ref, lg_ref, st_ref, do_ref, dq_ref, dk_ref, dv_ref, dlg_ref, dr_s, dl_s):
        d, t = pl.program_id(0), pl.program_id(1)

        def log_gamma(hh):
            return jnp.max(_log_sigmoid(lg_ref[0, hh]), axis=-1, keepdims=True)

        @pl.when(t == 0)
        def _():
            dr_s[...] = jnp.zeros_like(dr_s)
            dl_s[...] = jnp.zeros_like(dl_s)

        @pl.when(t < nl)
        def _():
            for hh in range(H):
                qc, vc = pl.ds(hh * DK, DK), pl.ds(hh * DV, DV)
                li, diff, mask, qd, kd, cd = _ret_decays(d, log_gamma(hh))
                qv, kv, vv, dov = q_ref[:, qc], k_ref[:, qc] * K_SCALE, v_ref[:, vc], do_ref[:, vc]
                qb, kb, vb, dob = _bf(qv), _bf(kv), _bf(vv), _bf(dov)
                r, drn = st_ref[0, hh, 0], dr_s[hh]
                rb, drb = _bf(r), _bf(drn)
                p = _dot_nt(qb, kb)
                dp = _dot_nt(dob, vb) * mask
                dpb = _bf(dp)
                doq = _bf(dov * qd)
                dq_inter = _dot_nt(doq, rb)
                dk_state = kd * _dot_nt(vb, drb)
                dq_ref[0, :, qc] = _dot(dpb, kb) + dq_inter
                dk_ref[0, :, qc] = (_dot_tn(dpb, qb) + dk_state) * K_SCALE
                dv_ref[0, :, vc] = _dot_tn(_bf(p * mask), dob) + _dot(_bf(kv * kd), drb)
                dr_s[hh] = cd * drn + _dot_tn(qb, doq)
                dl_s[hh] += (jnp.sum(dp * p * diff) + jnp.sum((li + 1.0) * qv * dq_inter)
                             + jnp.sum((C - 1.0 - li) * kv * dk_state) + C * jnp.sum(cd * r * drn))

        @pl.when(t >= nl)
        def _():
            for hh in range(H):
                qc, vc = pl.ds(hh * DK, DK), pl.ds(hh * DV, DV)
                e, w = _ctx_weights(d, t - nl, log_gamma(hh), l_len)
                kv, vb, drb = k_ref[:, qc] * K_SCALE, _bf(v_ref[:, vc]), _bf(dr_s[hh])
                dkc = w * _dot_nt(vb, drb)
                dq_ref[0, :, qc] = jnp.zeros((C, DK), F32)
                dk_ref[0, :, qc] = dkc * K_SCALE
                dv_ref[0, :, vc] = _dot(_bf(kv * w), drb)
                dl_s[hh] += jnp.sum(e * kv * dkc)

        @pl.when(t == last)
        def _():
            for hh in range(H):
                dlg_ref[0, hh] = dl_s[hh] * (1.0 / (1.0 + jnp.exp(lg_ref[0, hh])))

    return pl.pallas_call(
        body, name="ret_bwd", grid=(2, nl + nc),
        in_specs=[pl.BlockSpec((C, H * DK), lambda d, t: (blk(d, t), 0)),
                  pl.BlockSpec((C, H * DK), lambda d, t: (blk(d, t), 1)),
                  pl.BlockSpec((C, H * DV), lambda d, t: (blk(d, t), 1)),
                  pl.BlockSpec((1, H, 1, 128), lambda d, t: (d, 0, 0, 0)),
                  pl.BlockSpec((1, H, 1, DK, DV), lambda d, t: (d, 0, jnp.maximum(nl - 1 - t, 0), 0, 0)),
                  pl.BlockSpec((C, H * DV), lambda d, t: (blk(d, t), 0))],
        out_specs=[pl.BlockSpec((1, C, H * DK), lambda d, t: (d, blk(d, t), 0)),
                   pl.BlockSpec((1, C, H * DK), lambda d, t: (d, blk(d, t), 0)),
                   pl.BlockSpec((1, C, H * DV), lambda d, t: (d, blk(d, t), 0)),
                   pl.BlockSpec((1, H, 1, 128), lambda d, t: (d, 0, 0, 0))],
        out_shape=[SDS((2, R, H * DK), F32), SDS((2, R, H * DK), F32), SDS((2, R, H * DV), F32),
                   SDS((2, H, 1, 128), F32)],
        scratch_shapes=[pltpu.VMEM((H, DK, DV), F32), pltpu.VMEM((H, 1, 128), F32)],
        compiler_params=_params(2),
    )(proj, proj, proj, lgt, states, do)


def _readout_fwd(o2, proj, gn_w):
    R = proj.shape[0]
    H, DV = RET_HEADS, RET_DV
    W = H * DV
    T = ROW_TILE

    def body(o_ref, g_ref, w_ref, out_ref):
        for hh in range(H):
            cols = pl.ds(hh * DV, DV)
            y = o_ref[0, :, cols] + o_ref[1, :, cols]
            yc = y - jnp.mean(y, axis=-1, keepdims=True)
            yn = yc * lax.rsqrt(jnp.mean(yc * yc, axis=-1, keepdims=True) + EPS) * w_ref[:, cols]
            g = g_ref[:, cols]
            out_ref[:, cols] = _bf(g * _sigmoid(g) * yn)

    return pl.pallas_call(
        body, name="readout_fwd", grid=(R // T,),
        in_specs=[pl.BlockSpec((2, T, W), lambda i: (0, i, 0)), pl.BlockSpec((T, W), lambda i: (i, 2)),
                  pl.BlockSpec((1, W), lambda i: (0, 0))],
        out_specs=pl.BlockSpec((T, W), lambda i: (i, 0)),
        out_shape=SDS((R, W), BF16), compiler_params=_params(1),
    )(o2, proj, gn_w)


def _readout_bwd(o2, proj, gn_w, dgated):
    R = proj.shape[0]
    H, DV = RET_HEADS, RET_DV
    W = H * DV
    T = ROW_TILE

    def body(o_ref, g_ref, w_ref, d_ref, do_ref, dg_ref, dw_ref):
        i = pl.program_id(0)

        @pl.when(i == 0)
        def _():
            dw_ref[...] = jnp.zeros_like(dw_ref)

        for hh in range(H):
            cols = pl.ds(hh * DV, DV)
            y = o_ref[0, :, cols] + o_ref[1, :, cols]
            yc = y - jnp.mean(y, axis=-1, keepdims=True)
            rstd = lax.rsqrt(jnp.mean(yc * yc, axis=-1, keepdims=True) + EPS)
            yn0 = yc * rstd
            wv = w_ref[:, cols]
            g = g_ref[:, cols]
            s = _sigmoid(g)
            dgt = d_ref[:, cols]
            dyn = dgt * (g * s)
            dg_ref[:, cols] = _bf(dgt * (yn0 * wv) * (s * (1.0 + g * (1.0 - s))))
            dw_ref[:, cols] += jnp.sum(dyn * yn0, axis=0, keepdims=True)
            a = dyn * wv
            do_ref[:, cols] = rstd * (a - jnp.mean(a, axis=-1, keepdims=True)
                                      - yn0 * jnp.mean(a * yn0, axis=-1, keepdims=True))

    return pl.pallas_call(
        body, name="readout_bwd", grid=(R // T,),
        in_specs=[pl.BlockSpec((2, T, W), lambda i: (0, i, 0)), pl.BlockSpec((T, W), lambda i: (i, 2)),
                  pl.BlockSpec((1, W), lambda i: (0, 0)), pl.BlockSpec((T, W), lambda i: (i, 0))],
        out_specs=[pl.BlockSpec((T, W), lambda i: (i, 0)), pl.BlockSpec((T, W), lambda i: (i, 0)),
                   pl.BlockSpec((1, W), lambda i: (0, 0))],
        out_shape=[SDS((R, W), F32), SDS((R, W), BF16), SDS((1, W), F32)],
        compiler_params=_params(1),
    )(o2, proj, gn_w, dgated)


def _ret_dproj(dq2, dk2, dv2, dg):
    R = dg.shape[0]
    NQ, NV = RET_HEADS * RET_DK, RET_HEADS * RET_DV
    T = ROW_TILE

    def body(dq_ref, dk_ref, dv_ref, dg_ref, o_ref):
        o_ref[:, pl.ds(0, NQ)] = _bf(dq_ref[0] + dq_ref[1])
        o_ref[:, pl.ds(NQ, NQ)] = _bf(dk_ref[0] + dk_ref[1])
        o_ref[:, pl.ds(2 * NQ, NV)] = _bf(dv_ref[0] + dv_ref[1])
        o_ref[:, pl.ds(2 * NQ + NV, NV)] = dg_ref[...]

    return pl.pallas_call(
        body, name="ret_dproj", grid=(R // T,),
        in_specs=[pl.BlockSpec((2, T, NQ), lambda i: (0, i, 0)), pl.BlockSpec((2, T, NQ), lambda i: (0, i, 0)),
                  pl.BlockSpec((2, T, NV), lambda i: (0, i, 0)), pl.BlockSpec((T, NV), lambda i: (i, 0))],
        out_specs=pl.BlockSpec((T, 2 * NQ + 2 * NV), lambda i: (i, 0)),
        out_shape=SDS((R, 2 * NQ + 2 * NV), BF16), compiler_params=_params(1),
    )(dq2, dk2, dv2, dg)


def _silu(v):
    return v * _sigmoid(v)


def _ada_fwd(c_rows, ada_w, ada_b_shard):
    depth, D, cols = ada_w.shape

    def body(c_ref, w_ref, b_ref, o_ref):
        o_ref[0] = _dot(_bf(_silu(c_ref[...])), _bf(w_ref[0])) + b_ref[0]

    return pl.pallas_call(
        body, name="ada_fwd", grid=(depth,),
        in_specs=[pl.BlockSpec((16, D), lambda i: (0, 0)), pl.BlockSpec((1, D, cols), lambda i: (i, 0, 0)),
                  pl.BlockSpec((1, 1, cols), lambda i: (i, 0, 0))],
        out_specs=pl.BlockSpec((1, 16, cols), lambda i: (i, 0, 0)),
        out_shape=SDS((depth, 16, cols), F32), compiler_params=_params(1),
    )(c_rows, ada_w, ada_b_shard)


def _ada_bwd(c_rows, ada_w, d_lat, d_ctx):
    depth, D, cols = ada_w.shape

    def body(c_ref, w_ref, dl_ref, dc_ref, dw_ref, pc_ref):
        i = pl.program_id(0)
        cv = c_ref[...]
        a = _silu(cv)
        dcs = jnp.broadcast_to(jnp.sum(dc_ref[0], axis=0, keepdims=True), (8, cols))
        dw_ref[0] = _dot_tn(_bf(a[0:8]), _bf(dl_ref[0])) + _dot_tn(_bf(a[8:16]), _bf(dcs))

        @pl.when(i == 0)
        def _():
            pc_ref[...] = jnp.zeros_like(pc_ref)

        pc_ref[...] += _dot_nt(_bf(dcs), _bf(w_ref[0]))

        @pl.when(i == depth - 1)
        def _():
            cc = c_ref[pl.ds(8, 1), :]
            s = _sigmoid(cc)
            pc_ref[...] = pc_ref[...] * (s * (1.0 + cc * (1.0 - s)))

    return pl.pallas_call(
        body, name="ada_bwd", grid=(depth,),
        in_specs=[pl.BlockSpec((16, D), lambda i: (0, 0)), pl.BlockSpec((1, D, cols), lambda i: (i, 0, 0)),
                  pl.BlockSpec((1, 8, cols), lambda i: (i, 0, 0)), pl.BlockSpec((1, 8, cols), lambda i: (i, 0, 0))],
        out_specs=[pl.BlockSpec((1, D, cols), lambda i: (i, 0, 0)), pl.BlockSpec((8, D), lambda i: (0, 0))],
        out_shape=[SDS((depth, D, cols), F32), SDS((8, D), F32)], compiler_params=_params(1),
    )(c_rows, ada_w, d_lat, d_ctx)


def _adamw(w, g, m, v, name):
    shape = w.shape
    n = g.shape[0]
    cols = shape[-1]
    rows = w.size // cols
    tr = _pick(rows, 512, 8) if rows * cols * 4 > (1 << 20) else rows
    spec = pl.BlockSpec((tr, cols), lambda i: (i, 0))

    def body(w_ref, g_ref, m_ref, v_ref, go_ref, d_ref, mo_ref, vo_ref):
        gs = g_ref[0].astype(F32)
        for k in range(1, n):
            gs = gs + g_ref[k].astype(F32)
        mn = ADAM_B1 * m_ref[...] + (1.0 - ADAM_B1) * gs
        vn = ADAM_B2 * v_ref[...] + (1.0 - ADAM_B2) * jnp.square(gs)
        m_hat = mn / (1.0 - ADAM_B1 ** ADAM_STEP)
        v_hat = vn / (1.0 - ADAM_B2 ** ADAM_STEP)
        go_ref[...] = gs
        d_ref[...] = -ADAM_LR * (m_hat / (jnp.sqrt(v_hat) + ADAM_EPS) + ADAM_WD * w_ref[...])
        mo_ref[...] = mn
        vo_ref[...] = vn

    outs = pl.pallas_call(
        body, name=name, grid=(rows // tr,),
        in_specs=[spec, pl.BlockSpec((n, tr, cols), lambda i: (0, i, 0)), spec, spec],
        out_specs=[spec] * 4, out_shape=[SDS((rows, cols), F32)] * 4, compiler_params=_params(1),
    )(w.reshape(rows, cols), g.reshape(n, rows, cols), m.reshape(rows, cols), v.reshape(rows, cols))
    return tuple(o.reshape(shape) for o in outs)


def _sum_slots(own, recv, name):
    shape, n, cols = own.shape, recv.shape[0], own.shape[-1]
    own, recv = own.reshape(-1, cols), recv.reshape(n, -1, cols)
    rows = own.shape[0]
    tr = _pick(rows, 512, 16)

    def body(own_ref, r_ref, o_ref):
        acc = own_ref[...].astype(F32)
        for k in range(n):
            acc = acc + r_ref[k].astype(F32)
        o_ref[...] = acc

    return pl.pallas_call(
        body, name=name, grid=(rows // tr,),
        in_specs=[pl.BlockSpec((tr, cols), lambda i: (i, 0)), pl.BlockSpec((n, tr, cols), lambda i: (0, i, 0))],
        out_specs=pl.BlockSpec((tr, cols), lambda i: (i, 0)),
        out_shape=SDS((rows, cols), F32), compiler_params=_params(1),
    )(own, recv).reshape(shape)


def _position():
    return lax.axis_index("x"), lax.axis_index("y"), lax.axis_index("c")


def _peer(k, x, y, c):
    return (1 - x if k & 4 else x, 1 - y if k & 2 else y, 1 - c if k & 1 else c)


def _index(pos):
    return 4 * pos[0] + 2 * pos[1] + pos[2]


def _gather_small(v, name):
    rows, lanes = v.shape

    def body(x_ref, out_ref, send_sems, recv_sems, local_sem):
        me = _position()
        mine = pltpu.make_async_copy(x_ref, out_ref.at[_index(me)], local_sem)
        mine.start()

        def copy(k, slot):
            return pltpu.make_async_remote_copy(
                src_ref=x_ref, dst_ref=out_ref.at[slot], send_sem=send_sems.at[k - 1],
                recv_sem=recv_sems.at[k - 1], device_id=_peer(k, *me), device_id_type=MESH)

        sends = [copy(k, _index(me)) for k in range(1, N_DEV)]
        for cp in sends:
            cp.start()
        for k in range(1, N_DEV):
            copy(k, _index(_peer(k, *me))).wait_recv()
        for cp in sends:
            cp.wait_send()
        mine.wait()

    return pl.pallas_call(
        body, name=name, out_shape=SDS((N_DEV, rows, lanes), v.dtype),
        in_specs=[pl.BlockSpec(memory_space=pltpu.VMEM)],
        out_specs=pl.BlockSpec(memory_space=pltpu.VMEM),
        scratch_shapes=[pltpu.SemaphoreType.DMA((N_DEV - 1,)), pltpu.SemaphoreType.DMA((N_DEV - 1,)),
                        pltpu.SemaphoreType.DMA],
        compiler_params=pltpu.CompilerParams(vmem_limit_bytes=VMEM_LIMIT_V7X),
    )(v)


def _gather_big(v, name):
    rows, cols = v.shape

    def body(x_ref, out_ref, send_sems, recv_sems, local_sem):
        x, y, c = _position()
        me, sibling = (x, y, c), (x, y, 1 - c)
        chips = [(1 - x, y), (x, 1 - y), (1 - x, 1 - y)]

        def copy(k, block, to, src=None):
            slot = out_ref.at[_index(block)]
            return pltpu.make_async_remote_copy(
                src_ref=slot if src is None else src, dst_ref=slot, send_sem=send_sems.at[k],
                recv_sem=recv_sems.at[k], device_id=to, device_id_type=MESH)

        mine = pltpu.make_async_copy(x_ref, out_ref.at[_index(me)], local_sem)
        mine.start()
        first = [copy(0, me, sibling, src=x_ref)]
        first += [copy(1 + j, me, (*chip, c), src=x_ref) for j, chip in enumerate(chips)]
        for cp in first:
            cp.start()
        passed = [copy(4 + j, (*chip, c), sibling) for j, chip in enumerate(chips)]
        for j, chip in enumerate(chips):
            copy(1 + j, (*chip, c), me).wait_recv()
            passed[j].start()
        copy(0, sibling, me).wait_recv()
        for j, chip in enumerate(chips):
            copy(4 + j, (*chip, 1 - c), me).wait_recv()
        for cp in first + passed:
            cp.wait_send()
        mine.wait()

    return pl.pallas_call(
        body, name=name, out_shape=SDS((N_DEV, rows, cols), v.dtype),
        in_specs=[pl.BlockSpec(memory_space=pl.ANY)],
        out_specs=pl.BlockSpec(memory_space=pl.ANY),
        scratch_shapes=[pltpu.SemaphoreType.DMA((N_DEV - 1,)), pltpu.SemaphoreType.DMA((N_DEV - 1,)),
                        pltpu.SemaphoreType.DMA],
    )(v)


HBM_SPEC = pl.BlockSpec(memory_space=pltpu.HBM)
SEM_SPEC = pl.BlockSpec(memory_space=pltpu.SEMAPHORE)
SPLIT_EFFECT = pltpu.SideEffectType.DATAFLOW_SIDE_EFFECTING


def _split_start(srcs, gather, name):
    n = len(srcs)
    lands = [jnp.zeros(((N_DEV,) + s.shape) if gather else s.shape, s.dtype) for s in srcs]

    def body(*refs):
        src_refs, land_refs, sems, token = refs[:n], refs[n:2 * n], refs[2 * n:4 * n], refs[-1]
        me = _position()
        for a in range(n):
            for k in range(1, N_DEV):
                peer = _peer(k, *me)
                pltpu.make_async_remote_copy(
                    src_ref=src_refs[a] if gather else src_refs[a].at[_index(peer)],
                    dst_ref=land_refs[a].at[_index(me)], send_sem=sems[2 * a], recv_sem=sems[2 * a + 1],
                    device_id=peer, device_id_type=MESH).start()
        token[...] = jnp.zeros_like(token)

    hbm = lambda arrays: tuple(pltpu.HBM(a.shape, a.dtype) for a in arrays)
    outs = pl.pallas_call(
        body, name=name,
        out_shape=(pltpu.SemaphoreType.DMA(()),) * (2 * n) + hbm(srcs) + hbm(lands) + (SDS((8, 128), F32),),
        in_specs=(HBM_SPEC,) * (2 * n),
        out_specs=(SEM_SPEC,) * (2 * n) + (HBM_SPEC,) * (2 * n) + (pl.BlockSpec(memory_space=pltpu.VMEM),),
        input_output_aliases={a: 2 * n + a for a in range(2 * n)},
        compiler_params=pltpu.CompilerParams(has_side_effects=SPLIT_EFFECT),
    )(*[pltpu.with_memory_space_constraint(a, pltpu.HBM) for a in list(srcs) + lands])
    return outs[:2 * n], outs[2 * n:3 * n], outs[3 * n:4 * n], outs[-1]


def _split_wait(flight, after, name):
    sems, srcs, lands, _ = flight
    n = len(srcs)

    def body(*refs):
        land_refs, sem_refs = refs[n:2 * n], refs[2 * n:4 * n]
        me = _position()
        for a in range(n):
            seven = land_refs[a].at[pl.ds(0, N_DEV - 1)]
            copies = pltpu.make_async_remote_copy(
                src_ref=seven, dst_ref=seven, send_sem=sem_refs[2 * a], recv_sem=sem_refs[2 * a + 1],
                device_id=_peer(1, *me), device_id_type=MESH)
            copies.wait_send()
            copies.wait_recv()

    outs = pl.pallas_call(
        body, name=name,
        out_shape=tuple(pltpu.HBM(a.shape, a.dtype) for a in list(srcs) + list(lands)),
        in_specs=(HBM_SPEC,) * (2 * n) + (SEM_SPEC,) * (2 * n) + (pl.BlockSpec(memory_space=pl.ANY),),
        out_specs=(HBM_SPEC,) * (2 * n), input_output_aliases={a: a for a in range(2 * n)},
        compiler_params=pltpu.CompilerParams(has_side_effects=SPLIT_EFFECT),
    )(*srcs, *lands, *sems, after)
    return outs[:n], outs[n:]


def _pack_rows(arrays, lanes, dtype):
    flat = jnp.concatenate([a.astype(dtype).reshape(-1) for a in arrays])
    pad = (-flat.size) % (16 * lanes)
    if pad:
        flat = jnp.concatenate([flat, jnp.zeros((pad,), dtype)])
    return flat.reshape(-1, lanes)


def _unpack_rows(packed, shapes):
    n = packed.shape[0]
    flat = packed.reshape(n, -1)
    out, off = [], 0
    for shp in shapes:
        size = math.prod(shp)
        out.append(flat[:, off:off + size].reshape((n,) + tuple(shp)))
        off += size
    return out


def _unshard(g8, axis):
    moved = jnp.moveaxis(g8, 0, axis)
    shp = list(moved.shape)
    shp[axis:axis + 2] = [shp[axis] * shp[axis + 1]]
    return moved.reshape(shp)


def _split8(full, axis):
    shp = list(full.shape)
    shp[axis:axis + 1] = [N_DEV, shp[axis] // N_DEV]
    return jnp.moveaxis(full.reshape(shp), axis, 0)


def _my_shard(g, axis, me):
    size = g.shape[axis + 1] // N_DEV
    return lax.dynamic_slice_in_dim(g, me * size, size, axis=axis + 1)


BIG_WEIGHTS = ("ffn_w_up", "ffn_w_down", "attn_w_qkv", "attn_w_o", "ret_w_in", "ret_w_out", "pool_w")
LAYER_WEIGHTS = (
    (("ffn_w_up", 0, "cols"), ("ffn_w_down", 0, "rows"), ("pool_w", 0, "pool")),
    (("ffn_w_up", 1, "cols"), ("ffn_w_down", 1, "rows"), ("attn_w_qkv", 0, "cols"), ("attn_w_o", 0, "rows")),
    (("ffn_w_up", 2, "cols"), ("ffn_w_down", 2, "rows"), ("ret_w_in", 0, "cols"), ("ret_w_out", 0, "rows")),
    (("ffn_w_up", 3, "cols"), ("ffn_w_down", 3, "rows"), ("pool_w", 1, "pool")),
)


def _shard_to_send(w, kind):
    w = w.astype(BF16)
    return w.T if kind == "cols" else w


def _full_from_land(land, kind):
    return _unshard(land, 1) if kind == "pool" else land.reshape(-1, land.shape[-1])


def _grad_to_send(g, kind):
    return _split8(g, 1).astype(BF16) if kind == "pool" else g.astype(BF16).reshape(N_DEV, -1, g.shape[-1])


def _shard_grad(gsum, kind):
    return gsum.T if kind == "cols" else gsum
SMALL_SHARDED = (("norm_w", 2), ("pool_b", 1), ("pool_scale", 1), ("ret_gn_w", 1), ("ffn_conv_w", 2))
REPLICATED = ("ada_b", "attn_q_gain", "attn_k_gain", "ret_decay_logit", "ffn_conv_b")
WEIGHT_ORDER = ("c_ctx", "ada_w", "ada_b", "norm_w", "pool_w", "pool_b", "pool_scale", "attn_w_qkv",
                "attn_q_gain", "attn_k_gain", "attn_w_o", "ret_w_in", "ret_decay_logit", "ret_gn_w",
                "ret_w_out", "ffn_w_up", "ffn_conv_w", "ffn_conv_b", "ffn_w_down")


def _local_step(x0, target, mods, P, get_weights, put_grads, s_len, l_len):
    n_lat = s_len // ROW_TILE
    nw = P["norm_w"]
    lgt = jnp.broadcast_to(P["ret_decay_logit"][0][:, :, None, None], (2, RET_HEADS, 1, 128))
    cos, sin = _rope_tables(s_len, l_len)
    h_dtype = [F32 if i % 3 == 0 else BF16 for i in range(DEPTH)]
    saved = []
    mods = list(mods)
    X = x0
    h = _res_norm(X, None, None, 0, nw[0, 0], mods[0], 0, h_dtype[0], n_lat, "norm_first")
    for i in range(DEPTH):
        kind, j, mod = i % 3, i // 3, mods[i]
        W = get_weights(i, X)
        sv = {"X": X, "h": h, "W": W}
        if kind == 0:
            y = _pool_fwd(h, W["pool_w"], P["pool_b"][j:j + 1], P["pool_scale"][j:j + 1],
                          n_lat, s_len, l_len, f"pool_fwd{i}")
        elif kind == 1:
            qkv = _mm(h, W["attn_w_qkv"], "nt", F32, f"qkv{i}")
            q, k, v = _qk_prep_fwd(qkv, P["attn_q_gain"][j:j + 1], P["attn_k_gain"][j:j + 1], cos, sin)
            o, lse = _flash_fwd(q, k, v, s_len, l_len)
            y = _mm(o, W["attn_w_o"], "nn", F32, f"attn_out{i}")
            sv.update(qkv=qkv, q=q, k=k, v=v, o=o, lse=lse)
        else:
            proj = _mm(h, W["ret_w_in"], "nt", F32, f"ret_in{i}")
            o2, states = _ret_fwd(proj, lgt, s_len, l_len)
            gated = _readout_fwd(o2, proj, P["ret_gn_w"][j:j + 1])
            y = _mm(gated, W["ret_w_out"], "nn", F32, f"ret_out{i}")
            sv.update(proj=proj, o2=o2, states=states, gated=gated)
        X1, h2 = _res_norm(X, y, mod, 0, nw[i, 1], mod, 1, BF16, n_lat, f"res_norm_mid{i}")
        u = _mm(h2, W["ffn_w_up"], "nt", F32, f"ffn_up{i}")
        gact = _conv_gate_fwd(u, P["ffn_conv_w"][i], P["ffn_conv_b"][i:i + 1], n_lat, f"conv_gate_fwd{i}")
        f = _mm(gact, W["ffn_w_down"], "nn", F32, f"ffn_down{i}")
        sv.update(y=y, X1=X1, h2=h2, u=u, gact=gact, f=f)
        saved.append(sv)
        if i + 1 < DEPTH:
            X, h = _res_norm(X1, f, mod, 1, nw[i + 1, 0], mods[i + 1], 0, h_dtype[i + 1], n_lat,
                             f"res_norm_end{i}")
        else:
            X = _res_norm(X1, f, mod, 1, None, None, 0, None, n_lat, "res_last")

    dX, loss = _loss_bwd(X, target, n_lat)
    G = {name: [None] * P[name].shape[0] for name in
         ("pool_b", "pool_scale", "attn_q_gain", "attn_k_gain", "ret_decay_logit", "ret_gn_w", "ffn_conv_w",
          "ffn_conv_b")}
    dnw = [[None, None] for _ in range(DEPTH)]
    dmods = [None] * DEPTH
    for i in reversed(range(DEPTH)):
        kind, j, mod, sv = i % 3, i // 3, mods[i], saved[i]
        W, gl = sv["W"], {}
        df, dg2 = _gate_bwd(dX, sv["f"], mod, 1, BF16, n_lat, f"gate_bwd_ffn{i}")
        dgact = _mm(df, W["ffn_w_down"], "nt", F32, f"ffn_down_dx{i}")
        gl["ffn_w_down"] = _mm(sv["gact"], df, "tn", BF16, f"ffn_down_dw{i}")
        du, dcw, dcb = _conv_gate_bwd(sv["u"], dgact, P["ffn_conv_w"][i], P["ffn_conv_b"][i:i + 1], n_lat,
                                      f"conv_gate_bwd{i}")
        G["ffn_conv_w"][i], G["ffn_conv_b"][i] = dcw, dcb[0]
        dh2 = _mm(du, W["ffn_w_up"], "nn", F32, f"ffn_up_dx{i}")
        gl["ffn_w_up"] = _mm(du, sv["h2"], "tn", BF16, f"ffn_up_dw{i}")
        dX1, dnw[i][1], dsh2, dsc2 = _norm_bwd(dX, dh2, sv["X1"], nw[i, 1], mod, 1, n_lat, f"norm_bwd_ffn{i}")
        dy, dg1 = _gate_bwd(dX1, sv["y"], mod, 0, F32 if kind == 0 else BF16, n_lat, f"gate_bwd_mix{i}")
        h = sv["h"]
        if kind == 0:
            dh, dpw, dpb, dps = _pool_bwd(h, dy, W["pool_w"], P["pool_b"][j:j + 1], P["pool_scale"][j:j + 1],
                                          n_lat, s_len, l_len, f"pool_bwd{i}")
            gl["pool_w"], G["pool_b"][j], G["pool_scale"][j] = dpw, dpb[0], dps[0]
        elif kind == 1:
            do = _mm(dy, W["attn_w_o"], "nt", F32, f"attn_out_dx{i}")
            gl["attn_w_o"] = _mm(sv["o"], dy, "tn", BF16, f"attn_out_dw{i}")
            dq, dk, dv = _flash_bwd(sv["q"], sv["k"], sv["v"], sv["o"], sv["lse"], do, s_len, l_len)
            dqkv, dqg, dkg = _qk_prep_bwd(sv["qkv"], dq, dk, dv, P["attn_q_gain"][j:j + 1],
                                          P["attn_k_gain"][j:j + 1], cos, sin)
            G["attn_q_gain"][j], G["attn_k_gain"][j] = dqg[0], dkg[0]
            dh = _mm(dqkv, W["attn_w_qkv"], "nn", F32, f"qkv_dx{i}")
            gl["attn_w_qkv"] = _mm(dqkv, h, "tn", BF16, f"qkv_dw{i}")
        else:
            dgated = _mm(dy, W["ret_w_out"], "nt", F32, f"ret_out_dx{i}")
            gl["ret_w_out"] = _mm(sv["gated"], dy, "tn", BF16, f"ret_out_dw{i}")
            do, dg, dgn = _readout_bwd(sv["o2"], sv["proj"], P["ret_gn_w"][j:j + 1], dgated)
            dq2, dk2, dv2, dlg = _ret_bwd(sv["proj"], lgt, sv["states"], do, s_len, l_len)
            dproj = _ret_dproj(dq2, dk2, dv2, dg)
            G["ret_gn_w"][j], G["ret_decay_logit"][j] = dgn[0], dlg[:, :, 0, 0]
            dh = _mm(dproj, W["ret_w_in"], "nn", F32, f"ret_in_dx{i}")
            gl["ret_w_in"] = _mm(dproj, h, "tn", BF16, f"ret_in_dw{i}")
        dX, dnw[i][0], dsh1, dsc1 = _norm_bwd(dX1, dh, sv["X"], nw[i, 0], mod, 0, n_lat, f"norm_bwd_mix{i}")
        dmods[i] = jnp.concatenate([dsh1, dsc1, dg1, dsh2, dsc2, dg2], axis=1)
        zero = put_grads(i, gl)
        if i > 0:
            mods[i - 1] = mods[i - 1] + zero
    grads = {name: jnp.stack(parts) for name, parts in G.items()}
    grads["norm_w"] = jnp.stack([jnp.concatenate(pair, axis=0) for pair in dnw])
    return loss, dX, grads, jnp.stack(dmods)


def kernel(x, c, ctx, c_ctx, ada_w, ada_b, norm_w, pool_w, pool_b, pool_scale, attn_w_qkv, attn_q_gain,
           attn_k_gain, attn_w_o, ret_w_in, ret_decay_logit, ret_gn_w, ret_w_out, ffn_w_up, ffn_conv_w,
           ffn_conv_b, ffn_w_down, loss_target, m_c_ctx, m_ada_w, m_ada_b, m_norm_w, m_pool_w, m_pool_b,
           m_pool_scale, m_attn_w_qkv, m_attn_q_gain, m_attn_k_gain, m_attn_w_o, m_ret_w_in,
           m_ret_decay_logit, m_ret_gn_w, m_ret_w_out, m_ffn_w_up, m_ffn_conv_w, m_ffn_conv_b, m_ffn_w_down,
           v_c_ctx, v_ada_w, v_ada_b, v_norm_w, v_pool_w, v_pool_b, v_pool_scale, v_attn_w_qkv, v_attn_q_gain,
           v_attn_k_gain, v_attn_w_o, v_ret_w_in, v_ret_decay_logit, v_ret_gn_w, v_ret_w_out, v_ffn_w_up,
           v_ffn_conv_w, v_ffn_conv_b, v_ffn_w_down):
    A = dict(locals())
    me = _index(_position())
    s_len, D = x.shape[1], x.shape[2]
    l_len = ctx.shape[1]
    assert s_len % ROW_TILE == 0 and l_len % ROW_TILE == 0 and s_len % GRID_W == 0

    small = [A[n] for n, _ in SMALL_SHARDED]
    got = _gather_small(_pack_rows([c] + small, 128, F32), "gather_c_small")
    parts = _unpack_rows(got, [c.shape] + [a.shape for a in small])
    c_all = parts[0].reshape(N_DEV, D)
    P = {n: _unshard(g8, ax) for (n, ax), g8 in zip(SMALL_SHARDED, parts[1:])}

    c_rows = jnp.concatenate([c_all, c_ctx.reshape(1, D), jnp.zeros((7, D), F32)], axis=0)
    cols = ada_w.shape[2]
    ada_b_shard = lax.dynamic_slice_in_dim(ada_b, me * cols, cols, axis=1).reshape(DEPTH, 1, cols)
    mod_shard = _ada_fwd(c_rows, ada_w, ada_b_shard)
    got = _gather_small(mod_shard.reshape(-1, 128), "gather_mod").reshape(N_DEV, DEPTH, 16, cols)
    mod_lat = lax.dynamic_index_in_dim(got, me, axis=2, keepdims=False)
    mod_ctx = got[:, :, 8, :]
    mods = jnp.stack([jnp.moveaxis(mod_lat, 0, 1).reshape(DEPTH, 6, D),
                      jnp.moveaxis(mod_ctx, 0, 1).reshape(DEPTH, 6, D)], axis=1)

    shards = [[_shard_to_send(A[n][j], kind) for n, j, kind in lw] for lw in LAYER_WEIGHTS]
    pack0 = jnp.concatenate([s.reshape(-1, D) for s in shards[0]], axis=0)
    got0 = _gather_big(pack0, "gather_weights0")
    first, off = {}, 0
    for (n, j, kind), s in zip(LAYER_WEIGHTS[0], shards[0]):
        r = s.size // D
        first[n] = _full_from_land(got0[:, off:off + r].reshape((N_DEV,) + s.shape), kind)
        off += r
    flights, zero = {}, jnp.zeros((), F32)
    for i in range(1, DEPTH):
        flights[i] = _split_start(shards[i], True, f"gather_start{i}")
        zero = zero + flights[i][3][0, 0]
    mods = [mods[i] for i in range(DEPTH)]
    mods[0] = mods[0] + zero
    for n in REPLICATED:
        P[n] = A[n]

    def get_weights(i, x_now):
        if i == 0:
            return first
        owns, lands = _split_wait(flights[i], x_now, f"gather_wait{i}")
        return {n: _full_from_land(lax.dynamic_update_index_in_dim(land, own, me, axis=0), kind)
                for (n, j, kind), own, land in zip(LAYER_WEIGHTS[i], owns, lands)}

    sent = {}

    def put_grads(i, gl):
        sent[i] = _split_start([_grad_to_send(gl[n], kind) for n, j, kind in LAYER_WEIGHTS[i]], False,
                               f"exchange_start{i}")
        return sent[i][3][0, 0]

    x0 = jnp.concatenate([x[0], ctx[0]], axis=0)
    loss8, dx0, G, dmods = _local_step(x0, loss_target[0], mods, P, get_weights, put_grads, s_len, l_len)
    loss = lax.psum(loss8[0, 0], ("x", "y", "c"))
    grad_x = dx0[:s_len].reshape(x.shape)

    small_names = ["dmods"] + list(REPLICATED[1:]) + [n for n, _ in SMALL_SHARDED]
    small_parts = [dmods] + [G[n] for n in small_names[1:]]
    got = _gather_small(_pack_rows(small_parts, 128, F32), "gather_small_grads")
    S8 = dict(zip(small_names, _unpack_rows(got, [a.shape for a in small_parts])))

    dm = S8["dmods"].reshape(N_DEV, DEPTH, 2, 6 * D)
    dm_mine = lax.dynamic_slice_in_dim(dm, me * cols, cols, axis=3)
    g_ada_w, pc = _ada_bwd(c_rows, ada_w, jnp.moveaxis(dm_mine[:, :, 0], 0, 1), jnp.moveaxis(dm_mine[:, :, 1], 0, 1))
    pc8 = _gather_small(pc.reshape(-1, 128), "gather_c_ctx_grad").reshape(N_DEV, 8, D)

    def owner_sums(i, after):
        sends, lands = _split_wait(sent[i], after, f"exchange_wait{i}")
        out = {}
        for (n, j, kind), send, land in zip(LAYER_WEIGHTS[i], sends, lands):
            own = lax.dynamic_index_in_dim(send, me, axis=0, keepdims=False)
            out[(n, j)] = _shard_grad(_sum_slots(own, land, f"sum_slots_{n}{j}"), kind)
        return out

    shard_grads = {}
    for i in range(1, DEPTH):
        shard_grads.update(owner_sums(i, pc8))

    g_in = {"c_ctx": pc8[:, 0, :], "ada_w": g_ada_w[None],
            "ada_b": jnp.moveaxis(dm, 2, 1).reshape(2 * N_DEV, DEPTH, 6 * D)}
    for n in REPLICATED[1:]:
        g_in[n] = S8[n]
    for n, ax in SMALL_SHARDED:
        g_in[n] = _my_shard(S8[n], ax, me)

    def stacked(n):
        return jnp.stack([shard_grads[(n, j)] for j in range(A[n].shape[0])])[None]

    late = [n for n, j, kind in LAYER_WEIGHTS[0]]
    for n in BIG_WEIGHTS:
        if n not in late:
            g_in[n] = stacked(n)
    res = {n: _adamw(A[n], g_in[n], A["m_" + n], A["v_" + n], "adamw_" + n) for n in WEIGHT_ORDER if n not in late}
    done = sum(res[n][1].reshape(-1)[0] for n in res)
    shard_grads.update(owner_sums(0, done.reshape(1, 1)))
    for n in late:
        res[n] = _adamw(A[n], stacked(n), A["m_" + n], A["v_" + n], "adamw_" + n)
    outs = [loss, grad_x]
    for slot in range(4):
        outs += [res[n][slot] for n in WEIGHT_ORDER]
    return tuple(outs)
```

```python
import functools
import math

import jax
import jax.numpy as jnp
from jax import lax
from jax.experimental import pallas as pl
from jax.experimental.pallas import tpu as pltpu

F32 = jnp.float32
BF16 = jnp.bfloat16
SDS = jax.ShapeDtypeStruct
MESH = pl.DeviceIdType.MESH

N_DEV = 8
EPS = 1e-6
DEPTH = 4
GRID_W = 64
POOL_WINDOWS = (2, 4, 8, 16)
N_HEADS = 8
N_KV = 2
HEAD_DIM = 128
ROPE_THETA = 10000.0
RET_HEADS = 4
RET_DK = 256
RET_DV = 512
RET_CHUNK = 128
ADAM_LR = 0.001
ADAM_B1 = 0.9
ADAM_B2 = 0.999
ADAM_EPS = 1e-08
ADAM_WD = 0.01
ADAM_STEP = 10

ROW_TILE = 256
FLASH_FWD_TILE = 128
HALO = 8
VMEM_LIMIT_V7X = 56 * 1024 * 1024


def _params(n_axes=0):
    sem = ("arbitrary",) * n_axes if n_axes else None
    return pltpu.CompilerParams(dimension_semantics=sem, vmem_limit_bytes=VMEM_LIMIT_V7X)


def _pick(n, cap, mult):
    best = None
    for d in range(mult, min(n, cap) + 1, mult):
        if n % d == 0:
            best = d
    return best if best is not None else n


def _dot(a, b):
    return jnp.dot(a, b, preferred_element_type=F32)


def _dot_nt(a, b):
    return lax.dot_general(a, b, (((1,), (1,)), ((), ())), preferred_element_type=F32)


def _dot_tn(a, b):
    return lax.dot_general(a, b, (((0,), (0,)), ((), ())), preferred_element_type=F32)


def _bf(v):
    return v.astype(BF16)


def _sigmoid(v):
    return 0.5 * jnp.tanh(0.5 * v) + 0.5


MM_VMEM_BUDGET = 40 * 1024 * 1024
MM_STEP_BYTES = 1 << 20


def _divisors(n, mult, cap):
    return [d for d in range(mult, min(n, cap) + 1, mult) if n % d == 0] or [n]


def _mm_tiles(mode, M, N, K, a_item, b_item, o_item):
    best = None
    for tm in _divisors(M, 128 if mode == "tn" else 16, 2816):
        for tn in _divisors(N, 128, 2048):
            for tk in _divisors(K, 16 if mode == "tn" else 128, 2816):
                ni, nj, nk = M // tm, N // tn, K // tk
                vmem = 2 * (tm * tk * a_item + tk * tn * b_item + tm * tn * o_item) + tm * tn * 4
                if vmem > MM_VMEM_BUDGET:
                    continue
                a_reads = 1 if nk == 1 else nj
                b_reads = 1 if (nk == 1 and nj == 1) else ni
                cost = (M * K * a_item * a_reads + K * N * b_item * b_reads + M * N * o_item
                        + ni * nj * nk * MM_STEP_BYTES + (nk - 1) * M * N)
                if best is None or cost < best[0]:
                    best = (cost, tm, tn, tk)
    return best[1:]


def _mm(a, b, mode, out_dtype, name):
    if mode == "nn":
        (M, K), (K2, N) = a.shape, b.shape
    elif mode == "nt":
        (M, K), (N, K2) = a.shape, b.shape
    else:
        (K, M), (K2, N) = a.shape, b.shape
    assert K == K2, (a.shape, b.shape, mode)
    tm, tn, tk = _mm_tiles(mode, M, N, K, a.dtype.itemsize, b.dtype.itemsize, jnp.dtype(out_dtype).itemsize)
    nk = K // tk
    if mode == "nn":
        a_spec = pl.BlockSpec((tm, tk), lambda i, j, k: (i, k))
        b_spec = pl.BlockSpec((tk, tn), lambda i, j, k: (k, j))
    elif mode == "nt":
        a_spec = pl.BlockSpec((tm, tk), lambda i, j, k: (i, k))
        b_spec = pl.BlockSpec((tn, tk), lambda i, j, k: (j, k))
    else:
        a_spec = pl.BlockSpec((tk, tm), lambda i, j, k: (k, i))
        b_spec = pl.BlockSpec((tk, tn), lambda i, j, k: (k, j))
    dot = {"nn": _dot, "nt": _dot_nt, "tn": _dot_tn}[mode]

    def body(a_ref, b_ref, o_ref, acc_ref):
        part = dot(_bf(a_ref[...]), _bf(b_ref[...]))
        if nk == 1:
            o_ref[...] = part.astype(out_dtype)
        else:
            k = pl.program_id(2)

            @pl.when(k == 0)
            def _():
                acc_ref[...] = part

            @pl.when(k > 0)
            def _():
                acc_ref[...] += part

            @pl.when(k == nk - 1)
            def _():
                o_ref[...] = acc_ref[...].astype(out_dtype)

    return pl.pallas_call(
        body, name=name, grid=(M // tm, N // tn, nk),
        in_specs=[a_spec, b_spec],
        out_specs=pl.BlockSpec((tm, tn), lambda i, j, k: (i, j)),
        out_shape=SDS((M, N), out_dtype),
        scratch_shapes=[pltpu.VMEM((tm, tn), F32)],
        compiler_params=_params(3),
    )(a, b)


def _seg_spec(n_lat, d):
    return pl.BlockSpec((1, 6, d), lambda i: ((i >= n_lat).astype(jnp.int32), 0, 0))


def _seg_acc_spec(n_lat, d):
    return pl.BlockSpec((1, 1, d), lambda i: ((i >= n_lat).astype(jnp.int32), 0, 0))


def _res_norm(x, y, gmod, gk, nw, nmod, nk, h_dtype, n_lat, name):
    R, D = x.shape
    has_res, has_norm = y is not None, nw is not None
    row = pl.BlockSpec((ROW_TILE, D), lambda i: (i, 0))
    vec = pl.BlockSpec((1, D), lambda i: (0, 0))
    ins, specs, outs, ospecs = [x], [row], [], []
    if has_res:
        ins += [y, gmod]
        specs += [row, _seg_spec(n_lat, D)]
        outs.append(SDS((R, D), F32))
        ospecs.append(row)
    if has_norm:
        ins += [nw.reshape(1, D), nmod]
        specs += [vec, _seg_spec(n_lat, D)]
        outs.append(SDS((R, D), h_dtype))
        ospecs.append(row)

    def body(*refs):
        refs = list(refs)
        z = refs.pop(0)[...]
        if has_res:
            y_ref, g_ref = refs.pop(0), refs.pop(0)
            z = z + g_ref[0, pl.ds(3 * gk + 2, 1), :] * y_ref[...].astype(F32)
        if has_norm:
            nw_ref, m_ref = refs.pop(0), refs.pop(0)
        if has_res:
            refs.pop(0)[...] = z
        if has_norm:
            r = lax.rsqrt(jnp.mean(z * z, axis=-1, keepdims=True) + EPS)
            h = (z * r) * nw_ref[...]
            h = h * (1.0 + m_ref[0, pl.ds(3 * nk + 1, 1), :]) + m_ref[0, pl.ds(3 * nk, 1), :]
            refs.pop(0)[...] = h.astype(h_dtype)

    res = pl.pallas_call(
        body, name=name, grid=(R // ROW_TILE,), in_specs=specs, out_specs=ospecs,
        out_shape=outs, compiler_params=_params(1),
    )(*ins)
    return res if len(res) > 1 else res[0]


def _gate_bwd(dz, y, mod, k, out_dtype, n_lat, name):
    R, D = dz.shape
    row = pl.BlockSpec((ROW_TILE, D), lambda i: (i, 0))

    def body(dz_ref, y_ref, m_ref, dy_ref, dg_ref):
        i = pl.program_id(0)
        dzv = dz_ref[...]
        dy_ref[...] = (m_ref[0, pl.ds(3 * k + 2, 1), :] * dzv).astype(out_dtype)

        @pl.when((i == 0) | (i == n_lat))
        def _():
            dg_ref[...] = jnp.zeros_like(dg_ref)

        dg_ref[0] += jnp.sum(dzv * y_ref[...].astype(F32), axis=0, keepdims=True)

    return pl.pallas_call(
        body, name=name, grid=(R // ROW_TILE,),
        in_specs=[row, row, _seg_spec(n_lat, D)],
        out_specs=[row, _seg_acc_spec(n_lat, D)],
        out_shape=[SDS((R, D), out_dtype), SDS((2, 1, D), F32)],
        compiler_params=_params(1),
    )(dz, y, mod)


def _norm_bwd(dz, dh, x, nw, mod, k, n_lat, name):
    R, D = x.shape
    row = pl.BlockSpec((ROW_TILE, D), lambda i: (i, 0))
    vec = pl.BlockSpec((1, D), lambda i: (0, 0))

    def body(dz_ref, dh_ref, x_ref, nw_ref, m_ref, dx_ref, dnw_ref, dsh_ref, dsc_ref):
        i = pl.program_id(0)
        xv = x_ref[...]
        dhv = dh_ref[...].astype(F32)
        nwv = nw_ref[...]
        sc1 = 1.0 + m_ref[0, pl.ds(3 * k + 1, 1), :]
        r = lax.rsqrt(jnp.mean(xv * xv, axis=-1, keepdims=True) + EPS)
        xhat = xv * r
        a = dhv * (nwv * sc1)
        dx_ref[...] = dz_ref[...] + r * (a - xhat * jnp.mean(a * xhat, axis=-1, keepdims=True))

        @pl.when(i == 0)
        def _():
            dnw_ref[...] = jnp.zeros_like(dnw_ref)

        @pl.when((i == 0) | (i == n_lat))
        def _():
            dsh_ref[...] = jnp.zeros_like(dsh_ref)
            dsc_ref[...] = jnp.zeros_like(dsc_ref)

        dnw_ref[...] += jnp.sum(dhv * xhat, axis=0, keepdims=True) * sc1
        dsh_ref[0] += jnp.sum(dhv, axis=0, keepdims=True)
        dsc_ref[0] += jnp.sum(dhv * xhat, axis=0, keepdims=True) * nwv

    return pl.pallas_call(
        body, name=name, grid=(R // ROW_TILE,),
        in_specs=[row, row, row, vec, _seg_spec(n_lat, D)],
        out_specs=[row, vec, _seg_acc_spec(n_lat, D), _seg_acc_spec(n_lat, D)],
        out_shape=[SDS((R, D), F32), SDS((1, D), F32), SDS((2, 1, D), F32), SDS((2, 1, D), F32)],
        compiler_params=_params(1),
    )(dz, dh, x, nw.reshape(1, D), mod)


def _loss_bwd(xf, target, n_lat):
    R, D = xf.shape
    row = pl.BlockSpec((ROW_TILE, D), lambda i: (i, 0))
    tgt = pl.BlockSpec((ROW_TILE, D), lambda i: (jnp.minimum(i, n_lat - 1), 0))

    def body(x_ref, t_ref, dx_ref, loss_ref):
        i = pl.program_id(0)
        e = jnp.where(i < n_lat, x_ref[...] - t_ref[...], 0.0)
        dx_ref[...] = e * (1.0 / D)

        @pl.when(i == 0)
        def _():
            loss_ref[...] = jnp.zeros_like(loss_ref)

        loss_ref[...] += 0.5 * jnp.sum(jnp.mean(e * e, axis=-1, keepdims=True))

    return pl.pallas_call(
        body, name="loss_bwd", grid=(R // ROW_TILE,),
        in_specs=[row, tgt],
        out_specs=[row, pl.BlockSpec((8, 128), lambda i: (0, 0))],
        out_shape=[SDS((R, D), F32), SDS((8, 128), F32)],
        compiler_params=_params(1),
    )(xf, target)


def _halo_specs(n_tiles, width, tile=ROW_TILE):
    per = tile // HALO
    prev = pl.BlockSpec((HALO, width), lambda i: (jnp.maximum(i * per - 1, 0), 0))
    nxt = pl.BlockSpec((HALO, width), lambda i: (jnp.minimum((i + 1) * per, n_tiles * per - 1), 0))
    return prev, nxt


def _edge_flags(i, n_lat, n_tiles):
    first = (i == 0) | (i == n_lat)
    last = (i == n_lat - 1) | (i == n_tiles - 1)
    return first, last


def _conv_gate_fwd(u, conv_w, conv_b, n_lat, name):
    R, F2 = u.shape
    F = F2 // 2
    n_tiles = R // ROW_TILE
    T = ROW_TILE
    cw = _pick(F, 256, 128)
    row = pl.BlockSpec((T, F2), lambda i: (i, 0))
    prev, nxt = _halo_specs(n_tiles, F2)

    def body(u_ref, p_ref, n_ref, w_ref, b_ref, o_ref):
        i = pl.program_id(0)
        first, last = _edge_flags(i, n_lat, n_tiles)
        ridx = lax.broadcasted_iota(jnp.int32, (T, 1), 0)

        def conv(c0):
            cols = pl.ds(c0, cw)
            uv = u_ref[:, cols]
            pr = jnp.where(first, 0.0, p_ref[pl.ds(HALO - 1, 1), cols])
            nx = jnp.where(last, 0.0, n_ref[pl.ds(0, 1), cols])
            up = jnp.where(ridx == 0, pr, pltpu.roll(uv, 1, 0))
            un = jnp.where(ridx == T - 1, nx, pltpu.roll(uv, T - 1, 0))
            return (up * w_ref[pl.ds(0, 1), cols] + uv * w_ref[pl.ds(1, 1), cols]
                    + un * w_ref[pl.ds(2, 1), cols] + b_ref[:, cols])

        for c0 in range(0, F, cw):
            ca, cv = conv(c0), conv(F + c0)
            o_ref[:, pl.ds(c0, cw)] = (ca * _sigmoid(ca) * cv).astype(BF16)

    return pl.pallas_call(
        body, name=name, grid=(n_tiles,),
        in_specs=[row, prev, nxt, pl.BlockSpec((3, F2), lambda i: (0, 0)),
                  pl.BlockSpec((1, F2), lambda i: (0, 0))],
        out_specs=pl.BlockSpec((T, F), lambda i: (i, 0)),
        out_shape=SDS((R, F), BF16), compiler_params=_params(1),
    )(u, u, u, conv_w, conv_b)


def _conv_gate_bwd(u, dgact, conv_w, conv_b, n_lat, name):
    R, F2 = u.shape
    F = F2 // 2
    n_tiles = R // ROW_TILE
    T, N = ROW_TILE, ROW_TILE + 2 * HALO
    cw = _pick(F, 256, 128)
    rowu = pl.BlockSpec((T, F2), lambda i: (i, 0))
    rowg = pl.BlockSpec((T, F), lambda i: (i, 0))
    pu, nu = _halo_specs(n_tiles, F2)
    pg, ng = _halo_specs(n_tiles, F)

    def body(u_ref, pu_ref, nu_ref, g_ref, pg_ref, ng_ref, w_ref, b_ref, du_ref, dw_ref, db_ref):
        i = pl.program_id(0)
        first, last = _edge_flags(i, n_lat, n_tiles)

        @pl.when(i == 0)
        def _():
            dw_ref[...] = jnp.zeros_like(dw_ref)
            db_ref[...] = jnp.zeros_like(db_ref)

        def ext(t_ref, p_ref, n_ref, cols):
            pr = jnp.where(first, 0.0, p_ref[:, cols])
            nx = jnp.where(last, 0.0, n_ref[:, cols])
            return jnp.concatenate([pr, t_ref[:, cols], nx], axis=0)

        def conv(c0):
            cols = pl.ds(c0, cw)
            e = ext(u_ref, pu_ref, nu_ref, cols)
            up, un = pltpu.roll(e, 1, 0), pltpu.roll(e, N - 1, 0)
            c = (up * w_ref[pl.ds(0, 1), cols] + e * w_ref[pl.ds(1, 1), cols]
                 + un * w_ref[pl.ds(2, 1), cols] + b_ref[:, cols])
            return c, up, e, un

        def back(c0, dc, up, e, un):
            cols = pl.ds(c0, cw)
            du = (pltpu.roll(dc, N - 1, 0) * w_ref[pl.ds(0, 1), cols] + dc * w_ref[pl.ds(1, 1), cols]
                  + pltpu.roll(dc, 1, 0) * w_ref[pl.ds(2, 1), cols])
            du_ref[:, cols] = du[HALO:HALO + T].astype(BF16)
            dct = dc[HALO:HALO + T]
            dw_ref[pl.ds(0, 1), cols] += jnp.sum(dct * up[HALO:HALO + T], axis=0, keepdims=True)
            dw_ref[pl.ds(1, 1), cols] += jnp.sum(dct * e[HALO:HALO + T], axis=0, keepdims=True)
            dw_ref[pl.ds(2, 1), cols] += jnp.sum(dct * un[HALO:HALO + T], axis=0, keepdims=True)
            db_ref[:, cols] += jnp.sum(dct, axis=0, keepdims=True)

        for c0 in range(0, F, cw):
            dg = ext(g_ref, pg_ref, ng_ref, pl.ds(c0, cw))
            ca, upa, ea, una = conv(c0)
            cv, upv, ev, unv = conv(F + c0)
            s = _sigmoid(ca)
            back(F + c0, dg * (ca * s), upv, ev, unv)
            back(c0, dg * cv * (s * (1.0 + ca * (1.0 - s))), upa, ea, una)

    return pl.pallas_call(
        body, name=name, grid=(n_tiles,),
        in_specs=[rowu, pu, nu, rowg, pg, ng, pl.BlockSpec((3, F2), lambda i: (0, 0)),
                  pl.BlockSpec((1, F2), lambda i: (0, 0))],
        out_specs=[rowu, pl.BlockSpec((3, F2), lambda i: (0, 0)), pl.BlockSpec((1, F2), lambda i: (0, 0))],
        out_shape=[SDS((R, F2), BF16), SDS((3, F2), F32), SDS((1, F2), F32)],
        compiler_params=_params(1),
    )(u, u, u, dgact, dgact, dgact, conv_w, conv_b)


def _pool_counts(i, n_lat, s_len, l_len, n_rows, offset):
    ctx = i >= n_lat
    t0 = jnp.where(ctx, i - n_lat, i) * ROW_TILE + offset
    seg = jnp.where(ctx, l_len, s_len)
    t = t0 + lax.broadcasted_iota(jnp.int32, (n_rows, 1), 0)
    out = []
    for win in POOL_WINDOWS:
        cnt = jnp.minimum(t + win // 2, seg) - jnp.maximum(t - win // 2, 0)
        out.append(jnp.maximum(cnt, 1).astype(F32))
    return out


def _window_sum(e, lo, hi, n):
    acc = None
    for j in range(lo, hi + 1):
        term = e if j == 0 else pltpu.roll(e, (-j) % n, 0)
        acc = term if acc is None else acc + term
    return acc


def _pool_fwd(h, w, b, scale, n_lat, s_len, l_len, name):
    R, D = h.shape
    G = D // 4
    n_tiles = R // ROW_TILE
    T, N = ROW_TILE, ROW_TILE + 2 * HALO
    row = pl.BlockSpec((T, D), lambda i: (i, 0))
    prev, nxt = _halo_specs(n_tiles, D)
    vec = pl.BlockSpec((1, D), lambda i: (0, 0))

    def body(h_ref, p_ref, n_ref, w_ref, b_ref, s_ref, y_ref):
        i = pl.program_id(0)
        first, last = _edge_flags(i, n_lat, n_tiles)
        cnts = _pool_counts(i, n_lat, s_len, l_len, T, 0)
        for g, win in enumerate(POOL_WINDOWS):
            cols = pl.ds(g * G, G)
            pr = jnp.where(first, 0.0, p_ref[:, cols])
            nx = jnp.where(last, 0.0, n_ref[:, cols])
            hv = h_ref[:, cols]
            e = jnp.concatenate([pr, hv, nx], axis=0)
            mean = _window_sum(e, -(win // 2), win // 2 - 1, N)[HALO:HALO + T] / cnts[g]
            yg = _dot(_bf(mean - hv), w_ref[g])
            y_ref[:, cols] = (yg + b_ref[:, cols]) * s_ref[:, cols]

    return pl.pallas_call(
        body, name=name, grid=(n_tiles,),
        in_specs=[row, prev, nxt, pl.BlockSpec((4, G, G), lambda i: (0, 0, 0)), vec, vec],
        out_specs=row, out_shape=SDS((R, D), F32), compiler_params=_params(1),
    )(h, h, h, w, b, scale)


def _pool_bwd(h, dy, w, b, scale, n_lat, s_len, l_len, name):
    R, D = h.shape
    G = D // 4
    n_tiles = R // ROW_TILE
    T, N = ROW_TILE, ROW_TILE + 2 * HALO
    row = pl.BlockSpec((T, D), lambda i: (i, 0))
    prev, nxt = _halo_specs(n_tiles, D)
    vec = pl.BlockSpec((1, D), lambda i: (0, 0))
    wspec = pl.BlockSpec((4, G, G), lambda i: (0, 0, 0))

    def body(h_ref, ph_ref, nh_ref, d_ref, pd_ref, nd_ref, w_ref, b_ref, s_ref,
             dh_ref, dw_ref, db_ref, ds_ref):
        i = pl.program_id(0)
        first, last = _edge_flags(i, n_lat, n_tiles)

        @pl.when(i == 0)
        def _():
            dw_ref[...] = jnp.zeros_like(dw_ref)
            db_ref[...] = jnp.zeros_like(db_ref)
            ds_ref[...] = jnp.zeros_like(ds_ref)

        cnts = _pool_counts(i, n_lat, s_len, l_len, T, 0)
        cnts_ext = _pool_counts(i, n_lat, s_len, l_len, N, -HALO)
        for g, win in enumerate(POOL_WINDOWS):
            cols = pl.ds(g * G, G)

            def ext(t_ref, p_ref, n_ref):
                pr = jnp.where(first, 0.0, p_ref[:, cols])
                nx = jnp.where(last, 0.0, n_ref[:, cols])
                return jnp.concatenate([pr, t_ref[:, cols], nx], axis=0)

            hv = h_ref[:, cols]
            mean = _window_sum(ext(h_ref, ph_ref, nh_ref), -(win // 2), win // 2 - 1, N)[HALO:HALO + T] / cnts[g]
            z = _bf(mean - hv)
            sc = s_ref[:, cols]
            dye = ext(d_ref, pd_ref, nd_ref)
            dt = _bf(dye * sc)
            dz = _dot_nt(dt, w_ref[g])
            dm = dz / cnts_ext[g]
            dh = _window_sum(dm, -(win // 2 - 1), win // 2, N) - dz
            dh_ref[:, cols] = dh[HALO:HALO + T]
            dyt = dye[HALO:HALO + T]
            dw_ref[g] += _dot_tn(z, dt[HALO:HALO + T])
            db_ref[:, cols] += jnp.sum(dyt * sc, axis=0, keepdims=True)
            ds_ref[:, cols] += jnp.sum(dyt * (_dot(z, w_ref[g]) + b_ref[:, cols]), axis=0, keepdims=True)

    return pl.pallas_call(
        body, name=name, grid=(n_tiles,),
        in_specs=[row, prev, nxt, row, prev, nxt, wspec, vec, vec],
        out_specs=[row, wspec, vec, vec],
        out_shape=[SDS((R, D), F32), SDS((4, G, G), F32), SDS((1, D), F32), SDS((1, D), F32)],
        compiler_params=_params(1),
    )(h, h, h, dy, dy, dy, w, b, scale)


def _rope_tables(s_len, l_len):
    t = jnp.arange(s_len)
    row = (t // GRID_W).astype(F32)
    col = (t % GRID_W).astype(F32)
    axis_dim = HEAD_DIM // 2
    inv = ROPE_THETA ** (-jnp.arange(0, axis_dim, 2, dtype=F32) / axis_dim)
    ar, ac = row[:, None] * inv, col[:, None] * inv
    cos = jnp.concatenate([jnp.cos(ar), jnp.cos(ar), jnp.cos(ac), jnp.cos(ac)], axis=-1)
    sin = jnp.concatenate([-jnp.sin(ar), jnp.sin(ar), -jnp.sin(ac), jnp.sin(ac)], axis=-1)
    cos = jnp.concatenate([cos, jnp.ones((l_len, HEAD_DIM), F32)], axis=0)
    sin = jnp.concatenate([sin, jnp.zeros((l_len, HEAD_DIM), F32)], axis=0)
    return cos, sin


def _swap_halves(v):
    lane = lax.broadcasted_iota(jnp.int32, v.shape, 1)
    return jnp.where((lane % 64) < 32, pltpu.roll(v, 96, 1), pltpu.roll(v, 32, 1))


def _qk_prep_fwd(qkv, q_gain, k_gain, cos, sin):
    R = qkv.shape[0]
    NQ, NK = N_HEADS * HEAD_DIM, N_KV * HEAD_DIM
    T = ROW_TILE
    vec = pl.BlockSpec((1, HEAD_DIM), lambda i: (0, 0))
    tab = pl.BlockSpec((T, HEAD_DIM), lambda i: (i, 0))

    def body(x_ref, qg_ref, kg_ref, c_ref, s_ref, q_ref, k_ref, v_ref):
        cosv, sinv = c_ref[...], s_ref[...]

        def prep(c0, gain):
            xh = x_ref[:, pl.ds(c0, HEAD_DIM)]
            xn = xh * lax.rsqrt(jnp.mean(xh * xh, axis=-1, keepdims=True) + EPS) * gain
            return _bf(xn * cosv + _swap_halves(xn) * sinv)

        for hd in range(N_HEADS):
            q_ref[:, pl.ds(hd * HEAD_DIM, HEAD_DIM)] = prep(hd * HEAD_DIM, qg_ref[...])
        for hd in range(N_KV):
            k_ref[:, pl.ds(hd * HEAD_DIM, HEAD_DIM)] = prep(NQ + hd * HEAD_DIM, kg_ref[...])
            v_ref[:, pl.ds(2 * hd * HEAD_DIM, HEAD_DIM)] = _bf(x_ref[:, pl.ds(NQ + NK + hd * HEAD_DIM, HEAD_DIM)])
            v_ref[:, pl.ds((2 * hd + 1) * HEAD_DIM, HEAD_DIM)] = jnp.ones((T, HEAD_DIM), BF16)

    return pl.pallas_call(
        body, name="qk_prep_fwd", grid=(R // T,),
        in_specs=[pl.BlockSpec((T, NQ + 2 * NK), lambda i: (i, 0)), vec, vec, tab, tab],
        out_specs=[pl.BlockSpec((T, NQ), lambda i: (i, 0)), pl.BlockSpec((T, NK), lambda i: (i, 0)),
                   pl.BlockSpec((T, 2 * NK), lambda i: (i, 0))],
        out_shape=[SDS((R, NQ), BF16), SDS((R, NK), BF16), SDS((R, 2 * NK), BF16)],
        compiler_params=_params(1),
    )(qkv, q_gain, k_gain, cos, sin)


def _qk_prep_bwd(qkv, dq, dk, dv, q_gain, k_gain, cos, sin):
    R = qkv.shape[0]
    NQ, NK = N_HEADS * HEAD_DIM, N_KV * HEAD_DIM
    T = ROW_TILE
    vec = pl.BlockSpec((1, HEAD_DIM), lambda i: (0, 0))
    tab = pl.BlockSpec((T, HEAD_DIM), lambda i: (i, 0))

    def body(x_ref, dq_ref, dk_ref, dv_ref, qg_ref, kg_ref, c_ref, s_ref, o_ref, dqg_ref, dkg_ref):
        i = pl.program_id(0)
        cosv, sinv = c_ref[...], s_ref[...]

        @pl.when(i == 0)
        def _():
            dqg_ref[...] = jnp.zeros_like(dqg_ref)
            dkg_ref[...] = jnp.zeros_like(dkg_ref)

        def back(c0, dout, gain, dg_ref):
            xh = x_ref[:, pl.ds(c0, HEAD_DIM)]
            r = lax.rsqrt(jnp.mean(xh * xh, axis=-1, keepdims=True) + EPS)
            xhat = xh * r
            dxn = dout * cosv + _swap_halves(dout * sinv)
            dg_ref[...] += jnp.sum(dxn * xhat, axis=0, keepdims=True)
            a = dxn * gain
            o_ref[:, pl.ds(c0, HEAD_DIM)] = _bf(r * (a - xhat * jnp.mean(a * xhat, axis=-1, keepdims=True)))

        for hd in range(N_HEADS):
            back(hd * HEAD_DIM, dq_ref[:, pl.ds(hd * HEAD_DIM, HEAD_DIM)], qg_ref[...], dqg_ref)
        for hd in range(N_KV):
            back(NQ + hd * HEAD_DIM, dk_ref[:, pl.ds(hd * HEAD_DIM, HEAD_DIM)], kg_ref[...], dkg_ref)
        o_ref[:, pl.ds(NQ + NK, NK)] = _bf(dv_ref[...])

    return pl.pallas_call(
        body, name="qk_prep_bwd", grid=(R // T,),
        in_specs=[pl.BlockSpec((T, NQ + 2 * NK), lambda i: (i, 0)), pl.BlockSpec((T, NQ), lambda i: (i, 0)),
                  pl.BlockSpec((T, NK), lambda i: (i, 0)), pl.BlockSpec((T, NK), lambda i: (i, 0)),
                  vec, vec, tab, tab],
        out_specs=[pl.BlockSpec((T, NQ + 2 * NK), lambda i: (i, 0)), vec, vec],
        out_shape=[SDS((R, NQ + 2 * NK), BF16), SDS((1, HEAD_DIM), F32), SDS((1, HEAD_DIM), F32)],
        compiler_params=_params(1),
    )(qkv, dq, dk, dv, q_gain, k_gain, cos, sin)


def _flash_fwd(q, k, v, s_len, l_len):
    R = q.shape[0]
    T = FLASH_FWD_TILE
    n_lat = s_len // T
    ck = _pick(s_len, 512, 128)
    scale = HEAD_DIM ** -0.5
    group = N_HEADS // N_KV
    GW = group * HEAD_DIM
    M = group * T
    chunks = s_len // ck
    to_log2 = scale * math.log2(math.e)

    def body(q_ref, k_ref, v_ref, o_ref, lse_ref, s_s, sc_s, ml_s, mb_s, acc_s):
        i = pl.program_id(1)
        qv = jnp.concatenate([q_ref[:, pl.ds(hh * HEAD_DIM, HEAD_DIM)] for hh in range(group)], axis=0)

        ml_s[...] = jnp.full_like(ml_s, -jnp.inf)

        def lane_max(s, n):
            m = ml_s[...]
            for t in range(n // HEAD_DIM):
                m = jnp.maximum(m, s[:, t * HEAD_DIM:(t + 1) * HEAD_DIM])
            ml_s[...] = m

        @pl.when(i < n_lat)
        def _():
            def loop(c, carry):
                s = _dot_nt(qv, k_ref[pl.ds(pl.multiple_of(c * ck, ck), ck), :])
                s_s[c] = s
                lane_max(s, ck)
                return carry
            lax.fori_loop(0, chunks, loop, 0, unroll=4 if chunks % 4 == 0 else 1)

        sc = _dot_nt(qv, k_ref[pl.ds(s_len, l_len), :])
        sc_s[...] = sc
        lane_max(sc, l_len)
        m_row = jnp.max(ml_s[...], axis=-1, keepdims=True) * to_log2
        mb_s[...] = jnp.broadcast_to(m_row, (M, ck))

        acc_s[...] = jnp.zeros_like(acc_s)

        @pl.when(i < n_lat)
        def _():
            def loop(c, carry):
                p = jnp.exp2(s_s[c] * to_log2 - mb_s[...])
                acc_s[...] += _dot(_bf(p), v_ref[pl.ds(pl.multiple_of(c * ck, ck), ck), :])
                return carry
            lax.fori_loop(0, chunks, loop, 0, unroll=4 if chunks % 4 == 0 else 1)

        p = jnp.exp2(sc_s[...] * to_log2 - mb_s[:, pl.ds(0, l_len)])
        acc_s[...] += _dot(_bf(p), v_ref[pl.ds(s_len, l_len), :])
        l_rep = acc_s[:, pl.ds(HEAD_DIM, HEAD_DIM)]
        o = acc_s[:, pl.ds(0, HEAD_DIM)] / l_rep
        for hh in range(group):
            o_ref[:, pl.ds(hh * HEAD_DIM, HEAD_DIM)] = o[hh * T:(hh + 1) * T]
        lse = (mb_s[:, pl.ds(0, HEAD_DIM)] + jnp.log2(l_rep)) * math.log(2.0)
        lse_ref[...] = jnp.max(lse, axis=-1, keepdims=True).reshape(group, T, 1)

    return pl.pallas_call(
        body, name="flash_fwd", grid=(N_KV, R // T),
        in_specs=[pl.BlockSpec((T, GW), lambda g, i: (i, g)),
                  pl.BlockSpec((R, HEAD_DIM), lambda g, i: (0, g)),
                  pl.BlockSpec((R, 2 * HEAD_DIM), lambda g, i: (0, g))],
        out_specs=[pl.BlockSpec((T, GW), lambda g, i: (i, g)),
                   pl.BlockSpec((group, T, 1), lambda g, i: (g, i, 0))],
        out_shape=[SDS((R, N_HEADS * HEAD_DIM), F32), SDS((N_HEADS, R, 1), F32)],
        scratch_shapes=[pltpu.VMEM((chunks, M, ck), F32), pltpu.VMEM((M, l_len), F32), pltpu.VMEM((M, HEAD_DIM), F32),
                        pltpu.VMEM((M, ck), F32), pltpu.VMEM((M, 2 * HEAD_DIM), F32)],
        compiler_params=_params(2),
    )(q, k, v)


def _flash_bwd(q, k, v, o, lse, do, s_len, l_len):
    R = q.shape[0]
    T = ROW_TILE
    n_lat = s_len // T
    ck = _pick(s_len, 512, 128)
    scale = HEAD_DIM ** -0.5
    group = N_HEADS // N_KV
    GW = group * HEAD_DIM
    qspec = pl.BlockSpec((T, GW), lambda g, i: (i, g))
    kspec = pl.BlockSpec((R, HEAD_DIM), lambda g, i: (0, g))

    M = group * T
    log2e = math.log2(math.e)

    def body(q_ref, do_ref, o_ref, lse_ref, k_ref, v_ref, dq_ref, dk_ref, dv_ref, dq_s, lse_s, delta_s):
        i = pl.program_id(1)

        @pl.when(i == 0)
        def _():
            dk_ref[...] = jnp.zeros_like(dk_ref)
            dv_ref[...] = jnp.zeros_like(dv_ref)

        def stacked(ref):
            return jnp.concatenate([ref[:, pl.ds(hh * HEAD_DIM, HEAD_DIM)] for hh in range(group)], axis=0)

        qv = stacked(q_ref)
        dov = stacked(do_ref)
        dob = _bf(dov)
        delta_s[...] = jnp.broadcast_to(jnp.sum(dov * stacked(o_ref), axis=-1, keepdims=True), (M, ck))
        lse_s[...] = jnp.broadcast_to(lse_ref[...].reshape(M, 1) * log2e, (M, ck))
        dq_s[...] = jnp.zeros_like(dq_s)

        def step(rows, n):
            kv, vv = k_ref[rows, :], v_ref[rows, :]
            p = jnp.exp2(_dot_nt(qv, kv) * (scale * log2e) - lse_s[:, pl.ds(0, n)])
            dv_ref[rows, :] += _dot_tn(_bf(p), dob)
            ds = _bf(p * (_dot_nt(dob, vv) - delta_s[:, pl.ds(0, n)]) * scale)
            dq_s[...] += _dot(ds, kv)
            dk_ref[rows, :] += _dot_tn(ds, qv)

        @pl.when(i < n_lat)
        def _():
            def loop(c, carry):
                step(pl.ds(pl.multiple_of(c * ck, ck), ck), ck)
                return carry
            lax.fori_loop(0, s_len // ck, loop, 0)

        step(pl.ds(s_len, l_len), l_len)
        for hh in range(group):
            dq_ref[:, pl.ds(hh * HEAD_DIM, HEAD_DIM)] = dq_s[pl.ds(hh * T, T), :]

    return pl.pallas_call(
        body, name="flash_bwd", grid=(N_KV, R // T),
        in_specs=[qspec, qspec, qspec, pl.BlockSpec((group, T, 1), lambda g, i: (g, i, 0)), kspec,
                  pl.BlockSpec((R, HEAD_DIM), lambda g, i: (0, 2 * g))],
        out_specs=[qspec, kspec, kspec],
        out_shape=[SDS((R, N_HEADS * HEAD_DIM), F32), SDS((R, N_KV * HEAD_DIM), F32),
                   SDS((R, N_KV * HEAD_DIM), F32)],
        scratch_shapes=[pltpu.VMEM((M, HEAD_DIM), F32), pltpu.VMEM((M, ck), F32), pltpu.VMEM((M, ck), F32)],
        compiler_params=_params(2),
    )(q, do, o, lse, k, v)


K_SCALE = RET_DK ** -0.5


def _log_sigmoid(v):
    return -(jnp.maximum(-v, 0.0) + jnp.log(1.0 + jnp.exp(-jnp.abs(v))))


def _ret_decays(d, lg):
    C = RET_CHUNK
    ic = lax.broadcasted_iota(jnp.int32, (C, 1), 0)
    ir = lax.broadcasted_iota(jnp.int32, (1, C), 1)
    li = jnp.where(d == 0, ic, C - 1 - ic).astype(F32)
    lj = jnp.where(d == 0, ir, C - 1 - ir).astype(F32)
    diff = li - lj
    mask = jnp.where(diff >= 0, jnp.exp(jnp.maximum(diff, 0.0) * lg), 0.0)
    qd = jnp.exp((li + 1.0) * lg)
    kd = jnp.exp((C - 1.0 - li) * lg)
    cd = jnp.exp(C * lg)
    return li, diff, mask, qd, kd, cd


def _ctx_weights(d, t, lg, l_len):
    C = RET_CHUNK
    j = (t * C + lax.broadcasted_iota(jnp.int32, (C, 1), 0)).astype(F32)
    e = jnp.where(d == 0, (l_len - 1.0) - j, j)
    return e, jnp.exp(e * lg)


def _ret_specs(n_lat_c, n_ctx_c, ctx_first):
    def blk(d, t):
        if ctx_first:
            n = jnp.maximum(t - n_ctx_c, 0)
            lat = jnp.where(d == 0, n, n_lat_c - 1 - n)
            return jnp.where(t < n_ctx_c, n_lat_c + t, lat)
        n = jnp.minimum(t, n_lat_c - 1)
        lat = jnp.where(d == 0, n_lat_c - 1 - n, n)
        return jnp.where(t >= n_lat_c, t, lat)
    return blk


def _ret_fwd(proj, lgt, s_len, l_len):
    R = proj.shape[0]
    C, H, DK, DV = RET_CHUNK, RET_HEADS, RET_DK, RET_DV
    nl, nc = s_len // C, l_len // C
    blk = _ret_specs(nl, nc, True)

    def body(q_ref, k_ref, v_ref, lg_ref, o_ref, st_ref, r_s):
        d, t = pl.program_id(0), pl.program_id(1)

        @pl.when(t == 0)
        def _():
            r_s[...] = jnp.zeros_like(r_s)

        def log_gamma(hh):
            return jnp.max(_log_sigmoid(lg_ref[0, hh]), axis=-1, keepdims=True)

        @pl.when(t < nc)
        def _():
            for hh in range(H):
                qc, vc = pl.ds(hh * DK, DK), pl.ds(hh * DV, DV)
                _, w = _ctx_weights(d, t, log_gamma(hh), l_len)
                r_s[hh] += _dot_tn(_bf(k_ref[:, qc] * K_SCALE * w), _bf(v_ref[:, vc]))
                o_ref[0, :, vc] = jnp.zeros((C, DV), F32)

        @pl.when(t >= nc)
        def _():
            for hh in range(H):
                qc, vc = pl.ds(hh * DK, DK), pl.ds(hh * DV, DV)
                _, _, mask, qd, kd, cd = _ret_decays(d, log_gamma(hh))
                qb, kv, vb = _bf(q_ref[:, qc]), k_ref[:, qc] * K_SCALE, _bf(v_ref[:, vc])
                r = r_s[hh]
                st_ref[0, hh, 0] = r
                att = _dot_nt(qb, _bf(kv)) * mask
                o_ref[0, :, vc] = _dot(_bf(att), vb) + _dot(qb, _bf(r)) * qd
                r_s[hh] = r * cd + _dot_tn(_bf(kv * kd), vb)

    return pl.pallas_call(
        body, name="ret_fwd", grid=(2, nc + nl),
        in_specs=[pl.BlockSpec((C, H * DK), lambda d, t: (blk(d, t), 0)),
                  pl.BlockSpec((C, H * DK), lambda d, t: (blk(d, t), 1)),
                  pl.BlockSpec((C, H * DV), lambda d, t: (blk(d, t), 1)),
                  pl.BlockSpec((1, H, 1, 128), lambda d, t: (d, 0, 0, 0))],
        out_specs=[pl.BlockSpec((1, C, H * DV), lambda d, t: (d, blk(d, t), 0)),
                   pl.BlockSpec((1, H, 1, DK, DV), lambda d, t: (d, 0, jnp.maximum(t - nc, 0), 0, 0))],
        out_shape=[SDS((2, R, H * DV), F32), SDS((2, H, nl, DK, DV), F32)],
        scratch_shapes=[pltpu.VMEM((H, DK, DV), F32)],
        compiler_params=_params(2),
    )(proj, proj, proj, lgt)


def _ret_bwd(proj, lgt, states, do, s_len, l_len):
    R = proj.shape[0]
    C, H, DK, DV = RET_CHUNK, RET_HEADS, RET_DK, RET_DV
    nl, nc = s_len // C, l_len // C
    blk = _ret_specs(nl, nc, False)
    last = nl + nc - 1

    def body(q_ref, k_ref, v_ref, lg_ref, st_ref, do_ref, dq_ref, dk_ref, dv_ref, dlg_ref, dr_s, dl_s):
        d, t = pl.program_id(0), pl.program_id(1)

        def log_gamma(hh):
            return jnp.max(_log_sigmoid(lg_ref[0, hh]), axis=-1, keepdims=True)

        @pl.when(t == 0)
        def _():
            dr_s[...] = jnp.zeros_like(dr_s)
            dl_s[...] = jnp.zeros_like(dl_s)

        @pl.when(t < nl)
        def _():
            for hh in range(H):
                qc, vc = pl.ds(hh * DK, DK), pl.ds(hh * DV, DV)
                li, diff, mask, qd, kd, cd = _ret_decays(d, log_gamma(hh))
                qv, kv, vv, dov = q_ref[:, qc], k_ref[:, qc] * K_SCALE, v_ref[:, vc], do_ref[:, vc]
                qb, kb, vb, dob = _bf(qv), _bf(kv), _bf(vv), _bf(dov)
                r, drn = st_ref[0, hh, 0], dr_s[hh]
                rb, drb = _bf(r), _bf(drn)
                p = _dot_nt(qb, kb)
                dp = _dot_nt(dob, vb) * mask
                dpb = _bf(dp)
                doq = _bf(dov * qd)
                dq_inter = _dot_nt(doq, rb)
                dk_state = kd * _dot_nt(vb, drb)
                dq_ref[0, :, qc] = _dot(dpb, kb) + dq_inter
                dk_ref[0, :, qc] = (_dot_tn(dpb, qb) + dk_state) * K_SCALE
                dv_ref[0, :, vc] = _dot_tn(_bf(p * mask), dob) + _dot(_bf(kv * kd), drb)
                dr_s[hh] = cd * drn + _dot_tn(qb, doq)
                dl_s[hh] += (jnp.sum(dp * p * diff) + jnp.sum((li + 1.0) * qv * dq_inter)
                             + jnp.sum((C - 1.0 - li) * kv * dk_state) + C * jnp.sum(cd * r * drn))

        @pl.when(t >= nl)
        def _():
            for hh in range(H):
                qc, vc = pl.ds(hh * DK, DK), pl.ds(hh * DV, DV)
                e, w = _ctx_weights(d, t - nl, log_gamma(hh), l_len)
                kv, vb, drb = k_ref[:, qc] * K_SCALE, _bf(v_ref[:, vc]), _bf(dr_s[hh])
                dkc = w * _dot_nt(vb, drb)
                dq_ref[0, :, qc] = jnp.zeros((C, DK), F32)
                dk_ref[0, :, qc] = dkc * K_SCALE
                dv_ref[0, :, vc] = _dot(_bf(kv * w), drb)
                dl_s[hh] += jnp.sum(e * kv * dkc)

        @pl.when(t == last)
        def _():
            for hh in range(H):
                dlg_ref[0, hh] = dl_s[hh] * (1.0 / (1.0 + jnp.exp(lg_ref[0, hh])))

    return pl.pallas_call(
        body, name="ret_bwd", grid=(2, nl + nc),
        in_specs=[pl.BlockSpec((C, H * DK), lambda d, t: (blk(d, t), 0)),
                  pl.BlockSpec((C, H * DK), lambda d, t: (blk(d, t), 1)),
                  pl.BlockSpec((C, H * DV), lambda d, t: (blk(d, t), 1)),
                  pl.BlockSpec((1, H, 1, 128), lambda d, t: (d, 0, 0, 0)),
                  pl.BlockSpec((1, H, 1, DK, DV), lambda d, t: (d, 0, jnp.maximum(nl - 1 - t, 0), 0, 0)),
                  pl.BlockSpec((C, H * DV), lambda d, t: (blk(d, t), 0))],
        out_specs=[pl.BlockSpec((1, C, H * DK), lambda d, t: (d, blk(d, t), 0)),
                   pl.BlockSpec((1, C, H * DK), lambda d, t: (d, blk(d, t), 0)),
                   pl.BlockSpec((1, C, H * DV), lambda d, t: (d, blk(d, t), 0)),
                   pl.BlockSpec((1, H, 1, 128), lambda d, t: (d, 0, 0, 0))],
        out_shape=[SDS((2, R, H * DK), F32), SDS((2, R, H * DK), F32), SDS((2, R, H * DV), F32),
                   SDS((2, H, 1, 128), F32)],
        scratch_shapes=[pltpu.VMEM((H, DK, DV), F32), pltpu.VMEM((H, 1, 128), F32)],
        compiler_params=_params(2),
    )(proj, proj, proj, lgt, states, do)


def _readout_fwd(o2, proj, gn_w):
    R = proj.shape[0]
    H, DV = RET_HEADS, RET_DV
    W = H * DV
    T = ROW_TILE

    def body(o_ref, g_ref, w_ref, out_ref):
        for hh in range(H):
            cols = pl.ds(hh * DV, DV)
            y = o_ref[0, :, cols] + o_ref[1, :, cols]
            yc = y - jnp.mean(y, axis=-1, keepdims=True)
            yn = yc * lax.rsqrt(jnp.mean(yc * yc, axis=-1, keepdims=True) + EPS) * w_ref[:, cols]
            g = g_ref[:, cols]
            out_ref[:, cols] = _bf(g * _sigmoid(g) * yn)

    return pl.pallas_call(
        body, name="readout_fwd", grid=(R // T,),
        in_specs=[pl.BlockSpec((2, T, W), lambda i: (0, i, 0)), pl.BlockSpec((T, W), lambda i: (i, 2)),
                  pl.BlockSpec((1, W), lambda i: (0, 0))],
        out_specs=pl.BlockSpec((T, W), lambda i: (i, 0)),
        out_shape=SDS((R, W), BF16), compiler_params=_params(1),
    )(o2, proj, gn_w)


def _readout_bwd(o2, proj, gn_w, dgated):
    R = proj.shape[0]
    H, DV = RET_HEADS, RET_DV
    W = H * DV
    T = ROW_TILE

    def body(o_ref, g_ref, w_ref, d_ref, do_ref, dg_ref, dw_ref):
        i = pl.program_id(0)

        @pl.when(i == 0)
        def _():
            dw_ref[...] = jnp.zeros_like(dw_ref)

        for hh in range(H):
            cols = pl.ds(hh * DV, DV)
            y = o_ref[0, :, cols] + o_ref[1, :, cols]
            yc = y - jnp.mean(y, axis=-1, keepdims=True)
            rstd = lax.rsqrt(jnp.mean(yc * yc, axis=-1, keepdims=True) + EPS)
            yn0 = yc * rstd
            wv = w_ref[:, cols]
            g = g_ref[:, cols]
            s = _sigmoid(g)
            dgt = d_ref[:, cols]
            dyn = dgt * (g * s)
            dg_ref[:, cols] = _bf(dgt * (yn0 * wv) * (s * (1.0 + g * (1.0 - s))))
            dw_ref[:, cols] += jnp.sum(dyn * yn0, axis=0, keepdims=True)
            a = dyn * wv
            do_ref[:, cols] = rstd * (a - jnp.mean(a, axis=-1, keepdims=True)
                                      - yn0 * jnp.mean(a * yn0, axis=-1, keepdims=True))

    return pl.pallas_call(
        body, name="readout_bwd", grid=(R // T,),
        in_specs=[pl.BlockSpec((2, T, W), lambda i: (0, i, 0)), pl.BlockSpec((T, W), lambda i: (i, 2)),
                  pl.BlockSpec((1, W), lambda i: (0, 0)), pl.BlockSpec((T, W), lambda i: (i, 0))],
        out_specs=[pl.BlockSpec((T, W), lambda i: (i, 0)), pl.BlockSpec((T, W), lambda i: (i, 0)),
                   pl.BlockSpec((1, W), lambda i: (0, 0))],
        out_shape=[SDS((R, W), F32), SDS((R, W), BF16), SDS((1, W), F32)],
        compiler_params=_params(1),
    )(o2, proj, gn_w, dgated)


def _ret_dproj(dq2, dk2, dv2, dg):
    R = dg.shape[0]
    NQ, NV = RET_HEADS * RET_DK, RET_HEADS * RET_DV
    T = ROW_TILE

    def body(dq_ref, dk_ref, dv_ref, dg_ref, o_ref):
        o_ref[:, pl.ds(0, NQ)] = _bf(dq_ref[0] + dq_ref[1])
        o_ref[:, pl.ds(NQ, NQ)] = _bf(dk_ref[0] + dk_ref[1])
        o_ref[:, pl.ds(2 * NQ, NV)] = _bf(dv_ref[0] + dv_ref[1])
        o_ref[:, pl.ds(2 * NQ + NV, NV)] = dg_ref[...]

    return pl.pallas_call(
        body, name="ret_dproj", grid=(R // T,),
        in_specs=[pl.BlockSpec((2, T, NQ), lambda i: (0, i, 0)), pl.BlockSpec((2, T, NQ), lambda i: (0, i, 0)),
                  pl.BlockSpec((2, T, NV), lambda i: (0, i, 0)), pl.BlockSpec((T, NV), lambda i: (i, 0))],
        out_specs=pl.BlockSpec((T, 2 * NQ + 2 * NV), lambda i: (i, 0)),
        out_shape=SDS((R, 2 * NQ + 2 * NV), BF16), compiler_params=_params(1),
    )(dq2, dk2, dv2, dg)


def _silu(v):
    return v * _sigmoid(v)


def _ada_fwd(c_rows, ada_w, ada_b_shard):
    depth, D, cols = ada_w.shape

    def body(c_ref, w_ref, b_ref, o_ref):
        o_ref[0] = _dot(_bf(_silu(c_ref[...])), _bf(w_ref[0])) + b_ref[0]

    return pl.pallas_call(
        body, name="ada_fwd", grid=(depth,),
        in_specs=[pl.BlockSpec((16, D), lambda i: (0, 0)), pl.BlockSpec((1, D, cols), lambda i: (i, 0, 0)),
                  pl.BlockSpec((1, 1, cols), lambda i: (i, 0, 0))],
        out_specs=pl.BlockSpec((1, 16, cols), lambda i: (i, 0, 0)),
        out_shape=SDS((depth, 16, cols), F32), compiler_params=_params(1),
    )(c_rows, ada_w, ada_b_shard)


def _ada_bwd(c_rows, ada_w, d_lat, d_ctx):
    depth, D, cols = ada_w.shape

    def body(c_ref, w_ref, dl_ref, dc_ref, dw_ref, pc_ref):
        i = pl.program_id(0)
        cv = c_ref[...]
        a = _silu(cv)
        dcs = jnp.broadcast_to(jnp.sum(dc_ref[0], axis=0, keepdims=True), (8, cols))
        dw_ref[0] = _dot_tn(_bf(a[0:8]), _bf(dl_ref[0])) + _dot_tn(_bf(a[8:16]), _bf(dcs))

        @pl.when(i == 0)
        def _():
            pc_ref[...] = jnp.zeros_like(pc_ref)

        pc_ref[...] += _dot_nt(_bf(dcs), _bf(w_ref[0]))

        @pl.when(i == depth - 1)
        def _():
            cc = c_ref[pl.ds(8, 1), :]
            s = _sigmoid(cc)
            pc_ref[...] = pc_ref[...] * (s * (1.0 + cc * (1.0 - s)))

    return pl.pallas_call(
        body, name="ada_bwd", grid=(depth,),
        in_specs=[pl.BlockSpec((16, D), lambda i: (0, 0)), pl.BlockSpec((1, D, cols), lambda i: (i, 0, 0)),
                  pl.BlockSpec((1, 8, cols), lambda i: (i, 0, 0)), pl.BlockSpec((1, 8, cols), lambda i: (i, 0, 0))],
        out_specs=[pl.BlockSpec((1, D, cols), lambda i: (i, 0, 0)), pl.BlockSpec((8, D), lambda i: (0, 0))],
        out_shape=[SDS((depth, D, cols), F32), SDS((8, D), F32)], compiler_params=_params(1),
    )(c_rows, ada_w, d_lat, d_ctx)


def _adamw(w, g, m, v, name):
    shape = w.shape
    n = g.shape[0]
    cols = shape[-1]
    rows = w.size // cols
    tr = _pick(rows, 512, 8) if rows * cols * 4 > (1 << 20) else rows
    spec = pl.BlockSpec((tr, cols), lambda i: (i, 0))

    def body(w_ref, g_ref, m_ref, v_ref, go_ref, d_ref, mo_ref, vo_ref):
        gs = g_ref[0].astype(F32)
        for k in range(1, n):
            gs = gs + g_ref[k].astype(F32)
        mn = ADAM_B1 * m_ref[...] + (1.0 - ADAM_B1) * gs
        vn = ADAM_B2 * v_ref[...] + (1.0 - ADAM_B2) * jnp.square(gs)
        m_hat = mn / (1.0 - ADAM_B1 ** ADAM_STEP)
        v_hat = vn / (1.0 - ADAM_B2 ** ADAM_STEP)
        go_ref[...] = gs
        d_ref[...] = -ADAM_LR * (m_hat / (jnp.sqrt(v_hat) + ADAM_EPS) + ADAM_WD * w_ref[...])
        mo_ref[...] = mn
        vo_ref[...] = vn

    outs = pl.pallas_call(
        body, name=name, grid=(rows // tr,),
        in_specs=[spec, pl.BlockSpec((n, tr, cols), lambda i: (0, i, 0)), spec, spec],
        out_specs=[spec] * 4, out_shape=[SDS((rows, cols), F32)] * 4, compiler_params=_params(1),
    )(w.reshape(rows, cols), g.reshape(n, rows, cols), m.reshape(rows, cols), v.reshape(rows, cols))
    return tuple(o.reshape(shape) for o in outs)


def _sum_slots(own, recv, name):
    shape, n, cols = own.shape, recv.shape[0], own.shape[-1]
    own, recv = own.reshape(-1, cols), recv.reshape(n, -1, cols)
    rows = own.shape[0]
    tr = _pick(rows, 512, 16)

    def body(own_ref, r_ref, o_ref):
        acc = own_ref[...].astype(F32)
        for k in range(n):
            acc = acc + r_ref[k].astype(F32)
        o_ref[...] = acc

    return pl.pallas_call(
        body, name=name, grid=(rows // tr,),
        in_specs=[pl.BlockSpec((tr, cols), lambda i: (i, 0)), pl.BlockSpec((n, tr, cols), lambda i: (0, i, 0))],
        out_specs=pl.BlockSpec((tr, cols), lambda i: (i, 0)),
        out_shape=SDS((rows, cols), F32), compiler_params=_params(1),
    )(own, recv).reshape(shape)


def _position():
    return lax.axis_index("x"), lax.axis_index("y"), lax.axis_index("c")


def _peer(k, x, y, c):
    return (1 - x if k & 4 else x, 1 - y if k & 2 else y, 1 - c if k & 1 else c)


def _index(pos):
    return 4 * pos[0] + 2 * pos[1] + pos[2]


def _gather_small(v, name):
    rows, lanes = v.shape

    def body(x_ref, out_ref, send_sems, recv_sems, local_sem):
        me = _position()
        mine = pltpu.make_async_copy(x_ref, out_ref.at[_index(me)], local_sem)
        mine.start()

        def copy(k, slot):
            return pltpu.make_async_remote_copy(
                src_ref=x_ref, dst_ref=out_ref.at[slot], send_sem=send_sems.at[k - 1],
                recv_sem=recv_sems.at[k - 1], device_id=_peer(k, *me), device_id_type=MESH)

        sends = [copy(k, _index(me)) for k in range(1, N_DEV)]
        for cp in sends:
            cp.start()
        for k in range(1, N_DEV):
            copy(k, _index(_peer(k, *me))).wait_recv()
        for cp in sends:
            cp.wait_send()
        mine.wait()

    return pl.pallas_call(
        body, name=name, out_shape=SDS((N_DEV, rows, lanes), v.dtype),
        in_specs=[pl.BlockSpec(memory_space=pltpu.VMEM)],
        out_specs=pl.BlockSpec(memory_space=pltpu.VMEM),
        scratch_shapes=[pltpu.SemaphoreType.DMA((N_DEV - 1,)), pltpu.SemaphoreType.DMA((N_DEV - 1,)),
                        pltpu.SemaphoreType.DMA],
        compiler_params=pltpu.CompilerParams(vmem_limit_bytes=VMEM_LIMIT_V7X),
    )(v)


def _gather_big(v, name):
    rows, cols = v.shape

    def body(x_ref, out_ref, send_sems, recv_sems, local_sem):
        x, y, c = _position()
        me, sibling = (x, y, c), (x, y, 1 - c)
        chips = [(1 - x, y), (x, 1 - y), (1 - x, 1 - y)]

        def copy(k, block, to, src=None):
            slot = out_ref.at[_index(block)]
            return pltpu.make_async_remote_copy(
                src_ref=slot if src is None else src, dst_ref=slot, send_sem=send_sems.at[k],
                recv_sem=recv_sems.at[k], device_id=to, device_id_type=MESH)

        mine = pltpu.make_async_copy(x_ref, out_ref.at[_index(me)], local_sem)
        mine.start()
        first = [copy(0, me, sibling, src=x_ref)]
        first += [copy(1 + j, me, (*chip, c), src=x_ref) for j, chip in enumerate(chips)]
        for cp in first:
            cp.start()
        passed = [copy(4 + j, (*chip, c), sibling) for j, chip in enumerate(chips)]
        for j, chip in enumerate(chips):
            copy(1 + j, (*chip, c), me).wait_recv()
            passed[j].start()
        copy(0, sibling, me).wait_recv()
        for j, chip in enumerate(chips):
            copy(4 + j, (*chip, 1 - c), me).wait_recv()
        for cp in first + passed:
            cp.wait_send()
        mine.wait()

    return pl.pallas_call(
        body, name=name, out_shape=SDS((N_DEV, rows, cols), v.dtype),
        in_specs=[pl.BlockSpec(memory_space=pl.ANY)],
        out_specs=pl.BlockSpec(memory_space=pl.ANY),
        scratch_shapes=[pltpu.SemaphoreType.DMA((N_DEV - 1,)), pltpu.SemaphoreType.DMA((N_DEV - 1,)),
                        pltpu.SemaphoreType.DMA],
    )(v)


HBM_SPEC = pl.BlockSpec(memory_space=pltpu.HBM)
SEM_SPEC = pl.BlockSpec(memory_space=pltpu.SEMAPHORE)
SPLIT_EFFECT = pltpu.SideEffectType.DATAFLOW_SIDE_EFFECTING


def _split_start(srcs, gather, name):
    n = len(srcs)
    lands = [jnp.zeros(((N_DEV,) + s.shape) if gather else s.shape, s.dtype) for s in srcs]

    def body(*refs):
        src_refs, land_refs, sems, token = refs[:n], refs[n:2 * n], refs[2 * n:4 * n], refs[-1]
        me = _position()
        for a in range(n):
            for k in range(1, N_DEV):
                peer = _peer(k, *me)
                pltpu.make_async_remote_copy(
                    src_ref=src_refs[a] if gather else src_refs[a].at[_index(peer)],
                    dst_ref=land_refs[a].at[_index(me)], send_sem=sems[2 * a], recv_sem=sems[2 * a + 1],
                    device_id=peer, device_id_type=MESH).start()
        token[...] = jnp.zeros_like(token)

    hbm = lambda arrays: tuple(pltpu.HBM(a.shape, a.dtype) for a in arrays)
    outs = pl.pallas_call(
        body, name=name,
        out_shape=(pltpu.SemaphoreType.DMA(()),) * (2 * n) + hbm(srcs) + hbm(lands) + (SDS((8, 128), F32),),
        in_specs=(HBM_SPEC,) * (2 * n),
        out_specs=(SEM_SPEC,) * (2 * n) + (HBM_SPEC,) * (2 * n) + (pl.BlockSpec(memory_space=pltpu.VMEM),),
        input_output_aliases={a: 2 * n + a for a in range(2 * n)},
        compiler_params=pltpu.CompilerParams(has_side_effects=SPLIT_EFFECT),
    )(*[pltpu.with_memory_space_constraint(a, pltpu.HBM) for a in list(srcs) + lands])
    return outs[:2 * n], outs[2 * n:3 * n], outs[3 * n:4 * n], outs[-1]


def _split_wait(flight, after, name):
    sems, srcs, lands, _ = flight
    n = len(srcs)

    def body(*refs):
        land_refs, sem_refs = refs[n:2 * n], refs[2 * n:4 * n]
        me = _position()
        for a in range(n):
            seven = land_refs[a].at[pl.ds(0, N_DEV - 1)]
            copies = pltpu.make_async_remote_copy(
                src_ref=seven, dst_ref=seven, send_sem=sem_refs[2 * a], recv_sem=sem_refs[2 * a + 1],
                device_id=_peer(1, *me), device_id_type=MESH)
            copies.wait_send()
            copies.wait_recv()

    outs = pl.pallas_call(
        body, name=name,
        out_shape=tuple(pltpu.HBM(a.shape, a.dtype) for a in list(srcs) + list(lands)),
        in_specs=(HBM_SPEC,) * (2 * n) + (SEM_SPEC,) * (2 * n) + (pl.BlockSpec(memory_space=pl.ANY),),
        out_specs=(HBM_SPEC,) * (2 * n), input_output_aliases={a: a for a in range(2 * n)},
        compiler_params=pltpu.CompilerParams(has_side_effects=SPLIT_EFFECT),
    )(*srcs, *lands, *sems, after)
    return outs[:n], outs[n:]


def _pack_rows(arrays, lanes, dtype):
    flat = jnp.concatenate([a.astype(dtype).reshape(-1) for a in arrays])
    pad = (-flat.size) % (16 * lanes)
    if pad:
        flat = jnp.concatenate([flat, jnp.zeros((pad,), dtype)])
    return flat.reshape(-1, lanes)


def _unpack_rows(packed, shapes):
    n = packed.shape[0]
    flat = packed.reshape(n, -1)
    out, off = [], 0
    for shp in shapes:
        size = math.prod(shp)
        out.append(flat[:, off:off + size].reshape((n,) + tuple(shp)))
        off += size
    return out


def _unshard(g8, axis):
    moved = jnp.moveaxis(g8, 0, axis)
    shp = list(moved.shape)
    shp[axis:axis + 2] = [shp[axis] * shp[axis + 1]]
    return moved.reshape(shp)


def _split8(full, axis):
    shp = list(full.shape)
    shp[axis:axis + 1] = [N_DEV, shp[axis] // N_DEV]
    return jnp.moveaxis(full.reshape(shp), axis, 0)


def _my_shard(g, axis, me):
    size = g.shape[axis + 1] // N_DEV
    return lax.dynamic_slice_in_dim(g, me * size, size, axis=axis + 1)


BIG_WEIGHTS = ("ffn_w_up", "ffn_w_down", "attn_w_qkv", "attn_w_o", "ret_w_in", "ret_w_out", "pool_w")
LAYER_WEIGHTS = (
    (("ffn_w_up", 0, "cols"), ("ffn_w_down", 0, "rows"), ("pool_w", 0, "pool")),
    (("ffn_w_up", 1, "cols"), ("ffn_w_down", 1, "rows"), ("attn_w_qkv", 0, "cols"), ("attn_w_o", 0, "rows")),
    (("ffn_w_up", 2, "cols"), ("ffn_w_down", 2, "rows"), ("ret_w_in", 0, "cols"), ("ret_w_out", 0, "rows")),
    (("ffn_w_up", 3, "cols"), ("ffn_w_down", 3, "rows"), ("pool_w", 1, "pool")),
)


GRAD_GROUPS = {"3": LAYER_WEIGHTS[3], "2": LAYER_WEIGHTS[2], "1": LAYER_WEIGHTS[1],
               "0ffn": LAYER_WEIGHTS[0][:2], "0mix": LAYER_WEIGHTS[0][2:]}


def _shard_to_send(w, kind):
    w = w.astype(BF16)
    return w.T if kind == "cols" else w


def _full_from_land(land, kind):
    return _unshard(land, 1) if kind == "pool" else land.reshape(-1, land.shape[-1])


def _grad_to_send(g, kind):
    return _split8(g, 1).astype(BF16) if kind == "pool" else g.astype(BF16).reshape(N_DEV, -1, g.shape[-1])


def _shard_grad(gsum, kind):
    return gsum.T if kind == "cols" else gsum
SMALL_SHARDED = (("norm_w", 2), ("pool_b", 1), ("pool_scale", 1), ("ret_gn_w", 1), ("ffn_conv_w", 2))
REPLICATED = ("ada_b", "attn_q_gain", "attn_k_gain", "ret_decay_logit", "ffn_conv_b")
WEIGHT_ORDER = ("c_ctx", "ada_w", "ada_b", "norm_w", "pool_w", "pool_b", "pool_scale", "attn_w_qkv",
                "attn_q_gain", "attn_k_gain", "attn_w_o", "ret_w_in", "ret_decay_logit", "ret_gn_w",
                "ret_w_out", "ffn_w_up", "ffn_conv_w", "ffn_conv_b", "ffn_w_down")


def _local_step(x0, target, mods, P, get_weights, put_grads, s_len, l_len):
    n_lat = s_len // ROW_TILE
    nw = P["norm_w"]
    lgt = jnp.broadcast_to(P["ret_decay_logit"][0][:, :, None, None], (2, RET_HEADS, 1, 128))
    cos, sin = _rope_tables(s_len, l_len)
    h_dtype = [F32 if i % 3 == 0 else BF16 for i in range(DEPTH)]
    saved = []
    mods = list(mods)
    X = x0
    h = _res_norm(X, None, None, 0, nw[0, 0], mods[0], 0, h_dtype[0], n_lat, "norm_first")
    for i in range(DEPTH):
        kind, j, mod = i % 3, i // 3, mods[i]
        W = get_weights(i, X)
        sv = {"X": X, "h": h, "W": W}
        if kind == 0:
            y = _pool_fwd(h, W["pool_w"], P["pool_b"][j:j + 1], P["pool_scale"][j:j + 1],
                          n_lat, s_len, l_len, f"pool_fwd{i}")
        elif kind == 1:
            qkv = _mm(h, W["attn_w_qkv"], "nt", F32, f"qkv{i}")
            q, k, v = _qk_prep_fwd(qkv, P["attn_q_gain"][j:j + 1], P["attn_k_gain"][j:j + 1], cos, sin)
            o, lse = _flash_fwd(q, k, v, s_len, l_len)
            y = _mm(o, W["attn_w_o"], "nn", F32, f"attn_out{i}")
            sv.update(qkv=qkv, q=q, k=k, v=v, o=o, lse=lse)
        else:
            proj = _mm(h, W["ret_w_in"], "nt", F32, f"ret_in{i}")
            o2, states = _ret_fwd(proj, lgt, s_len, l_len)
            gated = _readout_fwd(o2, proj, P["ret_gn_w"][j:j + 1])
            y = _mm(gated, W["ret_w_out"], "nn", F32, f"ret_out{i}")
            sv.update(proj=proj, o2=o2, states=states, gated=gated)
        X1, h2 = _res_norm(X, y, mod, 0, nw[i, 1], mod, 1, BF16, n_lat, f"res_norm_mid{i}")
        u = _mm(h2, W["ffn_w_up"], "nt", F32, f"ffn_up{i}")
        gact = _conv_gate_fwd(u, P["ffn_conv_w"][i], P["ffn_conv_b"][i:i + 1], n_lat, f"conv_gate_fwd{i}")
        f = _mm(gact, W["ffn_w_down"], "nn", F32, f"ffn_down{i}")
        sv.update(y=y, X1=X1, h2=h2, u=u, gact=gact, f=f)
        saved.append(sv)
        if i + 1 < DEPTH:
            X, h = _res_norm(X1, f, mod, 1, nw[i + 1, 0], mods[i + 1], 0, h_dtype[i + 1], n_lat,
                             f"res_norm_end{i}")
        else:
            X = _res_norm(X1, f, mod, 1, None, None, 0, None, n_lat, "res_last")

    dX, loss = _loss_bwd(X, target, n_lat)
    G = {name: [None] * P[name].shape[0] for name in
         ("pool_b", "pool_scale", "attn_q_gain", "attn_k_gain", "ret_decay_logit", "ret_gn_w", "ffn_conv_w",
          "ffn_conv_b")}
    dnw = [[None, None] for _ in range(DEPTH)]
    dmods = [None] * DEPTH
    for i in reversed(range(DEPTH)):
        kind, j, mod, sv = i % 3, i // 3, mods[i], saved[i]
        W, gl = sv["W"], {}
        df, dg2 = _gate_bwd(dX, sv["f"], mod, 1, BF16, n_lat, f"gate_bwd_ffn{i}")
        dgact = _mm(df, W["ffn_w_down"], "nt", F32, f"ffn_down_dx{i}")
        gl["ffn_w_down"] = _mm(sv["gact"], df, "tn", BF16, f"ffn_down_dw{i}")
        du, dcw, dcb = _conv_gate_bwd(sv["u"], dgact, P["ffn_conv_w"][i], P["ffn_conv_b"][i:i + 1], n_lat,
                                      f"conv_gate_bwd{i}")
        G["ffn_conv_w"][i], G["ffn_conv_b"][i] = dcw, dcb[0]
        dh2 = _mm(du, W["ffn_w_up"], "nn", F32, f"ffn_up_dx{i}")
        gl["ffn_w_up"] = _mm(du, sv["h2"], "tn", BF16, f"ffn_up_dw{i}")
        if i == 0:
            mod = mod + put_grads("0ffn", gl)
        dX1, dnw[i][1], dsh2, dsc2 = _norm_bwd(dX, dh2, sv["X1"], nw[i, 1], mod, 1, n_lat, f"norm_bwd_ffn{i}")
        dy, dg1 = _gate_bwd(dX1, sv["y"], mod, 0, F32 if kind == 0 else BF16, n_lat, f"gate_bwd_mix{i}")
        h = sv["h"]
        if kind == 0:
            dh, dpw, dpb, dps = _pool_bwd(h, dy, W["pool_w"], P["pool_b"][j:j + 1], P["pool_scale"][j:j + 1],
                                          n_lat, s_len, l_len, f"pool_bwd{i}")
            gl["pool_w"], G["pool_b"][j], G["pool_scale"][j] = dpw, dpb[0], dps[0]
        elif kind == 1:
            do = _mm(dy, W["attn_w_o"], "nt", F32, f"attn_out_dx{i}")
            gl["attn_w_o"] = _mm(sv["o"], dy, "tn", BF16, f"attn_out_dw{i}")
            dq, dk, dv = _flash_bwd(sv["q"], sv["k"], sv["v"], sv["o"], sv["lse"], do, s_len, l_len)
            dqkv, dqg, dkg = _qk_prep_bwd(sv["qkv"], dq, dk, dv, P["attn_q_gain"][j:j + 1],
                                          P["attn_k_gain"][j:j + 1], cos, sin)
            G["attn_q_gain"][j], G["attn_k_gain"][j] = dqg[0], dkg[0]
            dh = _mm(dqkv, W["attn_w_qkv"], "nn", F32, f"qkv_dx{i}")
            gl["attn_w_qkv"] = _mm(dqkv, h, "tn", BF16, f"qkv_dw{i}")
        else:
            dgated = _mm(dy, W["ret_w_out"], "nt", F32, f"ret_out_dx{i}")
            gl["ret_w_out"] = _mm(sv["gated"], dy, "tn", BF16, f"ret_out_dw{i}")
            do, dg, dgn = _readout_bwd(sv["o2"], sv["proj"], P["ret_gn_w"][j:j + 1], dgated)
            dq2, dk2, dv2, dlg = _ret_bwd(sv["proj"], lgt, sv["states"], do, s_len, l_len)
            dproj = _ret_dproj(dq2, dk2, dv2, dg)
            G["ret_gn_w"][j], G["ret_decay_logit"][j] = dgn[0], dlg[:, :, 0, 0]
            dh = _mm(dproj, W["ret_w_in"], "nn", F32, f"ret_in_dx{i}")
            gl["ret_w_in"] = _mm(dproj, h, "tn", BF16, f"ret_in_dw{i}")
        dX, dnw[i][0], dsh1, dsc1 = _norm_bwd(dX1, dh, sv["X"], nw[i, 0], mod, 0, n_lat, f"norm_bwd_mix{i}")
        dmods[i] = jnp.concatenate([dsh1, dsc1, dg1, dsh2, dsc2, dg2], axis=1)
        zero = put_grads(str(i) if i > 0 else "0mix", gl)
        if i > 0:
            mods[i - 1] = mods[i - 1] + zero
    grads = {name: jnp.stack(parts) for name, parts in G.items()}
    grads["norm_w"] = jnp.stack([jnp.concatenate(pair, axis=0) for pair in dnw])
    return loss, dX, grads, jnp.stack(dmods)


def kernel(x, c, ctx, c_ctx, ada_w, ada_b, norm_w, pool_w, pool_b, pool_scale, attn_w_qkv, attn_q_gain,
           attn_k_gain, attn_w_o, ret_w_in, ret_decay_logit, ret_gn_w, ret_w_out, ffn_w_up, ffn_conv_w,
           ffn_conv_b, ffn_w_down, loss_target, m_c_ctx, m_ada_w, m_ada_b, m_norm_w, m_pool_w, m_pool_b,
           m_pool_scale, m_attn_w_qkv, m_attn_q_gain, m_attn_k_gain, m_attn_w_o, m_ret_w_in,
           m_ret_decay_logit, m_ret_gn_w, m_ret_w_out, m_ffn_w_up, m_ffn_conv_w, m_ffn_conv_b, m_ffn_w_down,
           v_c_ctx, v_ada_w, v_ada_b, v_norm_w, v_pool_w, v_pool_b, v_pool_scale, v_attn_w_qkv, v_attn_q_gain,
           v_attn_k_gain, v_attn_w_o, v_ret_w_in, v_ret_decay_logit, v_ret_gn_w, v_ret_w_out, v_ffn_w_up,
           v_ffn_conv_w, v_ffn_conv_b, v_ffn_w_down):
    A = dict(locals())
    me = _index(_position())
    s_len, D = x.shape[1], x.shape[2]
    l_len = ctx.shape[1]
    assert s_len % ROW_TILE == 0 and l_len % ROW_TILE == 0 and s_len % GRID_W == 0

    small = [A[n] for n, _ in SMALL_SHARDED]
    got = _gather_small(_pack_rows([c] + small, 128, F32), "gather_c_small")
    parts = _unpack_rows(got, [c.shape] + [a.shape for a in small])
    c_all = parts[0].reshape(N_DEV, D)
    P = {n: _unshard(g8, ax) for (n, ax), g8 in zip(SMALL_SHARDED, parts[1:])}

    c_rows = jnp.concatenate([c_all, c_ctx.reshape(1, D), jnp.zeros((7, D), F32)], axis=0)
    cols = ada_w.shape[2]
    ada_b_shard = lax.dynamic_slice_in_dim(ada_b, me * cols, cols, axis=1).reshape(DEPTH, 1, cols)
    mod_shard = _ada_fwd(c_rows, ada_w, ada_b_shard)
    got = _gather_small(mod_shard.reshape(-1, 128), "gather_mod").reshape(N_DEV, DEPTH, 16, cols)
    mod_lat = lax.dynamic_index_in_dim(got, me, axis=2, keepdims=False)
    mod_ctx = got[:, :, 8, :]
    mods = jnp.stack([jnp.moveaxis(mod_lat, 0, 1).reshape(DEPTH, 6, D),
                      jnp.moveaxis(mod_ctx, 0, 1).reshape(DEPTH, 6, D)], axis=1)

    shards = [[_shard_to_send(A[n][j], kind) for n, j, kind in lw] for lw in LAYER_WEIGHTS]
    pack0 = jnp.concatenate([s.reshape(-1, D) for s in shards[0]], axis=0)
    got0 = _gather_big(pack0, "gather_weights0")
    first, off = {}, 0
    for (n, j, kind), s in zip(LAYER_WEIGHTS[0], shards[0]):
        r = s.size // D
        first[n] = _full_from_land(got0[:, off:off + r].reshape((N_DEV,) + s.shape), kind)
        off += r
    flights, zero = {}, jnp.zeros((), F32)
    for i in range(1, DEPTH):
        flights[i] = _split_start(shards[i], True, f"gather_start{i}")
        zero = zero + flights[i][3][0, 0]
    mods = [mods[i] for i in range(DEPTH)]
    mods[0] = mods[0] + zero
    for n in REPLICATED:
        P[n] = A[n]

    def get_weights(i, x_now):
        if i == 0:
            return first
        owns, lands = _split_wait(flights[i], x_now, f"gather_wait{i}")
        return {n: _full_from_land(lax.dynamic_update_index_in_dim(land, own, me, axis=0), kind)
                for (n, j, kind), own, land in zip(LAYER_WEIGHTS[i], owns, lands)}

    sent = {}

    def put_grads(group, gl):
        sent[group] = _split_start([_grad_to_send(gl[n], kind) for n, j, kind in GRAD_GROUPS[group]], False,
                                   f"exchange_start_{group}")
        return sent[group][3][0, 0]

    x0 = jnp.concatenate([x[0], ctx[0]], axis=0)
    loss8, dx0, G, dmods = _local_step(x0, loss_target[0], mods, P, get_weights, put_grads, s_len, l_len)
    loss = lax.psum(loss8[0, 0], ("x", "y", "c"))
    grad_x = dx0[:s_len].reshape(x.shape)

    small_names = ["dmods"] + list(REPLICATED[1:]) + [n for n, _ in SMALL_SHARDED]
    small_parts = [dmods] + [G[n] for n in small_names[1:]]
    got = _gather_small(_pack_rows(small_parts, 128, F32), "gather_small_grads")
    S8 = dict(zip(small_names, _unpack_rows(got, [a.shape for a in small_parts])))

    dm = S8["dmods"].reshape(N_DEV, DEPTH, 2, 6 * D)
    dm_mine = lax.dynamic_slice_in_dim(dm, me * cols, cols, axis=3)
    g_ada_w, pc = _ada_bwd(c_rows, ada_w, jnp.moveaxis(dm_mine[:, :, 0], 0, 1), jnp.moveaxis(dm_mine[:, :, 1], 0, 1))
    pc8 = _gather_small(pc.reshape(-1, 128), "gather_c_ctx_grad").reshape(N_DEV, 8, D)

    def owner_sums(group, after):
        sends, lands = _split_wait(sent[group], after, f"exchange_wait_{group}")
        out = {}
        for (n, j, kind), send, land in zip(GRAD_GROUPS[group], sends, lands):
            own = lax.dynamic_index_in_dim(send, me, axis=0, keepdims=False)
            out[(n, j)] = _shard_grad(_sum_slots(own, land, f"sum_slots_{n}{j}"), kind)
        return out

    shard_grads = {}
    for group in ("3", "2", "1", "0mix"):
        shard_grads.update(owner_sums(group, pc8))

    g_in = {"c_ctx": pc8[:, 0, :], "ada_w": g_ada_w[None],
            "ada_b": jnp.moveaxis(dm, 2, 1).reshape(2 * N_DEV, DEPTH, 6 * D)}
    for n in REPLICATED[1:]:
        g_in[n] = S8[n]
    for n, ax in SMALL_SHARDED:
        g_in[n] = _my_shard(S8[n], ax, me)

    def stacked(n):
        return jnp.stack([shard_grads[(n, j)] for j in range(A[n].shape[0])])[None]

    late = [n for n, j, kind in GRAD_GROUPS["0ffn"]]
    for n in BIG_WEIGHTS:
        if n not in late:
            g_in[n] = stacked(n)
    res = {n: _adamw(A[n], g_in[n], A["m_" + n], A["v_" + n], "adamw_" + n) for n in WEIGHT_ORDER if n not in late}
    done = sum(res[n][1].reshape(-1)[0] for n in res)
    shard_grads.update(owner_sums("0ffn", done.reshape(1, 1)))
    for n in late:
        res[n] = _adamw(A[n], stacked(n), A["m_" + n], A["v_" + n], "adamw_" + n)
    outs = [loss, grad_x]
    for slot in range(4):
        outs += [res[n][slot] for n in WEIGHT_ORDER]
    return tuple(outs)
```

```python
import functools
import math

import jax
import jax.numpy as jnp
from jax import lax
from jax.experimental import pallas as pl
from jax.experimental.pallas import tpu as pltpu

F32 = jnp.float32
BF16 = jnp.bfloat16
SDS = jax.ShapeDtypeStruct
MESH = pl.DeviceIdType.MESH

N_DEV = 8
EPS = 1e-6
DEPTH = 4
GRID_W = 64
POOL_WINDOWS = (2, 4, 8, 16)
N_HEADS = 8
N_KV = 2
HEAD_DIM = 128
ROPE_THETA = 10000.0
RET_HEADS = 4
RET_DK = 256
RET_DV = 512
RET_CHUNK = 128
ADAM_LR = 0.001
ADAM_B1 = 0.9
ADAM_B2 = 0.999
ADAM_EPS = 1e-08
ADAM_WD = 0.01
ADAM_STEP = 10

ROW_TILE = 256
FLASH_FWD_TILE = 128
HALO = 8
VMEM_LIMIT_V7X = 56 * 1024 * 1024


def _params(n_axes=0):
    sem = ("arbitrary",) * n_axes if n_axes else None
    return pltpu.CompilerParams(dimension_semantics=sem, vmem_limit_bytes=VMEM_LIMIT_V7X)


def _pick(n, cap, mult):
    best = None
    for d in range(mult, min(n, cap) + 1, mult):
        if n % d == 0:
            best = d
    return best if best is not None else n


def _dot(a, b):
    return jnp.dot(a, b, preferred_element_type=F32)


def _dot_nt(a, b):
    return lax.dot_general(a, b, (((1,), (1,)), ((), ())), preferred_element_type=F32)


def _dot_tn(a, b):
    return lax.dot_general(a, b, (((0,), (0,)), ((), ())), preferred_element_type=F32)


def _bf(v):
    return v.astype(BF16)


def _sigmoid(v):
    return 0.5 * jnp.tanh(0.5 * v) + 0.5


MM_VMEM_BUDGET = 40 * 1024 * 1024
MM_STEP_BYTES = 1 << 20
MM_ACC_PASS_BYTES = 8


def _divisors(n, mult, cap):
    return [d for d in range(mult, min(n, cap) + 1, mult) if n % d == 0] or [n]


def _mm_tiles(mode, M, N, K, a_item, b_item, o_item):
    best = None
    for tm in _divisors(M, 128 if mode == "tn" else 16, 2816):
        for tn in _divisors(N, 128, 2048):
            for tk in _divisors(K, 16 if mode == "tn" else 128, 2816):
                ni, nj, nk = M // tm, N // tn, K // tk
                vmem = 2 * (tm * tk * a_item + tk * tn * b_item + tm * tn * o_item) + tm * tn * 4
                if vmem > MM_VMEM_BUDGET:
                    continue
                a_reads = 1 if nk == 1 else nj
                b_reads = 1 if (nk == 1 and nj == 1) else ni
                cost = (M * K * a_item * a_reads + K * N * b_item * b_reads + M * N * o_item
                        + ni * nj * nk * MM_STEP_BYTES + (nk - 1) * M * N * MM_ACC_PASS_BYTES)
                if best is None or cost < best[0]:
                    best = (cost, tm, tn, tk)
    return best[1:]


def _mm(a, b, mode, out_dtype, name):
    if mode == "nn":
        (M, K), (K2, N) = a.shape, b.shape
    elif mode == "nt":
        (M, K), (N, K2) = a.shape, b.shape
    else:
        (K, M), (K2, N) = a.shape, b.shape
    assert K == K2, (a.shape, b.shape, mode)
    tm, tn, tk = _mm_tiles(mode, M, N, K, a.dtype.itemsize, b.dtype.itemsize, jnp.dtype(out_dtype).itemsize)
    nk = K // tk
    if mode == "nn":
        a_spec = pl.BlockSpec((tm, tk), lambda i, j, k: (i, k))
        b_spec = pl.BlockSpec((tk, tn), lambda i, j, k: (k, j))
    elif mode == "nt":
        a_spec = pl.BlockSpec((tm, tk), lambda i, j, k: (i, k))
        b_spec = pl.BlockSpec((tn, tk), lambda i, j, k: (j, k))
    else:
        a_spec = pl.BlockSpec((tk, tm), lambda i, j, k: (k, i))
        b_spec = pl.BlockSpec((tk, tn), lambda i, j, k: (k, j))
    dot = {"nn": _dot, "nt": _dot_nt, "tn": _dot_tn}[mode]

    def body(a_ref, b_ref, o_ref, acc_ref):
        part = dot(_bf(a_ref[...]), _bf(b_ref[...]))
        if nk == 1:
            o_ref[...] = part.astype(out_dtype)
        else:
            k = pl.program_id(2)

            @pl.when(k == 0)
            def _():
                acc_ref[...] = part

            @pl.when(k > 0)
            def _():
                acc_ref[...] += part

            @pl.when(k == nk - 1)
            def _():
                o_ref[...] = acc_ref[...].astype(out_dtype)

    return pl.pallas_call(
        body, name=name, grid=(M // tm, N // tn, nk),
        in_specs=[a_spec, b_spec],
        out_specs=pl.BlockSpec((tm, tn), lambda i, j, k: (i, j)),
        out_shape=SDS((M, N), out_dtype),
        scratch_shapes=[pltpu.VMEM((tm, tn), F32)],
        compiler_params=_params(3),
    )(a, b)


def _seg_spec(n_lat, d):
    return pl.BlockSpec((1, 6, d), lambda i: ((i >= n_lat).astype(jnp.int32), 0, 0))


def _seg_acc_spec(n_lat, d):
    return pl.BlockSpec((1, 1, d), lambda i: ((i >= n_lat).astype(jnp.int32), 0, 0))


def _res_norm(x, y, gmod, gk, nw, nmod, nk, h_dtype, n_lat, name):
    R, D = x.shape
    has_res, has_norm = y is not None, nw is not None
    row = pl.BlockSpec((ROW_TILE, D), lambda i: (i, 0))
    vec = pl.BlockSpec((1, D), lambda i: (0, 0))
    ins, specs, outs, ospecs = [x], [row], [], []
    if has_res:
        ins += [y, gmod]
        specs += [row, _seg_spec(n_lat, D)]
        outs.append(SDS((R, D), F32))
        ospecs.append(row)
    if has_norm:
        ins += [nw.reshape(1, D), nmod]
        specs += [vec, _seg_spec(n_lat, D)]
        outs.append(SDS((R, D), h_dtype))
        ospecs.append(row)

    def body(*refs):
        refs = list(refs)
        z = refs.pop(0)[...]
        if has_res:
            y_ref, g_ref = refs.pop(0), refs.pop(0)
            z = z + g_ref[0, pl.ds(3 * gk + 2, 1), :] * y_ref[...].astype(F32)
        if has_norm:
            nw_ref, m_ref = refs.pop(0), refs.pop(0)
        if has_res:
            refs.pop(0)[...] = z
        if has_norm:
            r = lax.rsqrt(jnp.mean(z * z, axis=-1, keepdims=True) + EPS)
            h = (z * r) * nw_ref[...]
            h = h * (1.0 + m_ref[0, pl.ds(3 * nk + 1, 1), :]) + m_ref[0, pl.ds(3 * nk, 1), :]
            refs.pop(0)[...] = h.astype(h_dtype)

    res = pl.pallas_call(
        body, name=name, grid=(R // ROW_TILE,), in_specs=specs, out_specs=ospecs,
        out_shape=outs, compiler_params=_params(1),
    )(*ins)
    return res if len(res) > 1 else res[0]


def _gate_bwd(dz, y, mod, k, out_dtype, n_lat, name):
    R, D = dz.shape
    row = pl.BlockSpec((ROW_TILE, D), lambda i: (i, 0))

    def body(dz_ref, y_ref, m_ref, dy_ref, dg_ref):
        i = pl.program_id(0)
        dzv = dz_ref[...]
        dy_ref[...] = (m_ref[0, pl.ds(3 * k + 2, 1), :] * dzv).astype(out_dtype)

        @pl.when((i == 0) | (i == n_lat))
        def _():
            dg_ref[...] = jnp.zeros_like(dg_ref)

        dg_ref[0] += jnp.sum(dzv * y_ref[...].astype(F32), axis=0, keepdims=True)

    return pl.pallas_call(
        body, name=name, grid=(R // ROW_TILE,),
        in_specs=[row, row, _seg_spec(n_lat, D)],
        out_specs=[row, _seg_acc_spec(n_lat, D)],
        out_shape=[SDS((R, D), out_dtype), SDS((2, 1, D), F32)],
        compiler_params=_params(1),
    )(dz, y, mod)


def _norm_bwd(dz, dh, x, nw, mod, k, n_lat, name, gated=None):
    R, D = x.shape
    row = pl.BlockSpec((ROW_TILE, D), lambda i: (i, 0))
    vec = pl.BlockSpec((1, D), lambda i: (0, 0))
    ins, specs = [dz, dh, x, nw.reshape(1, D), mod], [row, row, row, vec, _seg_spec(n_lat, D)]
    outs = [SDS((R, D), F32), SDS((1, D), F32), SDS((2, 1, D), F32), SDS((2, 1, D), F32)]
    ospecs = [row, vec, _seg_acc_spec(n_lat, D), _seg_acc_spec(n_lat, D)]
    if gated is not None:
        y, gmod, gk, dy_dtype = gated
        ins += [y, gmod]
        specs += [row, _seg_spec(n_lat, D)]
        outs += [SDS((R, D), dy_dtype), SDS((2, 1, D), F32)]
        ospecs += [row, _seg_acc_spec(n_lat, D)]

    def body(dz_ref, dh_ref, x_ref, nw_ref, m_ref, *rest):
        if gated is not None:
            y_ref, g_ref, dx_ref, dnw_ref, dsh_ref, dsc_ref, dy_ref, dg_ref = rest
        else:
            dx_ref, dnw_ref, dsh_ref, dsc_ref = rest
        i = pl.program_id(0)
        xv = x_ref[...]
        dhv = dh_ref[...].astype(F32)
        nwv = nw_ref[...]
        sc1 = 1.0 + m_ref[0, pl.ds(3 * k + 1, 1), :]
        r = lax.rsqrt(jnp.mean(xv * xv, axis=-1, keepdims=True) + EPS)
        xhat = xv * r
        a = dhv * (nwv * sc1)
        dx = dz_ref[...] + r * (a - xhat * jnp.mean(a * xhat, axis=-1, keepdims=True))
        dx_ref[...] = dx

        @pl.when(i == 0)
        def _():
            dnw_ref[...] = jnp.zeros_like(dnw_ref)

        @pl.when((i == 0) | (i == n_lat))
        def _():
            dsh_ref[...] = jnp.zeros_like(dsh_ref)
            dsc_ref[...] = jnp.zeros_like(dsc_ref)
            if gated is not None:
                dg_ref[...] = jnp.zeros_like(dg_ref)

        dnw_ref[...] += jnp.sum(dhv * xhat, axis=0, keepdims=True) * sc1
        dsh_ref[0] += jnp.sum(dhv, axis=0, keepdims=True)
        dsc_ref[0] += jnp.sum(dhv * xhat, axis=0, keepdims=True) * nwv
        if gated is not None:
            dy_ref[...] = (g_ref[0, pl.ds(3 * gk + 2, 1), :] * dx).astype(dy_dtype)
            dg_ref[0] += jnp.sum(dx * y_ref[...].astype(F32), axis=0, keepdims=True)

    return pl.pallas_call(
        body, name=name, grid=(R // ROW_TILE,), in_specs=specs, out_specs=ospecs, out_shape=outs,
        compiler_params=_params(1),
    )(*ins)


def _loss_bwd(xf, target, n_lat):
    R, D = xf.shape
    row = pl.BlockSpec((ROW_TILE, D), lambda i: (i, 0))
    tgt = pl.BlockSpec((ROW_TILE, D), lambda i: (jnp.minimum(i, n_lat - 1), 0))

    def body(x_ref, t_ref, dx_ref, loss_ref):
        i = pl.program_id(0)
        e = jnp.where(i < n_lat, x_ref[...] - t_ref[...], 0.0)
        dx_ref[...] = e * (1.0 / D)

        @pl.when(i == 0)
        def _():
            loss_ref[...] = jnp.zeros_like(loss_ref)

        loss_ref[...] += 0.5 * jnp.sum(jnp.mean(e * e, axis=-1, keepdims=True))

    return pl.pallas_call(
        body, name="loss_bwd", grid=(R // ROW_TILE,),
        in_specs=[row, tgt],
        out_specs=[row, pl.BlockSpec((8, 128), lambda i: (0, 0))],
        out_shape=[SDS((R, D), F32), SDS((8, 128), F32)],
        compiler_params=_params(1),
    )(xf, target)


def _halo_specs(n_tiles, width, tile=ROW_TILE):
    per = tile // HALO
    prev = pl.BlockSpec((HALO, width), lambda i: (jnp.maximum(i * per - 1, 0), 0))
    nxt = pl.BlockSpec((HALO, width), lambda i: (jnp.minimum((i + 1) * per, n_tiles * per - 1), 0))
    return prev, nxt


def _edge_flags(i, n_lat, n_tiles):
    first = (i == 0) | (i == n_lat)
    last = (i == n_lat - 1) | (i == n_tiles - 1)
    return first, last


def _conv_gate_fwd(u, conv_w, conv_b, n_lat, name):
    R, F2 = u.shape
    F = F2 // 2
    n_tiles = R // ROW_TILE
    T = ROW_TILE
    cw = _pick(F, 256, 128)
    row = pl.BlockSpec((T, F2), lambda i: (i, 0))
    prev, nxt = _halo_specs(n_tiles, F2)

    def body(u_ref, p_ref, n_ref, w_ref, b_ref, o_ref):
        i = pl.program_id(0)
        first, last = _edge_flags(i, n_lat, n_tiles)
        ridx = lax.broadcasted_iota(jnp.int32, (T, 1), 0)

        def conv(c0):
            cols = pl.ds(c0, cw)
            uv = u_ref[:, cols]
            pr = jnp.where(first, 0.0, p_ref[pl.ds(HALO - 1, 1), cols])
            nx = jnp.where(last, 0.0, n_ref[pl.ds(0, 1), cols])
            up = jnp.where(ridx == 0, pr, pltpu.roll(uv, 1, 0))
            un = jnp.where(ridx == T - 1, nx, pltpu.roll(uv, T - 1, 0))
            return (up * w_ref[pl.ds(0, 1), cols] + uv * w_ref[pl.ds(1, 1), cols]
                    + un * w_ref[pl.ds(2, 1), cols] + b_ref[:, cols])

        for c0 in range(0, F, cw):
            ca, cv = conv(c0), conv(F + c0)
            o_ref[:, pl.ds(c0, cw)] = (ca * _sigmoid(ca) * cv).astype(BF16)

    return pl.pallas_call(
        body, name=name, grid=(n_tiles,),
        in_specs=[row, prev, nxt, pl.BlockSpec((3, F2), lambda i: (0, 0)),
                  pl.BlockSpec((1, F2), lambda i: (0, 0))],
        out_specs=pl.BlockSpec((T, F), lambda i: (i, 0)),
        out_shape=SDS((R, F), BF16), compiler_params=_params(1),
    )(u, u, u, conv_w, conv_b)


def _conv_gate_bwd(u, dgact, conv_w, conv_b, n_lat, name):
    R, F2 = u.shape
    F = F2 // 2
    n_tiles = R // ROW_TILE
    T, N = ROW_TILE, ROW_TILE + 2 * HALO
    cw = _pick(F, 256, 128)
    rowu = pl.BlockSpec((T, F2), lambda i: (i, 0))
    rowg = pl.BlockSpec((T, F), lambda i: (i, 0))
    pu, nu = _halo_specs(n_tiles, F2)
    pg, ng = _halo_specs(n_tiles, F)

    def body(u_ref, pu_ref, nu_ref, g_ref, pg_ref, ng_ref, w_ref, b_ref, du_ref, dw_ref, db_ref):
        i = pl.program_id(0)
        first, last = _edge_flags(i, n_lat, n_tiles)

        @pl.when(i == 0)
        def _():
            dw_ref[...] = jnp.zeros_like(dw_ref)
            db_ref[...] = jnp.zeros_like(db_ref)

        def ext(t_ref, p_ref, n_ref, cols):
            pr = jnp.where(first, 0.0, p_ref[:, cols])
            nx = jnp.where(last, 0.0, n_ref[:, cols])
            return jnp.concatenate([pr, t_ref[:, cols], nx], axis=0)

        def conv(c0):
            cols = pl.ds(c0, cw)
            e = ext(u_ref, pu_ref, nu_ref, cols)
            up, un = pltpu.roll(e, 1, 0), pltpu.roll(e, N - 1, 0)
            c = (up * w_ref[pl.ds(0, 1), cols] + e * w_ref[pl.ds(1, 1), cols]
                 + un * w_ref[pl.ds(2, 1), cols] + b_ref[:, cols])
            return c, up, e, un

        def back(c0, dc, up, e, un):
            cols = pl.ds(c0, cw)
            du = (pltpu.roll(dc, N - 1, 0) * w_ref[pl.ds(0, 1), cols] + dc * w_ref[pl.ds(1, 1), cols]
                  + pltpu.roll(dc, 1, 0) * w_ref[pl.ds(2, 1), cols])
            du_ref[:, cols] = du[HALO:HALO + T].astype(BF16)
            dct = dc[HALO:HALO + T]
            dw_ref[pl.ds(0, 1), cols] += jnp.sum(dct * up[HALO:HALO + T], axis=0, keepdims=True)
            dw_ref[pl.ds(1, 1), cols] += jnp.sum(dct * e[HALO:HALO + T], axis=0, keepdims=True)
            dw_ref[pl.ds(2, 1), cols] += jnp.sum(dct * un[HALO:HALO + T], axis=0, keepdims=True)
            db_ref[:, cols] += jnp.sum(dct, axis=0, keepdims=True)

        for c0 in range(0, F, cw):
            dg = ext(g_ref, pg_ref, ng_ref, pl.ds(c0, cw))
            ca, upa, ea, una = conv(c0)
            cv, upv, ev, unv = conv(F + c0)
            s = _sigmoid(ca)
            back(F + c0, dg * (ca * s), upv, ev, unv)
            back(c0, dg * cv * (s * (1.0 + ca * (1.0 - s))), upa, ea, una)

    return pl.pallas_call(
        body, name=name, grid=(n_tiles,),
        in_specs=[rowu, pu, nu, rowg, pg, ng, pl.BlockSpec((3, F2), lambda i: (0, 0)),
                  pl.BlockSpec((1, F2), lambda i: (0, 0))],
        out_specs=[rowu, pl.BlockSpec((3, F2), lambda i: (0, 0)), pl.BlockSpec((1, F2), lambda i: (0, 0))],
        out_shape=[SDS((R, F2), BF16), SDS((3, F2), F32), SDS((1, F2), F32)],
        compiler_params=_params(1),
    )(u, u, u, dgact, dgact, dgact, conv_w, conv_b)


def _pool_counts(i, n_lat, s_len, l_len, n_rows, offset):
    ctx = i >= n_lat
    t0 = jnp.where(ctx, i - n_lat, i) * ROW_TILE + offset
    seg = jnp.where(ctx, l_len, s_len)
    t = t0 + lax.broadcasted_iota(jnp.int32, (n_rows, 1), 0)
    out = []
    for win in POOL_WINDOWS:
        cnt = jnp.minimum(t + win // 2, seg) - jnp.maximum(t - win // 2, 0)
        out.append(jnp.maximum(cnt, 1).astype(F32))
    return out


def _window_sum(e, lo, hi, n):
    acc = None
    for j in range(lo, hi + 1):
        term = e if j == 0 else pltpu.roll(e, (-j) % n, 0)
        acc = term if acc is None else acc + term
    return acc


def _pool_fwd(h, w, b, scale, n_lat, s_len, l_len, name):
    R, D = h.shape
    G = D // 4
    n_tiles = R // ROW_TILE
    T, N = ROW_TILE, ROW_TILE + 2 * HALO
    row = pl.BlockSpec((T, D), lambda i: (i, 0))
    prev, nxt = _halo_specs(n_tiles, D)
    vec = pl.BlockSpec((1, D), lambda i: (0, 0))

    def body(h_ref, p_ref, n_ref, w_ref, b_ref, s_ref, y_ref):
        i = pl.program_id(0)
        first, last = _edge_flags(i, n_lat, n_tiles)
        cnts = _pool_counts(i, n_lat, s_len, l_len, T, 0)
        for g, win in enumerate(POOL_WINDOWS):
            cols = pl.ds(g * G, G)
            pr = jnp.where(first, 0.0, p_ref[:, cols])
            nx = jnp.where(last, 0.0, n_ref[:, cols])
            hv = h_ref[:, cols]
            e = jnp.concatenate([pr, hv, nx], axis=0)
            mean = _window_sum(e, -(win // 2), win // 2 - 1, N)[HALO:HALO + T] / cnts[g]
            yg = _dot(_bf(mean - hv), w_ref[g])
            y_ref[:, cols] = (yg + b_ref[:, cols]) * s_ref[:, cols]

    return pl.pallas_call(
        body, name=name, grid=(n_tiles,),
        in_specs=[row, prev, nxt, pl.BlockSpec((4, G, G), lambda i: (0, 0, 0)), vec, vec],
        out_specs=row, out_shape=SDS((R, D), F32), compiler_params=_params(1),
    )(h, h, h, w, b, scale)


def _pool_bwd(h, dy, w, b, scale, n_lat, s_len, l_len, name):
    R, D = h.shape
    G = D // 4
    n_tiles = R // ROW_TILE
    T, N = ROW_TILE, ROW_TILE + 2 * HALO
    row = pl.BlockSpec((T, D), lambda i: (i, 0))
    prev, nxt = _halo_specs(n_tiles, D)
    vec = pl.BlockSpec((1, D), lambda i: (0, 0))
    wspec = pl.BlockSpec((4, G, G), lambda i: (0, 0, 0))

    def body(h_ref, ph_ref, nh_ref, d_ref, pd_ref, nd_ref, w_ref, b_ref, s_ref,
             dh_ref, dw_ref, db_ref, ds_ref):
        i = pl.program_id(0)
        first, last = _edge_flags(i, n_lat, n_tiles)

        @pl.when(i == 0)
        def _():
            dw_ref[...] = jnp.zeros_like(dw_ref)
            db_ref[...] = jnp.zeros_like(db_ref)
            ds_ref[...] = jnp.zeros_like(ds_ref)

        cnts = _pool_counts(i, n_lat, s_len, l_len, T, 0)
        cnts_ext = _pool_counts(i, n_lat, s_len, l_len, N, -HALO)
        for g, win in enumerate(POOL_WINDOWS):
            cols = pl.ds(g * G, G)

            def ext(t_ref, p_ref, n_ref):
                pr = jnp.where(first, 0.0, p_ref[:, cols])
                nx = jnp.where(last, 0.0, n_ref[:, cols])
                return jnp.concatenate([pr, t_ref[:, cols], nx], axis=0)

            hv = h_ref[:, cols]
            mean = _window_sum(ext(h_ref, ph_ref, nh_ref), -(win // 2), win // 2 - 1, N)[HALO:HALO + T] / cnts[g]
            z = _bf(mean - hv)
            sc = s_ref[:, cols]
            dye = ext(d_ref, pd_ref, nd_ref)
            dt = _bf(dye * sc)
            dz = _dot_nt(dt, w_ref[g])
            dm = dz / cnts_ext[g]
            dh = _window_sum(dm, -(win // 2 - 1), win // 2, N) - dz
            dh_ref[:, cols] = dh[HALO:HALO + T]
            dyt = dye[HALO:HALO + T]
            dw_ref[g] += _dot_tn(z, dt[HALO:HALO + T])
            db_ref[:, cols] += jnp.sum(dyt * sc, axis=0, keepdims=True)
            ds_ref[:, cols] += jnp.sum(dyt * (_dot(z, w_ref[g]) + b_ref[:, cols]), axis=0, keepdims=True)

    return pl.pallas_call(
        body, name=name, grid=(n_tiles,),
        in_specs=[row, prev, nxt, row, prev, nxt, wspec, vec, vec],
        out_specs=[row, wspec, vec, vec],
        out_shape=[SDS((R, D), F32), SDS((4, G, G), F32), SDS((1, D), F32), SDS((1, D), F32)],
        compiler_params=_params(1),
    )(h, h, h, dy, dy, dy, w, b, scale)


def _rope_tables(s_len, l_len):
    t = jnp.arange(s_len)
    row = (t // GRID_W).astype(F32)
    col = (t % GRID_W).astype(F32)
    axis_dim = HEAD_DIM // 2
    inv = ROPE_THETA ** (-jnp.arange(0, axis_dim, 2, dtype=F32) / axis_dim)
    ar, ac = row[:, None] * inv, col[:, None] * inv
    cos = jnp.concatenate([jnp.cos(ar), jnp.cos(ar), jnp.cos(ac), jnp.cos(ac)], axis=-1)
    sin = jnp.concatenate([-jnp.sin(ar), jnp.sin(ar), -jnp.sin(ac), jnp.sin(ac)], axis=-1)
    cos = jnp.concatenate([cos, jnp.ones((l_len, HEAD_DIM), F32)], axis=0)
    sin = jnp.concatenate([sin, jnp.zeros((l_len, HEAD_DIM), F32)], axis=0)
    return cos, sin


def _swap_halves(v):
    lane = lax.broadcasted_iota(jnp.int32, v.shape, 1)
    return jnp.where((lane % 64) < 32, pltpu.roll(v, 96, 1), pltpu.roll(v, 32, 1))


def _qk_prep_fwd(qkv, q_gain, k_gain, cos, sin):
    R = qkv.shape[0]
    NQ, NK = N_HEADS * HEAD_DIM, N_KV * HEAD_DIM
    T = ROW_TILE
    vec = pl.BlockSpec((1, HEAD_DIM), lambda i: (0, 0))
    tab = pl.BlockSpec((T, HEAD_DIM), lambda i: (i, 0))

    def body(x_ref, qg_ref, kg_ref, c_ref, s_ref, q_ref, k_ref, v_ref):
        cosv, sinv = c_ref[...], s_ref[...]

        def prep(c0, gain):
            xh = x_ref[:, pl.ds(c0, HEAD_DIM)]
            xn = xh * lax.rsqrt(jnp.mean(xh * xh, axis=-1, keepdims=True) + EPS) * gain
            return _bf(xn * cosv + _swap_halves(xn) * sinv)

        for hd in range(N_HEADS):
            q_ref[:, pl.ds(hd * HEAD_DIM, HEAD_DIM)] = prep(hd * HEAD_DIM, qg_ref[...])
        for hd in range(N_KV):
            k_ref[:, pl.ds(hd * HEAD_DIM, HEAD_DIM)] = prep(NQ + hd * HEAD_DIM, kg_ref[...])
            v_ref[:, pl.ds(2 * hd * HEAD_DIM, HEAD_DIM)] = _bf(x_ref[:, pl.ds(NQ + NK + hd * HEAD_DIM, HEAD_DIM)])
            v_ref[:, pl.ds((2 * hd + 1) * HEAD_DIM, HEAD_DIM)] = jnp.ones((T, HEAD_DIM), BF16)

    return pl.pallas_call(
        body, name="qk_prep_fwd", grid=(R // T,),
        in_specs=[pl.BlockSpec((T, NQ + 2 * NK), lambda i: (i, 0)), vec, vec, tab, tab],
        out_specs=[pl.BlockSpec((T, NQ), lambda i: (i, 0)), pl.BlockSpec((T, NK), lambda i: (i, 0)),
                   pl.BlockSpec((T, 2 * NK), lambda i: (i, 0))],
        out_shape=[SDS((R, NQ), BF16), SDS((R, NK), BF16), SDS((R, 2 * NK), BF16)],
        compiler_params=_params(1),
    )(qkv, q_gain, k_gain, cos, sin)


def _qk_prep_bwd(qkv, dq, dk, dv, q_gain, k_gain, cos, sin):
    R = qkv.shape[0]
    NQ, NK = N_HEADS * HEAD_DIM, N_KV * HEAD_DIM
    T = ROW_TILE
    vec = pl.BlockSpec((1, HEAD_DIM), lambda i: (0, 0))
    tab = pl.BlockSpec((T, HEAD_DIM), lambda i: (i, 0))

    def body(x_ref, dq_ref, dk_ref, dv_ref, qg_ref, kg_ref, c_ref, s_ref, o_ref, dqg_ref, dkg_ref):
        i = pl.program_id(0)
        cosv, sinv = c_ref[...], s_ref[...]

        @pl.when(i == 0)
        def _():
            dqg_ref[...] = jnp.zeros_like(dqg_ref)
            dkg_ref[...] = jnp.zeros_like(dkg_ref)

        def back(c0, dout, gain, dg_ref):
            xh = x_ref[:, pl.ds(c0, HEAD_DIM)]
            r = lax.rsqrt(jnp.mean(xh * xh, axis=-1, keepdims=True) + EPS)
            xhat = xh * r
            dxn = dout * cosv + _swap_halves(dout * sinv)
            dg_ref[...] += jnp.sum(dxn * xhat, axis=0, keepdims=True)
            a = dxn * gain
            o_ref[:, pl.ds(c0, HEAD_DIM)] = _bf(r * (a - xhat * jnp.mean(a * xhat, axis=-1, keepdims=True)))

        for hd in range(N_HEADS):
            back(hd * HEAD_DIM, dq_ref[:, pl.ds(hd * HEAD_DIM, HEAD_DIM)], qg_ref[...], dqg_ref)
        for hd in range(N_KV):
            back(NQ + hd * HEAD_DIM, dk_ref[:, pl.ds(hd * HEAD_DIM, HEAD_DIM)], kg_ref[...], dkg_ref)
        o_ref[:, pl.ds(NQ + NK, NK)] = _bf(dv_ref[...])

    return pl.pallas_call(
        body, name="qk_prep_bwd", grid=(R // T,),
        in_specs=[pl.BlockSpec((T, NQ + 2 * NK), lambda i: (i, 0)), pl.BlockSpec((T, NQ), lambda i: (i, 0)),
                  pl.BlockSpec((T, NK), lambda i: (i, 0)), pl.BlockSpec((T, NK), lambda i: (i, 0)),
                  vec, vec, tab, tab],
        out_specs=[pl.BlockSpec((T, NQ + 2 * NK), lambda i: (i, 0)), vec, vec],
        out_shape=[SDS((R, NQ + 2 * NK), BF16), SDS((1, HEAD_DIM), F32), SDS((1, HEAD_DIM), F32)],
        compiler_params=_params(1),
    )(qkv, dq, dk, dv, q_gain, k_gain, cos, sin)


def _flash_fwd(q, k, v, s_len, l_len):
    R = q.shape[0]
    T = FLASH_FWD_TILE
    n_lat = s_len // T
    ck = _pick(s_len, 512, 128)
    scale = HEAD_DIM ** -0.5
    group = N_HEADS // N_KV
    GW = group * HEAD_DIM
    M = group * T
    chunks = s_len // ck
    to_log2 = scale * math.log2(math.e)

    def body(q_ref, k_ref, v_ref, o_ref, lse_ref, s_s, sc_s, ml_s, mb_s, acc_s):
        i = pl.program_id(1)
        qv = jnp.concatenate([q_ref[:, pl.ds(hh * HEAD_DIM, HEAD_DIM)] for hh in range(group)], axis=0)

        ml_s[...] = jnp.full_like(ml_s, -jnp.inf)

        def lane_max(s, n):
            m = ml_s[...]
            for t in range(n // HEAD_DIM):
                m = jnp.maximum(m, s[:, t * HEAD_DIM:(t + 1) * HEAD_DIM])
            ml_s[...] = m

        @pl.when(i < n_lat)
        def _():
            def loop(c, carry):
                s = _dot_nt(qv, k_ref[pl.ds(pl.multiple_of(c * ck, ck), ck), :])
                s_s[c] = s
                lane_max(s, ck)
                return carry
            lax.fori_loop(0, chunks, loop, 0, unroll=4 if chunks % 4 == 0 else 1)

        sc = _dot_nt(qv, k_ref[pl.ds(s_len, l_len), :])
        sc_s[...] = sc
        lane_max(sc, l_len)
        m_row = jnp.max(ml_s[...], axis=-1, keepdims=True) * to_log2
        mb_s[...] = jnp.broadcast_to(m_row, (M, ck))

        acc_s[...] = jnp.zeros_like(acc_s)

        @pl.when(i < n_lat)
        def _():
            def loop(c, carry):
                p = jnp.exp2(s_s[c] * to_log2 - mb_s[...])
                acc_s[...] += _dot(_bf(p), v_ref[pl.ds(pl.multiple_of(c * ck, ck), ck), :])
                return carry
            lax.fori_loop(0, chunks, loop, 0, unroll=4 if chunks % 4 == 0 else 1)

        p = jnp.exp2(sc_s[...] * to_log2 - mb_s[:, pl.ds(0, l_len)])
        acc_s[...] += _dot(_bf(p), v_ref[pl.ds(s_len, l_len), :])
        l_rep = acc_s[:, pl.ds(HEAD_DIM, HEAD_DIM)]
        o = acc_s[:, pl.ds(0, HEAD_DIM)] / l_rep
        for hh in range(group):
            o_ref[:, pl.ds(hh * HEAD_DIM, HEAD_DIM)] = o[hh * T:(hh + 1) * T]
        lse = (mb_s[:, pl.ds(0, HEAD_DIM)] + jnp.log2(l_rep)) * math.log(2.0)
        lse_ref[...] = jnp.max(lse, axis=-1, keepdims=True).reshape(group, T, 1)

    return pl.pallas_call(
        body, name="flash_fwd", grid=(N_KV, R // T),
        in_specs=[pl.BlockSpec((T, GW), lambda g, i: (i, g)),
                  pl.BlockSpec((R, HEAD_DIM), lambda g, i: (0, g)),
                  pl.BlockSpec((R, 2 * HEAD_DIM), lambda g, i: (0, g))],
        out_specs=[pl.BlockSpec((T, GW), lambda g, i: (i, g)),
                   pl.BlockSpec((group, T, 1), lambda g, i: (g, i, 0))],
        out_shape=[SDS((R, N_HEADS * HEAD_DIM), F32), SDS((N_HEADS, R, 1), F32)],
        scratch_shapes=[pltpu.VMEM((chunks, M, ck), F32), pltpu.VMEM((M, l_len), F32), pltpu.VMEM((M, HEAD_DIM), F32),
                        pltpu.VMEM((M, ck), F32), pltpu.VMEM((M, 2 * HEAD_DIM), F32)],
        compiler_params=_params(2),
    )(q, k, v)


def _flash_bwd(q, k, v, o, lse, do, s_len, l_len):
    R = q.shape[0]
    T = ROW_TILE
    n_lat = s_len // T
    ck = _pick(s_len, 512, 128)
    scale = HEAD_DIM ** -0.5
    group = N_HEADS // N_KV
    GW = group * HEAD_DIM
    qspec = pl.BlockSpec((T, GW), lambda g, i: (i, g))
    kspec = pl.BlockSpec((R, HEAD_DIM), lambda g, i: (0, g))

    M = group * T
    log2e = math.log2(math.e)

    def body(q_ref, do_ref, o_ref, lse_ref, k_ref, v_ref, dq_ref, dk_ref, dv_ref, dq_s, lse_s, delta_s):
        i = pl.program_id(1)

        @pl.when(i == 0)
        def _():
            dk_ref[...] = jnp.zeros_like(dk_ref)
            dv_ref[...] = jnp.zeros_like(dv_ref)

        def stacked(ref):
            return jnp.concatenate([ref[:, pl.ds(hh * HEAD_DIM, HEAD_DIM)] for hh in range(group)], axis=0)

        qv = stacked(q_ref)
        dov = stacked(do_ref)
        dob = _bf(dov)
        delta_s[...] = jnp.broadcast_to(jnp.sum(dov * stacked(o_ref), axis=-1, keepdims=True), (M, ck))
        lse_s[...] = jnp.broadcast_to(lse_ref[...].reshape(M, 1) * log2e, (M, ck))
        dq_s[...] = jnp.zeros_like(dq_s)

        def step(rows, n):
            kv, vv = k_ref[rows, :], v_ref[rows, :]
            p = jnp.exp2(_dot_nt(qv, kv) * (scale * log2e) - lse_s[:, pl.ds(0, n)])
            dv_ref[rows, :] += _dot_tn(_bf(p), dob)
            ds = _bf(p * (_dot_nt(dob, vv) - delta_s[:, pl.ds(0, n)]) * scale)
            dq_s[...] += _dot(ds, kv)
            dk_ref[rows, :] += _dot_tn(ds, qv)

        @pl.when(i < n_lat)
        def _():
            def loop(c, carry):
                step(pl.ds(pl.multiple_of(c * ck, ck), ck), ck)
                return carry
            lax.fori_loop(0, s_len // ck, loop, 0)

        step(pl.ds(s_len, l_len), l_len)
        for hh in range(group):
            dq_ref[:, pl.ds(hh * HEAD_DIM, HEAD_DIM)] = dq_s[pl.ds(hh * T, T), :]

    return pl.pallas_call(
        body, name="flash_bwd", grid=(N_KV, R // T),
        in_specs=[qspec, qspec, qspec, pl.BlockSpec((group, T, 1), lambda g, i: (g, i, 0)), kspec,
                  pl.BlockSpec((R, HEAD_DIM), lambda g, i: (0, 2 * g))],
        out_specs=[qspec, kspec, kspec],
        out_shape=[SDS((R, N_HEADS * HEAD_DIM), F32), SDS((R, N_KV * HEAD_DIM), F32),
                   SDS((R, N_KV * HEAD_DIM), F32)],
        scratch_shapes=[pltpu.VMEM((M, HEAD_DIM), F32), pltpu.VMEM((M, ck), F32), pltpu.VMEM((M, ck), F32)],
        compiler_params=_params(2),
    )(q, do, o, lse, k, v)


K_SCALE = RET_DK ** -0.5


def _log_sigmoid(v):
    return -(jnp.maximum(-v, 0.0) + jnp.log(1.0 + jnp.exp(-jnp.abs(v))))


def _ret_decays(d, lg):
    C = RET_CHUNK
    ic = lax.broadcasted_iota(jnp.int32, (C, 1), 0)
    ir = lax.broadcasted_iota(jnp.int32, (1, C), 1)
    li = jnp.where(d == 0, ic, C - 1 - ic).astype(F32)
    lj = jnp.where(d == 0, ir, C - 1 - ir).astype(F32)
    diff = li - lj
    mask = jnp.where(diff >= 0, jnp.exp(jnp.maximum(diff, 0.0) * lg), 0.0)
    qd = jnp.exp((li + 1.0) * lg)
    kd = jnp.exp((C - 1.0 - li) * lg)
    cd = jnp.exp(C * lg)
    return li, diff, mask, qd, kd, cd


def _ctx_weights(d, t, lg, l_len):
    C = RET_CHUNK
    j = (t * C + lax.broadcasted_iota(jnp.int32, (C, 1), 0)).astype(F32)
    e = jnp.where(d == 0, (l_len - 1.0) - j, j)
    return e, jnp.exp(e * lg)


def _ret_specs(n_lat_c, n_ctx_c, ctx_first):
    def blk(d, t):
        if ctx_first:
            n = jnp.maximum(t - n_ctx_c, 0)
            lat = jnp.where(d == 0, n, n_lat_c - 1 - n)
            return jnp.where(t < n_ctx_c, n_lat_c + t, lat)
        n = jnp.minimum(t, n_lat_c - 1)
        lat = jnp.where(d == 0, n_lat_c - 1 - n, n)
        return jnp.where(t >= n_lat_c, t, lat)
    return blk


def _ret_fwd(proj, lgt, s_len, l_len):
    R = proj.shape[0]
    C, H, DK, DV = RET_CHUNK, RET_HEADS, RET_DK, RET_DV
    nl, nc = s_len // C, l_len // C
    blk = _ret_specs(nl, nc, True)

    def body(q_ref, k_ref, v_ref, lg_ref, o_ref, st_ref, r_s):
        d, t = pl.program_id(0), pl.program_id(1)

        @pl.when(t == 0)
        def _():
            r_s[...] = jnp.zeros_like(r_s)

        def log_gamma(hh):
            return jnp.max(_log_sigmoid(lg_ref[0, hh]), axis=-1, keepdims=True)

        @pl.when(t < nc)
        def _():
            for hh in range(H):
                qc, vc = pl.ds(hh * DK, DK), pl.ds(hh * DV, DV)
                _, w = _ctx_weights(d, t, log_gamma(hh), l_len)
                r_s[hh] += _dot_tn(_bf(k_ref[:, qc] * K_SCALE * w), _bf(v_ref[:, vc]))
                o_ref[0, :, vc] = jnp.zeros((C, DV), F32)

        @pl.when(t >= nc)
        def _():
            for hh in range(H):
                qc, vc = pl.ds(hh * DK, DK), pl.ds(hh * DV, DV)
                _, _, mask, qd, kd, cd = _ret_decays(d, log_gamma(hh))
                qb, kv, vb = _bf(q_ref[:, qc]), k_ref[:, qc] * K_SCALE, _bf(v_ref[:, vc])
                r = r_s[hh]
                st_ref[0, hh, 0] = r
                att = _dot_nt(qb, _bf(kv)) * mask
                o_ref[0, :, vc] = _dot(_bf(att), vb) + _dot(qb, _bf(r)) * qd
                r_s[hh] = r * cd + _dot_tn(_bf(kv * kd), vb)

    return pl.pallas_call(
        body, name="ret_fwd", grid=(2, nc + nl),
        in_specs=[pl.BlockSpec((C, H * DK), lambda d, t: (blk(d, t), 0)),
                  pl.BlockSpec((C, H * DK), lambda d, t: (blk(d, t), 1)),
                  pl.BlockSpec((C, H * DV), lambda d, t: (blk(d, t), 1)),
                  pl.BlockSpec((1, H, 1, 128), lambda d, t: (d, 0, 0, 0))],
        out_specs=[pl.BlockSpec((1, C, H * DV), lambda d, t: (d, blk(d, t), 0)),
                   pl.BlockSpec((1, H, 1, DK, DV), lambda d, t: (d, 0, jnp.maximum(t - nc, 0), 0, 0))],
        out_shape=[SDS((2, R, H * DV), F32), SDS((2, H, nl, DK, DV), F32)],
        scratch_shapes=[pltpu.VMEM((H, DK, DV), F32)],
        compiler_params=_params(2),
    )(proj, proj, proj, lgt)


def _ret_bwd(proj, lgt, states, do, s_len, l_len):
    R = proj.shape[0]
    C, H, DK, DV = RET_CHUNK, RET_HEADS, RET_DK, RET_DV
    nl, nc = s_len // C, l_len // C
    blk = _ret_specs(nl, nc, False)
    last = nl + nc - 1

    def body(q_ref, k_ref, v_ref, lg_ref, st_ref, do_ref, dq_ref, dk_ref, dv_ref, dlg_ref, dr_s, dl_s):
        d, t = pl.program_id(0), pl.program_id(1)

        def log_gamma(hh):
            return jnp.max(_log_sigmoid(lg_ref[0, hh]), axis=-1, keepdims=True)

        @pl.when(t == 0)
        def _():
            dr_s[...] = jnp.zeros_like(dr_s)
            dl_s[...] = jnp.zeros_like(dl_s)

        @pl.when(t < nl)
        def _():
            for hh in range(H):
                qc, vc = pl.ds(hh * DK, DK), pl.ds(hh * DV, DV)
                li, diff, mask, qd, kd, cd = _ret_decays(d, log_gamma(hh))
                qv, kv, vv, dov = q_ref[:, qc], k_ref[:, qc] * K_SCALE, v_ref[:, vc], do_ref[:, vc]
                qb, kb, vb, dob = _bf(qv), _bf(kv), _bf(vv), _bf(dov)
                r, drn = st_ref[0, hh, 0], dr_s[hh]
                rb, drb = _bf(r), _bf(drn)
                p = _dot_nt(qb, kb)
                dp = _dot_nt(dob, vb) * mask
                dpb = _bf(dp)
                doq = _bf(dov * qd)
                dq_inter = _dot_nt(doq, rb)
                dk_state = kd * _dot_nt(vb, drb)
                dq_ref[0, :, qc] = _dot(dpb, kb) + dq_inter
                dk_ref[0, :, qc] = (_dot_tn(dpb, qb) + dk_state) * K_SCALE
                dv_ref[0, :, vc] = _dot_tn(_bf(p * mask), dob) + _dot(_bf(kv * kd), drb)
                dr_s[hh] = cd * drn + _dot_tn(qb, doq)
                dl_s[hh] += (jnp.sum(dp * p * diff) + jnp.sum((li + 1.0) * qv * dq_inter)
                             + jnp.sum((C - 1.0 - li) * kv * dk_state) + C * jnp.sum(cd * r * drn))

        @pl.when(t >= nl)
        def _():
            for hh in range(H):
                qc, vc = pl.ds(hh * DK, DK), pl.ds(hh * DV, DV)
                e, w = _ctx_weights(d, t - nl, log_gamma(hh), l_len)
                kv, vb, drb = k_ref[:, qc] * K_SCALE, _bf(v_ref[:, vc]), _bf(dr_s[hh])
                dkc = w * _dot_nt(vb, drb)
                dq_ref[0, :, qc] = jnp.zeros((C, DK), F32)
                dk_ref[0, :, qc] = dkc * K_SCALE
                dv_ref[0, :, vc] = _dot(_bf(kv * w), drb)
                dl_s[hh] += jnp.sum(e * kv * dkc)

        @pl.when(t == last)
        def _():
            for hh in range(H):
                dlg_ref[0, hh] = dl_s[hh] * (1.0 / (1.0 + jnp.exp(lg_ref[0, hh])))

    return pl.pallas_call(
        body, name="ret_bwd", grid=(2, nl + nc),
        in_specs=[pl.BlockSpec((C, H * DK), lambda d, t: (blk(d, t), 0)),
                  pl.BlockSpec((C, H * DK), lambda d, t: (blk(d, t), 1)),
                  pl.BlockSpec((C, H * DV), lambda d, t: (blk(d, t), 1)),
                  pl.BlockSpec((1, H, 1, 128), lambda d, t: (d, 0, 0, 0)),
                  pl.BlockSpec((1, H, 1, DK, DV), lambda d, t: (d, 0, jnp.maximum(nl - 1 - t, 0), 0, 0)),
                  pl.BlockSpec((C, H * DV), lambda d, t: (blk(d, t), 0))],
        out_specs=[pl.BlockSpec((1, C, H * DK), lambda d, t: (d, blk(d, t), 0)),
                   pl.BlockSpec((1, C, H * DK), lambda d, t: (d, blk(d, t), 0)),
                   pl.BlockSpec((1, C, H * DV), lambda d, t: (d, blk(d, t), 0)),
                   pl.BlockSpec((1, H, 1, 128), lambda d, t: (d, 0, 0, 0))],
        out_shape=[SDS((2, R, H * DK), F32), SDS((2, R, H * DK), F32), SDS((2, R, H * DV), F32),
                   SDS((2, H, 1, 128), F32)],
        scratch_shapes=[pltpu.VMEM((H, DK, DV), F32), pltpu.VMEM((H, 1, 128), F32)],
        compiler_params=_params(2),
    )(proj, proj, proj, lgt, states, do)


def _readout_fwd(o2, proj, gn_w):
    R = proj.shape[0]
    H, DV = RET_HEADS, RET_DV
    W = H * DV
    T = ROW_TILE

    def body(o_ref, g_ref, w_ref, out_ref):
        for hh in range(H):
            cols = pl.ds(hh * DV, DV)
            y = o_ref[0, :, cols] + o_ref[1, :, cols]
            yc = y - jnp.mean(y, axis=-1, keepdims=True)
            yn = yc * lax.rsqrt(jnp.mean(yc * yc, axis=-1, keepdims=True) + EPS) * w_ref[:, cols]
            g = g_ref[:, cols]
            out_ref[:, cols] = _bf(g * _sigmoid(g) * yn)

    return pl.pallas_call(
        body, name="readout_fwd", grid=(R // T,),
        in_specs=[pl.BlockSpec((2, T, W), lambda i: (0, i, 0)), pl.BlockSpec((T, W), lambda i: (i, 2)),
                  pl.BlockSpec((1, W), lambda i: (0, 0))],
        out_specs=pl.BlockSpec((T, W), lambda i: (i, 0)),
        out_shape=SDS((R, W), BF16), compiler_params=_params(1),
    )(o2, proj, gn_w)


def _readout_bwd(o2, proj, gn_w, dgated):
    R = proj.shape[0]
    H, DV = RET_HEADS, RET_DV
    W = H * DV
    T = ROW_TILE

    def body(o_ref, g_ref, w_ref, d_ref, do_ref, dg_ref, dw_ref):
        i = pl.program_id(0)

        @pl.when(i == 0)
        def _():
            dw_ref[...] = jnp.zeros_like(dw_ref)

        for hh in range(H):
            cols = pl.ds(hh * DV, DV)
            y = o_ref[0, :, cols] + o_ref[1, :, cols]
            yc = y - jnp.mean(y, axis=-1, keepdims=True)
            rstd = lax.rsqrt(jnp.mean(yc * yc, axis=-1, keepdims=True) + EPS)
            yn0 = yc * rstd
            wv = w_ref[:, cols]
            g = g_ref[:, cols]
            s = _sigmoid(g)
            dgt = d_ref[:, cols]
            dyn = dgt * (g * s)
            dg_ref[:, cols] = _bf(dgt * (yn0 * wv) * (s * (1.0 + g * (1.0 - s))))
            dw_ref[:, cols] += jnp.sum(dyn * yn0, axis=0, keepdims=True)
            a = dyn * wv
            do_ref[:, cols] = rstd * (a - jnp.mean(a, axis=-1, keepdims=True)
                                      - yn0 * jnp.mean(a * yn0, axis=-1, keepdims=True))

    return pl.pallas_call(
        body, name="readout_bwd", grid=(R // T,),
        in_specs=[pl.BlockSpec((2, T, W), lambda i: (0, i, 0)), pl.BlockSpec((T, W), lambda i: (i, 2)),
                  pl.BlockSpec((1, W), lambda i: (0, 0)), pl.BlockSpec((T, W), lambda i: (i, 0))],
        out_specs=[pl.BlockSpec((T, W), lambda i: (i, 0)), pl.BlockSpec((T, W), lambda i: (i, 0)),
                   pl.BlockSpec((1, W), lambda i: (0, 0))],
        out_shape=[SDS((R, W), F32), SDS((R, W), BF16), SDS((1, W), F32)],
        compiler_params=_params(1),
    )(o2, proj, gn_w, dgated)


def _ret_dproj(dq2, dk2, dv2, dg):
    R = dg.shape[0]
    NQ, NV = RET_HEADS * RET_DK, RET_HEADS * RET_DV
    T = ROW_TILE

    def body(dq_ref, dk_ref, dv_ref, dg_ref, o_ref):
        o_ref[:, pl.ds(0, NQ)] = _bf(dq_ref[0] + dq_ref[1])
        o_ref[:, pl.ds(NQ, NQ)] = _bf(dk_ref[0] + dk_ref[1])
        o_ref[:, pl.ds(2 * NQ, NV)] = _bf(dv_ref[0] + dv_ref[1])
        o_ref[:, pl.ds(2 * NQ + NV, NV)] = dg_ref[...]

    return pl.pallas_call(
        body, name="ret_dproj", grid=(R // T,),
        in_specs=[pl.BlockSpec((2, T, NQ), lambda i: (0, i, 0)), pl.BlockSpec((2, T, NQ), lambda i: (0, i, 0)),
                  pl.BlockSpec((2, T, NV), lambda i: (0, i, 0)), pl.BlockSpec((T, NV), lambda i: (i, 0))],
        out_specs=pl.BlockSpec((T, 2 * NQ + 2 * NV), lambda i: (i, 0)),
        out_shape=SDS((R, 2 * NQ + 2 * NV), BF16), compiler_params=_params(1),
    )(dq2, dk2, dv2, dg)


def _silu(v):
    return v * _sigmoid(v)


def _ada_fwd(c_rows, ada_w, ada_b_shard):
    depth, D, cols = ada_w.shape

    def body(c_ref, w_ref, b_ref, o_ref):
        o_ref[0] = _dot(_bf(_silu(c_ref[...])), _bf(w_ref[0])) + b_ref[0]

    return pl.pallas_call(
        body, name="ada_fwd", grid=(depth,),
        in_specs=[pl.BlockSpec((16, D), lambda i: (0, 0)), pl.BlockSpec((1, D, cols), lambda i: (i, 0, 0)),
                  pl.BlockSpec((1, 1, cols), lambda i: (i, 0, 0))],
        out_specs=pl.BlockSpec((1, 16, cols), lambda i: (i, 0, 0)),
        out_shape=SDS((depth, 16, cols), F32), compiler_params=_params(1),
    )(c_rows, ada_w, ada_b_shard)


def _ada_bwd(c_rows, ada_w, d_lat, d_ctx):
    depth, D, cols = ada_w.shape

    def body(c_ref, w_ref, dl_ref, dc_ref, dw_ref, pc_ref):
        i = pl.program_id(0)
        cv = c_ref[...]
        a = _silu(cv)
        dcs = jnp.broadcast_to(jnp.sum(dc_ref[0], axis=0, keepdims=True), (8, cols))
        dw_ref[0] = _dot_tn(_bf(a[0:8]), _bf(dl_ref[0])) + _dot_tn(_bf(a[8:16]), _bf(dcs))

        @pl.when(i == 0)
        def _():
            pc_ref[...] = jnp.zeros_like(pc_ref)

        pc_ref[...] += _dot_nt(_bf(dcs), _bf(w_ref[0]))

        @pl.when(i == depth - 1)
        def _():
            cc = c_ref[pl.ds(8, 1), :]
            s = _sigmoid(cc)
            pc_ref[...] = pc_ref[...] * (s * (1.0 + cc * (1.0 - s)))

    return pl.pallas_call(
        body, name="ada_bwd", grid=(depth,),
        in_specs=[pl.BlockSpec((16, D), lambda i: (0, 0)), pl.BlockSpec((1, D, cols), lambda i: (i, 0, 0)),
                  pl.BlockSpec((1, 8, cols), lambda i: (i, 0, 0)), pl.BlockSpec((1, 8, cols), lambda i: (i, 0, 0))],
        out_specs=[pl.BlockSpec((1, D, cols), lambda i: (i, 0, 0)), pl.BlockSpec((8, D), lambda i: (0, 0))],
        out_shape=[SDS((depth, D, cols), F32), SDS((8, D), F32)], compiler_params=_params(1),
    )(c_rows, ada_w, d_lat, d_ctx)


def _adamw(w, g, m, v, name):
    shape = w.shape
    n = g.shape[0]
    cols = shape[-1]
    rows = w.size // cols
    tr = _pick(rows, 512, 8) if rows * cols * 4 > (1 << 20) else rows
    spec = pl.BlockSpec((tr, cols), lambda i: (i, 0))

    def body(w_ref, g_ref, m_ref, v_ref, go_ref, d_ref, mo_ref, vo_ref):
        gs = g_ref[0].astype(F32)
        for k in range(1, n):
            gs = gs + g_ref[k].astype(F32)
        mn = ADAM_B1 * m_ref[...] + (1.0 - ADAM_B1) * gs
        vn = ADAM_B2 * v_ref[...] + (1.0 - ADAM_B2) * jnp.square(gs)
        m_hat = mn / (1.0 - ADAM_B1 ** ADAM_STEP)
        v_hat = vn / (1.0 - ADAM_B2 ** ADAM_STEP)
        go_ref[...] = gs
        d_ref[...] = -ADAM_LR * (m_hat / (jnp.sqrt(v_hat) + ADAM_EPS) + ADAM_WD * w_ref[...])
        mo_ref[...] = mn
        vo_ref[...] = vn

    outs = pl.pallas_call(
        body, name=name, grid=(rows // tr,),
        in_specs=[spec, pl.BlockSpec((n, tr, cols), lambda i: (0, i, 0)), spec, spec],
        out_specs=[spec] * 4, out_shape=[SDS((rows, cols), F32)] * 4, compiler_params=_params(1),
    )(w.reshape(rows, cols), g.reshape(n, rows, cols), m.reshape(rows, cols), v.reshape(rows, cols))
    return tuple(o.reshape(shape) for o in outs)


def _sum_slots(own, recv, name):
    shape, n, cols = own.shape, recv.shape[0], own.shape[-1]
    own, recv = own.reshape(-1, cols), recv.reshape(n, -1, cols)
    rows = own.shape[0]
    tr = _pick(rows, 512, 16)

    def body(own_ref, r_ref, o_ref):
        acc = own_ref[...].astype(F32)
        for k in range(n):
            acc = acc + r_ref[k].astype(F32)
        o_ref[...] = acc

    return pl.pallas_call(
        body, name=name, grid=(rows // tr,),
        in_specs=[pl.BlockSpec((tr, cols), lambda i: (i, 0)), pl.BlockSpec((n, tr, cols), lambda i: (0, i, 0))],
        out_specs=pl.BlockSpec((tr, cols), lambda i: (i, 0)),
        out_shape=SDS((rows, cols), F32), compiler_params=_params(1),
    )(own, recv).reshape(shape)


def _position():
    return lax.axis_index("x"), lax.axis_index("y"), lax.axis_index("c")


def _peer(k, x, y, c):
    return (1 - x if k & 4 else x, 1 - y if k & 2 else y, 1 - c if k & 1 else c)


def _index(pos):
    return 4 * pos[0] + 2 * pos[1] + pos[2]


def _gather_small(v, name):
    rows, lanes = v.shape

    def body(x_ref, out_ref, send_sems, recv_sems, local_sem):
        me = _position()
        mine = pltpu.make_async_copy(x_ref, out_ref.at[_index(me)], local_sem)
        mine.start()

        def copy(k, slot):
            return pltpu.make_async_remote_copy(
                src_ref=x_ref, dst_ref=out_ref.at[slot], send_sem=send_sems.at[k - 1],
                recv_sem=recv_sems.at[k - 1], device_id=_peer(k, *me), device_id_type=MESH)

        sends = [copy(k, _index(me)) for k in range(1, N_DEV)]
        for cp in sends:
            cp.start()
        for k in range(1, N_DEV):
            copy(k, _index(_peer(k, *me))).wait_recv()
        for cp in sends:
            cp.wait_send()
        mine.wait()

    return pl.pallas_call(
        body, name=name, out_shape=SDS((N_DEV, rows, lanes), v.dtype),
        in_specs=[pl.BlockSpec(memory_space=pltpu.VMEM)],
        out_specs=pl.BlockSpec(memory_space=pltpu.VMEM),
        scratch_shapes=[pltpu.SemaphoreType.DMA((N_DEV - 1,)), pltpu.SemaphoreType.DMA((N_DEV - 1,)),
                        pltpu.SemaphoreType.DMA],
        compiler_params=pltpu.CompilerParams(vmem_limit_bytes=VMEM_LIMIT_V7X),
    )(v)


def _gather_big(v, name):
    rows, cols = v.shape

    def body(x_ref, out_ref, send_sems, recv_sems, local_sem):
        x, y, c = _position()
        me, sibling = (x, y, c), (x, y, 1 - c)
        chips = [(1 - x, y), (x, 1 - y), (1 - x, 1 - y)]

        def copy(k, block, to, src=None):
            slot = out_ref.at[_index(block)]
            return pltpu.make_async_remote_copy(
                src_ref=slot if src is None else src, dst_ref=slot, send_sem=send_sems.at[k],
                recv_sem=recv_sems.at[k], device_id=to, device_id_type=MESH)

        mine = pltpu.make_async_copy(x_ref, out_ref.at[_index(me)], local_sem)
        mine.start()
        first = [copy(0, me, sibling, src=x_ref)]
        first += [copy(1 + j, me, (*chip, c), src=x_ref) for j, chip in enumerate(chips)]
        for cp in first:
            cp.start()
        passed = [copy(4 + j, (*chip, c), sibling) for j, chip in enumerate(chips)]
        for j, chip in enumerate(chips):
            copy(1 + j, (*chip, c), me).wait_recv()
            passed[j].start()
        copy(0, sibling, me).wait_recv()
        for j, chip in enumerate(chips):
            copy(4 + j, (*chip, 1 - c), me).wait_recv()
        for cp in first + passed:
            cp.wait_send()
        mine.wait()

    return pl.pallas_call(
        body, name=name, out_shape=SDS((N_DEV, rows, cols), v.dtype),
        in_specs=[pl.BlockSpec(memory_space=pl.ANY)],
        out_specs=pl.BlockSpec(memory_space=pl.ANY),
        scratch_shapes=[pltpu.SemaphoreType.DMA((N_DEV - 1,)), pltpu.SemaphoreType.DMA((N_DEV - 1,)),
                        pltpu.SemaphoreType.DMA],
    )(v)


HBM_SPEC = pl.BlockSpec(memory_space=pltpu.HBM)
SEM_SPEC = pl.BlockSpec(memory_space=pltpu.SEMAPHORE)
SPLIT_EFFECT = pltpu.SideEffectType.DATAFLOW_SIDE_EFFECTING


def _split_start(srcs, gather, name):
    n = len(srcs)
    lands = [jnp.zeros(((N_DEV,) + s.shape) if gather else s.shape, s.dtype) for s in srcs]

    def body(*refs):
        src_refs, land_refs, sems, token = refs[:n], refs[n:2 * n], refs[2 * n:4 * n], refs[-1]
        me = _position()
        for a in range(n):
            for k in range(1, N_DEV):
                peer = _peer(k, *me)
                pltpu.make_async_remote_copy(
                    src_ref=src_refs[a] if gather else src_refs[a].at[_index(peer)],
                    dst_ref=land_refs[a].at[_index(me)], send_sem=sems[2 * a], recv_sem=sems[2 * a + 1],
                    device_id=peer, device_id_type=MESH).start()
        token[...] = jnp.zeros_like(token)

    hbm = lambda arrays: tuple(pltpu.HBM(a.shape, a.dtype) for a in arrays)
    outs = pl.pallas_call(
        body, name=name,
        out_shape=(pltpu.SemaphoreType.DMA(()),) * (2 * n) + hbm(srcs) + hbm(lands) + (SDS((8, 128), F32),),
        in_specs=(HBM_SPEC,) * (2 * n),
        out_specs=(SEM_SPEC,) * (2 * n) + (HBM_SPEC,) * (2 * n) + (pl.BlockSpec(memory_space=pltpu.VMEM),),
        input_output_aliases={a: 2 * n + a for a in range(2 * n)},
        compiler_params=pltpu.CompilerParams(has_side_effects=SPLIT_EFFECT),
    )(*[pltpu.with_memory_space_constraint(a, pltpu.HBM) for a in list(srcs) + lands])
    return outs[:2 * n], outs[2 * n:3 * n], outs[3 * n:4 * n], outs[-1]


def _split_wait(flight, after, name):
    sems, srcs, lands, _ = flight
    n = len(srcs)

    def body(*refs):
        land_refs, sem_refs = refs[n:2 * n], refs[2 * n:4 * n]
        me = _position()
        for a in range(n):
            seven = land_refs[a].at[pl.ds(0, N_DEV - 1)]
            copies = pltpu.make_async_remote_copy(
                src_ref=seven, dst_ref=seven, send_sem=sem_refs[2 * a], recv_sem=sem_refs[2 * a + 1],
                device_id=_peer(1, *me), device_id_type=MESH)
            copies.wait_send()
            copies.wait_recv()

    outs = pl.pallas_call(
        body, name=name,
        out_shape=tuple(pltpu.HBM(a.shape, a.dtype) for a in list(srcs) + list(lands)),
        in_specs=(HBM_SPEC,) * (2 * n) + (SEM_SPEC,) * (2 * n) + (pl.BlockSpec(memory_space=pl.ANY),),
        out_specs=(HBM_SPEC,) * (2 * n), input_output_aliases={a: a for a in range(2 * n)},
        compiler_params=pltpu.CompilerParams(has_side_effects=SPLIT_EFFECT),
    )(*srcs, *lands, *sems, after)
    return outs[:n], outs[n:]


def _pack_rows(arrays, lanes, dtype):
    flat = jnp.concatenate([a.astype(dtype).reshape(-1) for a in arrays])
    pad = (-flat.size) % (16 * lanes)
    if pad:
        flat = jnp.concatenate([flat, jnp.zeros((pad,), dtype)])
    return flat.reshape(-1, lanes)


def _unpack_rows(packed, shapes):
    n = packed.shape[0]
    flat = packed.reshape(n, -1)
    out, off = [], 0
    for shp in shapes:
        size = math.prod(shp)
        out.append(flat[:, off:off + size].reshape((n,) + tuple(shp)))
        off += size
    return out


def _unshard(g8, axis):
    moved = jnp.moveaxis(g8, 0, axis)
    shp = list(moved.shape)
    shp[axis:axis + 2] = [shp[axis] * shp[axis + 1]]
    return moved.reshape(shp)


def _split8(full, axis):
    shp = list(full.shape)
    shp[axis:axis + 1] = [N_DEV, shp[axis] // N_DEV]
    return jnp.moveaxis(full.reshape(shp), axis, 0)


def _my_shard(g, axis, me):
    size = g.shape[axis + 1] // N_DEV
    return lax.dynamic_slice_in_dim(g, me * size, size, axis=axis + 1)


BIG_WEIGHTS = ("ffn_w_up", "ffn_w_down", "attn_w_qkv", "attn_w_o", "ret_w_in", "ret_w_out", "pool_w")
LAYER_WEIGHTS = (
    (("ffn_w_up", 0, "cols"), ("ffn_w_down", 0, "rows"), ("pool_w", 0, "pool")),
    (("ffn_w_up", 1, "cols"), ("ffn_w_down", 1, "rows"), ("attn_w_qkv", 0, "cols"), ("attn_w_o", 0, "rows")),
    (("ffn_w_up", 2, "cols"), ("ffn_w_down", 2, "rows"), ("ret_w_in", 0, "cols"), ("ret_w_out", 0, "rows")),
    (("ffn_w_up", 3, "cols"), ("ffn_w_down", 3, "rows"), ("pool_w", 1, "pool")),
)


GRAD_GROUPS = {"3": LAYER_WEIGHTS[3], "2": LAYER_WEIGHTS[2], "1": LAYER_WEIGHTS[1],
               "0ffn": LAYER_WEIGHTS[0][:2], "0mix": LAYER_WEIGHTS[0][2:]}


def _shard_to_send(w, kind):
    w = w.astype(BF16)
    return w.T if kind == "cols" else w


def _full_from_land(land, kind):
    return _unshard(land, 1) if kind == "pool" else land.reshape(-1, land.shape[-1])


def _grad_to_send(g, kind):
    return _split8(g, 1).astype(BF16) if kind == "pool" else g.astype(BF16).reshape(N_DEV, -1, g.shape[-1])


def _shard_grad(gsum, kind):
    return gsum.T if kind == "cols" else gsum
SMALL_SHARDED = (("norm_w", 2), ("pool_b", 1), ("pool_scale", 1), ("ret_gn_w", 1), ("ffn_conv_w", 2))
REPLICATED = ("ada_b", "attn_q_gain", "attn_k_gain", "ret_decay_logit", "ffn_conv_b")
WEIGHT_ORDER = ("c_ctx", "ada_w", "ada_b", "norm_w", "pool_w", "pool_b", "pool_scale", "attn_w_qkv",
                "attn_q_gain", "attn_k_gain", "attn_w_o", "ret_w_in", "ret_decay_logit", "ret_gn_w",
                "ret_w_out", "ffn_w_up", "ffn_conv_w", "ffn_conv_b", "ffn_w_down")


def _local_step(x0, target, mods, P, get_weights, put_grads, s_len, l_len):
    n_lat = s_len // ROW_TILE
    nw = P["norm_w"]
    lgt = jnp.broadcast_to(P["ret_decay_logit"][0][:, :, None, None], (2, RET_HEADS, 1, 128))
    cos, sin = _rope_tables(s_len, l_len)
    h_dtype = [F32 if i % 3 == 0 else BF16 for i in range(DEPTH)]
    saved = []
    mods = list(mods)
    X = x0
    h = _res_norm(X, None, None, 0, nw[0, 0], mods[0], 0, h_dtype[0], n_lat, "norm_first")
    for i in range(DEPTH):
        kind, j, mod = i % 3, i // 3, mods[i]
        W = get_weights(i, X)
        sv = {"X": X, "h": h, "W": W}
        if kind == 0:
            y = _pool_fwd(h, W["pool_w"], P["pool_b"][j:j + 1], P["pool_scale"][j:j + 1],
                          n_lat, s_len, l_len, f"pool_fwd{i}")
        elif kind == 1:
            qkv = _mm(h, W["attn_w_qkv"], "nt", F32, f"qkv{i}")
            q, k, v = _qk_prep_fwd(qkv, P["attn_q_gain"][j:j + 1], P["attn_k_gain"][j:j + 1], cos, sin)
            o, lse = _flash_fwd(q, k, v, s_len, l_len)
            y = _mm(o, W["attn_w_o"], "nn", F32, f"attn_out{i}")
            sv.update(qkv=qkv, q=q, k=k, v=v, o=o, lse=lse)
        else:
            proj = _mm(h, W["ret_w_in"], "nt", F32, f"ret_in{i}")
            o2, states = _ret_fwd(proj, lgt, s_len, l_len)
            gated = _readout_fwd(o2, proj, P["ret_gn_w"][j:j + 1])
            y = _mm(gated, W["ret_w_out"], "nn", F32, f"ret_out{i}")
            sv.update(proj=proj, o2=o2, states=states, gated=gated)
        X1, h2 = _res_norm(X, y, mod, 0, nw[i, 1], mod, 1, BF16, n_lat, f"res_norm_mid{i}")
        u = _mm(h2, W["ffn_w_up"], "nt", F32, f"ffn_up{i}")
        gact = _conv_gate_fwd(u, P["ffn_conv_w"][i], P["ffn_conv_b"][i:i + 1], n_lat, f"conv_gate_fwd{i}")
        f = _mm(gact, W["ffn_w_down"], "nn", F32, f"ffn_down{i}")
        sv.update(y=y, X1=X1, h2=h2, u=u, gact=gact, f=f)
        saved.append(sv)
        if i + 1 < DEPTH:
            X, h = _res_norm(X1, f, mod, 1, nw[i + 1, 0], mods[i + 1], 0, h_dtype[i + 1], n_lat,
                             f"res_norm_end{i}")
        else:
            X = _res_norm(X1, f, mod, 1, None, None, 0, None, n_lat, "res_last")

    dX, loss = _loss_bwd(X, target, n_lat)
    G = {name: [None] * P[name].shape[0] for name in
         ("pool_b", "pool_scale", "attn_q_gain", "attn_k_gain", "ret_decay_logit", "ret_gn_w", "ffn_conv_w",
          "ffn_conv_b")}
    dnw = [[None, None] for _ in range(DEPTH)]
    dmods = [None] * DEPTH
    for i in reversed(range(DEPTH)):
        kind, j, mod, sv = i % 3, i // 3, mods[i], saved[i]
        W, gl = sv["W"], {}
        if i == DEPTH - 1:
            df, dg2 = _gate_bwd(dX, sv["f"], mod, 1, BF16, n_lat, f"gate_bwd_ffn{i}")
        dgact = _mm(df, W["ffn_w_down"], "nt", F32, f"ffn_down_dx{i}")
        gl["ffn_w_down"] = _mm(sv["gact"], df, "tn", BF16, f"ffn_down_dw{i}")
        du, dcw, dcb = _conv_gate_bwd(sv["u"], dgact, P["ffn_conv_w"][i], P["ffn_conv_b"][i:i + 1], n_lat,
                                      f"conv_gate_bwd{i}")
        G["ffn_conv_w"][i], G["ffn_conv_b"][i] = dcw, dcb[0]
        dh2 = _mm(du, W["ffn_w_up"], "nn", F32, f"ffn_up_dx{i}")
        gl["ffn_w_up"] = _mm(du, sv["h2"], "tn", BF16, f"ffn_up_dw{i}")
        if i == 0:
            mod = mod + put_grads("0ffn", gl)
        dX1, dnw[i][1], dsh2, dsc2, dy, dg1 = _norm_bwd(
            dX, dh2, sv["X1"], nw[i, 1], mod, 1, n_lat, f"norm_bwd_ffn{i}",
            gated=(sv["y"], mod, 0, F32 if kind == 0 else BF16))
        h = sv["h"]
        if kind == 0:
            dh, dpw, dpb, dps = _pool_bwd(h, dy, W["pool_w"], P["pool_b"][j:j + 1], P["pool_scale"][j:j + 1],
                                          n_lat, s_len, l_len, f"pool_bwd{i}")
            gl["pool_w"], G["pool_b"][j], G["pool_scale"][j] = dpw, dpb[0], dps[0]
        elif kind == 1:
            do = _mm(dy, W["attn_w_o"], "nt", F32, f"attn_out_dx{i}")
            gl["attn_w_o"] = _mm(sv["o"], dy, "tn", BF16, f"attn_out_dw{i}")
            dq, dk, dv = _flash_bwd(sv["q"], sv["k"], sv["v"], sv["o"], sv["lse"], do, s_len, l_len)
            dqkv, dqg, dkg = _qk_prep_bwd(sv["qkv"], dq, dk, dv, P["attn_q_gain"][j:j + 1],
                                          P["attn_k_gain"][j:j + 1], cos, sin)
            G["attn_q_gain"][j], G["attn_k_gain"][j] = dqg[0], dkg[0]
            dh = _mm(dqkv, W["attn_w_qkv"], "nn", F32, f"qkv_dx{i}")
            gl["attn_w_qkv"] = _mm(dqkv, h, "tn", BF16, f"qkv_dw{i}")
        else:
            dgated = _mm(dy, W["ret_w_out"], "nt", F32, f"ret_out_dx{i}")
            gl["ret_w_out"] = _mm(sv["gated"], dy, "tn", BF16, f"ret_out_dw{i}")
            do, dg, dgn = _readout_bwd(sv["o2"], sv["proj"], P["ret_gn_w"][j:j + 1], dgated)
            dq2, dk2, dv2, dlg = _ret_bwd(sv["proj"], lgt, sv["states"], do, s_len, l_len)
            dproj = _ret_dproj(dq2, dk2, dv2, dg)
            G["ret_gn_w"][j], G["ret_decay_logit"][j] = dgn[0], dlg[:, :, 0, 0]
            dh = _mm(dproj, W["ret_w_in"], "nn", F32, f"ret_in_dx{i}")
            gl["ret_w_in"] = _mm(dproj, h, "tn", BF16, f"ret_in_dw{i}")
        zero = put_grads(str(i) if i > 0 else "0mix", gl)
        if i > 0:
            mods[i - 1] = mods[i - 1] + zero
            dX, dnw[i][0], dsh1, dsc1, df_below, dg2_below = _norm_bwd(
                dX1, dh, sv["X"], nw[i, 0], mod, 0, n_lat, f"norm_bwd_mix{i}",
                gated=(saved[i - 1]["f"], mods[i - 1], 1, BF16))
        else:
            dX, dnw[i][0], dsh1, dsc1 = _norm_bwd(dX1, dh, sv["X"], nw[i, 0], mod, 0, n_lat, f"norm_bwd_mix{i}")
        dmods[i] = jnp.concatenate([dsh1, dsc1, dg1, dsh2, dsc2, dg2], axis=1)
        if i > 0:
            df, dg2 = df_below, dg2_below
    grads = {name: jnp.stack(parts) for name, parts in G.items()}
    grads["norm_w"] = jnp.stack([jnp.concatenate(pair, axis=0) for pair in dnw])
    return loss, dX, grads, jnp.stack(dmods)


def kernel(x, c, ctx, c_ctx, ada_w, ada_b, norm_w, pool_w, pool_b, pool_scale, attn_w_qkv, attn_q_gain,
           attn_k_gain, attn_w_o, ret_w_in, ret_decay_logit, ret_gn_w, ret_w_out, ffn_w_up, ffn_conv_w,
           ffn_conv_b, ffn_w_down, loss_target, m_c_ctx, m_ada_w, m_ada_b, m_norm_w, m_pool_w, m_pool_b,
           m_pool_scale, m_attn_w_qkv, m_attn_q_gain, m_attn_k_gain, m_attn_w_o, m_ret_w_in,
           m_ret_decay_logit, m_ret_gn_w, m_ret_w_out, m_ffn_w_up, m_ffn_conv_w, m_ffn_conv_b, m_ffn_w_down,
           v_c_ctx, v_ada_w, v_ada_b, v_norm_w, v_pool_w, v_pool_b, v_pool_scale, v_attn_w_qkv, v_attn_q_gain,
           v_attn_k_gain, v_attn_w_o, v_ret_w_in, v_ret_decay_logit, v_ret_gn_w, v_ret_w_out, v_ffn_w_up,
           v_ffn_conv_w, v_ffn_conv_b, v_ffn_w_down):
    A = dict(locals())
    me = _index(_position())
    s_len, D = x.shape[1], x.shape[2]
    l_len = ctx.shape[1]
    assert s_len % ROW_TILE == 0 and l_len % ROW_TILE == 0 and s_len % GRID_W == 0

    small = [A[n] for n, _ in SMALL_SHARDED]
    got = _gather_small(_pack_rows([c] + small, 128, F32), "gather_c_small")
    parts = _unpack_rows(got, [c.shape] + [a.shape for a in small])
    c_all = parts[0].reshape(N_DEV, D)
    P = {n: _unshard(g8, ax) for (n, ax), g8 in zip(SMALL_SHARDED, parts[1:])}

    c_rows = jnp.concatenate([c_all, c_ctx.reshape(1, D), jnp.zeros((7, D), F32)], axis=0)
    cols = ada_w.shape[2]
    ada_b_shard = lax.dynamic_slice_in_dim(ada_b, me * cols, cols, axis=1).reshape(DEPTH, 1, cols)
    mod_shard = _ada_fwd(c_rows, ada_w, ada_b_shard)
    got = _gather_small(mod_shard.reshape(-1, 128), "gather_mod").reshape(N_DEV, DEPTH, 16, cols)
    mod_lat = lax.dynamic_index_in_dim(got, me, axis=2, keepdims=False)
    mod_ctx = got[:, :, 8, :]
    mods = jnp.stack([jnp.moveaxis(mod_lat, 0, 1).reshape(DEPTH, 6, D),
                      jnp.moveaxis(mod_ctx, 0, 1).reshape(DEPTH, 6, D)], axis=1)

    shards = [[_shard_to_send(A[n][j], kind) for n, j, kind in lw] for lw in LAYER_WEIGHTS]
    pack0 = jnp.concatenate([s.reshape(-1, D) for s in shards[0]], axis=0)
    got0 = _gather_big(pack0, "gather_weights0")
    first, off = {}, 0
    for (n, j, kind), s in zip(LAYER_WEIGHTS[0], shards[0]):
        r = s.size // D
        first[n] = _full_from_land(got0[:, off:off + r].reshape((N_DEV,) + s.shape), kind)
        off += r
    flights, zero = {}, jnp.zeros((), F32)
    for i in range(1, DEPTH):
        flights[i] = _split_start(shards[i], True, f"gather_start{i}")
        zero = zero + flights[i][3][0, 0]
    mods = [mods[i] for i in range(DEPTH)]
    mods[0] = mods[0] + zero
    for n in REPLICATED:
        P[n] = A[n]

    def get_weights(i, x_now):
        if i == 0:
            return first
        owns, lands = _split_wait(flights[i], x_now, f"gather_wait{i}")
        return {n: _full_from_land(lax.dynamic_update_index_in_dim(land, own, me, axis=0), kind)
                for (n, j, kind), own, land in zip(LAYER_WEIGHTS[i], owns, lands)}

    sent = {}

    def put_grads(group, gl):
        sent[group] = _split_start([_grad_to_send(gl[n], kind) for n, j, kind in GRAD_GROUPS[group]], False,
                                   f"exchange_start_{group}")
        return sent[group][3][0, 0]

    x0 = jnp.concatenate([x[0], ctx[0]], axis=0)
    loss8, dx0, G, dmods = _local_step(x0, loss_target[0], mods, P, get_weights, put_grads, s_len, l_len)
    loss = lax.psum(loss8[0, 0], ("x", "y", "c"))
    grad_x = dx0[:s_len].reshape(x.shape)

    small_names = ["dmods"] + list(REPLICATED[1:]) + [n for n, _ in SMALL_SHARDED]
    small_parts = [dmods] + [G[n] for n in small_names[1:]]
    got = _gather_small(_pack_rows(small_parts, 128, F32), "gather_small_grads")
    S8 = dict(zip(small_names, _unpack_rows(got, [a.shape for a in small_parts])))

    dm = S8["dmods"].reshape(N_DEV, DEPTH, 2, 6 * D)
    dm_mine = lax.dynamic_slice_in_dim(dm, me * cols, cols, axis=3)
    g_ada_w, pc = _ada_bwd(c_rows, ada_w, jnp.moveaxis(dm_mine[:, :, 0], 0, 1), jnp.moveaxis(dm_mine[:, :, 1], 0, 1))
    pc8 = _gather_small(pc.reshape(-1, 128), "gather_c_ctx_grad").reshape(N_DEV, 8, D)

    def owner_sums(group, after):
        sends, lands = _split_wait(sent[group], after, f"exchange_wait_{group}")
        out = {}
        for (n, j, kind), send, land in zip(GRAD_GROUPS[group], sends, lands):
            own = lax.dynamic_index_in_dim(send, me, axis=0, keepdims=False)
            out[(n, j)] = _shard_grad(_sum_slots(own, land, f"sum_slots_{n}{j}"), kind)
        return out

    shard_grads = {}
    for group in ("3", "2", "1", "0mix"):
        shard_grads.update(owner_sums(group, pc8))

    g_in = {"c_ctx": pc8[:, 0, :], "ada_w": g_ada_w[None],
            "ada_b": jnp.moveaxis(dm, 2, 1).reshape(2 * N_DEV, DEPTH, 6 * D)}
    for n in REPLICATED[1:]:
        g_in[n] = S8[n]
    for n, ax in SMALL_SHARDED:
        g_in[n] = _my_shard(S8[n], ax, me)

    def stacked(n):
        return jnp.stack([shard_grads[(n, j)] for j in range(A[n].shape[0])])[None]

    late = [n for n, j, kind in GRAD_GROUPS["0ffn"]]
    for n in BIG_WEIGHTS:
        if n not in late:
            g_in[n] = stacked(n)
    res = {n: _adamw(A[n], g_in[n], A["m_" + n], A["v_" + n], "adamw_" + n) for n in WEIGHT_ORDER if n not in late}
    done = sum(res[n][1].reshape(-1)[0] for n in res)
    shard_grads.update(owner_sums("0ffn", done.reshape(1, 1)))
    for n in late:
        res[n] = _adamw(A[n], stacked(n), A["m_" + n], A["v_" + n], "adamw_" + n)
    outs = [loss, grad_x]
    for slot in range(4):
        outs += [res[n][slot] for n in WEIGHT_ORDER]
    return tuple(outs)
```

```python
import functools
import math

import jax
import jax.numpy as jnp
from jax import lax
from jax.experimental import pallas as pl
from jax.experimental.pallas import tpu as pltpu

F32 = jnp.float32
BF16 = jnp.bfloat16
SDS = jax.ShapeDtypeStruct
MESH = pl.DeviceIdType.MESH

N_DEV = 8
EPS = 1e-6
DEPTH = 4
GRID_W = 64
POOL_WINDOWS = (2, 4, 8, 16)
N_HEADS = 8
N_KV = 2
HEAD_DIM = 128
ROPE_THETA = 10000.0
RET_HEADS = 4
RET_DK = 256
RET_DV = 512
RET_CHUNK = 128
ADAM_LR = 0.001
ADAM_B1 = 0.9
ADAM_B2 = 0.999
ADAM_EPS = 1e-08
ADAM_WD = 0.01
ADAM_STEP = 10

ROW_TILE = 256
FFN_HIDDEN_DTYPE = BF16
FLASH_FWD_TILE = 128
HALO = 8
VMEM_LIMIT_V7X = 56 * 1024 * 1024


def _params(n_axes=0):
    sem = ("arbitrary",) * n_axes if n_axes else None
    return pltpu.CompilerParams(dimension_semantics=sem, vmem_limit_bytes=VMEM_LIMIT_V7X)


def _pick(n, cap, mult):
    best = None
    for d in range(mult, min(n, cap) + 1, mult):
        if n % d == 0:
            best = d
    return best if best is not None else n


def _dot(a, b):
    return jnp.dot(a, b, preferred_element_type=F32)


def _dot_nt(a, b):
    return lax.dot_general(a, b, (((1,), (1,)), ((), ())), preferred_element_type=F32)


def _dot_tn(a, b):
    return lax.dot_general(a, b, (((0,), (0,)), ((), ())), preferred_element_type=F32)


def _bf(v):
    return v.astype(BF16)


def _sigmoid(v):
    return 0.5 * jnp.tanh(0.5 * v) + 0.5


MM_VMEM_BUDGET = 40 * 1024 * 1024
MM_STEP_BYTES = 1 << 20
MM_ACC_PASS_BYTES = 8


def _divisors(n, mult, cap):
    return [d for d in range(mult, min(n, cap) + 1, mult) if n % d == 0] or [n]


def _mm_tiles(mode, M, N, K, a_item, b_item, o_item):
    best = None
    for tm in _divisors(M, 128 if mode == "tn" else 16, 2816):
        for tn in _divisors(N, 128, 2048):
            for tk in _divisors(K, 16 if mode == "tn" else 128, 2816):
                ni, nj, nk = M // tm, N // tn, K // tk
                vmem = 2 * (tm * tk * a_item + tk * tn * b_item + tm * tn * o_item) + tm * tn * 4
                if vmem > MM_VMEM_BUDGET:
                    continue
                a_reads = 1 if nk == 1 else nj
                b_reads = 1 if (nk == 1 and nj == 1) else ni
                cost = (M * K * a_item * a_reads + K * N * b_item * b_reads + M * N * o_item
                        + ni * nj * nk * MM_STEP_BYTES + (nk - 1) * M * N * MM_ACC_PASS_BYTES)
                if best is None or cost < best[0]:
                    best = (cost, tm, tn, tk)
    return best[1:]


def _mm(a, b, mode, out_dtype, name):
    if mode == "nn":
        (M, K), (K2, N) = a.shape, b.shape
    elif mode == "nt":
        (M, K), (N, K2) = a.shape, b.shape
    else:
        (K, M), (K2, N) = a.shape, b.shape
    assert K == K2, (a.shape, b.shape, mode)
    tm, tn, tk = _mm_tiles(mode, M, N, K, a.dtype.itemsize, b.dtype.itemsize, jnp.dtype(out_dtype).itemsize)
    nk = K // tk
    if mode == "nn":
        a_spec = pl.BlockSpec((tm, tk), lambda i, j, k: (i, k))
        b_spec = pl.BlockSpec((tk, tn), lambda i, j, k: (k, j))
    elif mode == "nt":
        a_spec = pl.BlockSpec((tm, tk), lambda i, j, k: (i, k))
        b_spec = pl.BlockSpec((tn, tk), lambda i, j, k: (j, k))
    else:
        a_spec = pl.BlockSpec((tk, tm), lambda i, j, k: (k, i))
        b_spec = pl.BlockSpec((tk, tn), lambda i, j, k: (k, j))
    dot = {"nn": _dot, "nt": _dot_nt, "tn": _dot_tn}[mode]

    def body(a_ref, b_ref, o_ref, acc_ref):
        part = dot(_bf(a_ref[...]), _bf(b_ref[...]))
        if nk == 1:
            o_ref[...] = part.astype(out_dtype)
        else:
            k = pl.program_id(2)

            @pl.when(k == 0)
            def _():
                acc_ref[...] = part

            @pl.when(k > 0)
            def _():
                acc_ref[...] += part

            @pl.when(k == nk - 1)
            def _():
                o_ref[...] = acc_ref[...].astype(out_dtype)

    return pl.pallas_call(
        body, name=name, grid=(M // tm, N // tn, nk),
        in_specs=[a_spec, b_spec],
        out_specs=pl.BlockSpec((tm, tn), lambda i, j, k: (i, j)),
        out_shape=SDS((M, N), out_dtype),
        scratch_shapes=[pltpu.VMEM((tm, tn), F32)],
        compiler_params=_params(3),
    )(a, b)


def _seg_spec(n_lat, d):
    return pl.BlockSpec((1, 6, d), lambda i: ((i >= n_lat).astype(jnp.int32), 0, 0))


def _seg_acc_spec(n_lat, d):
    return pl.BlockSpec((1, 1, d), lambda i: ((i >= n_lat).astype(jnp.int32), 0, 0))


def _res_norm(x, y, gmod, gk, nw, nmod, nk, h_dtype, n_lat, name):
    R, D = x.shape
    has_res, has_norm = y is not None, nw is not None
    row = pl.BlockSpec((ROW_TILE, D), lambda i: (i, 0))
    vec = pl.BlockSpec((1, D), lambda i: (0, 0))
    ins, specs, outs, ospecs = [x], [row], [], []
    if has_res:
        ins += [y, gmod]
        specs += [row, _seg_spec(n_lat, D)]
        outs.append(SDS((R, D), F32))
        ospecs.append(row)
    if has_norm:
        ins += [nw.reshape(1, D), nmod]
        specs += [vec, _seg_spec(n_lat, D)]
        outs.append(SDS((R, D), h_dtype))
        ospecs.append(row)

    def body(*refs):
        refs = list(refs)
        z = refs.pop(0)[...]
        if has_res:
            y_ref, g_ref = refs.pop(0), refs.pop(0)
            z = z + g_ref[0, pl.ds(3 * gk + 2, 1), :] * y_ref[...].astype(F32)
        if has_norm:
            nw_ref, m_ref = refs.pop(0), refs.pop(0)
        if has_res:
            refs.pop(0)[...] = z
        if has_norm:
            r = lax.rsqrt(jnp.mean(z * z, axis=-1, keepdims=True) + EPS)
            h = (z * r) * nw_ref[...]
            h = h * (1.0 + m_ref[0, pl.ds(3 * nk + 1, 1), :]) + m_ref[0, pl.ds(3 * nk, 1), :]
            refs.pop(0)[...] = h.astype(h_dtype)

    res = pl.pallas_call(
        body, name=name, grid=(R // ROW_TILE,), in_specs=specs, out_specs=ospecs,
        out_shape=outs, compiler_params=_params(1),
    )(*ins)
    return res if len(res) > 1 else res[0]


def _gate_bwd(dz, y, mod, k, out_dtype, n_lat, name):
    R, D = dz.shape
    row = pl.BlockSpec((ROW_TILE, D), lambda i: (i, 0))

    def body(dz_ref, y_ref, m_ref, dy_ref, dg_ref):
        i = pl.program_id(0)
        dzv = dz_ref[...]
        dy_ref[...] = (m_ref[0, pl.ds(3 * k + 2, 1), :] * dzv).astype(out_dtype)

        @pl.when((i == 0) | (i == n_lat))
        def _():
            dg_ref[...] = jnp.zeros_like(dg_ref)

        dg_ref[0] += jnp.sum(dzv * y_ref[...].astype(F32), axis=0, keepdims=True)

    return pl.pallas_call(
        body, name=name, grid=(R // ROW_TILE,),
        in_specs=[row, row, _seg_spec(n_lat, D)],
        out_specs=[row, _seg_acc_spec(n_lat, D)],
        out_shape=[SDS((R, D), out_dtype), SDS((2, 1, D), F32)],
        compiler_params=_params(1),
    )(dz, y, mod)


def _norm_bwd(dz, dh, x, nw, mod, k, n_lat, name, gated=None):
    R, D = x.shape
    row = pl.BlockSpec((ROW_TILE, D), lambda i: (i, 0))
    vec = pl.BlockSpec((1, D), lambda i: (0, 0))
    ins, specs = [dz, dh, x, nw.reshape(1, D), mod], [row, row, row, vec, _seg_spec(n_lat, D)]
    outs = [SDS((R, D), F32), SDS((1, D), F32), SDS((2, 1, D), F32), SDS((2, 1, D), F32)]
    ospecs = [row, vec, _seg_acc_spec(n_lat, D), _seg_acc_spec(n_lat, D)]
    if gated is not None:
        y, gmod, gk, dy_dtype = gated
        ins += [y, gmod]
        specs += [row, _seg_spec(n_lat, D)]
        outs += [SDS((R, D), dy_dtype), SDS((2, 1, D), F32)]
        ospecs += [row, _seg_acc_spec(n_lat, D)]

    def body(dz_ref, dh_ref, x_ref, nw_ref, m_ref, *rest):
        if gated is not None:
            y_ref, g_ref, dx_ref, dnw_ref, dsh_ref, dsc_ref, dy_ref, dg_ref = rest
        else:
            dx_ref, dnw_ref, dsh_ref, dsc_ref = rest
        i = pl.program_id(0)
        xv = x_ref[...]
        dhv = dh_ref[...].astype(F32)
        nwv = nw_ref[...]
        sc1 = 1.0 + m_ref[0, pl.ds(3 * k + 1, 1), :]
        r = lax.rsqrt(jnp.mean(xv * xv, axis=-1, keepdims=True) + EPS)
        xhat = xv * r
        a = dhv * (nwv * sc1)
        dx = dz_ref[...] + r * (a - xhat * jnp.mean(a * xhat, axis=-1, keepdims=True))
        dx_ref[...] = dx

        @pl.when(i == 0)
        def _():
            dnw_ref[...] = jnp.zeros_like(dnw_ref)

        @pl.when((i == 0) | (i == n_lat))
        def _():
            dsh_ref[...] = jnp.zeros_like(dsh_ref)
            dsc_ref[...] = jnp.zeros_like(dsc_ref)
            if gated is not None:
                dg_ref[...] = jnp.zeros_like(dg_ref)

        dnw_ref[...] += jnp.sum(dhv * xhat, axis=0, keepdims=True) * sc1
        dsh_ref[0] += jnp.sum(dhv, axis=0, keepdims=True)
        dsc_ref[0] += jnp.sum(dhv * xhat, axis=0, keepdims=True) * nwv
        if gated is not None:
            dy_ref[...] = (g_ref[0, pl.ds(3 * gk + 2, 1), :] * dx).astype(dy_dtype)
            dg_ref[0] += jnp.sum(dx * y_ref[...].astype(F32), axis=0, keepdims=True)

    return pl.pallas_call(
        body, name=name, grid=(R // ROW_TILE,), in_specs=specs, out_specs=ospecs, out_shape=outs,
        compiler_params=_params(1),
    )(*ins)


def _loss_bwd(xf, target, n_lat):
    R, D = xf.shape
    row = pl.BlockSpec((ROW_TILE, D), lambda i: (i, 0))
    tgt = pl.BlockSpec((ROW_TILE, D), lambda i: (jnp.minimum(i, n_lat - 1), 0))

    def body(x_ref, t_ref, dx_ref, loss_ref):
        i = pl.program_id(0)
        e = jnp.where(i < n_lat, x_ref[...] - t_ref[...], 0.0)
        dx_ref[...] = e * (1.0 / D)

        @pl.when(i == 0)
        def _():
            loss_ref[...] = jnp.zeros_like(loss_ref)

        loss_ref[...] += 0.5 * jnp.sum(jnp.mean(e * e, axis=-1, keepdims=True))

    return pl.pallas_call(
        body, name="loss_bwd", grid=(R // ROW_TILE,),
        in_specs=[row, tgt],
        out_specs=[row, pl.BlockSpec((8, 128), lambda i: (0, 0))],
        out_shape=[SDS((R, D), F32), SDS((8, 128), F32)],
        compiler_params=_params(1),
    )(xf, target)


def _halo_rows(dtype):
    return HALO * (4 // jnp.dtype(dtype).itemsize)


def _halo_specs(n_tiles, width, tile=ROW_TILE, rows=HALO):
    per = tile // rows
    prev = pl.BlockSpec((rows, width), lambda i: (jnp.maximum(i * per - 1, 0), 0))
    nxt = pl.BlockSpec((rows, width), lambda i: (jnp.minimum((i + 1) * per, n_tiles * per - 1), 0))
    return prev, nxt


def _near_rows(p_ref, n_ref, cols, first, last):
    rows = p_ref.shape[0]
    pr = jnp.where(first, 0.0, p_ref[pl.ds(rows - HALO, HALO), cols].astype(F32))
    nx = jnp.where(last, 0.0, n_ref[pl.ds(0, HALO), cols].astype(F32))
    return pr, nx


def _edge_flags(i, n_lat, n_tiles):
    first = (i == 0) | (i == n_lat)
    last = (i == n_lat - 1) | (i == n_tiles - 1)
    return first, last


def _conv_gate_fwd(u, conv_w, conv_b, n_lat, name):
    R, F2 = u.shape
    F = F2 // 2
    n_tiles = R // ROW_TILE
    T = ROW_TILE
    cw = _pick(F, 256, 128)
    row = pl.BlockSpec((T, F2), lambda i: (i, 0))
    prev, nxt = _halo_specs(n_tiles, F2, rows=_halo_rows(u.dtype))

    def body(u_ref, p_ref, n_ref, w_ref, b_ref, o_ref):
        i = pl.program_id(0)
        first, last = _edge_flags(i, n_lat, n_tiles)
        ridx = lax.broadcasted_iota(jnp.int32, (T, 1), 0)
        hidx = lax.broadcasted_iota(jnp.int32, (HALO, 1), 0)

        def conv(c0):
            cols = pl.ds(c0, cw)
            uv = u_ref[:, cols].astype(F32)
            pr8, nx8 = _near_rows(p_ref, n_ref, cols, first, last)
            pr = jnp.sum(jnp.where(hidx == HALO - 1, pr8, 0.0), axis=0, keepdims=True)
            nx = jnp.sum(jnp.where(hidx == 0, nx8, 0.0), axis=0, keepdims=True)
            up = jnp.where(ridx == 0, pr, pltpu.roll(uv, 1, 0))
            un = jnp.where(ridx == T - 1, nx, pltpu.roll(uv, T - 1, 0))
            return (up * w_ref[pl.ds(0, 1), cols] + uv * w_ref[pl.ds(1, 1), cols]
                    + un * w_ref[pl.ds(2, 1), cols] + b_ref[:, cols])

        for c0 in range(0, F, cw):
            ca, cv = conv(c0), conv(F + c0)
            o_ref[:, pl.ds(c0, cw)] = (ca * _sigmoid(ca) * cv).astype(BF16)

    return pl.pallas_call(
        body, name=name, grid=(n_tiles,),
        in_specs=[row, prev, nxt, pl.BlockSpec((3, F2), lambda i: (0, 0)),
                  pl.BlockSpec((1, F2), lambda i: (0, 0))],
        out_specs=pl.BlockSpec((T, F), lambda i: (i, 0)),
        out_shape=SDS((R, F), BF16), compiler_params=_params(1),
    )(u, u, u, conv_w, conv_b)


def _conv_gate_bwd(u, dgact, conv_w, conv_b, n_lat, name):
    R, F2 = u.shape
    F = F2 // 2
    n_tiles = R // ROW_TILE
    T, N = ROW_TILE, ROW_TILE + 2 * HALO
    cw = _pick(F, 256, 128)
    rowu = pl.BlockSpec((T, F2), lambda i: (i, 0))
    rowg = pl.BlockSpec((T, F), lambda i: (i, 0))
    pu, nu = _halo_specs(n_tiles, F2, rows=_halo_rows(u.dtype))
    pg, ng = _halo_specs(n_tiles, F, rows=_halo_rows(dgact.dtype))

    def body(u_ref, pu_ref, nu_ref, g_ref, pg_ref, ng_ref, w_ref, b_ref, du_ref, dw_ref, db_ref):
        i = pl.program_id(0)
        first, last = _edge_flags(i, n_lat, n_tiles)

        @pl.when(i == 0)
        def _():
            dw_ref[...] = jnp.zeros_like(dw_ref)
            db_ref[...] = jnp.zeros_like(db_ref)

        def ext(t_ref, p_ref, n_ref, cols):
            pr, nx = _near_rows(p_ref, n_ref, cols, first, last)
            return jnp.concatenate([pr, t_ref[:, cols].astype(F32), nx], axis=0)

        def conv(c0):
            cols = pl.ds(c0, cw)
            e = ext(u_ref, pu_ref, nu_ref, cols)
            up, un = pltpu.roll(e, 1, 0), pltpu.roll(e, N - 1, 0)
            c = (up * w_ref[pl.ds(0, 1), cols] + e * w_ref[pl.ds(1, 1), cols]
                 + un * w_ref[pl.ds(2, 1), cols] + b_ref[:, cols])
            return c, up, e, un

        def back(c0, dc, up, e, un):
            cols = pl.ds(c0, cw)
            du = (pltpu.roll(dc, N - 1, 0) * w_ref[pl.ds(0, 1), cols] + dc * w_ref[pl.ds(1, 1), cols]
                  + pltpu.roll(dc, 1, 0) * w_ref[pl.ds(2, 1), cols])
            du_ref[:, cols] = du[HALO:HALO + T].astype(BF16)
            dct = dc[HALO:HALO + T]
            dw_ref[pl.ds(0, 1), cols] += jnp.sum(dct * up[HALO:HALO + T], axis=0, keepdims=True)
            dw_ref[pl.ds(1, 1), cols] += jnp.sum(dct * e[HALO:HALO + T], axis=0, keepdims=True)
            dw_ref[pl.ds(2, 1), cols] += jnp.sum(dct * un[HALO:HALO + T], axis=0, keepdims=True)
            db_ref[:, cols] += jnp.sum(dct, axis=0, keepdims=True)

        for c0 in range(0, F, cw):
            dg = ext(g_ref, pg_ref, ng_ref, pl.ds(c0, cw))
            ca, upa, ea, una = conv(c0)
            cv, upv, ev, unv = conv(F + c0)
            s = _sigmoid(ca)
            back(F + c0, dg * (ca * s), upv, ev, unv)
            back(c0, dg * cv * (s * (1.0 + ca * (1.0 - s))), upa, ea, una)

    return pl.pallas_call(
        body, name=name, grid=(n_tiles,),
        in_specs=[rowu, pu, nu, rowg, pg, ng, pl.BlockSpec((3, F2), lambda i: (0, 0)),
                  pl.BlockSpec((1, F2), lambda i: (0, 0))],
        out_specs=[rowu, pl.BlockSpec((3, F2), lambda i: (0, 0)), pl.BlockSpec((1, F2), lambda i: (0, 0))],
        out_shape=[SDS((R, F2), BF16), SDS((3, F2), F32), SDS((1, F2), F32)],
        compiler_params=_params(1),
    )(u, u, u, dgact, dgact, dgact, conv_w, conv_b)


def _pool_counts(i, n_lat, s_len, l_len, n_rows, offset):
    ctx = i >= n_lat
    t0 = jnp.where(ctx, i - n_lat, i) * ROW_TILE + offset
    seg = jnp.where(ctx, l_len, s_len)
    t = t0 + lax.broadcasted_iota(jnp.int32, (n_rows, 1), 0)
    out = []
    for win in POOL_WINDOWS:
        cnt = jnp.minimum(t + win // 2, seg) - jnp.maximum(t - win // 2, 0)
        out.append(jnp.maximum(cnt, 1).astype(F32))
    return out


def _window_sum(e, lo, hi, n):
    acc = None
    for j in range(lo, hi + 1):
        term = e if j == 0 else pltpu.roll(e, (-j) % n, 0)
        acc = term if acc is None else acc + term
    return acc


def _pool_fwd(h, w, b, scale, n_lat, s_len, l_len, name):
    R, D = h.shape
    G = D // 4
    n_tiles = R // ROW_TILE
    T, N = ROW_TILE, ROW_TILE + 2 * HALO
    row = pl.BlockSpec((T, D), lambda i: (i, 0))
    prev, nxt = _halo_specs(n_tiles, D)
    vec = pl.BlockSpec((1, D), lambda i: (0, 0))

    def body(h_ref, p_ref, n_ref, w_ref, b_ref, s_ref, y_ref):
        i = pl.program_id(0)
        first, last = _edge_flags(i, n_lat, n_tiles)
        cnts = _pool_counts(i, n_lat, s_len, l_len, T, 0)
        for g, win in enumerate(POOL_WINDOWS):
            cols = pl.ds(g * G, G)
            pr = jnp.where(first, 0.0, p_ref[:, cols])
            nx = jnp.where(last, 0.0, n_ref[:, cols])
            hv = h_ref[:, cols]
            e = jnp.concatenate([pr, hv, nx], axis=0)
            mean = _window_sum(e, -(win // 2), win // 2 - 1, N)[HALO:HALO + T] / cnts[g]
            yg = _dot(_bf(mean - hv), w_ref[g])
            y_ref[:, cols] = (yg + b_ref[:, cols]) * s_ref[:, cols]

    return pl.pallas_call(
        body, name=name, grid=(n_tiles,),
        in_specs=[row, prev, nxt, pl.BlockSpec((4, G, G), lambda i: (0, 0, 0)), vec, vec],
        out_specs=row, out_shape=SDS((R, D), F32), compiler_params=_params(1),
    )(h, h, h, w, b, scale)


def _pool_bwd(h, dy, w, b, scale, n_lat, s_len, l_len, name):
    R, D = h.shape
    G = D // 4
    n_tiles = R // ROW_TILE
    T, N = ROW_TILE, ROW_TILE + 2 * HALO
    row = pl.BlockSpec((T, D), lambda i: (i, 0))
    prev, nxt = _halo_specs(n_tiles, D)
    vec = pl.BlockSpec((1, D), lambda i: (0, 0))
    wspec = pl.BlockSpec((4, G, G), lambda i: (0, 0, 0))

    def body(h_ref, ph_ref, nh_ref, d_ref, pd_ref, nd_ref, w_ref, b_ref, s_ref,
             dh_ref, dw_ref, db_ref, ds_ref):
        i = pl.program_id(0)
        first, last = _edge_flags(i, n_lat, n_tiles)

        @pl.when(i == 0)
        def _():
            dw_ref[...] = jnp.zeros_like(dw_ref)
            db_ref[...] = jnp.zeros_like(db_ref)
            ds_ref[...] = jnp.zeros_like(ds_ref)

        cnts = _pool_counts(i, n_lat, s_len, l_len, T, 0)
        cnts_ext = _pool_counts(i, n_lat, s_len, l_len, N, -HALO)
        for g, win in enumerate(POOL_WINDOWS):
            cols = pl.ds(g * G, G)

            def ext(t_ref, p_ref, n_ref):
                pr = jnp.where(first, 0.0, p_ref[:, cols])
                nx = jnp.where(last, 0.0, n_ref[:, cols])
                return jnp.concatenate([pr, t_ref[:, cols], nx], axis=0)

            hv = h_ref[:, cols]
            mean = _window_sum(ext(h_ref, ph_ref, nh_ref), -(win // 2), win // 2 - 1, N)[HALO:HALO + T] / cnts[g]
            z = _bf(mean - hv)
            sc = s_ref[:, cols]
            dye = ext(d_ref, pd_ref, nd_ref)
            dt = _bf(dye * sc)
            dz = _dot_nt(dt, w_ref[g])
            dm = dz / cnts_ext[g]
            dh = _window_sum(dm, -(win // 2 - 1), win // 2, N) - dz
            dh_ref[:, cols] = dh[HALO:HALO + T]
            dyt = dye[HALO:HALO + T]
            dw_ref[g] += _dot_tn(z, dt[HALO:HALO + T])
            db_ref[:, cols] += jnp.sum(dyt * sc, axis=0, keepdims=True)
            ds_ref[:, cols] += jnp.sum(dyt * (_dot(z, w_ref[g]) + b_ref[:, cols]), axis=0, keepdims=True)

    return pl.pallas_call(
        body, name=name, grid=(n_tiles,),
        in_specs=[row, prev, nxt, row, prev, nxt, wspec, vec, vec],
        out_specs=[row, wspec, vec, vec],
        out_shape=[SDS((R, D), F32), SDS((4, G, G), F32), SDS((1, D), F32), SDS((1, D), F32)],
        compiler_params=_params(1),
    )(h, h, h, dy, dy, dy, w, b, scale)


def _rope_tables(s_len, l_len):
    t = jnp.arange(s_len)
    row = (t // GRID_W).astype(F32)
    col = (t % GRID_W).astype(F32)
    axis_dim = HEAD_DIM // 2
    inv = ROPE_THETA ** (-jnp.arange(0, axis_dim, 2, dtype=F32) / axis_dim)
    ar, ac = row[:, None] * inv, col[:, None] * inv
    cos = jnp.concatenate([jnp.cos(ar), jnp.cos(ar), jnp.cos(ac), jnp.cos(ac)], axis=-1)
    sin = jnp.concatenate([-jnp.sin(ar), jnp.sin(ar), -jnp.sin(ac), jnp.sin(ac)], axis=-1)
    cos = jnp.concatenate([cos, jnp.ones((l_len, HEAD_DIM), F32)], axis=0)
    sin = jnp.concatenate([sin, jnp.zeros((l_len, HEAD_DIM), F32)], axis=0)
    return cos, sin


def _swap_halves(v):
    lane = lax.broadcasted_iota(jnp.int32, v.shape, 1)
    return jnp.where((lane % 64) < 32, pltpu.roll(v, 96, 1), pltpu.roll(v, 32, 1))


def _qk_prep_fwd(qkv, q_gain, k_gain, cos, sin):
    R = qkv.shape[0]
    NQ, NK = N_HEADS * HEAD_DIM, N_KV * HEAD_DIM
    T = ROW_TILE
    vec = pl.BlockSpec((1, HEAD_DIM), lambda i: (0, 0))
    tab = pl.BlockSpec((T, HEAD_DIM), lambda i: (i, 0))

    def body(x_ref, qg_ref, kg_ref, c_ref, s_ref, q_ref, k_ref, v_ref):
        cosv, sinv = c_ref[...], s_ref[...]

        def prep(c0, gain):
            xh = x_ref[:, pl.ds(c0, HEAD_DIM)]
            xn = xh * lax.rsqrt(jnp.mean(xh * xh, axis=-1, keepdims=True) + EPS) * gain
            return _bf(xn * cosv + _swap_halves(xn) * sinv)

        for hd in range(N_HEADS):
            q_ref[:, pl.ds(hd * HEAD_DIM, HEAD_DIM)] = prep(hd * HEAD_DIM, qg_ref[...])
        for hd in range(N_KV):
            k_ref[:, pl.ds(hd * HEAD_DIM, HEAD_DIM)] = prep(NQ + hd * HEAD_DIM, kg_ref[...])
            v_ref[:, pl.ds(2 * hd * HEAD_DIM, HEAD_DIM)] = _bf(x_ref[:, pl.ds(NQ + NK + hd * HEAD_DIM, HEAD_DIM)])
            v_ref[:, pl.ds((2 * hd + 1) * HEAD_DIM, HEAD_DIM)] = jnp.ones((T, HEAD_DIM), BF16)

    return pl.pallas_call(
        body, name="qk_prep_fwd", grid=(R // T,),
        in_specs=[pl.BlockSpec((T, NQ + 2 * NK), lambda i: (i, 0)), vec, vec, tab, tab],
        out_specs=[pl.BlockSpec((T, NQ), lambda i: (i, 0)), pl.BlockSpec((T, NK), lambda i: (i, 0)),
                   pl.BlockSpec((T, 2 * NK), lambda i: (i, 0))],
        out_shape=[SDS((R, NQ), BF16), SDS((R, NK), BF16), SDS((R, 2 * NK), BF16)],
        compiler_params=_params(1),
    )(qkv, q_gain, k_gain, cos, sin)


def _qk_prep_bwd(qkv, dq, dk, dv, q_gain, k_gain, cos, sin):
    R = qkv.shape[0]
    NQ, NK = N_HEADS * HEAD_DIM, N_KV * HEAD_DIM
    T = ROW_TILE
    vec = pl.BlockSpec((1, HEAD_DIM), lambda i: (0, 0))
    tab = pl.BlockSpec((T, HEAD_DIM), lambda i: (i, 0))

    def body(x_ref, dq_ref, dk_ref, dv_ref, qg_ref, kg_ref, c_ref, s_ref, o_ref, dqg_ref, dkg_ref):
        i = pl.program_id(0)
        cosv, sinv = c_ref[...], s_ref[...]

        @pl.when(i == 0)
        def _():
            dqg_ref[...] = jnp.zeros_like(dqg_ref)
            dkg_ref[...] = jnp.zeros_like(dkg_ref)

        def back(c0, dout, gain, dg_ref):
            xh = x_ref[:, pl.ds(c0, HEAD_DIM)]
            r = lax.rsqrt(jnp.mean(xh * xh, axis=-1, keepdims=True) + EPS)
            xhat = xh * r
            dxn = dout * cosv + _swap_halves(dout * sinv)
            dg_ref[...] += jnp.sum(dxn * xhat, axis=0, keepdims=True)
            a = dxn * gain
            o_ref[:, pl.ds(c0, HEAD_DIM)] = _bf(r * (a - xhat * jnp.mean(a * xhat, axis=-1, keepdims=True)))

        for hd in range(N_HEADS):
            back(hd * HEAD_DIM, dq_ref[:, pl.ds(hd * HEAD_DIM, HEAD_DIM)], qg_ref[...], dqg_ref)
        for hd in range(N_KV):
            back(NQ + hd * HEAD_DIM, dk_ref[:, pl.ds(hd * HEAD_DIM, HEAD_DIM)], kg_ref[...], dkg_ref)
        o_ref[:, pl.ds(NQ + NK, NK)] = _bf(dv_ref[...])

    return pl.pallas_call(
        body, name="qk_prep_bwd", grid=(R // T,),
        in_specs=[pl.BlockSpec((T, NQ + 2 * NK), lambda i: (i, 0)), pl.BlockSpec((T, NQ), lambda i: (i, 0)),
                  pl.BlockSpec((T, NK), lambda i: (i, 0)), pl.BlockSpec((T, NK), lambda i: (i, 0)),
                  vec, vec, tab, tab],
        out_specs=[pl.BlockSpec((T, NQ + 2 * NK), lambda i: (i, 0)), vec, vec],
        out_shape=[SDS((R, NQ + 2 * NK), BF16), SDS((1, HEAD_DIM), F32), SDS((1, HEAD_DIM), F32)],
        compiler_params=_params(1),
    )(qkv, dq, dk, dv, q_gain, k_gain, cos, sin)


def _flash_fwd(q, k, v, s_len, l_len):
    R = q.shape[0]
    T = FLASH_FWD_TILE
    n_lat = s_len // T
    ck = _pick(s_len, 512, 128)
    scale = HEAD_DIM ** -0.5
    group = N_HEADS // N_KV
    GW = group * HEAD_DIM
    M = group * T
    chunks = s_len // ck
    to_log2 = scale * math.log2(math.e)

    def body(q_ref, k_ref, v_ref, o_ref, lse_ref, s_s, sc_s, ml_s, mb_s, acc_s):
        i = pl.program_id(1)
        qv = jnp.concatenate([q_ref[:, pl.ds(hh * HEAD_DIM, HEAD_DIM)] for hh in range(group)], axis=0)

        ml_s[...] = jnp.full_like(ml_s, -jnp.inf)

        def lane_max(s, n):
            m = ml_s[...]
            for t in range(n // HEAD_DIM):
                m = jnp.maximum(m, s[:, t * HEAD_DIM:(t + 1) * HEAD_DIM])
            ml_s[...] = m

        @pl.when(i < n_lat)
        def _():
            def loop(c, carry):
                s = _dot_nt(qv, k_ref[pl.ds(pl.multiple_of(c * ck, ck), ck), :])
                s_s[c] = s
                lane_max(s, ck)
                return carry
            lax.fori_loop(0, chunks, loop, 0, unroll=4 if chunks % 4 == 0 else 1)

        sc = _dot_nt(qv, k_ref[pl.ds(s_len, l_len), :])
        sc_s[...] = sc
        lane_max(sc, l_len)
        m_row = jnp.max(ml_s[...], axis=-1, keepdims=True) * to_log2
        mb_s[...] = jnp.broadcast_to(m_row, (M, ck))

        acc_s[...] = jnp.zeros_like(acc_s)

        @pl.when(i < n_lat)
        def _():
            def loop(c, carry):
                p = jnp.exp2(s_s[c] * to_log2 - mb_s[...])
                acc_s[...] += _dot(_bf(p), v_ref[pl.ds(pl.multiple_of(c * ck, ck), ck), :])
                return carry
            lax.fori_loop(0, chunks, loop, 0, unroll=4 if chunks % 4 == 0 else 1)

        p = jnp.exp2(sc_s[...] * to_log2 - mb_s[:, pl.ds(0, l_len)])
        acc_s[...] += _dot(_bf(p), v_ref[pl.ds(s_len, l_len), :])
        l_rep = acc_s[:, pl.ds(HEAD_DIM, HEAD_DIM)]
        o = acc_s[:, pl.ds(0, HEAD_DIM)] / l_rep
        for hh in range(group):
            o_ref[:, pl.ds(hh * HEAD_DIM, HEAD_DIM)] = o[hh * T:(hh + 1) * T]
        lse = (mb_s[:, pl.ds(0, HEAD_DIM)] + jnp.log2(l_rep)) * math.log(2.0)
        lse_ref[...] = jnp.max(lse, axis=-1, keepdims=True).reshape(group, T, 1)

    return pl.pallas_call(
        body, name="flash_fwd", grid=(N_KV, R // T),
        in_specs=[pl.BlockSpec((T, GW), lambda g, i: (i, g)),
                  pl.BlockSpec((R, HEAD_DIM), lambda g, i: (0, g)),
                  pl.BlockSpec((R, 2 * HEAD_DIM), lambda g, i: (0, g))],
        out_specs=[pl.BlockSpec((T, GW), lambda g, i: (i, g)),
                   pl.BlockSpec((group, T, 1), lambda g, i: (g, i, 0))],
        out_shape=[SDS((R, N_HEADS * HEAD_DIM), F32), SDS((N_HEADS, R, 1), F32)],
        scratch_shapes=[pltpu.VMEM((chunks, M, ck), F32), pltpu.VMEM((M, l_len), F32), pltpu.VMEM((M, HEAD_DIM), F32),
                        pltpu.VMEM((M, ck), F32), pltpu.VMEM((M, 2 * HEAD_DIM), F32)],
        compiler_params=_params(2),
    )(q, k, v)


def _flash_bwd(q, k, v, o, lse, do, s_len, l_len):
    R = q.shape[0]
    T = ROW_TILE
    n_lat = s_len // T
    ck = _pick(s_len, 512, 128)
    scale = HEAD_DIM ** -0.5
    group = N_HEADS // N_KV
    GW = group * HEAD_DIM
    qspec = pl.BlockSpec((T, GW), lambda g, i: (i, g))
    kspec = pl.BlockSpec((R, HEAD_DIM), lambda g, i: (0, g))

    M = group * T
    log2e = math.log2(math.e)

    def body(q_ref, do_ref, o_ref, lse_ref, k_ref, v_ref, dq_ref, dk_ref, dv_ref, dq_s, lse_s, delta_s):
        i = pl.program_id(1)

        @pl.when(i == 0)
        def _():
            dk_ref[...] = jnp.zeros_like(dk_ref)
            dv_ref[...] = jnp.zeros_like(dv_ref)

        def stacked(ref):
            return jnp.concatenate([ref[:, pl.ds(hh * HEAD_DIM, HEAD_DIM)] for hh in range(group)], axis=0)

        qv = stacked(q_ref)
        dov = stacked(do_ref)
        dob = _bf(dov)
        delta_s[...] = jnp.broadcast_to(jnp.sum(dov * stacked(o_ref), axis=-1, keepdims=True), (M, ck))
        lse_s[...] = jnp.broadcast_to(lse_ref[...].reshape(M, 1) * log2e, (M, ck))
        dq_s[...] = jnp.zeros_like(dq_s)

        def step(rows, n):
            kv, vv = k_ref[rows, :], v_ref[rows, :]
            p = jnp.exp2(_dot_nt(qv, kv) * (scale * log2e) - lse_s[:, pl.ds(0, n)])
            dv_ref[rows, :] += _dot_tn(_bf(p), dob)
            ds = _bf(p * (_dot_nt(dob, vv) - delta_s[:, pl.ds(0, n)]) * scale)
            dq_s[...] += _dot(ds, kv)
            dk_ref[rows, :] += _dot_tn(ds, qv)

        @pl.when(i < n_lat)
        def _():
            def loop(c, carry):
                step(pl.ds(pl.multiple_of(c * ck, ck), ck), ck)
                return carry
            lax.fori_loop(0, s_len // ck, loop, 0, unroll=2 if (s_len // ck) % 2 == 0 else 1)

        step(pl.ds(s_len, l_len), l_len)
        for hh in range(group):
            dq_ref[:, pl.ds(hh * HEAD_DIM, HEAD_DIM)] = dq_s[pl.ds(hh * T, T), :]

    return pl.pallas_call(
        body, name="flash_bwd", grid=(N_KV, R // T),
        in_specs=[qspec, qspec, qspec, pl.BlockSpec((group, T, 1), lambda g, i: (g, i, 0)), kspec,
                  pl.BlockSpec((R, HEAD_DIM), lambda g, i: (0, 2 * g))],
        out_specs=[qspec, kspec, kspec],
        out_shape=[SDS((R, N_HEADS * HEAD_DIM), F32), SDS((R, N_KV * HEAD_DIM), F32),
                   SDS((R, N_KV * HEAD_DIM), F32)],
        scratch_shapes=[pltpu.VMEM((M, HEAD_DIM), F32), pltpu.VMEM((M, ck), F32), pltpu.VMEM((M, ck), F32)],
        compiler_params=_params(2),
    )(q, do, o, lse, k, v)


K_SCALE = RET_DK ** -0.5


def _log_sigmoid(v):
    return -(jnp.maximum(-v, 0.0) + jnp.log(1.0 + jnp.exp(-jnp.abs(v))))


def _ret_decays(d, lg):
    C = RET_CHUNK
    ic = lax.broadcasted_iota(jnp.int32, (C, 1), 0)
    ir = lax.broadcasted_iota(jnp.int32, (1, C), 1)
    li = jnp.where(d == 0, ic, C - 1 - ic).astype(F32)
    lj = jnp.where(d == 0, ir, C - 1 - ir).astype(F32)
    diff = li - lj
    mask = jnp.where(diff >= 0, jnp.exp(jnp.maximum(diff, 0.0) * lg), 0.0)
    qd = jnp.exp((li + 1.0) * lg)
    kd = jnp.exp((C - 1.0 - li) * lg)
    cd = jnp.exp(C * lg)
    return li, diff, mask, qd, kd, cd


def _ctx_weights(d, t, lg, l_len):
    C = RET_CHUNK
    j = (t * C + lax.broadcasted_iota(jnp.int32, (C, 1), 0)).astype(F32)
    e = jnp.where(d == 0, (l_len - 1.0) - j, j)
    return e, jnp.exp(e * lg)


def _ret_specs(n_lat_c, n_ctx_c, ctx_first):
    def blk(d, t):
        if ctx_first:
            n = jnp.maximum(t - n_ctx_c, 0)
            lat = jnp.where(d == 0, n, n_lat_c - 1 - n)
            return jnp.where(t < n_ctx_c, n_lat_c + t, lat)
        n = jnp.minimum(t, n_lat_c - 1)
        lat = jnp.where(d == 0, n_lat_c - 1 - n, n)
        return jnp.where(t >= n_lat_c, t, lat)
    return blk


def _ret_fwd(proj, lgt, s_len, l_len):
    R = proj.shape[0]
    C, H, DK, DV = RET_CHUNK, RET_HEADS, RET_DK, RET_DV
    nl, nc = s_len // C, l_len // C
    blk = _ret_specs(nl, nc, True)

    def body(q_ref, k_ref, v_ref, lg_ref, o_ref, st_ref, r_s):
        d, t = pl.program_id(0), pl.program_id(1)

        @pl.when(t == 0)
        def _():
            r_s[...] = jnp.zeros_like(r_s)

        def log_gamma(hh):
            return jnp.max(_log_sigmoid(lg_ref[0, hh]), axis=-1, keepdims=True)

        @pl.when(t < nc)
        def _():
            for hh in range(H):
                qc, vc = pl.ds(hh * DK, DK), pl.ds(hh * DV, DV)
                _, w = _ctx_weights(d, t, log_gamma(hh), l_len)
                r_s[hh] += _dot_tn(_bf(k_ref[:, qc] * K_SCALE * w), _bf(v_ref[:, vc]))
                o_ref[0, :, vc] = jnp.zeros((C, DV), F32)

        @pl.when(t >= nc)
        def _():
            for hh in range(H):
                qc, vc = pl.ds(hh * DK, DK), pl.ds(hh * DV, DV)
                _, _, mask, qd, kd, cd = _ret_decays(d, log_gamma(hh))
                qb, kv, vb = _bf(q_ref[:, qc]), k_ref[:, qc] * K_SCALE, _bf(v_ref[:, vc])
                r = r_s[hh]
                st_ref[0, hh, 0] = r
                att = _dot_nt(qb, _bf(kv)) * mask
                o_ref[0, :, vc] = _dot(_bf(att), vb) + _dot(qb, _bf(r)) * qd
                r_s[hh] = r * cd + _dot_tn(_bf(kv * kd), vb)

    return pl.pallas_call(
        body, name="ret_fwd", grid=(2, nc + nl),
        in_specs=[pl.BlockSpec((C, H * DK), lambda d, t: (blk(d, t), 0)),
                  pl.BlockSpec((C, H * DK), lambda d, t: (blk(d, t), 1)),
                  pl.BlockSpec((C, H * DV), lambda d, t: (blk(d, t), 1)),
                  pl.BlockSpec((1, H, 1, 128), lambda d, t: (d, 0, 0, 0))],
        out_specs=[pl.BlockSpec((1, C, H * DV), lambda d, t: (d, blk(d, t), 0)),
                   pl.BlockSpec((1, H, 1, DK, DV), lambda d, t: (d, 0, jnp.maximum(t - nc, 0), 0, 0))],
        out_shape=[SDS((2, R, H * DV), F32), SDS((2, H, nl, DK, DV), F32)],
        scratch_shapes=[pltpu.VMEM((H, DK, DV), F32)],
        compiler_params=_params(2),
    )(proj, proj, proj, lgt)


def _ret_bwd(proj, lgt, states, do, s_len, l_len):
    R = proj.shape[0]
    C, H, DK, DV = RET_CHUNK, RET_HEADS, RET_DK, RET_DV
    nl, nc = s_len // C, l_len // C
    blk = _ret_specs(nl, nc, False)
    last = nl + nc - 1

    def body(q_ref, k_ref, v_ref, lg_ref, st_ref, do_ref, dq_ref, dk_ref, dv_ref, dlg_ref, dr_s, dl_s):
        d, t = pl.program_id(0), pl.program_id(1)

        def log_gamma(hh):
            return jnp.max(_log_sigmoid(lg_ref[0, hh]), axis=-1, keepdims=True)

        @pl.when(t == 0)
        def _():
            dr_s[...] = jnp.zeros_like(dr_s)
            dl_s[...] = jnp.zeros_like(dl_s)

        @pl.when(t < nl)
        def _():
            for hh in range(H):
                qc, vc = pl.ds(hh * DK, DK), pl.ds(hh * DV, DV)
                li, diff, mask, qd, kd, cd = _ret_decays(d, log_gamma(hh))
                qv, kv, vv, dov = q_ref[:, qc], k_ref[:, qc] * K_SCALE, v_ref[:, vc], do_ref[:, vc]
                qb, kb, vb, dob = _bf(qv), _bf(kv), _bf(vv), _bf(dov)
                r, drn = st_ref[0, hh, 0], dr_s[hh]
                rb, drb = _bf(r), _bf(drn)
                p = _dot_nt(qb, kb)
                dp = _dot_nt(dob, vb) * mask
                dpb = _bf(dp)
                doq = _bf(dov * qd)
                dq_inter = _dot_nt(doq, rb)
                dk_state = kd * _dot_nt(vb, drb)
                dq_ref[0, :, qc] = _dot(dpb, kb) + dq_inter
                dk_ref[0, :, qc] = (_dot_tn(dpb, qb) + dk_state) * K_SCALE
                dv_ref[0, :, vc] = _dot_tn(_bf(p * mask), dob) + _dot(_bf(kv * kd), drb)
                dr_s[hh] = cd * drn + _dot_tn(qb, doq)
                dl_s[hh] += (jnp.sum(dp * p * diff) + jnp.sum((li + 1.0) * qv * dq_inter)
                             + jnp.sum((C - 1.0 - li) * kv * dk_state) + C * jnp.sum(cd * r * drn))

        @pl.when(t >= nl)
        def _():
            for hh in range(H):
                qc, vc = pl.ds(hh * DK, DK), pl.ds(hh * DV, DV)
                e, w = _ctx_weights(d, t - nl, log_gamma(hh), l_len)
                kv, vb, drb = k_ref[:, qc] * K_SCALE, _bf(v_ref[:, vc]), _bf(dr_s[hh])
                dkc = w * _dot_nt(vb, drb)
                dq_ref[0, :, qc] = jnp.zeros((C, DK), F32)
                dk_ref[0, :, qc] = dkc * K_SCALE
                dv_ref[0, :, vc] = _dot(_bf(kv * w), drb)
                dl_s[hh] += jnp.sum(e * kv * dkc)

        @pl.when(t == last)
        def _():
            for hh in range(H):
                dlg_ref[0, hh] = dl_s[hh] * (1.0 / (1.0 + jnp.exp(lg_ref[0, hh])))

    return pl.pallas_call(
        body, name="ret_bwd", grid=(2, nl + nc),
        in_specs=[pl.BlockSpec((C, H * DK), lambda d, t: (blk(d, t), 0)),
                  pl.BlockSpec((C, H * DK), lambda d, t: (blk(d, t), 1)),
                  pl.BlockSpec((C, H * DV), lambda d, t: (blk(d, t), 1)),
                  pl.BlockSpec((1, H, 1, 128), lambda d, t: (d, 0, 0, 0)),
                  pl.BlockSpec((1, H, 1, DK, DV), lambda d, t: (d, 0, jnp.maximum(nl - 1 - t, 0), 0, 0)),
                  pl.BlockSpec((C, H * DV), lambda d, t: (blk(d, t), 0))],
        out_specs=[pl.BlockSpec((1, C, H * DK), lambda d, t: (d, blk(d, t), 0)),
                   pl.BlockSpec((1, C, H * DK), lambda d, t: (d, blk(d, t), 0)),
                   pl.BlockSpec((1, C, H * DV), lambda d, t: (d, blk(d, t), 0)),
                   pl.BlockSpec((1, H, 1, 128), lambda d, t: (d, 0, 0, 0))],
        out_shape=[SDS((2, R, H * DK), F32), SDS((2, R, H * DK), F32), SDS((2, R, H * DV), F32),
                   SDS((2, H, 1, 128), F32)],
        scratch_shapes=[pltpu.VMEM((H, DK, DV), F32), pltpu.VMEM((H, 1, 128), F32)],
        compiler_params=_params(2),
    )(proj, proj, proj, lgt, states, do)


def _readout_fwd(o2, proj, gn_w):
    R = proj.shape[0]
    H, DV = RET_HEADS, RET_DV
    W = H * DV
    T = ROW_TILE

    def body(o_ref, g_ref, w_ref, out_ref):
        for hh in range(H):
            cols = pl.ds(hh * DV, DV)
            y = o_ref[0, :, cols] + o_ref[1, :, cols]
            yc = y - jnp.mean(y, axis=-1, keepdims=True)
            yn = yc * lax.rsqrt(jnp.mean(yc * yc, axis=-1, keepdims=True) + EPS) * w_ref[:, cols]
            g = g_ref[:, cols]
            out_ref[:, cols] = _bf(g * _sigmoid(g) * yn)

    return pl.pallas_call(
        body, name="readout_fwd", grid=(R // T,),
        in_specs=[pl.BlockSpec((2, T, W), lambda i: (0, i, 0)), pl.BlockSpec((T, W), lambda i: (i, 2)),
                  pl.BlockSpec((1, W), lambda i: (0, 0))],
        out_specs=pl.BlockSpec((T, W), lambda i: (i, 0)),
        out_shape=SDS((R, W), BF16), compiler_params=_params(1),
    )(o2, proj, gn_w)


def _readout_bwd(o2, proj, gn_w, dgated):
    R = proj.shape[0]
    H, DV = RET_HEADS, RET_DV
    W = H * DV
    T = ROW_TILE

    def body(o_ref, g_ref, w_ref, d_ref, do_ref, dg_ref, dw_ref):
        i = pl.program_id(0)

        @pl.when(i == 0)
        def _():
            dw_ref[...] = jnp.zeros_like(dw_ref)

        for hh in range(H):
            cols = pl.ds(hh * DV, DV)
            y = o_ref[0, :, cols] + o_ref[1, :, cols]
            yc = y - jnp.mean(y, axis=-1, keepdims=True)
            rstd = lax.rsqrt(jnp.mean(yc * yc, axis=-1, keepdims=True) + EPS)
            yn0 = yc * rstd
            wv = w_ref[:, cols]
            g = g_ref[:, cols]
            s = _sigmoid(g)
            dgt = d_ref[:, cols]
            dyn = dgt * (g * s)
            dg_ref[:, cols] = _bf(dgt * (yn0 * wv) * (s * (1.0 + g * (1.0 - s))))
            dw_ref[:, cols] += jnp.sum(dyn * yn0, axis=0, keepdims=True)
            a = dyn * wv
            do_ref[:, cols] = rstd * (a - jnp.mean(a, axis=-1, keepdims=True)
                                      - yn0 * jnp.mean(a * yn0, axis=-1, keepdims=True))

    return pl.pallas_call(
        body, name="readout_bwd", grid=(R // T,),
        in_specs=[pl.BlockSpec((2, T, W), lambda i: (0, i, 0)), pl.BlockSpec((T, W), lambda i: (i, 2)),
                  pl.BlockSpec((1, W), lambda i: (0, 0)), pl.BlockSpec((T, W), lambda i: (i, 0))],
        out_specs=[pl.BlockSpec((T, W), lambda i: (i, 0)), pl.BlockSpec((T, W), lambda i: (i, 0)),
                   pl.BlockSpec((1, W), lambda i: (0, 0))],
        out_shape=[SDS((R, W), F32), SDS((R, W), BF16), SDS((1, W), F32)],
        compiler_params=_params(1),
    )(o2, proj, gn_w, dgated)


def _ret_dproj(dq2, dk2, dv2, dg):
    R = dg.shape[0]
    NQ, NV = RET_HEADS * RET_DK, RET_HEADS * RET_DV
    T = ROW_TILE

    def body(dq_ref, dk_ref, dv_ref, dg_ref, o_ref):
        o_ref[:, pl.ds(0, NQ)] = _bf(dq_ref[0] + dq_ref[1])
        o_ref[:, pl.ds(NQ, NQ)] = _bf(dk_ref[0] + dk_ref[1])
        o_ref[:, pl.ds(2 * NQ, NV)] = _bf(dv_ref[0] + dv_ref[1])
        o_ref[:, pl.ds(2 * NQ + NV, NV)] = dg_ref[...]

    return pl.pallas_call(
        body, name="ret_dproj", grid=(R // T,),
        in_specs=[pl.BlockSpec((2, T, NQ), lambda i: (0, i, 0)), pl.BlockSpec((2, T, NQ), lambda i: (0, i, 0)),
                  pl.BlockSpec((2, T, NV), lambda i: (0, i, 0)), pl.BlockSpec((T, NV), lambda i: (i, 0))],
        out_specs=pl.BlockSpec((T, 2 * NQ + 2 * NV), lambda i: (i, 0)),
        out_shape=SDS((R, 2 * NQ + 2 * NV), BF16), compiler_params=_params(1),
    )(dq2, dk2, dv2, dg)


def _silu(v):
    return v * _sigmoid(v)


def _ada_fwd(c_rows, ada_w, ada_b_shard):
    depth, D, cols = ada_w.shape

    def body(c_ref, w_ref, b_ref, o_ref):
        o_ref[0] = _dot(_bf(_silu(c_ref[...])), _bf(w_ref[0])) + b_ref[0]

    return pl.pallas_call(
        body, name="ada_fwd", grid=(depth,),
        in_specs=[pl.BlockSpec((16, D), lambda i: (0, 0)), pl.BlockSpec((1, D, cols), lambda i: (i, 0, 0)),
                  pl.BlockSpec((1, 1, cols), lambda i: (i, 0, 0))],
        out_specs=pl.BlockSpec((1, 16, cols), lambda i: (i, 0, 0)),
        out_shape=SDS((depth, 16, cols), F32), compiler_params=_params(1),
    )(c_rows, ada_w, ada_b_shard)


def _ada_bwd(c_rows, ada_w, d_lat, d_ctx):
    depth, D, cols = ada_w.shape

    def body(c_ref, w_ref, dl_ref, dc_ref, dw_ref, pc_ref):
        i = pl.program_id(0)
        cv = c_ref[...]
        a = _silu(cv)
        dcs = jnp.broadcast_to(jnp.sum(dc_ref[0], axis=0, keepdims=True), (8, cols))
        dw_ref[0] = _dot_tn(_bf(a[0:8]), _bf(dl_ref[0])) + _dot_tn(_bf(a[8:16]), _bf(dcs))

        @pl.when(i == 0)
        def _():
            pc_ref[...] = jnp.zeros_like(pc_ref)

        pc_ref[...] += _dot_nt(_bf(dcs), _bf(w_ref[0]))

        @pl.when(i == depth - 1)
        def _():
            cc = c_ref[pl.ds(8, 1), :]
            s = _sigmoid(cc)
            pc_ref[...] = pc_ref[...] * (s * (1.0 + cc * (1.0 - s)))

    return pl.pallas_call(
        body, name="ada_bwd", grid=(depth,),
        in_specs=[pl.BlockSpec((16, D), lambda i: (0, 0)), pl.BlockSpec((1, D, cols), lambda i: (i, 0, 0)),
                  pl.BlockSpec((1, 8, cols), lambda i: (i, 0, 0)), pl.BlockSpec((1, 8, cols), lambda i: (i, 0, 0))],
        out_specs=[pl.BlockSpec((1, D, cols), lambda i: (i, 0, 0)), pl.BlockSpec((8, D), lambda i: (0, 0))],
        out_shape=[SDS((depth, D, cols), F32), SDS((8, D), F32)], compiler_params=_params(1),
    )(c_rows, ada_w, d_lat, d_ctx)


def _adamw(w, g, m, v, name):
    shape = w.shape
    n = g.shape[0]
    cols = shape[-1]
    rows = w.size // cols
    tr = _pick(rows, 512, 8) if rows * cols * 4 > (1 << 20) else rows
    spec = pl.BlockSpec((tr, cols), lambda i: (i, 0))

    def body(w_ref, g_ref, m_ref, v_ref, go_ref, d_ref, mo_ref, vo_ref):
        gs = g_ref[0].astype(F32)
        for k in range(1, n):
            gs = gs + g_ref[k].astype(F32)
        mn = ADAM_B1 * m_ref[...] + (1.0 - ADAM_B1) * gs
        vn = ADAM_B2 * v_ref[...] + (1.0 - ADAM_B2) * jnp.square(gs)
        m_hat = mn / (1.0 - ADAM_B1 ** ADAM_STEP)
        v_hat = vn / (1.0 - ADAM_B2 ** ADAM_STEP)
        go_ref[...] = gs
        d_ref[...] = -ADAM_LR * (m_hat / (jnp.sqrt(v_hat) + ADAM_EPS) + ADAM_WD * w_ref[...])
        mo_ref[...] = mn
        vo_ref[...] = vn

    outs = pl.pallas_call(
        body, name=name, grid=(rows // tr,),
        in_specs=[spec, pl.BlockSpec((n, tr, cols), lambda i: (0, i, 0)), spec, spec],
        out_specs=[spec] * 4, out_shape=[SDS((rows, cols), F32)] * 4, compiler_params=_params(1),
    )(w.reshape(rows, cols), g.reshape(n, rows, cols), m.reshape(rows, cols), v.reshape(rows, cols))
    return tuple(o.reshape(shape) for o in outs)


def _sum_slots(own, recv, name):
    shape, n, cols = own.shape, recv.shape[0], own.shape[-1]
    own, recv = own.reshape(-1, cols), recv.reshape(n, -1, cols)
    rows = own.shape[0]
    tr = _pick(rows, 512, 16)

    def body(own_ref, r_ref, o_ref):
        acc = own_ref[...].astype(F32)
        for k in range(n):
            acc = acc + r_ref[k].astype(F32)
        o_ref[...] = acc

    return pl.pallas_call(
        body, name=name, grid=(rows // tr,),
        in_specs=[pl.BlockSpec((tr, cols), lambda i: (i, 0)), pl.BlockSpec((n, tr, cols), lambda i: (0, i, 0))],
        out_specs=pl.BlockSpec((tr, cols), lambda i: (i, 0)),
        out_shape=SDS((rows, cols), F32), compiler_params=_params(1),
    )(own, recv).reshape(shape)


def _position():
    return lax.axis_index("x"), lax.axis_index("y"), lax.axis_index("c")


def _peer(k, x, y, c):
    return (1 - x if k & 4 else x, 1 - y if k & 2 else y, 1 - c if k & 1 else c)


def _index(pos):
    return 4 * pos[0] + 2 * pos[1] + pos[2]


def _gather_small(v, name):
    rows, lanes = v.shape

    def body(x_ref, out_ref, send_sems, recv_sems, local_sem):
        me = _position()
        mine = pltpu.make_async_copy(x_ref, out_ref.at[_index(me)], local_sem)
        mine.start()

        def copy(k, slot):
            return pltpu.make_async_remote_copy(
                src_ref=x_ref, dst_ref=out_ref.at[slot], send_sem=send_sems.at[k - 1],
                recv_sem=recv_sems.at[k - 1], device_id=_peer(k, *me), device_id_type=MESH)

        sends = [copy(k, _index(me)) for k in range(1, N_DEV)]
        for cp in sends:
            cp.start()
        for k in range(1, N_DEV):
            copy(k, _index(_peer(k, *me))).wait_recv()
        for cp in sends:
            cp.wait_send()
        mine.wait()

    return pl.pallas_call(
        body, name=name, out_shape=SDS((N_DEV, rows, lanes), v.dtype),
        in_specs=[pl.BlockSpec(memory_space=pltpu.VMEM)],
        out_specs=pl.BlockSpec(memory_space=pltpu.VMEM),
        scratch_shapes=[pltpu.SemaphoreType.DMA((N_DEV - 1,)), pltpu.SemaphoreType.DMA((N_DEV - 1,)),
                        pltpu.SemaphoreType.DMA],
        compiler_params=pltpu.CompilerParams(vmem_limit_bytes=VMEM_LIMIT_V7X),
    )(v)


def _gather_big(v, name):
    rows, cols = v.shape

    def body(x_ref, out_ref, send_sems, recv_sems, local_sem):
        x, y, c = _position()
        me, sibling = (x, y, c), (x, y, 1 - c)
        chips = [(1 - x, y), (x, 1 - y), (1 - x, 1 - y)]

        def copy(k, block, to, src=None):
            slot = out_ref.at[_index(block)]
            return pltpu.make_async_remote_copy(
                src_ref=slot if src is None else src, dst_ref=slot, send_sem=send_sems.at[k],
                recv_sem=recv_sems.at[k], device_id=to, device_id_type=MESH)

        mine = pltpu.make_async_copy(x_ref, out_ref.at[_index(me)], local_sem)
        mine.start()
        first = [copy(0, me, sibling, src=x_ref)]
        first += [copy(1 + j, me, (*chip, c), src=x_ref) for j, chip in enumerate(chips)]
        for cp in first:
            cp.start()
        passed = [copy(4 + j, (*chip, c), sibling) for j, chip in enumerate(chips)]
        for j, chip in enumerate(chips):
            copy(1 + j, (*chip, c), me).wait_recv()
            passed[j].start()
        copy(0, sibling, me).wait_recv()
        for j, chip in enumerate(chips):
            copy(4 + j, (*chip, 1 - c), me).wait_recv()
        for cp in first + passed:
            cp.wait_send()
        mine.wait()

    return pl.pallas_call(
        body, name=name, out_shape=SDS((N_DEV, rows, cols), v.dtype),
        in_specs=[pl.BlockSpec(memory_space=pl.ANY)],
        out_specs=pl.BlockSpec(memory_space=pl.ANY),
        scratch_shapes=[pltpu.SemaphoreType.DMA((N_DEV - 1,)), pltpu.SemaphoreType.DMA((N_DEV - 1,)),
                        pltpu.SemaphoreType.DMA],
    )(v)


HBM_SPEC = pl.BlockSpec(memory_space=pltpu.HBM)
SEM_SPEC = pl.BlockSpec(memory_space=pltpu.SEMAPHORE)
SPLIT_EFFECT = pltpu.SideEffectType.DATAFLOW_SIDE_EFFECTING


def _split_start(srcs, gather, name):
    n = len(srcs)
    lands = [jnp.zeros(((N_DEV,) + s.shape) if gather else s.shape, s.dtype) for s in srcs]

    def body(*refs):
        src_refs, land_refs, sems, token = refs[:n], refs[n:2 * n], refs[2 * n:4 * n], refs[-1]
        me = _position()
        for a in range(n):
            for k in range(1, N_DEV):
                peer = _peer(k, *me)
                pltpu.make_async_remote_copy(
                    src_ref=src_refs[a] if gather else src_refs[a].at[_index(peer)],
                    dst_ref=land_refs[a].at[_index(me)], send_sem=sems[2 * a], recv_sem=sems[2 * a + 1],
                    device_id=peer, device_id_type=MESH).start()
        token[...] = jnp.zeros_like(token)

    hbm = lambda arrays: tuple(pltpu.HBM(a.shape, a.dtype) for a in arrays)
    outs = pl.pallas_call(
        body, name=name,
        out_shape=(pltpu.SemaphoreType.DMA(()),) * (2 * n) + hbm(srcs) + hbm(lands) + (SDS((8, 128), F32),),
        in_specs=(HBM_SPEC,) * (2 * n),
        out_specs=(SEM_SPEC,) * (2 * n) + (HBM_SPEC,) * (2 * n) + (pl.BlockSpec(memory_space=pltpu.VMEM),),
        input_output_aliases={a: 2 * n + a for a in range(2 * n)},
        compiler_params=pltpu.CompilerParams(has_side_effects=SPLIT_EFFECT),
    )(*[pltpu.with_memory_space_constraint(a, pltpu.HBM) for a in list(srcs) + lands])
    return outs[:2 * n], outs[2 * n:3 * n], outs[3 * n:4 * n], outs[-1]


def _split_wait(flight, after, name):
    sems, srcs, lands, _ = flight
    n = len(srcs)

    def body(*refs):
        land_refs, sem_refs = refs[n:2 * n], refs[2 * n:4 * n]
        me = _position()
        for a in range(n):
            seven = land_refs[a].at[pl.ds(0, N_DEV - 1)]
            copies = pltpu.make_async_remote_copy(
                src_ref=seven, dst_ref=seven, send_sem=sem_refs[2 * a], recv_sem=sem_refs[2 * a + 1],
                device_id=_peer(1, *me), device_id_type=MESH)
            copies.wait_send()
            copies.wait_recv()

    outs = pl.pallas_call(
        body, name=name,
        out_shape=tuple(pltpu.HBM(a.shape, a.dtype) for a in list(srcs) + list(lands)),
        in_specs=(HBM_SPEC,) * (2 * n) + (SEM_SPEC,) * (2 * n) + (pl.BlockSpec(memory_space=pl.ANY),),
        out_specs=(HBM_SPEC,) * (2 * n), input_output_aliases={a: a for a in range(2 * n)},
        compiler_params=pltpu.CompilerParams(has_side_effects=SPLIT_EFFECT),
    )(*srcs, *lands, *sems, after)
    return outs[:n], outs[n:]


def _pack_rows(arrays, lanes, dtype):
    flat = jnp.concatenate([a.astype(dtype).reshape(-1) for a in arrays])
    pad = (-flat.size) % (16 * lanes)
    if pad:
        flat = jnp.concatenate([flat, jnp.zeros((pad,), dtype)])
    return flat.reshape(-1, lanes)


def _unpack_rows(packed, shapes):
    n = packed.shape[0]
    flat = packed.reshape(n, -1)
    out, off = [], 0
    for shp in shapes:
        size = math.prod(shp)
        out.append(flat[:, off:off + size].reshape((n,) + tuple(shp)))
        off += size
    return out


def _unshard(g8, axis):
    moved = jnp.moveaxis(g8, 0, axis)
    shp = list(moved.shape)
    shp[axis:axis + 2] = [shp[axis] * shp[axis + 1]]
    return moved.reshape(shp)


def _split8(full, axis):
    shp = list(full.shape)
    shp[axis:axis + 1] = [N_DEV, shp[axis] // N_DEV]
    return jnp.moveaxis(full.reshape(shp), axis, 0)


def _my_shard(g, axis, me):
    size = g.shape[axis + 1] // N_DEV
    return lax.dynamic_slice_in_dim(g, me * size, size, axis=axis + 1)


BIG_WEIGHTS = ("ffn_w_up", "ffn_w_down", "attn_w_qkv", "attn_w_o", "ret_w_in", "ret_w_out", "pool_w")
LAYER_WEIGHTS = (
    (("ffn_w_up", 0, "cols"), ("ffn_w_down", 0, "rows"), ("pool_w", 0, "pool")),
    (("ffn_w_up", 1, "cols"), ("ffn_w_down", 1, "rows"), ("attn_w_qkv", 0, "cols"), ("attn_w_o", 0, "rows")),
    (("ffn_w_up", 2, "cols"), ("ffn_w_down", 2, "rows"), ("ret_w_in", 0, "cols"), ("ret_w_out", 0, "rows")),
    (("ffn_w_up", 3, "cols"), ("ffn_w_down", 3, "rows"), ("pool_w", 1, "pool")),
)


GRAD_GROUPS = {"3": LAYER_WEIGHTS[3], "2": LAYER_WEIGHTS[2], "1": LAYER_WEIGHTS[1],
               "0ffn": LAYER_WEIGHTS[0][:2], "0mix": LAYER_WEIGHTS[0][2:]}


def _shard_to_send(w, kind):
    w = w.astype(BF16)
    return w.T if kind == "cols" else w


def _full_from_land(land, kind):
    return _unshard(land, 1) if kind == "pool" else land.reshape(-1, land.shape[-1])


def _grad_to_send(g, kind):
    return _split8(g, 1).astype(BF16) if kind == "pool" else g.astype(BF16).reshape(N_DEV, -1, g.shape[-1])


def _shard_grad(gsum, kind):
    return gsum.T if kind == "cols" else gsum
SMALL_SHARDED = (("norm_w", 2), ("pool_b", 1), ("pool_scale", 1), ("ret_gn_w", 1), ("ffn_conv_w", 2))
REPLICATED = ("ada_b", "attn_q_gain", "attn_k_gain", "ret_decay_logit", "ffn_conv_b")
WEIGHT_ORDER = ("c_ctx", "ada_w", "ada_b", "norm_w", "pool_w", "pool_b", "pool_scale", "attn_w_qkv",
                "attn_q_gain", "attn_k_gain", "attn_w_o", "ret_w_in", "ret_decay_logit", "ret_gn_w",
                "ret_w_out", "ffn_w_up", "ffn_conv_w", "ffn_conv_b", "ffn_w_down")


def _local_step(x0, target, mods, P, get_weights, put_grads, s_len, l_len):
    n_lat = s_len // ROW_TILE
    nw = P["norm_w"]
    lgt = jnp.broadcast_to(P["ret_decay_logit"][0][:, :, None, None], (2, RET_HEADS, 1, 128))
    cos, sin = _rope_tables(s_len, l_len)
    h_dtype = [F32 if i % 3 == 0 else BF16 for i in range(DEPTH)]
    saved = []
    mods = list(mods)
    X = x0
    h = _res_norm(X, None, None, 0, nw[0, 0], mods[0], 0, h_dtype[0], n_lat, "norm_first")
    for i in range(DEPTH):
        kind, j, mod = i % 3, i // 3, mods[i]
        W = get_weights(i, X)
        sv = {"X": X, "h": h, "W": W}
        if kind == 0:
            y = _pool_fwd(h, W["pool_w"], P["pool_b"][j:j + 1], P["pool_scale"][j:j + 1],
                          n_lat, s_len, l_len, f"pool_fwd{i}")
        elif kind == 1:
            qkv = _mm(h, W["attn_w_qkv"], "nt", F32, f"qkv{i}")
            q, k, v = _qk_prep_fwd(qkv, P["attn_q_gain"][j:j + 1], P["attn_k_gain"][j:j + 1], cos, sin)
            o, lse = _flash_fwd(q, k, v, s_len, l_len)
            y = _mm(o, W["attn_w_o"], "nn", F32, f"attn_out{i}")
            sv.update(qkv=qkv, q=q, k=k, v=v, o=o, lse=lse)
        else:
            proj = _mm(h, W["ret_w_in"], "nt", F32, f"ret_in{i}")
            o2, states = _ret_fwd(proj, lgt, s_len, l_len)
            gated = _readout_fwd(o2, proj, P["ret_gn_w"][j:j + 1])
            y = _mm(gated, W["ret_w_out"], "nn", F32, f"ret_out{i}")
            sv.update(proj=proj, o2=o2, states=states, gated=gated)
        X1, h2 = _res_norm(X, y, mod, 0, nw[i, 1], mod, 1, BF16, n_lat, f"res_norm_mid{i}")
        u = _mm(h2, W["ffn_w_up"], "nt", FFN_HIDDEN_DTYPE, f"ffn_up{i}")
        gact = _conv_gate_fwd(u, P["ffn_conv_w"][i], P["ffn_conv_b"][i:i + 1], n_lat, f"conv_gate_fwd{i}")
        f = _mm(gact, W["ffn_w_down"], "nn", F32, f"ffn_down{i}")
        sv.update(y=y, X1=X1, h2=h2, u=u, gact=gact, f=f)
        saved.append(sv)
        if i + 1 < DEPTH:
            X, h = _res_norm(X1, f, mod, 1, nw[i + 1, 0], mods[i + 1], 0, h_dtype[i + 1], n_lat,
                             f"res_norm_end{i}")
        else:
            X = _res_norm(X1, f, mod, 1, None, None, 0, None, n_lat, "res_last")

    dX, loss = _loss_bwd(X, target, n_lat)
    G = {name: [None] * P[name].shape[0] for name in
         ("pool_b", "pool_scale", "attn_q_gain", "attn_k_gain", "ret_decay_logit", "ret_gn_w", "ffn_conv_w",
          "ffn_conv_b")}
    dnw = [[None, None] for _ in range(DEPTH)]
    dmods = [None] * DEPTH
    for i in reversed(range(DEPTH)):
        kind, j, mod, sv = i % 3, i // 3, mods[i], saved[i]
        W, gl = sv["W"], {}
        if i == DEPTH - 1:
            df, dg2 = _gate_bwd(dX, sv["f"], mod, 1, BF16, n_lat, f"gate_bwd_ffn{i}")
        dgact = _mm(df, W["ffn_w_down"], "nt", FFN_HIDDEN_DTYPE, f"ffn_down_dx{i}")
        gl["ffn_w_down"] = _mm(sv["gact"], df, "tn", BF16, f"ffn_down_dw{i}")
        du, dcw, dcb = _conv_gate_bwd(sv["u"], dgact, P["ffn_conv_w"][i], P["ffn_conv_b"][i:i + 1], n_lat,
                                      f"conv_gate_bwd{i}")
        G["ffn_conv_w"][i], G["ffn_conv_b"][i] = dcw, dcb[0]
        dh2 = _mm(du, W["ffn_w_up"], "nn", F32, f"ffn_up_dx{i}")
        gl["ffn_w_up"] = _mm(du, sv["h2"], "tn", BF16, f"ffn_up_dw{i}")
        if i == 0:
            mod = mod + put_grads("0ffn", gl)
        dX1, dnw[i][1], dsh2, dsc2, dy, dg1 = _norm_bwd(
            dX, dh2, sv["X1"], nw[i, 1], mod, 1, n_lat, f"norm_bwd_ffn{i}",
            gated=(sv["y"], mod, 0, F32 if kind == 0 else BF16))
        h = sv["h"]
        if kind == 0:
            dh, dpw, dpb, dps = _pool_bwd(h, dy, W["pool_w"], P["pool_b"][j:j + 1], P["pool_scale"][j:j + 1],
                                          n_lat, s_len, l_len, f"pool_bwd{i}")
            gl["pool_w"], G["pool_b"][j], G["pool_scale"][j] = dpw, dpb[0], dps[0]
        elif kind == 1:
            do = _mm(dy, W["attn_w_o"], "nt", F32, f"attn_out_dx{i}")
            gl["attn_w_o"] = _mm(sv["o"], dy, "tn", BF16, f"attn_out_dw{i}")
            dq, dk, dv = _flash_bwd(sv["q"], sv["k"], sv["v"], sv["o"], sv["lse"], do, s_len, l_len)
            dqkv, dqg, dkg = _qk_prep_bwd(sv["qkv"], dq, dk, dv, P["attn_q_gain"][j:j + 1],
                                          P["attn_k_gain"][j:j + 1], cos, sin)
            G["attn_q_gain"][j], G["attn_k_gain"][j] = dqg[0], dkg[0]
            dh = _mm(dqkv, W["attn_w_qkv"], "nn", F32, f"qkv_dx{i}")
            gl["attn_w_qkv"] = _mm(dqkv, h, "tn", BF16, f"qkv_dw{i}")
        else:
            dgated = _mm(dy, W["ret_w_out"], "nt", F32, f"ret_out_dx{i}")
            gl["ret_w_out"] = _mm(sv["gated"], dy, "tn", BF16, f"ret_out_dw{i}")
            do, dg, dgn = _readout_bwd(sv["o2"], sv["proj"], P["ret_gn_w"][j:j + 1], dgated)
            dq2, dk2, dv2, dlg = _ret_bwd(sv["proj"], lgt, sv["states"], do, s_len, l_len)
            dproj = _ret_dproj(dq2, dk2, dv2, dg)
            G["ret_gn_w"][j], G["ret_decay_logit"][j] = dgn[0], dlg[:, :, 0, 0]
            dh = _mm(dproj, W["ret_w_in"], "nn", F32, f"ret_in_dx{i}")
            gl["ret_w_in"] = _mm(dproj, h, "tn", BF16, f"ret_in_dw{i}")
        zero = put_grads(str(i) if i > 0 else "0mix", gl)
        if i > 0:
            mods[i - 1] = mods[i - 1] + zero
            dX, dnw[i][0], dsh1, dsc1, df_below, dg2_below = _norm_bwd(
                dX1, dh, sv["X"], nw[i, 0], mod, 0, n_lat, f"norm_bwd_mix{i}",
                gated=(saved[i - 1]["f"], mods[i - 1], 1, BF16))
        else:
            dX, dnw[i][0], dsh1, dsc1 = _norm_bwd(dX1, dh, sv["X"], nw[i, 0], mod, 0, n_lat, f"norm_bwd_mix{i}")
        dmods[i] = jnp.concatenate([dsh1, dsc1, dg1, dsh2, dsc2, dg2], axis=1)
        if i > 0:
            df, dg2 = df_below, dg2_below
    grads = {name: jnp.stack(parts) for name, parts in G.items()}
    grads["norm_w"] = jnp.stack([jnp.concatenate(pair, axis=0) for pair in dnw])
    return loss, dX, grads, jnp.stack(dmods)


def kernel(x, c, ctx, c_ctx, ada_w, ada_b, norm_w, pool_w, pool_b, pool_scale, attn_w_qkv, attn_q_gain,
           attn_k_gain, attn_w_o, ret_w_in, ret_decay_logit, ret_gn_w, ret_w_out, ffn_w_up, ffn_conv_w,
           ffn_conv_b, ffn_w_down, loss_target, m_c_ctx, m_ada_w, m_ada_b, m_norm_w, m_pool_w, m_pool_b,
           m_pool_scale, m_attn_w_qkv, m_attn_q_gain, m_attn_k_gain, m_attn_w_o, m_ret_w_in,
           m_ret_decay_logit, m_ret_gn_w, m_ret_w_out, m_ffn_w_up, m_ffn_conv_w, m_ffn_conv_b, m_ffn_w_down,
           v_c_ctx, v_ada_w, v_ada_b, v_norm_w, v_pool_w, v_pool_b, v_pool_scale, v_attn_w_qkv, v_attn_q_gain,
           v_attn_k_gain, v_attn_w_o, v_ret_w_in, v_ret_decay_logit, v_ret_gn_w, v_ret_w_out, v_ffn_w_up,
           v_ffn_conv_w, v_ffn_conv_b, v_ffn_w_down):
    A = dict(locals())
    me = _index(_position())
    s_len, D = x.shape[1], x.shape[2]
    l_len = ctx.shape[1]
    assert s_len % ROW_TILE == 0 and l_len % ROW_TILE == 0 and s_len % GRID_W == 0

    small = [A[n] for n, _ in SMALL_SHARDED]
    got = _gather_small(_pack_rows([c] + small, 128, F32), "gather_c_small")
    parts = _unpack_rows(got, [c.shape] + [a.shape for a in small])
    c_all = parts[0].reshape(N_DEV, D)
    P = {n: _unshard(g8, ax) for (n, ax), g8 in zip(SMALL_SHARDED, parts[1:])}

    c_rows = jnp.concatenate([c_all, c_ctx.reshape(1, D), jnp.zeros((7, D), F32)], axis=0)
    cols = ada_w.shape[2]
    ada_b_shard = lax.dynamic_slice_in_dim(ada_b, me * cols, cols, axis=1).reshape(DEPTH, 1, cols)
    mod_shard = _ada_fwd(c_rows, ada_w, ada_b_shard)
    got = _gather_small(mod_shard.reshape(-1, 128), "gather_mod").reshape(N_DEV, DEPTH, 16, cols)
    mod_lat = lax.dynamic_index_in_dim(got, me, axis=2, keepdims=False)
    mod_ctx = got[:, :, 8, :]
    mods = jnp.stack([jnp.moveaxis(mod_lat, 0, 1).reshape(DEPTH, 6, D),
                      jnp.moveaxis(mod_ctx, 0, 1).reshape(DEPTH, 6, D)], axis=1)

    shards = [[_shard_to_send(A[n][j], kind) for n, j, kind in lw] for lw in LAYER_WEIGHTS]
    pack0 = jnp.concatenate([s.reshape(-1, D) for s in shards[0]], axis=0)
    got0 = _gather_big(pack0, "gather_weights0")
    first, off = {}, 0
    for (n, j, kind), s in zip(LAYER_WEIGHTS[0], shards[0]):
        r = s.size // D
        first[n] = _full_from_land(got0[:, off:off + r].reshape((N_DEV,) + s.shape), kind)
        off += r
    flights, zero = {}, jnp.zeros((), F32)
    for i in range(1, DEPTH):
        flights[i] = _split_start(shards[i], True, f"gather_start{i}")
        zero = zero + flights[i][3][0, 0]
    mods = [mods[i] for i in range(DEPTH)]
    mods[0] = mods[0] + zero
    for n in REPLICATED:
        P[n] = A[n]

    def get_weights(i, x_now):
        if i == 0:
            return first
        owns, lands = _split_wait(flights[i], x_now, f"gather_wait{i}")
        return {n: _full_from_land(lax.dynamic_update_index_in_dim(land, own, me, axis=0), kind)
                for (n, j, kind), own, land in zip(LAYER_WEIGHTS[i], owns, lands)}

    sent = {}

    def put_grads(group, gl):
        sent[group] = _split_start([_grad_to_send(gl[n], kind) for n, j, kind in GRAD_GROUPS[group]], False,
                                   f"exchange_start_{group}")
        return sent[group][3][0, 0]

    x0 = jnp.concatenate([x[0], ctx[0]], axis=0)
    loss8, dx0, G, dmods = _local_step(x0, loss_target[0], mods, P, get_weights, put_grads, s_len, l_len)
    loss = lax.psum(loss8[0, 0], ("x", "y", "c"))
    grad_x = dx0[:s_len].reshape(x.shape)

    small_names = ["dmods"] + list(REPLICATED[1:]) + [n for n, _ in SMALL_SHARDED]
    small_parts = [dmods] + [G[n] for n in small_names[1:]]
    got = _gather_small(_pack_rows(small_parts, 128, F32), "gather_small_grads")
    S8 = dict(zip(small_names, _unpack_rows(got, [a.shape for a in small_parts])))

    dm = S8["dmods"].reshape(N_DEV, DEPTH, 2, 6 * D)
    dm_mine = lax.dynamic_slice_in_dim(dm, me * cols, cols, axis=3)
    g_ada_w, pc = _ada_bwd(c_rows, ada_w, jnp.moveaxis(dm_mine[:, :, 0], 0, 1), jnp.moveaxis(dm_mine[:, :, 1], 0, 1))
    pc8 = _gather_small(pc.reshape(-1, 128), "gather_c_ctx_grad").reshape(N_DEV, 8, D)

    def owner_sums(group, after):
        sends, lands = _split_wait(sent[group], after, f"exchange_wait_{group}")
        out = {}
        for (n, j, kind), send, land in zip(GRAD_GROUPS[group], sends, lands):
            own = lax.dynamic_index_in_dim(send, me, axis=0, keepdims=False)
            out[(n, j)] = _shard_grad(_sum_slots(own, land, f"sum_slots_{n}{j}"), kind)
        return out

    shard_grads = {}
    for group in ("3", "2", "1", "0mix"):
        shard_grads.update(owner_sums(group, pc8))

    g_in = {"c_ctx": pc8[:, 0, :], "ada_w": g_ada_w[None],
            "ada_b": jnp.moveaxis(dm, 2, 1).reshape(2 * N_DEV, DEPTH, 6 * D)}
    for n in REPLICATED[1:]:
        g_in[n] = S8[n]
    for n, ax in SMALL_SHARDED:
        g_in[n] = _my_shard(S8[n], ax, me)

    def stacked(n):
        return jnp.stack([shard_grads[(n, j)] for j in range(A[n].shape[0])])[None]

    late = [n for n, j, kind in GRAD_GROUPS["0ffn"]]
    for n in BIG_WEIGHTS:
        if n not in late:
            g_in[n] = stacked(n)
    res = {n: _adamw(A[n], g_in[n], A["m_" + n], A["v_" + n], "adamw_" + n) for n in WEIGHT_ORDER if n not in late}
    done = sum(res[n][1].reshape(-1)[0] for n in res)
    shard_grads.update(owner_sums("0ffn", done.reshape(1, 1)))
    for n in late:
        res[n] = _adamw(A[n], stacked(n), A["m_" + n], A["v_" + n], "adamw_" + n)
    outs = [loss, grad_x]
    for slot in range(4):
        outs += [res[n][slot] for n in WEIGHT_ORDER]
    return tuple(outs)
```

```python
import functools
import math

import jax
import jax.numpy as jnp
from jax import lax
from jax.experimental import pallas as pl
from jax.experimental.pallas import tpu as pltpu

F32 = jnp.float32
BF16 = jnp.bfloat16
SDS = jax.ShapeDtypeStruct
MESH = pl.DeviceIdType.MESH

N_DEV = 8
EPS = 1e-6
DEPTH = 4
GRID_W = 64
POOL_WINDOWS = (2, 4, 8, 16)
N_HEADS = 8
N_KV = 2
HEAD_DIM = 128
ROPE_THETA = 10000.0
RET_HEADS = 4
RET_DK = 256
RET_DV = 512
RET_CHUNK = 128
ADAM_LR = 0.001
ADAM_B1 = 0.9
ADAM_B2 = 0.999
ADAM_EPS = 1e-08
ADAM_WD = 0.01
ADAM_STEP = 10

ROW_TILE = 256
FFN_HIDDEN_DTYPE = BF16
FLASH_FWD_TILE = 128
HALO = 8
VMEM_LIMIT_V7X = 56 * 1024 * 1024


def _params(n_axes=0):
    sem = ("arbitrary",) * n_axes if n_axes else None
    return pltpu.CompilerParams(dimension_semantics=sem, vmem_limit_bytes=VMEM_LIMIT_V7X)


def _pick(n, cap, mult):
    best = None
    for d in range(mult, min(n, cap) + 1, mult):
        if n % d == 0:
            best = d
    return best if best is not None else n


def _dot(a, b):
    return jnp.dot(a, b, preferred_element_type=F32)


def _dot_nt(a, b):
    return lax.dot_general(a, b, (((1,), (1,)), ((), ())), preferred_element_type=F32)


def _dot_tn(a, b):
    return lax.dot_general(a, b, (((0,), (0,)), ((), ())), preferred_element_type=F32)


def _bf(v):
    return v.astype(BF16)


def _sigmoid(v):
    return 0.5 * jnp.tanh(0.5 * v) + 0.5


MM_VMEM_BUDGET = 40 * 1024 * 1024
MM_STEP_BYTES = 1 << 20
MM_ACC_PASS_BYTES = 8


def _divisors(n, mult, cap):
    return [d for d in range(mult, min(n, cap) + 1, mult) if n % d == 0] or [n]


def _mm_tiles(mode, M, N, K, a_item, b_item, o_item):
    best = None
    for tm in _divisors(M, 128 if mode == "tn" else 16, 2816):
        for tn in _divisors(N, 128, 2048):
            for tk in _divisors(K, 16 if mode == "tn" else 128, 2816):
                ni, nj, nk = M // tm, N // tn, K // tk
                vmem = 2 * (tm * tk * a_item + tk * tn * b_item + tm * tn * o_item) + tm * tn * 4
                if vmem > MM_VMEM_BUDGET:
                    continue
                a_reads = 1 if nk == 1 else nj
                b_reads = 1 if (nk == 1 and nj == 1) else ni
                cost = (M * K * a_item * a_reads + K * N * b_item * b_reads + M * N * o_item
                        + ni * nj * nk * MM_STEP_BYTES + (nk - 1) * M * N * MM_ACC_PASS_BYTES)
                if best is None or cost < best[0]:
                    best = (cost, tm, tn, tk)
    return best[1:]


def _mm(a, b, mode, out_dtype, name):
    if mode == "nn":
        (M, K), (K2, N) = a.shape, b.shape
    elif mode == "nt":
        (M, K), (N, K2) = a.shape, b.shape
    else:
        (K, M), (K2, N) = a.shape, b.shape
    assert K == K2, (a.shape, b.shape, mode)
    tm, tn, tk = _mm_tiles(mode, M, N, K, a.dtype.itemsize, b.dtype.itemsize, jnp.dtype(out_dtype).itemsize)
    nk = K // tk
    if mode == "nn":
        a_spec = pl.BlockSpec((tm, tk), lambda i, j, k: (i, k))
        b_spec = pl.BlockSpec((tk, tn), lambda i, j, k: (k, j))
    elif mode == "nt":
        a_spec = pl.BlockSpec((tm, tk), lambda i, j, k: (i, k))
        b_spec = pl.BlockSpec((tn, tk), lambda i, j, k: (j, k))
    else:
        a_spec = pl.BlockSpec((tk, tm), lambda i, j, k: (k, i))
        b_spec = pl.BlockSpec((tk, tn), lambda i, j, k: (k, j))
    dot = {"nn": _dot, "nt": _dot_nt, "tn": _dot_tn}[mode]

    def body(a_ref, b_ref, o_ref, acc_ref):
        part = dot(_bf(a_ref[...]), _bf(b_ref[...]))
        if nk == 1:
            o_ref[...] = part.astype(out_dtype)
        else:
            k = pl.program_id(2)

            @pl.when(k == 0)
            def _():
                acc_ref[...] = part

            @pl.when(k > 0)
            def _():
                acc_ref[...] += part

            @pl.when(k == nk - 1)
            def _():
                o_ref[...] = acc_ref[...].astype(out_dtype)

    return pl.pallas_call(
        body, name=name, grid=(M // tm, N // tn, nk),
        in_specs=[a_spec, b_spec],
        out_specs=pl.BlockSpec((tm, tn), lambda i, j, k: (i, j)),
        out_shape=SDS((M, N), out_dtype),
        scratch_shapes=[pltpu.VMEM((tm, tn), F32)],
        compiler_params=_params(3),
    )(a, b)


def _seg_spec(n_lat, d):
    return pl.BlockSpec((1, 6, d), lambda i: ((i >= n_lat).astype(jnp.int32), 0, 0))


def _seg_acc_spec(n_lat, d):
    return pl.BlockSpec((1, 1, d), lambda i: ((i >= n_lat).astype(jnp.int32), 0, 0))


def _res_norm(x, y, gmod, gk, nw, nmod, nk, h_dtype, n_lat, name):
    R, D = x.shape
    has_res, has_norm = y is not None, nw is not None
    row = pl.BlockSpec((ROW_TILE, D), lambda i: (i, 0))
    vec = pl.BlockSpec((1, D), lambda i: (0, 0))
    ins, specs, outs, ospecs = [x], [row], [], []
    if has_res:
        ins += [y, gmod]
        specs += [row, _seg_spec(n_lat, D)]
        outs.append(SDS((R, D), F32))
        ospecs.append(row)
    if has_norm:
        ins += [nw.reshape(1, D), nmod]
        specs += [vec, _seg_spec(n_lat, D)]
        outs.append(SDS((R, D), h_dtype))
        ospecs.append(row)

    def body(*refs):
        refs = list(refs)
        z = refs.pop(0)[...]
        if has_res:
            y_ref, g_ref = refs.pop(0), refs.pop(0)
            z = z + g_ref[0, pl.ds(3 * gk + 2, 1), :] * y_ref[...].astype(F32)
        if has_norm:
            nw_ref, m_ref = refs.pop(0), refs.pop(0)
        if has_res:
            refs.pop(0)[...] = z
        if has_norm:
            r = lax.rsqrt(jnp.mean(z * z, axis=-1, keepdims=True) + EPS)
            h = (z * r) * nw_ref[...]
            h = h * (1.0 + m_ref[0, pl.ds(3 * nk + 1, 1), :]) + m_ref[0, pl.ds(3 * nk, 1), :]
            refs.pop(0)[...] = h.astype(h_dtype)

    res = pl.pallas_call(
        body, name=name, grid=(R // ROW_TILE,), in_specs=specs, out_specs=ospecs,
        out_shape=outs, compiler_params=_params(1),
    )(*ins)
    return res if len(res) > 1 else res[0]


def _gate_bwd(dz, y, mod, k, out_dtype, n_lat, name):
    R, D = dz.shape
    row = pl.BlockSpec((ROW_TILE, D), lambda i: (i, 0))

    def body(dz_ref, y_ref, m_ref, dy_ref, dg_ref):
        i = pl.program_id(0)
        dzv = dz_ref[...]
        dy_ref[...] = (m_ref[0, pl.ds(3 * k + 2, 1), :] * dzv).astype(out_dtype)

        @pl.when((i == 0) | (i == n_lat))
        def _():
            dg_ref[...] = jnp.zeros_like(dg_ref)

        dg_ref[0] += jnp.sum(dzv * y_ref[...].astype(F32), axis=0, keepdims=True)

    return pl.pallas_call(
        body, name=name, grid=(R // ROW_TILE,),
        in_specs=[row, row, _seg_spec(n_lat, D)],
        out_specs=[row, _seg_acc_spec(n_lat, D)],
        out_shape=[SDS((R, D), out_dtype), SDS((2, 1, D), F32)],
        compiler_params=_params(1),
    )(dz, y, mod)


def _norm_bwd(dz, dh, x, nw, mod, k, n_lat, name, gated=None):
    R, D = x.shape
    row = pl.BlockSpec((ROW_TILE, D), lambda i: (i, 0))
    vec = pl.BlockSpec((1, D), lambda i: (0, 0))
    ins, specs = [dz, dh, x, nw.reshape(1, D), mod], [row, row, row, vec, _seg_spec(n_lat, D)]
    outs = [SDS((R, D), F32), SDS((1, D), F32), SDS((2, 1, D), F32), SDS((2, 1, D), F32)]
    ospecs = [row, vec, _seg_acc_spec(n_lat, D), _seg_acc_spec(n_lat, D)]
    if gated is not None:
        y, gmod, gk, dy_dtype = gated
        ins += [y, gmod]
        specs += [row, _seg_spec(n_lat, D)]
        outs += [SDS((R, D), dy_dtype), SDS((2, 1, D), F32)]
        ospecs += [row, _seg_acc_spec(n_lat, D)]

    def body(dz_ref, dh_ref, x_ref, nw_ref, m_ref, *rest):
        if gated is not None:
            y_ref, g_ref, dx_ref, dnw_ref, dsh_ref, dsc_ref, dy_ref, dg_ref = rest
        else:
            dx_ref, dnw_ref, dsh_ref, dsc_ref = rest
        i = pl.program_id(0)
        xv = x_ref[...]
        dhv = dh_ref[...].astype(F32)
        nwv = nw_ref[...]
        sc1 = 1.0 + m_ref[0, pl.ds(3 * k + 1, 1), :]
        r = lax.rsqrt(jnp.mean(xv * xv, axis=-1, keepdims=True) + EPS)
        xhat = xv * r
        a = dhv * (nwv * sc1)
        dx = dz_ref[...] + r * (a - xhat * jnp.mean(a * xhat, axis=-1, keepdims=True))
        dx_ref[...] = dx

        @pl.when(i == 0)
        def _():
            dnw_ref[...] = jnp.zeros_like(dnw_ref)

        @pl.when((i == 0) | (i == n_lat))
        def _():
            dsh_ref[...] = jnp.zeros_like(dsh_ref)
            dsc_ref[...] = jnp.zeros_like(dsc_ref)
            if gated is not None:
                dg_ref[...] = jnp.zeros_like(dg_ref)

        dnw_ref[...] += jnp.sum(dhv * xhat, axis=0, keepdims=True) * sc1
        dsh_ref[0] += jnp.sum(dhv, axis=0, keepdims=True)
        dsc_ref[0] += jnp.sum(dhv * xhat, axis=0, keepdims=True) * nwv
        if gated is not None:
            dy_ref[...] = (g_ref[0, pl.ds(3 * gk + 2, 1), :] * dx).astype(dy_dtype)
            dg_ref[0] += jnp.sum(dx * y_ref[...].astype(F32), axis=0, keepdims=True)

    return pl.pallas_call(
        body, name=name, grid=(R // ROW_TILE,), in_specs=specs, out_specs=ospecs, out_shape=outs,
        compiler_params=_params(1),
    )(*ins)


def _loss_bwd(xf, target, n_lat):
    R, D = xf.shape
    row = pl.BlockSpec((ROW_TILE, D), lambda i: (i, 0))
    tgt = pl.BlockSpec((ROW_TILE, D), lambda i: (jnp.minimum(i, n_lat - 1), 0))

    def body(x_ref, t_ref, dx_ref, loss_ref):
        i = pl.program_id(0)
        e = jnp.where(i < n_lat, x_ref[...] - t_ref[...], 0.0)
        dx_ref[...] = e * (1.0 / D)

        @pl.when(i == 0)
        def _():
            loss_ref[...] = jnp.zeros_like(loss_ref)

        loss_ref[...] += 0.5 * jnp.sum(jnp.mean(e * e, axis=-1, keepdims=True))

    return pl.pallas_call(
        body, name="loss_bwd", grid=(R // ROW_TILE,),
        in_specs=[row, tgt],
        out_specs=[row, pl.BlockSpec((8, 128), lambda i: (0, 0))],
        out_shape=[SDS((R, D), F32), SDS((8, 128), F32)],
        compiler_params=_params(1),
    )(xf, target)


def _halo_rows(dtype):
    return HALO * (4 // jnp.dtype(dtype).itemsize)


def _halo_specs(n_tiles, width, tile=ROW_TILE, rows=HALO):
    per = tile // rows
    prev = pl.BlockSpec((rows, width), lambda i: (jnp.maximum(i * per - 1, 0), 0))
    nxt = pl.BlockSpec((rows, width), lambda i: (jnp.minimum((i + 1) * per, n_tiles * per - 1), 0))
    return prev, nxt


SHIFT_K = 256


def _shift_matrix(n_out, first_row, deltas):
    half = n_out // 2
    out = []
    for h, start in enumerate((0, 2 * _halo_rows(BF16))):
        r = lax.broadcasted_iota(jnp.int32, (half, SHIFT_K), 0) + (first_row + h * half - start)
        j = lax.broadcasted_iota(jnp.int32, (half, SHIFT_K), 1)
        out.append(jnp.concatenate([(j == r + d).astype(F32) for d in deltas], axis=0).astype(BF16))
    return out


def _shifted_rows(t_ref, p_ref, n_ref, cols, first, last, picks, n_blocks):
    pr = jnp.where(first, jnp.zeros_like(p_ref[:, cols]), p_ref[:, cols])
    nx = jnp.where(last, jnp.zeros_like(n_ref[:, cols]), n_ref[:, cols])
    e = jnp.concatenate([pr, t_ref[:, cols], nx], axis=0)
    start = 2 * pr.shape[0]
    top, bot = _dot(picks[0], e[0:SHIFT_K]), _dot(picks[1], e[start:start + SHIFT_K])
    half = picks[0].shape[0] // n_blocks
    return [jnp.concatenate([top[k * half:(k + 1) * half], bot[k * half:(k + 1) * half]], axis=0)
            for k in range(n_blocks)]


def _edge_flags(i, n_lat, n_tiles):
    first = (i == 0) | (i == n_lat)
    last = (i == n_lat - 1) | (i == n_tiles - 1)
    return first, last


def _conv_gate_fwd(u, conv_w, conv_b, n_lat, name):
    R, F2 = u.shape
    F = F2 // 2
    n_tiles = R // ROW_TILE
    T = ROW_TILE
    cw = _pick(F, 256, 128)
    row = pl.BlockSpec((T, F2), lambda i: (i, 0))
    prev, nxt = _halo_specs(n_tiles, F2, rows=_halo_rows(u.dtype))

    assert u.dtype == BF16

    def body(u_ref, p_ref, n_ref, w_ref, b_ref, o_ref):
        i = pl.program_id(0)
        first, last = _edge_flags(i, n_lat, n_tiles)
        taps = _shift_matrix(T, _halo_rows(BF16), (-1, 0, 1))

        def conv(c0):
            cols = pl.ds(c0, cw)
            up, uv, un = _shifted_rows(u_ref, p_ref, n_ref, cols, first, last, taps, 3)
            return (up * w_ref[pl.ds(0, 1), cols] + uv * w_ref[pl.ds(1, 1), cols]
                    + un * w_ref[pl.ds(2, 1), cols] + b_ref[:, cols])

        for c0 in range(0, F, cw):
            ca, cv = conv(c0), conv(F + c0)
            o_ref[:, pl.ds(c0, cw)] = (ca * _sigmoid(ca) * cv).astype(BF16)

    return pl.pallas_call(
        body, name=name, grid=(n_tiles,),
        in_specs=[row, prev, nxt, pl.BlockSpec((3, F2), lambda i: (0, 0)),
                  pl.BlockSpec((1, F2), lambda i: (0, 0))],
        out_specs=pl.BlockSpec((T, F), lambda i: (i, 0)),
        out_shape=SDS((R, F), BF16), compiler_params=_params(1),
    )(u, u, u, conv_w, conv_b)


def _conv_gate_bwd(u, dgact, conv_w, conv_b, n_lat, name):
    R, F2 = u.shape
    F = F2 // 2
    n_tiles = R // ROW_TILE
    T, N = ROW_TILE, ROW_TILE + 2 * HALO
    cw = _pick(F, 256, 128)
    rowu = pl.BlockSpec((T, F2), lambda i: (i, 0))
    rowg = pl.BlockSpec((T, F), lambda i: (i, 0))
    pu, nu = _halo_specs(n_tiles, F2, rows=_halo_rows(u.dtype))
    pg, ng = _halo_specs(n_tiles, F, rows=_halo_rows(dgact.dtype))

    assert u.dtype == BF16 and dgact.dtype == BF16

    def body(u_ref, pu_ref, nu_ref, g_ref, pg_ref, ng_ref, w_ref, b_ref, du_ref, dw_ref, db_ref):
        i = pl.program_id(0)
        first, last = _edge_flags(i, n_lat, n_tiles)

        @pl.when(i == 0)
        def _():
            dw_ref[...] = jnp.zeros_like(dw_ref)
            db_ref[...] = jnp.zeros_like(db_ref)

        taps = _shift_matrix(N, _halo_rows(BF16) - HALO, (-1, 0, 1))
        same = _shift_matrix(N, _halo_rows(BF16) - HALO, (0,))

        def conv(c0):
            cols = pl.ds(c0, cw)
            up, e, un = _shifted_rows(u_ref, pu_ref, nu_ref, cols, first, last, taps, 3)
            c = (up * w_ref[pl.ds(0, 1), cols] + e * w_ref[pl.ds(1, 1), cols]
                 + un * w_ref[pl.ds(2, 1), cols] + b_ref[:, cols])
            return c, up, e, un

        def back(c0, dc, up, e, un):
            cols = pl.ds(c0, cw)
            du = (pltpu.roll(dc, N - 1, 0) * w_ref[pl.ds(0, 1), cols] + dc * w_ref[pl.ds(1, 1), cols]
                  + pltpu.roll(dc, 1, 0) * w_ref[pl.ds(2, 1), cols])
            du_ref[:, cols] = du[HALO:HALO + T].astype(BF16)
            dct = dc[HALO:HALO + T]
            dw_ref[pl.ds(0, 1), cols] += jnp.sum(dct * up[HALO:HALO + T], axis=0, keepdims=True)
            dw_ref[pl.ds(1, 1), cols] += jnp.sum(dct * e[HALO:HALO + T], axis=0, keepdims=True)
            dw_ref[pl.ds(2, 1), cols] += jnp.sum(dct * un[HALO:HALO + T], axis=0, keepdims=True)
            db_ref[:, cols] += jnp.sum(dct, axis=0, keepdims=True)

        for c0 in range(0, F, cw):
            dg, = _shifted_rows(g_ref, pg_ref, ng_ref, pl.ds(c0, cw), first, last, same, 1)
            ca, upa, ea, una = conv(c0)
            cv, upv, ev, unv = conv(F + c0)
            s = _sigmoid(ca)
            back(F + c0, dg * (ca * s), upv, ev, unv)
            back(c0, dg * cv * (s * (1.0 + ca * (1.0 - s))), upa, ea, una)

    return pl.pallas_call(
        body, name=name, grid=(n_tiles,),
        in_specs=[rowu, pu, nu, rowg, pg, ng, pl.BlockSpec((3, F2), lambda i: (0, 0)),
                  pl.BlockSpec((1, F2), lambda i: (0, 0))],
        out_specs=[rowu, pl.BlockSpec((3, F2), lambda i: (0, 0)), pl.BlockSpec((1, F2), lambda i: (0, 0))],
        out_shape=[SDS((R, F2), BF16), SDS((3, F2), F32), SDS((1, F2), F32)],
        compiler_params=_params(1),
    )(u, u, u, dgact, dgact, dgact, conv_w, conv_b)


def _pool_counts(i, n_lat, s_len, l_len, n_rows, offset):
    ctx = i >= n_lat
    t0 = jnp.where(ctx, i - n_lat, i) * ROW_TILE + offset
    seg = jnp.where(ctx, l_len, s_len)
    t = t0 + lax.broadcasted_iota(jnp.int32, (n_rows, 1), 0)
    out = []
    for win in POOL_WINDOWS:
        cnt = jnp.minimum(t + win // 2, seg) - jnp.maximum(t - win // 2, 0)
        out.append(jnp.maximum(cnt, 1).astype(F32))
    return out


def _window_sum(e, lo, hi, n):
    acc = None
    for j in range(lo, hi + 1):
        term = e if j == 0 else pltpu.roll(e, (-j) % n, 0)
        acc = term if acc is None else acc + term
    return acc


def _pool_fwd(h, w, b, scale, n_lat, s_len, l_len, name):
    R, D = h.shape
    G = D // 4
    n_tiles = R // ROW_TILE
    T, N = ROW_TILE, ROW_TILE + 2 * HALO
    row = pl.BlockSpec((T, D), lambda i: (i, 0))
    prev, nxt = _halo_specs(n_tiles, D)
    vec = pl.BlockSpec((1, D), lambda i: (0, 0))

    def body(h_ref, p_ref, n_ref, w_ref, b_ref, s_ref, y_ref):
        i = pl.program_id(0)
        first, last = _edge_flags(i, n_lat, n_tiles)
        cnts = _pool_counts(i, n_lat, s_len, l_len, T, 0)
        for g, win in enumerate(POOL_WINDOWS):
            cols = pl.ds(g * G, G)
            pr = jnp.where(first, 0.0, p_ref[:, cols])
            nx = jnp.where(last, 0.0, n_ref[:, cols])
            hv = h_ref[:, cols]
            e = jnp.concatenate([pr, hv, nx], axis=0)
            mean = _window_sum(e, -(win // 2), win // 2 - 1, N)[HALO:HALO + T] / cnts[g]
            yg = _dot(_bf(mean - hv), w_ref[g])
            y_ref[:, cols] = (yg + b_ref[:, cols]) * s_ref[:, cols]

    return pl.pallas_call(
        body, name=name, grid=(n_tiles,),
        in_specs=[row, prev, nxt, pl.BlockSpec((4, G, G), lambda i: (0, 0, 0)), vec, vec],
        out_specs=row, out_shape=SDS((R, D), F32), compiler_params=_params(1),
    )(h, h, h, w, b, scale)


def _pool_bwd(h, dy, w, b, scale, n_lat, s_len, l_len, name):
    R, D = h.shape
    G = D // 4
    n_tiles = R // ROW_TILE
    T, N = ROW_TILE, ROW_TILE + 2 * HALO
    row = pl.BlockSpec((T, D), lambda i: (i, 0))
    prev, nxt = _halo_specs(n_tiles, D)
    vec = pl.BlockSpec((1, D), lambda i: (0, 0))
    wspec = pl.BlockSpec((4, G, G), lambda i: (0, 0, 0))

    def body(h_ref, ph_ref, nh_ref, d_ref, pd_ref, nd_ref, w_ref, b_ref, s_ref,
             dh_ref, dw_ref, db_ref, ds_ref):
        i = pl.program_id(0)
        first, last = _edge_flags(i, n_lat, n_tiles)

        @pl.when(i == 0)
        def _():
            dw_ref[...] = jnp.zeros_like(dw_ref)
            db_ref[...] = jnp.zeros_like(db_ref)
            ds_ref[...] = jnp.zeros_like(ds_ref)

        cnts = _pool_counts(i, n_lat, s_len, l_len, T, 0)
        cnts_ext = _pool_counts(i, n_lat, s_len, l_len, N, -HALO)
        for g, win in enumerate(POOL_WINDOWS):
            cols = pl.ds(g * G, G)

            def ext(t_ref, p_ref, n_ref):
                pr = jnp.where(first, 0.0, p_ref[:, cols])
                nx = jnp.where(last, 0.0, n_ref[:, cols])
                return jnp.concatenate([pr, t_ref[:, cols], nx], axis=0)

            hv = h_ref[:, cols]
            mean = _window_sum(ext(h_ref, ph_ref, nh_ref), -(win // 2), win // 2 - 1, N)[HALO:HALO + T] / cnts[g]
            z = _bf(mean - hv)
            sc = s_ref[:, cols]
            dye = ext(d_ref, pd_ref, nd_ref)
            dt = _bf(dye * sc)
            dz = _dot_nt(dt, w_ref[g])
            dm = dz / cnts_ext[g]
            dh = _window_sum(dm, -(win // 2 - 1), win // 2, N) - dz
            dh_ref[:, cols] = dh[HALO:HALO + T]
            dyt = dye[HALO:HALO + T]
            dw_ref[g] += _dot_tn(z, dt[HALO:HALO + T])
            db_ref[:, cols] += jnp.sum(dyt * sc, axis=0, keepdims=True)
            ds_ref[:, cols] += jnp.sum(dyt * (_dot(z, w_ref[g]) + b_ref[:, cols]), axis=0, keepdims=True)

    return pl.pallas_call(
        body, name=name, grid=(n_tiles,),
        in_specs=[row, prev, nxt, row, prev, nxt, wspec, vec, vec],
        out_specs=[row, wspec, vec, vec],
        out_shape=[SDS((R, D), F32), SDS((4, G, G), F32), SDS((1, D), F32), SDS((1, D), F32)],
        compiler_params=_params(1),
    )(h, h, h, dy, dy, dy, w, b, scale)


def _rope_tables(s_len, l_len):
    t = jnp.arange(s_len)
    row = (t // GRID_W).astype(F32)
    col = (t % GRID_W).astype(F32)
    axis_dim = HEAD_DIM // 2
    inv = ROPE_THETA ** (-jnp.arange(0, axis_dim, 2, dtype=F32) / axis_dim)
    ar, ac = row[:, None] * inv, col[:, None] * inv
    cos = jnp.concatenate([jnp.cos(ar), jnp.cos(ar), jnp.cos(ac), jnp.cos(ac)], axis=-1)
    sin = jnp.concatenate([-jnp.sin(ar), jnp.sin(ar), -jnp.sin(ac), jnp.sin(ac)], axis=-1)
    cos = jnp.concatenate([cos, jnp.ones((l_len, HEAD_DIM), F32)], axis=0)
    sin = jnp.concatenate([sin, jnp.zeros((l_len, HEAD_DIM), F32)], axis=0)
    return cos, sin


def _swap_halves(v):
    lane = lax.broadcasted_iota(jnp.int32, v.shape, 1)
    return jnp.where((lane % 64) < 32, pltpu.roll(v, 96, 1), pltpu.roll(v, 32, 1))


def _qk_prep_fwd(qkv, q_gain, k_gain, cos, sin):
    R = qkv.shape[0]
    NQ, NK = N_HEADS * HEAD_DIM, N_KV * HEAD_DIM
    T = ROW_TILE
    vec = pl.BlockSpec((1, HEAD_DIM), lambda i: (0, 0))
    tab = pl.BlockSpec((T, HEAD_DIM), lambda i: (i, 0))

    def body(x_ref, qg_ref, kg_ref, c_ref, s_ref, q_ref, k_ref, v_ref):
        cosv, sinv = c_ref[...], s_ref[...]

        def prep(c0, gain):
            xh = x_ref[:, pl.ds(c0, HEAD_DIM)]
            xn = xh * lax.rsqrt(jnp.mean(xh * xh, axis=-1, keepdims=True) + EPS) * gain
            return _bf(xn * cosv + _swap_halves(xn) * sinv)

        for hd in range(N_HEADS):
            q_ref[:, pl.ds(hd * HEAD_DIM, HEAD_DIM)] = prep(hd * HEAD_DIM, qg_ref[...])
        for hd in range(N_KV):
            k_ref[:, pl.ds(hd * HEAD_DIM, HEAD_DIM)] = prep(NQ + hd * HEAD_DIM, kg_ref[...])
            v_ref[:, pl.ds(2 * hd * HEAD_DIM, HEAD_DIM)] = _bf(x_ref[:, pl.ds(NQ + NK + hd * HEAD_DIM, HEAD_DIM)])
            v_ref[:, pl.ds((2 * hd + 1) * HEAD_DIM, HEAD_DIM)] = jnp.ones((T, HEAD_DIM), BF16)

    return pl.pallas_call(
        body, name="qk_prep_fwd", grid=(R // T,),
        in_specs=[pl.BlockSpec((T, NQ + 2 * NK), lambda i: (i, 0)), vec, vec, tab, tab],
        out_specs=[pl.BlockSpec((T, NQ), lambda i: (i, 0)), pl.BlockSpec((T, NK), lambda i: (i, 0)),
                   pl.BlockSpec((T, 2 * NK), lambda i: (i, 0))],
        out_shape=[SDS((R, NQ), BF16), SDS((R, NK), BF16), SDS((R, 2 * NK), BF16)],
        compiler_params=_params(1),
    )(qkv, q_gain, k_gain, cos, sin)


def _qk_prep_bwd(qkv, dq, dk, dv, q_gain, k_gain, cos, sin):
    R = qkv.shape[0]
    NQ, NK = N_HEADS * HEAD_DIM, N_KV * HEAD_DIM
    T = ROW_TILE
    vec = pl.BlockSpec((1, HEAD_DIM), lambda i: (0, 0))
    tab = pl.BlockSpec((T, HEAD_DIM), lambda i: (i, 0))

    def body(x_ref, dq_ref, dk_ref, dv_ref, qg_ref, kg_ref, c_ref, s_ref, o_ref, dqg_ref, dkg_ref):
        i = pl.program_id(0)
        cosv, sinv = c_ref[...], s_ref[...]

        @pl.when(i == 0)
        def _():
            dqg_ref[...] = jnp.zeros_like(dqg_ref)
            dkg_ref[...] = jnp.zeros_like(dkg_ref)

        def back(c0, dout, gain, dg_ref):
            xh = x_ref[:, pl.ds(c0, HEAD_DIM)]
            r = lax.rsqrt(jnp.mean(xh * xh, axis=-1, keepdims=True) + EPS)
            xhat = xh * r
            dxn = dout * cosv + _swap_halves(dout * sinv)
            dg_ref[...] += jnp.sum(dxn * xhat, axis=0, keepdims=True)
            a = dxn * gain
            o_ref[:, pl.ds(c0, HEAD_DIM)] = _bf(r * (a - xhat * jnp.mean(a * xhat, axis=-1, keepdims=True)))

        for hd in range(N_HEADS):
            back(hd * HEAD_DIM, dq_ref[:, pl.ds(hd * HEAD_DIM, HEAD_DIM)], qg_ref[...], dqg_ref)
        for hd in range(N_KV):
            back(NQ + hd * HEAD_DIM, dk_ref[:, pl.ds(hd * HEAD_DIM, HEAD_DIM)], kg_ref[...], dkg_ref)
        o_ref[:, pl.ds(NQ + NK, NK)] = _bf(dv_ref[...])

    return pl.pallas_call(
        body, name="qk_prep_bwd", grid=(R // T,),
        in_specs=[pl.BlockSpec((T, NQ + 2 * NK), lambda i: (i, 0)), pl.BlockSpec((T, NQ), lambda i: (i, 0)),
                  pl.BlockSpec((T, NK), lambda i: (i, 0)), pl.BlockSpec((T, NK), lambda i: (i, 0)),
                  vec, vec, tab, tab],
        out_specs=[pl.BlockSpec((T, NQ + 2 * NK), lambda i: (i, 0)), vec, vec],
        out_shape=[SDS((R, NQ + 2 * NK), BF16), SDS((1, HEAD_DIM), F32), SDS((1, HEAD_DIM), F32)],
        compiler_params=_params(1),
    )(qkv, dq, dk, dv, q_gain, k_gain, cos, sin)


def _flash_fwd(q, k, v, s_len, l_len):
    R = q.shape[0]
    T = FLASH_FWD_TILE
    n_lat = s_len // T
    ck = _pick(s_len, 512, 128)
    scale = HEAD_DIM ** -0.5
    group = N_HEADS // N_KV
    GW = group * HEAD_DIM
    M = group * T
    chunks = s_len // ck
    to_log2 = scale * math.log2(math.e)

    def body(q_ref, k_ref, v_ref, o_ref, lse_ref, s_s, sc_s, ml_s, mb_s, acc_s):
        i = pl.program_id(1)
        qv = jnp.concatenate([q_ref[:, pl.ds(hh * HEAD_DIM, HEAD_DIM)] for hh in range(group)], axis=0)

        ml_s[...] = jnp.full_like(ml_s, -jnp.inf)

        def lane_max(s, n):
            m = ml_s[...]
            for t in range(n // HEAD_DIM):
                m = jnp.maximum(m, s[:, t * HEAD_DIM:(t + 1) * HEAD_DIM])
            ml_s[...] = m

        @pl.when(i < n_lat)
        def _():
            def loop(c, carry):
                s = _dot_nt(qv, k_ref[pl.ds(pl.multiple_of(c * ck, ck), ck), :])
                s_s[c] = s
                lane_max(s, ck)
                return carry
            lax.fori_loop(0, chunks, loop, 0, unroll=4 if chunks % 4 == 0 else 1)

        sc = _dot_nt(qv, k_ref[pl.ds(s_len, l_len), :])
        sc_s[...] = sc
        lane_max(sc, l_len)
        m_row = jnp.max(ml_s[...], axis=-1, keepdims=True) * to_log2
        mb_s[...] = jnp.broadcast_to(m_row, (M, ck))

        acc_s[...] = jnp.zeros_like(acc_s)

        @pl.when(i < n_lat)
        def _():
            def loop(c, carry):
                p = jnp.exp2(s_s[c] * to_log2 - mb_s[...])
                acc_s[...] += _dot(_bf(p), v_ref[pl.ds(pl.multiple_of(c * ck, ck), ck), :])
                return carry
            lax.fori_loop(0, chunks, loop, 0, unroll=4 if chunks % 4 == 0 else 1)

        p = jnp.exp2(sc_s[...] * to_log2 - mb_s[:, pl.ds(0, l_len)])
        acc_s[...] += _dot(_bf(p), v_ref[pl.ds(s_len, l_len), :])
        l_rep = acc_s[:, pl.ds(HEAD_DIM, HEAD_DIM)]
        o = acc_s[:, pl.ds(0, HEAD_DIM)] / l_rep
        for hh in range(group):
            o_ref[:, pl.ds(hh * HEAD_DIM, HEAD_DIM)] = o[hh * T:(hh + 1) * T]
        lse = (mb_s[:, pl.ds(0, HEAD_DIM)] + jnp.log2(l_rep)) * math.log(2.0)
        lse_ref[...] = jnp.max(lse, axis=-1, keepdims=True).reshape(group, T, 1)

    return pl.pallas_call(
        body, name="flash_fwd", grid=(N_KV, R // T),
        in_specs=[pl.BlockSpec((T, GW), lambda g, i: (i, g)),
                  pl.BlockSpec((R, HEAD_DIM), lambda g, i: (0, g)),
                  pl.BlockSpec((R, 2 * HEAD_DIM), lambda g, i: (0, g))],
        out_specs=[pl.BlockSpec((T, GW), lambda g, i: (i, g)),
                   pl.BlockSpec((group, T, 1), lambda g, i: (g, i, 0))],
        out_shape=[SDS((R, N_HEADS * HEAD_DIM), F32), SDS((N_HEADS, R, 1), F32)],
        scratch_shapes=[pltpu.VMEM((chunks, M, ck), F32), pltpu.VMEM((M, l_len), F32), pltpu.VMEM((M, HEAD_DIM), F32),
                        pltpu.VMEM((M, ck), F32), pltpu.VMEM((M, 2 * HEAD_DIM), F32)],
        compiler_params=_params(2),
    )(q, k, v)


def _flash_bwd(q, k, v, o, lse, do, s_len, l_len):
    R = q.shape[0]
    T = ROW_TILE
    n_lat = s_len // T
    ck = _pick(s_len, 512, 128)
    scale = HEAD_DIM ** -0.5
    group = N_HEADS // N_KV
    GW = group * HEAD_DIM
    qspec = pl.BlockSpec((T, GW), lambda g, i: (i, g))
    kspec = pl.BlockSpec((R, HEAD_DIM), lambda g, i: (0, g))

    M = group * T
    log2e = math.log2(math.e)

    def body(q_ref, do_ref, o_ref, lse_ref, k_ref, v_ref, dq_ref, dk_ref, dv_ref, dq_s, lse_s, delta_s):
        i = pl.program_id(1)

        @pl.when(i == 0)
        def _():
            dk_ref[...] = jnp.zeros_like(dk_ref)
            dv_ref[...] = jnp.zeros_like(dv_ref)

        def stacked(ref):
            return jnp.concatenate([ref[:, pl.ds(hh * HEAD_DIM, HEAD_DIM)] for hh in range(group)], axis=0)

        qv = stacked(q_ref)
        dov = stacked(do_ref)
        dob = _bf(dov)
        delta_s[...] = jnp.broadcast_to(jnp.sum(dov * stacked(o_ref), axis=-1, keepdims=True), (M, ck))
        lse_s[...] = jnp.broadcast_to(lse_ref[...].reshape(M, 1) * log2e, (M, ck))
        dq_s[...] = jnp.zeros_like(dq_s)

        def step(rows, n):
            kv, vv = k_ref[rows, :], v_ref[rows, :]
            p = jnp.exp2(_dot_nt(qv, kv) * (scale * log2e) - lse_s[:, pl.ds(0, n)])
            dv_ref[rows, :] += _dot_tn(_bf(p), dob)
            ds = _bf(p * (_dot_nt(dob, vv) - delta_s[:, pl.ds(0, n)]) * scale)
            dq_s[...] += _dot(ds, kv)
            dk_ref[rows, :] += _dot_tn(ds, qv)

        @pl.when(i < n_lat)
        def _():
            def loop(c, carry):
                step(pl.ds(pl.multiple_of(c * ck, ck), ck), ck)
                return carry
            lax.fori_loop(0, s_len // ck, loop, 0, unroll=2 if (s_len // ck) % 2 == 0 else 1)

        step(pl.ds(s_len, l_len), l_len)
        for hh in range(group):
            dq_ref[:, pl.ds(hh * HEAD_DIM, HEAD_DIM)] = dq_s[pl.ds(hh * T, T), :]

    return pl.pallas_call(
        body, name="flash_bwd", grid=(N_KV, R // T),
        in_specs=[qspec, qspec, qspec, pl.BlockSpec((group, T, 1), lambda g, i: (g, i, 0)), kspec,
                  pl.BlockSpec((R, HEAD_DIM), lambda g, i: (0, 2 * g))],
        out_specs=[qspec, kspec, kspec],
        out_shape=[SDS((R, N_HEADS * HEAD_DIM), F32), SDS((R, N_KV * HEAD_DIM), F32),
                   SDS((R, N_KV * HEAD_DIM), F32)],
        scratch_shapes=[pltpu.VMEM((M, HEAD_DIM), F32), pltpu.VMEM((M, ck), F32), pltpu.VMEM((M, ck), F32)],
        compiler_params=_params(2),
    )(q, do, o, lse, k, v)


K_SCALE = RET_DK ** -0.5


def _log_sigmoid(v):
    return -(jnp.maximum(-v, 0.0) + jnp.log(1.0 + jnp.exp(-jnp.abs(v))))


def _ret_decays(d, lg):
    C = RET_CHUNK
    ic = lax.broadcasted_iota(jnp.int32, (C, 1), 0)
    ir = lax.broadcasted_iota(jnp.int32, (1, C), 1)
    li = jnp.where(d == 0, ic, C - 1 - ic).astype(F32)
    lj = jnp.where(d == 0, ir, C - 1 - ir).astype(F32)
    diff = li - lj
    mask = jnp.where(diff >= 0, jnp.exp(jnp.maximum(diff, 0.0) * lg), 0.0)
    qd = jnp.exp((li + 1.0) * lg)
    kd = jnp.exp((C - 1.0 - li) * lg)
    cd = jnp.exp(C * lg)
    return li, diff, mask, qd, kd, cd


def _ctx_weights(d, t, lg, l_len):
    C = RET_CHUNK
    j = (t * C + lax.broadcasted_iota(jnp.int32, (C, 1), 0)).astype(F32)
    e = jnp.where(d == 0, (l_len - 1.0) - j, j)
    return e, jnp.exp(e * lg)


def _ret_specs(n_lat_c, n_ctx_c, ctx_first):
    def blk(d, t):
        if ctx_first:
            n = jnp.maximum(t - n_ctx_c, 0)
            lat = jnp.where(d == 0, n, n_lat_c - 1 - n)
            return jnp.where(t < n_ctx_c, n_lat_c + t, lat)
        n = jnp.minimum(t, n_lat_c - 1)
        lat = jnp.where(d == 0, n_lat_c - 1 - n, n)
        return jnp.where(t >= n_lat_c, t, lat)
    return blk


def _ret_fwd(proj, lgt, s_len, l_len):
    R = proj.shape[0]
    C, H, DK, DV = RET_CHUNK, RET_HEADS, RET_DK, RET_DV
    nl, nc = s_len // C, l_len // C
    blk = _ret_specs(nl, nc, True)

    def body(q_ref, k_ref, v_ref, lg_ref, o_ref, st_ref, r_s):
        d, t = pl.program_id(0), pl.program_id(1)

        @pl.when(t == 0)
        def _():
            r_s[...] = jnp.zeros_like(r_s)

        def log_gamma(hh):
            return jnp.max(_log_sigmoid(lg_ref[0, hh]), axis=-1, keepdims=True)

        @pl.when(t < nc)
        def _():
            for hh in range(H):
                qc, vc = pl.ds(hh * DK, DK), pl.ds(hh * DV, DV)
                _, w = _ctx_weights(d, t, log_gamma(hh), l_len)
                r_s[hh] += _dot_tn(_bf(k_ref[:, qc] * K_SCALE * w), _bf(v_ref[:, vc]))
                o_ref[0, :, vc] = jnp.zeros((C, DV), F32)

        @pl.when(t >= nc)
        def _():
            for hh in range(H):
                qc, vc = pl.ds(hh * DK, DK), pl.ds(hh * DV, DV)
                _, _, mask, qd, kd, cd = _ret_decays(d, log_gamma(hh))
                qb, kv, vb = _bf(q_ref[:, qc]), k_ref[:, qc] * K_SCALE, _bf(v_ref[:, vc])
                r = r_s[hh]
                st_ref[0, hh, 0] = r
                att = _dot_nt(qb, _bf(kv)) * mask
                o_ref[0, :, vc] = _dot(_bf(att), vb) + _dot(qb, _bf(r)) * qd
                r_s[hh] = r * cd + _dot_tn(_bf(kv * kd), vb)

    return pl.pallas_call(
        body, name="ret_fwd", grid=(2, nc + nl),
        in_specs=[pl.BlockSpec((C, H * DK), lambda d, t: (blk(d, t), 0)),
                  pl.BlockSpec((C, H * DK), lambda d, t: (blk(d, t), 1)),
                  pl.BlockSpec((C, H * DV), lambda d, t: (blk(d, t), 1)),
                  pl.BlockSpec((1, H, 1, 128), lambda d, t: (d, 0, 0, 0))],
        out_specs=[pl.BlockSpec((1, C, H * DV), lambda d, t: (d, blk(d, t), 0)),
                   pl.BlockSpec((1, H, 1, DK, DV), lambda d, t: (d, 0, jnp.maximum(t - nc, 0), 0, 0))],
        out_shape=[SDS((2, R, H * DV), F32), SDS((2, H, nl, DK, DV), F32)],
        scratch_shapes=[pltpu.VMEM((H, DK, DV), F32)],
        compiler_params=_params(2),
    )(proj, proj, proj, lgt)


def _ret_bwd(proj, lgt, states, do, s_len, l_len):
    R = proj.shape[0]
    C, H, DK, DV = RET_CHUNK, RET_HEADS, RET_DK, RET_DV
    nl, nc = s_len // C, l_len // C
    blk = _ret_specs(nl, nc, False)
    last = nl + nc - 1

    def body(q_ref, k_ref, v_ref, lg_ref, st_ref, do_ref, dq_ref, dk_ref, dv_ref, dlg_ref, dr_s, dl_s):
        d, t = pl.program_id(0), pl.program_id(1)

        def log_gamma(hh):
            return jnp.max(_log_sigmoid(lg_ref[0, hh]), axis=-1, keepdims=True)

        @pl.when(t == 0)
        def _():
            dr_s[...] = jnp.zeros_like(dr_s)
            dl_s[...] = jnp.zeros_like(dl_s)

        @pl.when(t < nl)
        def _():
            for hh in range(H):
                qc, vc = pl.ds(hh * DK, DK), pl.ds(hh * DV, DV)
                li, diff, mask, qd, kd, cd = _ret_decays(d, log_gamma(hh))
                qv, kv, vv, dov = q_ref[:, qc], k_ref[:, qc] * K_SCALE, v_ref[:, vc], do_ref[:, vc]
                qb, kb, vb, dob = _bf(qv), _bf(kv), _bf(vv), _bf(dov)
                r, drn = st_ref[0, hh, 0], dr_s[hh]
                rb, drb = _bf(r), _bf(drn)
                p = _dot_nt(qb, kb)
                dp = _dot_nt(dob, vb) * mask
                dpb = _bf(dp)
                doq = _bf(dov * qd)
                dq_inter = _dot_nt(doq, rb)
                dk_state = kd * _dot_nt(vb, drb)
                dq_ref[0, :, qc] = _dot(dpb, kb) + dq_inter
                dk_ref[0, :, qc] = (_dot_tn(dpb, qb) + dk_state) * K_SCALE
                dv_ref[0, :, vc] = _dot_tn(_bf(p * mask), dob) + _dot(_bf(kv * kd), drb)
                dr_s[hh] = cd * drn + _dot_tn(qb, doq)
                dl_s[hh] += (jnp.sum(dp * p * diff) + jnp.sum((li + 1.0) * qv * dq_inter)
                             + jnp.sum((C - 1.0 - li) * kv * dk_state) + C * jnp.sum(cd * r * drn))

        @pl.when(t >= nl)
        def _():
            for hh in range(H):
                qc, vc = pl.ds(hh * DK, DK), pl.ds(hh * DV, DV)
                e, w = _ctx_weights(d, t - nl, log_gamma(hh), l_len)
                kv, vb, drb = k_ref[:, qc] * K_SCALE, _bf(v_ref[:, vc]), _bf(dr_s[hh])
                dkc = w * _dot_nt(vb, drb)
                dq_ref[0, :, qc] = jnp.zeros((C, DK), F32)
                dk_ref[0, :, qc] = dkc * K_SCALE
                dv_ref[0, :, vc] = _dot(_bf(kv * w), drb)
                dl_s[hh] += jnp.sum(e * kv * dkc)

        @pl.when(t == last)
        def _():
            for hh in range(H):
                dlg_ref[0, hh] = dl_s[hh] * (1.0 / (1.0 + jnp.exp(lg_ref[0, hh])))

    return pl.pallas_call(
        body, name="ret_bwd", grid=(2, nl + nc),
        in_specs=[pl.BlockSpec((C, H * DK), lambda d, t: (blk(d, t), 0)),
                  pl.BlockSpec((C, H * DK), lambda d, t: (blk(d, t), 1)),
                  pl.BlockSpec((C, H * DV), lambda d, t: (blk(d, t), 1)),
                  pl.BlockSpec((1, H, 1, 128), lambda d, t: (d, 0, 0, 0)),
                  pl.BlockSpec((1, H, 1, DK, DV), lambda d, t: (d, 0, jnp.maximum(nl - 1 - t, 0), 0, 0)),
                  pl.BlockSpec((C, H * DV), lambda d, t: (blk(d, t), 0))],
        out_specs=[pl.BlockSpec((1, C, H * DK), lambda d, t: (d, blk(d, t), 0)),
                   pl.BlockSpec((1, C, H * DK), lambda d, t: (d, blk(d, t), 0)),
                   pl.BlockSpec((1, C, H * DV), lambda d, t: (d, blk(d, t), 0)),
                   pl.BlockSpec((1, H, 1, 128), lambda d, t: (d, 0, 0, 0))],
        out_shape=[SDS((2, R, H * DK), F32), SDS((2, R, H * DK), F32), SDS((2, R, H * DV), F32),
                   SDS((2, H, 1, 128), F32)],
        scratch_shapes=[pltpu.VMEM((H, DK, DV), F32), pltpu.VMEM((H, 1, 128), F32)],
        compiler_params=_params(2),
    )(proj, proj, proj, lgt, states, do)


def _readout_fwd(o2, proj, gn_w):
    R = proj.shape[0]
    H, DV = RET_HEADS, RET_DV
    W = H * DV
    T = ROW_TILE

    def body(o_ref, g_ref, w_ref, out_ref):
        for hh in range(H):
            cols = pl.ds(hh * DV, DV)
            y = o_ref[0, :, cols] + o_ref[1, :, cols]
            yc = y - jnp.mean(y, axis=-1, keepdims=True)
            yn = yc * lax.rsqrt(jnp.mean(yc * yc, axis=-1, keepdims=True) + EPS) * w_ref[:, cols]
            g = g_ref[:, cols]
            out_ref[:, cols] = _bf(g * _sigmoid(g) * yn)

    return pl.pallas_call(
        body, name="readout_fwd", grid=(R // T,),
        in_specs=[pl.BlockSpec((2, T, W), lambda i: (0, i, 0)), pl.BlockSpec((T, W), lambda i: (i, 2)),
                  pl.BlockSpec((1, W), lambda i: (0, 0))],
        out_specs=pl.BlockSpec((T, W), lambda i: (i, 0)),
        out_shape=SDS((R, W), BF16), compiler_params=_params(1),
    )(o2, proj, gn_w)


def _readout_bwd(o2, proj, gn_w, dgated):
    R = proj.shape[0]
    H, DV = RET_HEADS, RET_DV
    W = H * DV
    T = ROW_TILE

    def body(o_ref, g_ref, w_ref, d_ref, do_ref, dg_ref, dw_ref):
        i = pl.program_id(0)

        @pl.when(i == 0)
        def _():
            dw_ref[...] = jnp.zeros_like(dw_ref)

        for hh in range(H):
            cols = pl.ds(hh * DV, DV)
            y = o_ref[0, :, cols] + o_ref[1, :, cols]
            yc = y - jnp.mean(y, axis=-1, keepdims=True)
            rstd = lax.rsqrt(jnp.mean(yc * yc, axis=-1, keepdims=True) + EPS)
            yn0 = yc * rstd
            wv = w_ref[:, cols]
            g = g_ref[:, cols]
            s = _sigmoid(g)
            dgt = d_ref[:, cols]
            dyn = dgt * (g * s)
            dg_ref[:, cols] = _bf(dgt * (yn0 * wv) * (s * (1.0 + g * (1.0 - s))))
            dw_ref[:, cols] += jnp.sum(dyn * yn0, axis=0, keepdims=True)
            a = dyn * wv
            do_ref[:, cols] = rstd * (a - jnp.mean(a, axis=-1, keepdims=True)
                                      - yn0 * jnp.mean(a * yn0, axis=-1, keepdims=True))

    return pl.pallas_call(
        body, name="readout_bwd", grid=(R // T,),
        in_specs=[pl.BlockSpec((2, T, W), lambda i: (0, i, 0)), pl.BlockSpec((T, W), lambda i: (i, 2)),
                  pl.BlockSpec((1, W), lambda i: (0, 0)), pl.BlockSpec((T, W), lambda i: (i, 0))],
        out_specs=[pl.BlockSpec((T, W), lambda i: (i, 0)), pl.BlockSpec((T, W), lambda i: (i, 0)),
                   pl.BlockSpec((1, W), lambda i: (0, 0))],
        out_shape=[SDS((R, W), F32), SDS((R, W), BF16), SDS((1, W), F32)],
        compiler_params=_params(1),
    )(o2, proj, gn_w, dgated)


def _ret_dproj(dq2, dk2, dv2, dg):
    R = dg.shape[0]
    NQ, NV = RET_HEADS * RET_DK, RET_HEADS * RET_DV
    T = ROW_TILE

    def body(dq_ref, dk_ref, dv_ref, dg_ref, o_ref):
        o_ref[:, pl.ds(0, NQ)] = _bf(dq_ref[0] + dq_ref[1])
        o_ref[:, pl.ds(NQ, NQ)] = _bf(dk_ref[0] + dk_ref[1])
        o_ref[:, pl.ds(2 * NQ, NV)] = _bf(dv_ref[0] + dv_ref[1])
        o_ref[:, pl.ds(2 * NQ + NV, NV)] = dg_ref[...]

    return pl.pallas_call(
        body, name="ret_dproj", grid=(R // T,),
        in_specs=[pl.BlockSpec((2, T, NQ), lambda i: (0, i, 0)), pl.BlockSpec((2, T, NQ), lambda i: (0, i, 0)),
                  pl.BlockSpec((2, T, NV), lambda i: (0, i, 0)), pl.BlockSpec((T, NV), lambda i: (i, 0))],
        out_specs=pl.BlockSpec((T, 2 * NQ + 2 * NV), lambda i: (i, 0)),
        out_shape=SDS((R, 2 * NQ + 2 * NV), BF16), compiler_params=_params(1),
    )(dq2, dk2, dv2, dg)


def _silu(v):
    return v * _sigmoid(v)


def _ada_fwd(c_rows, ada_w, ada_b_shard):
    depth, D, cols = ada_w.shape

    def body(c_ref, w_ref, b_ref, o_ref):
        o_ref[0] = _dot(_bf(_silu(c_ref[...])), _bf(w_ref[0])) + b_ref[0]

    return pl.pallas_call(
        body, name="ada_fwd", grid=(depth,),
        in_specs=[pl.BlockSpec((16, D), lambda i: (0, 0)), pl.BlockSpec((1, D, cols), lambda i: (i, 0, 0)),
                  pl.BlockSpec((1, 1, cols), lambda i: (i, 0, 0))],
        out_specs=pl.BlockSpec((1, 16, cols), lambda i: (i, 0, 0)),
        out_shape=SDS((depth, 16, cols), F32), compiler_params=_params(1),
    )(c_rows, ada_w, ada_b_shard)


def _ada_bwd(c_rows, ada_w, d_lat, d_ctx):
    depth, D, cols = ada_w.shape

    def body(c_ref, w_ref, dl_ref, dc_ref, dw_ref, pc_ref):
        i = pl.program_id(0)
        cv = c_ref[...]
        a = _silu(cv)
        dcs = jnp.broadcast_to(jnp.sum(dc_ref[0], axis=0, keepdims=True), (8, cols))
        dw_ref[0] = _dot_tn(_bf(a[0:8]), _bf(dl_ref[0])) + _dot_tn(_bf(a[8:16]), _bf(dcs))

        @pl.when(i == 0)
        def _():
            pc_ref[...] = jnp.zeros_like(pc_ref)

        pc_ref[...] += _dot_nt(_bf(dcs), _bf(w_ref[0]))

        @pl.when(i == depth - 1)
        def _():
            cc = c_ref[pl.ds(8, 1), :]
            s = _sigmoid(cc)
            pc_ref[...] = pc_ref[...] * (s * (1.0 + cc * (1.0 - s)))

    return pl.pallas_call(
        body, name="ada_bwd", grid=(depth,),
        in_specs=[pl.BlockSpec((16, D), lambda i: (0, 0)), pl.BlockSpec((1, D, cols), lambda i: (i, 0, 0)),
                  pl.BlockSpec((1, 8, cols), lambda i: (i, 0, 0)), pl.BlockSpec((1, 8, cols), lambda i: (i, 0, 0))],
        out_specs=[pl.BlockSpec((1, D, cols), lambda i: (i, 0, 0)), pl.BlockSpec((8, D), lambda i: (0, 0))],
        out_shape=[SDS((depth, D, cols), F32), SDS((8, D), F32)], compiler_params=_params(1),
    )(c_rows, ada_w, d_lat, d_ctx)


def _adamw(w, g, m, v, name):
    shape = w.shape
    n = g.shape[0]
    cols = shape[-1]
    rows = w.size // cols
    tr = _pick(rows, 512, 8) if rows * cols * 4 > (1 << 20) else rows
    spec = pl.BlockSpec((tr, cols), lambda i: (i, 0))

    def body(w_ref, g_ref, m_ref, v_ref, go_ref, d_ref, mo_ref, vo_ref):
        gs = g_ref[0].astype(F32)
        for k in range(1, n):
            gs = gs + g_ref[k].astype(F32)
        mn = ADAM_B1 * m_ref[...] + (1.0 - ADAM_B1) * gs
        vn = ADAM_B2 * v_ref[...] + (1.0 - ADAM_B2) * jnp.square(gs)
        m_hat = mn / (1.0 - ADAM_B1 ** ADAM_STEP)
        v_hat = vn / (1.0 - ADAM_B2 ** ADAM_STEP)
        go_ref[...] = gs
        d_ref[...] = -ADAM_LR * (m_hat / (jnp.sqrt(v_hat) + ADAM_EPS) + ADAM_WD * w_ref[...])
        mo_ref[...] = mn
        vo_ref[...] = vn

    outs = pl.pallas_call(
        body, name=name, grid=(rows // tr,),
        in_specs=[spec, pl.BlockSpec((n, tr, cols), lambda i: (0, i, 0)), spec, spec],
        out_specs=[spec] * 4, out_shape=[SDS((rows, cols), F32)] * 4, compiler_params=_params(1),
    )(w.reshape(rows, cols), g.reshape(n, rows, cols), m.reshape(rows, cols), v.reshape(rows, cols))
    return tuple(o.reshape(shape) for o in outs)


def _sum_slots(own, recv, name):
    shape, n, cols = own.shape, recv.shape[0], own.shape[-1]
    own, recv = own.reshape(-1, cols), recv.reshape(n, -1, cols)
    rows = own.shape[0]
    tr = _pick(rows, 512, 16)

    def body(own_ref, r_ref, o_ref):
        acc = own_ref[...].astype(F32)
        for k in range(n):
            acc = acc + r_ref[k].astype(F32)
        o_ref[...] = acc

    return pl.pallas_call(
        body, name=name, grid=(rows // tr,),
        in_specs=[pl.BlockSpec((tr, cols), lambda i: (i, 0)), pl.BlockSpec((n, tr, cols), lambda i: (0, i, 0))],
        out_specs=pl.BlockSpec((tr, cols), lambda i: (i, 0)),
        out_shape=SDS((rows, cols), F32), compiler_params=_params(1),
    )(own, recv).reshape(shape)


def _position():
    return lax.axis_index("x"), lax.axis_index("y"), lax.axis_index("c")


def _peer(k, x, y, c):
    return (1 - x if k & 4 else x, 1 - y if k & 2 else y, 1 - c if k & 1 else c)


def _index(pos):
    return 4 * pos[0] + 2 * pos[1] + pos[2]


def _gather_small(v, name):
    rows, lanes = v.shape

    def body(x_ref, out_ref, send_sems, recv_sems, local_sem):
        me = _position()
        mine = pltpu.make_async_copy(x_ref, out_ref.at[_index(me)], local_sem)
        mine.start()

        def copy(k, slot):
            return pltpu.make_async_remote_copy(
                src_ref=x_ref, dst_ref=out_ref.at[slot], send_sem=send_sems.at[k - 1],
                recv_sem=recv_sems.at[k - 1], device_id=_peer(k, *me), device_id_type=MESH)

        sends = [copy(k, _index(me)) for k in range(1, N_DEV)]
        for cp in sends:
            cp.start()
        for k in range(1, N_DEV):
            copy(k, _index(_peer(k, *me))).wait_recv()
        for cp in sends:
            cp.wait_send()
        mine.wait()

    return pl.pallas_call(
        body, name=name, out_shape=SDS((N_DEV, rows, lanes), v.dtype),
        in_specs=[pl.BlockSpec(memory_space=pltpu.VMEM)],
        out_specs=pl.BlockSpec(memory_space=pltpu.VMEM),
        scratch_shapes=[pltpu.SemaphoreType.DMA((N_DEV - 1,)), pltpu.SemaphoreType.DMA((N_DEV - 1,)),
                        pltpu.SemaphoreType.DMA],
        compiler_params=pltpu.CompilerParams(vmem_limit_bytes=VMEM_LIMIT_V7X),
    )(v)


def _gather_big(v, name):
    rows, cols = v.shape

    def body(x_ref, out_ref, send_sems, recv_sems, local_sem):
        x, y, c = _position()
        me, sibling = (x, y, c), (x, y, 1 - c)
        chips = [(1 - x, y), (x, 1 - y), (1 - x, 1 - y)]

        def copy(k, block, to, src=None):
            slot = out_ref.at[_index(block)]
            return pltpu.make_async_remote_copy(
                src_ref=slot if src is None else src, dst_ref=slot, send_sem=send_sems.at[k],
                recv_sem=recv_sems.at[k], device_id=to, device_id_type=MESH)

        mine = pltpu.make_async_copy(x_ref, out_ref.at[_index(me)], local_sem)
        mine.start()
        first = [copy(0, me, sibling, src=x_ref)]
        first += [copy(1 + j, me, (*chip, c), src=x_ref) for j, chip in enumerate(chips)]
        for cp in first:
            cp.start()
        passed = [copy(4 + j, (*chip, c), sibling) for j, chip in enumerate(chips)]
        for j, chip in enumerate(chips):
            copy(1 + j, (*chip, c), me).wait_recv()
            passed[j].start()
        copy(0, sibling, me).wait_recv()
        for j, chip in enumerate(chips):
            copy(4 + j, (*chip, 1 - c), me).wait_recv()
        for cp in first + passed:
            cp.wait_send()
        mine.wait()

    return pl.pallas_call(
        body, name=name, out_shape=SDS((N_DEV, rows, cols), v.dtype),
        in_specs=[pl.BlockSpec(memory_space=pl.ANY)],
        out_specs=pl.BlockSpec(memory_space=pl.ANY),
        scratch_shapes=[pltpu.SemaphoreType.DMA((N_DEV - 1,)), pltpu.SemaphoreType.DMA((N_DEV - 1,)),
                        pltpu.SemaphoreType.DMA],
    )(v)


HBM_SPEC = pl.BlockSpec(memory_space=pltpu.HBM)
SEM_SPEC = pl.BlockSpec(memory_space=pltpu.SEMAPHORE)
SPLIT_EFFECT = pltpu.SideEffectType.DATAFLOW_SIDE_EFFECTING


def _split_start(srcs, gather, name):
    n = len(srcs)
    lands = [jnp.zeros(((N_DEV,) + s.shape) if gather else s.shape, s.dtype) for s in srcs]

    def body(*refs):
        src_refs, land_refs, sems, token = refs[:n], refs[n:2 * n], refs[2 * n:4 * n], refs[-1]
        me = _position()
        for a in range(n):
            for k in range(1, N_DEV):
                peer = _peer(k, *me)
                pltpu.make_async_remote_copy(
                    src_ref=src_refs[a] if gather else src_refs[a].at[_index(peer)],
                    dst_ref=land_refs[a].at[_index(me)], send_sem=sems[2 * a], recv_sem=sems[2 * a + 1],
                    device_id=peer, device_id_type=MESH).start()
        token[...] = jnp.zeros_like(token)

    hbm = lambda arrays: tuple(pltpu.HBM(a.shape, a.dtype) for a in arrays)
    outs = pl.pallas_call(
        body, name=name,
        out_shape=(pltpu.SemaphoreType.DMA(()),) * (2 * n) + hbm(srcs) + hbm(lands) + (SDS((8, 128), F32),),
        in_specs=(HBM_SPEC,) * (2 * n),
        out_specs=(SEM_SPEC,) * (2 * n) + (HBM_SPEC,) * (2 * n) + (pl.BlockSpec(memory_space=pltpu.VMEM),),
        input_output_aliases={a: 2 * n + a for a in range(2 * n)},
        compiler_params=pltpu.CompilerParams(has_side_effects=SPLIT_EFFECT),
    )(*[pltpu.with_memory_space_constraint(a, pltpu.HBM) for a in list(srcs) + lands])
    return outs[:2 * n], outs[2 * n:3 * n], outs[3 * n:4 * n], outs[-1]


def _split_wait(flight, after, name):
    sems, srcs, lands, _ = flight
    n = len(srcs)

    def body(*refs):
        land_refs, sem_refs = refs[n:2 * n], refs[2 * n:4 * n]
        me = _position()
        for a in range(n):
            seven = land_refs[a].at[pl.ds(0, N_DEV - 1)]
            copies = pltpu.make_async_remote_copy(
                src_ref=seven, dst_ref=seven, send_sem=sem_refs[2 * a], recv_sem=sem_refs[2 * a + 1],
                device_id=_peer(1, *me), device_id_type=MESH)
            copies.wait_send()
            copies.wait_recv()

    outs = pl.pallas_call(
        body, name=name,
        out_shape=tuple(pltpu.HBM(a.shape, a.dtype) for a in list(srcs) + list(lands)),
        in_specs=(HBM_SPEC,) * (2 * n) + (SEM_SPEC,) * (2 * n) + (pl.BlockSpec(memory_space=pl.ANY),),
        out_specs=(HBM_SPEC,) * (2 * n), input_output_aliases={a: a for a in range(2 * n)},
        compiler_params=pltpu.CompilerParams(has_side_effects=SPLIT_EFFECT),
    )(*srcs, *lands, *sems, after)
    return outs[:n], outs[n:]


def _pack_rows(arrays, lanes, dtype):
    flat = jnp.concatenate([a.astype(dtype).reshape(-1) for a in arrays])
    pad = (-flat.size) % (16 * lanes)
    if pad:
        flat = jnp.concatenate([flat, jnp.zeros((pad,), dtype)])
    return flat.reshape(-1, lanes)


def _unpack_rows(packed, shapes):
    n = packed.shape[0]
    flat = packed.reshape(n, -1)
    out, off = [], 0
    for shp in shapes:
        size = math.prod(shp)
        out.append(flat[:, off:off + size].reshape((n,) + tuple(shp)))
        off += size
    return out


def _unshard(g8, axis):
    moved = jnp.moveaxis(g8, 0, axis)
    shp = list(moved.shape)
    shp[axis:axis + 2] = [shp[axis] * shp[axis + 1]]
    return moved.reshape(shp)


def _split8(full, axis):
    shp = list(full.shape)
    shp[axis:axis + 1] = [N_DEV, shp[axis] // N_DEV]
    return jnp.moveaxis(full.reshape(shp), axis, 0)


def _my_shard(g, axis, me):
    size = g.shape[axis + 1] // N_DEV
    return lax.dynamic_slice_in_dim(g, me * size, size, axis=axis + 1)


BIG_WEIGHTS = ("ffn_w_up", "ffn_w_down", "attn_w_qkv", "attn_w_o", "ret_w_in", "ret_w_out", "pool_w")
LAYER_WEIGHTS = (
    (("ffn_w_up", 0, "cols"), ("ffn_w_down", 0, "rows"), ("pool_w", 0, "pool")),
    (("ffn_w_up", 1, "cols"), ("ffn_w_down", 1, "rows"), ("attn_w_qkv", 0, "cols"), ("attn_w_o", 0, "rows")),
    (("ffn_w_up", 2, "cols"), ("ffn_w_down", 2, "rows"), ("ret_w_in", 0, "cols"), ("ret_w_out", 0, "rows")),
    (("ffn_w_up", 3, "cols"), ("ffn_w_down", 3, "rows"), ("pool_w", 1, "pool")),
)


GRAD_GROUPS = {"3": LAYER_WEIGHTS[3], "2": LAYER_WEIGHTS[2], "1": LAYER_WEIGHTS[1],
               "0ffn": LAYER_WEIGHTS[0][:2], "0mix": LAYER_WEIGHTS[0][2:]}


def _shard_to_send(w, kind):
    w = w.astype(BF16)
    return w.T if kind == "cols" else w


def _full_from_land(land, kind):
    return _unshard(land, 1) if kind == "pool" else land.reshape(-1, land.shape[-1])


def _grad_to_send(g, kind):
    return _split8(g, 1).astype(BF16) if kind == "pool" else g.astype(BF16).reshape(N_DEV, -1, g.shape[-1])


def _shard_grad(gsum, kind):
    return gsum.T if kind == "cols" else gsum
SMALL_SHARDED = (("norm_w", 2), ("pool_b", 1), ("pool_scale", 1), ("ret_gn_w", 1), ("ffn_conv_w", 2))
REPLICATED = ("ada_b", "attn_q_gain", "attn_k_gain", "ret_decay_logit", "ffn_conv_b")
WEIGHT_ORDER = ("c_ctx", "ada_w", "ada_b", "norm_w", "pool_w", "pool_b", "pool_scale", "attn_w_qkv",
                "attn_q_gain", "attn_k_gain", "attn_w_o", "ret_w_in", "ret_decay_logit", "ret_gn_w",
                "ret_w_out", "ffn_w_up", "ffn_conv_w", "ffn_conv_b", "ffn_w_down")


def _local_step(x0, target, mods, P, get_weights, put_grads, s_len, l_len):
    n_lat = s_len // ROW_TILE
    nw = P["norm_w"]
    lgt = jnp.broadcast_to(P["ret_decay_logit"][0][:, :, None, None], (2, RET_HEADS, 1, 128))
    cos, sin = _rope_tables(s_len, l_len)
    h_dtype = [F32 if i % 3 == 0 else BF16 for i in range(DEPTH)]
    saved = []
    mods = list(mods)
    X = x0
    h = _res_norm(X, None, None, 0, nw[0, 0], mods[0], 0, h_dtype[0], n_lat, "norm_first")
    for i in range(DEPTH):
        kind, j, mod = i % 3, i // 3, mods[i]
        W = get_weights(i, X)
        sv = {"X": X, "h": h, "W": W}
        if kind == 0:
            y = _pool_fwd(h, W["pool_w"], P["pool_b"][j:j + 1], P["pool_scale"][j:j + 1],
                          n_lat, s_len, l_len, f"pool_fwd{i}")
        elif kind == 1:
            qkv = _mm(h, W["attn_w_qkv"], "nt", F32, f"qkv{i}")
            q, k, v = _qk_prep_fwd(qkv, P["attn_q_gain"][j:j + 1], P["attn_k_gain"][j:j + 1], cos, sin)
            o, lse = _flash_fwd(q, k, v, s_len, l_len)
            y = _mm(o, W["attn_w_o"], "nn", F32, f"attn_out{i}")
            sv.update(qkv=qkv, q=q, k=k, v=v, o=o, lse=lse)
        else:
            proj = _mm(h, W["ret_w_in"], "nt", F32, f"ret_in{i}")
            o2, states = _ret_fwd(proj, lgt, s_len, l_len)
            gated = _readout_fwd(o2, proj, P["ret_gn_w"][j:j + 1])
            y = _mm(gated, W["ret_w_out"], "nn", F32, f"ret_out{i}")
            sv.update(proj=proj, o2=o2, states=states, gated=gated)
        X1, h2 = _res_norm(X, y, mod, 0, nw[i, 1], mod, 1, BF16, n_lat, f"res_norm_mid{i}")
        u = _mm(h2, W["ffn_w_up"], "nt", FFN_HIDDEN_DTYPE, f"ffn_up{i}")
        gact = _conv_gate_fwd(u, P["ffn_conv_w"][i], P["ffn_conv_b"][i:i + 1], n_lat, f"conv_gate_fwd{i}")
        f = _mm(gact, W["ffn_w_down"], "nn", F32, f"ffn_down{i}")
        sv.update(y=y, X1=X1, h2=h2, u=u, gact=gact, f=f)
        saved.append(sv)
        if i + 1 < DEPTH:
            X, h = _res_norm(X1, f, mod, 1, nw[i + 1, 0], mods[i + 1], 0, h_dtype[i + 1], n_lat,
                             f"res_norm_end{i}")
        else:
            X = _res_norm(X1, f, mod, 1, None, None, 0, None, n_lat, "res_last")

    dX, loss = _loss_bwd(X, target, n_lat)
    G = {name: [None] * P[name].shape[0] for name in
         ("pool_b", "pool_scale", "attn_q_gain", "attn_k_gain", "ret_decay_logit", "ret_gn_w", "ffn_conv_w",
          "ffn_conv_b")}
    dnw = [[None, None] for _ in range(DEPTH)]
    dmods = [None] * DEPTH
    for i in reversed(range(DEPTH)):
        kind, j, mod, sv = i % 3, i // 3, mods[i], saved[i]
        W, gl = sv["W"], {}
        if i == DEPTH - 1:
            df, dg2 = _gate_bwd(dX, sv["f"], mod, 1, BF16, n_lat, f"gate_bwd_ffn{i}")
        dgact = _mm(df, W["ffn_w_down"], "nt", FFN_HIDDEN_DTYPE, f"ffn_down_dx{i}")
        gl["ffn_w_down"] = _mm(sv["gact"], df, "tn", BF16, f"ffn_down_dw{i}")
        du, dcw, dcb = _conv_gate_bwd(sv["u"], dgact, P["ffn_conv_w"][i], P["ffn_conv_b"][i:i + 1], n_lat,
                                      f"conv_gate_bwd{i}")
        G["ffn_conv_w"][i], G["ffn_conv_b"][i] = dcw, dcb[0]
        dh2 = _mm(du, W["ffn_w_up"], "nn", F32, f"ffn_up_dx{i}")
        gl["ffn_w_up"] = _mm(du, sv["h2"], "tn", BF16, f"ffn_up_dw{i}")
        if i == 0:
            mod = mod + put_grads("0ffn", gl)
        dX1, dnw[i][1], dsh2, dsc2, dy, dg1 = _norm_bwd(
            dX, dh2, sv["X1"], nw[i, 1], mod, 1, n_lat, f"norm_bwd_ffn{i}",
            gated=(sv["y"], mod, 0, F32 if kind == 0 else BF16))
        h = sv["h"]
        if kind == 0:
            dh, dpw, dpb, dps = _pool_bwd(h, dy, W["pool_w"], P["pool_b"][j:j + 1], P["pool_scale"][j:j + 1],
                                          n_lat, s_len, l_len, f"pool_bwd{i}")
            gl["pool_w"], G["pool_b"][j], G["pool_scale"][j] = dpw, dpb[0], dps[0]
        elif kind == 1:
            do = _mm(dy, W["attn_w_o"], "nt", F32, f"attn_out_dx{i}")
            gl["attn_w_o"] = _mm(sv["o"], dy, "tn", BF16, f"attn_out_dw{i}")
            dq, dk, dv = _flash_bwd(sv["q"], sv["k"], sv["v"], sv["o"], sv["lse"], do, s_len, l_len)
            dqkv, dqg, dkg = _qk_prep_bwd(sv["qkv"], dq, dk, dv, P["attn_q_gain"][j:j + 1],
                                          P["attn_k_gain"][j:j + 1], cos, sin)
            G["attn_q_gain"][j], G["attn_k_gain"][j] = dqg[0], dkg[0]
            dh = _mm(dqkv, W["attn_w_qkv"], "nn", F32, f"qkv_dx{i}")
            gl["attn_w_qkv"] = _mm(dqkv, h, "tn", BF16, f"qkv_dw{i}")
        else:
            dgated = _mm(dy, W["ret_w_out"], "nt", F32, f"ret_out_dx{i}")
            gl["ret_w_out"] = _mm(sv["gated"], dy, "tn", BF16, f"ret_out_dw{i}")
            do, dg, dgn = _readout_bwd(sv["o2"], sv["proj"], P["ret_gn_w"][j:j + 1], dgated)
            dq2, dk2, dv2, dlg = _ret_bwd(sv["proj"], lgt, sv["states"], do, s_len, l_len)
            dproj = _ret_dproj(dq2, dk2, dv2, dg)
            G["ret_gn_w"][j], G["ret_decay_logit"][j] = dgn[0], dlg[:, :, 0, 0]
            dh = _mm(dproj, W["ret_w_in"], "nn", F32, f"ret_in_dx{i}")
            gl["ret_w_in"] = _mm(dproj, h, "tn", BF16, f"ret_in_dw{i}")
        zero = put_grads(str(i) if i > 0 else "0mix", gl)
        if i > 0:
            mods[i - 1] = mods[i - 1] + zero
            dX, dnw[i][0], dsh1, dsc1, df_below, dg2_below = _norm_bwd(
                dX1, dh, sv["X"], nw[i, 0], mod, 0, n_lat, f"norm_bwd_mix{i}",
                gated=(saved[i - 1]["f"], mods[i - 1], 1, BF16))
        else:
            dX, dnw[i][0], dsh1, dsc1 = _norm_bwd(dX1, dh, sv["X"], nw[i, 0], mod, 0, n_lat, f"norm_bwd_mix{i}")
        dmods[i] = jnp.concatenate([dsh1, dsc1, dg1, dsh2, dsc2, dg2], axis=1)
        if i > 0:
            df, dg2 = df_below, dg2_below
    grads = {name: jnp.stack(parts) for name, parts in G.items()}
    grads["norm_w"] = jnp.stack([jnp.concatenate(pair, axis=0) for pair in dnw])
    return loss, dX, grads, jnp.stack(dmods)


def kernel(x, c, ctx, c_ctx, ada_w, ada_b, norm_w, pool_w, pool_b, pool_scale, attn_w_qkv, attn_q_gain,
           attn_k_gain, attn_w_o, ret_w_in, ret_decay_logit, ret_gn_w, ret_w_out, ffn_w_up, ffn_conv_w,
           ffn_conv_b, ffn_w_down, loss_target, m_c_ctx, m_ada_w, m_ada_b, m_norm_w, m_pool_w, m_pool_b,
           m_pool_scale, m_attn_w_qkv, m_attn_q_gain, m_attn_k_gain, m_attn_w_o, m_ret_w_in,
           m_ret_decay_logit, m_ret_gn_w, m_ret_w_out, m_ffn_w_up, m_ffn_conv_w, m_ffn_conv_b, m_ffn_w_down,
           v_c_ctx, v_ada_w, v_ada_b, v_norm_w, v_pool_w, v_pool_b, v_pool_scale, v_attn_w_qkv, v_attn_q_gain,
           v_attn_k_gain, v_attn_w_o, v_ret_w_in, v_ret_decay_logit, v_ret_gn_w, v_ret_w_out, v_ffn_w_up,
           v_ffn_conv_w, v_ffn_conv_b, v_ffn_w_down):
    A = dict(locals())
    me = _index(_position())
    s_len, D = x.shape[1], x.shape[2]
    l_len = ctx.shape[1]
    assert s_len % ROW_TILE == 0 and l_len % ROW_TILE == 0 and s_len % GRID_W == 0

    small = [A[n] for n, _ in SMALL_SHARDED]
    got = _gather_small(_pack_rows([c] + small, 128, F32), "gather_c_small")
    parts = _unpack_rows(got, [c.shape] + [a.shape for a in small])
    c_all = parts[0].reshape(N_DEV, D)
    P = {n: _unshard(g8, ax) for (n, ax), g8 in zip(SMALL_SHARDED, parts[1:])}

    c_rows = jnp.concatenate([c_all, c_ctx.reshape(1, D), jnp.zeros((7, D), F32)], axis=0)
    cols = ada_w.shape[2]
    ada_b_shard = lax.dynamic_slice_in_dim(ada_b, me * cols, cols, axis=1).reshape(DEPTH, 1, cols)
    mod_shard = _ada_fwd(c_rows, ada_w, ada_b_shard)
    got = _gather_small(mod_shard.reshape(-1, 128), "gather_mod").reshape(N_DEV, DEPTH, 16, cols)
    mod_lat = lax.dynamic_index_in_dim(got, me, axis=2, keepdims=False)
    mod_ctx = got[:, :, 8, :]
    mods = jnp.stack([jnp.moveaxis(mod_lat, 0, 1).reshape(DEPTH, 6, D),
                      jnp.moveaxis(mod_ctx, 0, 1).reshape(DEPTH, 6, D)], axis=1)

    shards = [[_shard_to_send(A[n][j], kind) for n, j, kind in lw] for lw in LAYER_WEIGHTS]
    pack0 = jnp.concatenate([s.reshape(-1, D) for s in shards[0]], axis=0)
    got0 = _gather_big(pack0, "gather_weights0")
    first, off = {}, 0
    for (n, j, kind), s in zip(LAYER_WEIGHTS[0], shards[0]):
        r = s.size // D
        first[n] = _full_from_land(got0[:, off:off + r].reshape((N_DEV,) + s.shape), kind)
        off += r
    flights, zero = {}, jnp.zeros((), F32)
    for i in range(1, DEPTH):
        flights[i] = _split_start(shards[i], True, f"gather_start{i}")
        zero = zero + flights[i][3][0, 0]
    mods = [mods[i] for i in range(DEPTH)]
    mods[0] = mods[0] + zero
    for n in REPLICATED:
        P[n] = A[n]

    def get_weights(i, x_now):
        if i == 0:
            return first
        owns, lands = _split_wait(flights[i], x_now, f"gather_wait{i}")
        return {n: _full_from_land(lax.dynamic_update_index_in_dim(land, own, me, axis=0), kind)
                for (n, j, kind), own, land in zip(LAYER_WEIGHTS[i], owns, lands)}

    sent = {}

    def put_grads(group, gl):
        sent[group] = _split_start([_grad_to_send(gl[n], kind) for n, j, kind in GRAD_GROUPS[group]], False,
                                   f"exchange_start_{group}")
        return sent[group][3][0, 0]

    x0 = jnp.concatenate([x[0], ctx[0]], axis=0)
    loss8, dx0, G, dmods = _local_step(x0, loss_target[0], mods, P, get_weights, put_grads, s_len, l_len)
    loss = lax.psum(loss8[0, 0], ("x", "y", "c"))
    grad_x = dx0[:s_len].reshape(x.shape)

    small_names = ["dmods"] + list(REPLICATED[1:]) + [n for n, _ in SMALL_SHARDED]
    small_parts = [dmods] + [G[n] for n in small_names[1:]]
    got = _gather_small(_pack_rows(small_parts, 128, F32), "gather_small_grads")
    S8 = dict(zip(small_names, _unpack_rows(got, [a.shape for a in small_parts])))

    dm = S8["dmods"].reshape(N_DEV, DEPTH, 2, 6 * D)
    dm_mine = lax.dynamic_slice_in_dim(dm, me * cols, cols, axis=3)
    g_ada_w, pc = _ada_bwd(c_rows, ada_w, jnp.moveaxis(dm_mine[:, :, 0], 0, 1), jnp.moveaxis(dm_mine[:, :, 1], 0, 1))
    pc8 = _gather_small(pc.reshape(-1, 128), "gather_c_ctx_grad").reshape(N_DEV, 8, D)

    def owner_sums(group, after):
        sends, lands = _split_wait(sent[group], after, f"exchange_wait_{group}")
        out = {}
        for (n, j, kind), send, land in zip(GRAD_GROUPS[group], sends, lands):
            own = lax.dynamic_index_in_dim(send, me, axis=0, keepdims=False)
            out[(n, j)] = _shard_grad(_sum_slots(own, land, f"sum_slots_{n}{j}"), kind)
        return out

    shard_grads = {}
    for group in ("3", "2", "1", "0mix"):
        shard_grads.update(owner_sums(group, pc8))

    g_in = {"c_ctx": pc8[:, 0, :], "ada_w": g_ada_w[None],
            "ada_b": jnp.moveaxis(dm, 2, 1).reshape(2 * N_DEV, DEPTH, 6 * D)}
    for n in REPLICATED[1:]:
        g_in[n] = S8[n]
    for n, ax in SMALL_SHARDED:
        g_in[n] = _my_shard(S8[n], ax, me)

    def stacked(n):
        return jnp.stack([shard_grads[(n, j)] for j in range(A[n].shape[0])])[None]

    late = [n for n, j, kind in GRAD_GROUPS["0ffn"]]
    for n in BIG_WEIGHTS:
        if n not in late:
            g_in[n] = stacked(n)
    res = {n: _adamw(A[n], g_in[n], A["m_" + n], A["v_" + n], "adamw_" + n) for n in WEIGHT_ORDER if n not in late}
    done = sum(res[n][1].reshape(-1)[0] for n in res)
    shard_grads.update(owner_sums("0ffn", done.reshape(1, 1)))
    for n in late:
        res[n] = _adamw(A[n], stacked(n), A["m_" + n], A["v_" + n], "adamw_" + n)
    outs = [loss, grad_x]
    for slot in range(4):
        outs += [res[n][slot] for n in WEIGHT_ORDER]
    return tuple(outs)
```

```python
import functools
import math

import jax
import jax.numpy as jnp
from jax import lax
from jax.experimental import pallas as pl
from jax.experimental.pallas import tpu as pltpu

F32 = jnp.float32
BF16 = jnp.bfloat16
SDS = jax.ShapeDtypeStruct
MESH = pl.DeviceIdType.MESH

N_DEV = 8
EPS = 1e-6
DEPTH = 4
GRID_W = 64
POOL_WINDOWS = (2, 4, 8, 16)
N_HEADS = 8
N_KV = 2
HEAD_DIM = 128
ROPE_THETA = 10000.0
RET_HEADS = 4
RET_DK = 256
RET_DV = 512
RET_CHUNK = 128
ADAM_LR = 0.001
ADAM_B1 = 0.9
ADAM_B2 = 0.999
ADAM_EPS = 1e-08
ADAM_WD = 0.01
ADAM_STEP = 10

ROW_TILE = 256
FFN_HIDDEN_DTYPE = BF16
FLASH_FWD_TILE = 128
HALO = 8
VMEM_LIMIT_V7X = 56 * 1024 * 1024


def _params(n_axes=0):
    sem = ("arbitrary",) * n_axes if n_axes else None
    return pltpu.CompilerParams(dimension_semantics=sem, vmem_limit_bytes=VMEM_LIMIT_V7X)


def _pick(n, cap, mult):
    best = None
    for d in range(mult, min(n, cap) + 1, mult):
        if n % d == 0:
            best = d
    return best if best is not None else n


def _dot(a, b):
    return jnp.dot(a, b, preferred_element_type=F32)


def _dot_nt(a, b):
    return lax.dot_general(a, b, (((1,), (1,)), ((), ())), preferred_element_type=F32)


def _dot_tn(a, b):
    return lax.dot_general(a, b, (((0,), (0,)), ((), ())), preferred_element_type=F32)


def _bf(v):
    return v.astype(BF16)


def _sigmoid(v):
    return 0.5 * jnp.tanh(0.5 * v) + 0.5


MM_VMEM_BUDGET = 40 * 1024 * 1024
MM_STEP_BYTES = 1 << 20
MM_ACC_PASS_BYTES = 8


def _divisors(n, mult, cap):
    return [d for d in range(mult, min(n, cap) + 1, mult) if n % d == 0] or [n]


def _mm_tiles(mode, M, N, K, a_item, b_item, o_item):
    best = None
    for tm in _divisors(M, 128 if mode == "tn" else 16, 2816):
        for tn in _divisors(N, 128, 2048):
            for tk in _divisors(K, 16 if mode == "tn" else 128, 2816):
                ni, nj, nk = M // tm, N // tn, K // tk
                vmem = 2 * (tm * tk * a_item + tk * tn * b_item + tm * tn * o_item) + tm * tn * 4
                if vmem > MM_VMEM_BUDGET:
                    continue
                a_reads = 1 if nk == 1 else nj
                b_reads = 1 if (nk == 1 and nj == 1) else ni
                cost = (M * K * a_item * a_reads + K * N * b_item * b_reads + M * N * o_item
                        + ni * nj * nk * MM_STEP_BYTES + (nk - 1) * M * N * MM_ACC_PASS_BYTES)
                if best is None or cost < best[0]:
                    best = (cost, tm, tn, tk)
    return best[1:]


def _mm(a, b, mode, out_dtype, name):
    if mode == "nn":
        (M, K), (K2, N) = a.shape, b.shape
    elif mode == "nt":
        (M, K), (N, K2) = a.shape, b.shape
    else:
        (K, M), (K2, N) = a.shape, b.shape
    assert K == K2, (a.shape, b.shape, mode)
    tm, tn, tk = _mm_tiles(mode, M, N, K, a.dtype.itemsize, b.dtype.itemsize, jnp.dtype(out_dtype).itemsize)
    nk = K // tk
    if mode == "nn":
        a_spec = pl.BlockSpec((tm, tk), lambda i, j, k: (i, k))
        b_spec = pl.BlockSpec((tk, tn), lambda i, j, k: (k, j))
    elif mode == "nt":
        a_spec = pl.BlockSpec((tm, tk), lambda i, j, k: (i, k))
        b_spec = pl.BlockSpec((tn, tk), lambda i, j, k: (j, k))
    else:
        a_spec = pl.BlockSpec((tk, tm), lambda i, j, k: (k, i))
        b_spec = pl.BlockSpec((tk, tn), lambda i, j, k: (k, j))
    dot = {"nn": _dot, "nt": _dot_nt, "tn": _dot_tn}[mode]

    def body(a_ref, b_ref, o_ref, acc_ref):
        part = dot(_bf(a_ref[...]), _bf(b_ref[...]))
        if nk == 1:
            o_ref[...] = part.astype(out_dtype)
        else:
            k = pl.program_id(2)

            @pl.when(k == 0)
            def _():
                acc_ref[...] = part

            @pl.when(k > 0)
            def _():
                acc_ref[...] += part

            @pl.when(k == nk - 1)
            def _():
                o_ref[...] = acc_ref[...].astype(out_dtype)

    return pl.pallas_call(
        body, name=name, grid=(M // tm, N // tn, nk),
        in_specs=[a_spec, b_spec],
        out_specs=pl.BlockSpec((tm, tn), lambda i, j, k: (i, j)),
        out_shape=SDS((M, N), out_dtype),
        scratch_shapes=[pltpu.VMEM((tm, tn), F32)],
        compiler_params=_params(3),
    )(a, b)


def _seg_spec(n_lat, d):
    return pl.BlockSpec((1, 6, d), lambda i: ((i >= n_lat).astype(jnp.int32), 0, 0))


def _seg_acc_spec(n_lat, d):
    return pl.BlockSpec((1, 1, d), lambda i: ((i >= n_lat).astype(jnp.int32), 0, 0))


def _res_norm(x, y, gmod, gk, nw, nmod, nk, h_dtype, n_lat, name):
    R, D = x.shape
    has_res, has_norm = y is not None, nw is not None
    row = pl.BlockSpec((ROW_TILE, D), lambda i: (i, 0))
    vec = pl.BlockSpec((1, D), lambda i: (0, 0))
    ins, specs, outs, ospecs = [x], [row], [], []
    if has_res:
        ins += [y, gmod]
        specs += [row, _seg_spec(n_lat, D)]
        outs.append(SDS((R, D), F32))
        ospecs.append(row)
    if has_norm:
        ins += [nw.reshape(1, D), nmod]
        specs += [vec, _seg_spec(n_lat, D)]
        outs.append(SDS((R, D), h_dtype))
        ospecs.append(row)

    def body(*refs):
        refs = list(refs)
        z = refs.pop(0)[...]
        if has_res:
            y_ref, g_ref = refs.pop(0), refs.pop(0)
            z = z + g_ref[0, pl.ds(3 * gk + 2, 1), :] * y_ref[...].astype(F32)
        if has_norm:
            nw_ref, m_ref = refs.pop(0), refs.pop(0)
        if has_res:
            refs.pop(0)[...] = z
        if has_norm:
            r = lax.rsqrt(jnp.mean(z * z, axis=-1, keepdims=True) + EPS)
            h = (z * r) * nw_ref[...]
            h = h * (1.0 + m_ref[0, pl.ds(3 * nk + 1, 1), :]) + m_ref[0, pl.ds(3 * nk, 1), :]
            refs.pop(0)[...] = h.astype(h_dtype)

    res = pl.pallas_call(
        body, name=name, grid=(R // ROW_TILE,), in_specs=specs, out_specs=ospecs,
        out_shape=outs, compiler_params=_params(1),
    )(*ins)
    return res if len(res) > 1 else res[0]


def _gate_bwd(dz, y, mod, k, out_dtype, n_lat, name):
    R, D = dz.shape
    row = pl.BlockSpec((ROW_TILE, D), lambda i: (i, 0))

    def body(dz_ref, y_ref, m_ref, dy_ref, dg_ref):
        i = pl.program_id(0)
        dzv = dz_ref[...]
        dy_ref[...] = (m_ref[0, pl.ds(3 * k + 2, 1), :] * dzv).astype(out_dtype)

        @pl.when((i == 0) | (i == n_lat))
        def _():
            dg_ref[...] = jnp.zeros_like(dg_ref)

        dg_ref[0] += jnp.sum(dzv * y_ref[...].astype(F32), axis=0, keepdims=True)

    return pl.pallas_call(
        body, name=name, grid=(R // ROW_TILE,),
        in_specs=[row, row, _seg_spec(n_lat, D)],
        out_specs=[row, _seg_acc_spec(n_lat, D)],
        out_shape=[SDS((R, D), out_dtype), SDS((2, 1, D), F32)],
        compiler_params=_params(1),
    )(dz, y, mod)


def _norm_bwd(dz, dh, x, nw, mod, k, n_lat, name, gated=None):
    R, D = x.shape
    row = pl.BlockSpec((ROW_TILE, D), lambda i: (i, 0))
    vec = pl.BlockSpec((1, D), lambda i: (0, 0))
    ins, specs = [dz, dh, x, nw.reshape(1, D), mod], [row, row, row, vec, _seg_spec(n_lat, D)]
    outs = [SDS((R, D), F32), SDS((1, D), F32), SDS((2, 1, D), F32), SDS((2, 1, D), F32)]
    ospecs = [row, vec, _seg_acc_spec(n_lat, D), _seg_acc_spec(n_lat, D)]
    if gated is not None:
        y, gmod, gk, dy_dtype = gated
        ins += [y, gmod]
        specs += [row, _seg_spec(n_lat, D)]
        outs += [SDS((R, D), dy_dtype), SDS((2, 1, D), F32)]
        ospecs += [row, _seg_acc_spec(n_lat, D)]

    def body(dz_ref, dh_ref, x_ref, nw_ref, m_ref, *rest):
        if gated is not None:
            y_ref, g_ref, dx_ref, dnw_ref, dsh_ref, dsc_ref, dy_ref, dg_ref = rest
        else:
            dx_ref, dnw_ref, dsh_ref, dsc_ref = rest
        i = pl.program_id(0)
        xv = x_ref[...]
        dhv = dh_ref[...].astype(F32)
        nwv = nw_ref[...]
        sc1 = 1.0 + m_ref[0, pl.ds(3 * k + 1, 1), :]
        r = lax.rsqrt(jnp.mean(xv * xv, axis=-1, keepdims=True) + EPS)
        xhat = xv * r
        a = dhv * (nwv * sc1)
        dx = dz_ref[...] + r * (a - xhat * jnp.mean(a * xhat, axis=-1, keepdims=True))
        dx_ref[...] = dx

        @pl.when(i == 0)
        def _():
            dnw_ref[...] = jnp.zeros_like(dnw_ref)

        @pl.when((i == 0) | (i == n_lat))
        def _():
            dsh_ref[...] = jnp.zeros_like(dsh_ref)
            dsc_ref[...] = jnp.zeros_like(dsc_ref)
            if gated is not None:
                dg_ref[...] = jnp.zeros_like(dg_ref)

        dnw_ref[...] += jnp.sum(dhv * xhat, axis=0, keepdims=True) * sc1
        dsh_ref[0] += jnp.sum(dhv, axis=0, keepdims=True)
        dsc_ref[0] += jnp.sum(dhv * xhat, axis=0, keepdims=True) * nwv
        if gated is not None:
            dy_ref[...] = (g_ref[0, pl.ds(3 * gk + 2, 1), :] * dx).astype(dy_dtype)
            dg_ref[0] += jnp.sum(dx * y_ref[...].astype(F32), axis=0, keepdims=True)

    return pl.pallas_call(
        body, name=name, grid=(R // ROW_TILE,), in_specs=specs, out_specs=ospecs, out_shape=outs,
        compiler_params=_params(1),
    )(*ins)


def _loss_bwd(xf, target, n_lat):
    R, D = xf.shape
    row = pl.BlockSpec((ROW_TILE, D), lambda i: (i, 0))
    tgt = pl.BlockSpec((ROW_TILE, D), lambda i: (jnp.minimum(i, n_lat - 1), 0))

    def body(x_ref, t_ref, dx_ref, loss_ref):
        i = pl.program_id(0)
        e = jnp.where(i < n_lat, x_ref[...] - t_ref[...], 0.0)
        dx_ref[...] = e * (1.0 / D)

        @pl.when(i == 0)
        def _():
            loss_ref[...] = jnp.zeros_like(loss_ref)

        loss_ref[...] += 0.5 * jnp.sum(jnp.mean(e * e, axis=-1, keepdims=True))

    return pl.pallas_call(
        body, name="loss_bwd", grid=(R // ROW_TILE,),
        in_specs=[row, tgt],
        out_specs=[row, pl.BlockSpec((8, 128), lambda i: (0, 0))],
        out_shape=[SDS((R, D), F32), SDS((8, 128), F32)],
        compiler_params=_params(1),
    )(xf, target)


def _halo_rows(dtype):
    return HALO * (4 // jnp.dtype(dtype).itemsize)


def _halo_specs(n_tiles, width, tile=ROW_TILE, rows=HALO):
    per = tile // rows
    prev = pl.BlockSpec((rows, width), lambda i: (jnp.maximum(i * per - 1, 0), 0))
    nxt = pl.BlockSpec((rows, width), lambda i: (jnp.minimum((i + 1) * per, n_tiles * per - 1), 0))
    return prev, nxt


SHIFT_K = 256


def _shift_matrix(n_out, first_row, deltas):
    half = n_out // 2
    out = []
    for h, start in enumerate((0, 2 * _halo_rows(BF16))):
        r = lax.broadcasted_iota(jnp.int32, (half, SHIFT_K), 0) + (first_row + h * half - start)
        j = lax.broadcasted_iota(jnp.int32, (half, SHIFT_K), 1)
        out.append(jnp.concatenate([(j == r + d).astype(F32) for d in deltas], axis=0).astype(BF16))
    return out


def _shifted_rows(t_ref, p_ref, n_ref, cols, first, last, picks, n_blocks):
    pr = jnp.where(first, jnp.zeros_like(p_ref[:, cols]), p_ref[:, cols])
    nx = jnp.where(last, jnp.zeros_like(n_ref[:, cols]), n_ref[:, cols])
    e = jnp.concatenate([pr, t_ref[:, cols], nx], axis=0)
    start = 2 * pr.shape[0]
    top, bot = _dot(picks[0], e[0:SHIFT_K]), _dot(picks[1], e[start:start + SHIFT_K])
    half = picks[0].shape[0] // n_blocks
    return [jnp.concatenate([top[k * half:(k + 1) * half], bot[k * half:(k + 1) * half]], axis=0)
            for k in range(n_blocks)]


def _edge_flags(i, n_lat, n_tiles):
    first = (i == 0) | (i == n_lat)
    last = (i == n_lat - 1) | (i == n_tiles - 1)
    return first, last


def _conv_gate_fwd(u, conv_w, conv_b, n_lat, name):
    R, F2 = u.shape
    F = F2 // 2
    n_tiles = R // ROW_TILE
    T = ROW_TILE
    cw = _pick(F, 256, 128)
    row = pl.BlockSpec((T, F2), lambda i: (i, 0))
    prev, nxt = _halo_specs(n_tiles, F2, rows=_halo_rows(u.dtype))

    assert u.dtype == BF16

    def body(u_ref, p_ref, n_ref, w_ref, b_ref, o_ref):
        i = pl.program_id(0)
        first, last = _edge_flags(i, n_lat, n_tiles)
        taps = _shift_matrix(T, _halo_rows(BF16), (-1, 0, 1))

        def conv(c0):
            cols = pl.ds(c0, cw)
            up, uv, un = _shifted_rows(u_ref, p_ref, n_ref, cols, first, last, taps, 3)
            return (up * w_ref[pl.ds(0, 1), cols] + uv * w_ref[pl.ds(1, 1), cols]
                    + un * w_ref[pl.ds(2, 1), cols] + b_ref[:, cols])

        for c0 in range(0, F, cw):
            ca, cv = conv(c0), conv(F + c0)
            o_ref[:, pl.ds(c0, cw)] = (ca * _sigmoid(ca) * cv).astype(BF16)

    return pl.pallas_call(
        body, name=name, grid=(n_tiles,),
        in_specs=[row, prev, nxt, pl.BlockSpec((3, F2), lambda i: (0, 0)),
                  pl.BlockSpec((1, F2), lambda i: (0, 0))],
        out_specs=pl.BlockSpec((T, F), lambda i: (i, 0)),
        out_shape=SDS((R, F), BF16), compiler_params=_params(1),
    )(u, u, u, conv_w, conv_b)


def _conv_gate_bwd(u, dgact, conv_w, conv_b, n_lat, name):
    R, F2 = u.shape
    F = F2 // 2
    n_tiles = R // ROW_TILE
    T, N = ROW_TILE, ROW_TILE + 2 * HALO
    cw = _pick(F, 256, 128)
    rowu = pl.BlockSpec((T, F2), lambda i: (i, 0))
    rowg = pl.BlockSpec((T, F), lambda i: (i, 0))
    pu, nu = _halo_specs(n_tiles, F2, rows=_halo_rows(u.dtype))
    pg, ng = _halo_specs(n_tiles, F, rows=_halo_rows(dgact.dtype))

    assert u.dtype == BF16 and dgact.dtype == BF16

    def body(u_ref, pu_ref, nu_ref, g_ref, pg_ref, ng_ref, w_ref, b_ref, du_ref, dw_ref, db_ref):
        i = pl.program_id(0)
        first, last = _edge_flags(i, n_lat, n_tiles)

        @pl.when(i == 0)
        def _():
            dw_ref[...] = jnp.zeros_like(dw_ref)
            db_ref[...] = jnp.zeros_like(db_ref)

        taps = _shift_matrix(N, _halo_rows(BF16) - HALO, (-1, 0, 1))
        same = _shift_matrix(N, _halo_rows(BF16) - HALO, (0,))

        def conv(c0):
            cols = pl.ds(c0, cw)
            up, e, un = _shifted_rows(u_ref, pu_ref, nu_ref, cols, first, last, taps, 3)
            c = (up * w_ref[pl.ds(0, 1), cols] + e * w_ref[pl.ds(1, 1), cols]
                 + un * w_ref[pl.ds(2, 1), cols] + b_ref[:, cols])
            return c, up, e, un

        def back(c0, dc, up, e, un):
            cols = pl.ds(c0, cw)
            du = (pltpu.roll(dc, N - 1, 0) * w_ref[pl.ds(0, 1), cols] + dc * w_ref[pl.ds(1, 1), cols]
                  + pltpu.roll(dc, 1, 0) * w_ref[pl.ds(2, 1), cols])
            du_ref[:, cols] = du[HALO:HALO + T].astype(BF16)
            dct = dc[HALO:HALO + T]
            dw_ref[pl.ds(0, 1), cols] += jnp.sum(dct * up[HALO:HALO + T], axis=0, keepdims=True)
            dw_ref[pl.ds(1, 1), cols] += jnp.sum(dct * e[HALO:HALO + T], axis=0, keepdims=True)
            dw_ref[pl.ds(2, 1), cols] += jnp.sum(dct * un[HALO:HALO + T], axis=0, keepdims=True)
            db_ref[:, cols] += jnp.sum(dct, axis=0, keepdims=True)

        for c0 in range(0, F, cw):
            dg, = _shifted_rows(g_ref, pg_ref, ng_ref, pl.ds(c0, cw), first, last, same, 1)
            ca, upa, ea, una = conv(c0)
            cv, upv, ev, unv = conv(F + c0)
            s = _sigmoid(ca)
            back(F + c0, dg * (ca * s), upv, ev, unv)
            back(c0, dg * cv * (s * (1.0 + ca * (1.0 - s))), upa, ea, una)

    return pl.pallas_call(
        body, name=name, grid=(n_tiles,),
        in_specs=[rowu, pu, nu, rowg, pg, ng, pl.BlockSpec((3, F2), lambda i: (0, 0)),
                  pl.BlockSpec((1, F2), lambda i: (0, 0))],
        out_specs=[rowu, pl.BlockSpec((3, F2), lambda i: (0, 0)), pl.BlockSpec((1, F2), lambda i: (0, 0))],
        out_shape=[SDS((R, F2), BF16), SDS((3, F2), F32), SDS((1, F2), F32)],
        compiler_params=_params(1),
    )(u, u, u, dgact, dgact, dgact, conv_w, conv_b)


def _pool_counts(i, n_lat, s_len, l_len, n_rows, offset):
    ctx = i >= n_lat
    t0 = jnp.where(ctx, i - n_lat, i) * ROW_TILE + offset
    seg = jnp.where(ctx, l_len, s_len)
    t = t0 + lax.broadcasted_iota(jnp.int32, (n_rows, 1), 0)
    out = []
    for win in POOL_WINDOWS:
        cnt = jnp.minimum(t + win // 2, seg) - jnp.maximum(t - win // 2, 0)
        out.append(jnp.maximum(cnt, 1).astype(F32))
    return out


def _window_sum(e, lo, hi, n):
    acc = None
    for j in range(lo, hi + 1):
        term = e if j == 0 else pltpu.roll(e, (-j) % n, 0)
        acc = term if acc is None else acc + term
    return acc


def _pool_fwd(h, w, b, scale, n_lat, s_len, l_len, name):
    R, D = h.shape
    G = D // 4
    n_tiles = R // ROW_TILE
    T, N = ROW_TILE, ROW_TILE + 2 * HALO
    row = pl.BlockSpec((T, D), lambda i: (i, 0))
    prev, nxt = _halo_specs(n_tiles, D)
    vec = pl.BlockSpec((1, D), lambda i: (0, 0))

    def body(h_ref, p_ref, n_ref, w_ref, b_ref, s_ref, y_ref):
        i = pl.program_id(0)
        first, last = _edge_flags(i, n_lat, n_tiles)
        cnts = _pool_counts(i, n_lat, s_len, l_len, T, 0)
        for g, win in enumerate(POOL_WINDOWS):
            cols = pl.ds(g * G, G)
            pr = jnp.where(first, 0.0, p_ref[:, cols])
            nx = jnp.where(last, 0.0, n_ref[:, cols])
            hv = h_ref[:, cols]
            e = jnp.concatenate([pr, hv, nx], axis=0)
            mean = _window_sum(e, -(win // 2), win // 2 - 1, N)[HALO:HALO + T] / cnts[g]
            yg = _dot(_bf(mean - hv), w_ref[g])
            y_ref[:, cols] = (yg + b_ref[:, cols]) * s_ref[:, cols]

    return pl.pallas_call(
        body, name=name, grid=(n_tiles,),
        in_specs=[row, prev, nxt, pl.BlockSpec((4, G, G), lambda i: (0, 0, 0)), vec, vec],
        out_specs=row, out_shape=SDS((R, D), F32), compiler_params=_params(1),
    )(h, h, h, w, b, scale)


def _pool_bwd(h, dy, w, b, scale, n_lat, s_len, l_len, name):
    R, D = h.shape
    G = D // 4
    n_tiles = R // ROW_TILE
    T, N = ROW_TILE, ROW_TILE + 2 * HALO
    row = pl.BlockSpec((T, D), lambda i: (i, 0))
    prev, nxt = _halo_specs(n_tiles, D)
    vec = pl.BlockSpec((1, D), lambda i: (0, 0))
    wspec = pl.BlockSpec((4, G, G), lambda i: (0, 0, 0))

    def body(h_ref, ph_ref, nh_ref, d_ref, pd_ref, nd_ref, w_ref, b_ref, s_ref,
             dh_ref, dw_ref, db_ref, ds_ref):
        i = pl.program_id(0)
        first, last = _edge_flags(i, n_lat, n_tiles)

        @pl.when(i == 0)
        def _():
            dw_ref[...] = jnp.zeros_like(dw_ref)
            db_ref[...] = jnp.zeros_like(db_ref)
            ds_ref[...] = jnp.zeros_like(ds_ref)

        cnts = _pool_counts(i, n_lat, s_len, l_len, T, 0)
        cnts_ext = _pool_counts(i, n_lat, s_len, l_len, N, -HALO)
        for g, win in enumerate(POOL_WINDOWS):
            cols = pl.ds(g * G, G)

            def ext(t_ref, p_ref, n_ref):
                pr = jnp.where(first, 0.0, p_ref[:, cols])
                nx = jnp.where(last, 0.0, n_ref[:, cols])
                return jnp.concatenate([pr, t_ref[:, cols], nx], axis=0)

            hv = h_ref[:, cols]
            mean = _window_sum(ext(h_ref, ph_ref, nh_ref), -(win // 2), win // 2 - 1, N)[HALO:HALO + T] / cnts[g]
            z = _bf(mean - hv)
            sc = s_ref[:, cols]
            dye = ext(d_ref, pd_ref, nd_ref)
            dt = _bf(dye * sc)
            dz = _dot_nt(dt, w_ref[g])
            dm = dz / cnts_ext[g]
            dh = _window_sum(dm, -(win // 2 - 1), win // 2, N) - dz
            dh_ref[:, cols] = dh[HALO:HALO + T]
            dyt = dye[HALO:HALO + T]
            dw_ref[g] += _dot_tn(z, dt[HALO:HALO + T])
            db_ref[:, cols] += jnp.sum(dyt * sc, axis=0, keepdims=True)
            ds_ref[:, cols] += jnp.sum(dyt * (_dot(z, w_ref[g]) + b_ref[:, cols]), axis=0, keepdims=True)

    return pl.pallas_call(
        body, name=name, grid=(n_tiles,),
        in_specs=[row, prev, nxt, row, prev, nxt, wspec, vec, vec],
        out_specs=[row, wspec, vec, vec],
        out_shape=[SDS((R, D), F32), SDS((4, G, G), F32), SDS((1, D), F32), SDS((1, D), F32)],
        compiler_params=_params(1),
    )(h, h, h, dy, dy, dy, w, b, scale)


def _rope_tables(s_len, l_len):
    t = jnp.arange(s_len)
    row = (t // GRID_W).astype(F32)
    col = (t % GRID_W).astype(F32)
    axis_dim = HEAD_DIM // 2
    inv = ROPE_THETA ** (-jnp.arange(0, axis_dim, 2, dtype=F32) / axis_dim)
    ar, ac = row[:, None] * inv, col[:, None] * inv
    cos = jnp.concatenate([jnp.cos(ar), jnp.cos(ar), jnp.cos(ac), jnp.cos(ac)], axis=-1)
    sin = jnp.concatenate([-jnp.sin(ar), jnp.sin(ar), -jnp.sin(ac), jnp.sin(ac)], axis=-1)
    cos = jnp.concatenate([cos, jnp.ones((l_len, HEAD_DIM), F32)], axis=0)
    sin = jnp.concatenate([sin, jnp.zeros((l_len, HEAD_DIM), F32)], axis=0)
    return cos, sin


def _swap_halves(v):
    lane = lax.broadcasted_iota(jnp.int32, v.shape, 1)
    return jnp.where((lane % 64) < 32, pltpu.roll(v, 96, 1), pltpu.roll(v, 32, 1))


def _qk_prep_fwd(qkv, q_gain, k_gain, cos, sin):
    R = qkv.shape[0]
    NQ, NK = N_HEADS * HEAD_DIM, N_KV * HEAD_DIM
    T = ROW_TILE
    vec = pl.BlockSpec((1, HEAD_DIM), lambda i: (0, 0))
    tab = pl.BlockSpec((T, HEAD_DIM), lambda i: (i, 0))

    def body(x_ref, qg_ref, kg_ref, c_ref, s_ref, q_ref, k_ref, v_ref):
        cosv, sinv = c_ref[...], s_ref[...]

        def prep(c0, gain):
            xh = x_ref[:, pl.ds(c0, HEAD_DIM)]
            xn = xh * lax.rsqrt(jnp.mean(xh * xh, axis=-1, keepdims=True) + EPS) * gain
            return _bf(xn * cosv + _swap_halves(xn) * sinv)

        for hd in range(N_HEADS):
            q_ref[:, pl.ds(hd * HEAD_DIM, HEAD_DIM)] = prep(hd * HEAD_DIM, qg_ref[...])
        for hd in range(N_KV):
            k_ref[:, pl.ds(hd * HEAD_DIM, HEAD_DIM)] = prep(NQ + hd * HEAD_DIM, kg_ref[...])
            v_ref[:, pl.ds(2 * hd * HEAD_DIM, HEAD_DIM)] = _bf(x_ref[:, pl.ds(NQ + NK + hd * HEAD_DIM, HEAD_DIM)])
            v_ref[:, pl.ds((2 * hd + 1) * HEAD_DIM, HEAD_DIM)] = jnp.ones((T, HEAD_DIM), BF16)

    return pl.pallas_call(
        body, name="qk_prep_fwd", grid=(R // T,),
        in_specs=[pl.BlockSpec((T, NQ + 2 * NK), lambda i: (i, 0)), vec, vec, tab, tab],
        out_specs=[pl.BlockSpec((T, NQ), lambda i: (i, 0)), pl.BlockSpec((T, NK), lambda i: (i, 0)),
                   pl.BlockSpec((T, 2 * NK), lambda i: (i, 0))],
        out_shape=[SDS((R, NQ), BF16), SDS((R, NK), BF16), SDS((R, 2 * NK), BF16)],
        compiler_params=_params(1),
    )(qkv, q_gain, k_gain, cos, sin)


def _qk_prep_bwd(qkv, dq, dk, dv, q_gain, k_gain, cos, sin):
    R = qkv.shape[0]
    NQ, NK = N_HEADS * HEAD_DIM, N_KV * HEAD_DIM
    T = ROW_TILE
    vec = pl.BlockSpec((1, HEAD_DIM), lambda i: (0, 0))
    tab = pl.BlockSpec((T, HEAD_DIM), lambda i: (i, 0))

    def body(x_ref, dq_ref, dk_ref, dv_ref, qg_ref, kg_ref, c_ref, s_ref, o_ref, dqg_ref, dkg_ref):
        i = pl.program_id(0)
        cosv, sinv = c_ref[...], s_ref[...]

        @pl.when(i == 0)
        def _():
            dqg_ref[...] = jnp.zeros_like(dqg_ref)
            dkg_ref[...] = jnp.zeros_like(dkg_ref)

        def back(c0, dout, gain, dg_ref):
            xh = x_ref[:, pl.ds(c0, HEAD_DIM)]
            r = lax.rsqrt(jnp.mean(xh * xh, axis=-1, keepdims=True) + EPS)
            xhat = xh * r
            dxn = dout * cosv + _swap_halves(dout * sinv)
            dg_ref[...] += jnp.sum(dxn * xhat, axis=0, keepdims=True)
            a = dxn * gain
            o_ref[:, pl.ds(c0, HEAD_DIM)] = _bf(r * (a - xhat * jnp.mean(a * xhat, axis=-1, keepdims=True)))

        for hd in range(N_HEADS):
            back(hd * HEAD_DIM, dq_ref[:, pl.ds(hd * HEAD_DIM, HEAD_DIM)], qg_ref[...], dqg_ref)
        for hd in range(N_KV):
            back(NQ + hd * HEAD_DIM, dk_ref[:, pl.ds(hd * HEAD_DIM, HEAD_DIM)], kg_ref[...], dkg_ref)
        o_ref[:, pl.ds(NQ + NK, NK)] = _bf(dv_ref[...])

    return pl.pallas_call(
        body, name="qk_prep_bwd", grid=(R // T,),
        in_specs=[pl.BlockSpec((T, NQ + 2 * NK), lambda i: (i, 0)), pl.BlockSpec((T, NQ), lambda i: (i, 0)),
                  pl.BlockSpec((T, NK), lambda i: (i, 0)), pl.BlockSpec((T, NK), lambda i: (i, 0)),
                  vec, vec, tab, tab],
        out_specs=[pl.BlockSpec((T, NQ + 2 * NK), lambda i: (i, 0)), vec, vec],
        out_shape=[SDS((R, NQ + 2 * NK), BF16), SDS((1, HEAD_DIM), F32), SDS((1, HEAD_DIM), F32)],
        compiler_params=_params(1),
    )(qkv, dq, dk, dv, q_gain, k_gain, cos, sin)


def _flash_fwd(q, k, v, s_len, l_len):
    R = q.shape[0]
    T = FLASH_FWD_TILE
    n_lat = s_len // T
    ck = _pick(s_len, 512, 128)
    scale = HEAD_DIM ** -0.5
    group = N_HEADS // N_KV
    GW = group * HEAD_DIM
    M = group * T
    chunks = s_len // ck
    to_log2 = scale * math.log2(math.e)

    def body(q_ref, k_ref, v_ref, o_ref, lse_ref, s_s, sc_s, ml_s, mb_s, acc_s):
        i = pl.program_id(1)
        qv = jnp.concatenate([q_ref[:, pl.ds(hh * HEAD_DIM, HEAD_DIM)] for hh in range(group)], axis=0)

        ml_s[...] = jnp.full_like(ml_s, -jnp.inf)

        def lane_max(s, n):
            m = ml_s[...]
            for t in range(n // HEAD_DIM):
                m = jnp.maximum(m, s[:, t * HEAD_DIM:(t + 1) * HEAD_DIM])
            ml_s[...] = m

        @pl.when(i < n_lat)
        def _():
            def loop(c, carry):
                s = _dot_nt(qv, k_ref[pl.ds(pl.multiple_of(c * ck, ck), ck), :])
                s_s[c] = s
                lane_max(s, ck)
                return carry
            lax.fori_loop(0, chunks, loop, 0, unroll=4 if chunks % 4 == 0 else 1)

        sc = _dot_nt(qv, k_ref[pl.ds(s_len, l_len), :])
        sc_s[...] = sc
        lane_max(sc, l_len)
        m_row = jnp.max(ml_s[...], axis=-1, keepdims=True) * to_log2
        mb_s[...] = jnp.broadcast_to(m_row, (M, ck))

        acc_s[...] = jnp.zeros_like(acc_s)

        @pl.when(i < n_lat)
        def _():
            def loop(c, carry):
                p = jnp.exp2(s_s[c] * to_log2 - mb_s[...])
                acc_s[...] += _dot(_bf(p), v_ref[pl.ds(pl.multiple_of(c * ck, ck), ck), :])
                return carry
            lax.fori_loop(0, chunks, loop, 0, unroll=4 if chunks % 4 == 0 else 1)

        p = jnp.exp2(sc_s[...] * to_log2 - mb_s[:, pl.ds(0, l_len)])
        acc_s[...] += _dot(_bf(p), v_ref[pl.ds(s_len, l_len), :])
        l_rep = acc_s[:, pl.ds(HEAD_DIM, HEAD_DIM)]
        o = acc_s[:, pl.ds(0, HEAD_DIM)] / l_rep
        for hh in range(group):
            o_ref[:, pl.ds(hh * HEAD_DIM, HEAD_DIM)] = o[hh * T:(hh + 1) * T]
        lse = (mb_s[:, pl.ds(0, HEAD_DIM)] + jnp.log2(l_rep)) * math.log(2.0)
        lse_ref[...] = jnp.max(lse, axis=-1, keepdims=True).reshape(group, T, 1)

    return pl.pallas_call(
        body, name="flash_fwd", grid=(N_KV, R // T),
        in_specs=[pl.BlockSpec((T, GW), lambda g, i: (i, g)),
                  pl.BlockSpec((R, HEAD_DIM), lambda g, i: (0, g)),
                  pl.BlockSpec((R, 2 * HEAD_DIM), lambda g, i: (0, g))],
        out_specs=[pl.BlockSpec((T, GW), lambda g, i: (i, g)),
                   pl.BlockSpec((group, T, 1), lambda g, i: (g, i, 0))],
        out_shape=[SDS((R, N_HEADS * HEAD_DIM), F32), SDS((N_HEADS, R, 1), F32)],
        scratch_shapes=[pltpu.VMEM((chunks, M, ck), F32), pltpu.VMEM((M, l_len), F32), pltpu.VMEM((M, HEAD_DIM), F32),
                        pltpu.VMEM((M, ck), F32), pltpu.VMEM((M, 2 * HEAD_DIM), F32)],
        compiler_params=_params(2),
    )(q, k, v)


def _flash_bwd(q, k, v, o, lse, do, s_len, l_len):
    R = q.shape[0]
    T = ROW_TILE
    n_lat = s_len // T
    ck = _pick(s_len, 512, 128)
    scale = HEAD_DIM ** -0.5
    group = N_HEADS // N_KV
    GW = group * HEAD_DIM
    qspec = pl.BlockSpec((T, GW), lambda g, i: (i, g))
    kspec = pl.BlockSpec((R, HEAD_DIM), lambda g, i: (0, g))

    M = group * T
    log2e = math.log2(math.e)

    def body(q_ref, do_ref, o_ref, lse_ref, k_ref, v_ref, dq_ref, dk_ref, dv_ref, dq_s, lse_s, delta_s):
        i = pl.program_id(1)

        @pl.when(i == 0)
        def _():
            dk_ref[...] = jnp.zeros_like(dk_ref)
            dv_ref[...] = jnp.zeros_like(dv_ref)

        def stacked(ref):
            return jnp.concatenate([ref[:, pl.ds(hh * HEAD_DIM, HEAD_DIM)] for hh in range(group)], axis=0)

        qv = stacked(q_ref)
        dov = stacked(do_ref)
        dob = _bf(dov)
        delta_s[...] = jnp.broadcast_to(jnp.sum(dov * stacked(o_ref), axis=-1, keepdims=True), (M, ck))
        lse_s[...] = jnp.broadcast_to(lse_ref[...].reshape(M, 1) * log2e, (M, ck))
        dq_s[...] = jnp.zeros_like(dq_s)

        def step(rows, n):
            kv, vv = k_ref[rows, :], v_ref[rows, :]
            p = jnp.exp2(_dot_nt(qv, kv) * (scale * log2e) - lse_s[:, pl.ds(0, n)])
            dv_ref[rows, :] += _dot_tn(_bf(p), dob)
            ds = _bf(p * (_dot_nt(dob, vv) - delta_s[:, pl.ds(0, n)]) * scale)
            dq_s[...] += _dot(ds, kv)
            dk_ref[rows, :] += _dot_tn(ds, qv)

        @pl.when(i < n_lat)
        def _():
            def loop(c, carry):
                step(pl.ds(pl.multiple_of(c * ck, ck), ck), ck)
                return carry
            lax.fori_loop(0, s_len // ck, loop, 0, unroll=4 if (s_len // ck) % 4 == 0 else 1)

        step(pl.ds(s_len, l_len), l_len)
        for hh in range(group):
            dq_ref[:, pl.ds(hh * HEAD_DIM, HEAD_DIM)] = dq_s[pl.ds(hh * T, T), :]

    return pl.pallas_call(
        body, name="flash_bwd", grid=(N_KV, R // T),
        in_specs=[qspec, qspec, qspec, pl.BlockSpec((group, T, 1), lambda g, i: (g, i, 0)), kspec,
                  pl.BlockSpec((R, HEAD_DIM), lambda g, i: (0, 2 * g))],
        out_specs=[qspec, kspec, kspec],
        out_shape=[SDS((R, N_HEADS * HEAD_DIM), F32), SDS((R, N_KV * HEAD_DIM), F32),
                   SDS((R, N_KV * HEAD_DIM), F32)],
        scratch_shapes=[pltpu.VMEM((M, HEAD_DIM), F32), pltpu.VMEM((M, ck), F32), pltpu.VMEM((M, ck), F32)],
        compiler_params=_params(2),
    )(q, do, o, lse, k, v)


K_SCALE = RET_DK ** -0.5


def _log_sigmoid(v):
    return -(jnp.maximum(-v, 0.0) + jnp.log(1.0 + jnp.exp(-jnp.abs(v))))


def _ret_decays(d, lg):
    C = RET_CHUNK
    ic = lax.broadcasted_iota(jnp.int32, (C, 1), 0)
    ir = lax.broadcasted_iota(jnp.int32, (1, C), 1)
    li = jnp.where(d == 0, ic, C - 1 - ic).astype(F32)
    lj = jnp.where(d == 0, ir, C - 1 - ir).astype(F32)
    diff = li - lj
    mask = jnp.where(diff >= 0, jnp.exp(jnp.maximum(diff, 0.0) * lg), 0.0)
    qd = jnp.exp((li + 1.0) * lg)
    kd = jnp.exp((C - 1.0 - li) * lg)
    cd = jnp.exp(C * lg)
    return li, diff, mask, qd, kd, cd


def _ctx_weights(d, t, lg, l_len):
    C = RET_CHUNK
    j = (t * C + lax.broadcasted_iota(jnp.int32, (C, 1), 0)).astype(F32)
    e = jnp.where(d == 0, (l_len - 1.0) - j, j)
    return e, jnp.exp(e * lg)


def _mirrored(x, i, n_lat):
    return jnp.where(i < n_lat, jnp.concatenate([x[RET_CHUNK:], x[:RET_CHUNK]], axis=0), x)


def _mirror_tile(n_lat):
    return lambda i: jnp.where(i < n_lat, n_lat - 1 - i, i)


def _ret_fwd(proj, lgt, s_len, l_len):
    R = proj.shape[0]
    C, H, DK, DV = RET_CHUNK, RET_HEADS, RET_DK, RET_DV
    nl, nc = s_len // C, l_len // C

    def stored(t):
        return jnp.where(t < nc, nl + t, jnp.maximum(t - nc, 0))

    def actual(d, t):
        n = jnp.maximum(t - nc, 0)
        return jnp.where(t < nc, nl + t, n if d == 0 else nl - 1 - n)

    def body(q0_ref, k0_ref, v0_ref, q1_ref, k1_ref, v1_ref, lg_ref, o_ref, st_ref, r_s):
        t = pl.program_id(0)
        qkv = ((q0_ref, k0_ref, v0_ref), (q1_ref, k1_ref, v1_ref))

        @pl.when(t == 0)
        def _():
            r_s[...] = jnp.zeros_like(r_s)

        def log_gamma(d, hh):
            return jnp.max(_log_sigmoid(lg_ref[d, hh]), axis=-1, keepdims=True)

        @pl.when(t < nc)
        def _():
            for d, (q_ref, k_ref, v_ref) in enumerate(qkv):
                for hh in range(H):
                    qc, vc = pl.ds(hh * DK, DK), pl.ds(hh * DV, DV)
                    _, w = _ctx_weights(d, t, log_gamma(d, hh), l_len)
                    r_s[d, hh] += _dot_tn(_bf(k_ref[:, qc] * K_SCALE * w), _bf(v_ref[:, vc]))
                    o_ref[d, :, vc] = jnp.zeros((C, DV), F32)

        @pl.when(t >= nc)
        def _():
            for d, (q_ref, k_ref, v_ref) in enumerate(qkv):
                for hh in range(H):
                    qc, vc = pl.ds(hh * DK, DK), pl.ds(hh * DV, DV)
                    _, _, mask, qd, kd, cd = _ret_decays(d, log_gamma(d, hh))
                    qb, kv, vb = _bf(q_ref[:, qc]), k_ref[:, qc] * K_SCALE, _bf(v_ref[:, vc])
                    r = r_s[d, hh]
                    st_ref[d, hh, 0] = r
                    att = _dot_nt(qb, _bf(kv)) * mask
                    o_ref[d, :, vc] = _dot(_bf(att), vb) + _dot(qb, _bf(r)) * qd
                    r_s[d, hh] = r * cd + _dot_tn(_bf(kv * kd), vb)

    def rows(d):
        return [pl.BlockSpec((C, H * DK), lambda t: (actual(d, t), 0)),
                pl.BlockSpec((C, H * DK), lambda t: (actual(d, t), 1)),
                pl.BlockSpec((C, H * DV), lambda t: (actual(d, t), 1))]

    return pl.pallas_call(
        body, name="ret_fwd", grid=(nc + nl,),
        in_specs=rows(0) + rows(1) + [pl.BlockSpec((2, H, 1, 128), lambda t: (0, 0, 0, 0))],
        out_specs=[pl.BlockSpec((2, C, H * DV), lambda t: (0, stored(t), 0)),
                   pl.BlockSpec((2, H, 1, DK, DV), lambda t: (0, 0, jnp.maximum(t - nc, 0), 0, 0))],
        out_shape=[SDS((2, R, H * DV), F32), SDS((2, H, nl, DK, DV), F32)],
        scratch_shapes=[pltpu.VMEM((2, H, DK, DV), F32)],
        compiler_params=_params(1),
    )(proj, proj, proj, proj, proj, proj, lgt)


def _ret_bwd(proj, lgt, states, do, s_len, l_len):
    R = proj.shape[0]
    C, H, DK, DV = RET_CHUNK, RET_HEADS, RET_DK, RET_DV
    nl, nc = s_len // C, l_len // C
    last = nl + nc - 1

    def stored(t):
        return jnp.where(t < nl, jnp.maximum(nl - 1 - t, 0), t)

    def actual(d, t):
        return stored(t) if d == 0 else t

    def body(q0_ref, k0_ref, v0_ref, do0_ref, q1_ref, k1_ref, v1_ref, do1_ref, lg_ref, st_ref,
             dq_ref, dk_ref, dv_ref, dlg_ref, dr_s, dl_s):
        t = pl.program_id(0)
        ins = ((q0_ref, k0_ref, v0_ref, do0_ref), (q1_ref, k1_ref, v1_ref, do1_ref))

        def log_gamma(d, hh):
            return jnp.max(_log_sigmoid(lg_ref[d, hh]), axis=-1, keepdims=True)

        @pl.when(t == 0)
        def _():
            dr_s[...] = jnp.zeros_like(dr_s)
            dl_s[...] = jnp.zeros_like(dl_s)

        @pl.when(t < nl)
        def _():
            for d, (q_ref, k_ref, v_ref, do_ref) in enumerate(ins):
                for hh in range(H):
                    qc, vc = pl.ds(hh * DK, DK), pl.ds(hh * DV, DV)
                    li, diff, mask, qd, kd, cd = _ret_decays(d, log_gamma(d, hh))
                    qv, kv, vv, dov = q_ref[:, qc], k_ref[:, qc] * K_SCALE, v_ref[:, vc], do_ref[:, vc]
                    qb, kb, vb, dob = _bf(qv), _bf(kv), _bf(vv), _bf(dov)
                    r, drn = st_ref[d, hh, 0], dr_s[d, hh]
                    rb, drb = _bf(r), _bf(drn)
                    p = _dot_nt(qb, kb)
                    dp = _dot_nt(dob, vb) * mask
                    dpb = _bf(dp)
                    doq = _bf(dov * qd)
                    dq_inter = _dot_nt(doq, rb)
                    dk_state = kd * _dot_nt(vb, drb)
                    dq_ref[d, :, qc] = _dot(dpb, kb) + dq_inter
                    dk_ref[d, :, qc] = (_dot_tn(dpb, qb) + dk_state) * K_SCALE
                    dv_ref[d, :, vc] = _dot_tn(_bf(p * mask), dob) + _dot(_bf(kv * kd), drb)
                    dr_s[d, hh] = cd * drn + _dot_tn(qb, doq)
                    dl_s[d, hh] += (jnp.sum(dp * p * diff) + jnp.sum((li + 1.0) * qv * dq_inter)
                                    + jnp.sum((C - 1.0 - li) * kv * dk_state) + C * jnp.sum(cd * r * drn))

        @pl.when(t >= nl)
        def _():
            for d, (q_ref, k_ref, v_ref, do_ref) in enumerate(ins):
                for hh in range(H):
                    qc, vc = pl.ds(hh * DK, DK), pl.ds(hh * DV, DV)
                    e, w = _ctx_weights(d, t - nl, log_gamma(d, hh), l_len)
                    kv, vb, drb = k_ref[:, qc] * K_SCALE, _bf(v_ref[:, vc]), _bf(dr_s[d, hh])
                    dkc = w * _dot_nt(vb, drb)
                    dq_ref[d, :, qc] = jnp.zeros((C, DK), F32)
                    dk_ref[d, :, qc] = dkc * K_SCALE
                    dv_ref[d, :, vc] = _dot(_bf(kv * w), drb)
                    dl_s[d, hh] += jnp.sum(e * kv * dkc)

        @pl.when(t == last)
        def _():
            for d in range(2):
                for hh in range(H):
                    dlg_ref[d, hh] = dl_s[d, hh] * (1.0 / (1.0 + jnp.exp(lg_ref[d, hh])))

    def rows(d):
        return [pl.BlockSpec((C, H * DK), lambda t: (actual(d, t), 0)),
                pl.BlockSpec((C, H * DK), lambda t: (actual(d, t), 1)),
                pl.BlockSpec((C, H * DV), lambda t: (actual(d, t), 1)),
                pl.BlockSpec((C, H * DV), lambda t: (actual(d, t), 0))]

    return pl.pallas_call(
        body, name="ret_bwd", grid=(nl + nc,),
        in_specs=rows(0) + rows(1) + [
            pl.BlockSpec((2, H, 1, 128), lambda t: (0, 0, 0, 0)),
            pl.BlockSpec((2, H, 1, DK, DV), lambda t: (0, 0, jnp.maximum(nl - 1 - t, 0), 0, 0))],
        out_specs=[pl.BlockSpec((2, C, H * DK), lambda t: (0, stored(t), 0)),
                   pl.BlockSpec((2, C, H * DK), lambda t: (0, stored(t), 0)),
                   pl.BlockSpec((2, C, H * DV), lambda t: (0, stored(t), 0)),
                   pl.BlockSpec((2, H, 1, 128), lambda t: (0, 0, 0, 0))],
        out_shape=[SDS((2, R, H * DK), F32), SDS((2, R, H * DK), F32), SDS((2, R, H * DV), F32),
                   SDS((2, H, 1, 128), F32)],
        scratch_shapes=[pltpu.VMEM((2, H, DK, DV), F32), pltpu.VMEM((2, H, 1, 128), F32)],
        compiler_params=_params(1),
    )(proj, proj, proj, do, proj, proj, proj, do, lgt, states)


def _readout_fwd(o2, proj, gn_w, n_lat):
    R = proj.shape[0]
    H, DV = RET_HEADS, RET_DV
    W = H * DV
    T = ROW_TILE
    assert T == 2 * RET_CHUNK

    def body(o_ref, ob_ref, g_ref, w_ref, out_ref):
        i = pl.program_id(0)
        for hh in range(H):
            cols = pl.ds(hh * DV, DV)
            y = o_ref[0, :, cols] + _mirrored(ob_ref[0, :, cols], i, n_lat)
            yc = y - jnp.mean(y, axis=-1, keepdims=True)
            yn = yc * lax.rsqrt(jnp.mean(yc * yc, axis=-1, keepdims=True) + EPS) * w_ref[:, cols]
            g = g_ref[:, cols]
            out_ref[:, cols] = _bf(g * _sigmoid(g) * yn)

    return pl.pallas_call(
        body, name="readout_fwd", grid=(R // T,),
        in_specs=[pl.BlockSpec((1, T, W), lambda i: (0, i, 0)),
                  pl.BlockSpec((1, T, W), lambda i: (1, _mirror_tile(n_lat)(i), 0)),
                  pl.BlockSpec((T, W), lambda i: (i, 2)), pl.BlockSpec((1, W), lambda i: (0, 0))],
        out_specs=pl.BlockSpec((T, W), lambda i: (i, 0)),
        out_shape=SDS((R, W), BF16), compiler_params=_params(1),
    )(o2, o2, proj, gn_w)


def _readout_bwd(o2, proj, gn_w, dgated, n_lat):
    R = proj.shape[0]
    H, DV = RET_HEADS, RET_DV
    W = H * DV
    T = ROW_TILE

    def body(o_ref, ob_ref, g_ref, w_ref, d_ref, do_ref, dg_ref, dw_ref):
        i = pl.program_id(0)

        @pl.when(i == 0)
        def _():
            dw_ref[...] = jnp.zeros_like(dw_ref)

        for hh in range(H):
            cols = pl.ds(hh * DV, DV)
            y = o_ref[0, :, cols] + _mirrored(ob_ref[0, :, cols], i, n_lat)
            yc = y - jnp.mean(y, axis=-1, keepdims=True)
            rstd = lax.rsqrt(jnp.mean(yc * yc, axis=-1, keepdims=True) + EPS)
            yn0 = yc * rstd
            wv = w_ref[:, cols]
            g = g_ref[:, cols]
            s = _sigmoid(g)
            dgt = d_ref[:, cols]
            dyn = dgt * (g * s)
            dg_ref[:, cols] = _bf(dgt * (yn0 * wv) * (s * (1.0 + g * (1.0 - s))))
            dw_ref[:, cols] += jnp.sum(dyn * yn0, axis=0, keepdims=True)
            a = dyn * wv
            do_ref[:, cols] = rstd * (a - jnp.mean(a, axis=-1, keepdims=True)
                                      - yn0 * jnp.mean(a * yn0, axis=-1, keepdims=True))

    return pl.pallas_call(
        body, name="readout_bwd", grid=(R // T,),
        in_specs=[pl.BlockSpec((1, T, W), lambda i: (0, i, 0)),
                  pl.BlockSpec((1, T, W), lambda i: (1, _mirror_tile(n_lat)(i), 0)),
                  pl.BlockSpec((T, W), lambda i: (i, 2)),
                  pl.BlockSpec((1, W), lambda i: (0, 0)), pl.BlockSpec((T, W), lambda i: (i, 0))],
        out_specs=[pl.BlockSpec((T, W), lambda i: (i, 0)), pl.BlockSpec((T, W), lambda i: (i, 0)),
                   pl.BlockSpec((1, W), lambda i: (0, 0))],
        out_shape=[SDS((R, W), F32), SDS((R, W), BF16), SDS((1, W), F32)],
        compiler_params=_params(1),
    )(o2, o2, proj, gn_w, dgated)


def _ret_dproj(dq2, dk2, dv2, dg, n_lat):
    R = dg.shape[0]
    NQ, NV = RET_HEADS * RET_DK, RET_HEADS * RET_DV
    T = ROW_TILE

    def body(dq_ref, dqb_ref, dk_ref, dkb_ref, dv_ref, dvb_ref, dg_ref, o_ref):
        i = pl.program_id(0)
        o_ref[:, pl.ds(0, NQ)] = _bf(dq_ref[0] + _mirrored(dqb_ref[0], i, n_lat))
        o_ref[:, pl.ds(NQ, NQ)] = _bf(dk_ref[0] + _mirrored(dkb_ref[0], i, n_lat))
        o_ref[:, pl.ds(2 * NQ, NV)] = _bf(dv_ref[0] + _mirrored(dvb_ref[0], i, n_lat))
        o_ref[:, pl.ds(2 * NQ + NV, NV)] = dg_ref[...]

    def both(width):
        return [pl.BlockSpec((1, T, width), lambda i: (0, i, 0)),
                pl.BlockSpec((1, T, width), lambda i: (1, _mirror_tile(n_lat)(i), 0))]

    return pl.pallas_call(
        body, name="ret_dproj", grid=(R // T,),
        in_specs=both(NQ) + both(NQ) + both(NV) + [pl.BlockSpec((T, NV), lambda i: (i, 0))],
        out_specs=pl.BlockSpec((T, 2 * NQ + 2 * NV), lambda i: (i, 0)),
        out_shape=SDS((R, 2 * NQ + 2 * NV), BF16), compiler_params=_params(1),
    )(dq2, dq2, dk2, dk2, dv2, dv2, dg)


def _silu(v):
    return v * _sigmoid(v)


def _ada_fwd(c_rows, ada_w, ada_b_shard):
    depth, D, cols = ada_w.shape

    def body(c_ref, w_ref, b_ref, o_ref):
        o_ref[0] = _dot(_bf(_silu(c_ref[...])), _bf(w_ref[0])) + b_ref[0]

    return pl.pallas_call(
        body, name="ada_fwd", grid=(depth,),
        in_specs=[pl.BlockSpec((16, D), lambda i: (0, 0)), pl.BlockSpec((1, D, cols), lambda i: (i, 0, 0)),
                  pl.BlockSpec((1, 1, cols), lambda i: (i, 0, 0))],
        out_specs=pl.BlockSpec((1, 16, cols), lambda i: (i, 0, 0)),
        out_shape=SDS((depth, 16, cols), F32), compiler_params=_params(1),
    )(c_rows, ada_w, ada_b_shard)


def _ada_bwd(c_rows, ada_w, d_lat, d_ctx):
    depth, D, cols = ada_w.shape

    def body(c_ref, w_ref, dl_ref, dc_ref, dw_ref, pc_ref):
        i = pl.program_id(0)
        cv = c_ref[...]
        a = _silu(cv)
        dcs = jnp.broadcast_to(jnp.sum(dc_ref[0], axis=0, keepdims=True), (8, cols))
        dw_ref[0] = _dot_tn(_bf(a[0:8]), _bf(dl_ref[0])) + _dot_tn(_bf(a[8:16]), _bf(dcs))

        @pl.when(i == 0)
        def _():
            pc_ref[...] = jnp.zeros_like(pc_ref)

        pc_ref[...] += _dot_nt(_bf(dcs), _bf(w_ref[0]))

        @pl.when(i == depth - 1)
        def _():
            cc = c_ref[pl.ds(8, 1), :]
            s = _sigmoid(cc)
            pc_ref[...] = pc_ref[...] * (s * (1.0 + cc * (1.0 - s)))

    return pl.pallas_call(
        body, name="ada_bwd", grid=(depth,),
        in_specs=[pl.BlockSpec((16, D), lambda i: (0, 0)), pl.BlockSpec((1, D, cols), lambda i: (i, 0, 0)),
                  pl.BlockSpec((1, 8, cols), lambda i: (i, 0, 0)), pl.BlockSpec((1, 8, cols), lambda i: (i, 0, 0))],
        out_specs=[pl.BlockSpec((1, D, cols), lambda i: (i, 0, 0)), pl.BlockSpec((8, D), lambda i: (0, 0))],
        out_shape=[SDS((depth, D, cols), F32), SDS((8, D), F32)], compiler_params=_params(1),
    )(c_rows, ada_w, d_lat, d_ctx)


def _adamw(w, g, m, v, name):
    shape = w.shape
    n = g.shape[0]
    cols = shape[-1]
    rows = w.size // cols
    tr = _pick(rows, 512, 8) if rows * cols * 4 > (1 << 20) else rows
    spec = pl.BlockSpec((tr, cols), lambda i: (i, 0))

    def body(w_ref, g_ref, m_ref, v_ref, go_ref, d_ref, mo_ref, vo_ref):
        gs = g_ref[0].astype(F32)
        for k in range(1, n):
            gs = gs + g_ref[k].astype(F32)
        mn = ADAM_B1 * m_ref[...] + (1.0 - ADAM_B1) * gs
        vn = ADAM_B2 * v_ref[...] + (1.0 - ADAM_B2) * jnp.square(gs)
        m_hat = mn / (1.0 - ADAM_B1 ** ADAM_STEP)
        v_hat = vn / (1.0 - ADAM_B2 ** ADAM_STEP)
        go_ref[...] = gs
        d_ref[...] = -ADAM_LR * (m_hat / (jnp.sqrt(v_hat) + ADAM_EPS) + ADAM_WD * w_ref[...])
        mo_ref[...] = mn
        vo_ref[...] = vn

    outs = pl.pallas_call(
        body, name=name, grid=(rows // tr,),
        in_specs=[spec, pl.BlockSpec((n, tr, cols), lambda i: (0, i, 0)), spec, spec],
        out_specs=[spec] * 4, out_shape=[SDS((rows, cols), F32)] * 4, compiler_params=_params(1),
    )(w.reshape(rows, cols), g.reshape(n, rows, cols), m.reshape(rows, cols), v.reshape(rows, cols))
    return tuple(o.reshape(shape) for o in outs)


def _sum_slots(own, recv, name):
    shape, n, cols = own.shape, recv.shape[0], own.shape[-1]
    own, recv = own.reshape(-1, cols), recv.reshape(n, -1, cols)
    rows = own.shape[0]
    tr = _pick(rows, 512, 16)

    def body(own_ref, r_ref, o_ref):
        acc = own_ref[...].astype(F32)
        for k in range(n):
            acc = acc + r_ref[k].astype(F32)
        o_ref[...] = acc

    return pl.pallas_call(
        body, name=name, grid=(rows // tr,),
        in_specs=[pl.BlockSpec((tr, cols), lambda i: (i, 0)), pl.BlockSpec((n, tr, cols), lambda i: (0, i, 0))],
        out_specs=pl.BlockSpec((tr, cols), lambda i: (i, 0)),
        out_shape=SDS((rows, cols), F32), compiler_params=_params(1),
    )(own, recv).reshape(shape)


def _position():
    return lax.axis_index("x"), lax.axis_index("y"), lax.axis_index("c")


def _peer(k, x, y, c):
    return (1 - x if k & 4 else x, 1 - y if k & 2 else y, 1 - c if k & 1 else c)


def _index(pos):
    return 4 * pos[0] + 2 * pos[1] + pos[2]


def _gather_small(v, name):
    rows, lanes = v.shape

    def body(x_ref, out_ref, send_sems, recv_sems, local_sem):
        me = _position()
        mine = pltpu.make_async_copy(x_ref, out_ref.at[_index(me)], local_sem)
        mine.start()

        def copy(k, slot):
            return pltpu.make_async_remote_copy(
                src_ref=x_ref, dst_ref=out_ref.at[slot], send_sem=send_sems.at[k - 1],
                recv_sem=recv_sems.at[k - 1], device_id=_peer(k, *me), device_id_type=MESH)

        sends = [copy(k, _index(me)) for k in range(1, N_DEV)]
        for cp in sends:
            cp.start()
        for k in range(1, N_DEV):
            copy(k, _index(_peer(k, *me))).wait_recv()
        for cp in sends:
            cp.wait_send()
        mine.wait()

    return pl.pallas_call(
        body, name=name, out_shape=SDS((N_DEV, rows, lanes), v.dtype),
        in_specs=[pl.BlockSpec(memory_space=pltpu.VMEM)],
        out_specs=pl.BlockSpec(memory_space=pltpu.VMEM),
        scratch_shapes=[pltpu.SemaphoreType.DMA((N_DEV - 1,)), pltpu.SemaphoreType.DMA((N_DEV - 1,)),
                        pltpu.SemaphoreType.DMA],
        compiler_params=pltpu.CompilerParams(vmem_limit_bytes=VMEM_LIMIT_V7X),
    )(v)


def _gather_big(v, name):
    rows, cols = v.shape

    def body(x_ref, out_ref, send_sems, recv_sems, local_sem):
        x, y, c = _position()
        me, sibling = (x, y, c), (x, y, 1 - c)
        chips = [(1 - x, y), (x, 1 - y), (1 - x, 1 - y)]

        def copy(k, block, to, src=None):
            slot = out_ref.at[_index(block)]
            return pltpu.make_async_remote_copy(
                src_ref=slot if src is None else src, dst_ref=slot, send_sem=send_sems.at[k],
                recv_sem=recv_sems.at[k], device_id=to, device_id_type=MESH)

        mine = pltpu.make_async_copy(x_ref, out_ref.at[_index(me)], local_sem)
        mine.start()
        first = [copy(0, me, sibling, src=x_ref)]
        first += [copy(1 + j, me, (*chip, c), src=x_ref) for j, chip in enumerate(chips)]
        for cp in first:
            cp.start()
        passed = [copy(4 + j, (*chip, c), sibling) for j, chip in enumerate(chips)]
        for j, chip in enumerate(chips):
            copy(1 + j, (*chip, c), me).wait_recv()
            passed[j].start()
        copy(0, sibling, me).wait_recv()
        for j, chip in enumerate(chips):
            copy(4 + j, (*chip, 1 - c), me).wait_recv()
        for cp in first + passed:
            cp.wait_send()
        mine.wait()

    return pl.pallas_call(
        body, name=name, out_shape=SDS((N_DEV, rows, cols), v.dtype),
        in_specs=[pl.BlockSpec(memory_space=pl.ANY)],
        out_specs=pl.BlockSpec(memory_space=pl.ANY),
        scratch_shapes=[pltpu.SemaphoreType.DMA((N_DEV - 1,)), pltpu.SemaphoreType.DMA((N_DEV - 1,)),
                        pltpu.SemaphoreType.DMA],
    )(v)


HBM_SPEC = pl.BlockSpec(memory_space=pltpu.HBM)
SEM_SPEC = pl.BlockSpec(memory_space=pltpu.SEMAPHORE)
SPLIT_EFFECT = pltpu.SideEffectType.DATAFLOW_SIDE_EFFECTING


def _split_start(srcs, gather, name):
    n = len(srcs)
    lands = [jnp.zeros(((N_DEV,) + s.shape) if gather else s.shape, s.dtype) for s in srcs]

    def body(*refs):
        src_refs, land_refs, sems, token = refs[:n], refs[n:2 * n], refs[2 * n:4 * n], refs[-1]
        me = _position()
        for a in range(n):
            for k in range(1, N_DEV):
                peer = _peer(k, *me)
                pltpu.make_async_remote_copy(
                    src_ref=src_refs[a] if gather else src_refs[a].at[_index(peer)],
                    dst_ref=land_refs[a].at[_index(me)], send_sem=sems[2 * a], recv_sem=sems[2 * a + 1],
                    device_id=peer, device_id_type=MESH).start()
        token[...] = jnp.zeros_like(token)

    hbm = lambda arrays: tuple(pltpu.HBM(a.shape, a.dtype) for a in arrays)
    outs = pl.pallas_call(
        body, name=name,
        out_shape=(pltpu.SemaphoreType.DMA(()),) * (2 * n) + hbm(srcs) + hbm(lands) + (SDS((8, 128), F32),),
        in_specs=(HBM_SPEC,) * (2 * n),
        out_specs=(SEM_SPEC,) * (2 * n) + (HBM_SPEC,) * (2 * n) + (pl.BlockSpec(memory_space=pltpu.VMEM),),
        input_output_aliases={a: 2 * n + a for a in range(2 * n)},
        compiler_params=pltpu.CompilerParams(has_side_effects=SPLIT_EFFECT),
    )(*[pltpu.with_memory_space_constraint(a, pltpu.HBM) for a in list(srcs) + lands])
    return outs[:2 * n], outs[2 * n:3 * n], outs[3 * n:4 * n], outs[-1]


def _split_wait(flight, after, name):
    sems, srcs, lands, _ = flight
    n = len(srcs)

    def body(*refs):
        land_refs, sem_refs = refs[n:2 * n], refs[2 * n:4 * n]
        me = _position()
        for a in range(n):
            seven = land_refs[a].at[pl.ds(0, N_DEV - 1)]
            copies = pltpu.make_async_remote_copy(
                src_ref=seven, dst_ref=seven, send_sem=sem_refs[2 * a], recv_sem=sem_refs[2 * a + 1],
                device_id=_peer(1, *me), device_id_type=MESH)
            copies.wait_send()
            copies.wait_recv()

    outs = pl.pallas_call(
        body, name=name,
        out_shape=tuple(pltpu.HBM(a.shape, a.dtype) for a in list(srcs) + list(lands)),
        in_specs=(HBM_SPEC,) * (2 * n) + (SEM_SPEC,) * (2 * n) + (pl.BlockSpec(memory_space=pl.ANY),),
        out_specs=(HBM_SPEC,) * (2 * n), input_output_aliases={a: a for a in range(2 * n)},
        compiler_params=pltpu.CompilerParams(has_side_effects=SPLIT_EFFECT),
    )(*srcs, *lands, *sems, after)
    return outs[:n], outs[n:]


def _pack_rows(arrays, lanes, dtype):
    flat = jnp.concatenate([a.astype(dtype).reshape(-1) for a in arrays])
    pad = (-flat.size) % (16 * lanes)
    if pad:
        flat = jnp.concatenate([flat, jnp.zeros((pad,), dtype)])
    return flat.reshape(-1, lanes)


def _unpack_rows(packed, shapes):
    n = packed.shape[0]
    flat = packed.reshape(n, -1)
    out, off = [], 0
    for shp in shapes:
        size = math.prod(shp)
        out.append(flat[:, off:off + size].reshape((n,) + tuple(shp)))
        off += size
    return out


def _unshard(g8, axis):
    moved = jnp.moveaxis(g8, 0, axis)
    shp = list(moved.shape)
    shp[axis:axis + 2] = [shp[axis] * shp[axis + 1]]
    return moved.reshape(shp)


def _split8(full, axis):
    shp = list(full.shape)
    shp[axis:axis + 1] = [N_DEV, shp[axis] // N_DEV]
    return jnp.moveaxis(full.reshape(shp), axis, 0)


def _my_shard(g, axis, me):
    size = g.shape[axis + 1] // N_DEV
    return lax.dynamic_slice_in_dim(g, me * size, size, axis=axis + 1)


BIG_WEIGHTS = ("ffn_w_up", "ffn_w_down", "attn_w_qkv", "attn_w_o", "ret_w_in", "ret_w_out", "pool_w")
LAYER_WEIGHTS = (
    (("ffn_w_up", 0, "cols"), ("ffn_w_down", 0, "rows"), ("pool_w", 0, "pool")),
    (("ffn_w_up", 1, "cols"), ("ffn_w_down", 1, "rows"), ("attn_w_qkv", 0, "cols"), ("attn_w_o", 0, "rows")),
    (("ffn_w_up", 2, "cols"), ("ffn_w_down", 2, "rows"), ("ret_w_in", 0, "cols"), ("ret_w_out", 0, "rows")),
    (("ffn_w_up", 3, "cols"), ("ffn_w_down", 3, "rows"), ("pool_w", 1, "pool")),
)


GRAD_GROUPS = {"3": LAYER_WEIGHTS[3], "2": LAYER_WEIGHTS[2], "1": LAYER_WEIGHTS[1],
               "0ffn": LAYER_WEIGHTS[0][:2], "0mix": LAYER_WEIGHTS[0][2:]}


def _shard_to_send(w, kind):
    w = w.astype(BF16)
    return w.T if kind == "cols" else w


def _full_from_land(land, kind):
    return _unshard(land, 1) if kind == "pool" else land.reshape(-1, land.shape[-1])


def _grad_to_send(g, kind):
    return _split8(g, 1).astype(BF16) if kind == "pool" else g.astype(BF16).reshape(N_DEV, -1, g.shape[-1])


def _shard_grad(gsum, kind):
    return gsum.T if kind == "cols" else gsum
SMALL_SHARDED = (("norm_w", 2), ("pool_b", 1), ("pool_scale", 1), ("ret_gn_w", 1), ("ffn_conv_w", 2))
REPLICATED = ("ada_b", "attn_q_gain", "attn_k_gain", "ret_decay_logit", "ffn_conv_b")
WEIGHT_ORDER = ("c_ctx", "ada_w", "ada_b", "norm_w", "pool_w", "pool_b", "pool_scale", "attn_w_qkv",
                "attn_q_gain", "attn_k_gain", "attn_w_o", "ret_w_in", "ret_decay_logit", "ret_gn_w",
                "ret_w_out", "ffn_w_up", "ffn_conv_w", "ffn_conv_b", "ffn_w_down")


def _local_step(x0, target, mods, P, get_weights, put_grads, s_len, l_len):
    n_lat = s_len // ROW_TILE
    nw = P["norm_w"]
    lgt = jnp.broadcast_to(P["ret_decay_logit"][0][:, :, None, None], (2, RET_HEADS, 1, 128))
    cos, sin = _rope_tables(s_len, l_len)
    h_dtype = [F32 if i % 3 == 0 else BF16 for i in range(DEPTH)]
    saved = []
    mods = list(mods)
    X = x0
    h = _res_norm(X, None, None, 0, nw[0, 0], mods[0], 0, h_dtype[0], n_lat, "norm_first")
    for i in range(DEPTH):
        kind, j, mod = i % 3, i // 3, mods[i]
        W = get_weights(i, X)
        sv = {"X": X, "h": h, "W": W}
        if kind == 0:
            y = _pool_fwd(h, W["pool_w"], P["pool_b"][j:j + 1], P["pool_scale"][j:j + 1],
                          n_lat, s_len, l_len, f"pool_fwd{i}")
        elif kind == 1:
            qkv = _mm(h, W["attn_w_qkv"], "nt", F32, f"qkv{i}")
            q, k, v = _qk_prep_fwd(qkv, P["attn_q_gain"][j:j + 1], P["attn_k_gain"][j:j + 1], cos, sin)
            o, lse = _flash_fwd(q, k, v, s_len, l_len)
            y = _mm(o, W["attn_w_o"], "nn", F32, f"attn_out{i}")
            sv.update(qkv=qkv, q=q, k=k, v=v, o=o, lse=lse)
        else:
            proj = _mm(h, W["ret_w_in"], "nt", F32, f"ret_in{i}")
            o2, states = _ret_fwd(proj, lgt, s_len, l_len)
            gated = _readout_fwd(o2, proj, P["ret_gn_w"][j:j + 1], n_lat)
            y = _mm(gated, W["ret_w_out"], "nn", F32, f"ret_out{i}")
            sv.update(proj=proj, o2=o2, states=states, gated=gated)
        X1, h2 = _res_norm(X, y, mod, 0, nw[i, 1], mod, 1, BF16, n_lat, f"res_norm_mid{i}")
        u = _mm(h2, W["ffn_w_up"], "nt", FFN_HIDDEN_DTYPE, f"ffn_up{i}")
        gact = _conv_gate_fwd(u, P["ffn_conv_w"][i], P["ffn_conv_b"][i:i + 1], n_lat, f"conv_gate_fwd{i}")
        f = _mm(gact, W["ffn_w_down"], "nn", F32, f"ffn_down{i}")
        sv.update(y=y, X1=X1, h2=h2, u=u, gact=gact, f=f)
        saved.append(sv)
        if i + 1 < DEPTH:
            X, h = _res_norm(X1, f, mod, 1, nw[i + 1, 0], mods[i + 1], 0, h_dtype[i + 1], n_lat,
                             f"res_norm_end{i}")
        else:
            X = _res_norm(X1, f, mod, 1, None, None, 0, None, n_lat, "res_last")

    dX, loss = _loss_bwd(X, target, n_lat)
    G = {name: [None] * P[name].shape[0] for name in
         ("pool_b", "pool_scale", "attn_q_gain", "attn_k_gain", "ret_decay_logit", "ret_gn_w", "ffn_conv_w",
          "ffn_conv_b")}
    dnw = [[None, None] for _ in range(DEPTH)]
    dmods = [None] * DEPTH
    for i in reversed(range(DEPTH)):
        kind, j, mod, sv = i % 3, i // 3, mods[i], saved[i]
        W, gl = sv["W"], {}
        if i == DEPTH - 1:
            df, dg2 = _gate_bwd(dX, sv["f"], mod, 1, BF16, n_lat, f"gate_bwd_ffn{i}")
        dgact = _mm(df, W["ffn_w_down"], "nt", FFN_HIDDEN_DTYPE, f"ffn_down_dx{i}")
        gl["ffn_w_down"] = _mm(sv["gact"], df, "tn", BF16, f"ffn_down_dw{i}")
        du, dcw, dcb = _conv_gate_bwd(sv["u"], dgact, P["ffn_conv_w"][i], P["ffn_conv_b"][i:i + 1], n_lat,
                                      f"conv_gate_bwd{i}")
        G["ffn_conv_w"][i], G["ffn_conv_b"][i] = dcw, dcb[0]
        dh2 = _mm(du, W["ffn_w_up"], "nn", F32, f"ffn_up_dx{i}")
        gl["ffn_w_up"] = _mm(du, sv["h2"], "tn", BF16, f"ffn_up_dw{i}")
        if i == 0:
            mod = mod + put_grads("0ffn", gl)
        dX1, dnw[i][1], dsh2, dsc2, dy, dg1 = _norm_bwd(
            dX, dh2, sv["X1"], nw[i, 1], mod, 1, n_lat, f"norm_bwd_ffn{i}",
            gated=(sv["y"], mod, 0, F32 if kind == 0 else BF16))
        h = sv["h"]
        if kind == 0:
            dh, dpw, dpb, dps = _pool_bwd(h, dy, W["pool_w"], P["pool_b"][j:j + 1], P["pool_scale"][j:j + 1],
                                          n_lat, s_len, l_len, f"pool_bwd{i}")
            gl["pool_w"], G["pool_b"][j], G["pool_scale"][j] = dpw, dpb[0], dps[0]
        elif kind == 1:
            do = _mm(dy, W["attn_w_o"], "nt", F32, f"attn_out_dx{i}")
            gl["attn_w_o"] = _mm(sv["o"], dy, "tn", BF16, f"attn_out_dw{i}")
            dq, dk, dv = _flash_bwd(sv["q"], sv["k"], sv["v"], sv["o"], sv["lse"], do, s_len, l_len)
            dqkv, dqg, dkg = _qk_prep_bwd(sv["qkv"], dq, dk, dv, P["attn_q_gain"][j:j + 1],
                                          P["attn_k_gain"][j:j + 1], cos, sin)
            G["attn_q_gain"][j], G["attn_k_gain"][j] = dqg[0], dkg[0]
            dh = _mm(dqkv, W["attn_w_qkv"], "nn", F32, f"qkv_dx{i}")
            gl["attn_w_qkv"] = _mm(dqkv, h, "tn", BF16, f"qkv_dw{i}")
        else:
            dgated = _mm(dy, W["ret_w_out"], "nt", F32, f"ret_out_dx{i}")
            gl["ret_w_out"] = _mm(sv["gated"], dy, "tn", BF16, f"ret_out_dw{i}")
            do, dg, dgn = _readout_bwd(sv["o2"], sv["proj"], P["ret_gn_w"][j:j + 1], dgated, n_lat)
            dq2, dk2, dv2, dlg = _ret_bwd(sv["proj"], lgt, sv["states"], do, s_len, l_len)
            dproj = _ret_dproj(dq2, dk2, dv2, dg, n_lat)
            G["ret_gn_w"][j], G["ret_decay_logit"][j] = dgn[0], dlg[:, :, 0, 0]
            dh = _mm(dproj, W["ret_w_in"], "nn", F32, f"ret_in_dx{i}")
            gl["ret_w_in"] = _mm(dproj, h, "tn", BF16, f"ret_in_dw{i}")
        zero = put_grads(str(i) if i > 0 else "0mix", gl)
        if i > 0:
            mods[i - 1] = mods[i - 1] + zero
            dX, dnw[i][0], dsh1, dsc1, df_below, dg2_below = _norm_bwd(
                dX1, dh, sv["X"], nw[i, 0], mod, 0, n_lat, f"norm_bwd_mix{i}",
                gated=(saved[i - 1]["f"], mods[i - 1], 1, BF16))
        else:
            dX, dnw[i][0], dsh1, dsc1 = _norm_bwd(dX1, dh, sv["X"], nw[i, 0], mod, 0, n_lat, f"norm_bwd_mix{i}")
        dmods[i] = jnp.concatenate([dsh1, dsc1, dg1, dsh2, dsc2, dg2], axis=1)
        if i > 0:
            df, dg2 = df_below, dg2_below
    grads = {name: jnp.stack(parts) for name, parts in G.items()}
    grads["norm_w"] = jnp.stack([jnp.concatenate(pair, axis=0) for pair in dnw])
    return loss, dX, grads, jnp.stack(dmods)


def kernel(x, c, ctx, c_ctx, ada_w, ada_b, norm_w, pool_w, pool_b, pool_scale, attn_w_qkv, attn_q_gain,
           attn_k_gain, attn_w_o, ret_w_in, ret_decay_logit, ret_gn_w, ret_w_out, ffn_w_up, ffn_conv_w,
           ffn_conv_b, ffn_w_down, loss_target, m_c_ctx, m_ada_w, m_ada_b, m_norm_w, m_pool_w, m_pool_b,
           m_pool_scale, m_attn_w_qkv, m_attn_q_gain, m_attn_k_gain, m_attn_w_o, m_ret_w_in,
           m_ret_decay_logit, m_ret_gn_w, m_ret_w_out, m_ffn_w_up, m_ffn_conv_w, m_ffn_conv_b, m_ffn_w_down,
           v_c_ctx, v_ada_w, v_ada_b, v_norm_w, v_pool_w, v_pool_b, v_pool_scale, v_attn_w_qkv, v_attn_q_gain,
           v_attn_k_gain, v_attn_w_o, v_ret_w_in, v_ret_decay_logit, v_ret_gn_w, v_ret_w_out, v_ffn_w_up,
           v_ffn_conv_w, v_ffn_conv_b, v_ffn_w_down):
    A = dict(locals())
    me = _index(_position())
    s_len, D = x.shape[1], x.shape[2]
    l_len = ctx.shape[1]
    assert s_len % ROW_TILE == 0 and l_len % ROW_TILE == 0 and s_len % GRID_W == 0

    small = [A[n] for n, _ in SMALL_SHARDED]
    got = _gather_small(_pack_rows([c] + small, 128, F32), "gather_c_small")
    parts = _unpack_rows(got, [c.shape] + [a.shape for a in small])
    c_all = parts[0].reshape(N_DEV, D)
    P = {n: _unshard(g8, ax) for (n, ax), g8 in zip(SMALL_SHARDED, parts[1:])}

    c_rows = jnp.concatenate([c_all, c_ctx.reshape(1, D), jnp.zeros((7, D), F32)], axis=0)
    cols = ada_w.shape[2]
    ada_b_shard = lax.dynamic_slice_in_dim(ada_b, me * cols, cols, axis=1).reshape(DEPTH, 1, cols)
    mod_shard = _ada_fwd(c_rows, ada_w, ada_b_shard)
    got = _gather_small(mod_shard.reshape(-1, 128), "gather_mod").reshape(N_DEV, DEPTH, 16, cols)
    mod_lat = lax.dynamic_index_in_dim(got, me, axis=2, keepdims=False)
    mod_ctx = got[:, :, 8, :]
    mods = jnp.stack([jnp.moveaxis(mod_lat, 0, 1).reshape(DEPTH, 6, D),
                      jnp.moveaxis(mod_ctx, 0, 1).reshape(DEPTH, 6, D)], axis=1)

    shards = [[_shard_to_send(A[n][j], kind) for n, j, kind in lw] for lw in LAYER_WEIGHTS]
    pack0 = jnp.concatenate([s.reshape(-1, D) for s in shards[0]], axis=0)
    got0 = _gather_big(pack0, "gather_weights0")
    first, off = {}, 0
    for (n, j, kind), s in zip(LAYER_WEIGHTS[0], shards[0]):
        r = s.size // D
        first[n] = _full_from_land(got0[:, off:off + r].reshape((N_DEV,) + s.shape), kind)
        off += r
    flights, zero = {}, jnp.zeros((), F32)
    for i in range(1, DEPTH):
        flights[i] = _split_start(shards[i], True, f"gather_start{i}")
        zero = zero + flights[i][3][0, 0]
    mods = [mods[i] for i in range(DEPTH)]
    mods[0] = mods[0] + zero
    for n in REPLICATED:
        P[n] = A[n]

    def get_weights(i, x_now):
        if i == 0:
            return first
        owns, lands = _split_wait(flights[i], x_now, f"gather_wait{i}")
        return {n: _full_from_land(lax.dynamic_update_index_in_dim(land, own, me, axis=0), kind)
                for (n, j, kind), own, land in zip(LAYER_WEIGHTS[i], owns, lands)}

    sent = {}

    def put_grads(group, gl):
        sent[group] = _split_start([_grad_to_send(gl[n], kind) for n, j, kind in GRAD_GROUPS[group]], False,
                                   f"exchange_start_{group}")
        return sent[group][3][0, 0]

    x0 = jnp.concatenate([x[0], ctx[0]], axis=0)
    loss8, dx0, G, dmods = _local_step(x0, loss_target[0], mods, P, get_weights, put_grads, s_len, l_len)
    loss = lax.psum(loss8[0, 0], ("x", "y", "c"))
    grad_x = dx0[:s_len].reshape(x.shape)

    small_names = ["dmods"] + list(REPLICATED[1:]) + [n for n, _ in SMALL_SHARDED]
    small_parts = [dmods] + [G[n] for n in small_names[1:]]
    got = _gather_small(_pack_rows(small_parts, 128, F32), "gather_small_grads")
    S8 = dict(zip(small_names, _unpack_rows(got, [a.shape for a in small_parts])))

    dm = S8["dmods"].reshape(N_DEV, DEPTH, 2, 6 * D)
    dm_mine = lax.dynamic_slice_in_dim(dm, me * cols, cols, axis=3)
    g_ada_w, pc = _ada_bwd(c_rows, ada_w, jnp.moveaxis(dm_mine[:, :, 0], 0, 1), jnp.moveaxis(dm_mine[:, :, 1], 0, 1))
    pc8 = _gather_small(pc.reshape(-1, 128), "gather_c_ctx_grad").reshape(N_DEV, 8, D)

    def owner_sums(group, after):
        sends, lands = _split_wait(sent[group], after, f"exchange_wait_{group}")
        out = {}
        for (n, j, kind), send, land in zip(GRAD_GROUPS[group], sends, lands):
            own = lax.dynamic_index_in_dim(send, me, axis=0, keepdims=False)
            out[(n, j)] = _shard_grad(_sum_slots(own, land, f"sum_slots_{n}{j}"), kind)
        return out

    shard_grads = {}
    for group in ("3", "2", "1", "0mix"):
        shard_grads.update(owner_sums(group, pc8))

    g_in = {"c_ctx": pc8[:, 0, :], "ada_w": g_ada_w[None],
            "ada_b": jnp.moveaxis(dm, 2, 1).reshape(2 * N_DEV, DEPTH, 6 * D)}
    for n in REPLICATED[1:]:
        g_in[n] = S8[n]
    for n, ax in SMALL_SHARDED:
        g_in[n] = _my_shard(S8[n], ax, me)

    def stacked(n):
        return jnp.stack([shard_grads[(n, j)] for j in range(A[n].shape[0])])[None]

    late = [n for n, j, kind in GRAD_GROUPS["0ffn"]]
    for n in BIG_WEIGHTS:
        if n not in late:
            g_in[n] = stacked(n)
    res = {n: _adamw(A[n], g_in[n], A["m_" + n], A["v_" + n], "adamw_" + n) for n in WEIGHT_ORDER if n not in late}
    done = sum(res[n][1].reshape(-1)[0] for n in res)
    shard_grads.update(owner_sums("0ffn", done.reshape(1, 1)))
    for n in late:
        res[n] = _adamw(A[n], stacked(n), A["m_" + n], A["v_" + n], "adamw_" + n)
    outs = [loss, grad_x]
    for slot in range(4):
        outs += [res[n][slot] for n in WEIGHT_ORDER]
    return tuple(outs)
```

```python
import functools
import math

import jax
import jax.numpy as jnp
from jax import lax
from jax.experimental import pallas as pl
from jax.experimental.pallas import tpu as pltpu

F32 = jnp.float32
BF16 = jnp.bfloat16
SDS = jax.ShapeDtypeStruct
MESH = pl.DeviceIdType.MESH

N_DEV = 8
EPS = 1e-6
DEPTH = 4
GRID_W = 64
POOL_WINDOWS = (2, 4, 8, 16)
N_HEADS = 8
N_KV = 2
HEAD_DIM = 128
ROPE_THETA = 10000.0
RET_HEADS = 4
RET_DK = 256
RET_DV = 512
RET_CHUNK = 128
ADAM_LR = 0.001
ADAM_B1 = 0.9
ADAM_B2 = 0.999
ADAM_EPS = 1e-08
ADAM_WD = 0.01
ADAM_STEP = 10

ROW_TILE = 256
FFN_HIDDEN_DTYPE = BF16
FLASH_FWD_TILE = 128
HALO = 8
VMEM_LIMIT_V7X = 56 * 1024 * 1024


def _params(n_axes=0):
    sem = ("arbitrary",) * n_axes if n_axes else None
    return pltpu.CompilerParams(dimension_semantics=sem, vmem_limit_bytes=VMEM_LIMIT_V7X)


def _pick(n, cap, mult):
    best = None
    for d in range(mult, min(n, cap) + 1, mult):
        if n % d == 0:
            best = d
    return best if best is not None else n


def _dot(a, b):
    return jnp.dot(a, b, preferred_element_type=F32)


def _dot_nt(a, b):
    return lax.dot_general(a, b, (((1,), (1,)), ((), ())), preferred_element_type=F32)


def _dot_tn(a, b):
    return lax.dot_general(a, b, (((0,), (0,)), ((), ())), preferred_element_type=F32)


def _bf(v):
    return v.astype(BF16)


def _sigmoid(v):
    return 0.5 * jnp.tanh(0.5 * v) + 0.5


MM_VMEM_BUDGET = 40 * 1024 * 1024
MM_STEP_BYTES = 1 << 20
MM_ACC_PASS_BYTES = 8


def _divisors(n, mult, cap):
    return [d for d in range(mult, min(n, cap) + 1, mult) if n % d == 0] or [n]


def _mm_tiles(mode, M, N, K, a_item, b_item, o_item):
    best = None
    for tm in _divisors(M, 128 if mode == "tn" else 16, 2816):
        for tn in _divisors(N, 128, 2048):
            for tk in _divisors(K, 16 if mode == "tn" else 128, 2816):
                ni, nj, nk = M // tm, N // tn, K // tk
                vmem = 2 * (tm * tk * a_item + tk * tn * b_item + tm * tn * o_item) + tm * tn * 4
                if vmem > MM_VMEM_BUDGET:
                    continue
                a_reads = 1 if nk == 1 else nj
                b_reads = 1 if (nk == 1 and nj == 1) else ni
                cost = (M * K * a_item * a_reads + K * N * b_item * b_reads + M * N * o_item
                        + ni * nj * nk * MM_STEP_BYTES + (nk - 1) * M * N * MM_ACC_PASS_BYTES)
                if best is None or cost < best[0]:
                    best = (cost, tm, tn, tk)
    return best[1:]


def _mm(a, b, mode, out_dtype, name):
    if mode == "nn":
        (M, K), (K2, N) = a.shape, b.shape
    elif mode == "nt":
        (M, K), (N, K2) = a.shape, b.shape
    else:
        (K, M), (K2, N) = a.shape, b.shape
    assert K == K2, (a.shape, b.shape, mode)
    tm, tn, tk = _mm_tiles(mode, M, N, K, a.dtype.itemsize, b.dtype.itemsize, jnp.dtype(out_dtype).itemsize)
    nk = K // tk
    if mode == "nn":
        a_spec = pl.BlockSpec((tm, tk), lambda i, j, k: (i, k))
        b_spec = pl.BlockSpec((tk, tn), lambda i, j, k: (k, j))
    elif mode == "nt":
        a_spec = pl.BlockSpec((tm, tk), lambda i, j, k: (i, k))
        b_spec = pl.BlockSpec((tn, tk), lambda i, j, k: (j, k))
    else:
        a_spec = pl.BlockSpec((tk, tm), lambda i, j, k: (k, i))
        b_spec = pl.BlockSpec((tk, tn), lambda i, j, k: (k, j))
    dot = {"nn": _dot, "nt": _dot_nt, "tn": _dot_tn}[mode]

    def body(a_ref, b_ref, o_ref, acc_ref):
        part = dot(_bf(a_ref[...]), _bf(b_ref[...]))
        if nk == 1:
            o_ref[...] = part.astype(out_dtype)
        else:
            k = pl.program_id(2)

            @pl.when(k == 0)
            def _():
                acc_ref[...] = part

            @pl.when(k > 0)
            def _():
                acc_ref[...] += part

            @pl.when(k == nk - 1)
            def _():
                o_ref[...] = acc_ref[...].astype(out_dtype)

    return pl.pallas_call(
        body, name=name, grid=(M // tm, N // tn, nk),
        in_specs=[a_spec, b_spec],
        out_specs=pl.BlockSpec((tm, tn), lambda i, j, k: (i, j)),
        out_shape=SDS((M, N), out_dtype),
        scratch_shapes=[pltpu.VMEM((tm, tn), F32)],
        compiler_params=_params(3),
    )(a, b)


def _seg_spec(n_lat, d):
    return pl.BlockSpec((1, 6, d), lambda i: ((i >= n_lat).astype(jnp.int32), 0, 0))


def _seg_acc_spec(n_lat, d):
    return pl.BlockSpec((1, 1, d), lambda i: ((i >= n_lat).astype(jnp.int32), 0, 0))


def _res_norm(x, y, gmod, gk, nw, nmod, nk, h_dtype, n_lat, name):
    R, D = x.shape
    has_res, has_norm = y is not None, nw is not None
    row = pl.BlockSpec((ROW_TILE, D), lambda i: (i, 0))
    vec = pl.BlockSpec((1, D), lambda i: (0, 0))
    ins, specs, outs, ospecs = [x], [row], [], []
    if has_res:
        ins += [y, gmod]
        specs += [row, _seg_spec(n_lat, D)]
        outs.append(SDS((R, D), F32))
        ospecs.append(row)
    if has_norm:
        ins += [nw.reshape(1, D), nmod]
        specs += [vec, _seg_spec(n_lat, D)]
        outs.append(SDS((R, D), h_dtype))
        ospecs.append(row)

    def body(*refs):
        refs = list(refs)
        z = refs.pop(0)[...]
        if has_res:
            y_ref, g_ref = refs.pop(0), refs.pop(0)
            z = z + g_ref[0, pl.ds(3 * gk + 2, 1), :] * y_ref[...].astype(F32)
        if has_norm:
            nw_ref, m_ref = refs.pop(0), refs.pop(0)
        if has_res:
            refs.pop(0)[...] = z
        if has_norm:
            r = lax.rsqrt(jnp.mean(z * z, axis=-1, keepdims=True) + EPS)
            h = (z * r) * nw_ref[...]
            h = h * (1.0 + m_ref[0, pl.ds(3 * nk + 1, 1), :]) + m_ref[0, pl.ds(3 * nk, 1), :]
            refs.pop(0)[...] = h.astype(h_dtype)

    res = pl.pallas_call(
        body, name=name, grid=(R // ROW_TILE,), in_specs=specs, out_specs=ospecs,
        out_shape=outs, compiler_params=_params(1),
    )(*ins)
    return res if len(res) > 1 else res[0]


def _gate_bwd(dz, y, mod, k, out_dtype, n_lat, name):
    R, D = dz.shape
    row = pl.BlockSpec((ROW_TILE, D), lambda i: (i, 0))

    def body(dz_ref, y_ref, m_ref, dy_ref, dg_ref):
        i = pl.program_id(0)
        dzv = dz_ref[...]
        dy_ref[...] = (m_ref[0, pl.ds(3 * k + 2, 1), :] * dzv).astype(out_dtype)

        @pl.when((i == 0) | (i == n_lat))
        def _():
            dg_ref[...] = jnp.zeros_like(dg_ref)

        dg_ref[0] += jnp.sum(dzv * y_ref[...].astype(F32), axis=0, keepdims=True)

    return pl.pallas_call(
        body, name=name, grid=(R // ROW_TILE,),
        in_specs=[row, row, _seg_spec(n_lat, D)],
        out_specs=[row, _seg_acc_spec(n_lat, D)],
        out_shape=[SDS((R, D), out_dtype), SDS((2, 1, D), F32)],
        compiler_params=_params(1),
    )(dz, y, mod)


def _norm_bwd(dz, dh, x, nw, mod, k, n_lat, name, gated=None):
    R, D = x.shape
    row = pl.BlockSpec((ROW_TILE, D), lambda i: (i, 0))
    vec = pl.BlockSpec((1, D), lambda i: (0, 0))
    ins, specs = [dz, dh, x, nw.reshape(1, D), mod], [row, row, row, vec, _seg_spec(n_lat, D)]
    outs = [SDS((R, D), F32), SDS((1, D), F32), SDS((2, 1, D), F32), SDS((2, 1, D), F32)]
    ospecs = [row, vec, _seg_acc_spec(n_lat, D), _seg_acc_spec(n_lat, D)]
    if gated is not None:
        y, gmod, gk, dy_dtype = gated
        ins += [y, gmod]
        specs += [row, _seg_spec(n_lat, D)]
        outs += [SDS((R, D), dy_dtype), SDS((2, 1, D), F32)]
        ospecs += [row, _seg_acc_spec(n_lat, D)]

    def body(dz_ref, dh_ref, x_ref, nw_ref, m_ref, *rest):
        if gated is not None:
            y_ref, g_ref, dx_ref, dnw_ref, dsh_ref, dsc_ref, dy_ref, dg_ref = rest
        else:
            dx_ref, dnw_ref, dsh_ref, dsc_ref = rest
        i = pl.program_id(0)
        xv = x_ref[...]
        dhv = dh_ref[...].astype(F32)
        nwv = nw_ref[...]
        sc1 = 1.0 + m_ref[0, pl.ds(3 * k + 1, 1), :]
        r = lax.rsqrt(jnp.mean(xv * xv, axis=-1, keepdims=True) + EPS)
        xhat = xv * r
        a = dhv * (nwv * sc1)
        dx = dz_ref[...] + r * (a - xhat * jnp.mean(a * xhat, axis=-1, keepdims=True))
        dx_ref[...] = dx

        @pl.when(i == 0)
        def _():
            dnw_ref[...] = jnp.zeros_like(dnw_ref)

        @pl.when((i == 0) | (i == n_lat))
        def _():
            dsh_ref[...] = jnp.zeros_like(dsh_ref)
            dsc_ref[...] = jnp.zeros_like(dsc_ref)
            if gated is not None:
                dg_ref[...] = jnp.zeros_like(dg_ref)

        dnw_ref[...] += jnp.sum(dhv * xhat, axis=0, keepdims=True) * sc1
        dsh_ref[0] += jnp.sum(dhv, axis=0, keepdims=True)
        dsc_ref[0] += jnp.sum(dhv * xhat, axis=0, keepdims=True) * nwv
        if gated is not None:
            dy_ref[...] = (g_ref[0, pl.ds(3 * gk + 2, 1), :] * dx).astype(dy_dtype)
            dg_ref[0] += jnp.sum(dx * y_ref[...].astype(F32), axis=0, keepdims=True)

    return pl.pallas_call(
        body, name=name, grid=(R // ROW_TILE,), in_specs=specs, out_specs=ospecs, out_shape=outs,
        compiler_params=_params(1),
    )(*ins)


def _loss_bwd(xf, target, n_lat):
    R, D = xf.shape
    row = pl.BlockSpec((ROW_TILE, D), lambda i: (i, 0))
    tgt = pl.BlockSpec((ROW_TILE, D), lambda i: (jnp.minimum(i, n_lat - 1), 0))

    def body(x_ref, t_ref, dx_ref, loss_ref):
        i = pl.program_id(0)
        e = jnp.where(i < n_lat, x_ref[...] - t_ref[...], 0.0)
        dx_ref[...] = e * (1.0 / D)

        @pl.when(i == 0)
        def _():
            loss_ref[...] = jnp.zeros_like(loss_ref)

        loss_ref[...] += 0.5 * jnp.sum(jnp.mean(e * e, axis=-1, keepdims=True))

    return pl.pallas_call(
        body, name="loss_bwd", grid=(R // ROW_TILE,),
        in_specs=[row, tgt],
        out_specs=[row, pl.BlockSpec((8, 128), lambda i: (0, 0))],
        out_shape=[SDS((R, D), F32), SDS((8, 128), F32)],
        compiler_params=_params(1),
    )(xf, target)


def _halo_rows(dtype):
    return HALO * (4 // jnp.dtype(dtype).itemsize)


def _halo_specs(n_tiles, width, tile=ROW_TILE, rows=HALO):
    per = tile // rows
    prev = pl.BlockSpec((rows, width), lambda i: (jnp.maximum(i * per - 1, 0), 0))
    nxt = pl.BlockSpec((rows, width), lambda i: (jnp.minimum((i + 1) * per, n_tiles * per - 1), 0))
    return prev, nxt


SHIFT_K = 256


def _shift_matrix(n_out, first_row, deltas):
    half = n_out // 2
    out = []
    for h, start in enumerate((0, 2 * _halo_rows(BF16))):
        r = lax.broadcasted_iota(jnp.int32, (half, SHIFT_K), 0) + (first_row + h * half - start)
        j = lax.broadcasted_iota(jnp.int32, (half, SHIFT_K), 1)
        out.append(jnp.concatenate([(j == r + d).astype(F32) for d in deltas], axis=0).astype(BF16))
    return out


def _shifted_rows(t_ref, p_ref, n_ref, cols, first, last, picks, n_blocks):
    pr = jnp.where(first, jnp.zeros_like(p_ref[:, cols]), p_ref[:, cols])
    nx = jnp.where(last, jnp.zeros_like(n_ref[:, cols]), n_ref[:, cols])
    e = jnp.concatenate([pr, t_ref[:, cols], nx], axis=0)
    start = 2 * pr.shape[0]
    top, bot = _dot(picks[0], e[0:SHIFT_K]), _dot(picks[1], e[start:start + SHIFT_K])
    half = picks[0].shape[0] // n_blocks
    return [jnp.concatenate([top[k * half:(k + 1) * half], bot[k * half:(k + 1) * half]], axis=0)
            for k in range(n_blocks)]


def _edge_flags(i, n_lat, n_tiles):
    first = (i == 0) | (i == n_lat)
    last = (i == n_lat - 1) | (i == n_tiles - 1)
    return first, last


def _conv_gate_fwd(u, conv_w, conv_b, n_lat, name):
    R, F2 = u.shape
    F = F2 // 2
    n_tiles = R // ROW_TILE
    T = ROW_TILE
    cw = _pick(F, 256, 128)
    row = pl.BlockSpec((T, F2), lambda i: (i, 0))
    prev, nxt = _halo_specs(n_tiles, F2, rows=_halo_rows(u.dtype))

    assert u.dtype == BF16

    def body(u_ref, p_ref, n_ref, w_ref, b_ref, o_ref):
        i = pl.program_id(0)
        first, last = _edge_flags(i, n_lat, n_tiles)
        taps = _shift_matrix(T, _halo_rows(BF16), (-1, 0, 1))

        def conv(c0):
            cols = pl.ds(c0, cw)
            up, uv, un = _shifted_rows(u_ref, p_ref, n_ref, cols, first, last, taps, 3)
            return (up * w_ref[pl.ds(0, 1), cols] + uv * w_ref[pl.ds(1, 1), cols]
                    + un * w_ref[pl.ds(2, 1), cols] + b_ref[:, cols])

        for c0 in range(0, F, cw):
            ca, cv = conv(c0), conv(F + c0)
            o_ref[:, pl.ds(c0, cw)] = (ca * _sigmoid(ca) * cv).astype(BF16)

    return pl.pallas_call(
        body, name=name, grid=(n_tiles,),
        in_specs=[row, prev, nxt, pl.BlockSpec((3, F2), lambda i: (0, 0)),
                  pl.BlockSpec((1, F2), lambda i: (0, 0))],
        out_specs=pl.BlockSpec((T, F), lambda i: (i, 0)),
        out_shape=SDS((R, F), BF16), compiler_params=_params(1),
    )(u, u, u, conv_w, conv_b)


def _conv_gate_bwd(u, dgact, conv_w, conv_b, n_lat, name):
    R, F2 = u.shape
    F = F2 // 2
    n_tiles = R // ROW_TILE
    T, N = ROW_TILE, ROW_TILE + 2 * HALO
    cw = _pick(F, 256, 128)
    rowu = pl.BlockSpec((T, F2), lambda i: (i, 0))
    rowg = pl.BlockSpec((T, F), lambda i: (i, 0))
    pu, nu = _halo_specs(n_tiles, F2, rows=_halo_rows(u.dtype))
    pg, ng = _halo_specs(n_tiles, F, rows=_halo_rows(dgact.dtype))

    assert u.dtype == BF16 and dgact.dtype == BF16

    def body(u_ref, pu_ref, nu_ref, g_ref, pg_ref, ng_ref, w_ref, b_ref, du_ref, dw_ref, db_ref):
        i = pl.program_id(0)
        first, last = _edge_flags(i, n_lat, n_tiles)

        @pl.when(i == 0)
        def _():
            dw_ref[...] = jnp.zeros_like(dw_ref)
            db_ref[...] = jnp.zeros_like(db_ref)

        taps = _shift_matrix(N, _halo_rows(BF16) - HALO, (-1, 0, 1))
        same = _shift_matrix(N, _halo_rows(BF16) - HALO, (0,))

        def conv(c0):
            cols = pl.ds(c0, cw)
            up, e, un = _shifted_rows(u_ref, pu_ref, nu_ref, cols, first, last, taps, 3)
            c = (up * w_ref[pl.ds(0, 1), cols] + e * w_ref[pl.ds(1, 1), cols]
                 + un * w_ref[pl.ds(2, 1), cols] + b_ref[:, cols])
            return c, up, e, un

        def back(c0, dc, up, e, un):
            cols = pl.ds(c0, cw)
            du = (pltpu.roll(dc, N - 1, 0) * w_ref[pl.ds(0, 1), cols] + dc * w_ref[pl.ds(1, 1), cols]
                  + pltpu.roll(dc, 1, 0) * w_ref[pl.ds(2, 1), cols])
            du_ref[:, cols] = du[HALO:HALO + T].astype(BF16)
            dct = dc[HALO:HALO + T]
            dw_ref[pl.ds(0, 1), cols] += jnp.sum(dct * up[HALO:HALO + T], axis=0, keepdims=True)
            dw_ref[pl.ds(1, 1), cols] += jnp.sum(dct * e[HALO:HALO + T], axis=0, keepdims=True)
            dw_ref[pl.ds(2, 1), cols] += jnp.sum(dct * un[HALO:HALO + T], axis=0, keepdims=True)
            db_ref[:, cols] += jnp.sum(dct, axis=0, keepdims=True)

        for c0 in range(0, F, cw):
            dg, = _shifted_rows(g_ref, pg_ref, ng_ref, pl.ds(c0, cw), first, last, same, 1)
            ca, upa, ea, una = conv(c0)
            cv, upv, ev, unv = conv(F + c0)
            s = _sigmoid(ca)
            back(F + c0, dg * (ca * s), upv, ev, unv)
            back(c0, dg * cv * (s * (1.0 + ca * (1.0 - s))), upa, ea, una)

    return pl.pallas_call(
        body, name=name, grid=(n_tiles,),
        in_specs=[rowu, pu, nu, rowg, pg, ng, pl.BlockSpec((3, F2), lambda i: (0, 0)),
                  pl.BlockSpec((1, F2), lambda i: (0, 0))],
        out_specs=[rowu, pl.BlockSpec((3, F2), lambda i: (0, 0)), pl.BlockSpec((1, F2), lambda i: (0, 0))],
        out_shape=[SDS((R, F2), BF16), SDS((3, F2), F32), SDS((1, F2), F32)],
        compiler_params=_params(1),
    )(u, u, u, dgact, dgact, dgact, conv_w, conv_b)


def _pool_counts(i, n_lat, s_len, l_len, n_rows, offset):
    ctx = i >= n_lat
    t0 = jnp.where(ctx, i - n_lat, i) * ROW_TILE + offset
    seg = jnp.where(ctx, l_len, s_len)
    t = t0 + lax.broadcasted_iota(jnp.int32, (n_rows, 1), 0)
    out = []
    for win in POOL_WINDOWS:
        cnt = jnp.minimum(t + win // 2, seg) - jnp.maximum(t - win // 2, 0)
        out.append(jnp.maximum(cnt, 1).astype(F32))
    return out


def _window_sum(e, lo, hi, n):
    acc = None
    for j in range(lo, hi + 1):
        term = e if j == 0 else pltpu.roll(e, (-j) % n, 0)
        acc = term if acc is None else acc + term
    return acc


def _pool_fwd(h, w, b, scale, n_lat, s_len, l_len, name):
    R, D = h.shape
    G = D // 4
    n_tiles = R // ROW_TILE
    T, N = ROW_TILE, ROW_TILE + 2 * HALO
    row = pl.BlockSpec((T, D), lambda i: (i, 0))
    prev, nxt = _halo_specs(n_tiles, D)
    vec = pl.BlockSpec((1, D), lambda i: (0, 0))

    def body(h_ref, p_ref, n_ref, w_ref, b_ref, s_ref, y_ref):
        i = pl.program_id(0)
        first, last = _edge_flags(i, n_lat, n_tiles)
        cnts = _pool_counts(i, n_lat, s_len, l_len, T, 0)
        for g, win in enumerate(POOL_WINDOWS):
            cols = pl.ds(g * G, G)
            pr = jnp.where(first, 0.0, p_ref[:, cols])
            nx = jnp.where(last, 0.0, n_ref[:, cols])
            hv = h_ref[:, cols]
            e = jnp.concatenate([pr, hv, nx], axis=0)
            mean = _window_sum(e, -(win // 2), win // 2 - 1, N)[HALO:HALO + T] / cnts[g]
            yg = _dot(_bf(mean - hv), w_ref[g])
            y_ref[:, cols] = (yg + b_ref[:, cols]) * s_ref[:, cols]

    return pl.pallas_call(
        body, name=name, grid=(n_tiles,),
        in_specs=[row, prev, nxt, pl.BlockSpec((4, G, G), lambda i: (0, 0, 0)), vec, vec],
        out_specs=row, out_shape=SDS((R, D), F32), compiler_params=_params(1),
    )(h, h, h, w, b, scale)


def _pool_bwd(h, dy, w, b, scale, n_lat, s_len, l_len, name):
    R, D = h.shape
    G = D // 4
    n_tiles = R // ROW_TILE
    T, N = ROW_TILE, ROW_TILE + 2 * HALO
    row = pl.BlockSpec((T, D), lambda i: (i, 0))
    prev, nxt = _halo_specs(n_tiles, D)
    vec = pl.BlockSpec((1, D), lambda i: (0, 0))
    wspec = pl.BlockSpec((4, G, G), lambda i: (0, 0, 0))

    def body(h_ref, ph_ref, nh_ref, d_ref, pd_ref, nd_ref, w_ref, b_ref, s_ref,
             dh_ref, dw_ref, db_ref, ds_ref):
        i = pl.program_id(0)
        first, last = _edge_flags(i, n_lat, n_tiles)

        @pl.when(i == 0)
        def _():
            dw_ref[...] = jnp.zeros_like(dw_ref)
            db_ref[...] = jnp.zeros_like(db_ref)
            ds_ref[...] = jnp.zeros_like(ds_ref)

        cnts = _pool_counts(i, n_lat, s_len, l_len, T, 0)
        cnts_ext = _pool_counts(i, n_lat, s_len, l_len, N, -HALO)
        for g, win in enumerate(POOL_WINDOWS):
            cols = pl.ds(g * G, G)

            def ext(t_ref, p_ref, n_ref):
                pr = jnp.where(first, 0.0, p_ref[:, cols])
                nx = jnp.where(last, 0.0, n_ref[:, cols])
                return jnp.concatenate([pr, t_ref[:, cols], nx], axis=0)

            hv = h_ref[:, cols]
            mean = _window_sum(ext(h_ref, ph_ref, nh_ref), -(win // 2), win // 2 - 1, N)[HALO:HALO + T] / cnts[g]
            z = _bf(mean - hv)
            sc = s_ref[:, cols]
            dye = ext(d_ref, pd_ref, nd_ref)
            dt = _bf(dye * sc)
            dz = _dot_nt(dt, w_ref[g])
            dm = dz / cnts_ext[g]
            dh = _window_sum(dm, -(win // 2 - 1), win // 2, N) - dz
            dh_ref[:, cols] = dh[HALO:HALO + T]
            dyt = dye[HALO:HALO + T]
            dw_ref[g] += _dot_tn(z, dt[HALO:HALO + T])
            db_ref[:, cols] += jnp.sum(dyt * sc, axis=0, keepdims=True)
            ds_ref[:, cols] += jnp.sum(dyt * (_dot(z, w_ref[g]) + b_ref[:, cols]), axis=0, keepdims=True)

    return pl.pallas_call(
        body, name=name, grid=(n_tiles,),
        in_specs=[row, prev, nxt, row, prev, nxt, wspec, vec, vec],
        out_specs=[row, wspec, vec, vec],
        out_shape=[SDS((R, D), F32), SDS((4, G, G), F32), SDS((1, D), F32), SDS((1, D), F32)],
        compiler_params=_params(1),
    )(h, h, h, dy, dy, dy, w, b, scale)


def _rope_tables(s_len, l_len):
    t = jnp.arange(s_len)
    row = (t // GRID_W).astype(F32)
    col = (t % GRID_W).astype(F32)
    axis_dim = HEAD_DIM // 2
    inv = ROPE_THETA ** (-jnp.arange(0, axis_dim, 2, dtype=F32) / axis_dim)
    ar, ac = row[:, None] * inv, col[:, None] * inv
    cos = jnp.concatenate([jnp.cos(ar), jnp.cos(ar), jnp.cos(ac), jnp.cos(ac)], axis=-1)
    sin = jnp.concatenate([-jnp.sin(ar), jnp.sin(ar), -jnp.sin(ac), jnp.sin(ac)], axis=-1)
    cos = jnp.concatenate([cos, jnp.ones((l_len, HEAD_DIM), F32)], axis=0)
    sin = jnp.concatenate([sin, jnp.zeros((l_len, HEAD_DIM), F32)], axis=0)
    return cos, sin


def _swap_halves(v):
    lane = lax.broadcasted_iota(jnp.int32, v.shape, 1)
    return jnp.where((lane % 64) < 32, pltpu.roll(v, 96, 1), pltpu.roll(v, 32, 1))


def _qk_prep_fwd(qkv, q_gain, k_gain, cos, sin):
    R = qkv.shape[0]
    NQ, NK = N_HEADS * HEAD_DIM, N_KV * HEAD_DIM
    T = ROW_TILE
    vec = pl.BlockSpec((1, HEAD_DIM), lambda i: (0, 0))
    tab = pl.BlockSpec((T, HEAD_DIM), lambda i: (i, 0))

    def body(x_ref, qg_ref, kg_ref, c_ref, s_ref, q_ref, k_ref, v_ref):
        cosv, sinv = c_ref[...], s_ref[...]

        def prep(c0, gain):
            xh = x_ref[:, pl.ds(c0, HEAD_DIM)]
            xn = xh * lax.rsqrt(jnp.mean(xh * xh, axis=-1, keepdims=True) + EPS) * gain
            return _bf(xn * cosv + _swap_halves(xn) * sinv)

        for hd in range(N_HEADS):
            q_ref[:, pl.ds(hd * HEAD_DIM, HEAD_DIM)] = prep(hd * HEAD_DIM, qg_ref[...])
        for hd in range(N_KV):
            k_ref[:, pl.ds(hd * HEAD_DIM, HEAD_DIM)] = prep(NQ + hd * HEAD_DIM, kg_ref[...])
            v_ref[:, pl.ds(2 * hd * HEAD_DIM, HEAD_DIM)] = _bf(x_ref[:, pl.ds(NQ + NK + hd * HEAD_DIM, HEAD_DIM)])
            v_ref[:, pl.ds((2 * hd + 1) * HEAD_DIM, HEAD_DIM)] = jnp.ones((T, HEAD_DIM), BF16)

    return pl.pallas_call(
        body, name="qk_prep_fwd", grid=(R // T,),
        in_specs=[pl.BlockSpec((T, NQ + 2 * NK), lambda i: (i, 0)), vec, vec, tab, tab],
        out_specs=[pl.BlockSpec((T, NQ), lambda i: (i, 0)), pl.BlockSpec((T, NK), lambda i: (i, 0)),
                   pl.BlockSpec((T, 2 * NK), lambda i: (i, 0))],
        out_shape=[SDS((R, NQ), BF16), SDS((R, NK), BF16), SDS((R, 2 * NK), BF16)],
        compiler_params=_params(1),
    )(qkv, q_gain, k_gain, cos, sin)


def _qk_prep_bwd(qkv, dq, dk, dv, q_gain, k_gain, cos, sin):
    R = qkv.shape[0]
    NQ, NK = N_HEADS * HEAD_DIM, N_KV * HEAD_DIM
    T = ROW_TILE
    vec = pl.BlockSpec((1, HEAD_DIM), lambda i: (0, 0))
    tab = pl.BlockSpec((T, HEAD_DIM), lambda i: (i, 0))

    def body(x_ref, dq_ref, dk_ref, dv_ref, qg_ref, kg_ref, c_ref, s_ref, o_ref, dqg_ref, dkg_ref):
        i = pl.program_id(0)
        cosv, sinv = c_ref[...], s_ref[...]

        @pl.when(i == 0)
        def _():
            dqg_ref[...] = jnp.zeros_like(dqg_ref)
            dkg_ref[...] = jnp.zeros_like(dkg_ref)

        def back(c0, dout, gain, dg_ref):
            xh = x_ref[:, pl.ds(c0, HEAD_DIM)]
            r = lax.rsqrt(jnp.mean(xh * xh, axis=-1, keepdims=True) + EPS)
            xhat = xh * r
            dxn = dout * cosv + _swap_halves(dout * sinv)
            dg_ref[...] += jnp.sum(dxn * xhat, axis=0, keepdims=True)
            a = dxn * gain
            o_ref[:, pl.ds(c0, HEAD_DIM)] = _bf(r * (a - xhat * jnp.mean(a * xhat, axis=-1, keepdims=True)))

        for hd in range(N_HEADS):
            back(hd * HEAD_DIM, dq_ref[:, pl.ds(hd * HEAD_DIM, HEAD_DIM)], qg_ref[...], dqg_ref)
        for hd in range(N_KV):
            back(NQ + hd * HEAD_DIM, dk_ref[:, pl.ds(hd * HEAD_DIM, HEAD_DIM)], kg_ref[...], dkg_ref)
        o_ref[:, pl.ds(NQ + NK, NK)] = _bf(dv_ref[...])

    return pl.pallas_call(
        body, name="qk_prep_bwd", grid=(R // T,),
        in_specs=[pl.BlockSpec((T, NQ + 2 * NK), lambda i: (i, 0)), pl.BlockSpec((T, NQ), lambda i: (i, 0)),
                  pl.BlockSpec((T, NK), lambda i: (i, 0)), pl.BlockSpec((T, NK), lambda i: (i, 0)),
                  vec, vec, tab, tab],
        out_specs=[pl.BlockSpec((T, NQ + 2 * NK), lambda i: (i, 0)), vec, vec],
        out_shape=[SDS((R, NQ + 2 * NK), BF16), SDS((1, HEAD_DIM), F32), SDS((1, HEAD_DIM), F32)],
        compiler_params=_params(1),
    )(qkv, dq, dk, dv, q_gain, k_gain, cos, sin)


def _flash_fwd(q, k, v, s_len, l_len):
    R = q.shape[0]
    T = FLASH_FWD_TILE
    n_lat = s_len // T
    ck = _pick(s_len, 512, 128)
    scale = HEAD_DIM ** -0.5
    group = N_HEADS // N_KV
    GW = group * HEAD_DIM
    M = group * T
    chunks = s_len // ck
    to_log2 = scale * math.log2(math.e)

    def body(q_ref, k_ref, v_ref, o_ref, lse_ref, s_s, sc_s, ml_s, mb_s, acc_s):
        i = pl.program_id(1)
        qv = jnp.concatenate([q_ref[:, pl.ds(hh * HEAD_DIM, HEAD_DIM)] for hh in range(group)], axis=0)

        ml_s[...] = jnp.full_like(ml_s, -jnp.inf)

        def lane_max(s, n):
            m = ml_s[...]
            for t in range(n // HEAD_DIM):
                m = jnp.maximum(m, s[:, t * HEAD_DIM:(t + 1) * HEAD_DIM])
            ml_s[...] = m

        @pl.when(i < n_lat)
        def _():
            def loop(c, carry):
                s = _dot_nt(qv, k_ref[pl.ds(pl.multiple_of(c * ck, ck), ck), :])
                s_s[c] = s
                lane_max(s, ck)
                return carry
            lax.fori_loop(0, chunks, loop, 0, unroll=4 if chunks % 4 == 0 else 1)

        sc = _dot_nt(qv, k_ref[pl.ds(s_len, l_len), :])
        sc_s[...] = sc
        lane_max(sc, l_len)
        m_row = jnp.max(ml_s[...], axis=-1, keepdims=True) * to_log2
        mb_s[...] = jnp.broadcast_to(m_row, (M, ck))

        acc_s[...] = jnp.zeros_like(acc_s)

        @pl.when(i < n_lat)
        def _():
            def loop(c, carry):
                p = jnp.exp2(s_s[c] * to_log2 - mb_s[...])
                acc_s[...] += _dot(_bf(p), v_ref[pl.ds(pl.multiple_of(c * ck, ck), ck), :])
                return carry
            lax.fori_loop(0, chunks, loop, 0, unroll=4 if chunks % 4 == 0 else 1)

        p = jnp.exp2(sc_s[...] * to_log2 - mb_s[:, pl.ds(0, l_len)])
        acc_s[...] += _dot(_bf(p), v_ref[pl.ds(s_len, l_len), :])
        l_rep = acc_s[:, pl.ds(HEAD_DIM, HEAD_DIM)]
        o = acc_s[:, pl.ds(0, HEAD_DIM)] / l_rep
        for hh in range(group):
            o_ref[:, pl.ds(hh * HEAD_DIM, HEAD_DIM)] = o[hh * T:(hh + 1) * T]
        lse = (mb_s[:, pl.ds(0, HEAD_DIM)] + jnp.log2(l_rep)) * math.log(2.0)
        lse_ref[...] = jnp.max(lse, axis=-1, keepdims=True).reshape(group, T, 1)

    return pl.pallas_call(
        body, name="flash_fwd", grid=(N_KV, R // T),
        in_specs=[pl.BlockSpec((T, GW), lambda g, i: (i, g)),
                  pl.BlockSpec((R, HEAD_DIM), lambda g, i: (0, g)),
                  pl.BlockSpec((R, 2 * HEAD_DIM), lambda g, i: (0, g))],
        out_specs=[pl.BlockSpec((T, GW), lambda g, i: (i, g)),
                   pl.BlockSpec((group, T, 1), lambda g, i: (g, i, 0))],
        out_shape=[SDS((R, N_HEADS * HEAD_DIM), F32), SDS((N_HEADS, R, 1), F32)],
        scratch_shapes=[pltpu.VMEM((chunks, M, ck), F32), pltpu.VMEM((M, l_len), F32), pltpu.VMEM((M, HEAD_DIM), F32),
                        pltpu.VMEM((M, ck), F32), pltpu.VMEM((M, 2 * HEAD_DIM), F32)],
        compiler_params=_params(2),
    )(q, k, v)


def _flash_bwd(q, k, v, o, lse, do, s_len, l_len):
    R = q.shape[0]
    T = ROW_TILE
    n_lat = s_len // T
    ck = _pick(s_len, 512, 128)
    scale = HEAD_DIM ** -0.5
    group = N_HEADS // N_KV
    GW = group * HEAD_DIM
    qspec = pl.BlockSpec((T, GW), lambda g, i: (i, g))
    kspec = pl.BlockSpec((R, HEAD_DIM), lambda g, i: (0, g))

    M = group * T
    log2e = math.log2(math.e)

    def body(q_ref, do_ref, o_ref, lse_ref, k_ref, v_ref, dq_ref, dk_ref, dv_ref, dq_s, lse_s, delta_s):
        i = pl.program_id(1)

        @pl.when(i == 0)
        def _():
            dk_ref[...] = jnp.zeros_like(dk_ref)
            dv_ref[...] = jnp.zeros_like(dv_ref)

        def stacked(ref):
            return jnp.concatenate([ref[:, pl.ds(hh * HEAD_DIM, HEAD_DIM)] for hh in range(group)], axis=0)

        qv = stacked(q_ref)
        dov = stacked(do_ref)
        dob = _bf(dov)
        delta_s[...] = jnp.broadcast_to(jnp.sum(dov * stacked(o_ref), axis=-1, keepdims=True), (M, ck))
        lse_s[...] = jnp.broadcast_to(lse_ref[...].reshape(M, 1) * log2e, (M, ck))
        dq_s[...] = jnp.zeros_like(dq_s)

        def step(rows, n):
            kv, vv = k_ref[rows, :], v_ref[rows, :]
            p = jnp.exp2(_dot_nt(qv, kv) * (scale * log2e) - lse_s[:, pl.ds(0, n)])
            dv_ref[rows, :] += _dot_tn(_bf(p), dob)
            ds = _bf(p * (_dot_nt(dob, vv) - delta_s[:, pl.ds(0, n)]) * scale)
            dq_s[...] += _dot(ds, kv)
            dk_ref[rows, :] += _dot_tn(ds, qv)

        @pl.when(i < n_lat)
        def _():
            def loop(c, carry):
                step(pl.ds(pl.multiple_of(c * ck, ck), ck), ck)
                return carry
            lax.fori_loop(0, s_len // ck, loop, 0, unroll=4 if (s_len // ck) % 4 == 0 else 1)

        step(pl.ds(s_len, l_len), l_len)
        for hh in range(group):
            dq_ref[:, pl.ds(hh * HEAD_DIM, HEAD_DIM)] = dq_s[pl.ds(hh * T, T), :]

    return pl.pallas_call(
        body, name="flash_bwd", grid=(N_KV, R // T),
        in_specs=[qspec, qspec, qspec, pl.BlockSpec((group, T, 1), lambda g, i: (g, i, 0)), kspec,
                  pl.BlockSpec((R, HEAD_DIM), lambda g, i: (0, 2 * g))],
        out_specs=[qspec, kspec, kspec],
        out_shape=[SDS((R, N_HEADS * HEAD_DIM), F32), SDS((R, N_KV * HEAD_DIM), F32),
                   SDS((R, N_KV * HEAD_DIM), F32)],
        scratch_shapes=[pltpu.VMEM((M, HEAD_DIM), F32), pltpu.VMEM((M, ck), F32), pltpu.VMEM((M, ck), F32)],
        compiler_params=_params(2),
    )(q, do, o, lse, k, v)


K_SCALE = RET_DK ** -0.5


def _log_sigmoid(v):
    return -(jnp.maximum(-v, 0.0) + jnp.log(1.0 + jnp.exp(-jnp.abs(v))))


def _ret_decays(d, lg):
    C = RET_CHUNK
    ic = lax.broadcasted_iota(jnp.int32, (C, 1), 0)
    ir = lax.broadcasted_iota(jnp.int32, (1, C), 1)
    li = jnp.where(d == 0, ic, C - 1 - ic).astype(F32)
    lj = jnp.where(d == 0, ir, C - 1 - ir).astype(F32)
    diff = li - lj
    mask = jnp.where(diff >= 0, jnp.exp(jnp.maximum(diff, 0.0) * lg), 0.0)
    qd = jnp.exp((li + 1.0) * lg)
    kd = jnp.exp((C - 1.0 - li) * lg)
    cd = jnp.exp(C * lg)
    return li, diff, mask, qd, kd, cd


def _ctx_weights(d, t, lg, l_len):
    C = RET_CHUNK
    j = (t * C + lax.broadcasted_iota(jnp.int32, (C, 1), 0)).astype(F32)
    e = jnp.where(d == 0, (l_len - 1.0) - j, j)
    return e, jnp.exp(e * lg)


def _mirrored(x, i, n_lat):
    return jnp.where(i < n_lat, jnp.concatenate([x[RET_CHUNK:], x[:RET_CHUNK]], axis=0), x)


def _mirror_tile(n_lat):
    return lambda i: jnp.where(i < n_lat, n_lat - 1 - i, i)


def _ret_fwd(proj, lgt, s_len, l_len):
    R = proj.shape[0]
    C, H, DK, DV = RET_CHUNK, RET_HEADS, RET_DK, RET_DV
    nl, nc = s_len // C, l_len // C

    def stored(t):
        return jnp.where(t < nc, nl + t, jnp.maximum(t - nc, 0))

    def actual(d, t):
        n = jnp.maximum(t - nc, 0)
        return jnp.where(t < nc, nl + t, n if d == 0 else nl - 1 - n)

    def body(q0_ref, k0_ref, v0_ref, q1_ref, k1_ref, v1_ref, lg_ref, o_ref, st_ref, r_s):
        t = pl.program_id(0)
        qkv = ((q0_ref, k0_ref, v0_ref), (q1_ref, k1_ref, v1_ref))

        @pl.when(t == 0)
        def _():
            r_s[...] = jnp.zeros_like(r_s)

        def log_gamma(d, hh):
            return jnp.max(_log_sigmoid(lg_ref[d, hh]), axis=-1, keepdims=True)

        @pl.when(t < nc)
        def _():
            for d, (q_ref, k_ref, v_ref) in enumerate(qkv):
                for hh in range(H):
                    qc, vc = pl.ds(hh * DK, DK), pl.ds(hh * DV, DV)
                    _, w = _ctx_weights(d, t, log_gamma(d, hh), l_len)
                    r_s[d, hh] += _dot_tn(_bf(k_ref[:, qc] * K_SCALE * w), _bf(v_ref[:, vc]))
                    o_ref[d, :, vc] = jnp.zeros((C, DV), F32)

        @pl.when(t >= nc)
        def _():
            for d, (q_ref, k_ref, v_ref) in enumerate(qkv):
                for hh in range(H):
                    qc, vc = pl.ds(hh * DK, DK), pl.ds(hh * DV, DV)
                    _, _, mask, qd, kd, cd = _ret_decays(d, log_gamma(d, hh))
                    qb, kv, vb = _bf(q_ref[:, qc]), k_ref[:, qc] * K_SCALE, _bf(v_ref[:, vc])
                    r = r_s[d, hh]
                    st_ref[d, hh, 0] = r
                    att = _dot_nt(qb, _bf(kv)) * mask
                    o_ref[d, :, vc] = _dot(_bf(att), vb) + _dot(qb, _bf(r)) * qd
                    r_s[d, hh] = r * cd + _dot_tn(_bf(kv * kd), vb)

    def rows(d):
        return [pl.BlockSpec((C, H * DK), lambda t: (actual(d, t), 0)),
                pl.BlockSpec((C, H * DK), lambda t: (actual(d, t), 1)),
                pl.BlockSpec((C, H * DV), lambda t: (actual(d, t), 1))]

    return pl.pallas_call(
        body, name="ret_fwd", grid=(nc + nl,),
        in_specs=rows(0) + rows(1) + [pl.BlockSpec((2, H, 1, 128), lambda t: (0, 0, 0, 0))],
        out_specs=[pl.BlockSpec((2, C, H * DV), lambda t: (0, stored(t), 0)),
                   pl.BlockSpec((2, H, 1, DK, DV), lambda t: (0, 0, jnp.maximum(t - nc, 0), 0, 0))],
        out_shape=[SDS((2, R, H * DV), F32), SDS((2, H, nl, DK, DV), F32)],
        scratch_shapes=[pltpu.VMEM((2, H, DK, DV), F32)],
        compiler_params=_params(1),
    )(proj, proj, proj, proj, proj, proj, lgt)


def _ret_bwd(proj, lgt, states, do, s_len, l_len):
    R = proj.shape[0]
    C, H, DK, DV = RET_CHUNK, RET_HEADS, RET_DK, RET_DV
    nl, nc = s_len // C, l_len // C
    last = nl + nc - 1

    def stored(t):
        return jnp.where(t < nl, jnp.maximum(nl - 1 - t, 0), t)

    def actual(d, t):
        return stored(t) if d == 0 else t

    def body(q0_ref, k0_ref, v0_ref, do0_ref, q1_ref, k1_ref, v1_ref, do1_ref, lg_ref, st_ref,
             dq_ref, dk_ref, dv_ref, dlg_ref, dr_s, dl_s):
        t = pl.program_id(0)
        ins = ((q0_ref, k0_ref, v0_ref, do0_ref), (q1_ref, k1_ref, v1_ref, do1_ref))

        def log_gamma(d, hh):
            return jnp.max(_log_sigmoid(lg_ref[d, hh]), axis=-1, keepdims=True)

        @pl.when(t == 0)
        def _():
            dr_s[...] = jnp.zeros_like(dr_s)
            dl_s[...] = jnp.zeros_like(dl_s)

        @pl.when(t < nl)
        def _():
            for d, (q_ref, k_ref, v_ref, do_ref) in enumerate(ins):
                for hh in range(H):
                    qc, vc = pl.ds(hh * DK, DK), pl.ds(hh * DV, DV)
                    li, diff, mask, qd, kd, cd = _ret_decays(d, log_gamma(d, hh))
                    qv, kv, vv, dov = q_ref[:, qc], k_ref[:, qc] * K_SCALE, v_ref[:, vc], do_ref[:, vc]
                    qb, kb, vb, dob = _bf(qv), _bf(kv), _bf(vv), _bf(dov)
                    r, drn = st_ref[d, hh, 0], dr_s[d, hh]
                    rb, drb = _bf(r), _bf(drn)
                    p = _dot_nt(qb, kb)
                    dp = _dot_nt(dob, vb) * mask
                    dpb = _bf(dp)
                    doq = _bf(dov * qd)
                    dq_inter = _dot_nt(doq, rb)
                    dk_state = kd * _dot_nt(vb, drb)
                    dq_ref[d, :, qc] = _dot(dpb, kb) + dq_inter
                    dk_ref[d, :, qc] = (_dot_tn(dpb, qb) + dk_state) * K_SCALE
                    dv_ref[d, :, vc] = _dot_tn(_bf(p * mask), dob) + _dot(_bf(kv * kd), drb)
                    dr_s[d, hh] = cd * drn + _dot_tn(qb, doq)
                    dl_s[d, hh] += (jnp.sum(dp * p * diff) + jnp.sum((li + 1.0) * qv * dq_inter)
                                    + jnp.sum((C - 1.0 - li) * kv * dk_state) + C * jnp.sum(cd * r * drn))

        @pl.when(t >= nl)
        def _():
            for d, (q_ref, k_ref, v_ref, do_ref) in enumerate(ins):
                for hh in range(H):
                    qc, vc = pl.ds(hh * DK, DK), pl.ds(hh * DV, DV)
                    e, w = _ctx_weights(d, t - nl, log_gamma(d, hh), l_len)
                    kv, vb, drb = k_ref[:, qc] * K_SCALE, _bf(v_ref[:, vc]), _bf(dr_s[d, hh])
                    dkc = w * _dot_nt(vb, drb)
                    dq_ref[d, :, qc] = jnp.zeros((C, DK), F32)
                    dk_ref[d, :, qc] = dkc * K_SCALE
                    dv_ref[d, :, vc] = _dot(_bf(kv * w), drb)
                    dl_s[d, hh] += jnp.sum(e * kv * dkc)

        @pl.when(t == last)
        def _():
            for d in range(2):
                for hh in range(H):
                    dlg_ref[d, hh] = dl_s[d, hh] * (1.0 / (1.0 + jnp.exp(lg_ref[d, hh])))

    def rows(d):
        return [pl.BlockSpec((C, H * DK), lambda t: (actual(d, t), 0)),
                pl.BlockSpec((C, H * DK), lambda t: (actual(d, t), 1)),
                pl.BlockSpec((C, H * DV), lambda t: (actual(d, t), 1)),
                pl.BlockSpec((C, H * DV), lambda t: (actual(d, t), 0))]

    return pl.pallas_call(
        body, name="ret_bwd", grid=(nl + nc,),
        in_specs=rows(0) + rows(1) + [
            pl.BlockSpec((2, H, 1, 128), lambda t: (0, 0, 0, 0)),
            pl.BlockSpec((2, H, 1, DK, DV), lambda t: (0, 0, jnp.maximum(nl - 1 - t, 0), 0, 0))],
        out_specs=[pl.BlockSpec((2, C, H * DK), lambda t: (0, stored(t), 0)),
                   pl.BlockSpec((2, C, H * DK), lambda t: (0, stored(t), 0)),
                   pl.BlockSpec((2, C, H * DV), lambda t: (0, stored(t), 0)),
                   pl.BlockSpec((2, H, 1, 128), lambda t: (0, 0, 0, 0))],
        out_shape=[SDS((2, R, H * DK), F32), SDS((2, R, H * DK), F32), SDS((2, R, H * DV), F32),
                   SDS((2, H, 1, 128), F32)],
        scratch_shapes=[pltpu.VMEM((2, H, DK, DV), F32), pltpu.VMEM((2, H, 1, 128), F32)],
        compiler_params=_params(1),
    )(proj, proj, proj, do, proj, proj, proj, do, lgt, states)


def _readout_fwd(o2, proj, gn_w, n_lat):
    R = proj.shape[0]
    H, DV = RET_HEADS, RET_DV
    W = H * DV
    T = ROW_TILE
    assert T == 2 * RET_CHUNK

    def body(o_ref, ob_ref, g_ref, w_ref, out_ref):
        i = pl.program_id(0)
        for hh in range(H):
            cols = pl.ds(hh * DV, DV)
            y = o_ref[0, :, cols] + _mirrored(ob_ref[0, :, cols], i, n_lat)
            yc = y - jnp.mean(y, axis=-1, keepdims=True)
            yn = yc * lax.rsqrt(jnp.mean(yc * yc, axis=-1, keepdims=True) + EPS) * w_ref[:, cols]
            g = g_ref[:, cols]
            out_ref[:, cols] = _bf(g * _sigmoid(g) * yn)

    return pl.pallas_call(
        body, name="readout_fwd", grid=(R // T,),
        in_specs=[pl.BlockSpec((1, T, W), lambda i: (0, i, 0)),
                  pl.BlockSpec((1, T, W), lambda i: (1, _mirror_tile(n_lat)(i), 0)),
                  pl.BlockSpec((T, W), lambda i: (i, 2)), pl.BlockSpec((1, W), lambda i: (0, 0))],
        out_specs=pl.BlockSpec((T, W), lambda i: (i, 0)),
        out_shape=SDS((R, W), BF16), compiler_params=_params(1),
    )(o2, o2, proj, gn_w)


def _readout_bwd(o2, proj, gn_w, dgated, n_lat):
    R = proj.shape[0]
    H, DV = RET_HEADS, RET_DV
    W = H * DV
    T = ROW_TILE

    def body(o_ref, ob_ref, g_ref, w_ref, d_ref, do_ref, dg_ref, dw_ref):
        i = pl.program_id(0)

        @pl.when(i == 0)
        def _():
            dw_ref[...] = jnp.zeros_like(dw_ref)

        for hh in range(H):
            cols = pl.ds(hh * DV, DV)
            y = o_ref[0, :, cols] + _mirrored(ob_ref[0, :, cols], i, n_lat)
            yc = y - jnp.mean(y, axis=-1, keepdims=True)
            rstd = lax.rsqrt(jnp.mean(yc * yc, axis=-1, keepdims=True) + EPS)
            yn0 = yc * rstd
            wv = w_ref[:, cols]
            g = g_ref[:, cols]
            s = _sigmoid(g)
            dgt = d_ref[:, cols]
            dyn = dgt * (g * s)
            dg_ref[:, cols] = _bf(dgt * (yn0 * wv) * (s * (1.0 + g * (1.0 - s))))
            dw_ref[:, cols] += jnp.sum(dyn * yn0, axis=0, keepdims=True)
            a = dyn * wv
            do_ref[:, cols] = rstd * (a - jnp.mean(a, axis=-1, keepdims=True)
                                      - yn0 * jnp.mean(a * yn0, axis=-1, keepdims=True))

    return pl.pallas_call(
        body, name="readout_bwd", grid=(R // T,),
        in_specs=[pl.BlockSpec((1, T, W), lambda i: (0, i, 0)),
                  pl.BlockSpec((1, T, W), lambda i: (1, _mirror_tile(n_lat)(i), 0)),
                  pl.BlockSpec((T, W), lambda i: (i, 2)),
                  pl.BlockSpec((1, W), lambda i: (0, 0)), pl.BlockSpec((T, W), lambda i: (i, 0))],
        out_specs=[pl.BlockSpec((T, W), lambda i: (i, 0)), pl.BlockSpec((T, W), lambda i: (i, 0)),
                   pl.BlockSpec((1, W), lambda i: (0, 0))],
        out_shape=[SDS((R, W), F32), SDS((R, W), BF16), SDS((1, W), F32)],
        compiler_params=_params(1),
    )(o2, o2, proj, gn_w, dgated)


def _ret_dproj(dq2, dk2, dv2, dg, n_lat):
    R = dg.shape[0]
    NQ, NV = RET_HEADS * RET_DK, RET_HEADS * RET_DV
    T = ROW_TILE

    def body(dq_ref, dqb_ref, dk_ref, dkb_ref, dv_ref, dvb_ref, dg_ref, o_ref):
        i = pl.program_id(0)
        o_ref[:, pl.ds(0, NQ)] = _bf(dq_ref[0] + _mirrored(dqb_ref[0], i, n_lat))
        o_ref[:, pl.ds(NQ, NQ)] = _bf(dk_ref[0] + _mirrored(dkb_ref[0], i, n_lat))
        o_ref[:, pl.ds(2 * NQ, NV)] = _bf(dv_ref[0] + _mirrored(dvb_ref[0], i, n_lat))
        o_ref[:, pl.ds(2 * NQ + NV, NV)] = dg_ref[...]

    def both(width):
        return [pl.BlockSpec((1, T, width), lambda i: (0, i, 0)),
                pl.BlockSpec((1, T, width), lambda i: (1, _mirror_tile(n_lat)(i), 0))]

    return pl.pallas_call(
        body, name="ret_dproj", grid=(R // T,),
        in_specs=both(NQ) + both(NQ) + both(NV) + [pl.BlockSpec((T, NV), lambda i: (i, 0))],
        out_specs=pl.BlockSpec((T, 2 * NQ + 2 * NV), lambda i: (i, 0)),
        out_shape=SDS((R, 2 * NQ + 2 * NV), BF16), compiler_params=_params(1),
    )(dq2, dq2, dk2, dk2, dv2, dv2, dg)


def _silu(v):
    return v * _sigmoid(v)


def _ada_fwd(c_rows, ada_w, ada_b_shard):
    depth, D, cols = ada_w.shape

    def body(c_ref, w_ref, b_ref, o_ref):
        o_ref[0] = _dot(_bf(_silu(c_ref[...])), _bf(w_ref[0])) + b_ref[0]

    return pl.pallas_call(
        body, name="ada_fwd", grid=(depth,),
        in_specs=[pl.BlockSpec((16, D), lambda i: (0, 0)), pl.BlockSpec((1, D, cols), lambda i: (i, 0, 0)),
                  pl.BlockSpec((1, 1, cols), lambda i: (i, 0, 0))],
        out_specs=pl.BlockSpec((1, 16, cols), lambda i: (i, 0, 0)),
        out_shape=SDS((depth, 16, cols), F32), compiler_params=_params(1),
    )(c_rows, ada_w, ada_b_shard)


def _ada_bwd(c_rows, ada_w, d_lat, d_ctx):
    depth, D, cols = ada_w.shape

    def body(c_ref, w_ref, dl_ref, dc_ref, dw_ref, pc_ref):
        i = pl.program_id(0)
        cv = c_ref[...]
        a = _silu(cv)
        dcs = jnp.broadcast_to(jnp.sum(dc_ref[0], axis=0, keepdims=True), (8, cols))
        dw_ref[0] = _dot_tn(_bf(a[0:8]), _bf(dl_ref[0])) + _dot_tn(_bf(a[8:16]), _bf(dcs))

        @pl.when(i == 0)
        def _():
            pc_ref[...] = jnp.zeros_like(pc_ref)

        pc_ref[...] += _dot_nt(_bf(dcs), _bf(w_ref[0]))

        @pl.when(i == depth - 1)
        def _():
            cc = c_ref[pl.ds(8, 1), :]
            s = _sigmoid(cc)
            pc_ref[...] = pc_ref[...] * (s * (1.0 + cc * (1.0 - s)))

    return pl.pallas_call(
        body, name="ada_bwd", grid=(depth,),
        in_specs=[pl.BlockSpec((16, D), lambda i: (0, 0)), pl.BlockSpec((1, D, cols), lambda i: (i, 0, 0)),
                  pl.BlockSpec((1, 8, cols), lambda i: (i, 0, 0)), pl.BlockSpec((1, 8, cols), lambda i: (i, 0, 0))],
        out_specs=[pl.BlockSpec((1, D, cols), lambda i: (i, 0, 0)), pl.BlockSpec((8, D), lambda i: (0, 0))],
        out_shape=[SDS((depth, D, cols), F32), SDS((8, D), F32)], compiler_params=_params(1),
    )(c_rows, ada_w, d_lat, d_ctx)


def _adamw(w, g, m, v, name):
    shape = w.shape
    n = g.shape[0]
    cols = shape[-1]
    rows = w.size // cols
    tr = _pick(rows, 512, 8) if rows * cols * 4 > (1 << 20) else rows
    spec = pl.BlockSpec((tr, cols), lambda i: (i, 0))

    def body(w_ref, g_ref, m_ref, v_ref, go_ref, d_ref, mo_ref, vo_ref):
        gs = g_ref[0].astype(F32)
        for k in range(1, n):
            gs = gs + g_ref[k].astype(F32)
        mn = ADAM_B1 * m_ref[...] + (1.0 - ADAM_B1) * gs
        vn = ADAM_B2 * v_ref[...] + (1.0 - ADAM_B2) * jnp.square(gs)
        m_hat = mn / (1.0 - ADAM_B1 ** ADAM_STEP)
        v_hat = vn / (1.0 - ADAM_B2 ** ADAM_STEP)
        go_ref[...] = gs
        d_ref[...] = -ADAM_LR * (m_hat / (jnp.sqrt(v_hat) + ADAM_EPS) + ADAM_WD * w_ref[...])
        mo_ref[...] = mn
        vo_ref[...] = vn

    outs = pl.pallas_call(
        body, name=name, grid=(rows // tr,),
        in_specs=[spec, pl.BlockSpec((n, tr, cols), lambda i: (0, i, 0)), spec, spec],
        out_specs=[spec] * 4, out_shape=[SDS((rows, cols), F32)] * 4, compiler_params=_params(1),
    )(w.reshape(rows, cols), g.reshape(n, rows, cols), m.reshape(rows, cols), v.reshape(rows, cols))
    return tuple(o.reshape(shape) for o in outs)


def _sum_slots(own, recv, name):
    shape, n, cols = own.shape, recv.shape[0], own.shape[-1]
    own, recv = own.reshape(-1, cols), recv.reshape(n, -1, cols)
    rows = own.shape[0]
    tr = _pick(rows, 512, 16)

    def body(own_ref, r_ref, o_ref):
        acc = own_ref[...].astype(F32)
        for k in range(n):
            acc = acc + r_ref[k].astype(F32)
        o_ref[...] = acc

    return pl.pallas_call(
        body, name=name, grid=(rows // tr,),
        in_specs=[pl.BlockSpec((tr, cols), lambda i: (i, 0)), pl.BlockSpec((n, tr, cols), lambda i: (0, i, 0))],
        out_specs=pl.BlockSpec((tr, cols), lambda i: (i, 0)),
        out_shape=SDS((rows, cols), F32), compiler_params=_params(1),
    )(own, recv).reshape(shape)


def _position():
    return lax.axis_index("x"), lax.axis_index("y"), lax.axis_index("c")


def _peer(k, x, y, c):
    return (1 - x if k & 4 else x, 1 - y if k & 2 else y, 1 - c if k & 1 else c)


def _index(pos):
    return 4 * pos[0] + 2 * pos[1] + pos[2]


def _gather_small(v, name):
    rows, lanes = v.shape

    def body(x_ref, out_ref, send_sems, recv_sems, local_sem):
        me = _position()
        mine = pltpu.make_async_copy(x_ref, out_ref.at[_index(me)], local_sem)
        mine.start()

        def copy(k, slot):
            return pltpu.make_async_remote_copy(
                src_ref=x_ref, dst_ref=out_ref.at[slot], send_sem=send_sems.at[k - 1],
                recv_sem=recv_sems.at[k - 1], device_id=_peer(k, *me), device_id_type=MESH)

        sends = [copy(k, _index(me)) for k in range(1, N_DEV)]
        for cp in sends:
            cp.start()
        for k in range(1, N_DEV):
            copy(k, _index(_peer(k, *me))).wait_recv()
        for cp in sends:
            cp.wait_send()
        mine.wait()

    return pl.pallas_call(
        body, name=name, out_shape=SDS((N_DEV, rows, lanes), v.dtype),
        in_specs=[pl.BlockSpec(memory_space=pltpu.VMEM)],
        out_specs=pl.BlockSpec(memory_space=pltpu.VMEM),
        scratch_shapes=[pltpu.SemaphoreType.DMA((N_DEV - 1,)), pltpu.SemaphoreType.DMA((N_DEV - 1,)),
                        pltpu.SemaphoreType.DMA],
        compiler_params=pltpu.CompilerParams(vmem_limit_bytes=VMEM_LIMIT_V7X),
    )(v)


HBM_SPEC = pl.BlockSpec(memory_space=pltpu.HBM)
SEM_SPEC = pl.BlockSpec(memory_space=pltpu.SEMAPHORE)
SPLIT_EFFECT = pltpu.SideEffectType.DATAFLOW_SIDE_EFFECTING


def _split_start(srcs, gather, name):
    n = len(srcs)
    lands = [jnp.zeros(((N_DEV,) + s.shape) if gather else s.shape, s.dtype) for s in srcs]

    def body(*refs):
        src_refs, land_refs, sems, token = refs[:n], refs[n:2 * n], refs[2 * n:4 * n], refs[-1]
        me = _position()
        for a in range(n):
            for k in range(1, N_DEV):
                peer = _peer(k, *me)
                pltpu.make_async_remote_copy(
                    src_ref=src_refs[a] if gather else src_refs[a].at[_index(peer)],
                    dst_ref=land_refs[a].at[_index(me)], send_sem=sems[2 * a], recv_sem=sems[2 * a + 1],
                    device_id=peer, device_id_type=MESH).start()
        token[...] = jnp.zeros_like(token)

    hbm = lambda arrays: tuple(pltpu.HBM(a.shape, a.dtype) for a in arrays)
    outs = pl.pallas_call(
        body, name=name,
        out_shape=(pltpu.SemaphoreType.DMA(()),) * (2 * n) + hbm(srcs) + hbm(lands) + (SDS((8, 128), F32),),
        in_specs=(HBM_SPEC,) * (2 * n),
        out_specs=(SEM_SPEC,) * (2 * n) + (HBM_SPEC,) * (2 * n) + (pl.BlockSpec(memory_space=pltpu.VMEM),),
        input_output_aliases={a: 2 * n + a for a in range(2 * n)},
        compiler_params=pltpu.CompilerParams(has_side_effects=SPLIT_EFFECT),
    )(*[pltpu.with_memory_space_constraint(a, pltpu.HBM) for a in list(srcs) + lands])
    return outs[:2 * n], outs[2 * n:3 * n], outs[3 * n:4 * n], outs[-1]


def _split_wait(flight, after, name):
    sems, srcs, lands, _ = flight
    n = len(srcs)

    def body(*refs):
        land_refs, sem_refs = refs[n:2 * n], refs[2 * n:4 * n]
        me = _position()
        for a in range(n):
            seven = land_refs[a].at[pl.ds(0, N_DEV - 1)]
            copies = pltpu.make_async_remote_copy(
                src_ref=seven, dst_ref=seven, send_sem=sem_refs[2 * a], recv_sem=sem_refs[2 * a + 1],
                device_id=_peer(1, *me), device_id_type=MESH)
            copies.wait_send()
            copies.wait_recv()

    outs = pl.pallas_call(
        body, name=name,
        out_shape=tuple(pltpu.HBM(a.shape, a.dtype) for a in list(srcs) + list(lands)),
        in_specs=(HBM_SPEC,) * (2 * n) + (SEM_SPEC,) * (2 * n) + (pl.BlockSpec(memory_space=pl.ANY),),
        out_specs=(HBM_SPEC,) * (2 * n), input_output_aliases={a: a for a in range(2 * n)},
        compiler_params=pltpu.CompilerParams(has_side_effects=SPLIT_EFFECT),
    )(*srcs, *lands, *sems, after)
    return outs[:n], outs[n:]


def _pack_rows(arrays, lanes, dtype):
    flat = jnp.concatenate([a.astype(dtype).reshape(-1) for a in arrays])
    pad = (-flat.size) % (16 * lanes)
    if pad:
        flat = jnp.concatenate([flat, jnp.zeros((pad,), dtype)])
    return flat.reshape(-1, lanes)


def _unpack_rows(packed, shapes):
    n = packed.shape[0]
    flat = packed.reshape(n, -1)
    out, off = [], 0
    for shp in shapes:
        size = math.prod(shp)
        out.append(flat[:, off:off + size].reshape((n,) + tuple(shp)))
        off += size
    return out


def _unshard(g8, axis):
    moved = jnp.moveaxis(g8, 0, axis)
    shp = list(moved.shape)
    shp[axis:axis + 2] = [shp[axis] * shp[axis + 1]]
    return moved.reshape(shp)


def _split8(full, axis):
    shp = list(full.shape)
    shp[axis:axis + 1] = [N_DEV, shp[axis] // N_DEV]
    return jnp.moveaxis(full.reshape(shp), axis, 0)


def _my_shard(g, axis, me):
    size = g.shape[axis + 1] // N_DEV
    return lax.dynamic_slice_in_dim(g, me * size, size, axis=axis + 1)


BIG_WEIGHTS = ("ffn_w_up", "ffn_w_down", "attn_w_qkv", "attn_w_o", "ret_w_in", "ret_w_out", "pool_w")
LAYER_WEIGHTS = (
    (("ffn_w_up", 0, "cols"), ("ffn_w_down", 0, "rows"), ("pool_w", 0, "pool")),
    (("ffn_w_up", 1, "cols"), ("ffn_w_down", 1, "rows"), ("attn_w_qkv", 0, "cols"), ("attn_w_o", 0, "rows")),
    (("ffn_w_up", 2, "cols"), ("ffn_w_down", 2, "rows"), ("ret_w_in", 0, "cols"), ("ret_w_out", 0, "rows")),
    (("ffn_w_up", 3, "cols"), ("ffn_w_down", 3, "rows"), ("pool_w", 1, "pool")),
)


GATHER_GROUPS = (LAYER_WEIGHTS[0][:2],) + LAYER_WEIGHTS[1:]
GRAD_GROUPS = {"3": LAYER_WEIGHTS[3], "2": LAYER_WEIGHTS[2], "1": LAYER_WEIGHTS[1],
               "0ffn": LAYER_WEIGHTS[0][:2], "0mix": LAYER_WEIGHTS[0][2:]}


def _shard_to_send(w, kind):
    w = w.astype(BF16)
    return w.T if kind == "cols" else w


def _full_from_land(land, kind):
    return _unshard(land, 1) if kind == "pool" else land.reshape(-1, land.shape[-1])


def _grad_to_send(g, kind):
    return _split8(g, 1).astype(BF16) if kind == "pool" else g.astype(BF16).reshape(N_DEV, -1, g.shape[-1])


def _shard_grad(gsum, kind):
    return gsum.T if kind == "cols" else gsum
SMALL_SHARDED = (("norm_w", 2), ("pool_b", 1), ("pool_scale", 1), ("ret_gn_w", 1), ("ffn_conv_w", 2))
REPLICATED = ("ada_b", "attn_q_gain", "attn_k_gain", "ret_decay_logit", "ffn_conv_b")
WEIGHT_ORDER = ("c_ctx", "ada_w", "ada_b", "norm_w", "pool_w", "pool_b", "pool_scale", "attn_w_qkv",
                "attn_q_gain", "attn_k_gain", "attn_w_o", "ret_w_in", "ret_decay_logit", "ret_gn_w",
                "ret_w_out", "ffn_w_up", "ffn_conv_w", "ffn_conv_b", "ffn_w_down")


def _local_step(x0, target, mods, P, get_weights, put_grads, s_len, l_len):
    n_lat = s_len // ROW_TILE
    nw = P["norm_w"]
    lgt = jnp.broadcast_to(P["ret_decay_logit"][0][:, :, None, None], (2, RET_HEADS, 1, 128))
    cos, sin = _rope_tables(s_len, l_len)
    h_dtype = [F32 if i % 3 == 0 else BF16 for i in range(DEPTH)]
    saved = []
    mods = list(mods)
    X = x0
    h = _res_norm(X, None, None, 0, nw[0, 0], mods[0], 0, h_dtype[0], n_lat, "norm_first")
    for i in range(DEPTH):
        kind, j, mod = i % 3, i // 3, mods[i]
        W = dict(get_weights(i, "mix", h))
        sv = {"X": X, "h": h, "W": W}
        if kind == 0:
            y = _pool_fwd(h, W["pool_w"], P["pool_b"][j:j + 1], P["pool_scale"][j:j + 1],
                          n_lat, s_len, l_len, f"pool_fwd{i}")
        elif kind == 1:
            qkv = _mm(h, W["attn_w_qkv"], "nt", F32, f"qkv{i}")
            q, k, v = _qk_prep_fwd(qkv, P["attn_q_gain"][j:j + 1], P["attn_k_gain"][j:j + 1], cos, sin)
            o, lse = _flash_fwd(q, k, v, s_len, l_len)
            y = _mm(o, W["attn_w_o"], "nn", F32, f"attn_out{i}")
            sv.update(qkv=qkv, q=q, k=k, v=v, o=o, lse=lse)
        else:
            proj = _mm(h, W["ret_w_in"], "nt", F32, f"ret_in{i}")
            o2, states = _ret_fwd(proj, lgt, s_len, l_len)
            gated = _readout_fwd(o2, proj, P["ret_gn_w"][j:j + 1], n_lat)
            y = _mm(gated, W["ret_w_out"], "nn", F32, f"ret_out{i}")
            sv.update(proj=proj, o2=o2, states=states, gated=gated)
        X1, h2 = _res_norm(X, y, mod, 0, nw[i, 1], mod, 1, BF16, n_lat, f"res_norm_mid{i}")
        W.update(get_weights(i, "ffn", h2))
        u = _mm(h2, W["ffn_w_up"], "nt", FFN_HIDDEN_DTYPE, f"ffn_up{i}")
        gact = _conv_gate_fwd(u, P["ffn_conv_w"][i], P["ffn_conv_b"][i:i + 1], n_lat, f"conv_gate_fwd{i}")
        f = _mm(gact, W["ffn_w_down"], "nn", F32, f"ffn_down{i}")
        sv.update(y=y, X1=X1, h2=h2, u=u, gact=gact, f=f)
        saved.append(sv)
        if i + 1 < DEPTH:
            X, h = _res_norm(X1, f, mod, 1, nw[i + 1, 0], mods[i + 1], 0, h_dtype[i + 1], n_lat,
                             f"res_norm_end{i}")
        else:
            X = _res_norm(X1, f, mod, 1, None, None, 0, None, n_lat, "res_last")

    dX, loss = _loss_bwd(X, target, n_lat)
    G = {name: [None] * P[name].shape[0] for name in
         ("pool_b", "pool_scale", "attn_q_gain", "attn_k_gain", "ret_decay_logit", "ret_gn_w", "ffn_conv_w",
          "ffn_conv_b")}
    dnw = [[None, None] for _ in range(DEPTH)]
    dmods = [None] * DEPTH
    for i in reversed(range(DEPTH)):
        kind, j, mod, sv = i % 3, i // 3, mods[i], saved[i]
        W, gl = sv["W"], {}
        if i == DEPTH - 1:
            df, dg2 = _gate_bwd(dX, sv["f"], mod, 1, BF16, n_lat, f"gate_bwd_ffn{i}")
        dgact = _mm(df, W["ffn_w_down"], "nt", FFN_HIDDEN_DTYPE, f"ffn_down_dx{i}")
        gl["ffn_w_down"] = _mm(sv["gact"], df, "tn", BF16, f"ffn_down_dw{i}")
        du, dcw, dcb = _conv_gate_bwd(sv["u"], dgact, P["ffn_conv_w"][i], P["ffn_conv_b"][i:i + 1], n_lat,
                                      f"conv_gate_bwd{i}")
        G["ffn_conv_w"][i], G["ffn_conv_b"][i] = dcw, dcb[0]
        dh2 = _mm(du, W["ffn_w_up"], "nn", F32, f"ffn_up_dx{i}")
        gl["ffn_w_up"] = _mm(du, sv["h2"], "tn", BF16, f"ffn_up_dw{i}")
        if i == 0:
            mod = mod + put_grads("0ffn", gl)
        dX1, dnw[i][1], dsh2, dsc2, dy, dg1 = _norm_bwd(
            dX, dh2, sv["X1"], nw[i, 1], mod, 1, n_lat, f"norm_bwd_ffn{i}",
            gated=(sv["y"], mod, 0, F32 if kind == 0 else BF16))
        h = sv["h"]
        if kind == 0:
            dh, dpw, dpb, dps = _pool_bwd(h, dy, W["pool_w"], P["pool_b"][j:j + 1], P["pool_scale"][j:j + 1],
                                          n_lat, s_len, l_len, f"pool_bwd{i}")
            gl["pool_w"], G["pool_b"][j], G["pool_scale"][j] = dpw, dpb[0], dps[0]
        elif kind == 1:
            do = _mm(dy, W["attn_w_o"], "nt", F32, f"attn_out_dx{i}")
            gl["attn_w_o"] = _mm(sv["o"], dy, "tn", BF16, f"attn_out_dw{i}")
            dq, dk, dv = _flash_bwd(sv["q"], sv["k"], sv["v"], sv["o"], sv["lse"], do, s_len, l_len)
            dqkv, dqg, dkg = _qk_prep_bwd(sv["qkv"], dq, dk, dv, P["attn_q_gain"][j:j + 1],
                                          P["attn_k_gain"][j:j + 1], cos, sin)
            G["attn_q_gain"][j], G["attn_k_gain"][j] = dqg[0], dkg[0]
            dh = _mm(dqkv, W["attn_w_qkv"], "nn", F32, f"qkv_dx{i}")
            gl["attn_w_qkv"] = _mm(dqkv, h, "tn", BF16, f"qkv_dw{i}")
        else:
            dgated = _mm(dy, W["ret_w_out"], "nt", F32, f"ret_out_dx{i}")
            gl["ret_w_out"] = _mm(sv["gated"], dy, "tn", BF16, f"ret_out_dw{i}")
            do, dg, dgn = _readout_bwd(sv["o2"], sv["proj"], P["ret_gn_w"][j:j + 1], dgated, n_lat)
            dq2, dk2, dv2, dlg = _ret_bwd(sv["proj"], lgt, sv["states"], do, s_len, l_len)
            dproj = _ret_dproj(dq2, dk2, dv2, dg, n_lat)
            G["ret_gn_w"][j], G["ret_decay_logit"][j] = dgn[0], dlg[:, :, 0, 0]
            dh = _mm(dproj, W["ret_w_in"], "nn", F32, f"ret_in_dx{i}")
            gl["ret_w_in"] = _mm(dproj, h, "tn", BF16, f"ret_in_dw{i}")
        zero = put_grads(str(i) if i > 0 else "0mix", gl)
        if i > 0:
            mods[i - 1] = mods[i - 1] + zero
            dX, dnw[i][0], dsh1, dsc1, df_below, dg2_below = _norm_bwd(
                dX1, dh, sv["X"], nw[i, 0], mod, 0, n_lat, f"norm_bwd_mix{i}",
                gated=(saved[i - 1]["f"], mods[i - 1], 1, BF16))
        else:
            dX, dnw[i][0], dsh1, dsc1 = _norm_bwd(dX1, dh, sv["X"], nw[i, 0], mod, 0, n_lat, f"norm_bwd_mix{i}")
        dmods[i] = jnp.concatenate([dsh1, dsc1, dg1, dsh2, dsc2, dg2], axis=1)
        if i > 0:
            df, dg2 = df_below, dg2_below
    grads = {name: jnp.stack(parts) for name, parts in G.items()}
    grads["norm_w"] = jnp.stack([jnp.concatenate(pair, axis=0) for pair in dnw])
    return loss, dX, grads, jnp.stack(dmods)


def kernel(x, c, ctx, c_ctx, ada_w, ada_b, norm_w, pool_w, pool_b, pool_scale, attn_w_qkv, attn_q_gain,
           attn_k_gain, attn_w_o, ret_w_in, ret_decay_logit, ret_gn_w, ret_w_out, ffn_w_up, ffn_conv_w,
           ffn_conv_b, ffn_w_down, loss_target, m_c_ctx, m_ada_w, m_ada_b, m_norm_w, m_pool_w, m_pool_b,
           m_pool_scale, m_attn_w_qkv, m_attn_q_gain, m_attn_k_gain, m_attn_w_o, m_ret_w_in,
           m_ret_decay_logit, m_ret_gn_w, m_ret_w_out, m_ffn_w_up, m_ffn_conv_w, m_ffn_conv_b, m_ffn_w_down,
           v_c_ctx, v_ada_w, v_ada_b, v_norm_w, v_pool_w, v_pool_b, v_pool_scale, v_attn_w_qkv, v_attn_q_gain,
           v_attn_k_gain, v_attn_w_o, v_ret_w_in, v_ret_decay_logit, v_ret_gn_w, v_ret_w_out, v_ffn_w_up,
           v_ffn_conv_w, v_ffn_conv_b, v_ffn_w_down):
    A = dict(locals())
    me = _index(_position())
    s_len, D = x.shape[1], x.shape[2]
    l_len = ctx.shape[1]
    assert s_len % ROW_TILE == 0 and l_len % ROW_TILE == 0 and s_len % GRID_W == 0

    shards = [[_shard_to_send(A[n][j], kind) for n, j, kind in group] for group in GATHER_GROUPS]
    flights, zero = [], jnp.zeros((), F32)
    for i in range(DEPTH):
        flights.append(_split_start(shards[i], True, f"gather_start{i}"))
        zero = zero + flights[i][3][0, 0]

    small = [A[n] for n, _ in SMALL_SHARDED]
    first_parts = [c + zero] + small + [pool_w[0]]
    got = _gather_small(_pack_rows(first_parts, 128, F32), "gather_c_small")
    parts = _unpack_rows(got, [a.shape for a in first_parts])
    c_all = parts[0].reshape(N_DEV, D)
    P = {n: _unshard(g8, ax) for (n, ax), g8 in zip(SMALL_SHARDED, parts[1:-1])}
    first_pool_w = _unshard(parts[-1], 1).astype(BF16)

    c_rows = jnp.concatenate([c_all, c_ctx.reshape(1, D), jnp.zeros((7, D), F32)], axis=0)
    cols = ada_w.shape[2]
    ada_b_shard = lax.dynamic_slice_in_dim(ada_b, me * cols, cols, axis=1).reshape(DEPTH, 1, cols)
    mod_shard = _ada_fwd(c_rows, ada_w, ada_b_shard)
    got = _gather_small(mod_shard.reshape(-1, 128), "gather_mod").reshape(N_DEV, DEPTH, 16, cols)
    mod_lat = lax.dynamic_index_in_dim(got, me, axis=2, keepdims=False)
    mod_ctx = got[:, :, 8, :]
    mods = jnp.stack([jnp.moveaxis(mod_lat, 0, 1).reshape(DEPTH, 6, D),
                      jnp.moveaxis(mod_ctx, 0, 1).reshape(DEPTH, 6, D)], axis=1)

    mods = [mods[i] for i in range(DEPTH)]
    for n in REPLICATED:
        P[n] = A[n]
    landed = {}

    def get_weights(i, part, x_now):
        if i == 0 and part == "mix":
            return {"pool_w": first_pool_w}
        if i not in landed:
            owns, lands = _split_wait(flights[i], x_now, f"gather_wait{i}")
            landed[i] = {n: _full_from_land(lax.dynamic_update_index_in_dim(land, own, me, axis=0), kind)
                         for (n, j, kind), own, land in zip(GATHER_GROUPS[i], owns, lands)}
        return landed[i]

    sent = {}

    def put_grads(group, gl):
        sent[group] = _split_start([_grad_to_send(gl[n], kind) for n, j, kind in GRAD_GROUPS[group]], False,
                                   f"exchange_start_{group}")
        return sent[group][3][0, 0]

    x0 = jnp.concatenate([x[0], ctx[0]], axis=0)
    loss8, dx0, G, dmods = _local_step(x0, loss_target[0], mods, P, get_weights, put_grads, s_len, l_len)
    loss = lax.psum(loss8[0, 0], ("x", "y", "c"))
    grad_x = dx0[:s_len].reshape(x.shape)

    small_names = ["dmods"] + list(REPLICATED[1:]) + [n for n, _ in SMALL_SHARDED]
    small_parts = [dmods] + [G[n] for n in small_names[1:]]
    got = _gather_small(_pack_rows(small_parts, 128, F32), "gather_small_grads")
    S8 = dict(zip(small_names, _unpack_rows(got, [a.shape for a in small_parts])))

    dm = S8["dmods"].reshape(N_DEV, DEPTH, 2, 6 * D)
    dm_mine = lax.dynamic_slice_in_dim(dm, me * cols, cols, axis=3)
    g_ada_w, pc = _ada_bwd(c_rows, ada_w, jnp.moveaxis(dm_mine[:, :, 0], 0, 1), jnp.moveaxis(dm_mine[:, :, 1], 0, 1))
    pc8 = _gather_small(pc.reshape(-1, 128), "gather_c_ctx_grad").reshape(N_DEV, 8, D)

    def owner_sums(group, after):
        sends, lands = _split_wait(sent[group], after, f"exchange_wait_{group}")
        out = {}
        for (n, j, kind), send, land in zip(GRAD_GROUPS[group], sends, lands):
            own = lax.dynamic_index_in_dim(send, me, axis=0, keepdims=False)
            out[(n, j)] = _shard_grad(_sum_slots(own, land, f"sum_slots_{n}{j}"), kind)
        return out

    shard_grads = {}
    for group in ("3", "2", "1", "0mix"):
        shard_grads.update(owner_sums(group, pc8))

    g_in = {"c_ctx": pc8[:, 0, :], "ada_w": g_ada_w[None],
            "ada_b": jnp.moveaxis(dm, 2, 1).reshape(2 * N_DEV, DEPTH, 6 * D)}
    for n in REPLICATED[1:]:
        g_in[n] = S8[n]
    for n, ax in SMALL_SHARDED:
        g_in[n] = _my_shard(S8[n], ax, me)

    def stacked(n):
        return jnp.stack([shard_grads[(n, j)] for j in range(A[n].shape[0])])[None]

    late = [n for n, j, kind in GRAD_GROUPS["0ffn"]]
    for n in BIG_WEIGHTS:
        if n not in late:
            g_in[n] = stacked(n)
    res = {n: _adamw(A[n], g_in[n], A["m_" + n], A["v_" + n], "adamw_" + n) for n in WEIGHT_ORDER if n not in late}
    done = sum(res[n][1].reshape(-1)[0] for n in res)
    shard_grads.update(owner_sums("0ffn", done.reshape(1, 1)))
    for n in late:
        res[n] = _adamw(A[n], stacked(n), A["m_" + n], A["v_" + n], "adamw_" + n)
    outs = [loss, grad_x]
    for slot in range(4):
        outs += [res[n][slot] for n in WEIGHT_ORDER]
    return tuple(outs)
```

```python
import functools
import math

import jax
import jax.numpy as jnp
from jax import lax
from jax.experimental import pallas as pl
from jax.experimental.pallas import tpu as pltpu

F32 = jnp.float32
BF16 = jnp.bfloat16
SDS = jax.ShapeDtypeStruct
MESH = pl.DeviceIdType.MESH

N_DEV = 8
EPS = 1e-6
DEPTH = 4
GRID_W = 64
POOL_WINDOWS = (2, 4, 8, 16)
N_HEADS = 8
N_KV = 2
HEAD_DIM = 128
ROPE_THETA = 10000.0
RET_HEADS = 4
RET_DK = 256
RET_DV = 512
RET_CHUNK = 128
ADAM_LR = 0.001
ADAM_B1 = 0.9
ADAM_B2 = 0.999
ADAM_EPS = 1e-08
ADAM_WD = 0.01
ADAM_STEP = 10

ROW_TILE = 256
FFN_HIDDEN_DTYPE = BF16
FLASH_FWD_TILE = 128
HALO = 8
VMEM_LIMIT_V7X = 56 * 1024 * 1024


def _params(n_axes=0):
    sem = ("arbitrary",) * n_axes if n_axes else None
    return pltpu.CompilerParams(dimension_semantics=sem, vmem_limit_bytes=VMEM_LIMIT_V7X)


def _pick(n, cap, mult):
    best = None
    for d in range(mult, min(n, cap) + 1, mult):
        if n % d == 0:
            best = d
    return best if best is not None else n


def _dot(a, b):
    return jnp.dot(a, b, preferred_element_type=F32)


def _dot_nt(a, b):
    return lax.dot_general(a, b, (((1,), (1,)), ((), ())), preferred_element_type=F32)


def _dot_tn(a, b):
    return lax.dot_general(a, b, (((0,), (0,)), ((), ())), preferred_element_type=F32)


def _bf(v):
    return v.astype(BF16)


def _sigmoid(v):
    return 0.5 * jnp.tanh(0.5 * v) + 0.5


MM_VMEM_BUDGET = 40 * 1024 * 1024
MM_STEP_BYTES = 1 << 20
MM_ACC_PASS_BYTES = 8


def _divisors(n, mult, cap):
    return [d for d in range(mult, min(n, cap) + 1, mult) if n % d == 0] or [n]


def _mm_tiles(mode, M, N, K, a_item, b_item, o_item):
    best = None
    for tm in _divisors(M, 128 if mode == "tn" else 16, 2816):
        for tn in _divisors(N, 128, 2048):
            for tk in _divisors(K, 16 if mode == "tn" else 128, 2816):
                ni, nj, nk = M // tm, N // tn, K // tk
                vmem = 2 * (tm * tk * a_item + tk * tn * b_item + tm * tn * o_item) + tm * tn * 4
                if vmem > MM_VMEM_BUDGET:
                    continue
                a_reads = 1 if nk == 1 else nj
                b_reads = 1 if (nk == 1 and nj == 1) else ni
                cost = (M * K * a_item * a_reads + K * N * b_item * b_reads + M * N * o_item
                        + ni * nj * nk * MM_STEP_BYTES + (nk - 1) * M * N * MM_ACC_PASS_BYTES)
                if best is None or cost < best[0]:
                    best = (cost, tm, tn, tk)
    return best[1:]


def _mm(a, b, mode, out_dtype, name):
    if mode == "nn":
        (M, K), (K2, N) = a.shape, b.shape
    elif mode == "nt":
        (M, K), (N, K2) = a.shape, b.shape
    else:
        (K, M), (K2, N) = a.shape, b.shape
    assert K == K2, (a.shape, b.shape, mode)
    tm, tn, tk = _mm_tiles(mode, M, N, K, a.dtype.itemsize, b.dtype.itemsize, jnp.dtype(out_dtype).itemsize)
    nk = K // tk
    if mode == "nn":
        a_spec = pl.BlockSpec((tm, tk), lambda i, j, k: (i, k))
        b_spec = pl.BlockSpec((tk, tn), lambda i, j, k: (k, j))
    elif mode == "nt":
        a_spec = pl.BlockSpec((tm, tk), lambda i, j, k: (i, k))
        b_spec = pl.BlockSpec((tn, tk), lambda i, j, k: (j, k))
    else:
        a_spec = pl.BlockSpec((tk, tm), lambda i, j, k: (k, i))
        b_spec = pl.BlockSpec((tk, tn), lambda i, j, k: (k, j))
    dot = {"nn": _dot, "nt": _dot_nt, "tn": _dot_tn}[mode]

    def body(a_ref, b_ref, o_ref, acc_ref):
        part = dot(_bf(a_ref[...]), _bf(b_ref[...]))
        if nk == 1:
            o_ref[...] = part.astype(out_dtype)
        else:
            k = pl.program_id(2)

            @pl.when(k == 0)
            def _():
                acc_ref[...] = part

            @pl.when(k > 0)
            def _():
                acc_ref[...] += part

            @pl.when(k == nk - 1)
            def _():
                o_ref[...] = acc_ref[...].astype(out_dtype)

    return pl.pallas_call(
        body, name=name, grid=(M // tm, N // tn, nk),
        in_specs=[a_spec, b_spec],
        out_specs=pl.BlockSpec((tm, tn), lambda i, j, k: (i, j)),
        out_shape=SDS((M, N), out_dtype),
        scratch_shapes=[pltpu.VMEM((tm, tn), F32)],
        compiler_params=_params(3),
    )(a, b)


def _seg_spec(n_lat, d):
    return pl.BlockSpec((1, 6, d), lambda i: ((i >= n_lat).astype(jnp.int32), 0, 0))


def _seg_acc_spec(n_lat, d):
    return pl.BlockSpec((1, 1, d), lambda i: ((i >= n_lat).astype(jnp.int32), 0, 0))


def _res_norm(x, y, gmod, gk, nw, nmod, nk, h_dtype, n_lat, name):
    R, D = x.shape
    has_res, has_norm = y is not None, nw is not None
    row = pl.BlockSpec((ROW_TILE, D), lambda i: (i, 0))
    vec = pl.BlockSpec((1, D), lambda i: (0, 0))
    ins, specs, outs, ospecs = [x], [row], [], []
    if has_res:
        ins += [y, gmod]
        specs += [row, _seg_spec(n_lat, D)]
        outs.append(SDS((R, D), F32))
        ospecs.append(row)
    if has_norm:
        ins += [nw.reshape(1, D), nmod]
        specs += [vec, _seg_spec(n_lat, D)]
        outs.append(SDS((R, D), h_dtype))
        ospecs.append(row)

    def body(*refs):
        refs = list(refs)
        z = refs.pop(0)[...]
        if has_res:
            y_ref, g_ref = refs.pop(0), refs.pop(0)
            z = z + g_ref[0, pl.ds(3 * gk + 2, 1), :] * y_ref[...].astype(F32)
        if has_norm:
            nw_ref, m_ref = refs.pop(0), refs.pop(0)
        if has_res:
            refs.pop(0)[...] = z
        if has_norm:
            r = lax.rsqrt(jnp.mean(z * z, axis=-1, keepdims=True) + EPS)
            h = (z * r) * nw_ref[...]
            h = h * (1.0 + m_ref[0, pl.ds(3 * nk + 1, 1), :]) + m_ref[0, pl.ds(3 * nk, 1), :]
            refs.pop(0)[...] = h.astype(h_dtype)

    res = pl.pallas_call(
        body, name=name, grid=(R // ROW_TILE,), in_specs=specs, out_specs=ospecs,
        out_shape=outs, compiler_params=_params(1),
    )(*ins)
    return res if len(res) > 1 else res[0]


def _gate_bwd(dz, y, mod, k, out_dtype, n_lat, name):
    R, D = dz.shape
    row = pl.BlockSpec((ROW_TILE, D), lambda i: (i, 0))

    def body(dz_ref, y_ref, m_ref, dy_ref, dg_ref):
        i = pl.program_id(0)
        dzv = dz_ref[...]
        dy_ref[...] = (m_ref[0, pl.ds(3 * k + 2, 1), :] * dzv).astype(out_dtype)

        @pl.when((i == 0) | (i == n_lat))
        def _():
            dg_ref[...] = jnp.zeros_like(dg_ref)

        dg_ref[0] += jnp.sum(dzv * y_ref[...].astype(F32), axis=0, keepdims=True)

    return pl.pallas_call(
        body, name=name, grid=(R // ROW_TILE,),
        in_specs=[row, row, _seg_spec(n_lat, D)],
        out_specs=[row, _seg_acc_spec(n_lat, D)],
        out_shape=[SDS((R, D), out_dtype), SDS((2, 1, D), F32)],
        compiler_params=_params(1),
    )(dz, y, mod)


def _norm_bwd(dz, dh, x, nw, mod, k, n_lat, name, gated=None):
    R, D = x.shape
    row = pl.BlockSpec((ROW_TILE, D), lambda i: (i, 0))
    vec = pl.BlockSpec((1, D), lambda i: (0, 0))
    ins, specs = [dz, dh, x, nw.reshape(1, D), mod], [row, row, row, vec, _seg_spec(n_lat, D)]
    outs = [SDS((R, D), F32), SDS((1, D), F32), SDS((2, 1, D), F32), SDS((2, 1, D), F32)]
    ospecs = [row, vec, _seg_acc_spec(n_lat, D), _seg_acc_spec(n_lat, D)]
    if gated is not None:
        y, gmod, gk, dy_dtype = gated
        ins += [y, gmod]
        specs += [row, _seg_spec(n_lat, D)]
        outs += [SDS((R, D), dy_dtype), SDS((2, 1, D), F32)]
        ospecs += [row, _seg_acc_spec(n_lat, D)]

    def body(dz_ref, dh_ref, x_ref, nw_ref, m_ref, *rest):
        if gated is not None:
            y_ref, g_ref, dx_ref, dnw_ref, dsh_ref, dsc_ref, dy_ref, dg_ref = rest
        else:
            dx_ref, dnw_ref, dsh_ref, dsc_ref = rest
        i = pl.program_id(0)
        xv = x_ref[...]
        dhv = dh_ref[...].astype(F32)
        nwv = nw_ref[...]
        sc1 = 1.0 + m_ref[0, pl.ds(3 * k + 1, 1), :]
        r = lax.rsqrt(jnp.mean(xv * xv, axis=-1, keepdims=True) + EPS)
        xhat = xv * r
        a = dhv * (nwv * sc1)
        dx = dz_ref[...] + r * (a - xhat * jnp.mean(a * xhat, axis=-1, keepdims=True))
        dx_ref[...] = dx

        @pl.when(i == 0)
        def _():
            dnw_ref[...] = jnp.zeros_like(dnw_ref)

        @pl.when((i == 0) | (i == n_lat))
        def _():
            dsh_ref[...] = jnp.zeros_like(dsh_ref)
            dsc_ref[...] = jnp.zeros_like(dsc_ref)
            if gated is not None:
                dg_ref[...] = jnp.zeros_like(dg_ref)

        dnw_ref[...] += jnp.sum(dhv * xhat, axis=0, keepdims=True) * sc1
        dsh_ref[0] += jnp.sum(dhv, axis=0, keepdims=True)
        dsc_ref[0] += jnp.sum(dhv * xhat, axis=0, keepdims=True) * nwv
        if gated is not None:
            dy_ref[...] = (g_ref[0, pl.ds(3 * gk + 2, 1), :] * dx).astype(dy_dtype)
            dg_ref[0] += jnp.sum(dx * y_ref[...].astype(F32), axis=0, keepdims=True)

    return pl.pallas_call(
        body, name=name, grid=(R // ROW_TILE,), in_specs=specs, out_specs=ospecs, out_shape=outs,
        compiler_params=_params(1),
    )(*ins)


def _loss_bwd(xf, target, n_lat):
    R, D = xf.shape
    row = pl.BlockSpec((ROW_TILE, D), lambda i: (i, 0))
    tgt = pl.BlockSpec((ROW_TILE, D), lambda i: (jnp.minimum(i, n_lat - 1), 0))

    def body(x_ref, t_ref, dx_ref, loss_ref):
        i = pl.program_id(0)
        e = jnp.where(i < n_lat, x_ref[...] - t_ref[...], 0.0)
        dx_ref[...] = e * (1.0 / D)

        @pl.when(i == 0)
        def _():
            loss_ref[...] = jnp.zeros_like(loss_ref)

        loss_ref[...] += 0.5 * jnp.sum(jnp.mean(e * e, axis=-1, keepdims=True))

    return pl.pallas_call(
        body, name="loss_bwd", grid=(R // ROW_TILE,),
        in_specs=[row, tgt],
        out_specs=[row, pl.BlockSpec((8, 128), lambda i: (0, 0))],
        out_shape=[SDS((R, D), F32), SDS((8, 128), F32)],
        compiler_params=_params(1),
    )(xf, target)


def _halo_rows(dtype):
    return HALO * (4 // jnp.dtype(dtype).itemsize)


def _halo_specs(n_tiles, width, tile=ROW_TILE, rows=HALO):
    per = tile // rows
    prev = pl.BlockSpec((rows, width), lambda i: (jnp.maximum(i * per - 1, 0), 0))
    nxt = pl.BlockSpec((rows, width), lambda i: (jnp.minimum((i + 1) * per, n_tiles * per - 1), 0))
    return prev, nxt


SHIFT_K = 256


def _shift_matrix(n_out, first_row, deltas):
    half = n_out // 2
    out = []
    for h, start in enumerate((0, 2 * _halo_rows(BF16))):
        r = lax.broadcasted_iota(jnp.int32, (half, SHIFT_K), 0) + (first_row + h * half - start)
        j = lax.broadcasted_iota(jnp.int32, (half, SHIFT_K), 1)
        out.append(jnp.concatenate([(j == r + d).astype(F32) for d in deltas], axis=0).astype(BF16))
    return out


def _shifted_rows(t_ref, p_ref, n_ref, cols, first, last, picks, n_blocks):
    pr = jnp.where(first, jnp.zeros_like(p_ref[:, cols]), p_ref[:, cols])
    nx = jnp.where(last, jnp.zeros_like(n_ref[:, cols]), n_ref[:, cols])
    e = jnp.concatenate([pr, t_ref[:, cols], nx], axis=0)
    start = 2 * pr.shape[0]
    top, bot = _dot(picks[0], e[0:SHIFT_K]), _dot(picks[1], e[start:start + SHIFT_K])
    half = picks[0].shape[0] // n_blocks
    return [jnp.concatenate([top[k * half:(k + 1) * half], bot[k * half:(k + 1) * half]], axis=0)
            for k in range(n_blocks)]


def _edge_flags(i, n_lat, n_tiles):
    first = (i == 0) | (i == n_lat)
    last = (i == n_lat - 1) | (i == n_tiles - 1)
    return first, last


def _conv_gate_fwd(u, conv_w, conv_b, n_lat, name):
    R, F2 = u.shape
    F = F2 // 2
    n_tiles = R // ROW_TILE
    T = ROW_TILE
    cw = _pick(F, 256, 128)
    row = pl.BlockSpec((T, F2), lambda i: (i, 0))
    prev, nxt = _halo_specs(n_tiles, F2, rows=_halo_rows(u.dtype))

    assert u.dtype == BF16

    def body(u_ref, p_ref, n_ref, w_ref, b_ref, o_ref):
        i = pl.program_id(0)
        first, last = _edge_flags(i, n_lat, n_tiles)
        taps = _shift_matrix(T, _halo_rows(BF16), (-1, 0, 1))

        def conv(c0):
            cols = pl.ds(c0, cw)
            up, uv, un = _shifted_rows(u_ref, p_ref, n_ref, cols, first, last, taps, 3)
            return (up * w_ref[pl.ds(0, 1), cols] + uv * w_ref[pl.ds(1, 1), cols]
                    + un * w_ref[pl.ds(2, 1), cols] + b_ref[:, cols])

        for c0 in range(0, F, cw):
            ca, cv = conv(c0), conv(F + c0)
            o_ref[:, pl.ds(c0, cw)] = (ca * _sigmoid(ca) * cv).astype(BF16)

    return pl.pallas_call(
        body, name=name, grid=(n_tiles,),
        in_specs=[row, prev, nxt, pl.BlockSpec((3, F2), lambda i: (0, 0)),
                  pl.BlockSpec((1, F2), lambda i: (0, 0))],
        out_specs=pl.BlockSpec((T, F), lambda i: (i, 0)),
        out_shape=SDS((R, F), BF16), compiler_params=_params(1),
    )(u, u, u, conv_w, conv_b)


def _conv_gate_bwd(u, dgact, conv_w, conv_b, n_lat, name):
    R, F2 = u.shape
    F = F2 // 2
    n_tiles = R // ROW_TILE
    T, N = ROW_TILE, ROW_TILE + 2 * HALO
    cw = _pick(F, 256, 128)
    rowu = pl.BlockSpec((T, F2), lambda i: (i, 0))
    rowg = pl.BlockSpec((T, F), lambda i: (i, 0))
    pu, nu = _halo_specs(n_tiles, F2, rows=_halo_rows(u.dtype))
    pg, ng = _halo_specs(n_tiles, F, rows=_halo_rows(dgact.dtype))

    assert u.dtype == BF16 and dgact.dtype == BF16

    def body(u_ref, pu_ref, nu_ref, g_ref, pg_ref, ng_ref, w_ref, b_ref, du_ref, dw_ref, db_ref):
        i = pl.program_id(0)
        first, last = _edge_flags(i, n_lat, n_tiles)

        @pl.when(i == 0)
        def _():
            dw_ref[...] = jnp.zeros_like(dw_ref)
            db_ref[...] = jnp.zeros_like(db_ref)

        taps = _shift_matrix(N, _halo_rows(BF16) - HALO, (-1, 0, 1))
        same = _shift_matrix(N, _halo_rows(BF16) - HALO, (0,))

        def conv(c0):
            cols = pl.ds(c0, cw)
            up, e, un = _shifted_rows(u_ref, pu_ref, nu_ref, cols, first, last, taps, 3)
            c = (up * w_ref[pl.ds(0, 1), cols] + e * w_ref[pl.ds(1, 1), cols]
                 + un * w_ref[pl.ds(2, 1), cols] + b_ref[:, cols])
            return c, up, e, un

        def back(c0, dc, up, e, un):
            cols = pl.ds(c0, cw)
            du = (pltpu.roll(dc, N - 1, 0) * w_ref[pl.ds(0, 1), cols] + dc * w_ref[pl.ds(1, 1), cols]
                  + pltpu.roll(dc, 1, 0) * w_ref[pl.ds(2, 1), cols])
            du_ref[:, cols] = du[HALO:HALO + T].astype(BF16)
            dct = dc[HALO:HALO + T]
            dw_ref[pl.ds(0, 1), cols] += jnp.sum(dct * up[HALO:HALO + T], axis=0, keepdims=True)
            dw_ref[pl.ds(1, 1), cols] += jnp.sum(dct * e[HALO:HALO + T], axis=0, keepdims=True)
            dw_ref[pl.ds(2, 1), cols] += jnp.sum(dct * un[HALO:HALO + T], axis=0, keepdims=True)
            db_ref[:, cols] += jnp.sum(dct, axis=0, keepdims=True)

        for c0 in range(0, F, cw):
            dg, = _shifted_rows(g_ref, pg_ref, ng_ref, pl.ds(c0, cw), first, last, same, 1)
            ca, upa, ea, una = conv(c0)
            cv, upv, ev, unv = conv(F + c0)
            s = _sigmoid(ca)
            back(F + c0, dg * (ca * s), upv, ev, unv)
            back(c0, dg * cv * (s * (1.0 + ca * (1.0 - s))), upa, ea, una)

    return pl.pallas_call(
        body, name=name, grid=(n_tiles,),
        in_specs=[rowu, pu, nu, rowg, pg, ng, pl.BlockSpec((3, F2), lambda i: (0, 0)),
                  pl.BlockSpec((1, F2), lambda i: (0, 0))],
        out_specs=[rowu, pl.BlockSpec((3, F2), lambda i: (0, 0)), pl.BlockSpec((1, F2), lambda i: (0, 0))],
        out_shape=[SDS((R, F2), BF16), SDS((3, F2), F32), SDS((1, F2), F32)],
        compiler_params=_params(1),
    )(u, u, u, dgact, dgact, dgact, conv_w, conv_b)


def _pool_counts(i, n_lat, s_len, l_len, n_rows, offset):
    ctx = i >= n_lat
    t0 = jnp.where(ctx, i - n_lat, i) * ROW_TILE + offset
    seg = jnp.where(ctx, l_len, s_len)
    t = t0 + lax.broadcasted_iota(jnp.int32, (n_rows, 1), 0)
    out = []
    for win in POOL_WINDOWS:
        cnt = jnp.minimum(t + win // 2, seg) - jnp.maximum(t - win // 2, 0)
        out.append(jnp.maximum(cnt, 1).astype(F32))
    return out


def _window_sum(e, lo, hi, n):
    acc = None
    for j in range(lo, hi + 1):
        term = e if j == 0 else pltpu.roll(e, (-j) % n, 0)
        acc = term if acc is None else acc + term
    return acc


def _pool_fwd(h, w, b, scale, n_lat, s_len, l_len, name):
    R, D = h.shape
    G = D // 4
    n_tiles = R // ROW_TILE
    T, N = ROW_TILE, ROW_TILE + 2 * HALO
    row = pl.BlockSpec((T, D), lambda i: (i, 0))
    prev, nxt = _halo_specs(n_tiles, D)
    vec = pl.BlockSpec((1, D), lambda i: (0, 0))

    def body(h_ref, p_ref, n_ref, w_ref, b_ref, s_ref, y_ref):
        i = pl.program_id(0)
        first, last = _edge_flags(i, n_lat, n_tiles)
        cnts = _pool_counts(i, n_lat, s_len, l_len, T, 0)
        for g, win in enumerate(POOL_WINDOWS):
            cols = pl.ds(g * G, G)
            pr = jnp.where(first, 0.0, p_ref[:, cols])
            nx = jnp.where(last, 0.0, n_ref[:, cols])
            hv = h_ref[:, cols]
            e = jnp.concatenate([pr, hv, nx], axis=0)
            mean = _window_sum(e, -(win // 2), win // 2 - 1, N)[HALO:HALO + T] / cnts[g]
            yg = _dot(_bf(mean - hv), w_ref[g])
            y_ref[:, cols] = (yg + b_ref[:, cols]) * s_ref[:, cols]

    return pl.pallas_call(
        body, name=name, grid=(n_tiles,),
        in_specs=[row, prev, nxt, pl.BlockSpec((4, G, G), lambda i: (0, 0, 0)), vec, vec],
        out_specs=row, out_shape=SDS((R, D), F32), compiler_params=_params(1),
    )(h, h, h, w, b, scale)


def _pool_bwd(h, dy, w, b, scale, n_lat, s_len, l_len, name):
    R, D = h.shape
    G = D // 4
    n_tiles = R // ROW_TILE
    T, N = ROW_TILE, ROW_TILE + 2 * HALO
    row = pl.BlockSpec((T, D), lambda i: (i, 0))
    prev, nxt = _halo_specs(n_tiles, D)
    vec = pl.BlockSpec((1, D), lambda i: (0, 0))
    wspec = pl.BlockSpec((4, G, G), lambda i: (0, 0, 0))

    def body(h_ref, ph_ref, nh_ref, d_ref, pd_ref, nd_ref, w_ref, b_ref, s_ref,
             dh_ref, dw_ref, db_ref, ds_ref):
        i = pl.program_id(0)
        first, last = _edge_flags(i, n_lat, n_tiles)

        @pl.when(i == 0)
        def _():
            dw_ref[...] = jnp.zeros_like(dw_ref)
            db_ref[...] = jnp.zeros_like(db_ref)
            ds_ref[...] = jnp.zeros_like(ds_ref)

        cnts = _pool_counts(i, n_lat, s_len, l_len, T, 0)
        cnts_ext = _pool_counts(i, n_lat, s_len, l_len, N, -HALO)
        for g, win in enumerate(POOL_WINDOWS):
            cols = pl.ds(g * G, G)

            def ext(t_ref, p_ref, n_ref):
                pr = jnp.where(first, 0.0, p_ref[:, cols])
                nx = jnp.where(last, 0.0, n_ref[:, cols])
                return jnp.concatenate([pr, t_ref[:, cols], nx], axis=0)

            hv = h_ref[:, cols]
            mean = _window_sum(ext(h_ref, ph_ref, nh_ref), -(win // 2), win // 2 - 1, N)[HALO:HALO + T] / cnts[g]
            z = _bf(mean - hv)
            sc = s_ref[:, cols]
            dye = ext(d_ref, pd_ref, nd_ref)
            dt = _bf(dye * sc)
            dz = _dot_nt(dt, w_ref[g])
            dm = dz / cnts_ext[g]
            dh = _window_sum(dm, -(win // 2 - 1), win // 2, N) - dz
            dh_ref[:, cols] = dh[HALO:HALO + T]
            dyt = dye[HALO:HALO + T]
            dw_ref[g] += _dot_tn(z, dt[HALO:HALO + T])
            db_ref[:, cols] += jnp.sum(dyt * sc, axis=0, keepdims=True)
            ds_ref[:, cols] += jnp.sum(dyt * (_dot(z, w_ref[g]) + b_ref[:, cols]), axis=0, keepdims=True)

    return pl.pallas_call(
        body, name=name, grid=(n_tiles,),
        in_specs=[row, prev, nxt, row, prev, nxt, wspec, vec, vec],
        out_specs=[row, wspec, vec, vec],
        out_shape=[SDS((R, D), F32), SDS((4, G, G), F32), SDS((1, D), F32), SDS((1, D), F32)],
        compiler_params=_params(1),
    )(h, h, h, dy, dy, dy, w, b, scale)


def _rope_tables(s_len, l_len):
    t = jnp.arange(s_len)
    row = (t // GRID_W).astype(F32)
    col = (t % GRID_W).astype(F32)
    axis_dim = HEAD_DIM // 2
    inv = ROPE_THETA ** (-jnp.arange(0, axis_dim, 2, dtype=F32) / axis_dim)
    ar, ac = row[:, None] * inv, col[:, None] * inv
    cos = jnp.concatenate([jnp.cos(ar), jnp.cos(ar), jnp.cos(ac), jnp.cos(ac)], axis=-1)
    sin = jnp.concatenate([-jnp.sin(ar), jnp.sin(ar), -jnp.sin(ac), jnp.sin(ac)], axis=-1)
    cos = jnp.concatenate([cos, jnp.ones((l_len, HEAD_DIM), F32)], axis=0)
    sin = jnp.concatenate([sin, jnp.zeros((l_len, HEAD_DIM), F32)], axis=0)
    return cos, sin


def _swap_halves(v):
    lane = lax.broadcasted_iota(jnp.int32, v.shape, 1)
    return jnp.where((lane % 64) < 32, pltpu.roll(v, 96, 1), pltpu.roll(v, 32, 1))


def _qk_prep_fwd(qkv, q_gain, k_gain, cos, sin):
    R = qkv.shape[0]
    NQ, NK = N_HEADS * HEAD_DIM, N_KV * HEAD_DIM
    T = ROW_TILE
    vec = pl.BlockSpec((1, HEAD_DIM), lambda i: (0, 0))
    tab = pl.BlockSpec((T, HEAD_DIM), lambda i: (i, 0))

    def body(x_ref, qg_ref, kg_ref, c_ref, s_ref, q_ref, k_ref, v_ref):
        cosv, sinv = c_ref[...], s_ref[...]

        def prep(c0, gain):
            xh = x_ref[:, pl.ds(c0, HEAD_DIM)]
            xn = xh * lax.rsqrt(jnp.mean(xh * xh, axis=-1, keepdims=True) + EPS) * gain
            return _bf(xn * cosv + _swap_halves(xn) * sinv)

        for hd in range(N_HEADS):
            q_ref[:, pl.ds(hd * HEAD_DIM, HEAD_DIM)] = prep(hd * HEAD_DIM, qg_ref[...])
        for hd in range(N_KV):
            k_ref[:, pl.ds(hd * HEAD_DIM, HEAD_DIM)] = prep(NQ + hd * HEAD_DIM, kg_ref[...])
            v_ref[:, pl.ds(2 * hd * HEAD_DIM, HEAD_DIM)] = _bf(x_ref[:, pl.ds(NQ + NK + hd * HEAD_DIM, HEAD_DIM)])
            v_ref[:, pl.ds((2 * hd + 1) * HEAD_DIM, HEAD_DIM)] = jnp.ones((T, HEAD_DIM), BF16)

    return pl.pallas_call(
        body, name="qk_prep_fwd", grid=(R // T,),
        in_specs=[pl.BlockSpec((T, NQ + 2 * NK), lambda i: (i, 0)), vec, vec, tab, tab],
        out_specs=[pl.BlockSpec((T, NQ), lambda i: (i, 0)), pl.BlockSpec((T, NK), lambda i: (i, 0)),
                   pl.BlockSpec((T, 2 * NK), lambda i: (i, 0))],
        out_shape=[SDS((R, NQ), BF16), SDS((R, NK), BF16), SDS((R, 2 * NK), BF16)],
        compiler_params=_params(1),
    )(qkv, q_gain, k_gain, cos, sin)


def _qk_prep_bwd(qkv, dq, dk, dv, q_gain, k_gain, cos, sin):
    R = qkv.shape[0]
    NQ, NK = N_HEADS * HEAD_DIM, N_KV * HEAD_DIM
    T = ROW_TILE
    vec = pl.BlockSpec((1, HEAD_DIM), lambda i: (0, 0))
    tab = pl.BlockSpec((T, HEAD_DIM), lambda i: (i, 0))

    def body(x_ref, dq_ref, dk_ref, dv_ref, qg_ref, kg_ref, c_ref, s_ref, o_ref, dqg_ref, dkg_ref):
        i = pl.program_id(0)
        cosv, sinv = c_ref[...], s_ref[...]

        @pl.when(i == 0)
        def _():
            dqg_ref[...] = jnp.zeros_like(dqg_ref)
            dkg_ref[...] = jnp.zeros_like(dkg_ref)

        def back(c0, dout, gain, dg_ref):
            xh = x_ref[:, pl.ds(c0, HEAD_DIM)]
            r = lax.rsqrt(jnp.mean(xh * xh, axis=-1, keepdims=True) + EPS)
            xhat = xh * r
            dxn = dout * cosv + _swap_halves(dout * sinv)
            dg_ref[...] += jnp.sum(dxn * xhat, axis=0, keepdims=True)
            a = dxn * gain
            o_ref[:, pl.ds(c0, HEAD_DIM)] = _bf(r * (a - xhat * jnp.mean(a * xhat, axis=-1, keepdims=True)))

        for hd in range(N_HEADS):
            back(hd * HEAD_DIM, dq_ref[:, pl.ds(hd * HEAD_DIM, HEAD_DIM)], qg_ref[...], dqg_ref)
        for hd in range(N_KV):
            back(NQ + hd * HEAD_DIM, dk_ref[:, pl.ds(hd * HEAD_DIM, HEAD_DIM)], kg_ref[...], dkg_ref)
        o_ref[:, pl.ds(NQ + NK, NK)] = _bf(dv_ref[...])

    return pl.pallas_call(
        body, name="qk_prep_bwd", grid=(R // T,),
        in_specs=[pl.BlockSpec((T, NQ + 2 * NK), lambda i: (i, 0)), pl.BlockSpec((T, NQ), lambda i: (i, 0)),
                  pl.BlockSpec((T, NK), lambda i: (i, 0)), pl.BlockSpec((T, NK), lambda i: (i, 0)),
                  vec, vec, tab, tab],
        out_specs=[pl.BlockSpec((T, NQ + 2 * NK), lambda i: (i, 0)), vec, vec],
        out_shape=[SDS((R, NQ + 2 * NK), BF16), SDS((1, HEAD_DIM), F32), SDS((1, HEAD_DIM), F32)],
        compiler_params=_params(1),
    )(qkv, dq, dk, dv, q_gain, k_gain, cos, sin)


def _flash_fwd(q, k, v, s_len, l_len):
    R = q.shape[0]
    T = FLASH_FWD_TILE
    n_lat = s_len // T
    ck = _pick(s_len, 512, 128)
    scale = HEAD_DIM ** -0.5
    group = N_HEADS // N_KV
    GW = group * HEAD_DIM
    M = group * T
    chunks = s_len // ck
    to_log2 = scale * math.log2(math.e)

    def body(q_ref, k_ref, v_ref, o_ref, lse_ref, s_s, sc_s, ml_s, mb_s, acc_s):
        i = pl.program_id(1)
        qv = jnp.concatenate([q_ref[:, pl.ds(hh * HEAD_DIM, HEAD_DIM)] for hh in range(group)], axis=0)

        ml_s[...] = jnp.full_like(ml_s, -jnp.inf)

        def lane_max(s, n):
            m = ml_s[...]
            for t in range(n // HEAD_DIM):
                m = jnp.maximum(m, s[:, t * HEAD_DIM:(t + 1) * HEAD_DIM])
            ml_s[...] = m

        @pl.when(i < n_lat)
        def _():
            def loop(c, carry):
                s = _dot_nt(qv, k_ref[pl.ds(pl.multiple_of(c * ck, ck), ck), :])
                s_s[c] = s
                lane_max(s, ck)
                return carry
            lax.fori_loop(0, chunks, loop, 0, unroll=4 if chunks % 4 == 0 else 1)

        sc = _dot_nt(qv, k_ref[pl.ds(s_len, l_len), :])
        sc_s[...] = sc
        lane_max(sc, l_len)
        m_row = jnp.max(ml_s[...], axis=-1, keepdims=True) * to_log2
        mb_s[...] = jnp.broadcast_to(m_row, (M, ck))

        acc_s[...] = jnp.zeros_like(acc_s)

        @pl.when(i < n_lat)
        def _():
            def loop(c, carry):
                p = jnp.exp2(s_s[c] * to_log2 - mb_s[...])
                acc_s[...] += _dot(_bf(p), v_ref[pl.ds(pl.multiple_of(c * ck, ck), ck), :])
                return carry
            lax.fori_loop(0, chunks, loop, 0, unroll=4 if chunks % 4 == 0 else 1)

        p = jnp.exp2(sc_s[...] * to_log2 - mb_s[:, pl.ds(0, l_len)])
        acc_s[...] += _dot(_bf(p), v_ref[pl.ds(s_len, l_len), :])
        l_rep = acc_s[:, pl.ds(HEAD_DIM, HEAD_DIM)]
        o = acc_s[:, pl.ds(0, HEAD_DIM)] / l_rep
        for hh in range(group):
            o_ref[:, pl.ds(hh * HEAD_DIM, HEAD_DIM)] = o[hh * T:(hh + 1) * T]
        lse = (mb_s[:, pl.ds(0, HEAD_DIM)] + jnp.log2(l_rep)) * math.log(2.0)
        lse_ref[...] = jnp.max(lse, axis=-1, keepdims=True).reshape(group, T, 1)

    return pl.pallas_call(
        body, name="flash_fwd", grid=(N_KV, R // T),
        in_specs=[pl.BlockSpec((T, GW), lambda g, i: (i, g)),
                  pl.BlockSpec((R, HEAD_DIM), lambda g, i: (0, g)),
                  pl.BlockSpec((R, 2 * HEAD_DIM), lambda g, i: (0, g))],
        out_specs=[pl.BlockSpec((T, GW), lambda g, i: (i, g)),
                   pl.BlockSpec((group, T, 1), lambda g, i: (g, i, 0))],
        out_shape=[SDS((R, N_HEADS * HEAD_DIM), F32), SDS((N_HEADS, R, 1), F32)],
        scratch_shapes=[pltpu.VMEM((chunks, M, ck), F32), pltpu.VMEM((M, l_len), F32), pltpu.VMEM((M, HEAD_DIM), F32),
                        pltpu.VMEM((M, ck), F32), pltpu.VMEM((M, 2 * HEAD_DIM), F32)],
        compiler_params=_params(2),
    )(q, k, v)


def _flash_bwd(q, k, v, o, lse, do, s_len, l_len):
    R = q.shape[0]
    T = ROW_TILE
    n_lat = s_len // T
    ck = _pick(s_len, 512, 128)
    scale = HEAD_DIM ** -0.5
    group = N_HEADS // N_KV
    GW = group * HEAD_DIM
    qspec = pl.BlockSpec((T, GW), lambda g, i: (i, g))
    kspec = pl.BlockSpec((R, HEAD_DIM), lambda g, i: (0, g))

    M = group * T
    log2e = math.log2(math.e)

    def body(q_ref, do_ref, o_ref, lse_ref, k_ref, v_ref, dq_ref, dk_ref, dv_ref, dq_s, lse_s, delta_s):
        i = pl.program_id(1)

        @pl.when(i == 0)
        def _():
            dk_ref[...] = jnp.zeros_like(dk_ref)
            dv_ref[...] = jnp.zeros_like(dv_ref)

        def stacked(ref):
            return jnp.concatenate([ref[:, pl.ds(hh * HEAD_DIM, HEAD_DIM)] for hh in range(group)], axis=0)

        qv = stacked(q_ref)
        dov = stacked(do_ref)
        dob = _bf(dov)
        delta_s[...] = jnp.broadcast_to(jnp.sum(dov * stacked(o_ref), axis=-1, keepdims=True), (M, ck))
        lse_s[...] = jnp.broadcast_to(lse_ref[...].reshape(M, 1) * log2e, (M, ck))
        dq_s[...] = jnp.zeros_like(dq_s)

        def step(rows, n):
            kv, vv = k_ref[rows, :], v_ref[rows, :]
            p = jnp.exp2(_dot_nt(qv, kv) * (scale * log2e) - lse_s[:, pl.ds(0, n)])
            dv_ref[rows, :] += _dot_tn(_bf(p), dob)
            ds = _bf(p * (_dot_nt(dob, vv) - delta_s[:, pl.ds(0, n)]) * scale)
            dq_s[...] += _dot(ds, kv)
            dk_ref[rows, :] += _dot_tn(ds, qv)

        @pl.when(i < n_lat)
        def _():
            def loop(c, carry):
                step(pl.ds(pl.multiple_of(c * ck, ck), ck), ck)
                return carry
            lax.fori_loop(0, s_len // ck, loop, 0, unroll=4 if (s_len // ck) % 4 == 0 else 1)

        step(pl.ds(s_len, l_len), l_len)
        for hh in range(group):
            dq_ref[:, pl.ds(hh * HEAD_DIM, HEAD_DIM)] = dq_s[pl.ds(hh * T, T), :]

    return pl.pallas_call(
        body, name="flash_bwd", grid=(N_KV, R // T),
        in_specs=[qspec, qspec, qspec, pl.BlockSpec((group, T, 1), lambda g, i: (g, i, 0)), kspec,
                  pl.BlockSpec((R, HEAD_DIM), lambda g, i: (0, 2 * g))],
        out_specs=[qspec, kspec, kspec],
        out_shape=[SDS((R, N_HEADS * HEAD_DIM), F32), SDS((R, N_KV * HEAD_DIM), F32),
                   SDS((R, N_KV * HEAD_DIM), F32)],
        scratch_shapes=[pltpu.VMEM((M, HEAD_DIM), F32), pltpu.VMEM((M, ck), F32), pltpu.VMEM((M, ck), F32)],
        compiler_params=_params(2),
    )(q, do, o, lse, k, v)


K_SCALE = RET_DK ** -0.5


def _log_sigmoid(v):
    return -(jnp.maximum(-v, 0.0) + jnp.log(1.0 + jnp.exp(-jnp.abs(v))))


def _ret_decays(d, lg):
    C = RET_CHUNK
    ic = lax.broadcasted_iota(jnp.int32, (C, 1), 0)
    ir = lax.broadcasted_iota(jnp.int32, (1, C), 1)
    li = jnp.where(d == 0, ic, C - 1 - ic).astype(F32)
    lj = jnp.where(d == 0, ir, C - 1 - ir).astype(F32)
    diff = li - lj
    mask = jnp.where(diff >= 0, jnp.exp(jnp.maximum(diff, 0.0) * lg), 0.0)
    qd = jnp.exp((li + 1.0) * lg)
    kd = jnp.exp((C - 1.0 - li) * lg)
    cd = jnp.exp(C * lg)
    return li, diff, mask, qd, kd, cd


def _ctx_weights(d, t, lg, l_len):
    C = RET_CHUNK
    j = (t * C + lax.broadcasted_iota(jnp.int32, (C, 1), 0)).astype(F32)
    e = jnp.where(d == 0, (l_len - 1.0) - j, j)
    return e, jnp.exp(e * lg)


def _mirrored(x, i, n_lat):
    return jnp.where(i < n_lat, jnp.concatenate([x[RET_CHUNK:], x[:RET_CHUNK]], axis=0), x)


def _mirror_tile(n_lat):
    return lambda i: jnp.where(i < n_lat, n_lat - 1 - i, i)


def _ret_fwd(proj, lgt, s_len, l_len):
    R = proj.shape[0]
    C, H, DK, DV = RET_CHUNK, RET_HEADS, RET_DK, RET_DV
    nl, nc = s_len // C, l_len // C

    def stored(t):
        return jnp.where(t < nc, nl + t, jnp.maximum(t - nc, 0))

    def actual(d, t):
        n = jnp.maximum(t - nc, 0)
        return jnp.where(t < nc, nl + t, n if d == 0 else nl - 1 - n)

    def body(q0_ref, k0_ref, v0_ref, q1_ref, k1_ref, v1_ref, lg_ref, o_ref, st_ref, r_s):
        t = pl.program_id(0)
        qkv = ((q0_ref, k0_ref, v0_ref), (q1_ref, k1_ref, v1_ref))

        @pl.when(t == 0)
        def _():
            r_s[...] = jnp.zeros_like(r_s)

        def log_gamma(d, hh):
            return jnp.max(_log_sigmoid(lg_ref[d, hh]), axis=-1, keepdims=True)

        @pl.when(t < nc)
        def _():
            for d, (q_ref, k_ref, v_ref) in enumerate(qkv):
                for hh in range(H):
                    qc, vc = pl.ds(hh * DK, DK), pl.ds(hh * DV, DV)
                    _, w = _ctx_weights(d, t, log_gamma(d, hh), l_len)
                    r_s[d, hh] += _dot_tn(_bf(k_ref[:, qc] * K_SCALE * w), _bf(v_ref[:, vc]))
                    o_ref[d, :, vc] = jnp.zeros((C, DV), F32)

        @pl.when(t >= nc)
        def _():
            for d, (q_ref, k_ref, v_ref) in enumerate(qkv):
                for hh in range(H):
                    qc, vc = pl.ds(hh * DK, DK), pl.ds(hh * DV, DV)
                    _, _, mask, qd, kd, cd = _ret_decays(d, log_gamma(d, hh))
                    qb, kv, vb = _bf(q_ref[:, qc]), k_ref[:, qc] * K_SCALE, _bf(v_ref[:, vc])
                    r = r_s[d, hh]
                    st_ref[d, hh, 0] = r
                    att = _dot_nt(qb, _bf(kv)) * mask
                    o_ref[d, :, vc] = _dot(_bf(att), vb) + _dot(qb, _bf(r)) * qd
                    r_s[d, hh] = r * cd + _dot_tn(_bf(kv * kd), vb)

    def rows(d):
        return [pl.BlockSpec((C, H * DK), lambda t: (actual(d, t), 0)),
                pl.BlockSpec((C, H * DK), lambda t: (actual(d, t), 1)),
                pl.BlockSpec((C, H * DV), lambda t: (actual(d, t), 1))]

    return pl.pallas_call(
        body, name="ret_fwd", grid=(nc + nl,),
        in_specs=rows(0) + rows(1) + [pl.BlockSpec((2, H, 1, 128), lambda t: (0, 0, 0, 0))],
        out_specs=[pl.BlockSpec((2, C, H * DV), lambda t: (0, stored(t), 0)),
                   pl.BlockSpec((2, H, 1, DK, DV), lambda t: (0, 0, jnp.maximum(t - nc, 0), 0, 0))],
        out_shape=[SDS((2, R, H * DV), F32), SDS((2, H, nl, DK, DV), F32)],
        scratch_shapes=[pltpu.VMEM((2, H, DK, DV), F32)],
        compiler_params=_params(1),
    )(proj, proj, proj, proj, proj, proj, lgt)


def _ret_bwd(proj, lgt, states, do, s_len, l_len):
    R = proj.shape[0]
    C, H, DK, DV = RET_CHUNK, RET_HEADS, RET_DK, RET_DV
    nl, nc = s_len // C, l_len // C
    last = nl + nc - 1

    def stored(t):
        return jnp.where(t < nl, jnp.maximum(nl - 1 - t, 0), t)

    def actual(d, t):
        return stored(t) if d == 0 else t

    def body(q0_ref, k0_ref, v0_ref, do0_ref, q1_ref, k1_ref, v1_ref, do1_ref, lg_ref, st_ref,
             dq_ref, dk_ref, dv_ref, dlg_ref, dr_s, dl_s):
        t = pl.program_id(0)
        ins = ((q0_ref, k0_ref, v0_ref, do0_ref), (q1_ref, k1_ref, v1_ref, do1_ref))

        def log_gamma(d, hh):
            return jnp.max(_log_sigmoid(lg_ref[d, hh]), axis=-1, keepdims=True)

        @pl.when(t == 0)
        def _():
            dr_s[...] = jnp.zeros_like(dr_s)
            dl_s[...] = jnp.zeros_like(dl_s)

        @pl.when(t < nl)
        def _():
            for d, (q_ref, k_ref, v_ref, do_ref) in enumerate(ins):
                for hh in range(H):
                    qc, vc = pl.ds(hh * DK, DK), pl.ds(hh * DV, DV)
                    li, diff, mask, qd, kd, cd = _ret_decays(d, log_gamma(d, hh))
                    qv, kv, vv, dov = q_ref[:, qc], k_ref[:, qc] * K_SCALE, v_ref[:, vc], do_ref[:, vc]
                    qb, kb, vb, dob = _bf(qv), _bf(kv), _bf(vv), _bf(dov)
                    r, drn = st_ref[d, hh, 0], dr_s[d, hh]
                    rb, drb = _bf(r), _bf(drn)
                    p = _dot_nt(qb, kb)
                    dp = _dot_nt(dob, vb) * mask
                    dpb = _bf(dp)
                    doq = _bf(dov * qd)
                    dq_inter = _dot_nt(doq, rb)
                    dk_state = kd * _dot_nt(vb, drb)
                    dq_ref[d, :, qc] = _dot(dpb, kb) + dq_inter
                    dk_ref[d, :, qc] = (_dot_tn(dpb, qb) + dk_state) * K_SCALE
                    dv_ref[d, :, vc] = _dot_tn(_bf(p * mask), dob) + _dot(_bf(kv * kd), drb)
                    dr_s[d, hh] = cd * drn + _dot_tn(qb, doq)
                    dl_s[d, hh] += (jnp.sum(dp * p * diff) + jnp.sum((li + 1.0) * qv * dq_inter)
                                    + jnp.sum((C - 1.0 - li) * kv * dk_state) + C * jnp.sum(cd * r * drn))

        @pl.when(t >= nl)
        def _():
            for d, (q_ref, k_ref, v_ref, do_ref) in enumerate(ins):
                for hh in range(H):
                    qc, vc = pl.ds(hh * DK, DK), pl.ds(hh * DV, DV)
                    e, w = _ctx_weights(d, t - nl, log_gamma(d, hh), l_len)
                    kv, vb, drb = k_ref[:, qc] * K_SCALE, _bf(v_ref[:, vc]), _bf(dr_s[d, hh])
                    dkc = w * _dot_nt(vb, drb)
                    dq_ref[d, :, qc] = jnp.zeros((C, DK), F32)
                    dk_ref[d, :, qc] = dkc * K_SCALE
                    dv_ref[d, :, vc] = _dot(_bf(kv * w), drb)
                    dl_s[d, hh] += jnp.sum(e * kv * dkc)

        @pl.when(t == last)
        def _():
            for d in range(2):
                for hh in range(H):
                    dlg_ref[d, hh] = dl_s[d, hh] * (1.0 / (1.0 + jnp.exp(lg_ref[d, hh])))

    def rows(d):
        return [pl.BlockSpec((C, H * DK), lambda t: (actual(d, t), 0)),
                pl.BlockSpec((C, H * DK), lambda t: (actual(d, t), 1)),
                pl.BlockSpec((C, H * DV), lambda t: (actual(d, t), 1)),
                pl.BlockSpec((C, H * DV), lambda t: (actual(d, t), 0))]

    return pl.pallas_call(
        body, name="ret_bwd", grid=(nl + nc,),
        in_specs=rows(0) + rows(1) + [
            pl.BlockSpec((2, H, 1, 128), lambda t: (0, 0, 0, 0)),
            pl.BlockSpec((2, H, 1, DK, DV), lambda t: (0, 0, jnp.maximum(nl - 1 - t, 0), 0, 0))],
        out_specs=[pl.BlockSpec((2, C, H * DK), lambda t: (0, stored(t), 0)),
                   pl.BlockSpec((2, C, H * DK), lambda t: (0, stored(t), 0)),
                   pl.BlockSpec((2, C, H * DV), lambda t: (0, stored(t), 0)),
                   pl.BlockSpec((2, H, 1, 128), lambda t: (0, 0, 0, 0))],
        out_shape=[SDS((2, R, H * DK), F32), SDS((2, R, H * DK), F32), SDS((2, R, H * DV), F32),
                   SDS((2, H, 1, 128), F32)],
        scratch_shapes=[pltpu.VMEM((2, H, DK, DV), F32), pltpu.VMEM((2, H, 1, 128), F32)],
        compiler_params=_params(1),
    )(proj, proj, proj, do, proj, proj, proj, do, lgt, states)


def _readout_fwd(o2, proj, gn_w, n_lat):
    R = proj.shape[0]
    H, DV = RET_HEADS, RET_DV
    W = H * DV
    T = ROW_TILE
    assert T == 2 * RET_CHUNK

    def body(o_ref, ob_ref, g_ref, w_ref, out_ref):
        i = pl.program_id(0)
        for hh in range(H):
            cols = pl.ds(hh * DV, DV)
            y = o_ref[0, :, cols] + _mirrored(ob_ref[0, :, cols], i, n_lat)
            yc = y - jnp.mean(y, axis=-1, keepdims=True)
            yn = yc * lax.rsqrt(jnp.mean(yc * yc, axis=-1, keepdims=True) + EPS) * w_ref[:, cols]
            g = g_ref[:, cols]
            out_ref[:, cols] = _bf(g * _sigmoid(g) * yn)

    return pl.pallas_call(
        body, name="readout_fwd", grid=(R // T,),
        in_specs=[pl.BlockSpec((1, T, W), lambda i: (0, i, 0)),
                  pl.BlockSpec((1, T, W), lambda i: (1, _mirror_tile(n_lat)(i), 0)),
                  pl.BlockSpec((T, W), lambda i: (i, 2)), pl.BlockSpec((1, W), lambda i: (0, 0))],
        out_specs=pl.BlockSpec((T, W), lambda i: (i, 0)),
        out_shape=SDS((R, W), BF16), compiler_params=_params(1),
    )(o2, o2, proj, gn_w)


def _readout_bwd(o2, proj, gn_w, dgated, n_lat):
    R = proj.shape[0]
    H, DV = RET_HEADS, RET_DV
    W = H * DV
    T = ROW_TILE

    def body(o_ref, ob_ref, g_ref, w_ref, d_ref, do_ref, dg_ref, dw_ref):
        i = pl.program_id(0)

        @pl.when(i == 0)
        def _():
            dw_ref[...] = jnp.zeros_like(dw_ref)

        for hh in range(H):
            cols = pl.ds(hh * DV, DV)
            y = o_ref[0, :, cols] + _mirrored(ob_ref[0, :, cols], i, n_lat)
            yc = y - jnp.mean(y, axis=-1, keepdims=True)
            rstd = lax.rsqrt(jnp.mean(yc * yc, axis=-1, keepdims=True) + EPS)
            yn0 = yc * rstd
            wv = w_ref[:, cols]
            g = g_ref[:, cols]
            s = _sigmoid(g)
            dgt = d_ref[:, cols]
            dyn = dgt * (g * s)
            dg_ref[:, cols] = _bf(dgt * (yn0 * wv) * (s * (1.0 + g * (1.0 - s))))
            dw_ref[:, cols] += jnp.sum(dyn * yn0, axis=0, keepdims=True)
            a = dyn * wv
            do_ref[:, cols] = rstd * (a - jnp.mean(a, axis=-1, keepdims=True)
                                      - yn0 * jnp.mean(a * yn0, axis=-1, keepdims=True))

    return pl.pallas_call(
        body, name="readout_bwd", grid=(R // T,),
        in_specs=[pl.BlockSpec((1, T, W), lambda i: (0, i, 0)),
                  pl.BlockSpec((1, T, W), lambda i: (1, _mirror_tile(n_lat)(i), 0)),
                  pl.BlockSpec((T, W), lambda i: (i, 2)),
                  pl.BlockSpec((1, W), lambda i: (0, 0)), pl.BlockSpec((T, W), lambda i: (i, 0))],
        out_specs=[pl.BlockSpec((T, W), lambda i: (i, 0)), pl.BlockSpec((T, W), lambda i: (i, 0)),
                   pl.BlockSpec((1, W), lambda i: (0, 0))],
        out_shape=[SDS((R, W), F32), SDS((R, W), BF16), SDS((1, W), F32)],
        compiler_params=_params(1),
    )(o2, o2, proj, gn_w, dgated)


def _ret_dproj(dq2, dk2, dv2, dg, n_lat):
    R = dg.shape[0]
    NQ, NV = RET_HEADS * RET_DK, RET_HEADS * RET_DV
    T = ROW_TILE

    def body(dq_ref, dqb_ref, dk_ref, dkb_ref, dv_ref, dvb_ref, dg_ref, o_ref):
        i = pl.program_id(0)
        o_ref[:, pl.ds(0, NQ)] = _bf(dq_ref[0] + _mirrored(dqb_ref[0], i, n_lat))
        o_ref[:, pl.ds(NQ, NQ)] = _bf(dk_ref[0] + _mirrored(dkb_ref[0], i, n_lat))
        o_ref[:, pl.ds(2 * NQ, NV)] = _bf(dv_ref[0] + _mirrored(dvb_ref[0], i, n_lat))
        o_ref[:, pl.ds(2 * NQ + NV, NV)] = dg_ref[...]

    def both(width):
        return [pl.BlockSpec((1, T, width), lambda i: (0, i, 0)),
                pl.BlockSpec((1, T, width), lambda i: (1, _mirror_tile(n_lat)(i), 0))]

    return pl.pallas_call(
        body, name="ret_dproj", grid=(R // T,),
        in_specs=both(NQ) + both(NQ) + both(NV) + [pl.BlockSpec((T, NV), lambda i: (i, 0))],
        out_specs=pl.BlockSpec((T, 2 * NQ + 2 * NV), lambda i: (i, 0)),
        out_shape=SDS((R, 2 * NQ + 2 * NV), BF16), compiler_params=_params(1),
    )(dq2, dq2, dk2, dk2, dv2, dv2, dg)


def _silu(v):
    return v * _sigmoid(v)


def _ada_fwd(c_rows, ada_w, ada_b_shard):
    depth, D, cols = ada_w.shape

    def body(c_ref, w_ref, b_ref, o_ref):
        o_ref[0] = _dot(_bf(_silu(c_ref[...])), _bf(w_ref[0])) + b_ref[0]

    return pl.pallas_call(
        body, name="ada_fwd", grid=(depth,),
        in_specs=[pl.BlockSpec((16, D), lambda i: (0, 0)), pl.BlockSpec((1, D, cols), lambda i: (i, 0, 0)),
                  pl.BlockSpec((1, 1, cols), lambda i: (i, 0, 0))],
        out_specs=pl.BlockSpec((1, 16, cols), lambda i: (i, 0, 0)),
        out_shape=SDS((depth, 16, cols), F32), compiler_params=_params(1),
    )(c_rows, ada_w, ada_b_shard)


def _ada_bwd(c_rows, ada_w, d_lat, d_ctx):
    depth, D, cols = ada_w.shape

    def body(c_ref, w_ref, dl_ref, dc_ref, dw_ref, pc_ref):
        i = pl.program_id(0)
        cv = c_ref[...]
        a = _silu(cv)
        dcs = jnp.broadcast_to(jnp.sum(dc_ref[0], axis=0, keepdims=True), (8, cols))
        dw_ref[0] = _dot_tn(_bf(a[0:8]), _bf(dl_ref[0])) + _dot_tn(_bf(a[8:16]), _bf(dcs))

        @pl.when(i == 0)
        def _():
            pc_ref[...] = jnp.zeros_like(pc_ref)

        pc_ref[...] += _dot_nt(_bf(dcs), _bf(w_ref[0]))

        @pl.when(i == depth - 1)
        def _():
            cc = c_ref[pl.ds(8, 1), :]
            s = _sigmoid(cc)
            pc_ref[...] = pc_ref[...] * (s * (1.0 + cc * (1.0 - s)))

    return pl.pallas_call(
        body, name="ada_bwd", grid=(depth,),
        in_specs=[pl.BlockSpec((16, D), lambda i: (0, 0)), pl.BlockSpec((1, D, cols), lambda i: (i, 0, 0)),
                  pl.BlockSpec((1, 8, cols), lambda i: (i, 0, 0)), pl.BlockSpec((1, 8, cols), lambda i: (i, 0, 0))],
        out_specs=[pl.BlockSpec((1, D, cols), lambda i: (i, 0, 0)), pl.BlockSpec((8, D), lambda i: (0, 0))],
        out_shape=[SDS((depth, D, cols), F32), SDS((8, D), F32)], compiler_params=_params(1),
    )(c_rows, ada_w, d_lat, d_ctx)


def _adamw(w, g, m, v, name):
    shape = w.shape
    n = g.shape[0]
    cols = shape[-1]
    rows = w.size // cols
    tr = _pick(rows, 512, 8) if rows * cols * 4 > (1 << 20) else rows
    spec = pl.BlockSpec((tr, cols), lambda i: (i, 0))

    def body(w_ref, g_ref, m_ref, v_ref, go_ref, d_ref, mo_ref, vo_ref):
        gs = g_ref[0].astype(F32)
        for k in range(1, n):
            gs = gs + g_ref[k].astype(F32)
        mn = ADAM_B1 * m_ref[...] + (1.0 - ADAM_B1) * gs
        vn = ADAM_B2 * v_ref[...] + (1.0 - ADAM_B2) * jnp.square(gs)
        m_hat = mn / (1.0 - ADAM_B1 ** ADAM_STEP)
        v_hat = vn / (1.0 - ADAM_B2 ** ADAM_STEP)
        go_ref[...] = gs
        d_ref[...] = -ADAM_LR * (m_hat / (jnp.sqrt(v_hat) + ADAM_EPS) + ADAM_WD * w_ref[...])
        mo_ref[...] = mn
        vo_ref[...] = vn

    outs = pl.pallas_call(
        body, name=name, grid=(rows // tr,),
        in_specs=[spec, pl.BlockSpec((n, tr, cols), lambda i: (0, i, 0)), spec, spec],
        out_specs=[spec] * 4, out_shape=[SDS((rows, cols), F32)] * 4, compiler_params=_params(1),
    )(w.reshape(rows, cols), g.reshape(n, rows, cols), m.reshape(rows, cols), v.reshape(rows, cols))
    return tuple(o.reshape(shape) for o in outs)


def _sum_slots(own, recv, name):
    shape, n, cols = own.shape, recv.shape[0], own.shape[-1]
    own, recv = own.reshape(-1, cols), recv.reshape(n, -1, cols)
    rows = own.shape[0]
    tr = _pick(rows, 512, 16)

    def body(own_ref, r_ref, o_ref):
        acc = own_ref[...].astype(F32)
        for k in range(n):
            acc = acc + r_ref[k].astype(F32)
        o_ref[...] = acc

    return pl.pallas_call(
        body, name=name, grid=(rows // tr,),
        in_specs=[pl.BlockSpec((tr, cols), lambda i: (i, 0)), pl.BlockSpec((n, tr, cols), lambda i: (0, i, 0))],
        out_specs=pl.BlockSpec((tr, cols), lambda i: (i, 0)),
        out_shape=SDS((rows, cols), F32), compiler_params=_params(1),
    )(own, recv).reshape(shape)


def _position():
    return lax.axis_index("x"), lax.axis_index("y"), lax.axis_index("c")


def _peer(k, x, y, c):
    return (1 - x if k & 4 else x, 1 - y if k & 2 else y, 1 - c if k & 1 else c)


def _index(pos):
    return 4 * pos[0] + 2 * pos[1] + pos[2]


def _gather_small(v, name):
    rows, lanes = v.shape

    def body(x_ref, out_ref, send_sems, recv_sems, local_sem):
        me = _position()
        mine = pltpu.make_async_copy(x_ref, out_ref.at[_index(me)], local_sem)
        mine.start()

        def copy(k, slot):
            return pltpu.make_async_remote_copy(
                src_ref=x_ref, dst_ref=out_ref.at[slot], send_sem=send_sems.at[k - 1],
                recv_sem=recv_sems.at[k - 1], device_id=_peer(k, *me), device_id_type=MESH)

        sends = [copy(k, _index(me)) for k in range(1, N_DEV)]
        for cp in sends:
            cp.start()
        for k in range(1, N_DEV):
            copy(k, _index(_peer(k, *me))).wait_recv()
        for cp in sends:
            cp.wait_send()
        mine.wait()

    return pl.pallas_call(
        body, name=name, out_shape=SDS((N_DEV, rows, lanes), v.dtype),
        in_specs=[pl.BlockSpec(memory_space=pltpu.VMEM)],
        out_specs=pl.BlockSpec(memory_space=pltpu.VMEM),
        scratch_shapes=[pltpu.SemaphoreType.DMA((N_DEV - 1,)), pltpu.SemaphoreType.DMA((N_DEV - 1,)),
                        pltpu.SemaphoreType.DMA],
        compiler_params=pltpu.CompilerParams(vmem_limit_bytes=VMEM_LIMIT_V7X),
    )(v)


HBM_SPEC = pl.BlockSpec(memory_space=pltpu.HBM)
SEM_SPEC = pl.BlockSpec(memory_space=pltpu.SEMAPHORE)
SPLIT_EFFECT = pltpu.SideEffectType.DATAFLOW_SIDE_EFFECTING


def _split_start(srcs, gather, name):
    n = len(srcs)
    lands = [jnp.zeros(((N_DEV,) + s.shape) if gather else s.shape, s.dtype) for s in srcs]

    def body(*refs):
        src_refs, land_refs, sems, token = refs[:n], refs[n:2 * n], refs[2 * n:4 * n], refs[-1]
        me = _position()
        for a in range(n):
            for k in range(1, N_DEV):
                peer = _peer(k, *me)
                pltpu.make_async_remote_copy(
                    src_ref=src_refs[a] if gather else src_refs[a].at[_index(peer)],
                    dst_ref=land_refs[a].at[_index(me)], send_sem=sems[2 * a], recv_sem=sems[2 * a + 1],
                    device_id=peer, device_id_type=MESH).start()
        token[...] = jnp.zeros_like(token)

    hbm = lambda arrays: tuple(pltpu.HBM(a.shape, a.dtype) for a in arrays)
    outs = pl.pallas_call(
        body, name=name,
        out_shape=(pltpu.SemaphoreType.DMA(()),) * (2 * n) + hbm(srcs) + hbm(lands) + (SDS((8, 128), F32),),
        in_specs=(HBM_SPEC,) * (2 * n),
        out_specs=(SEM_SPEC,) * (2 * n) + (HBM_SPEC,) * (2 * n) + (pl.BlockSpec(memory_space=pltpu.VMEM),),
        input_output_aliases={a: 2 * n + a for a in range(2 * n)},
        compiler_params=pltpu.CompilerParams(has_side_effects=SPLIT_EFFECT),
    )(*[pltpu.with_memory_space_constraint(a, pltpu.HBM) for a in list(srcs) + lands])
    return outs[:2 * n], outs[2 * n:3 * n], outs[3 * n:4 * n], outs[-1]


def _split_wait(flight, after, name):
    sems, srcs, lands, _ = flight
    n = len(srcs)

    def body(*refs):
        land_refs, sem_refs = refs[n:2 * n], refs[2 * n:4 * n]
        me = _position()
        for a in range(n):
            seven = land_refs[a].at[pl.ds(0, N_DEV - 1)]
            copies = pltpu.make_async_remote_copy(
                src_ref=seven, dst_ref=seven, send_sem=sem_refs[2 * a], recv_sem=sem_refs[2 * a + 1],
                device_id=_peer(1, *me), device_id_type=MESH)
            copies.wait_send()
            copies.wait_recv()

    outs = pl.pallas_call(
        body, name=name,
        out_shape=tuple(pltpu.HBM(a.shape, a.dtype) for a in list(srcs) + list(lands)),
        in_specs=(HBM_SPEC,) * (2 * n) + (SEM_SPEC,) * (2 * n) + (pl.BlockSpec(memory_space=pl.ANY),),
        out_specs=(HBM_SPEC,) * (2 * n), input_output_aliases={a: a for a in range(2 * n)},
        compiler_params=pltpu.CompilerParams(has_side_effects=SPLIT_EFFECT),
    )(*srcs, *lands, *sems, after)
    return outs[:n], outs[n:]


def _pack_rows(arrays, lanes, dtype):
    flat = jnp.concatenate([a.astype(dtype).reshape(-1) for a in arrays])
    pad = (-flat.size) % (16 * lanes)
    if pad:
        flat = jnp.concatenate([flat, jnp.zeros((pad,), dtype)])
    return flat.reshape(-1, lanes)


def _unpack_rows(packed, shapes):
    n = packed.shape[0]
    flat = packed.reshape(n, -1)
    out, off = [], 0
    for shp in shapes:
        size = math.prod(shp)
        out.append(flat[:, off:off + size].reshape((n,) + tuple(shp)))
        off += size
    return out


def _unshard(g8, axis):
    moved = jnp.moveaxis(g8, 0, axis)
    shp = list(moved.shape)
    shp[axis:axis + 2] = [shp[axis] * shp[axis + 1]]
    return moved.reshape(shp)


def _split8(full, axis):
    shp = list(full.shape)
    shp[axis:axis + 1] = [N_DEV, shp[axis] // N_DEV]
    return jnp.moveaxis(full.reshape(shp), axis, 0)


def _my_shard(g, axis, me):
    size = g.shape[axis + 1] // N_DEV
    return lax.dynamic_slice_in_dim(g, me * size, size, axis=axis + 1)


BIG_WEIGHTS = ("ffn_w_up", "ffn_w_down", "attn_w_qkv", "attn_w_o", "ret_w_in", "ret_w_out", "pool_w")
LAYER_WEIGHTS = (
    (("ffn_w_up", 0, "cols"), ("ffn_w_down", 0, "rows"), ("pool_w", 0, "pool")),
    (("ffn_w_up", 1, "cols"), ("ffn_w_down", 1, "rows"), ("attn_w_qkv", 0, "cols"), ("attn_w_o", 0, "rows")),
    (("ffn_w_up", 2, "cols"), ("ffn_w_down", 2, "rows"), ("ret_w_in", 0, "cols"), ("ret_w_out", 0, "rows")),
    (("ffn_w_up", 3, "cols"), ("ffn_w_down", 3, "rows"), ("pool_w", 1, "pool")),
)


GATHER_GROUPS = (LAYER_WEIGHTS[0][:2],) + LAYER_WEIGHTS[1:]
GRAD_GROUPS = {"3": LAYER_WEIGHTS[3], "2": LAYER_WEIGHTS[2], "1": LAYER_WEIGHTS[1],
               "0ffn": LAYER_WEIGHTS[0][:2], "0mix": LAYER_WEIGHTS[0][2:]}


def _shard_to_send(w, kind):
    w = w.astype(BF16)
    return w.T if kind == "cols" else w


def _full_from_land(land, kind):
    return _unshard(land, 1) if kind == "pool" else land.reshape(-1, land.shape[-1])


def _grad_to_send(g, kind):
    return _split8(g, 1).astype(BF16) if kind == "pool" else g.astype(BF16).reshape(N_DEV, -1, g.shape[-1])


def _shard_grad(gsum, kind):
    return gsum.T if kind == "cols" else gsum
SMALL_SHARDED = (("norm_w", 2), ("pool_b", 1), ("pool_scale", 1), ("ret_gn_w", 1), ("ffn_conv_w", 2))
REPLICATED = ("ada_b", "attn_q_gain", "attn_k_gain", "ret_decay_logit", "ffn_conv_b")
WEIGHT_ORDER = ("c_ctx", "ada_w", "ada_b", "norm_w", "pool_w", "pool_b", "pool_scale", "attn_w_qkv",
                "attn_q_gain", "attn_k_gain", "attn_w_o", "ret_w_in", "ret_decay_logit", "ret_gn_w",
                "ret_w_out", "ffn_w_up", "ffn_conv_w", "ffn_conv_b", "ffn_w_down")


def _local_step(x0, target, mods, P, get_weights, put_grads, s_len, l_len):
    n_lat = s_len // ROW_TILE
    nw = P["norm_w"]
    lgt = jnp.broadcast_to(P["ret_decay_logit"][0][:, :, None, None], (2, RET_HEADS, 1, 128))
    cos, sin = _rope_tables(s_len, l_len)
    h_dtype = [F32 if i % 3 == 0 else BF16 for i in range(DEPTH)]
    saved = []
    mods = list(mods)
    X = x0
    h = _res_norm(X, None, None, 0, nw[0, 0], mods[0], 0, h_dtype[0], n_lat, "norm_first")
    for i in range(DEPTH):
        kind, j, mod = i % 3, i // 3, mods[i]
        W = dict(get_weights(i, "mix", h))
        sv = {"X": X, "h": h, "W": W}
        if kind == 0:
            y = _pool_fwd(h, W["pool_w"], P["pool_b"][j:j + 1], P["pool_scale"][j:j + 1],
                          n_lat, s_len, l_len, f"pool_fwd{i}")
        elif kind == 1:
            qkv = _mm(h, W["attn_w_qkv"], "nt", F32, f"qkv{i}")
            q, k, v = _qk_prep_fwd(qkv, P["attn_q_gain"][j:j + 1], P["attn_k_gain"][j:j + 1], cos, sin)
            o, lse = _flash_fwd(q, k, v, s_len, l_len)
            y = _mm(o, W["attn_w_o"], "nn", F32, f"attn_out{i}")
            sv.update(qkv=qkv, q=q, k=k, v=v, o=o, lse=lse)
        else:
            proj = _mm(h, W["ret_w_in"], "nt", F32, f"ret_in{i}")
            o2, states = _ret_fwd(proj, lgt, s_len, l_len)
            gated = _readout_fwd(o2, proj, P["ret_gn_w"][j:j + 1], n_lat)
            y = _mm(gated, W["ret_w_out"], "nn", F32, f"ret_out{i}")
            sv.update(proj=proj, o2=o2, states=states, gated=gated)
        X1, h2 = _res_norm(X, y, mod, 0, nw[i, 1], mod, 1, BF16, n_lat, f"res_norm_mid{i}")
        W.update(get_weights(i, "ffn", h2))
        u = _mm(h2, W["ffn_w_up"], "nt", FFN_HIDDEN_DTYPE, f"ffn_up{i}")
        gact = _conv_gate_fwd(u, P["ffn_conv_w"][i], P["ffn_conv_b"][i:i + 1], n_lat, f"conv_gate_fwd{i}")
        f = _mm(gact, W["ffn_w_down"], "nn", F32, f"ffn_down{i}")
        sv.update(y=y, X1=X1, h2=h2, u=u, gact=gact, f=f)
        saved.append(sv)
        if i + 1 < DEPTH:
            X, h = _res_norm(X1, f, mod, 1, nw[i + 1, 0], mods[i + 1], 0, h_dtype[i + 1], n_lat,
                             f"res_norm_end{i}")
        else:
            X = _res_norm(X1, f, mod, 1, None, None, 0, None, n_lat, "res_last")

    dX, loss = _loss_bwd(X, target, n_lat)
    G = {name: [None] * P[name].shape[0] for name in
         ("pool_b", "pool_scale", "attn_q_gain", "attn_k_gain", "ret_decay_logit", "ret_gn_w", "ffn_conv_w",
          "ffn_conv_b")}
    dnw = [[None, None] for _ in range(DEPTH)]
    dmods = [None] * DEPTH
    for i in reversed(range(DEPTH)):
        kind, j, mod, sv = i % 3, i // 3, mods[i], saved[i]
        W, gl = sv["W"], {}
        if i == DEPTH - 1:
            df, dg2 = _gate_bwd(dX, sv["f"], mod, 1, BF16, n_lat, f"gate_bwd_ffn{i}")
        dgact = _mm(df, W["ffn_w_down"], "nt", FFN_HIDDEN_DTYPE, f"ffn_down_dx{i}")
        gl["ffn_w_down"] = _mm(sv["gact"], df, "tn", BF16, f"ffn_down_dw{i}")
        du, dcw, dcb = _conv_gate_bwd(sv["u"], dgact, P["ffn_conv_w"][i], P["ffn_conv_b"][i:i + 1], n_lat,
                                      f"conv_gate_bwd{i}")
        G["ffn_conv_w"][i], G["ffn_conv_b"][i] = dcw, dcb[0]
        dh2 = _mm(du, W["ffn_w_up"], "nn", F32, f"ffn_up_dx{i}")
        gl["ffn_w_up"] = _mm(du, sv["h2"], "tn", BF16, f"ffn_up_dw{i}")
        if i == 0:
            mod = mod + put_grads("0ffn", gl)
        dX1, dnw[i][1], dsh2, dsc2, dy, dg1 = _norm_bwd(
            dX, dh2, sv["X1"], nw[i, 1], mod, 1, n_lat, f"norm_bwd_ffn{i}",
            gated=(sv["y"], mod, 0, F32 if kind == 0 else BF16))
        h = sv["h"]
        if kind == 0:
            dh, dpw, dpb, dps = _pool_bwd(h, dy, W["pool_w"], P["pool_b"][j:j + 1], P["pool_scale"][j:j + 1],
                                          n_lat, s_len, l_len, f"pool_bwd{i}")
            gl["pool_w"], G["pool_b"][j], G["pool_scale"][j] = dpw, dpb[0], dps[0]
        elif kind == 1:
            do = _mm(dy, W["attn_w_o"], "nt", F32, f"attn_out_dx{i}")
            gl["attn_w_o"] = _mm(sv["o"], dy, "tn", BF16, f"attn_out_dw{i}")
            dq, dk, dv = _flash_bwd(sv["q"], sv["k"], sv["v"], sv["o"], sv["lse"], do, s_len, l_len)
            dqkv, dqg, dkg = _qk_prep_bwd(sv["qkv"], dq, dk, dv, P["attn_q_gain"][j:j + 1],
                                          P["attn_k_gain"][j:j + 1], cos, sin)
            G["attn_q_gain"][j], G["attn_k_gain"][j] = dqg[0], dkg[0]
            dh = _mm(dqkv, W["attn_w_qkv"], "nn", F32, f"qkv_dx{i}")
            gl["attn_w_qkv"] = _mm(dqkv, h, "tn", BF16, f"qkv_dw{i}")
        else:
            dgated = _mm(dy, W["ret_w_out"], "nt", F32, f"ret_out_dx{i}")
            gl["ret_w_out"] = _mm(sv["gated"], dy, "tn", BF16, f"ret_out_dw{i}")
            do, dg, dgn = _readout_bwd(sv["o2"], sv["proj"], P["ret_gn_w"][j:j + 1], dgated, n_lat)
            dq2, dk2, dv2, dlg = _ret_bwd(sv["proj"], lgt, sv["states"], do, s_len, l_len)
            dproj = _ret_dproj(dq2, dk2, dv2, dg, n_lat)
            G["ret_gn_w"][j], G["ret_decay_logit"][j] = dgn[0], dlg[:, :, 0, 0]
            dh = _mm(dproj, W["ret_w_in"], "nn", F32, f"ret_in_dx{i}")
            gl["ret_w_in"] = _mm(dproj, h, "tn", BF16, f"ret_in_dw{i}")
        zero = put_grads(str(i) if i > 0 else "0mix", gl)
        if i > 0:
            mods[i - 1] = mods[i - 1] + zero
            dX, dnw[i][0], dsh1, dsc1, df_below, dg2_below = _norm_bwd(
                dX1, dh, sv["X"], nw[i, 0], mod, 0, n_lat, f"norm_bwd_mix{i}",
                gated=(saved[i - 1]["f"], mods[i - 1], 1, BF16))
        else:
            dX, dnw[i][0], dsh1, dsc1 = _norm_bwd(dX1, dh, sv["X"], nw[i, 0], mod, 0, n_lat, f"norm_bwd_mix{i}")
        dmods[i] = jnp.concatenate([dsh1, dsc1, dg1, dsh2, dsc2, dg2], axis=1)
        if i > 0:
            df, dg2 = df_below, dg2_below
    grads = {name: jnp.stack(parts) for name, parts in G.items()}
    grads["norm_w"] = jnp.stack([jnp.concatenate(pair, axis=0) for pair in dnw])
    return loss, dX, grads, jnp.stack(dmods)


def kernel(x, c, ctx, c_ctx, ada_w, ada_b, norm_w, pool_w, pool_b, pool_scale, attn_w_qkv, attn_q_gain,
           attn_k_gain, attn_w_o, ret_w_in, ret_decay_logit, ret_gn_w, ret_w_out, ffn_w_up, ffn_conv_w,
           ffn_conv_b, ffn_w_down, loss_target, m_c_ctx, m_ada_w, m_ada_b, m_norm_w, m_pool_w, m_pool_b,
           m_pool_scale, m_attn_w_qkv, m_attn_q_gain, m_attn_k_gain, m_attn_w_o, m_ret_w_in,
           m_ret_decay_logit, m_ret_gn_w, m_ret_w_out, m_ffn_w_up, m_ffn_conv_w, m_ffn_conv_b, m_ffn_w_down,
           v_c_ctx, v_ada_w, v_ada_b, v_norm_w, v_pool_w, v_pool_b, v_pool_scale, v_attn_w_qkv, v_attn_q_gain,
           v_attn_k_gain, v_attn_w_o, v_ret_w_in, v_ret_decay_logit, v_ret_gn_w, v_ret_w_out, v_ffn_w_up,
           v_ffn_conv_w, v_ffn_conv_b, v_ffn_w_down):
    A = dict(locals())
    me = _index(_position())
    s_len, D = x.shape[1], x.shape[2]
    l_len = ctx.shape[1]
    assert s_len % ROW_TILE == 0 and l_len % ROW_TILE == 0 and s_len % GRID_W == 0

    small = [A[n] for n, _ in SMALL_SHARDED]
    first_parts = [c] + small + [pool_w[0]]
    got = _gather_small(_pack_rows(first_parts, 128, F32), "gather_c_small")
    parts = _unpack_rows(got, [a.shape for a in first_parts])
    c_all = parts[0].reshape(N_DEV, D)
    P = {n: _unshard(g8, ax) for (n, ax), g8 in zip(SMALL_SHARDED, parts[1:-1])}
    first_pool_w = _unshard(parts[-1], 1).astype(BF16)

    c_rows = jnp.concatenate([c_all, c_ctx.reshape(1, D), jnp.zeros((7, D), F32)], axis=0)
    cols = ada_w.shape[2]
    ada_b_shard = lax.dynamic_slice_in_dim(ada_b, me * cols, cols, axis=1).reshape(DEPTH, 1, cols)
    mod_shard = _ada_fwd(c_rows, ada_w, ada_b_shard)
    got = _gather_small(mod_shard.reshape(-1, 128), "gather_mod").reshape(N_DEV, DEPTH, 16, cols)
    mod_lat = lax.dynamic_index_in_dim(got, me, axis=2, keepdims=False)
    mod_ctx = got[:, :, 8, :]
    mods = jnp.stack([jnp.moveaxis(mod_lat, 0, 1).reshape(DEPTH, 6, D),
                      jnp.moveaxis(mod_ctx, 0, 1).reshape(DEPTH, 6, D)], axis=1)

    shards = [[_shard_to_send(A[n][j], kind) for n, j, kind in group] for group in GATHER_GROUPS]
    shards, mods = lax.optimization_barrier((shards, mods))
    flights, zero = [], jnp.zeros((), F32)
    for i in range(DEPTH):
        flights.append(_split_start(shards[i], True, f"gather_start{i}"))
        zero = zero + flights[i][3][0, 0]
    mods = [mods[i] for i in range(DEPTH)]
    mods[0] = mods[0] + zero
    for n in REPLICATED:
        P[n] = A[n]
    landed = {}

    def get_weights(i, part, x_now):
        if i == 0 and part == "mix":
            return {"pool_w": first_pool_w}
        if i not in landed:
            owns, lands = _split_wait(flights[i], x_now, f"gather_wait{i}")
            landed[i] = {n: _full_from_land(lax.dynamic_update_index_in_dim(land, own, me, axis=0), kind)
                         for (n, j, kind), own, land in zip(GATHER_GROUPS[i], owns, lands)}
        return landed[i]

    sent = {}

    def put_grads(group, gl):
        sent[group] = _split_start([_grad_to_send(gl[n], kind) for n, j, kind in GRAD_GROUPS[group]], False,
                                   f"exchange_start_{group}")
        return sent[group][3][0, 0]

    x0 = jnp.concatenate([x[0], ctx[0]], axis=0)
    loss8, dx0, G, dmods = _local_step(x0, loss_target[0], mods, P, get_weights, put_grads, s_len, l_len)
    loss = lax.psum(loss8[0, 0], ("x", "y", "c"))
    grad_x = dx0[:s_len].reshape(x.shape)

    small_names = ["dmods"] + list(REPLICATED[1:]) + [n for n, _ in SMALL_SHARDED]
    small_parts = [dmods] + [G[n] for n in small_names[1:]]
    got = _gather_small(_pack_rows(small_parts, 128, F32), "gather_small_grads")
    S8 = dict(zip(small_names, _unpack_rows(got, [a.shape for a in small_parts])))

    dm = S8["dmods"].reshape(N_DEV, DEPTH, 2, 6 * D)
    dm_mine = lax.dynamic_slice_in_dim(dm, me * cols, cols, axis=3)
    g_ada_w, pc = _ada_bwd(c_rows, ada_w, jnp.moveaxis(dm_mine[:, :, 0], 0, 1), jnp.moveaxis(dm_mine[:, :, 1], 0, 1))
    pc8 = _gather_small(pc.reshape(-1, 128), "gather_c_ctx_grad").reshape(N_DEV, 8, D)

    def owner_sums(group, after):
        sends, lands = _split_wait(sent[group], after, f"exchange_wait_{group}")
        out = {}
        for (n, j, kind), send, land in zip(GRAD_GROUPS[group], sends, lands):
            own = lax.dynamic_index_in_dim(send, me, axis=0, keepdims=False)
            out[(n, j)] = _shard_grad(_sum_slots(own, land, f"sum_slots_{n}{j}"), kind)
        return out

    shard_grads = {}
    for group in ("3", "2", "1", "0mix"):
        shard_grads.update(owner_sums(group, pc8))

    g_in = {"c_ctx": pc8[:, 0, :], "ada_w": g_ada_w[None],
            "ada_b": jnp.moveaxis(dm, 2, 1).reshape(2 * N_DEV, DEPTH, 6 * D)}
    for n in REPLICATED[1:]:
        g_in[n] = S8[n]
    for n, ax in SMALL_SHARDED:
        g_in[n] = _my_shard(S8[n], ax, me)

    def stacked(n):
        return jnp.stack([shard_grads[(n, j)] for j in range(A[n].shape[0])])[None]

    late = [n for n, j, kind in GRAD_GROUPS["0ffn"]]
    for n in BIG_WEIGHTS:
        if n not in late:
            g_in[n] = stacked(n)
    res = {n: _adamw(A[n], g_in[n], A["m_" + n], A["v_" + n], "adamw_" + n) for n in WEIGHT_ORDER if n not in late}
    done = sum(res[n][1].reshape(-1)[0] for n in res)
    shard_grads.update(owner_sums("0ffn", done.reshape(1, 1)))
    for n in late:
        res[n] = _adamw(A[n], stacked(n), A["m_" + n], A["v_" + n], "adamw_" + n)
    outs = [loss, grad_x]
    for slot in range(4):
        outs += [res[n][slot] for n in WEIGHT_ORDER]
    return tuple(outs)
```

```python
import functools
import math

import jax
import jax.numpy as jnp
from jax import lax
from jax.experimental import pallas as pl
from jax.experimental.pallas import tpu as pltpu

F32 = jnp.float32
BF16 = jnp.bfloat16
SDS = jax.ShapeDtypeStruct
MESH = pl.DeviceIdType.MESH

N_DEV = 8
EPS = 1e-6
DEPTH = 4
GRID_W = 64
POOL_WINDOWS = (2, 4, 8, 16)
N_HEADS = 8
N_KV = 2
HEAD_DIM = 128
ROPE_THETA = 10000.0
RET_HEADS = 4
RET_DK = 256
RET_DV = 512
RET_CHUNK = 128
ADAM_LR = 0.001
ADAM_B1 = 0.9
ADAM_B2 = 0.999
ADAM_EPS = 1e-08
ADAM_WD = 0.01
ADAM_STEP = 10

ROW_TILE = 256
FFN_HIDDEN_DTYPE = BF16
FLASH_FWD_TILE = 128
HALO = 8
VMEM_LIMIT_V7X = 56 * 1024 * 1024


def _params(n_axes=0):
    sem = ("arbitrary",) * n_axes if n_axes else None
    return pltpu.CompilerParams(dimension_semantics=sem, vmem_limit_bytes=VMEM_LIMIT_V7X)


def _pick(n, cap, mult):
    best = None
    for d in range(mult, min(n, cap) + 1, mult):
        if n % d == 0:
            best = d
    return best if best is not None else n


def _dot(a, b):
    return jnp.dot(a, b, preferred_element_type=F32)


def _dot_nt(a, b):
    return lax.dot_general(a, b, (((1,), (1,)), ((), ())), preferred_element_type=F32)


def _dot_tn(a, b):
    return lax.dot_general(a, b, (((0,), (0,)), ((), ())), preferred_element_type=F32)


def _bf(v):
    return v.astype(BF16)


def _sigmoid(v):
    return 0.5 * jnp.tanh(0.5 * v) + 0.5


MM_VMEM_BUDGET = 40 * 1024 * 1024
MM_STEP_BYTES = 1 << 20
MM_ACC_PASS_BYTES = 8


def _divisors(n, mult, cap):
    return [d for d in range(mult, min(n, cap) + 1, mult) if n % d == 0] or [n]


def _mm_tiles(mode, M, N, K, a_item, b_item, o_item):
    best = None
    for tm in _divisors(M, 128 if mode == "tn" else 16, 2816):
        for tn in _divisors(N, 128, 2048):
            for tk in _divisors(K, 16 if mode == "tn" else 128, 2816):
                ni, nj, nk = M // tm, N // tn, K // tk
                vmem = 2 * (tm * tk * a_item + tk * tn * b_item + tm * tn * o_item) + tm * tn * 4
                if vmem > MM_VMEM_BUDGET:
                    continue
                a_reads = 1 if nk == 1 else nj
                b_reads = 1 if (nk == 1 and nj == 1) else ni
                cost = (M * K * a_item * a_reads + K * N * b_item * b_reads + M * N * o_item
                        + ni * nj * nk * MM_STEP_BYTES + (nk - 1) * M * N * MM_ACC_PASS_BYTES)
                if best is None or cost < best[0]:
                    best = (cost, tm, tn, tk)
    return best[1:]


def _mm(a, b, mode, out_dtype, name):
    if mode == "nn":
        (M, K), (K2, N) = a.shape, b.shape
    elif mode == "nt":
        (M, K), (N, K2) = a.shape, b.shape
    else:
        (K, M), (K2, N) = a.shape, b.shape
    assert K == K2, (a.shape, b.shape, mode)
    tm, tn, tk = _mm_tiles(mode, M, N, K, a.dtype.itemsize, b.dtype.itemsize, jnp.dtype(out_dtype).itemsize)
    nk = K // tk
    if mode == "nn":
        a_spec = pl.BlockSpec((tm, tk), lambda i, j, k: (i, k))
        b_spec = pl.BlockSpec((tk, tn), lambda i, j, k: (k, j))
    elif mode == "nt":
        a_spec = pl.BlockSpec((tm, tk), lambda i, j, k: (i, k))
        b_spec = pl.BlockSpec((tn, tk), lambda i, j, k: (j, k))
    else:
        a_spec = pl.BlockSpec((tk, tm), lambda i, j, k: (k, i))
        b_spec = pl.BlockSpec((tk, tn), lambda i, j, k: (k, j))
    dot = {"nn": _dot, "nt": _dot_nt, "tn": _dot_tn}[mode]

    def body(a_ref, b_ref, o_ref, acc_ref):
        part = dot(_bf(a_ref[...]), _bf(b_ref[...]))
        if nk == 1:
            o_ref[...] = part.astype(out_dtype)
        else:
            k = pl.program_id(2)

            @pl.when(k == 0)
            def _():
                acc_ref[...] = part

            @pl.when(k > 0)
            def _():
                acc_ref[...] += part

            @pl.when(k == nk - 1)
            def _():
                o_ref[...] = acc_ref[...].astype(out_dtype)

    return pl.pallas_call(
        body, name=name, grid=(M // tm, N // tn, nk),
        in_specs=[a_spec, b_spec],
        out_specs=pl.BlockSpec((tm, tn), lambda i, j, k: (i, j)),
        out_shape=SDS((M, N), out_dtype),
        scratch_shapes=[pltpu.VMEM((tm, tn), F32)],
        compiler_params=_params(3),
    )(a, b)


def _seg_spec(n_lat, d):
    return pl.BlockSpec((1, 6, d), lambda i: ((i >= n_lat).astype(jnp.int32), 0, 0))


def _seg_acc_spec(n_lat, d):
    return pl.BlockSpec((1, 1, d), lambda i: ((i >= n_lat).astype(jnp.int32), 0, 0))


def _res_norm(x, y, gmod, gk, nw, nmod, nk, h_dtype, n_lat, name):
    R, D = x.shape
    has_res, has_norm = y is not None, nw is not None
    row = pl.BlockSpec((ROW_TILE, D), lambda i: (i, 0))
    vec = pl.BlockSpec((1, D), lambda i: (0, 0))
    ins, specs, outs, ospecs = [x], [row], [], []
    if has_res:
        ins += [y, gmod]
        specs += [row, _seg_spec(n_lat, D)]
        outs.append(SDS((R, D), F32))
        ospecs.append(row)
    if has_norm:
        ins += [nw.reshape(1, D), nmod]
        specs += [vec, _seg_spec(n_lat, D)]
        outs.append(SDS((R, D), h_dtype))
        ospecs.append(row)

    def body(*refs):
        refs = list(refs)
        z = refs.pop(0)[...]
        if has_res:
            y_ref, g_ref = refs.pop(0), refs.pop(0)
            z = z + g_ref[0, pl.ds(3 * gk + 2, 1), :] * y_ref[...].astype(F32)
        if has_norm:
            nw_ref, m_ref = refs.pop(0), refs.pop(0)
        if has_res:
            refs.pop(0)[...] = z
        if has_norm:
            r = lax.rsqrt(jnp.mean(z * z, axis=-1, keepdims=True) + EPS)
            h = (z * r) * nw_ref[...]
            h = h * (1.0 + m_ref[0, pl.ds(3 * nk + 1, 1), :]) + m_ref[0, pl.ds(3 * nk, 1), :]
            refs.pop(0)[...] = h.astype(h_dtype)

    res = pl.pallas_call(
        body, name=name, grid=(R // ROW_TILE,), in_specs=specs, out_specs=ospecs,
        out_shape=outs, compiler_params=_params(1),
    )(*ins)
    return res if len(res) > 1 else res[0]


def _gate_bwd(dz, y, mod, k, out_dtype, n_lat, name):
    R, D = dz.shape
    row = pl.BlockSpec((ROW_TILE, D), lambda i: (i, 0))

    def body(dz_ref, y_ref, m_ref, dy_ref, dg_ref):
        i = pl.program_id(0)
        dzv = dz_ref[...]
        dy_ref[...] = (m_ref[0, pl.ds(3 * k + 2, 1), :] * dzv).astype(out_dtype)

        @pl.when((i == 0) | (i == n_lat))
        def _():
            dg_ref[...] = jnp.zeros_like(dg_ref)

        dg_ref[0] += jnp.sum(dzv * y_ref[...].astype(F32), axis=0, keepdims=True)

    return pl.pallas_call(
        body, name=name, grid=(R // ROW_TILE,),
        in_specs=[row, row, _seg_spec(n_lat, D)],
        out_specs=[row, _seg_acc_spec(n_lat, D)],
        out_shape=[SDS((R, D), out_dtype), SDS((2, 1, D), F32)],
        compiler_params=_params(1),
    )(dz, y, mod)


def _norm_bwd(dz, dh, x, nw, mod, k, n_lat, name, gated=None):
    R, D = x.shape
    row = pl.BlockSpec((ROW_TILE, D), lambda i: (i, 0))
    vec = pl.BlockSpec((1, D), lambda i: (0, 0))
    ins, specs = [dz, dh, x, nw.reshape(1, D), mod], [row, row, row, vec, _seg_spec(n_lat, D)]
    outs = [SDS((R, D), F32), SDS((1, D), F32), SDS((2, 1, D), F32), SDS((2, 1, D), F32)]
    ospecs = [row, vec, _seg_acc_spec(n_lat, D), _seg_acc_spec(n_lat, D)]
    if gated is not None:
        y, gmod, gk, dy_dtype = gated
        ins += [y, gmod]
        specs += [row, _seg_spec(n_lat, D)]
        outs += [SDS((R, D), dy_dtype), SDS((2, 1, D), F32)]
        ospecs += [row, _seg_acc_spec(n_lat, D)]

    def body(dz_ref, dh_ref, x_ref, nw_ref, m_ref, *rest):
        if gated is not None:
            y_ref, g_ref, dx_ref, dnw_ref, dsh_ref, dsc_ref, dy_ref, dg_ref = rest
        else:
            dx_ref, dnw_ref, dsh_ref, dsc_ref = rest
        i = pl.program_id(0)
        xv = x_ref[...]
        dhv = dh_ref[...].astype(F32)
        nwv = nw_ref[...]
        sc1 = 1.0 + m_ref[0, pl.ds(3 * k + 1, 1), :]
        r = lax.rsqrt(jnp.mean(xv * xv, axis=-1, keepdims=True) + EPS)
        xhat = xv * r
        a = dhv * (nwv * sc1)
        dx = dz_ref[...] + r * (a - xhat * jnp.mean(a * xhat, axis=-1, keepdims=True))
        dx_ref[...] = dx

        @pl.when(i == 0)
        def _():
            dnw_ref[...] = jnp.zeros_like(dnw_ref)

        @pl.when((i == 0) | (i == n_lat))
        def _():
            dsh_ref[...] = jnp.zeros_like(dsh_ref)
            dsc_ref[...] = jnp.zeros_like(dsc_ref)
            if gated is not None:
                dg_ref[...] = jnp.zeros_like(dg_ref)

        dnw_ref[...] += jnp.sum(dhv * xhat, axis=0, keepdims=True) * sc1
        dsh_ref[0] += jnp.sum(dhv, axis=0, keepdims=True)
        dsc_ref[0] += jnp.sum(dhv * xhat, axis=0, keepdims=True) * nwv
        if gated is not None:
            dy_ref[...] = (g_ref[0, pl.ds(3 * gk + 2, 1), :] * dx).astype(dy_dtype)
            dg_ref[0] += jnp.sum(dx * y_ref[...].astype(F32), axis=0, keepdims=True)

    return pl.pallas_call(
        body, name=name, grid=(R // ROW_TILE,), in_specs=specs, out_specs=ospecs, out_shape=outs,
        compiler_params=_params(1),
    )(*ins)


def _loss_bwd(xf, target, n_lat):
    R, D = xf.shape
    row = pl.BlockSpec((ROW_TILE, D), lambda i: (i, 0))
    tgt = pl.BlockSpec((ROW_TILE, D), lambda i: (jnp.minimum(i, n_lat - 1), 0))

    def body(x_ref, t_ref, dx_ref, loss_ref):
        i = pl.program_id(0)
        e = jnp.where(i < n_lat, x_ref[...] - t_ref[...], 0.0)
        dx_ref[...] = e * (1.0 / D)

        @pl.when(i == 0)
        def _():
            loss_ref[...] = jnp.zeros_like(loss_ref)

        loss_ref[...] += 0.5 * jnp.sum(jnp.mean(e * e, axis=-1, keepdims=True))

    return pl.pallas_call(
        body, name="loss_bwd", grid=(R // ROW_TILE,),
        in_specs=[row, tgt],
        out_specs=[row, pl.BlockSpec((8, 128), lambda i: (0, 0))],
        out_shape=[SDS((R, D), F32), SDS((8, 128), F32)],
        compiler_params=_params(1),
    )(xf, target)


def _halo_rows(dtype):
    return HALO * (4 // jnp.dtype(dtype).itemsize)


def _halo_specs(n_tiles, width, tile=ROW_TILE, rows=HALO):
    per = tile // rows
    prev = pl.BlockSpec((rows, width), lambda i: (jnp.maximum(i * per - 1, 0), 0))
    nxt = pl.BlockSpec((rows, width), lambda i: (jnp.minimum((i + 1) * per, n_tiles * per - 1), 0))
    return prev, nxt


SHIFT_K = 256


def _shift_matrix(n_out, first_row, deltas):
    half = n_out // 2
    out = []
    for h, start in enumerate((0, 2 * _halo_rows(BF16))):
        r = lax.broadcasted_iota(jnp.int32, (half, SHIFT_K), 0) + (first_row + h * half - start)
        j = lax.broadcasted_iota(jnp.int32, (half, SHIFT_K), 1)
        out.append(jnp.concatenate([(j == r + d).astype(F32) for d in deltas], axis=0).astype(BF16))
    return out


def _shifted_rows(t_ref, p_ref, n_ref, cols, first, last, picks, n_blocks):
    pr = jnp.where(first, jnp.zeros_like(p_ref[:, cols]), p_ref[:, cols])
    nx = jnp.where(last, jnp.zeros_like(n_ref[:, cols]), n_ref[:, cols])
    e = jnp.concatenate([pr, t_ref[:, cols], nx], axis=0)
    start = 2 * pr.shape[0]
    top, bot = _dot(picks[0], e[0:SHIFT_K]), _dot(picks[1], e[start:start + SHIFT_K])
    half = picks[0].shape[0] // n_blocks
    return [jnp.concatenate([top[k * half:(k + 1) * half], bot[k * half:(k + 1) * half]], axis=0)
            for k in range(n_blocks)]


def _edge_flags(i, n_lat, n_tiles):
    first = (i == 0) | (i == n_lat)
    last = (i == n_lat - 1) | (i == n_tiles - 1)
    return first, last


def _conv_gate_fwd(u, conv_w, conv_b, n_lat, name):
    R, F2 = u.shape
    F = F2 // 2
    n_tiles = R // ROW_TILE
    T = ROW_TILE
    cw = _pick(F, 256, 128)
    row = pl.BlockSpec((T, F2), lambda i: (i, 0))
    prev, nxt = _halo_specs(n_tiles, F2, rows=_halo_rows(u.dtype))

    assert u.dtype == BF16

    def body(u_ref, p_ref, n_ref, w_ref, b_ref, o_ref):
        i = pl.program_id(0)
        first, last = _edge_flags(i, n_lat, n_tiles)
        taps = _shift_matrix(T, _halo_rows(BF16), (-1, 0, 1))

        def conv(c0):
            cols = pl.ds(c0, cw)
            up, uv, un = _shifted_rows(u_ref, p_ref, n_ref, cols, first, last, taps, 3)
            return (up * w_ref[pl.ds(0, 1), cols] + uv * w_ref[pl.ds(1, 1), cols]
                    + un * w_ref[pl.ds(2, 1), cols] + b_ref[:, cols])

        for c0 in range(0, F, cw):
            ca, cv = conv(c0), conv(F + c0)
            o_ref[:, pl.ds(c0, cw)] = (ca * _sigmoid(ca) * cv).astype(BF16)

    return pl.pallas_call(
        body, name=name, grid=(n_tiles,),
        in_specs=[row, prev, nxt, pl.BlockSpec((3, F2), lambda i: (0, 0)),
                  pl.BlockSpec((1, F2), lambda i: (0, 0))],
        out_specs=pl.BlockSpec((T, F), lambda i: (i, 0)),
        out_shape=SDS((R, F), BF16), compiler_params=_params(1),
    )(u, u, u, conv_w, conv_b)


def _conv_gate_bwd(u, dgact, conv_w, conv_b, n_lat, name):
    R, F2 = u.shape
    F = F2 // 2
    n_tiles = R // ROW_TILE
    T, N = ROW_TILE, ROW_TILE + 2 * HALO
    cw = _pick(F, 256, 128)
    rowu = pl.BlockSpec((T, F2), lambda i: (i, 0))
    rowg = pl.BlockSpec((T, F), lambda i: (i, 0))
    pu, nu = _halo_specs(n_tiles, F2, rows=_halo_rows(u.dtype))
    pg, ng = _halo_specs(n_tiles, F, rows=_halo_rows(dgact.dtype))

    assert u.dtype == BF16 and dgact.dtype == BF16

    def body(u_ref, pu_ref, nu_ref, g_ref, pg_ref, ng_ref, w_ref, b_ref, du_ref, dw_ref, db_ref):
        i = pl.program_id(0)
        first, last = _edge_flags(i, n_lat, n_tiles)

        @pl.when(i == 0)
        def _():
            dw_ref[...] = jnp.zeros_like(dw_ref)
            db_ref[...] = jnp.zeros_like(db_ref)

        taps = _shift_matrix(N, _halo_rows(BF16) - HALO, (-1, 0, 1))
        same = _shift_matrix(N, _halo_rows(BF16) - HALO, (0,))

        def conv(c0):
            cols = pl.ds(c0, cw)
            up, e, un = _shifted_rows(u_ref, pu_ref, nu_ref, cols, first, last, taps, 3)
            c = (up * w_ref[pl.ds(0, 1), cols] + e * w_ref[pl.ds(1, 1), cols]
                 + un * w_ref[pl.ds(2, 1), cols] + b_ref[:, cols])
            return c, up, e, un

        def back(c0, dc, up, e, un):
            cols = pl.ds(c0, cw)
            du = (pltpu.roll(dc, N - 1, 0) * w_ref[pl.ds(0, 1), cols] + dc * w_ref[pl.ds(1, 1), cols]
                  + pltpu.roll(dc, 1, 0) * w_ref[pl.ds(2, 1), cols])
            du_ref[:, cols] = du[HALO:HALO + T].astype(BF16)
            dct = dc[HALO:HALO + T]
            dw_ref[pl.ds(0, 1), cols] += jnp.sum(dct * up[HALO:HALO + T], axis=0, keepdims=True)
            dw_ref[pl.ds(1, 1), cols] += jnp.sum(dct * e[HALO:HALO + T], axis=0, keepdims=True)
            dw_ref[pl.ds(2, 1), cols] += jnp.sum(dct * un[HALO:HALO + T], axis=0, keepdims=True)
            db_ref[:, cols] += jnp.sum(dct, axis=0, keepdims=True)

        for c0 in range(0, F, cw):
            dg, = _shifted_rows(g_ref, pg_ref, ng_ref, pl.ds(c0, cw), first, last, same, 1)
            ca, upa, ea, una = conv(c0)
            cv, upv, ev, unv = conv(F + c0)
            s = _sigmoid(ca)
            back(F + c0, dg * (ca * s), upv, ev, unv)
            back(c0, dg * cv * (s * (1.0 + ca * (1.0 - s))), upa, ea, una)

    return pl.pallas_call(
        body, name=name, grid=(n_tiles,),
        in_specs=[rowu, pu, nu, rowg, pg, ng, pl.BlockSpec((3, F2), lambda i: (0, 0)),
                  pl.BlockSpec((1, F2), lambda i: (0, 0))],
        out_specs=[rowu, pl.BlockSpec((3, F2), lambda i: (0, 0)), pl.BlockSpec((1, F2), lambda i: (0, 0))],
        out_shape=[SDS((R, F2), BF16), SDS((3, F2), F32), SDS((1, F2), F32)],
        compiler_params=_params(1),
    )(u, u, u, dgact, dgact, dgact, conv_w, conv_b)


def _pool_counts(i, n_lat, s_len, l_len, n_rows, offset):
    ctx = i >= n_lat
    t0 = jnp.where(ctx, i - n_lat, i) * ROW_TILE + offset
    seg = jnp.where(ctx, l_len, s_len)
    t = t0 + lax.broadcasted_iota(jnp.int32, (n_rows, 1), 0)
    out = []
    for win in POOL_WINDOWS:
        cnt = jnp.minimum(t + win // 2, seg) - jnp.maximum(t - win // 2, 0)
        out.append(jnp.maximum(cnt, 1).astype(F32))
    return out


def _window_sum(e, lo, hi, n):
    acc = None
    for j in range(lo, hi + 1):
        term = e if j == 0 else pltpu.roll(e, (-j) % n, 0)
        acc = term if acc is None else acc + term
    return acc


def _pool_fwd(h, w, b, scale, n_lat, s_len, l_len, name):
    R, D = h.shape
    G = D // 4
    n_tiles = R // ROW_TILE
    T, N = ROW_TILE, ROW_TILE + 2 * HALO
    row = pl.BlockSpec((T, D), lambda i: (i, 0))
    prev, nxt = _halo_specs(n_tiles, D)
    vec = pl.BlockSpec((1, D), lambda i: (0, 0))

    def body(h_ref, p_ref, n_ref, w_ref, b_ref, s_ref, y_ref):
        i = pl.program_id(0)
        first, last = _edge_flags(i, n_lat, n_tiles)
        cnts = _pool_counts(i, n_lat, s_len, l_len, T, 0)
        for g, win in enumerate(POOL_WINDOWS):
            cols = pl.ds(g * G, G)
            pr = jnp.where(first, 0.0, p_ref[:, cols])
            nx = jnp.where(last, 0.0, n_ref[:, cols])
            hv = h_ref[:, cols]
            e = jnp.concatenate([pr, hv, nx], axis=0)
            mean = _window_sum(e, -(win // 2), win // 2 - 1, N)[HALO:HALO + T] / cnts[g]
            yg = _dot(_bf(mean - hv), w_ref[g])
            y_ref[:, cols] = (yg + b_ref[:, cols]) * s_ref[:, cols]

    return pl.pallas_call(
        body, name=name, grid=(n_tiles,),
        in_specs=[row, prev, nxt, pl.BlockSpec((4, G, G), lambda i: (0, 0, 0)), vec, vec],
        out_specs=row, out_shape=SDS((R, D), F32), compiler_params=_params(1),
    )(h, h, h, w, b, scale)


def _pool_bwd(h, dy, w, b, scale, n_lat, s_len, l_len, name):
    R, D = h.shape
    G = D // 4
    n_tiles = R // ROW_TILE
    T, N = ROW_TILE, ROW_TILE + 2 * HALO
    row = pl.BlockSpec((T, D), lambda i: (i, 0))
    prev, nxt = _halo_specs(n_tiles, D)
    vec = pl.BlockSpec((1, D), lambda i: (0, 0))
    wspec = pl.BlockSpec((4, G, G), lambda i: (0, 0, 0))

    def body(h_ref, ph_ref, nh_ref, d_ref, pd_ref, nd_ref, w_ref, b_ref, s_ref,
             dh_ref, dw_ref, db_ref, ds_ref):
        i = pl.program_id(0)
        first, last = _edge_flags(i, n_lat, n_tiles)

        @pl.when(i == 0)
        def _():
            dw_ref[...] = jnp.zeros_like(dw_ref)
            db_ref[...] = jnp.zeros_like(db_ref)
            ds_ref[...] = jnp.zeros_like(ds_ref)

        cnts = _pool_counts(i, n_lat, s_len, l_len, T, 0)
        cnts_ext = _pool_counts(i, n_lat, s_len, l_len, N, -HALO)
        for g, win in enumerate(POOL_WINDOWS):
            cols = pl.ds(g * G, G)

            def ext(t_ref, p_ref, n_ref):
                pr = jnp.where(first, 0.0, p_ref[:, cols])
                nx = jnp.where(last, 0.0, n_ref[:, cols])
                return jnp.concatenate([pr, t_ref[:, cols], nx], axis=0)

            hv = h_ref[:, cols]
            mean = _window_sum(ext(h_ref, ph_ref, nh_ref), -(win // 2), win // 2 - 1, N)[HALO:HALO + T] / cnts[g]
            z = _bf(mean - hv)
            sc = s_ref[:, cols]
            dye = ext(d_ref, pd_ref, nd_ref)
            dt = _bf(dye * sc)
            dz = _dot_nt(dt, w_ref[g])
            dm = dz / cnts_ext[g]
            dh = _window_sum(dm, -(win // 2 - 1), win // 2, N) - dz
            dh_ref[:, cols] = dh[HALO:HALO + T]
            dyt = dye[HALO:HALO + T]
            dw_ref[g] += _dot_tn(z, dt[HALO:HALO + T])
            db_ref[:, cols] += jnp.sum(dyt * sc, axis=0, keepdims=True)
            ds_ref[:, cols] += jnp.sum(dyt * (_dot(z, w_ref[g]) + b_ref[:, cols]), axis=0, keepdims=True)

    return pl.pallas_call(
        body, name=name, grid=(n_tiles,),
        in_specs=[row, prev, nxt, row, prev, nxt, wspec, vec, vec],
        out_specs=[row, wspec, vec, vec],
        out_shape=[SDS((R, D), F32), SDS((4, G, G), F32), SDS((1, D), F32), SDS((1, D), F32)],
        compiler_params=_params(1),
    )(h, h, h, dy, dy, dy, w, b, scale)


def _rope_tables(s_len, l_len):
    t = jnp.arange(s_len)
    row = (t // GRID_W).astype(F32)
    col = (t % GRID_W).astype(F32)
    axis_dim = HEAD_DIM // 2
    inv = ROPE_THETA ** (-jnp.arange(0, axis_dim, 2, dtype=F32) / axis_dim)
    ar, ac = row[:, None] * inv, col[:, None] * inv
    cos = jnp.concatenate([jnp.cos(ar), jnp.cos(ar), jnp.cos(ac), jnp.cos(ac)], axis=-1)
    sin = jnp.concatenate([-jnp.sin(ar), jnp.sin(ar), -jnp.sin(ac), jnp.sin(ac)], axis=-1)
    cos = jnp.concatenate([cos, jnp.ones((l_len, HEAD_DIM), F32)], axis=0)
    sin = jnp.concatenate([sin, jnp.zeros((l_len, HEAD_DIM), F32)], axis=0)
    return cos, sin


def _swap_halves(v):
    lane = lax.broadcasted_iota(jnp.int32, v.shape, 1)
    return jnp.where((lane % 64) < 32, pltpu.roll(v, 96, 1), pltpu.roll(v, 32, 1))


def _qk_prep_fwd(qkv, q_gain, k_gain, cos, sin):
    R = qkv.shape[0]
    NQ, NK = N_HEADS * HEAD_DIM, N_KV * HEAD_DIM
    T = ROW_TILE
    vec = pl.BlockSpec((1, HEAD_DIM), lambda i: (0, 0))
    tab = pl.BlockSpec((T, HEAD_DIM), lambda i: (i, 0))

    def body(x_ref, qg_ref, kg_ref, c_ref, s_ref, q_ref, k_ref, v_ref):
        cosv, sinv = c_ref[...], s_ref[...]

        def prep(c0, gain):
            xh = x_ref[:, pl.ds(c0, HEAD_DIM)]
            xn = xh * lax.rsqrt(jnp.mean(xh * xh, axis=-1, keepdims=True) + EPS) * gain
            return _bf(xn * cosv + _swap_halves(xn) * sinv)

        for hd in range(N_HEADS):
            q_ref[:, pl.ds(hd * HEAD_DIM, HEAD_DIM)] = prep(hd * HEAD_DIM, qg_ref[...])
        for hd in range(N_KV):
            k_ref[:, pl.ds(hd * HEAD_DIM, HEAD_DIM)] = prep(NQ + hd * HEAD_DIM, kg_ref[...])
            v_ref[:, pl.ds(2 * hd * HEAD_DIM, HEAD_DIM)] = _bf(x_ref[:, pl.ds(NQ + NK + hd * HEAD_DIM, HEAD_DIM)])
            v_ref[:, pl.ds((2 * hd + 1) * HEAD_DIM, HEAD_DIM)] = jnp.ones((T, HEAD_DIM), BF16)

    return pl.pallas_call(
        body, name="qk_prep_fwd", grid=(R // T,),
        in_specs=[pl.BlockSpec((T, NQ + 2 * NK), lambda i: (i, 0)), vec, vec, tab, tab],
        out_specs=[pl.BlockSpec((T, NQ), lambda i: (i, 0)), pl.BlockSpec((T, NK), lambda i: (i, 0)),
                   pl.BlockSpec((T, 2 * NK), lambda i: (i, 0))],
        out_shape=[SDS((R, NQ), BF16), SDS((R, NK), BF16), SDS((R, 2 * NK), BF16)],
        compiler_params=_params(1),
    )(qkv, q_gain, k_gain, cos, sin)


def _qk_prep_bwd(qkv, dq, dk, dv, q_gain, k_gain, cos, sin):
    R = qkv.shape[0]
    NQ, NK = N_HEADS * HEAD_DIM, N_KV * HEAD_DIM
    T = ROW_TILE
    vec = pl.BlockSpec((1, HEAD_DIM), lambda i: (0, 0))
    tab = pl.BlockSpec((T, HEAD_DIM), lambda i: (i, 0))

    def body(x_ref, dq_ref, dk_ref, dv_ref, qg_ref, kg_ref, c_ref, s_ref, o_ref, dqg_ref, dkg_ref):
        i = pl.program_id(0)
        cosv, sinv = c_ref[...], s_ref[...]

        @pl.when(i == 0)
        def _():
            dqg_ref[...] = jnp.zeros_like(dqg_ref)
            dkg_ref[...] = jnp.zeros_like(dkg_ref)

        def back(c0, dout, gain, dg_ref):
            xh = x_ref[:, pl.ds(c0, HEAD_DIM)]
            r = lax.rsqrt(jnp.mean(xh * xh, axis=-1, keepdims=True) + EPS)
            xhat = xh * r
            dxn = dout * cosv + _swap_halves(dout * sinv)
            dg_ref[...] += jnp.sum(dxn * xhat, axis=0, keepdims=True)
            a = dxn * gain
            o_ref[:, pl.ds(c0, HEAD_DIM)] = _bf(r * (a - xhat * jnp.mean(a * xhat, axis=-1, keepdims=True)))

        for hd in range(N_HEADS):
            back(hd * HEAD_DIM, dq_ref[:, pl.ds(hd * HEAD_DIM, HEAD_DIM)], qg_ref[...], dqg_ref)
        for hd in range(N_KV):
            back(NQ + hd * HEAD_DIM, dk_ref[:, pl.ds(hd * HEAD_DIM, HEAD_DIM)], kg_ref[...], dkg_ref)
        o_ref[:, pl.ds(NQ + NK, NK)] = _bf(dv_ref[...])

    return pl.pallas_call(
        body, name="qk_prep_bwd", grid=(R // T,),
        in_specs=[pl.BlockSpec((T, NQ + 2 * NK), lambda i: (i, 0)), pl.BlockSpec((T, NQ), lambda i: (i, 0)),
                  pl.BlockSpec((T, NK), lambda i: (i, 0)), pl.BlockSpec((T, NK), lambda i: (i, 0)),
                  vec, vec, tab, tab],
        out_specs=[pl.BlockSpec((T, NQ + 2 * NK), lambda i: (i, 0)), vec, vec],
        out_shape=[SDS((R, NQ + 2 * NK), BF16), SDS((1, HEAD_DIM), F32), SDS((1, HEAD_DIM), F32)],
        compiler_params=_params(1),
    )(qkv, dq, dk, dv, q_gain, k_gain, cos, sin)


def _flash_fwd(q, k, v, s_len, l_len):
    R = q.shape[0]
    T = FLASH_FWD_TILE
    n_lat = s_len // T
    ck = _pick(s_len, 512, 128)
    scale = HEAD_DIM ** -0.5
    group = N_HEADS // N_KV
    GW = group * HEAD_DIM
    M = group * T
    chunks = s_len // ck
    to_log2 = scale * math.log2(math.e)

    def body(q_ref, k_ref, v_ref, o_ref, lse_ref, s_s, sc_s, ml_s, mb_s, acc_s):
        i = pl.program_id(1)
        qv = jnp.concatenate([q_ref[:, pl.ds(hh * HEAD_DIM, HEAD_DIM)] for hh in range(group)], axis=0)

        ml_s[...] = jnp.full_like(ml_s, -jnp.inf)

        def lane_max(s, n):
            m = ml_s[...]
            for t in range(n // HEAD_DIM):
                m = jnp.maximum(m, s[:, t * HEAD_DIM:(t + 1) * HEAD_DIM])
            ml_s[...] = m

        @pl.when(i < n_lat)
        def _():
            def loop(c, carry):
                s = _dot_nt(qv, k_ref[pl.ds(pl.multiple_of(c * ck, ck), ck), :])
                s_s[c] = s
                lane_max(s, ck)
                return carry
            lax.fori_loop(0, chunks, loop, 0, unroll=4 if chunks % 4 == 0 else 1)

        sc = _dot_nt(qv, k_ref[pl.ds(s_len, l_len), :])
        sc_s[...] = sc
        lane_max(sc, l_len)
        m_row = jnp.max(ml_s[...], axis=-1, keepdims=True) * to_log2
        mb_s[...] = jnp.broadcast_to(m_row, (M, ck))

        acc_s[...] = jnp.zeros_like(acc_s)

        @pl.when(i < n_lat)
        def _():
            def loop(c, carry):
                p = jnp.exp2(s_s[c] * to_log2 - mb_s[...])
                acc_s[...] += _dot(_bf(p), v_ref[pl.ds(pl.multiple_of(c * ck, ck), ck), :])
                return carry
            lax.fori_loop(0, chunks, loop, 0, unroll=4 if chunks % 4 == 0 else 1)

        p = jnp.exp2(sc_s[...] * to_log2 - mb_s[:, pl.ds(0, l_len)])
        acc_s[...] += _dot(_bf(p), v_ref[pl.ds(s_len, l_len), :])
        l_rep = acc_s[:, pl.ds(HEAD_DIM, HEAD_DIM)]
        o = acc_s[:, pl.ds(0, HEAD_DIM)] / l_rep
        for hh in range(group):
            o_ref[:, pl.ds(hh * HEAD_DIM, HEAD_DIM)] = o[hh * T:(hh + 1) * T]
        lse = (mb_s[:, pl.ds(0, HEAD_DIM)] + jnp.log2(l_rep)) * math.log(2.0)
        lse_ref[...] = jnp.max(lse, axis=-1, keepdims=True).reshape(group, T, 1)

    return pl.pallas_call(
        body, name="flash_fwd", grid=(N_KV, R // T),
        in_specs=[pl.BlockSpec((T, GW), lambda g, i: (i, g)),
                  pl.BlockSpec((R, HEAD_DIM), lambda g, i: (0, g)),
                  pl.BlockSpec((R, 2 * HEAD_DIM), lambda g, i: (0, g))],
        out_specs=[pl.BlockSpec((T, GW), lambda g, i: (i, g)),
                   pl.BlockSpec((group, T, 1), lambda g, i: (g, i, 0))],
        out_shape=[SDS((R, N_HEADS * HEAD_DIM), F32), SDS((N_HEADS, R, 1), F32)],
        scratch_shapes=[pltpu.VMEM((chunks, M, ck), F32), pltpu.VMEM((M, l_len), F32), pltpu.VMEM((M, HEAD_DIM), F32),
                        pltpu.VMEM((M, ck), F32), pltpu.VMEM((M, 2 * HEAD_DIM), F32)],
        compiler_params=_params(2),
    )(q, k, v)


def _flash_bwd(q, k, v, o, lse, do, s_len, l_len):
    R = q.shape[0]
    T = ROW_TILE
    n_lat = s_len // T
    ck = _pick(s_len, 512, 128)
    scale = HEAD_DIM ** -0.5
    group = N_HEADS // N_KV
    GW = group * HEAD_DIM
    qspec = pl.BlockSpec((T, GW), lambda g, i: (i, g))
    kspec = pl.BlockSpec((R, HEAD_DIM), lambda g, i: (0, g))

    M = group * T
    log2e = math.log2(math.e)

    def body(q_ref, do_ref, o_ref, lse_ref, k_ref, v_ref, dq_ref, dk_ref, dv_ref, dq_s, lse_s, delta_s):
        i = pl.program_id(1)

        @pl.when(i == 0)
        def _():
            dk_ref[...] = jnp.zeros_like(dk_ref)
            dv_ref[...] = jnp.zeros_like(dv_ref)

        def stacked(ref):
            return jnp.concatenate([ref[:, pl.ds(hh * HEAD_DIM, HEAD_DIM)] for hh in range(group)], axis=0)

        qv = stacked(q_ref)
        dov = stacked(do_ref)
        dob = _bf(dov)
        delta_s[...] = jnp.broadcast_to(jnp.sum(dov * stacked(o_ref), axis=-1, keepdims=True), (M, ck))
        lse_s[...] = jnp.broadcast_to(lse_ref[...].reshape(M, 1) * log2e, (M, ck))
        dq_s[...] = jnp.zeros_like(dq_s)

        def step(rows, n):
            kv, vv = k_ref[rows, :], v_ref[rows, :]
            p = jnp.exp2(_dot_nt(qv, kv) * (scale * log2e) - lse_s[:, pl.ds(0, n)])
            dv_ref[rows, :] += _dot_tn(_bf(p), dob)
            ds = _bf(p * (_dot_nt(dob, vv) - delta_s[:, pl.ds(0, n)]) * scale)
            dq_s[...] += _dot(ds, kv)
            dk_ref[rows, :] += _dot_tn(ds, qv)

        @pl.when(i < n_lat)
        def _():
            def loop(c, carry):
                step(pl.ds(pl.multiple_of(c * ck, ck), ck), ck)
                return carry
            lax.fori_loop(0, s_len // ck, loop, 0, unroll=4 if (s_len // ck) % 4 == 0 else 1)

        step(pl.ds(s_len, l_len), l_len)
        for hh in range(group):
            dq_ref[:, pl.ds(hh * HEAD_DIM, HEAD_DIM)] = dq_s[pl.ds(hh * T, T), :]

    return pl.pallas_call(
        body, name="flash_bwd", grid=(N_KV, R // T),
        in_specs=[qspec, qspec, qspec, pl.BlockSpec((group, T, 1), lambda g, i: (g, i, 0)), kspec,
                  pl.BlockSpec((R, HEAD_DIM), lambda g, i: (0, 2 * g))],
        out_specs=[qspec, kspec, kspec],
        out_shape=[SDS((R, N_HEADS * HEAD_DIM), F32), SDS((R, N_KV * HEAD_DIM), F32),
                   SDS((R, N_KV * HEAD_DIM), F32)],
        scratch_shapes=[pltpu.VMEM((M, HEAD_DIM), F32), pltpu.VMEM((M, ck), F32), pltpu.VMEM((M, ck), F32)],
        compiler_params=_params(2),
    )(q, do, o, lse, k, v)


K_SCALE = RET_DK ** -0.5


def _log_sigmoid(v):
    return -(jnp.maximum(-v, 0.0) + jnp.log(1.0 + jnp.exp(-jnp.abs(v))))


def _ret_decays(d, lg):
    C = RET_CHUNK
    ic = lax.broadcasted_iota(jnp.int32, (C, 1), 0)
    ir = lax.broadcasted_iota(jnp.int32, (1, C), 1)
    li = jnp.where(d == 0, ic, C - 1 - ic).astype(F32)
    lj = jnp.where(d == 0, ir, C - 1 - ir).astype(F32)
    diff = li - lj
    mask = jnp.where(diff >= 0, jnp.exp(jnp.maximum(diff, 0.0) * lg), 0.0)
    qd = jnp.exp((li + 1.0) * lg)
    kd = jnp.exp((C - 1.0 - li) * lg)
    cd = jnp.exp(C * lg)
    return li, diff, mask, qd, kd, cd


def _ctx_weights(d, t, lg, l_len):
    C = RET_CHUNK
    j = (t * C + lax.broadcasted_iota(jnp.int32, (C, 1), 0)).astype(F32)
    e = jnp.where(d == 0, (l_len - 1.0) - j, j)
    return e, jnp.exp(e * lg)


def _mirrored(x, i, n_lat):
    return jnp.where(i < n_lat, jnp.concatenate([x[RET_CHUNK:], x[:RET_CHUNK]], axis=0), x)


def _mirror_tile(n_lat):
    return lambda i: jnp.where(i < n_lat, n_lat - 1 - i, i)


def _ret_fwd(proj, lgt, s_len, l_len):
    R = proj.shape[0]
    C, H, DK, DV = RET_CHUNK, RET_HEADS, RET_DK, RET_DV
    nl, nc = s_len // C, l_len // C

    def stored(t):
        return jnp.where(t < nc, nl + t, jnp.maximum(t - nc, 0))

    def actual(d, t):
        n = jnp.maximum(t - nc, 0)
        return jnp.where(t < nc, nl + t, n if d == 0 else nl - 1 - n)

    def body(q0_ref, k0_ref, v0_ref, q1_ref, k1_ref, v1_ref, lg_ref, o_ref, st_ref, r_s):
        t = pl.program_id(0)
        qkv = ((q0_ref, k0_ref, v0_ref), (q1_ref, k1_ref, v1_ref))

        @pl.when(t == 0)
        def _():
            r_s[...] = jnp.zeros_like(r_s)

        def log_gamma(d, hh):
            return jnp.max(_log_sigmoid(lg_ref[d, hh]), axis=-1, keepdims=True)

        @pl.when(t < nc)
        def _():
            for d, (q_ref, k_ref, v_ref) in enumerate(qkv):
                for hh in range(H):
                    qc, vc = pl.ds(hh * DK, DK), pl.ds(hh * DV, DV)
                    _, w = _ctx_weights(d, t, log_gamma(d, hh), l_len)
                    r_s[d, hh] += _dot_tn(_bf(k_ref[:, qc] * K_SCALE * w), _bf(v_ref[:, vc]))
                    o_ref[d, :, vc] = jnp.zeros((C, DV), F32)

        @pl.when(t >= nc)
        def _():
            for d, (q_ref, k_ref, v_ref) in enumerate(qkv):
                for hh in range(H):
                    qc, vc = pl.ds(hh * DK, DK), pl.ds(hh * DV, DV)
                    _, _, mask, qd, kd, cd = _ret_decays(d, log_gamma(d, hh))
                    qb, kv, vb = _bf(q_ref[:, qc]), k_ref[:, qc] * K_SCALE, _bf(v_ref[:, vc])
                    r = r_s[d, hh]
                    st_ref[d, hh, 0] = r
                    att = _dot_nt(qb, _bf(kv)) * mask
                    o_ref[d, :, vc] = _dot(_bf(att), vb) + _dot(qb, _bf(r)) * qd
                    r_s[d, hh] = r * cd + _dot_tn(_bf(kv * kd), vb)

    def rows(d):
        return [pl.BlockSpec((C, H * DK), lambda t: (actual(d, t), 0)),
                pl.BlockSpec((C, H * DK), lambda t: (actual(d, t), 1)),
                pl.BlockSpec((C, H * DV), lambda t: (actual(d, t), 1))]

    return pl.pallas_call(
        body, name="ret_fwd", grid=(nc + nl,),
        in_specs=rows(0) + rows(1) + [pl.BlockSpec((2, H, 1, 128), lambda t: (0, 0, 0, 0))],
        out_specs=[pl.BlockSpec((2, C, H * DV), lambda t: (0, stored(t), 0)),
                   pl.BlockSpec((2, H, 1, DK, DV), lambda t: (0, 0, jnp.maximum(t - nc, 0), 0, 0))],
        out_shape=[SDS((2, R, H * DV), F32), SDS((2, H, nl, DK, DV), F32)],
        scratch_shapes=[pltpu.VMEM((2, H, DK, DV), F32)],
        compiler_params=_params(1),
    )(proj, proj, proj, proj, proj, proj, lgt)


def _ret_bwd(proj, lgt, states, do, s_len, l_len):
    R = proj.shape[0]
    C, H, DK, DV = RET_CHUNK, RET_HEADS, RET_DK, RET_DV
    nl, nc = s_len // C, l_len // C
    last = nl + nc - 1

    def stored(t):
        return jnp.where(t < nl, jnp.maximum(nl - 1 - t, 0), t)

    def actual(d, t):
        return stored(t) if d == 0 else t

    def body(q0_ref, k0_ref, v0_ref, do0_ref, q1_ref, k1_ref, v1_ref, do1_ref, lg_ref, st_ref,
             dq_ref, dk_ref, dv_ref, dlg_ref, dr_s, dl_s):
        t = pl.program_id(0)
        ins = ((q0_ref, k0_ref, v0_ref, do0_ref), (q1_ref, k1_ref, v1_ref, do1_ref))

        def log_gamma(d, hh):
            return jnp.max(_log_sigmoid(lg_ref[d, hh]), axis=-1, keepdims=True)

        @pl.when(t == 0)
        def _():
            dr_s[...] = jnp.zeros_like(dr_s)
            dl_s[...] = jnp.zeros_like(dl_s)

        @pl.when(t < nl)
        def _():
            for d, (q_ref, k_ref, v_ref, do_ref) in enumerate(ins):
                for hh in range(H):
                    qc, vc = pl.ds(hh * DK, DK), pl.ds(hh * DV, DV)
                    li, diff, mask, qd, kd, cd = _ret_decays(d, log_gamma(d, hh))
                    qv, kv, vv, dov = q_ref[:, qc], k_ref[:, qc] * K_SCALE, v_ref[:, vc], do_ref[:, vc]
                    qb, kb, vb, dob = _bf(qv), _bf(kv), _bf(vv), _bf(dov)
                    r, drn = st_ref[d, hh, 0], dr_s[d, hh]
                    rb, drb = _bf(r), _bf(drn)
                    p = _dot_nt(qb, kb)
                    dp = _dot_nt(dob, vb) * mask
                    dpb = _bf(dp)
                    doq = _bf(dov * qd)
                    dq_inter = _dot_nt(doq, rb)
                    dk_state = kd * _dot_nt(vb, drb)
                    dq_ref[d, :, qc] = _dot(dpb, kb) + dq_inter
                    dk_ref[d, :, qc] = (_dot_tn(dpb, qb) + dk_state) * K_SCALE
                    dv_ref[d, :, vc] = _dot_tn(_bf(p * mask), dob) + _dot(_bf(kv * kd), drb)
                    dr_s[d, hh] = cd * drn + _dot_tn(qb, doq)
                    dl_s[d, hh] += (jnp.sum(dp * p * diff) + jnp.sum((li + 1.0) * qv * dq_inter)
                                    + jnp.sum((C - 1.0 - li) * kv * dk_state) + C * jnp.sum(cd * r * drn))

        @pl.when(t >= nl)
        def _():
            for d, (q_ref, k_ref, v_ref, do_ref) in enumerate(ins):
                for hh in range(H):
                    qc, vc = pl.ds(hh * DK, DK), pl.ds(hh * DV, DV)
                    e, w = _ctx_weights(d, t - nl, log_gamma(d, hh), l_len)
                    kv, vb, drb = k_ref[:, qc] * K_SCALE, _bf(v_ref[:, vc]), _bf(dr_s[d, hh])
                    dkc = w * _dot_nt(vb, drb)
                    dq_ref[d, :, qc] = jnp.zeros((C, DK), F32)
                    dk_ref[d, :, qc] = dkc * K_SCALE
                    dv_ref[d, :, vc] = _dot(_bf(kv * w), drb)
                    dl_s[d, hh] += jnp.sum(e * kv * dkc)

        @pl.when(t == last)
        def _():
            for d in range(2):
                for hh in range(H):
                    dlg_ref[d, hh] = dl_s[d, hh] * (1.0 / (1.0 + jnp.exp(lg_ref[d, hh])))

    def rows(d):
        return [pl.BlockSpec((C, H * DK), lambda t: (actual(d, t), 0)),
                pl.BlockSpec((C, H * DK), lambda t: (actual(d, t), 1)),
                pl.BlockSpec((C, H * DV), lambda t: (actual(d, t), 1)),
                pl.BlockSpec((C, H * DV), lambda t: (actual(d, t), 0))]

    return pl.pallas_call(
        body, name="ret_bwd", grid=(nl + nc,),
        in_specs=rows(0) + rows(1) + [
            pl.BlockSpec((2, H, 1, 128), lambda t: (0, 0, 0, 0)),
            pl.BlockSpec((2, H, 1, DK, DV), lambda t: (0, 0, jnp.maximum(nl - 1 - t, 0), 0, 0))],
        out_specs=[pl.BlockSpec((2, C, H * DK), lambda t: (0, stored(t), 0)),
                   pl.BlockSpec((2, C, H * DK), lambda t: (0, stored(t), 0)),
                   pl.BlockSpec((2, C, H * DV), lambda t: (0, stored(t), 0)),
                   pl.BlockSpec((2, H, 1, 128), lambda t: (0, 0, 0, 0))],
        out_shape=[SDS((2, R, H * DK), F32), SDS((2, R, H * DK), F32), SDS((2, R, H * DV), F32),
                   SDS((2, H, 1, 128), F32)],
        scratch_shapes=[pltpu.VMEM((2, H, DK, DV), F32), pltpu.VMEM((2, H, 1, 128), F32)],
        compiler_params=_params(1),
    )(proj, proj, proj, do, proj, proj, proj, do, lgt, states)


def _readout_fwd(o2, proj, gn_w, n_lat):
    R = proj.shape[0]
    H, DV = RET_HEADS, RET_DV
    W = H * DV
    T = ROW_TILE
    assert T == 2 * RET_CHUNK

    def body(o_ref, ob_ref, g_ref, w_ref, out_ref):
        i = pl.program_id(0)
        for hh in range(H):
            cols = pl.ds(hh * DV, DV)
            y = o_ref[0, :, cols] + _mirrored(ob_ref[0, :, cols], i, n_lat)
            yc = y - jnp.mean(y, axis=-1, keepdims=True)
            yn = yc * lax.rsqrt(jnp.mean(yc * yc, axis=-1, keepdims=True) + EPS) * w_ref[:, cols]
            g = g_ref[:, cols]
            out_ref[:, cols] = _bf(g * _sigmoid(g) * yn)

    return pl.pallas_call(
        body, name="readout_fwd", grid=(R // T,),
        in_specs=[pl.BlockSpec((1, T, W), lambda i: (0, i, 0)),
                  pl.BlockSpec((1, T, W), lambda i: (1, _mirror_tile(n_lat)(i), 0)),
                  pl.BlockSpec((T, W), lambda i: (i, 2)), pl.BlockSpec((1, W), lambda i: (0, 0))],
        out_specs=pl.BlockSpec((T, W), lambda i: (i, 0)),
        out_shape=SDS((R, W), BF16), compiler_params=_params(1),
    )(o2, o2, proj, gn_w)


def _readout_bwd(o2, proj, gn_w, dgated, n_lat):
    R = proj.shape[0]
    H, DV = RET_HEADS, RET_DV
    W = H * DV
    T = ROW_TILE

    def body(o_ref, ob_ref, g_ref, w_ref, d_ref, do_ref, dg_ref, dw_ref):
        i = pl.program_id(0)

        @pl.when(i == 0)
        def _():
            dw_ref[...] = jnp.zeros_like(dw_ref)

        for hh in range(H):
            cols = pl.ds(hh * DV, DV)
            y = o_ref[0, :, cols] + _mirrored(ob_ref[0, :, cols], i, n_lat)
            yc = y - jnp.mean(y, axis=-1, keepdims=True)
            rstd = lax.rsqrt(jnp.mean(yc * yc, axis=-1, keepdims=True) + EPS)
            yn0 = yc * rstd
            wv = w_ref[:, cols]
            g = g_ref[:, cols]
            s = _sigmoid(g)
            dgt = d_ref[:, cols]
            dyn = dgt * (g * s)
            dg_ref[:, cols] = _bf(dgt * (yn0 * wv) * (s * (1.0 + g * (1.0 - s))))
            dw_ref[:, cols] += jnp.sum(dyn * yn0, axis=0, keepdims=True)
            a = dyn * wv
            do_ref[:, cols] = rstd * (a - jnp.mean(a, axis=-1, keepdims=True)
                                      - yn0 * jnp.mean(a * yn0, axis=-1, keepdims=True))

    return pl.pallas_call(
        body, name="readout_bwd", grid=(R // T,),
        in_specs=[pl.BlockSpec((1, T, W), lambda i: (0, i, 0)),
                  pl.BlockSpec((1, T, W), lambda i: (1, _mirror_tile(n_lat)(i), 0)),
                  pl.BlockSpec((T, W), lambda i: (i, 2)),
                  pl.BlockSpec((1, W), lambda i: (0, 0)), pl.BlockSpec((T, W), lambda i: (i, 0))],
        out_specs=[pl.BlockSpec((T, W), lambda i: (i, 0)), pl.BlockSpec((T, W), lambda i: (i, 0)),
                   pl.BlockSpec((1, W), lambda i: (0, 0))],
        out_shape=[SDS((R, W), F32), SDS((R, W), BF16), SDS((1, W), F32)],
        compiler_params=_params(1),
    )(o2, o2, proj, gn_w, dgated)


def _ret_dproj(dq2, dk2, dv2, dg, n_lat):
    R = dg.shape[0]
    NQ, NV = RET_HEADS * RET_DK, RET_HEADS * RET_DV
    T = ROW_TILE

    def body(dq_ref, dqb_ref, dk_ref, dkb_ref, dv_ref, dvb_ref, dg_ref, o_ref):
        i = pl.program_id(0)
        o_ref[:, pl.ds(0, NQ)] = _bf(dq_ref[0] + _mirrored(dqb_ref[0], i, n_lat))
        o_ref[:, pl.ds(NQ, NQ)] = _bf(dk_ref[0] + _mirrored(dkb_ref[0], i, n_lat))
        o_ref[:, pl.ds(2 * NQ, NV)] = _bf(dv_ref[0] + _mirrored(dvb_ref[0], i, n_lat))
        o_ref[:, pl.ds(2 * NQ + NV, NV)] = dg_ref[...]

    def both(width):
        return [pl.BlockSpec((1, T, width), lambda i: (0, i, 0)),
                pl.BlockSpec((1, T, width), lambda i: (1, _mirror_tile(n_lat)(i), 0))]

    return pl.pallas_call(
        body, name="ret_dproj", grid=(R // T,),
        in_specs=both(NQ) + both(NQ) + both(NV) + [pl.BlockSpec((T, NV), lambda i: (i, 0))],
        out_specs=pl.BlockSpec((T, 2 * NQ + 2 * NV), lambda i: (i, 0)),
        out_shape=SDS((R, 2 * NQ + 2 * NV), BF16), compiler_params=_params(1),
    )(dq2, dq2, dk2, dk2, dv2, dv2, dg)


def _silu(v):
    return v * _sigmoid(v)


def _ada_fwd(c_rows, ada_w, ada_b_shard):
    depth, D, cols = ada_w.shape

    def body(c_ref, w_ref, b_ref, o_ref):
        o_ref[0] = _dot(_bf(_silu(c_ref[...])), _bf(w_ref[0])) + b_ref[0]

    return pl.pallas_call(
        body, name="ada_fwd", grid=(depth,),
        in_specs=[pl.BlockSpec((16, D), lambda i: (0, 0)), pl.BlockSpec((1, D, cols), lambda i: (i, 0, 0)),
                  pl.BlockSpec((1, 1, cols), lambda i: (i, 0, 0))],
        out_specs=pl.BlockSpec((1, 16, cols), lambda i: (i, 0, 0)),
        out_shape=SDS((depth, 16, cols), F32), compiler_params=_params(1),
    )(c_rows, ada_w, ada_b_shard)


def _ada_bwd(c_rows, ada_w, d_lat, d_ctx):
    depth, D, cols = ada_w.shape

    def body(c_ref, w_ref, dl_ref, dc_ref, dw_ref, pc_ref):
        i = pl.program_id(0)
        cv = c_ref[...]
        a = _silu(cv)
        dcs = jnp.broadcast_to(jnp.sum(dc_ref[0], axis=0, keepdims=True), (8, cols))
        dw_ref[0] = _dot_tn(_bf(a[0:8]), _bf(dl_ref[0])) + _dot_tn(_bf(a[8:16]), _bf(dcs))

        @pl.when(i == 0)
        def _():
            pc_ref[...] = jnp.zeros_like(pc_ref)

        pc_ref[...] += _dot_nt(_bf(dcs), _bf(w_ref[0]))

        @pl.when(i == depth - 1)
        def _():
            cc = c_ref[pl.ds(8, 1), :]
            s = _sigmoid(cc)
            pc_ref[...] = pc_ref[...] * (s * (1.0 + cc * (1.0 - s)))

    return pl.pallas_call(
        body, name="ada_bwd", grid=(depth,),
        in_specs=[pl.BlockSpec((16, D), lambda i: (0, 0)), pl.BlockSpec((1, D, cols), lambda i: (i, 0, 0)),
                  pl.BlockSpec((1, 8, cols), lambda i: (i, 0, 0)), pl.BlockSpec((1, 8, cols), lambda i: (i, 0, 0))],
        out_specs=[pl.BlockSpec((1, D, cols), lambda i: (i, 0, 0)), pl.BlockSpec((8, D), lambda i: (0, 0))],
        out_shape=[SDS((depth, D, cols), F32), SDS((8, D), F32)], compiler_params=_params(1),
    )(c_rows, ada_w, d_lat, d_ctx)


def _adamw(w, g, m, v, name):
    shape = w.shape
    n = g.shape[0]
    cols = shape[-1]
    rows = w.size // cols
    tr = _pick(rows, 512, 8) if rows * cols * 4 > (1 << 20) else rows
    spec = pl.BlockSpec((tr, cols), lambda i: (i, 0))

    def body(w_ref, g_ref, m_ref, v_ref, go_ref, d_ref, mo_ref, vo_ref):
        gs = g_ref[0].astype(F32)
        for k in range(1, n):
            gs = gs + g_ref[k].astype(F32)
        mn = ADAM_B1 * m_ref[...] + (1.0 - ADAM_B1) * gs
        vn = ADAM_B2 * v_ref[...] + (1.0 - ADAM_B2) * jnp.square(gs)
        m_hat = mn / (1.0 - ADAM_B1 ** ADAM_STEP)
        v_hat = vn / (1.0 - ADAM_B2 ** ADAM_STEP)
        go_ref[...] = gs
        d_ref[...] = -ADAM_LR * (m_hat / (jnp.sqrt(v_hat) + ADAM_EPS) + ADAM_WD * w_ref[...])
        mo_ref[...] = mn
        vo_ref[...] = vn

    outs = pl.pallas_call(
        body, name=name, grid=(rows // tr,),
        in_specs=[spec, pl.BlockSpec((n, tr, cols), lambda i: (0, i, 0)), spec, spec],
        out_specs=[spec] * 4, out_shape=[SDS((rows, cols), F32)] * 4, compiler_params=_params(1),
    )(w.reshape(rows, cols), g.reshape(n, rows, cols), m.reshape(rows, cols), v.reshape(rows, cols))
    return tuple(o.reshape(shape) for o in outs)


def _sum_slots(own, recv, name):
    shape, n, cols = own.shape, recv.shape[0], own.shape[-1]
    own, recv = own.reshape(-1, cols), recv.reshape(n, -1, cols)
    rows = own.shape[0]
    tr = _pick(rows, 512, 16)

    def body(own_ref, r_ref, o_ref):
        acc = own_ref[...].astype(F32)
        for k in range(n):
            acc = acc + r_ref[k].astype(F32)
        o_ref[...] = acc

    return pl.pallas_call(
        body, name=name, grid=(rows // tr,),
        in_specs=[pl.BlockSpec((tr, cols), lambda i: (i, 0)), pl.BlockSpec((n, tr, cols), lambda i: (0, i, 0))],
        out_specs=pl.BlockSpec((tr, cols), lambda i: (i, 0)),
        out_shape=SDS((rows, cols), F32), compiler_params=_params(1),
    )(own, recv).reshape(shape)


def _position():
    return lax.axis_index("x"), lax.axis_index("y"), lax.axis_index("c")


def _peer(k, x, y, c):
    return (1 - x if k & 4 else x, 1 - y if k & 2 else y, 1 - c if k & 1 else c)


def _index(pos):
    return 4 * pos[0] + 2 * pos[1] + pos[2]


def _gather_small(v, name):
    rows, lanes = v.shape

    def body(x_ref, out_ref, send_sems, recv_sems, local_sem):
        me = _position()
        mine = pltpu.make_async_copy(x_ref, out_ref.at[_index(me)], local_sem)
        mine.start()

        def copy(k, slot):
            return pltpu.make_async_remote_copy(
                src_ref=x_ref, dst_ref=out_ref.at[slot], send_sem=send_sems.at[k - 1],
                recv_sem=recv_sems.at[k - 1], device_id=_peer(k, *me), device_id_type=MESH)

        sends = [copy(k, _index(me)) for k in range(1, N_DEV)]
        for cp in sends:
            cp.start()
        for k in range(1, N_DEV):
            copy(k, _index(_peer(k, *me))).wait_recv()
        for cp in sends:
            cp.wait_send()
        mine.wait()

    return pl.pallas_call(
        body, name=name, out_shape=SDS((N_DEV, rows, lanes), v.dtype),
        in_specs=[pl.BlockSpec(memory_space=pltpu.VMEM)],
        out_specs=pl.BlockSpec(memory_space=pltpu.VMEM),
        scratch_shapes=[pltpu.SemaphoreType.DMA((N_DEV - 1,)), pltpu.SemaphoreType.DMA((N_DEV - 1,)),
                        pltpu.SemaphoreType.DMA],
        compiler_params=pltpu.CompilerParams(vmem_limit_bytes=VMEM_LIMIT_V7X),
    )(v)


HBM_SPEC = pl.BlockSpec(memory_space=pltpu.HBM)
SEM_SPEC = pl.BlockSpec(memory_space=pltpu.SEMAPHORE)
SPLIT_EFFECT = pltpu.SideEffectType.DATAFLOW_SIDE_EFFECTING


def _split_start(srcs, gather, name):
    n = len(srcs)
    lands = [jnp.zeros(((N_DEV,) + s.shape) if gather else s.shape, s.dtype) for s in srcs]

    def body(*refs):
        src_refs, land_refs, sems, token = refs[:n], refs[n:2 * n], refs[2 * n:4 * n], refs[-1]
        me = _position()
        for a in range(n):
            for k in range(1, N_DEV):
                peer = _peer(k, *me)
                pltpu.make_async_remote_copy(
                    src_ref=src_refs[a] if gather else src_refs[a].at[_index(peer)],
                    dst_ref=land_refs[a].at[_index(me)], send_sem=sems[2 * a], recv_sem=sems[2 * a + 1],
                    device_id=peer, device_id_type=MESH).start()
        token[...] = jnp.zeros_like(token)

    hbm = lambda arrays: tuple(pltpu.HBM(a.shape, a.dtype) for a in arrays)
    outs = pl.pallas_call(
        body, name=name,
        out_shape=(pltpu.SemaphoreType.DMA(()),) * (2 * n) + hbm(srcs) + hbm(lands) + (SDS((8, 128), F32),),
        in_specs=(HBM_SPEC,) * (2 * n),
        out_specs=(SEM_SPEC,) * (2 * n) + (HBM_SPEC,) * (2 * n) + (pl.BlockSpec(memory_space=pltpu.VMEM),),
        input_output_aliases={a: 2 * n + a for a in range(2 * n)},
        compiler_params=pltpu.CompilerParams(has_side_effects=SPLIT_EFFECT),
    )(*[pltpu.with_memory_space_constraint(a, pltpu.HBM) for a in list(srcs) + lands])
    return outs[:2 * n], outs[2 * n:3 * n], outs[3 * n:4 * n], outs[-1]


def _split_wait(flight, after, name):
    sems, srcs, lands, _ = flight
    n = len(srcs)

    def body(*refs):
        land_refs, sem_refs = refs[n:2 * n], refs[2 * n:4 * n]
        me = _position()
        for a in range(n):
            seven = land_refs[a].at[pl.ds(0, N_DEV - 1)]
            copies = pltpu.make_async_remote_copy(
                src_ref=seven, dst_ref=seven, send_sem=sem_refs[2 * a], recv_sem=sem_refs[2 * a + 1],
                device_id=_peer(1, *me), device_id_type=MESH)
            copies.wait_send()
            copies.wait_recv()

    outs = pl.pallas_call(
        body, name=name,
        out_shape=tuple(pltpu.HBM(a.shape, a.dtype) for a in list(srcs) + list(lands)),
        in_specs=(HBM_SPEC,) * (2 * n) + (SEM_SPEC,) * (2 * n) + (pl.BlockSpec(memory_space=pl.ANY),),
        out_specs=(HBM_SPEC,) * (2 * n), input_output_aliases={a: a for a in range(2 * n)},
        compiler_params=pltpu.CompilerParams(has_side_effects=SPLIT_EFFECT),
    )(*srcs, *lands, *sems, after)
    return outs[:n], outs[n:]


def _pack_rows(arrays, lanes, dtype):
    flat = jnp.concatenate([a.astype(dtype).reshape(-1) for a in arrays])
    pad = (-flat.size) % (16 * lanes)
    if pad:
        flat = jnp.concatenate([flat, jnp.zeros((pad,), dtype)])
    return flat.reshape(-1, lanes)


def _unpack_rows(packed, shapes):
    n = packed.shape[0]
    flat = packed.reshape(n, -1)
    out, off = [], 0
    for shp in shapes:
        size = math.prod(shp)
        out.append(flat[:, off:off + size].reshape((n,) + tuple(shp)))
        off += size
    return out


def _unshard(g8, axis):
    moved = jnp.moveaxis(g8, 0, axis)
    shp = list(moved.shape)
    shp[axis:axis + 2] = [shp[axis] * shp[axis + 1]]
    return moved.reshape(shp)


def _split8(full, axis):
    shp = list(full.shape)
    shp[axis:axis + 1] = [N_DEV, shp[axis] // N_DEV]
    return jnp.moveaxis(full.reshape(shp), axis, 0)


def _my_shard(g, axis, me):
    size = g.shape[axis + 1] // N_DEV
    return lax.dynamic_slice_in_dim(g, me * size, size, axis=axis + 1)


BIG_WEIGHTS = ("ffn_w_up", "ffn_w_down", "attn_w_qkv", "attn_w_o", "ret_w_in", "ret_w_out", "pool_w")
LAYER_WEIGHTS = (
    (("ffn_w_up", 0, "cols"), ("ffn_w_down", 0, "rows"), ("pool_w", 0, "pool")),
    (("ffn_w_up", 1, "cols"), ("ffn_w_down", 1, "rows"), ("attn_w_qkv", 0, "cols"), ("attn_w_o", 0, "rows")),
    (("ffn_w_up", 2, "cols"), ("ffn_w_down", 2, "rows"), ("ret_w_in", 0, "cols"), ("ret_w_out", 0, "rows")),
    (("ffn_w_up", 3, "cols"), ("ffn_w_down", 3, "rows"), ("pool_w", 1, "pool")),
)


GATHER_GROUPS = (LAYER_WEIGHTS[0][:2],) + LAYER_WEIGHTS[1:]
GRAD_GROUPS = {"3": LAYER_WEIGHTS[3], "2": LAYER_WEIGHTS[2], "1": LAYER_WEIGHTS[1],
               "0ffn": LAYER_WEIGHTS[0][:2], "0mix": LAYER_WEIGHTS[0][2:]}


def _shard_to_send(w, kind):
    w = w.astype(BF16)
    return w.T if kind == "cols" else w


def _full_from_land(land, kind):
    return _unshard(land, 1) if kind == "pool" else land.reshape(-1, land.shape[-1])


def _grad_to_send(g, kind):
    return _split8(g, 1).astype(BF16) if kind == "pool" else g.astype(BF16).reshape(N_DEV, -1, g.shape[-1])


def _shard_grad(gsum, kind):
    return gsum.T if kind == "cols" else gsum
SMALL_SHARDED = (("norm_w", 2), ("pool_b", 1), ("pool_scale", 1), ("ret_gn_w", 1), ("ffn_conv_w", 2))
REPLICATED = ("ada_b", "attn_q_gain", "attn_k_gain", "ret_decay_logit", "ffn_conv_b")
WEIGHT_ORDER = ("c_ctx", "ada_w", "ada_b", "norm_w", "pool_w", "pool_b", "pool_scale", "attn_w_qkv",
                "attn_q_gain", "attn_k_gain", "attn_w_o", "ret_w_in", "ret_decay_logit", "ret_gn_w",
                "ret_w_out", "ffn_w_up", "ffn_conv_w", "ffn_conv_b", "ffn_w_down")


def _local_step(x0, target, mods, P, get_weights, put_grads, s_len, l_len):
    n_lat = s_len // ROW_TILE
    nw = P["norm_w"]
    lgt = jnp.broadcast_to(P["ret_decay_logit"][0][:, :, None, None], (2, RET_HEADS, 1, 128))
    cos, sin = _rope_tables(s_len, l_len)
    h_dtype = [F32 if i % 3 == 0 else BF16 for i in range(DEPTH)]
    saved = []
    mods = list(mods)
    X = x0
    h = _res_norm(X, None, None, 0, nw[0, 0], mods[0], 0, h_dtype[0], n_lat, "norm_first")
    for i in range(DEPTH):
        kind, j, mod = i % 3, i // 3, mods[i]
        W, zero = get_weights(i, "mix", h)
        W, mod = dict(W), mod + zero
        sv = {"X": X, "h": h, "W": W}
        if kind == 0:
            y = _pool_fwd(h, W["pool_w"], P["pool_b"][j:j + 1], P["pool_scale"][j:j + 1],
                          n_lat, s_len, l_len, f"pool_fwd{i}")
        elif kind == 1:
            qkv = _mm(h, W["attn_w_qkv"], "nt", F32, f"qkv{i}")
            q, k, v = _qk_prep_fwd(qkv, P["attn_q_gain"][j:j + 1], P["attn_k_gain"][j:j + 1], cos, sin)
            o, lse = _flash_fwd(q, k, v, s_len, l_len)
            y = _mm(o, W["attn_w_o"], "nn", F32, f"attn_out{i}")
            sv.update(qkv=qkv, q=q, k=k, v=v, o=o, lse=lse)
        else:
            proj = _mm(h, W["ret_w_in"], "nt", F32, f"ret_in{i}")
            o2, states = _ret_fwd(proj, lgt, s_len, l_len)
            gated = _readout_fwd(o2, proj, P["ret_gn_w"][j:j + 1], n_lat)
            y = _mm(gated, W["ret_w_out"], "nn", F32, f"ret_out{i}")
            sv.update(proj=proj, o2=o2, states=states, gated=gated)
        X1, h2 = _res_norm(X, y, mod, 0, nw[i, 1], mod, 1, BF16, n_lat, f"res_norm_mid{i}")
        W_ffn, zero = get_weights(i, "ffn", h2)
        W.update(W_ffn)
        mod = mod + zero
        u = _mm(h2, W["ffn_w_up"], "nt", FFN_HIDDEN_DTYPE, f"ffn_up{i}")
        gact = _conv_gate_fwd(u, P["ffn_conv_w"][i], P["ffn_conv_b"][i:i + 1], n_lat, f"conv_gate_fwd{i}")
        f = _mm(gact, W["ffn_w_down"], "nn", F32, f"ffn_down{i}")
        sv.update(y=y, X1=X1, h2=h2, u=u, gact=gact, f=f)
        saved.append(sv)
        if i + 1 < DEPTH:
            X, h = _res_norm(X1, f, mod, 1, nw[i + 1, 0], mods[i + 1], 0, h_dtype[i + 1], n_lat,
                             f"res_norm_end{i}")
        else:
            X = _res_norm(X1, f, mod, 1, None, None, 0, None, n_lat, "res_last")

    dX, loss = _loss_bwd(X, target, n_lat)
    G = {name: [None] * P[name].shape[0] for name in
         ("pool_b", "pool_scale", "attn_q_gain", "attn_k_gain", "ret_decay_logit", "ret_gn_w", "ffn_conv_w",
          "ffn_conv_b")}
    dnw = [[None, None] for _ in range(DEPTH)]
    dmods = [None] * DEPTH
    for i in reversed(range(DEPTH)):
        kind, j, mod, sv = i % 3, i // 3, mods[i], saved[i]
        W, gl = sv["W"], {}
        if i == DEPTH - 1:
            df, dg2 = _gate_bwd(dX, sv["f"], mod, 1, BF16, n_lat, f"gate_bwd_ffn{i}")
        dgact = _mm(df, W["ffn_w_down"], "nt", FFN_HIDDEN_DTYPE, f"ffn_down_dx{i}")
        gl["ffn_w_down"] = _mm(sv["gact"], df, "tn", BF16, f"ffn_down_dw{i}")
        du, dcw, dcb = _conv_gate_bwd(sv["u"], dgact, P["ffn_conv_w"][i], P["ffn_conv_b"][i:i + 1], n_lat,
                                      f"conv_gate_bwd{i}")
        G["ffn_conv_w"][i], G["ffn_conv_b"][i] = dcw, dcb[0]
        dh2 = _mm(du, W["ffn_w_up"], "nn", F32, f"ffn_up_dx{i}")
        gl["ffn_w_up"] = _mm(du, sv["h2"], "tn", BF16, f"ffn_up_dw{i}")
        if i == 0:
            mod = mod + put_grads("0ffn", gl)
        dX1, dnw[i][1], dsh2, dsc2, dy, dg1 = _norm_bwd(
            dX, dh2, sv["X1"], nw[i, 1], mod, 1, n_lat, f"norm_bwd_ffn{i}",
            gated=(sv["y"], mod, 0, F32 if kind == 0 else BF16))
        h = sv["h"]
        if kind == 0:
            dh, dpw, dpb, dps = _pool_bwd(h, dy, W["pool_w"], P["pool_b"][j:j + 1], P["pool_scale"][j:j + 1],
                                          n_lat, s_len, l_len, f"pool_bwd{i}")
            gl["pool_w"], G["pool_b"][j], G["pool_scale"][j] = dpw, dpb[0], dps[0]
        elif kind == 1:
            do = _mm(dy, W["attn_w_o"], "nt", F32, f"attn_out_dx{i}")
            gl["attn_w_o"] = _mm(sv["o"], dy, "tn", BF16, f"attn_out_dw{i}")
            dq, dk, dv = _flash_bwd(sv["q"], sv["k"], sv["v"], sv["o"], sv["lse"], do, s_len, l_len)
            dqkv, dqg, dkg = _qk_prep_bwd(sv["qkv"], dq, dk, dv, P["attn_q_gain"][j:j + 1],
                                          P["attn_k_gain"][j:j + 1], cos, sin)
            G["attn_q_gain"][j], G["attn_k_gain"][j] = dqg[0], dkg[0]
            dh = _mm(dqkv, W["attn_w_qkv"], "nn", F32, f"qkv_dx{i}")
            gl["attn_w_qkv"] = _mm(dqkv, h, "tn", BF16, f"qkv_dw{i}")
        else:
            dgated = _mm(dy, W["ret_w_out"], "nt", F32, f"ret_out_dx{i}")
            gl["ret_w_out"] = _mm(sv["gated"], dy, "tn", BF16, f"ret_out_dw{i}")
            do, dg, dgn = _readout_bwd(sv["o2"], sv["proj"], P["ret_gn_w"][j:j + 1], dgated, n_lat)
            dq2, dk2, dv2, dlg = _ret_bwd(sv["proj"], lgt, sv["states"], do, s_len, l_len)
            dproj = _ret_dproj(dq2, dk2, dv2, dg, n_lat)
            G["ret_gn_w"][j], G["ret_decay_logit"][j] = dgn[0], dlg[:, :, 0, 0]
            dh = _mm(dproj, W["ret_w_in"], "nn", F32, f"ret_in_dx{i}")
            gl["ret_w_in"] = _mm(dproj, h, "tn", BF16, f"ret_in_dw{i}")
        zero = put_grads(str(i) if i > 0 else "0mix", gl)
        if i > 0:
            mods[i - 1] = mods[i - 1] + zero
            dX, dnw[i][0], dsh1, dsc1, df_below, dg2_below = _norm_bwd(
                dX1, dh, sv["X"], nw[i, 0], mod, 0, n_lat, f"norm_bwd_mix{i}",
                gated=(saved[i - 1]["f"], mods[i - 1], 1, BF16))
        else:
            dX, dnw[i][0], dsh1, dsc1 = _norm_bwd(dX1, dh, sv["X"], nw[i, 0], mod, 0, n_lat, f"norm_bwd_mix{i}")
        dmods[i] = jnp.concatenate([dsh1, dsc1, dg1, dsh2, dsc2, dg2], axis=1)
        if i > 0:
            df, dg2 = df_below, dg2_below
    grads = {name: jnp.stack(parts) for name, parts in G.items()}
    grads["norm_w"] = jnp.stack([jnp.concatenate(pair, axis=0) for pair in dnw])
    return loss, dX, grads, jnp.stack(dmods)


def kernel(x, c, ctx, c_ctx, ada_w, ada_b, norm_w, pool_w, pool_b, pool_scale, attn_w_qkv, attn_q_gain,
           attn_k_gain, attn_w_o, ret_w_in, ret_decay_logit, ret_gn_w, ret_w_out, ffn_w_up, ffn_conv_w,
           ffn_conv_b, ffn_w_down, loss_target, m_c_ctx, m_ada_w, m_ada_b, m_norm_w, m_pool_w, m_pool_b,
           m_pool_scale, m_attn_w_qkv, m_attn_q_gain, m_attn_k_gain, m_attn_w_o, m_ret_w_in,
           m_ret_decay_logit, m_ret_gn_w, m_ret_w_out, m_ffn_w_up, m_ffn_conv_w, m_ffn_conv_b, m_ffn_w_down,
           v_c_ctx, v_ada_w, v_ada_b, v_norm_w, v_pool_w, v_pool_b, v_pool_scale, v_attn_w_qkv, v_attn_q_gain,
           v_attn_k_gain, v_attn_w_o, v_ret_w_in, v_ret_decay_logit, v_ret_gn_w, v_ret_w_out, v_ffn_w_up,
           v_ffn_conv_w, v_ffn_conv_b, v_ffn_w_down):
    A = dict(locals())
    me = _index(_position())
    s_len, D = x.shape[1], x.shape[2]
    l_len = ctx.shape[1]
    assert s_len % ROW_TILE == 0 and l_len % ROW_TILE == 0 and s_len % GRID_W == 0

    small = [A[n] for n, _ in SMALL_SHARDED]
    first_parts = [c] + small + [pool_w[0]]
    got = _gather_small(_pack_rows(first_parts, 128, F32), "gather_c_small")
    parts = _unpack_rows(got, [a.shape for a in first_parts])
    c_all = parts[0].reshape(N_DEV, D)
    P = {n: _unshard(g8, ax) for (n, ax), g8 in zip(SMALL_SHARDED, parts[1:-1])}
    first_pool_w = _unshard(parts[-1], 1).astype(BF16)

    c_rows = jnp.concatenate([c_all, c_ctx.reshape(1, D), jnp.zeros((7, D), F32)], axis=0)
    cols = ada_w.shape[2]
    ada_b_shard = lax.dynamic_slice_in_dim(ada_b, me * cols, cols, axis=1).reshape(DEPTH, 1, cols)
    mod_shard = _ada_fwd(c_rows, ada_w, ada_b_shard)
    got = _gather_small(mod_shard.reshape(-1, 128), "gather_mod").reshape(N_DEV, DEPTH, 16, cols)
    mod_lat = lax.dynamic_index_in_dim(got, me, axis=2, keepdims=False)
    mod_ctx = got[:, :, 8, :]
    mods = jnp.stack([jnp.moveaxis(mod_lat, 0, 1).reshape(DEPTH, 6, D),
                      jnp.moveaxis(mod_ctx, 0, 1).reshape(DEPTH, 6, D)], axis=1)

    shards = [[_shard_to_send(A[n][j], kind) for n, j, kind in group] for group in GATHER_GROUPS]
    flights, landed = {}, {}

    def start_gather(i, ready):
        mine, _ = lax.optimization_barrier((shards[i], ready))
        flights[i] = _split_start(mine, True, f"gather_start{i}")
        return flights[i][3][0, 0]

    mods = [mods[i] for i in range(DEPTH)]
    mods[0] = mods[0] + start_gather(0, mods[0])
    for n in REPLICATED:
        P[n] = A[n]

    def get_weights(i, part, x_now):
        if i == 0 and part == "mix":
            return {"pool_w": first_pool_w}, 0.0
        if i in landed:
            return landed[i], 0.0
        owns, lands = _split_wait(flights[i], x_now, f"gather_wait{i}")
        landed[i] = {n: _full_from_land(lax.dynamic_update_index_in_dim(land, own, me, axis=0), kind)
                     for (n, j, kind), own, land in zip(GATHER_GROUPS[i], owns, lands)}
        return landed[i], (start_gather(i + 1, lands) if i + 1 < DEPTH else 0.0)

    sent = {}

    def put_grads(group, gl):
        sent[group] = _split_start([_grad_to_send(gl[n], kind) for n, j, kind in GRAD_GROUPS[group]], False,
                                   f"exchange_start_{group}")
        return sent[group][3][0, 0]

    x0 = jnp.concatenate([x[0], ctx[0]], axis=0)
    loss8, dx0, G, dmods = _local_step(x0, loss_target[0], mods, P, get_weights, put_grads, s_len, l_len)
    loss = lax.psum(loss8[0, 0], ("x", "y", "c"))
    grad_x = dx0[:s_len].reshape(x.shape)

    small_names = ["dmods"] + list(REPLICATED[1:]) + [n for n, _ in SMALL_SHARDED]
    small_parts = [dmods] + [G[n] for n in small_names[1:]]
    got = _gather_small(_pack_rows(small_parts, 128, F32), "gather_small_grads")
    S8 = dict(zip(small_names, _unpack_rows(got, [a.shape for a in small_parts])))

    dm = S8["dmods"].reshape(N_DEV, DEPTH, 2, 6 * D)
    dm_mine = lax.dynamic_slice_in_dim(dm, me * cols, cols, axis=3)
    g_ada_w, pc = _ada_bwd(c_rows, ada_w, jnp.moveaxis(dm_mine[:, :, 0], 0, 1), jnp.moveaxis(dm_mine[:, :, 1], 0, 1))
    pc8 = _gather_small(pc.reshape(-1, 128), "gather_c_ctx_grad").reshape(N_DEV, 8, D)

    def owner_sums(group, after):
        sends, lands = _split_wait(sent[group], after, f"exchange_wait_{group}")
        out = {}
        for (n, j, kind), send, land in zip(GRAD_GROUPS[group], sends, lands):
            own = lax.dynamic_index_in_dim(send, me, axis=0, keepdims=False)
            out[(n, j)] = _shard_grad(_sum_slots(own, land, f"sum_slots_{n}{j}"), kind)
        return out

    shard_grads = {}
    for group in ("3", "2", "1", "0mix"):
        shard_grads.update(owner_sums(group, pc8))

    g_in = {"c_ctx": pc8[:, 0, :], "ada_w": g_ada_w[None],
            "ada_b": jnp.moveaxis(dm, 2, 1).reshape(2 * N_DEV, DEPTH, 6 * D)}
    for n in REPLICATED[1:]:
        g_in[n] = S8[n]
    for n, ax in SMALL_SHARDED:
        g_in[n] = _my_shard(S8[n], ax, me)

    def stacked(n):
        return jnp.stack([shard_grads[(n, j)] for j in range(A[n].shape[0])])[None]

    late = [n for n, j, kind in GRAD_GROUPS["0ffn"]]
    for n in BIG_WEIGHTS:
        if n not in late:
            g_in[n] = stacked(n)
    res = {n: _adamw(A[n], g_in[n], A["m_" + n], A["v_" + n], "adamw_" + n) for n in WEIGHT_ORDER if n not in late}
    done = sum(res[n][1].reshape(-1)[0] for n in res)
    shard_grads.update(owner_sums("0ffn", done.reshape(1, 1)))
    for n in late:
        res[n] = _adamw(A[n], stacked(n), A["m_" + n], A["v_" + n], "adamw_" + n)
    outs = [loss, grad_x]
    for slot in range(4):
        outs += [res[n][slot] for n in WEIGHT_ORDER]
    return tuple(outs)
```

```python
import functools
import math

import jax
import jax.numpy as jnp
from jax import lax
from jax.experimental import pallas as pl
from jax.experimental.pallas import tpu as pltpu

F32 = jnp.float32
BF16 = jnp.bfloat16
SDS = jax.ShapeDtypeStruct
MESH = pl.DeviceIdType.MESH

N_DEV = 8
EPS = 1e-6
DEPTH = 4
GRID_W = 64
POOL_WINDOWS = (2, 4, 8, 16)
N_HEADS = 8
N_KV = 2
HEAD_DIM = 128
ROPE_THETA = 10000.0
RET_HEADS = 4
RET_DK = 256
RET_DV = 512
RET_CHUNK = 128
ADAM_LR = 0.001
ADAM_B1 = 0.9
ADAM_B2 = 0.999
ADAM_EPS = 1e-08
ADAM_WD = 0.01
ADAM_STEP = 10

ROW_TILE = 256
FFN_HIDDEN_DTYPE = BF16
FLASH_FWD_TILE = 128
HALO = 8
VMEM_LIMIT_V7X = 56 * 1024 * 1024


def _params(n_axes=0):
    sem = ("arbitrary",) * n_axes if n_axes else None
    return pltpu.CompilerParams(dimension_semantics=sem, vmem_limit_bytes=VMEM_LIMIT_V7X)


def _pick(n, cap, mult):
    best = None
    for d in range(mult, min(n, cap) + 1, mult):
        if n % d == 0:
            best = d
    return best if best is not None else n


def _dot(a, b):
    return jnp.dot(a, b, preferred_element_type=F32)


def _dot_nt(a, b):
    return lax.dot_general(a, b, (((1,), (1,)), ((), ())), preferred_element_type=F32)


def _dot_tn(a, b):
    return lax.dot_general(a, b, (((0,), (0,)), ((), ())), preferred_element_type=F32)


def _bf(v):
    return v.astype(BF16)


def _sigmoid(v):
    return 0.5 * jnp.tanh(0.5 * v) + 0.5


MM_VMEM_BUDGET = 40 * 1024 * 1024
MM_STEP_BYTES = 1 << 20
MM_ACC_PASS_BYTES = 8


def _divisors(n, mult, cap):
    return [d for d in range(mult, min(n, cap) + 1, mult) if n % d == 0] or [n]


def _mm_tiles(mode, M, N, K, a_item, b_item, o_item):
    best = None
    for tm in _divisors(M, 128 if mode == "tn" else 16, 2816):
        for tn in _divisors(N, 128, 2048):
            for tk in _divisors(K, 16 if mode == "tn" else 128, 2816):
                ni, nj, nk = M // tm, N // tn, K // tk
                vmem = 2 * (tm * tk * a_item + tk * tn * b_item + tm * tn * o_item) + tm * tn * 4
                if vmem > MM_VMEM_BUDGET:
                    continue
                a_reads = 1 if nk == 1 else nj
                b_reads = 1 if (nk == 1 and nj == 1) else ni
                cost = (M * K * a_item * a_reads + K * N * b_item * b_reads + M * N * o_item
                        + ni * nj * nk * MM_STEP_BYTES + (nk - 1) * M * N * MM_ACC_PASS_BYTES)
                if best is None or cost < best[0]:
                    best = (cost, tm, tn, tk)
    return best[1:]


def _mm(a, b, mode, out_dtype, name):
    if mode == "nn":
        (M, K), (K2, N) = a.shape, b.shape
    elif mode == "nt":
        (M, K), (N, K2) = a.shape, b.shape
    else:
        (K, M), (K2, N) = a.shape, b.shape
    assert K == K2, (a.shape, b.shape, mode)
    tm, tn, tk = _mm_tiles(mode, M, N, K, a.dtype.itemsize, b.dtype.itemsize, jnp.dtype(out_dtype).itemsize)
    nk = K // tk
    if mode == "nn":
        a_spec = pl.BlockSpec((tm, tk), lambda i, j, k: (i, k))
        b_spec = pl.BlockSpec((tk, tn), lambda i, j, k: (k, j))
    elif mode == "nt":
        a_spec = pl.BlockSpec((tm, tk), lambda i, j, k: (i, k))
        b_spec = pl.BlockSpec((tn, tk), lambda i, j, k: (j, k))
    else:
        a_spec = pl.BlockSpec((tk, tm), lambda i, j, k: (k, i))
        b_spec = pl.BlockSpec((tk, tn), lambda i, j, k: (k, j))
    dot = {"nn": _dot, "nt": _dot_nt, "tn": _dot_tn}[mode]

    def body(a_ref, b_ref, o_ref, acc_ref):
        part = dot(_bf(a_ref[...]), _bf(b_ref[...]))
        if nk == 1:
            o_ref[...] = part.astype(out_dtype)
        else:
            k = pl.program_id(2)

            @pl.when(k == 0)
            def _():
                acc_ref[...] = part

            @pl.when(k > 0)
            def _():
                acc_ref[...] += part

            @pl.when(k == nk - 1)
            def _():
                o_ref[...] = acc_ref[...].astype(out_dtype)

    return pl.pallas_call(
        body, name=name, grid=(M // tm, N // tn, nk),
        in_specs=[a_spec, b_spec],
        out_specs=pl.BlockSpec((tm, tn), lambda i, j, k: (i, j)),
        out_shape=SDS((M, N), out_dtype),
        scratch_shapes=[pltpu.VMEM((tm, tn), F32)],
        compiler_params=_params(3),
    )(a, b)


def _seg_spec(n_lat, d):
    return pl.BlockSpec((1, 6, d), lambda i: ((i >= n_lat).astype(jnp.int32), 0, 0))


def _seg_acc_spec(n_lat, d):
    return pl.BlockSpec((1, 1, d), lambda i: ((i >= n_lat).astype(jnp.int32), 0, 0))


def _res_norm(x, y, gmod, gk, nw, nmod, nk, h_dtype, n_lat, name):
    R, D = x.shape
    has_res, has_norm = y is not None, nw is not None
    row = pl.BlockSpec((ROW_TILE, D), lambda i: (i, 0))
    vec = pl.BlockSpec((1, D), lambda i: (0, 0))
    ins, specs, outs, ospecs = [x], [row], [], []
    if has_res:
        ins += [y, gmod]
        specs += [row, _seg_spec(n_lat, D)]
        outs.append(SDS((R, D), F32))
        ospecs.append(row)
    if has_norm:
        ins += [nw.reshape(1, D), nmod]
        specs += [vec, _seg_spec(n_lat, D)]
        outs.append(SDS((R, D), h_dtype))
        ospecs.append(row)

    def body(*refs):
        refs = list(refs)
        z = refs.pop(0)[...]
        if has_res:
            y_ref, g_ref = refs.pop(0), refs.pop(0)
            z = z + g_ref[0, pl.ds(3 * gk + 2, 1), :] * y_ref[...].astype(F32)
        if has_norm:
            nw_ref, m_ref = refs.pop(0), refs.pop(0)
        if has_res:
            refs.pop(0)[...] = z
        if has_norm:
            r = lax.rsqrt(jnp.mean(z * z, axis=-1, keepdims=True) + EPS)
            h = (z * r) * nw_ref[...]
            h = h * (1.0 + m_ref[0, pl.ds(3 * nk + 1, 1), :]) + m_ref[0, pl.ds(3 * nk, 1), :]
            refs.pop(0)[...] = h.astype(h_dtype)

    res = pl.pallas_call(
        body, name=name, grid=(R // ROW_TILE,), in_specs=specs, out_specs=ospecs,
        out_shape=outs, compiler_params=_params(1),
    )(*ins)
    return res if len(res) > 1 else res[0]


def _gate_bwd(dz, y, mod, k, out_dtype, n_lat, name):
    R, D = dz.shape
    row = pl.BlockSpec((ROW_TILE, D), lambda i: (i, 0))

    def body(dz_ref, y_ref, m_ref, dy_ref, dg_ref):
        i = pl.program_id(0)
        dzv = dz_ref[...]
        dy_ref[...] = (m_ref[0, pl.ds(3 * k + 2, 1), :] * dzv).astype(out_dtype)

        @pl.when((i == 0) | (i == n_lat))
        def _():
            dg_ref[...] = jnp.zeros_like(dg_ref)

        dg_ref[0] += jnp.sum(dzv * y_ref[...].astype(F32), axis=0, keepdims=True)

    return pl.pallas_call(
        body, name=name, grid=(R // ROW_TILE,),
        in_specs=[row, row, _seg_spec(n_lat, D)],
        out_specs=[row, _seg_acc_spec(n_lat, D)],
        out_shape=[SDS((R, D), out_dtype), SDS((2, 1, D), F32)],
        compiler_params=_params(1),
    )(dz, y, mod)


def _norm_bwd(dz, dh, x, nw, mod, k, n_lat, name, gated=None):
    R, D = x.shape
    row = pl.BlockSpec((ROW_TILE, D), lambda i: (i, 0))
    vec = pl.BlockSpec((1, D), lambda i: (0, 0))
    ins, specs = [dz, dh, x, nw.reshape(1, D), mod], [row, row, row, vec, _seg_spec(n_lat, D)]
    outs = [SDS((R, D), F32), SDS((1, D), F32), SDS((2, 1, D), F32), SDS((2, 1, D), F32)]
    ospecs = [row, vec, _seg_acc_spec(n_lat, D), _seg_acc_spec(n_lat, D)]
    if gated is not None:
        y, gmod, gk, dy_dtype = gated
        ins += [y, gmod]
        specs += [row, _seg_spec(n_lat, D)]
        outs += [SDS((R, D), dy_dtype), SDS((2, 1, D), F32)]
        ospecs += [row, _seg_acc_spec(n_lat, D)]

    def body(dz_ref, dh_ref, x_ref, nw_ref, m_ref, *rest):
        if gated is not None:
            y_ref, g_ref, dx_ref, dnw_ref, dsh_ref, dsc_ref, dy_ref, dg_ref = rest
        else:
            dx_ref, dnw_ref, dsh_ref, dsc_ref = rest
        i = pl.program_id(0)
        xv = x_ref[...]
        dhv = dh_ref[...].astype(F32)
        nwv = nw_ref[...]
        sc1 = 1.0 + m_ref[0, pl.ds(3 * k + 1, 1), :]
        r = lax.rsqrt(jnp.mean(xv * xv, axis=-1, keepdims=True) + EPS)
        xhat = xv * r
        a = dhv * (nwv * sc1)
        dx = dz_ref[...] + r * (a - xhat * jnp.mean(a * xhat, axis=-1, keepdims=True))
        dx_ref[...] = dx

        @pl.when(i == 0)
        def _():
            dnw_ref[...] = jnp.zeros_like(dnw_ref)

        @pl.when((i == 0) | (i == n_lat))
        def _():
            dsh_ref[...] = jnp.zeros_like(dsh_ref)
            dsc_ref[...] = jnp.zeros_like(dsc_ref)
            if gated is not None:
                dg_ref[...] = jnp.zeros_like(dg_ref)

        dnw_ref[...] += jnp.sum(dhv * xhat, axis=0, keepdims=True) * sc1
        dsh_ref[0] += jnp.sum(dhv, axis=0, keepdims=True)
        dsc_ref[0] += jnp.sum(dhv * xhat, axis=0, keepdims=True) * nwv
        if gated is not None:
            dy_ref[...] = (g_ref[0, pl.ds(3 * gk + 2, 1), :] * dx).astype(dy_dtype)
            dg_ref[0] += jnp.sum(dx * y_ref[...].astype(F32), axis=0, keepdims=True)

    return pl.pallas_call(
        body, name=name, grid=(R // ROW_TILE,), in_specs=specs, out_specs=ospecs, out_shape=outs,
        compiler_params=_params(1),
    )(*ins)


def _loss_bwd(xf, target, n_lat):
    R, D = xf.shape
    row = pl.BlockSpec((ROW_TILE, D), lambda i: (i, 0))
    tgt = pl.BlockSpec((ROW_TILE, D), lambda i: (jnp.minimum(i, n_lat - 1), 0))

    def body(x_ref, t_ref, dx_ref, loss_ref):
        i = pl.program_id(0)
        e = jnp.where(i < n_lat, x_ref[...] - t_ref[...], 0.0)
        dx_ref[...] = e * (1.0 / D)

        @pl.when(i == 0)
        def _():
            loss_ref[...] = jnp.zeros_like(loss_ref)

        loss_ref[...] += 0.5 * jnp.sum(jnp.mean(e * e, axis=-1, keepdims=True))

    return pl.pallas_call(
        body, name="loss_bwd", grid=(R // ROW_TILE,),
        in_specs=[row, tgt],
        out_specs=[row, pl.BlockSpec((8, 128), lambda i: (0, 0))],
        out_shape=[SDS((R, D), F32), SDS((8, 128), F32)],
        compiler_params=_params(1),
    )(xf, target)


def _halo_rows(dtype):
    return HALO * (4 // jnp.dtype(dtype).itemsize)


def _halo_specs(n_tiles, width, tile=ROW_TILE, rows=HALO):
    per = tile // rows
    prev = pl.BlockSpec((rows, width), lambda i: (jnp.maximum(i * per - 1, 0), 0))
    nxt = pl.BlockSpec((rows, width), lambda i: (jnp.minimum((i + 1) * per, n_tiles * per - 1), 0))
    return prev, nxt


SHIFT_K = 256


def _shift_matrix(n_out, first_row, deltas):
    half = n_out // 2
    out = []
    for h, start in enumerate((0, 2 * _halo_rows(BF16))):
        r = lax.broadcasted_iota(jnp.int32, (half, SHIFT_K), 0) + (first_row + h * half - start)
        j = lax.broadcasted_iota(jnp.int32, (half, SHIFT_K), 1)
        out.append(jnp.concatenate([(j == r + d).astype(F32) for d in deltas], axis=0).astype(BF16))
    return out


def _shifted_rows(t_ref, p_ref, n_ref, cols, first, last, picks, n_blocks):
    pr = jnp.where(first, jnp.zeros_like(p_ref[:, cols]), p_ref[:, cols])
    nx = jnp.where(last, jnp.zeros_like(n_ref[:, cols]), n_ref[:, cols])
    e = jnp.concatenate([pr, t_ref[:, cols], nx], axis=0)
    start = 2 * pr.shape[0]
    top, bot = _dot(picks[0], e[0:SHIFT_K]), _dot(picks[1], e[start:start + SHIFT_K])
    half = picks[0].shape[0] // n_blocks
    return [jnp.concatenate([top[k * half:(k + 1) * half], bot[k * half:(k + 1) * half]], axis=0)
            for k in range(n_blocks)]


def _edge_flags(i, n_lat, n_tiles):
    first = (i == 0) | (i == n_lat)
    last = (i == n_lat - 1) | (i == n_tiles - 1)
    return first, last


def _conv_gate_fwd(u, conv_w, conv_b, n_lat, name):
    R, F2 = u.shape
    F = F2 // 2
    n_tiles = R // ROW_TILE
    T = ROW_TILE
    cw = _pick(F, 256, 128)
    row = pl.BlockSpec((T, F2), lambda i: (i, 0))
    prev, nxt = _halo_specs(n_tiles, F2, rows=_halo_rows(u.dtype))

    assert u.dtype == BF16

    def body(u_ref, p_ref, n_ref, w_ref, b_ref, o_ref):
        i = pl.program_id(0)
        first, last = _edge_flags(i, n_lat, n_tiles)
        taps = _shift_matrix(T, _halo_rows(BF16), (-1, 0, 1))

        def conv(c0):
            cols = pl.ds(c0, cw)
            up, uv, un = _shifted_rows(u_ref, p_ref, n_ref, cols, first, last, taps, 3)
            return (up * w_ref[pl.ds(0, 1), cols] + uv * w_ref[pl.ds(1, 1), cols]
                    + un * w_ref[pl.ds(2, 1), cols] + b_ref[:, cols])

        for c0 in range(0, F, cw):
            ca, cv = conv(c0), conv(F + c0)
            o_ref[:, pl.ds(c0, cw)] = (ca * _sigmoid(ca) * cv).astype(BF16)

    return pl.pallas_call(
        body, name=name, grid=(n_tiles,),
        in_specs=[row, prev, nxt, pl.BlockSpec((3, F2), lambda i: (0, 0)),
                  pl.BlockSpec((1, F2), lambda i: (0, 0))],
        out_specs=pl.BlockSpec((T, F), lambda i: (i, 0)),
        out_shape=SDS((R, F), BF16), compiler_params=_params(1),
    )(u, u, u, conv_w, conv_b)


def _conv_gate_bwd(u, dgact, conv_w, conv_b, n_lat, name):
    R, F2 = u.shape
    F = F2 // 2
    n_tiles = R // ROW_TILE
    T, N = ROW_TILE, ROW_TILE + 2 * HALO
    cw = _pick(F, 256, 128)
    rowu = pl.BlockSpec((T, F2), lambda i: (i, 0))
    rowg = pl.BlockSpec((T, F), lambda i: (i, 0))
    pu, nu = _halo_specs(n_tiles, F2, rows=_halo_rows(u.dtype))
    pg, ng = _halo_specs(n_tiles, F, rows=_halo_rows(dgact.dtype))

    assert u.dtype == BF16 and dgact.dtype == BF16

    def body(u_ref, pu_ref, nu_ref, g_ref, pg_ref, ng_ref, w_ref, b_ref, du_ref, dw_ref, db_ref):
        i = pl.program_id(0)
        first, last = _edge_flags(i, n_lat, n_tiles)

        @pl.when(i == 0)
        def _():
            dw_ref[...] = jnp.zeros_like(dw_ref)
            db_ref[...] = jnp.zeros_like(db_ref)

        taps = _shift_matrix(N, _halo_rows(BF16) - HALO, (-1, 0, 1))
        same = _shift_matrix(N, _halo_rows(BF16) - HALO, (0,))

        def conv(c0):
            cols = pl.ds(c0, cw)
            up, e, un = _shifted_rows(u_ref, pu_ref, nu_ref, cols, first, last, taps, 3)
            c = (up * w_ref[pl.ds(0, 1), cols] + e * w_ref[pl.ds(1, 1), cols]
                 + un * w_ref[pl.ds(2, 1), cols] + b_ref[:, cols])
            return c, up, e, un

        def back(c0, dc, up, e, un):
            cols = pl.ds(c0, cw)
            du = (pltpu.roll(dc, N - 1, 0) * w_ref[pl.ds(0, 1), cols] + dc * w_ref[pl.ds(1, 1), cols]
                  + pltpu.roll(dc, 1, 0) * w_ref[pl.ds(2, 1), cols])
            du_ref[:, cols] = du[HALO:HALO + T].astype(BF16)
            dct = dc[HALO:HALO + T]
            dw_ref[pl.ds(0, 1), cols] += jnp.sum(dct * up[HALO:HALO + T], axis=0, keepdims=True)
            dw_ref[pl.ds(1, 1), cols] += jnp.sum(dct * e[HALO:HALO + T], axis=0, keepdims=True)
            dw_ref[pl.ds(2, 1), cols] += jnp.sum(dct * un[HALO:HALO + T], axis=0, keepdims=True)
            db_ref[:, cols] += jnp.sum(dct, axis=0, keepdims=True)

        for c0 in range(0, F, cw):
            dg, = _shifted_rows(g_ref, pg_ref, ng_ref, pl.ds(c0, cw), first, last, same, 1)
            ca, upa, ea, una = conv(c0)
            cv, upv, ev, unv = conv(F + c0)
            s = _sigmoid(ca)
            back(F + c0, dg * (ca * s), upv, ev, unv)
            back(c0, dg * cv * (s * (1.0 + ca * (1.0 - s))), upa, ea, una)

    return pl.pallas_call(
        body, name=name, grid=(n_tiles,),
        in_specs=[rowu, pu, nu, rowg, pg, ng, pl.BlockSpec((3, F2), lambda i: (0, 0)),
                  pl.BlockSpec((1, F2), lambda i: (0, 0))],
        out_specs=[rowu, pl.BlockSpec((3, F2), lambda i: (0, 0)), pl.BlockSpec((1, F2), lambda i: (0, 0))],
        out_shape=[SDS((R, F2), BF16), SDS((3, F2), F32), SDS((1, F2), F32)],
        compiler_params=_params(1),
    )(u, u, u, dgact, dgact, dgact, conv_w, conv_b)


def _pool_counts(i, n_lat, s_len, l_len, n_rows, offset):
    ctx = i >= n_lat
    t0 = jnp.where(ctx, i - n_lat, i) * ROW_TILE + offset
    seg = jnp.where(ctx, l_len, s_len)
    t = t0 + lax.broadcasted_iota(jnp.int32, (n_rows, 1), 0)
    out = []
    for win in POOL_WINDOWS:
        cnt = jnp.minimum(t + win // 2, seg) - jnp.maximum(t - win // 2, 0)
        out.append(jnp.maximum(cnt, 1).astype(F32))
    return out


def _window_sum(e, lo, hi, n):
    acc = None
    for j in range(lo, hi + 1):
        term = e if j == 0 else pltpu.roll(e, (-j) % n, 0)
        acc = term if acc is None else acc + term
    return acc


def _pool_fwd(h, w, b, scale, n_lat, s_len, l_len, name):
    R, D = h.shape
    G = D // 4
    n_tiles = R // ROW_TILE
    T, N = ROW_TILE, ROW_TILE + 2 * HALO
    row = pl.BlockSpec((T, D), lambda i: (i, 0))
    prev, nxt = _halo_specs(n_tiles, D)
    vec = pl.BlockSpec((1, D), lambda i: (0, 0))

    def body(h_ref, p_ref, n_ref, w_ref, b_ref, s_ref, y_ref):
        i = pl.program_id(0)
        first, last = _edge_flags(i, n_lat, n_tiles)
        cnts = _pool_counts(i, n_lat, s_len, l_len, T, 0)
        for g, win in enumerate(POOL_WINDOWS):
            cols = pl.ds(g * G, G)
            pr = jnp.where(first, 0.0, p_ref[:, cols])
            nx = jnp.where(last, 0.0, n_ref[:, cols])
            hv = h_ref[:, cols]
            e = jnp.concatenate([pr, hv, nx], axis=0)
            mean = _window_sum(e, -(win // 2), win // 2 - 1, N)[HALO:HALO + T] / cnts[g]
            yg = _dot(_bf(mean - hv), w_ref[g])
            y_ref[:, cols] = (yg + b_ref[:, cols]) * s_ref[:, cols]

    return pl.pallas_call(
        body, name=name, grid=(n_tiles,),
        in_specs=[row, prev, nxt, pl.BlockSpec((4, G, G), lambda i: (0, 0, 0)), vec, vec],
        out_specs=row, out_shape=SDS((R, D), F32), compiler_params=_params(1),
    )(h, h, h, w, b, scale)


def _pool_bwd(h, dy, w, b, scale, n_lat, s_len, l_len, name):
    R, D = h.shape
    G = D // 4
    n_tiles = R // ROW_TILE
    T, N = ROW_TILE, ROW_TILE + 2 * HALO
    row = pl.BlockSpec((T, D), lambda i: (i, 0))
    prev, nxt = _halo_specs(n_tiles, D)
    vec = pl.BlockSpec((1, D), lambda i: (0, 0))
    wspec = pl.BlockSpec((4, G, G), lambda i: (0, 0, 0))

    def body(h_ref, ph_ref, nh_ref, d_ref, pd_ref, nd_ref, w_ref, b_ref, s_ref,
             dh_ref, dw_ref, db_ref, ds_ref):
        i = pl.program_id(0)
        first, last = _edge_flags(i, n_lat, n_tiles)

        @pl.when(i == 0)
        def _():
            dw_ref[...] = jnp.zeros_like(dw_ref)
            db_ref[...] = jnp.zeros_like(db_ref)
            ds_ref[...] = jnp.zeros_like(ds_ref)

        cnts = _pool_counts(i, n_lat, s_len, l_len, T, 0)
        cnts_ext = _pool_counts(i, n_lat, s_len, l_len, N, -HALO)
        for g, win in enumerate(POOL_WINDOWS):
            cols = pl.ds(g * G, G)

            def ext(t_ref, p_ref, n_ref):
                pr = jnp.where(first, 0.0, p_ref[:, cols])
                nx = jnp.where(last, 0.0, n_ref[:, cols])
                return jnp.concatenate([pr, t_ref[:, cols], nx], axis=0)

            hv = h_ref[:, cols]
            mean = _window_sum(ext(h_ref, ph_ref, nh_ref), -(win // 2), win // 2 - 1, N)[HALO:HALO + T] / cnts[g]
            z = _bf(mean - hv)
            sc = s_ref[:, cols]
            dye = ext(d_ref, pd_ref, nd_ref)
            dt = _bf(dye * sc)
            dz = _dot_nt(dt, w_ref[g])
            dm = dz / cnts_ext[g]
            dh = _window_sum(dm, -(win // 2 - 1), win // 2, N) - dz
            dh_ref[:, cols] = dh[HALO:HALO + T]
            dyt = dye[HALO:HALO + T]
            dw_ref[g] += _dot_tn(z, dt[HALO:HALO + T])
            db_ref[:, cols] += jnp.sum(dyt * sc, axis=0, keepdims=True)
            ds_ref[:, cols] += jnp.sum(dyt * (_dot(z, w_ref[g]) + b_ref[:, cols]), axis=0, keepdims=True)

    return pl.pallas_call(
        body, name=name, grid=(n_tiles,),
        in_specs=[row, prev, nxt, row, prev, nxt, wspec, vec, vec],
        out_specs=[row, wspec, vec, vec],
        out_shape=[SDS((R, D), F32), SDS((4, G, G), F32), SDS((1, D), F32), SDS((1, D), F32)],
        compiler_params=_params(1),
    )(h, h, h, dy, dy, dy, w, b, scale)


def _rope_tables(s_len, l_len):
    t = jnp.arange(s_len)
    row = (t // GRID_W).astype(F32)
    col = (t % GRID_W).astype(F32)
    axis_dim = HEAD_DIM // 2
    inv = ROPE_THETA ** (-jnp.arange(0, axis_dim, 2, dtype=F32) / axis_dim)
    ar, ac = row[:, None] * inv, col[:, None] * inv
    cos = jnp.concatenate([jnp.cos(ar), jnp.cos(ar), jnp.cos(ac), jnp.cos(ac)], axis=-1)
    sin = jnp.concatenate([-jnp.sin(ar), jnp.sin(ar), -jnp.sin(ac), jnp.sin(ac)], axis=-1)
    cos = jnp.concatenate([cos, jnp.ones((l_len, HEAD_DIM), F32)], axis=0)
    sin = jnp.concatenate([sin, jnp.zeros((l_len, HEAD_DIM), F32)], axis=0)
    return cos, sin


def _swap_halves(v):
    lane = lax.broadcasted_iota(jnp.int32, v.shape, 1)
    return jnp.where((lane % 64) < 32, pltpu.roll(v, 96, 1), pltpu.roll(v, 32, 1))


def _qk_prep_fwd(qkv, q_gain, k_gain, cos, sin):
    R = qkv.shape[0]
    NQ, NK = N_HEADS * HEAD_DIM, N_KV * HEAD_DIM
    T = ROW_TILE
    vec = pl.BlockSpec((1, HEAD_DIM), lambda i: (0, 0))
    tab = pl.BlockSpec((T, HEAD_DIM), lambda i: (i, 0))

    def body(x_ref, qg_ref, kg_ref, c_ref, s_ref, q_ref, k_ref, v_ref):
        cosv, sinv = c_ref[...], s_ref[...]

        def prep(c0, gain):
            xh = x_ref[:, pl.ds(c0, HEAD_DIM)]
            xn = xh * lax.rsqrt(jnp.mean(xh * xh, axis=-1, keepdims=True) + EPS) * gain
            return _bf(xn * cosv + _swap_halves(xn) * sinv)

        for hd in range(N_HEADS):
            q_ref[:, pl.ds(hd * HEAD_DIM, HEAD_DIM)] = prep(hd * HEAD_DIM, qg_ref[...])
        for hd in range(N_KV):
            k_ref[:, pl.ds(hd * HEAD_DIM, HEAD_DIM)] = prep(NQ + hd * HEAD_DIM, kg_ref[...])
            v_ref[:, pl.ds(2 * hd * HEAD_DIM, HEAD_DIM)] = _bf(x_ref[:, pl.ds(NQ + NK + hd * HEAD_DIM, HEAD_DIM)])
            v_ref[:, pl.ds((2 * hd + 1) * HEAD_DIM, HEAD_DIM)] = jnp.ones((T, HEAD_DIM), BF16)

    return pl.pallas_call(
        body, name="qk_prep_fwd", grid=(R // T,),
        in_specs=[pl.BlockSpec((T, NQ + 2 * NK), lambda i: (i, 0)), vec, vec, tab, tab],
        out_specs=[pl.BlockSpec((T, NQ), lambda i: (i, 0)), pl.BlockSpec((T, NK), lambda i: (i, 0)),
                   pl.BlockSpec((T, 2 * NK), lambda i: (i, 0))],
        out_shape=[SDS((R, NQ), BF16), SDS((R, NK), BF16), SDS((R, 2 * NK), BF16)],
        compiler_params=_params(1),
    )(qkv, q_gain, k_gain, cos, sin)


def _qk_prep_bwd(qkv, dq, dk, dv, q_gain, k_gain, cos, sin):
    R = qkv.shape[0]
    NQ, NK = N_HEADS * HEAD_DIM, N_KV * HEAD_DIM
    T = ROW_TILE
    vec = pl.BlockSpec((1, HEAD_DIM), lambda i: (0, 0))
    tab = pl.BlockSpec((T, HEAD_DIM), lambda i: (i, 0))

    def body(x_ref, dq_ref, dk_ref, dv_ref, qg_ref, kg_ref, c_ref, s_ref, o_ref, dqg_ref, dkg_ref):
        i = pl.program_id(0)
        cosv, sinv = c_ref[...], s_ref[...]

        @pl.when(i == 0)
        def _():
            dqg_ref[...] = jnp.zeros_like(dqg_ref)
            dkg_ref[...] = jnp.zeros_like(dkg_ref)

        def back(c0, dout, gain, dg_ref):
            xh = x_ref[:, pl.ds(c0, HEAD_DIM)]
            r = lax.rsqrt(jnp.mean(xh * xh, axis=-1, keepdims=True) + EPS)
            xhat = xh * r
            dxn = dout * cosv + _swap_halves(dout * sinv)
            dg_ref[...] += jnp.sum(dxn * xhat, axis=0, keepdims=True)
            a = dxn * gain
            o_ref[:, pl.ds(c0, HEAD_DIM)] = _bf(r * (a - xhat * jnp.mean(a * xhat, axis=-1, keepdims=True)))

        for hd in range(N_HEADS):
            back(hd * HEAD_DIM, dq_ref[:, pl.ds(hd * HEAD_DIM, HEAD_DIM)], qg_ref[...], dqg_ref)
        for hd in range(N_KV):
            back(NQ + hd * HEAD_DIM, dk_ref[:, pl.ds(hd * HEAD_DIM, HEAD_DIM)], kg_ref[...], dkg_ref)
        o_ref[:, pl.ds(NQ + NK, NK)] = _bf(dv_ref[...])

    return pl.pallas_call(
        body, name="qk_prep_bwd", grid=(R // T,),
        in_specs=[pl.BlockSpec((T, NQ + 2 * NK), lambda i: (i, 0)), pl.BlockSpec((T, NQ), lambda i: (i, 0)),
                  pl.BlockSpec((T, NK), lambda i: (i, 0)), pl.BlockSpec((T, NK), lambda i: (i, 0)),
                  vec, vec, tab, tab],
        out_specs=[pl.BlockSpec((T, NQ + 2 * NK), lambda i: (i, 0)), vec, vec],
        out_shape=[SDS((R, NQ + 2 * NK), BF16), SDS((1, HEAD_DIM), F32), SDS((1, HEAD_DIM), F32)],
        compiler_params=_params(1),
    )(qkv, dq, dk, dv, q_gain, k_gain, cos, sin)


def _flash_fwd(q, k, v, s_len, l_len):
    R = q.shape[0]
    T = FLASH_FWD_TILE
    n_lat = s_len // T
    ck = _pick(s_len, 1024, 128)
    scale = HEAD_DIM ** -0.5
    group = N_HEADS // N_KV
    GW = group * HEAD_DIM
    M = group * T
    chunks = s_len // ck
    to_log2 = scale * math.log2(math.e)

    def body(q_ref, k_ref, v_ref, o_ref, lse_ref, s_s, sc_s, ml_s, mb_s, acc_s):
        i = pl.program_id(1)
        qv = jnp.concatenate([q_ref[:, pl.ds(hh * HEAD_DIM, HEAD_DIM)] for hh in range(group)], axis=0)

        ml_s[...] = jnp.full_like(ml_s, -jnp.inf)

        def lane_max(s, n):
            m = ml_s[...]
            for t in range(n // HEAD_DIM):
                m = jnp.maximum(m, s[:, t * HEAD_DIM:(t + 1) * HEAD_DIM])
            ml_s[...] = m

        @pl.when(i < n_lat)
        def _():
            def loop(c, carry):
                s = _dot_nt(qv, k_ref[pl.ds(pl.multiple_of(c * ck, ck), ck), :])
                s_s[c] = s
                lane_max(s, ck)
                return carry
            lax.fori_loop(0, chunks, loop, 0, unroll=4 if chunks % 4 == 0 else 1)

        sc = _dot_nt(qv, k_ref[pl.ds(s_len, l_len), :])
        sc_s[...] = sc
        lane_max(sc, l_len)
        m_row = jnp.max(ml_s[...], axis=-1, keepdims=True) * to_log2
        mb_s[...] = jnp.broadcast_to(m_row, (M, ck))

        acc_s[...] = jnp.zeros_like(acc_s)

        @pl.when(i < n_lat)
        def _():
            def loop(c, carry):
                p = jnp.exp2(s_s[c] * to_log2 - mb_s[...])
                acc_s[...] += _dot(_bf(p), v_ref[pl.ds(pl.multiple_of(c * ck, ck), ck), :])
                return carry
            lax.fori_loop(0, chunks, loop, 0, unroll=4 if chunks % 4 == 0 else 1)

        p = jnp.exp2(sc_s[...] * to_log2 - mb_s[:, pl.ds(0, l_len)])
        acc_s[...] += _dot(_bf(p), v_ref[pl.ds(s_len, l_len), :])
        l_rep = acc_s[:, pl.ds(HEAD_DIM, HEAD_DIM)]
        o = acc_s[:, pl.ds(0, HEAD_DIM)] / l_rep
        for hh in range(group):
            o_ref[:, pl.ds(hh * HEAD_DIM, HEAD_DIM)] = o[hh * T:(hh + 1) * T]
        lse = (mb_s[:, pl.ds(0, HEAD_DIM)] + jnp.log2(l_rep)) * math.log(2.0)
        lse_ref[...] = jnp.max(lse, axis=-1, keepdims=True).reshape(group, T, 1)

    return pl.pallas_call(
        body, name="flash_fwd", grid=(N_KV, R // T),
        in_specs=[pl.BlockSpec((T, GW), lambda g, i: (i, g)),
                  pl.BlockSpec((R, HEAD_DIM), lambda g, i: (0, g)),
                  pl.BlockSpec((R, 2 * HEAD_DIM), lambda g, i: (0, g))],
        out_specs=[pl.BlockSpec((T, GW), lambda g, i: (i, g)),
                   pl.BlockSpec((group, T, 1), lambda g, i: (g, i, 0))],
        out_shape=[SDS((R, N_HEADS * HEAD_DIM), F32), SDS((N_HEADS, R, 1), F32)],
        scratch_shapes=[pltpu.VMEM((chunks, M, ck), F32), pltpu.VMEM((M, l_len), F32), pltpu.VMEM((M, HEAD_DIM), F32),
                        pltpu.VMEM((M, ck), F32), pltpu.VMEM((M, 2 * HEAD_DIM), F32)],
        compiler_params=_params(2),
    )(q, k, v)


def _flash_bwd(q, k, v, o, lse, do, s_len, l_len):
    R = q.shape[0]
    T = ROW_TILE
    n_lat = s_len // T
    ck = _pick(s_len, 512, 128)
    scale = HEAD_DIM ** -0.5
    group = N_HEADS // N_KV
    GW = group * HEAD_DIM
    qspec = pl.BlockSpec((T, GW), lambda g, i: (i, g))
    kspec = pl.BlockSpec((R, HEAD_DIM), lambda g, i: (0, g))

    M = group * T
    log2e = math.log2(math.e)

    def body(q_ref, do_ref, o_ref, lse_ref, k_ref, v_ref, dq_ref, dk_ref, dv_ref, dq_s, lse_s, delta_s):
        i = pl.program_id(1)

        @pl.when(i == 0)
        def _():
            dk_ref[...] = jnp.zeros_like(dk_ref)
            dv_ref[...] = jnp.zeros_like(dv_ref)

        def stacked(ref):
            return jnp.concatenate([ref[:, pl.ds(hh * HEAD_DIM, HEAD_DIM)] for hh in range(group)], axis=0)

        qv = stacked(q_ref)
        dov = stacked(do_ref)
        dob = _bf(dov)
        delta_s[...] = jnp.broadcast_to(jnp.sum(dov * stacked(o_ref), axis=-1, keepdims=True), (M, ck))
        lse_s[...] = jnp.broadcast_to(lse_ref[...].reshape(M, 1) * log2e, (M, ck))
        dq_s[...] = jnp.zeros_like(dq_s)

        def step(rows, n):
            kv, vv = k_ref[rows, :], v_ref[rows, :]
            p = jnp.exp2(_dot_nt(qv, kv) * (scale * log2e) - lse_s[:, pl.ds(0, n)])
            dv_ref[rows, :] += _dot_tn(_bf(p), dob)
            ds = _bf(p * (_dot_nt(dob, vv) - delta_s[:, pl.ds(0, n)]) * scale)
            dq_s[...] += _dot(ds, kv)
            dk_ref[rows, :] += _dot_tn(ds, qv)

        @pl.when(i < n_lat)
        def _():
            def loop(c, carry):
                step(pl.ds(pl.multiple_of(c * ck, ck), ck), ck)
                return carry
            lax.fori_loop(0, s_len // ck, loop, 0, unroll=4 if (s_len // ck) % 4 == 0 else 1)

        step(pl.ds(s_len, l_len), l_len)
        for hh in range(group):
            dq_ref[:, pl.ds(hh * HEAD_DIM, HEAD_DIM)] = dq_s[pl.ds(hh * T, T), :]

    return pl.pallas_call(
        body, name="flash_bwd", grid=(N_KV, R // T),
        in_specs=[qspec, qspec, qspec, pl.BlockSpec((group, T, 1), lambda g, i: (g, i, 0)), kspec,
                  pl.BlockSpec((R, HEAD_DIM), lambda g, i: (0, 2 * g))],
        out_specs=[qspec, kspec, kspec],
        out_shape=[SDS((R, N_HEADS * HEAD_DIM), F32), SDS((R, N_KV * HEAD_DIM), F32),
                   SDS((R, N_KV * HEAD_DIM), F32)],
        scratch_shapes=[pltpu.VMEM((M, HEAD_DIM), F32), pltpu.VMEM((M, ck), F32), pltpu.VMEM((M, ck), F32)],
        compiler_params=_params(2),
    )(q, do, o, lse, k, v)


K_SCALE = RET_DK ** -0.5


def _log_sigmoid(v):
    return -(jnp.maximum(-v, 0.0) + jnp.log(1.0 + jnp.exp(-jnp.abs(v))))


def _ret_decays(d, lg):
    C = RET_CHUNK
    ic = lax.broadcasted_iota(jnp.int32, (C, 1), 0)
    ir = lax.broadcasted_iota(jnp.int32, (1, C), 1)
    li = jnp.where(d == 0, ic, C - 1 - ic).astype(F32)
    lj = jnp.where(d == 0, ir, C - 1 - ir).astype(F32)
    diff = li - lj
    mask = jnp.where(diff >= 0, jnp.exp(jnp.maximum(diff, 0.0) * lg), 0.0)
    qd = jnp.exp((li + 1.0) * lg)
    kd = jnp.exp((C - 1.0 - li) * lg)
    cd = jnp.exp(C * lg)
    return li, diff, mask, qd, kd, cd


def _ctx_weights(d, t, lg, l_len):
    C = RET_CHUNK
    j = (t * C + lax.broadcasted_iota(jnp.int32, (C, 1), 0)).astype(F32)
    e = jnp.where(d == 0, (l_len - 1.0) - j, j)
    return e, jnp.exp(e * lg)


def _mirrored(x, i, n_lat):
    return jnp.where(i < n_lat, jnp.concatenate([x[RET_CHUNK:], x[:RET_CHUNK]], axis=0), x)


def _mirror_tile(n_lat):
    return lambda i: jnp.where(i < n_lat, n_lat - 1 - i, i)


def _ret_fwd(proj, lgt, s_len, l_len):
    R = proj.shape[0]
    C, H, DK, DV = RET_CHUNK, RET_HEADS, RET_DK, RET_DV
    nl, nc = s_len // C, l_len // C

    def stored(t):
        return jnp.where(t < nc, nl + t, jnp.maximum(t - nc, 0))

    def actual(d, t):
        n = jnp.maximum(t - nc, 0)
        return jnp.where(t < nc, nl + t, n if d == 0 else nl - 1 - n)

    def body(q0_ref, k0_ref, v0_ref, q1_ref, k1_ref, v1_ref, lg_ref, o_ref, st_ref, r_s):
        t = pl.program_id(0)
        qkv = ((q0_ref, k0_ref, v0_ref), (q1_ref, k1_ref, v1_ref))

        @pl.when(t == 0)
        def _():
            r_s[...] = jnp.zeros_like(r_s)

        def log_gamma(d, hh):
            return jnp.max(_log_sigmoid(lg_ref[d, hh]), axis=-1, keepdims=True)

        @pl.when(t < nc)
        def _():
            for d, (q_ref, k_ref, v_ref) in enumerate(qkv):
                for hh in range(H):
                    qc, vc = pl.ds(hh * DK, DK), pl.ds(hh * DV, DV)
                    _, w = _ctx_weights(d, t, log_gamma(d, hh), l_len)
                    r_s[d, hh] += _dot_tn(_bf(k_ref[:, qc] * K_SCALE * w), _bf(v_ref[:, vc]))
                    o_ref[d, :, vc] = jnp.zeros((C, DV), F32)

        @pl.when(t >= nc)
        def _():
            for d, (q_ref, k_ref, v_ref) in enumerate(qkv):
                for hh in range(H):
                    qc, vc = pl.ds(hh * DK, DK), pl.ds(hh * DV, DV)
                    _, _, mask, qd, kd, cd = _ret_decays(d, log_gamma(d, hh))
                    qb, kv, vb = _bf(q_ref[:, qc]), k_ref[:, qc] * K_SCALE, _bf(v_ref[:, vc])
                    r = r_s[d, hh]
                    st_ref[d, hh, 0] = r
                    att = _dot_nt(qb, _bf(kv)) * mask
                    o_ref[d, :, vc] = _dot(_bf(att), vb) + _dot(qb, _bf(r)) * qd
                    r_s[d, hh] = r * cd + _dot_tn(_bf(kv * kd), vb)

    def rows(d):
        return [pl.BlockSpec((C, H * DK), lambda t: (actual(d, t), 0)),
                pl.BlockSpec((C, H * DK), lambda t: (actual(d, t), 1)),
                pl.BlockSpec((C, H * DV), lambda t: (actual(d, t), 1))]

    return pl.pallas_call(
        body, name="ret_fwd", grid=(nc + nl,),
        in_specs=rows(0) + rows(1) + [pl.BlockSpec((2, H, 1, 128), lambda t: (0, 0, 0, 0))],
        out_specs=[pl.BlockSpec((2, C, H * DV), lambda t: (0, stored(t), 0)),
                   pl.BlockSpec((2, H, 1, DK, DV), lambda t: (0, 0, jnp.maximum(t - nc, 0), 0, 0))],
        out_shape=[SDS((2, R, H * DV), F32), SDS((2, H, nl, DK, DV), F32)],
        scratch_shapes=[pltpu.VMEM((2, H, DK, DV), F32)],
        compiler_params=_params(1),
    )(proj, proj, proj, proj, proj, proj, lgt)


def _ret_bwd(proj, lgt, states, do, s_len, l_len):
    R = proj.shape[0]
    C, H, DK, DV = RET_CHUNK, RET_HEADS, RET_DK, RET_DV
    nl, nc = s_len // C, l_len // C
    last = nl + nc - 1

    def stored(t):
        return jnp.where(t < nl, jnp.maximum(nl - 1 - t, 0), t)

    def actual(d, t):
        return stored(t) if d == 0 else t

    def body(q0_ref, k0_ref, v0_ref, do0_ref, q1_ref, k1_ref, v1_ref, do1_ref, lg_ref, st_ref,
             dq_ref, dk_ref, dv_ref, dlg_ref, dr_s, dl_s):
        t = pl.program_id(0)
        ins = ((q0_ref, k0_ref, v0_ref, do0_ref), (q1_ref, k1_ref, v1_ref, do1_ref))

        def log_gamma(d, hh):
            return jnp.max(_log_sigmoid(lg_ref[d, hh]), axis=-1, keepdims=True)

        @pl.when(t == 0)
        def _():
            dr_s[...] = jnp.zeros_like(dr_s)
            dl_s[...] = jnp.zeros_like(dl_s)

        @pl.when(t < nl)
        def _():
            for d, (q_ref, k_ref, v_ref, do_ref) in enumerate(ins):
                for hh in range(H):
                    qc, vc = pl.ds(hh * DK, DK), pl.ds(hh * DV, DV)
                    li, diff, mask, qd, kd, cd = _ret_decays(d, log_gamma(d, hh))
                    qv, kv, vv, dov = q_ref[:, qc], k_ref[:, qc] * K_SCALE, v_ref[:, vc], do_ref[:, vc]
                    qb, kb, vb, dob = _bf(qv), _bf(kv), _bf(vv), _bf(dov)
                    r, drn = st_ref[d, hh, 0], dr_s[d, hh]
                    rb, drb = _bf(r), _bf(drn)
                    p = _dot_nt(qb, kb)
                    dp = _dot_nt(dob, vb) * mask
                    dpb = _bf(dp)
                    doq = _bf(dov * qd)
                    dq_inter = _dot_nt(doq, rb)
                    dk_state = kd * _dot_nt(vb, drb)
                    dq_ref[d, :, qc] = _dot(dpb, kb) + dq_inter
                    dk_ref[d, :, qc] = (_dot_tn(dpb, qb) + dk_state) * K_SCALE
                    dv_ref[d, :, vc] = _dot_tn(_bf(p * mask), dob) + _dot(_bf(kv * kd), drb)
                    dr_s[d, hh] = cd * drn + _dot_tn(qb, doq)
                    dl_s[d, hh] += (jnp.sum(dp * p * diff) + jnp.sum((li + 1.0) * qv * dq_inter)
                                    + jnp.sum((C - 1.0 - li) * kv * dk_state) + C * jnp.sum(cd * r * drn))

        @pl.when(t >= nl)
        def _():
            for d, (q_ref, k_ref, v_ref, do_ref) in enumerate(ins):
                for hh in range(H):
                    qc, vc = pl.ds(hh * DK, DK), pl.ds(hh * DV, DV)
                    e, w = _ctx_weights(d, t - nl, log_gamma(d, hh), l_len)
                    kv, vb, drb = k_ref[:, qc] * K_SCALE, _bf(v_ref[:, vc]), _bf(dr_s[d, hh])
                    dkc = w * _dot_nt(vb, drb)
                    dq_ref[d, :, qc] = jnp.zeros((C, DK), F32)
                    dk_ref[d, :, qc] = dkc * K_SCALE
                    dv_ref[d, :, vc] = _dot(_bf(kv * w), drb)
                    dl_s[d, hh] += jnp.sum(e * kv * dkc)

        @pl.when(t == last)
        def _():
            for d in range(2):
                for hh in range(H):
                    dlg_ref[d, hh] = dl_s[d, hh] * (1.0 / (1.0 + jnp.exp(lg_ref[d, hh])))

    def rows(d):
        return [pl.BlockSpec((C, H * DK), lambda t: (actual(d, t), 0)),
                pl.BlockSpec((C, H * DK), lambda t: (actual(d, t), 1)),
                pl.BlockSpec((C, H * DV), lambda t: (actual(d, t), 1)),
                pl.BlockSpec((C, H * DV), lambda t: (actual(d, t), 0))]

    return pl.pallas_call(
        body, name="ret_bwd", grid=(nl + nc,),
        in_specs=rows(0) + rows(1) + [
            pl.BlockSpec((2, H, 1, 128), lambda t: (0, 0, 0, 0)),
            pl.BlockSpec((2, H, 1, DK, DV), lambda t: (0, 0, jnp.maximum(nl - 1 - t, 0), 0, 0))],
        out_specs=[pl.BlockSpec((2, C, H * DK), lambda t: (0, stored(t), 0)),
                   pl.BlockSpec((2, C, H * DK), lambda t: (0, stored(t), 0)),
                   pl.BlockSpec((2, C, H * DV), lambda t: (0, stored(t), 0)),
                   pl.BlockSpec((2, H, 1, 128), lambda t: (0, 0, 0, 0))],
        out_shape=[SDS((2, R, H * DK), F32), SDS((2, R, H * DK), F32), SDS((2, R, H * DV), F32),
                   SDS((2, H, 1, 128), F32)],
        scratch_shapes=[pltpu.VMEM((2, H, DK, DV), F32), pltpu.VMEM((2, H, 1, 128), F32)],
        compiler_params=_params(1),
    )(proj, proj, proj, do, proj, proj, proj, do, lgt, states)


def _readout_fwd(o2, proj, gn_w, n_lat):
    R = proj.shape[0]
    H, DV = RET_HEADS, RET_DV
    W = H * DV
    T = ROW_TILE
    assert T == 2 * RET_CHUNK

    def body(o_ref, ob_ref, g_ref, w_ref, out_ref):
        i = pl.program_id(0)
        for hh in range(H):
            cols = pl.ds(hh * DV, DV)
            y = o_ref[0, :, cols] + _mirrored(ob_ref[0, :, cols], i, n_lat)
            yc = y - jnp.mean(y, axis=-1, keepdims=True)
            yn = yc * lax.rsqrt(jnp.mean(yc * yc, axis=-1, keepdims=True) + EPS) * w_ref[:, cols]
            g = g_ref[:, cols]
            out_ref[:, cols] = _bf(g * _sigmoid(g) * yn)

    return pl.pallas_call(
        body, name="readout_fwd", grid=(R // T,),
        in_specs=[pl.BlockSpec((1, T, W), lambda i: (0, i, 0)),
                  pl.BlockSpec((1, T, W), lambda i: (1, _mirror_tile(n_lat)(i), 0)),
                  pl.BlockSpec((T, W), lambda i: (i, 2)), pl.BlockSpec((1, W), lambda i: (0, 0))],
        out_specs=pl.BlockSpec((T, W), lambda i: (i, 0)),
        out_shape=SDS((R, W), BF16), compiler_params=_params(1),
    )(o2, o2, proj, gn_w)


def _readout_bwd(o2, proj, gn_w, dgated, n_lat):
    R = proj.shape[0]
    H, DV = RET_HEADS, RET_DV
    W = H * DV
    T = ROW_TILE

    def body(o_ref, ob_ref, g_ref, w_ref, d_ref, do_ref, dg_ref, dw_ref):
        i = pl.program_id(0)

        @pl.when(i == 0)
        def _():
            dw_ref[...] = jnp.zeros_like(dw_ref)

        for hh in range(H):
            cols = pl.ds(hh * DV, DV)
            y = o_ref[0, :, cols] + _mirrored(ob_ref[0, :, cols], i, n_lat)
            yc = y - jnp.mean(y, axis=-1, keepdims=True)
            rstd = lax.rsqrt(jnp.mean(yc * yc, axis=-1, keepdims=True) + EPS)
            yn0 = yc * rstd
            wv = w_ref[:, cols]
            g = g_ref[:, cols]
            s = _sigmoid(g)
            dgt = d_ref[:, cols]
            dyn = dgt * (g * s)
            dg_ref[:, cols] = _bf(dgt * (yn0 * wv) * (s * (1.0 + g * (1.0 - s))))
            dw_ref[:, cols] += jnp.sum(dyn * yn0, axis=0, keepdims=True)
            a = dyn * wv
            do_ref[:, cols] = rstd * (a - jnp.mean(a, axis=-1, keepdims=True)
                                      - yn0 * jnp.mean(a * yn0, axis=-1, keepdims=True))

    return pl.pallas_call(
        body, name="readout_bwd", grid=(R // T,),
        in_specs=[pl.BlockSpec((1, T, W), lambda i: (0, i, 0)),
                  pl.BlockSpec((1, T, W), lambda i: (1, _mirror_tile(n_lat)(i), 0)),
                  pl.BlockSpec((T, W), lambda i: (i, 2)),
                  pl.BlockSpec((1, W), lambda i: (0, 0)), pl.BlockSpec((T, W), lambda i: (i, 0))],
        out_specs=[pl.BlockSpec((T, W), lambda i: (i, 0)), pl.BlockSpec((T, W), lambda i: (i, 0)),
                   pl.BlockSpec((1, W), lambda i: (0, 0))],
        out_shape=[SDS((R, W), F32), SDS((R, W), BF16), SDS((1, W), F32)],
        compiler_params=_params(1),
    )(o2, o2, proj, gn_w, dgated)


def _ret_dproj(dq2, dk2, dv2, dg, n_lat):
    R = dg.shape[0]
    NQ, NV = RET_HEADS * RET_DK, RET_HEADS * RET_DV
    T = ROW_TILE

    def body(dq_ref, dqb_ref, dk_ref, dkb_ref, dv_ref, dvb_ref, dg_ref, o_ref):
        i = pl.program_id(0)
        o_ref[:, pl.ds(0, NQ)] = _bf(dq_ref[0] + _mirrored(dqb_ref[0], i, n_lat))
        o_ref[:, pl.ds(NQ, NQ)] = _bf(dk_ref[0] + _mirrored(dkb_ref[0], i, n_lat))
        o_ref[:, pl.ds(2 * NQ, NV)] = _bf(dv_ref[0] + _mirrored(dvb_ref[0], i, n_lat))
        o_ref[:, pl.ds(2 * NQ + NV, NV)] = dg_ref[...]

    def both(width):
        return [pl.BlockSpec((1, T, width), lambda i: (0, i, 0)),
                pl.BlockSpec((1, T, width), lambda i: (1, _mirror_tile(n_lat)(i), 0))]

    return pl.pallas_call(
        body, name="ret_dproj", grid=(R // T,),
        in_specs=both(NQ) + both(NQ) + both(NV) + [pl.BlockSpec((T, NV), lambda i: (i, 0))],
        out_specs=pl.BlockSpec((T, 2 * NQ + 2 * NV), lambda i: (i, 0)),
        out_shape=SDS((R, 2 * NQ + 2 * NV), BF16), compiler_params=_params(1),
    )(dq2, dq2, dk2, dk2, dv2, dv2, dg)


def _silu(v):
    return v * _sigmoid(v)


def _ada_fwd(c_rows, ada_w, ada_b_shard):
    depth, D, cols = ada_w.shape

    def body(c_ref, w_ref, b_ref, o_ref):
        o_ref[0] = _dot(_bf(_silu(c_ref[...])), _bf(w_ref[0])) + b_ref[0]

    return pl.pallas_call(
        body, name="ada_fwd", grid=(depth,),
        in_specs=[pl.BlockSpec((16, D), lambda i: (0, 0)), pl.BlockSpec((1, D, cols), lambda i: (i, 0, 0)),
                  pl.BlockSpec((1, 1, cols), lambda i: (i, 0, 0))],
        out_specs=pl.BlockSpec((1, 16, cols), lambda i: (i, 0, 0)),
        out_shape=SDS((depth, 16, cols), F32), compiler_params=_params(1),
    )(c_rows, ada_w, ada_b_shard)


def _ada_bwd(c_rows, ada_w, d_lat, d_ctx):
    depth, D, cols = ada_w.shape

    def body(c_ref, w_ref, dl_ref, dc_ref, dw_ref, pc_ref):
        i = pl.program_id(0)
        cv = c_ref[...]
        a = _silu(cv)
        dcs = jnp.broadcast_to(jnp.sum(dc_ref[0], axis=0, keepdims=True), (8, cols))
        dw_ref[0] = _dot_tn(_bf(a[0:8]), _bf(dl_ref[0])) + _dot_tn(_bf(a[8:16]), _bf(dcs))

        @pl.when(i == 0)
        def _():
            pc_ref[...] = jnp.zeros_like(pc_ref)

        pc_ref[...] += _dot_nt(_bf(dcs), _bf(w_ref[0]))

        @pl.when(i == depth - 1)
        def _():
            cc = c_ref[pl.ds(8, 1), :]
            s = _sigmoid(cc)
            pc_ref[...] = pc_ref[...] * (s * (1.0 + cc * (1.0 - s)))

    return pl.pallas_call(
        body, name="ada_bwd", grid=(depth,),
        in_specs=[pl.BlockSpec((16, D), lambda i: (0, 0)), pl.BlockSpec((1, D, cols), lambda i: (i, 0, 0)),
                  pl.BlockSpec((1, 8, cols), lambda i: (i, 0, 0)), pl.BlockSpec((1, 8, cols), lambda i: (i, 0, 0))],
        out_specs=[pl.BlockSpec((1, D, cols), lambda i: (i, 0, 0)), pl.BlockSpec((8, D), lambda i: (0, 0))],
        out_shape=[SDS((depth, D, cols), F32), SDS((8, D), F32)], compiler_params=_params(1),
    )(c_rows, ada_w, d_lat, d_ctx)


def _adamw(w, g, m, v, name):
    shape = w.shape
    n = g.shape[0]
    cols = shape[-1]
    rows = w.size // cols
    tr = _pick(rows, 512, 8) if rows * cols * 4 > (1 << 20) else rows
    spec = pl.BlockSpec((tr, cols), lambda i: (i, 0))

    def body(w_ref, g_ref, m_ref, v_ref, go_ref, d_ref, mo_ref, vo_ref):
        gs = g_ref[0].astype(F32)
        for k in range(1, n):
            gs = gs + g_ref[k].astype(F32)
        mn = ADAM_B1 * m_ref[...] + (1.0 - ADAM_B1) * gs
        vn = ADAM_B2 * v_ref[...] + (1.0 - ADAM_B2) * jnp.square(gs)
        m_hat = mn / (1.0 - ADAM_B1 ** ADAM_STEP)
        v_hat = vn / (1.0 - ADAM_B2 ** ADAM_STEP)
        go_ref[...] = gs
        d_ref[...] = -ADAM_LR * (m_hat / (jnp.sqrt(v_hat) + ADAM_EPS) + ADAM_WD * w_ref[...])
        mo_ref[...] = mn
        vo_ref[...] = vn

    outs = pl.pallas_call(
        body, name=name, grid=(rows // tr,),
        in_specs=[spec, pl.BlockSpec((n, tr, cols), lambda i: (0, i, 0)), spec, spec],
        out_specs=[spec] * 4, out_shape=[SDS((rows, cols), F32)] * 4, compiler_params=_params(1),
    )(w.reshape(rows, cols), g.reshape(n, rows, cols), m.reshape(rows, cols), v.reshape(rows, cols))
    return tuple(o.reshape(shape) for o in outs)


def _sum_slots(own, recv, name):
    shape, n, cols = own.shape, recv.shape[0], own.shape[-1]
    own, recv = own.reshape(-1, cols), recv.reshape(n, -1, cols)
    rows = own.shape[0]
    tr = _pick(rows, 512, 16)

    def body(own_ref, r_ref, o_ref):
        acc = own_ref[...].astype(F32)
        for k in range(n):
            acc = acc + r_ref[k].astype(F32)
        o_ref[...] = acc

    return pl.pallas_call(
        body, name=name, grid=(rows // tr,),
        in_specs=[pl.BlockSpec((tr, cols), lambda i: (i, 0)), pl.BlockSpec((n, tr, cols), lambda i: (0, i, 0))],
        out_specs=pl.BlockSpec((tr, cols), lambda i: (i, 0)),
        out_shape=SDS((rows, cols), F32), compiler_params=_params(1),
    )(own, recv).reshape(shape)


def _position():
    return lax.axis_index("x"), lax.axis_index("y"), lax.axis_index("c")


def _peer(k, x, y, c):
    return (1 - x if k & 4 else x, 1 - y if k & 2 else y, 1 - c if k & 1 else c)


def _index(pos):
    return 4 * pos[0] + 2 * pos[1] + pos[2]


def _gather_small(v, name):
    rows, lanes = v.shape

    def body(x_ref, out_ref, send_sems, recv_sems, local_sem):
        me = _position()
        mine = pltpu.make_async_copy(x_ref, out_ref.at[_index(me)], local_sem)
        mine.start()

        def copy(k, slot):
            return pltpu.make_async_remote_copy(
                src_ref=x_ref, dst_ref=out_ref.at[slot], send_sem=send_sems.at[k - 1],
                recv_sem=recv_sems.at[k - 1], device_id=_peer(k, *me), device_id_type=MESH)

        sends = [copy(k, _index(me)) for k in range(1, N_DEV)]
        for cp in sends:
            cp.start()
        for k in range(1, N_DEV):
            copy(k, _index(_peer(k, *me))).wait_recv()
        for cp in sends:
            cp.wait_send()
        mine.wait()

    return pl.pallas_call(
        body, name=name, out_shape=SDS((N_DEV, rows, lanes), v.dtype),
        in_specs=[pl.BlockSpec(memory_space=pltpu.VMEM)],
        out_specs=pl.BlockSpec(memory_space=pltpu.VMEM),
        scratch_shapes=[pltpu.SemaphoreType.DMA((N_DEV - 1,)), pltpu.SemaphoreType.DMA((N_DEV - 1,)),
                        pltpu.SemaphoreType.DMA],
        compiler_params=pltpu.CompilerParams(vmem_limit_bytes=VMEM_LIMIT_V7X),
    )(v)


HBM_SPEC = pl.BlockSpec(memory_space=pltpu.HBM)
SEM_SPEC = pl.BlockSpec(memory_space=pltpu.SEMAPHORE)
SPLIT_EFFECT = pltpu.SideEffectType.DATAFLOW_SIDE_EFFECTING


def _split_start(srcs, gather, name):
    n = len(srcs)
    lands = [jnp.zeros(((N_DEV,) + s.shape) if gather else s.shape, s.dtype) for s in srcs]

    def body(*refs):
        src_refs, land_refs, sems, token = refs[:n], refs[n:2 * n], refs[2 * n:4 * n], refs[-1]
        me = _position()
        for a in range(n):
            for k in range(1, N_DEV):
                peer = _peer(k, *me)
                pltpu.make_async_remote_copy(
                    src_ref=src_refs[a] if gather else src_refs[a].at[_index(peer)],
                    dst_ref=land_refs[a].at[_index(me)], send_sem=sems[2 * a], recv_sem=sems[2 * a + 1],
                    device_id=peer, device_id_type=MESH).start()
        token[...] = jnp.zeros_like(token)

    hbm = lambda arrays: tuple(pltpu.HBM(a.shape, a.dtype) for a in arrays)
    outs = pl.pallas_call(
        body, name=name,
        out_shape=(pltpu.SemaphoreType.DMA(()),) * (2 * n) + hbm(srcs) + hbm(lands) + (SDS((8, 128), F32),),
        in_specs=(HBM_SPEC,) * (2 * n),
        out_specs=(SEM_SPEC,) * (2 * n) + (HBM_SPEC,) * (2 * n) + (pl.BlockSpec(memory_space=pltpu.VMEM),),
        input_output_aliases={a: 2 * n + a for a in range(2 * n)},
        compiler_params=pltpu.CompilerParams(has_side_effects=SPLIT_EFFECT),
    )(*[pltpu.with_memory_space_constraint(a, pltpu.HBM) for a in list(srcs) + lands])
    return outs[:2 * n], outs[2 * n:3 * n], outs[3 * n:4 * n], outs[-1]


def _split_wait(flight, after, name):
    sems, srcs, lands, _ = flight
    n = len(srcs)

    def body(*refs):
        land_refs, sem_refs = refs[n:2 * n], refs[2 * n:4 * n]
        me = _position()
        for a in range(n):
            seven = land_refs[a].at[pl.ds(0, N_DEV - 1)]
            copies = pltpu.make_async_remote_copy(
                src_ref=seven, dst_ref=seven, send_sem=sem_refs[2 * a], recv_sem=sem_refs[2 * a + 1],
                device_id=_peer(1, *me), device_id_type=MESH)
            copies.wait_send()
            copies.wait_recv()

    outs = pl.pallas_call(
        body, name=name,
        out_shape=tuple(pltpu.HBM(a.shape, a.dtype) for a in list(srcs) + list(lands)),
        in_specs=(HBM_SPEC,) * (2 * n) + (SEM_SPEC,) * (2 * n) + (pl.BlockSpec(memory_space=pl.ANY),),
        out_specs=(HBM_SPEC,) * (2 * n), input_output_aliases={a: a for a in range(2 * n)},
        compiler_params=pltpu.CompilerParams(has_side_effects=SPLIT_EFFECT),
    )(*srcs, *lands, *sems, after)
    return outs[:n], outs[n:]


def _pack_rows(arrays, lanes, dtype):
    flat = jnp.concatenate([a.astype(dtype).reshape(-1) for a in arrays])
    pad = (-flat.size) % (16 * lanes)
    if pad:
        flat = jnp.concatenate([flat, jnp.zeros((pad,), dtype)])
    return flat.reshape(-1, lanes)


def _unpack_rows(packed, shapes):
    n = packed.shape[0]
    flat = packed.reshape(n, -1)
    out, off = [], 0
    for shp in shapes:
        size = math.prod(shp)
        out.append(flat[:, off:off + size].reshape((n,) + tuple(shp)))
        off += size
    return out


def _unshard(g8, axis):
    moved = jnp.moveaxis(g8, 0, axis)
    shp = list(moved.shape)
    shp[axis:axis + 2] = [shp[axis] * shp[axis + 1]]
    return moved.reshape(shp)


def _split8(full, axis):
    shp = list(full.shape)
    shp[axis:axis + 1] = [N_DEV, shp[axis] // N_DEV]
    return jnp.moveaxis(full.reshape(shp), axis, 0)


def _my_shard(g, axis, me):
    size = g.shape[axis + 1] // N_DEV
    return lax.dynamic_slice_in_dim(g, me * size, size, axis=axis + 1)


BIG_WEIGHTS = ("ffn_w_up", "ffn_w_down", "attn_w_qkv", "attn_w_o", "ret_w_in", "ret_w_out", "pool_w")
LAYER_WEIGHTS = (
    (("ffn_w_up", 0, "cols"), ("ffn_w_down", 0, "rows"), ("pool_w", 0, "pool")),
    (("ffn_w_up", 1, "cols"), ("ffn_w_down", 1, "rows"), ("attn_w_qkv", 0, "cols"), ("attn_w_o", 0, "rows")),
    (("ffn_w_up", 2, "cols"), ("ffn_w_down", 2, "rows"), ("ret_w_in", 0, "cols"), ("ret_w_out", 0, "rows")),
    (("ffn_w_up", 3, "cols"), ("ffn_w_down", 3, "rows"), ("pool_w", 1, "pool")),
)


GATHER_GROUPS = (LAYER_WEIGHTS[0][:2],) + LAYER_WEIGHTS[1:]
GRAD_GROUPS = {"3": LAYER_WEIGHTS[3], "2": LAYER_WEIGHTS[2], "1": LAYER_WEIGHTS[1],
               "0ffn": LAYER_WEIGHTS[0][:2], "0mix": LAYER_WEIGHTS[0][2:]}


def _shard_to_send(w, kind):
    w = w.astype(BF16)
    return w.T if kind == "cols" else w


def _full_from_land(land, kind):
    return _unshard(land, 1) if kind == "pool" else land.reshape(-1, land.shape[-1])


def _grad_to_send(g, kind):
    return _split8(g, 1).astype(BF16) if kind == "pool" else g.astype(BF16).reshape(N_DEV, -1, g.shape[-1])


def _shard_grad(gsum, kind):
    return gsum.T if kind == "cols" else gsum
SMALL_SHARDED = (("norm_w", 2), ("pool_b", 1), ("pool_scale", 1), ("ret_gn_w", 1), ("ffn_conv_w", 2))
REPLICATED = ("ada_b", "attn_q_gain", "attn_k_gain", "ret_decay_logit", "ffn_conv_b")
WEIGHT_ORDER = ("c_ctx", "ada_w", "ada_b", "norm_w", "pool_w", "pool_b", "pool_scale", "attn_w_qkv",
                "attn_q_gain", "attn_k_gain", "attn_w_o", "ret_w_in", "ret_decay_logit", "ret_gn_w",
                "ret_w_out", "ffn_w_up", "ffn_conv_w", "ffn_conv_b", "ffn_w_down")


def _local_step(x0, target, mods, P, get_weights, put_grads, s_len, l_len):
    n_lat = s_len // ROW_TILE
    nw = P["norm_w"]
    lgt = jnp.broadcast_to(P["ret_decay_logit"][0][:, :, None, None], (2, RET_HEADS, 1, 128))
    cos, sin = _rope_tables(s_len, l_len)
    h_dtype = [F32 if i % 3 == 0 else BF16 for i in range(DEPTH)]
    saved = []
    mods = list(mods)
    X = x0
    h = _res_norm(X, None, None, 0, nw[0, 0], mods[0], 0, h_dtype[0], n_lat, "norm_first")
    for i in range(DEPTH):
        kind, j, mod = i % 3, i // 3, mods[i]
        W, zero = get_weights(i, "mix", h)
        W, mod = dict(W), mod + zero
        sv = {"X": X, "h": h, "W": W}
        if kind == 0:
            y = _pool_fwd(h, W["pool_w"], P["pool_b"][j:j + 1], P["pool_scale"][j:j + 1],
                          n_lat, s_len, l_len, f"pool_fwd{i}")
        elif kind == 1:
            qkv = _mm(h, W["attn_w_qkv"], "nt", F32, f"qkv{i}")
            q, k, v = _qk_prep_fwd(qkv, P["attn_q_gain"][j:j + 1], P["attn_k_gain"][j:j + 1], cos, sin)
            o, lse = _flash_fwd(q, k, v, s_len, l_len)
            y = _mm(o, W["attn_w_o"], "nn", F32, f"attn_out{i}")
            sv.update(qkv=qkv, q=q, k=k, v=v, o=o, lse=lse)
        else:
            proj = _mm(h, W["ret_w_in"], "nt", F32, f"ret_in{i}")
            o2, states = _ret_fwd(proj, lgt, s_len, l_len)
            gated = _readout_fwd(o2, proj, P["ret_gn_w"][j:j + 1], n_lat)
            y = _mm(gated, W["ret_w_out"], "nn", F32, f"ret_out{i}")
            sv.update(proj=proj, o2=o2, states=states, gated=gated)
        X1, h2 = _res_norm(X, y, mod, 0, nw[i, 1], mod, 1, BF16, n_lat, f"res_norm_mid{i}")
        W_ffn, zero = get_weights(i, "ffn", h2)
        W.update(W_ffn)
        mod = mod + zero
        u = _mm(h2, W["ffn_w_up"], "nt", FFN_HIDDEN_DTYPE, f"ffn_up{i}")
        gact = _conv_gate_fwd(u, P["ffn_conv_w"][i], P["ffn_conv_b"][i:i + 1], n_lat, f"conv_gate_fwd{i}")
        f = _mm(gact, W["ffn_w_down"], "nn", F32, f"ffn_down{i}")
        sv.update(y=y, X1=X1, h2=h2, u=u, gact=gact, f=f)
        saved.append(sv)
        if i + 1 < DEPTH:
            X, h = _res_norm(X1, f, mod, 1, nw[i + 1, 0], mods[i + 1], 0, h_dtype[i + 1], n_lat,
                             f"res_norm_end{i}")
        else:
            X = _res_norm(X1, f, mod, 1, None, None, 0, None, n_lat, "res_last")

    dX, loss = _loss_bwd(X, target, n_lat)
    G = {name: [None] * P[name].shape[0] for name in
         ("pool_b", "pool_scale", "attn_q_gain", "attn_k_gain", "ret_decay_logit", "ret_gn_w", "ffn_conv_w",
          "ffn_conv_b")}
    dnw = [[None, None] for _ in range(DEPTH)]
    dmods = [None] * DEPTH
    for i in reversed(range(DEPTH)):
        kind, j, mod, sv = i % 3, i // 3, mods[i], saved[i]
        W, gl = sv["W"], {}
        if i == DEPTH - 1:
            df, dg2 = _gate_bwd(dX, sv["f"], mod, 1, BF16, n_lat, f"gate_bwd_ffn{i}")
        dgact = _mm(df, W["ffn_w_down"], "nt", FFN_HIDDEN_DTYPE, f"ffn_down_dx{i}")
        gl["ffn_w_down"] = _mm(sv["gact"], df, "tn", BF16, f"ffn_down_dw{i}")
        du, dcw, dcb = _conv_gate_bwd(sv["u"], dgact, P["ffn_conv_w"][i], P["ffn_conv_b"][i:i + 1], n_lat,
                                      f"conv_gate_bwd{i}")
        G["ffn_conv_w"][i], G["ffn_conv_b"][i] = dcw, dcb[0]
        dh2 = _mm(du, W["ffn_w_up"], "nn", F32, f"ffn_up_dx{i}")
        gl["ffn_w_up"] = _mm(du, sv["h2"], "tn", BF16, f"ffn_up_dw{i}")
        if i == 0:
            mod = mod + put_grads("0ffn", gl)
        dX1, dnw[i][1], dsh2, dsc2, dy, dg1 = _norm_bwd(
            dX, dh2, sv["X1"], nw[i, 1], mod, 1, n_lat, f"norm_bwd_ffn{i}",
            gated=(sv["y"], mod, 0, F32 if kind == 0 else BF16))
        h = sv["h"]
        if kind == 0:
            dh, dpw, dpb, dps = _pool_bwd(h, dy, W["pool_w"], P["pool_b"][j:j + 1], P["pool_scale"][j:j + 1],
                                          n_lat, s_len, l_len, f"pool_bwd{i}")
            gl["pool_w"], G["pool_b"][j], G["pool_scale"][j] = dpw, dpb[0], dps[0]
        elif kind == 1:
            do = _mm(dy, W["attn_w_o"], "nt", F32, f"attn_out_dx{i}")
            gl["attn_w_o"] = _mm(sv["o"], dy, "tn", BF16, f"attn_out_dw{i}")
            dq, dk, dv = _flash_bwd(sv["q"], sv["k"], sv["v"], sv["o"], sv["lse"], do, s_len, l_len)
            dqkv, dqg, dkg = _qk_prep_bwd(sv["qkv"], dq, dk, dv, P["attn_q_gain"][j:j + 1],
                                          P["attn_k_gain"][j:j + 1], cos, sin)
            G["attn_q_gain"][j], G["attn_k_gain"][j] = dqg[0], dkg[0]
            dh = _mm(dqkv, W["attn_w_qkv"], "nn", F32, f"qkv_dx{i}")
            gl["attn_w_qkv"] = _mm(dqkv, h, "tn", BF16, f"qkv_dw{i}")
        else:
            dgated = _mm(dy, W["ret_w_out"], "nt", F32, f"ret_out_dx{i}")
            gl["ret_w_out"] = _mm(sv["gated"], dy, "tn", BF16, f"ret_out_dw{i}")
            do, dg, dgn = _readout_bwd(sv["o2"], sv["proj"], P["ret_gn_w"][j:j + 1], dgated, n_lat)
            dq2, dk2, dv2, dlg = _ret_bwd(sv["proj"], lgt, sv["states"], do, s_len, l_len)
            dproj = _ret_dproj(dq2, dk2, dv2, dg, n_lat)
            G["ret_gn_w"][j], G["ret_decay_logit"][j] = dgn[0], dlg[:, :, 0, 0]
            dh = _mm(dproj, W["ret_w_in"], "nn", F32, f"ret_in_dx{i}")
            gl["ret_w_in"] = _mm(dproj, h, "tn", BF16, f"ret_in_dw{i}")
        zero = put_grads(str(i) if i > 0 else "0mix", gl)
        if i > 0:
            mods[i - 1] = mods[i - 1] + zero
            dX, dnw[i][0], dsh1, dsc1, df_below, dg2_below = _norm_bwd(
                dX1, dh, sv["X"], nw[i, 0], mod, 0, n_lat, f"norm_bwd_mix{i}",
                gated=(saved[i - 1]["f"], mods[i - 1], 1, BF16))
        else:
            dX, dnw[i][0], dsh1, dsc1 = _norm_bwd(dX1, dh, sv["X"], nw[i, 0], mod, 0, n_lat, f"norm_bwd_mix{i}")
        dmods[i] = jnp.concatenate([dsh1, dsc1, dg1, dsh2, dsc2, dg2], axis=1)
        if i > 0:
            df, dg2 = df_below, dg2_below
    grads = {name: jnp.stack(parts) for name, parts in G.items()}
    grads["norm_w"] = jnp.stack([jnp.concatenate(pair, axis=0) for pair in dnw])
    return loss, dX, grads, jnp.stack(dmods)


def kernel(x, c, ctx, c_ctx, ada_w, ada_b, norm_w, pool_w, pool_b, pool_scale, attn_w_qkv, attn_q_gain,
           attn_k_gain, attn_w_o, ret_w_in, ret_decay_logit, ret_gn_w, ret_w_out, ffn_w_up, ffn_conv_w,
           ffn_conv_b, ffn_w_down, loss_target, m_c_ctx, m_ada_w, m_ada_b, m_norm_w, m_pool_w, m_pool_b,
           m_pool_scale, m_attn_w_qkv, m_attn_q_gain, m_attn_k_gain, m_attn_w_o, m_ret_w_in,
           m_ret_decay_logit, m_ret_gn_w, m_ret_w_out, m_ffn_w_up, m_ffn_conv_w, m_ffn_conv_b, m_ffn_w_down,
           v_c_ctx, v_ada_w, v_ada_b, v_norm_w, v_pool_w, v_pool_b, v_pool_scale, v_attn_w_qkv, v_attn_q_gain,
           v_attn_k_gain, v_attn_w_o, v_ret_w_in, v_ret_decay_logit, v_ret_gn_w, v_ret_w_out, v_ffn_w_up,
           v_ffn_conv_w, v_ffn_conv_b, v_ffn_w_down):
    A = dict(locals())
    me = _index(_position())
    s_len, D = x.shape[1], x.shape[2]
    l_len = ctx.shape[1]
    assert s_len % ROW_TILE == 0 and l_len % ROW_TILE == 0 and s_len % GRID_W == 0

    small = [A[n] for n, _ in SMALL_SHARDED]
    first_parts = [c] + small + [pool_w[0]]
    got = _gather_small(_pack_rows(first_parts, 128, F32), "gather_c_small")
    parts = _unpack_rows(got, [a.shape for a in first_parts])
    c_all = parts[0].reshape(N_DEV, D)
    P = {n: _unshard(g8, ax) for (n, ax), g8 in zip(SMALL_SHARDED, parts[1:-1])}
    first_pool_w = _unshard(parts[-1], 1).astype(BF16)

    c_rows = jnp.concatenate([c_all, c_ctx.reshape(1, D), jnp.zeros((7, D), F32)], axis=0)
    cols = ada_w.shape[2]
    ada_b_shard = lax.dynamic_slice_in_dim(ada_b, me * cols, cols, axis=1).reshape(DEPTH, 1, cols)
    mod_shard = _ada_fwd(c_rows, ada_w, ada_b_shard)
    got = _gather_small(mod_shard.reshape(-1, 128), "gather_mod").reshape(N_DEV, DEPTH, 16, cols)
    mod_lat = lax.dynamic_index_in_dim(got, me, axis=2, keepdims=False)
    mod_ctx = got[:, :, 8, :]
    mods = jnp.stack([jnp.moveaxis(mod_lat, 0, 1).reshape(DEPTH, 6, D),
                      jnp.moveaxis(mod_ctx, 0, 1).reshape(DEPTH, 6, D)], axis=1)

    shards = [[_shard_to_send(A[n][j], kind) for n, j, kind in group] for group in GATHER_GROUPS]
    flights, landed = {}, {}

    def start_gather(i, ready):
        mine, _ = lax.optimization_barrier((shards[i], ready))
        flights[i] = _split_start(mine, True, f"gather_start{i}")
        return flights[i][3][0, 0]

    mods = [mods[i] for i in range(DEPTH)]
    mods[0] = mods[0] + start_gather(0, mods[0])
    for n in REPLICATED:
        P[n] = A[n]

    def get_weights(i, part, x_now):
        if i == 0 and part == "mix":
            return {"pool_w": first_pool_w}, 0.0
        if i in landed:
            return landed[i], 0.0
        owns, lands = _split_wait(flights[i], x_now, f"gather_wait{i}")
        landed[i] = {n: _full_from_land(lax.dynamic_update_index_in_dim(land, own, me, axis=0), kind)
                     for (n, j, kind), own, land in zip(GATHER_GROUPS[i], owns, lands)}
        return landed[i], (start_gather(i + 1, lands) if i + 1 < DEPTH else 0.0)

    sent = {}

    def put_grads(group, gl):
        sent[group] = _split_start([_grad_to_send(gl[n], kind) for n, j, kind in GRAD_GROUPS[group]], False,
                                   f"exchange_start_{group}")
        return sent[group][3][0, 0]

    x0 = jnp.concatenate([x[0], ctx[0]], axis=0)
    loss8, dx0, G, dmods = _local_step(x0, loss_target[0], mods, P, get_weights, put_grads, s_len, l_len)
    loss = lax.psum(loss8[0, 0], ("x", "y", "c"))
    grad_x = dx0[:s_len].reshape(x.shape)

    small_names = ["dmods"] + list(REPLICATED[1:]) + [n for n, _ in SMALL_SHARDED]
    small_parts = [dmods] + [G[n] for n in small_names[1:]]
    got = _gather_small(_pack_rows(small_parts, 128, F32), "gather_small_grads")
    S8 = dict(zip(small_names, _unpack_rows(got, [a.shape for a in small_parts])))

    dm = S8["dmods"].reshape(N_DEV, DEPTH, 2, 6 * D)
    dm_mine = lax.dynamic_slice_in_dim(dm, me * cols, cols, axis=3)
    g_ada_w, pc = _ada_bwd(c_rows, ada_w, jnp.moveaxis(dm_mine[:, :, 0], 0, 1), jnp.moveaxis(dm_mine[:, :, 1], 0, 1))
    pc8 = _gather_small(pc.reshape(-1, 128), "gather_c_ctx_grad").reshape(N_DEV, 8, D)

    def owner_sums(group, after):
        sends, lands = _split_wait(sent[group], after, f"exchange_wait_{group}")
        out = {}
        for (n, j, kind), send, land in zip(GRAD_GROUPS[group], sends, lands):
            own = lax.dynamic_index_in_dim(send, me, axis=0, keepdims=False)
            out[(n, j)] = _shard_grad(_sum_slots(own, land, f"sum_slots_{n}{j}"), kind)
        return out

    shard_grads = {}
    for group in ("3", "2", "1", "0mix"):
        shard_grads.update(owner_sums(group, pc8))

    g_in = {"c_ctx": pc8[:, 0, :], "ada_w": g_ada_w[None],
            "ada_b": jnp.moveaxis(dm, 2, 1).reshape(2 * N_DEV, DEPTH, 6 * D)}
    for n in REPLICATED[1:]:
        g_in[n] = S8[n]
    for n, ax in SMALL_SHARDED:
        g_in[n] = _my_shard(S8[n], ax, me)

    def stacked(n):
        return jnp.stack([shard_grads[(n, j)] for j in range(A[n].shape[0])])[None]

    late = [n for n, j, kind in GRAD_GROUPS["0ffn"]]
    for n in BIG_WEIGHTS:
        if n not in late:
            g_in[n] = stacked(n)
    res = {n: _adamw(A[n], g_in[n], A["m_" + n], A["v_" + n], "adamw_" + n) for n in WEIGHT_ORDER if n not in late}
    done = sum(res[n][1].reshape(-1)[0] for n in res)
    shard_grads.update(owner_sums("0ffn", done.reshape(1, 1)))
    for n in late:
        res[n] = _adamw(A[n], stacked(n), A["m_" + n], A["v_" + n], "adamw_" + n)
    outs = [loss, grad_x]
    for slot in range(4):
        outs += [res[n][slot] for n in WEIGHT_ORDER]
    return tuple(outs)
```

```python
import functools
import math

import jax
import jax.numpy as jnp
from jax import lax
from jax.experimental import pallas as pl
from jax.experimental.pallas import tpu as pltpu

F32 = jnp.float32
BF16 = jnp.bfloat16
SDS = jax.ShapeDtypeStruct
MESH = pl.DeviceIdType.MESH

N_DEV = 8
EPS = 1e-6
DEPTH = 4
GRID_W = 64
POOL_WINDOWS = (2, 4, 8, 16)
N_HEADS = 8
N_KV = 2
HEAD_DIM = 128
ROPE_THETA = 10000.0
RET_HEADS = 4
RET_DK = 256
RET_DV = 512
RET_CHUNK = 128
ADAM_LR = 0.001
ADAM_B1 = 0.9
ADAM_B2 = 0.999
ADAM_EPS = 1e-08
ADAM_WD = 0.01
ADAM_STEP = 10

ROW_TILE = 256
FFN_HIDDEN_DTYPE = BF16
FLASH_FWD_TILE = 128
HALO = 8
VMEM_LIMIT_V7X = 56 * 1024 * 1024


def _params(n_axes=0):
    sem = ("arbitrary",) * n_axes if n_axes else None
    return pltpu.CompilerParams(dimension_semantics=sem, vmem_limit_bytes=VMEM_LIMIT_V7X)


def _pick(n, cap, mult):
    best = None
    for d in range(mult, min(n, cap) + 1, mult):
        if n % d == 0:
            best = d
    return best if best is not None else n


def _dot(a, b):
    return jnp.dot(a, b, preferred_element_type=F32)


def _dot_nt(a, b):
    return lax.dot_general(a, b, (((1,), (1,)), ((), ())), preferred_element_type=F32)


def _dot_tn(a, b):
    return lax.dot_general(a, b, (((0,), (0,)), ((), ())), preferred_element_type=F32)


def _bf(v):
    return v.astype(BF16)


def _sigmoid(v):
    return 0.5 * jnp.tanh(0.5 * v) + 0.5


MM_VMEM_BUDGET = 40 * 1024 * 1024
MM_STEP_BYTES = 1 << 20
MM_ACC_PASS_BYTES = 8


def _divisors(n, mult, cap):
    return [d for d in range(mult, min(n, cap) + 1, mult) if n % d == 0] or [n]


def _mm_tiles(mode, M, N, K, a_item, b_item, o_item):
    best = None
    for tm in _divisors(M, 128 if mode == "tn" else 16, 2816):
        for tn in _divisors(N, 128, 2048):
            for tk in _divisors(K, 16 if mode == "tn" else 128, 2816):
                ni, nj, nk = M // tm, N // tn, K // tk
                vmem = 2 * (tm * tk * a_item + tk * tn * b_item + tm * tn * o_item) + tm * tn * 4
                if vmem > MM_VMEM_BUDGET:
                    continue
                a_reads = 1 if nk == 1 else nj
                b_reads = 1 if (nk == 1 and nj == 1) else ni
                cost = (M * K * a_item * a_reads + K * N * b_item * b_reads + M * N * o_item
                        + ni * nj * nk * MM_STEP_BYTES + (nk - 1) * M * N * MM_ACC_PASS_BYTES)
                if best is None or cost < best[0]:
                    best = (cost, tm, tn, tk)
    return best[1:]


def _mm(a, b, mode, out_dtype, name):
    if mode == "nn":
        (M, K), (K2, N) = a.shape, b.shape
    elif mode == "nt":
        (M, K), (N, K2) = a.shape, b.shape
    else:
        (K, M), (K2, N) = a.shape, b.shape
    assert K == K2, (a.shape, b.shape, mode)
    tm, tn, tk = _mm_tiles(mode, M, N, K, a.dtype.itemsize, b.dtype.itemsize, jnp.dtype(out_dtype).itemsize)
    nk = K // tk
    if mode == "nn":
        a_spec = pl.BlockSpec((tm, tk), lambda i, j, k: (i, k))
        b_spec = pl.BlockSpec((tk, tn), lambda i, j, k: (k, j))
    elif mode == "nt":
        a_spec = pl.BlockSpec((tm, tk), lambda i, j, k: (i, k))
        b_spec = pl.BlockSpec((tn, tk), lambda i, j, k: (j, k))
    else:
        a_spec = pl.BlockSpec((tk, tm), lambda i, j, k: (k, i))
        b_spec = pl.BlockSpec((tk, tn), lambda i, j, k: (k, j))
    dot = {"nn": _dot, "nt": _dot_nt, "tn": _dot_tn}[mode]

    def body(a_ref, b_ref, o_ref, acc_ref):
        part = dot(_bf(a_ref[...]), _bf(b_ref[...]))
        if nk == 1:
            o_ref[...] = part.astype(out_dtype)
        else:
            k = pl.program_id(2)

            @pl.when(k == 0)
            def _():
                acc_ref[...] = part

            @pl.when(k > 0)
            def _():
                acc_ref[...] += part

            @pl.when(k == nk - 1)
            def _():
                o_ref[...] = acc_ref[...].astype(out_dtype)

    return pl.pallas_call(
        body, name=name, grid=(M // tm, N // tn, nk),
        in_specs=[a_spec, b_spec],
        out_specs=pl.BlockSpec((tm, tn), lambda i, j, k: (i, j)),
        out_shape=SDS((M, N), out_dtype),
        scratch_shapes=[pltpu.VMEM((tm, tn), F32)],
        compiler_params=_params(3),
    )(a, b)


def _seg_spec(n_lat, d):
    return pl.BlockSpec((1, 6, d), lambda i: ((i >= n_lat).astype(jnp.int32), 0, 0))


def _seg_acc_spec(n_lat, d):
    return pl.BlockSpec((1, 1, d), lambda i: ((i >= n_lat).astype(jnp.int32), 0, 0))


def _res_norm(x, y, gmod, gk, nw, nmod, nk, h_dtype, n_lat, name):
    R, D = x.shape
    has_res, has_norm = y is not None, nw is not None
    row = pl.BlockSpec((ROW_TILE, D), lambda i: (i, 0))
    vec = pl.BlockSpec((1, D), lambda i: (0, 0))
    ins, specs, outs, ospecs = [x], [row], [], []
    if has_res:
        ins += [y, gmod]
        specs += [row, _seg_spec(n_lat, D)]
        outs.append(SDS((R, D), F32))
        ospecs.append(row)
    if has_norm:
        ins += [nw.reshape(1, D), nmod]
        specs += [vec, _seg_spec(n_lat, D)]
        outs.append(SDS((R, D), h_dtype))
        ospecs.append(row)

    def body(*refs):
        refs = list(refs)
        z = refs.pop(0)[...]
        if has_res:
            y_ref, g_ref = refs.pop(0), refs.pop(0)
            z = z + g_ref[0, pl.ds(3 * gk + 2, 1), :] * y_ref[...].astype(F32)
        if has_norm:
            nw_ref, m_ref = refs.pop(0), refs.pop(0)
        if has_res:
            refs.pop(0)[...] = z
        if has_norm:
            r = lax.rsqrt(jnp.mean(z * z, axis=-1, keepdims=True) + EPS)
            h = (z * r) * nw_ref[...]
            h = h * (1.0 + m_ref[0, pl.ds(3 * nk + 1, 1), :]) + m_ref[0, pl.ds(3 * nk, 1), :]
            refs.pop(0)[...] = h.astype(h_dtype)

    res = pl.pallas_call(
        body, name=name, grid=(R // ROW_TILE,), in_specs=specs, out_specs=ospecs,
        out_shape=outs, compiler_params=_params(1),
    )(*ins)
    return res if len(res) > 1 else res[0]


def _gate_bwd(dz, y, mod, k, out_dtype, n_lat, name):
    R, D = dz.shape
    row = pl.BlockSpec((ROW_TILE, D), lambda i: (i, 0))

    def body(dz_ref, y_ref, m_ref, dy_ref, dg_ref):
        i = pl.program_id(0)
        dzv = dz_ref[...]
        dy_ref[...] = (m_ref[0, pl.ds(3 * k + 2, 1), :] * dzv).astype(out_dtype)

        @pl.when((i == 0) | (i == n_lat))
        def _():
            dg_ref[...] = jnp.zeros_like(dg_ref)

        dg_ref[0] += jnp.sum(dzv * y_ref[...].astype(F32), axis=0, keepdims=True)

    return pl.pallas_call(
        body, name=name, grid=(R // ROW_TILE,),
        in_specs=[row, row, _seg_spec(n_lat, D)],
        out_specs=[row, _seg_acc_spec(n_lat, D)],
        out_shape=[SDS((R, D), out_dtype), SDS((2, 1, D), F32)],
        compiler_params=_params(1),
    )(dz, y, mod)


def _norm_bwd(dz, dh, x, nw, mod, k, n_lat, name, gated=None):
    R, D = x.shape
    row = pl.BlockSpec((ROW_TILE, D), lambda i: (i, 0))
    vec = pl.BlockSpec((1, D), lambda i: (0, 0))
    ins, specs = [dz, dh, x, nw.reshape(1, D), mod], [row, row, row, vec, _seg_spec(n_lat, D)]
    outs = [SDS((R, D), F32), SDS((1, D), F32), SDS((2, 1, D), F32), SDS((2, 1, D), F32)]
    ospecs = [row, vec, _seg_acc_spec(n_lat, D), _seg_acc_spec(n_lat, D)]
    if gated is not None:
        y, gmod, gk, dy_dtype = gated
        ins += [y, gmod]
        specs += [row, _seg_spec(n_lat, D)]
        outs += [SDS((R, D), dy_dtype), SDS((2, 1, D), F32)]
        ospecs += [row, _seg_acc_spec(n_lat, D)]

    def body(dz_ref, dh_ref, x_ref, nw_ref, m_ref, *rest):
        if gated is not None:
            y_ref, g_ref, dx_ref, dnw_ref, dsh_ref, dsc_ref, dy_ref, dg_ref = rest
        else:
            dx_ref, dnw_ref, dsh_ref, dsc_ref = rest
        i = pl.program_id(0)
        xv = x_ref[...]
        dhv = dh_ref[...].astype(F32)
        nwv = nw_ref[...]
        sc1 = 1.0 + m_ref[0, pl.ds(3 * k + 1, 1), :]
        r = lax.rsqrt(jnp.mean(xv * xv, axis=-1, keepdims=True) + EPS)
        xhat = xv * r
        a = dhv * (nwv * sc1)
        dx = dz_ref[...] + r * (a - xhat * jnp.mean(a * xhat, axis=-1, keepdims=True))
        dx_ref[...] = dx

        @pl.when(i == 0)
        def _():
            dnw_ref[...] = jnp.zeros_like(dnw_ref)

        @pl.when((i == 0) | (i == n_lat))
        def _():
            dsh_ref[...] = jnp.zeros_like(dsh_ref)
            dsc_ref[...] = jnp.zeros_like(dsc_ref)
            if gated is not None:
                dg_ref[...] = jnp.zeros_like(dg_ref)

        dnw_ref[...] += jnp.sum(dhv * xhat, axis=0, keepdims=True) * sc1
        dsh_ref[0] += jnp.sum(dhv, axis=0, keepdims=True)
        dsc_ref[0] += jnp.sum(dhv * xhat, axis=0, keepdims=True) * nwv
        if gated is not None:
            dy_ref[...] = (g_ref[0, pl.ds(3 * gk + 2, 1), :] * dx).astype(dy_dtype)
            dg_ref[0] += jnp.sum(dx * y_ref[...].astype(F32), axis=0, keepdims=True)

    return pl.pallas_call(
        body, name=name, grid=(R // ROW_TILE,), in_specs=specs, out_specs=ospecs, out_shape=outs,
        compiler_params=_params(1),
    )(*ins)


def _loss_bwd(xf, target, n_lat):
    R, D = xf.shape
    row = pl.BlockSpec((ROW_TILE, D), lambda i: (i, 0))
    tgt = pl.BlockSpec((ROW_TILE, D), lambda i: (jnp.minimum(i, n_lat - 1), 0))

    def body(x_ref, t_ref, dx_ref, loss_ref):
        i = pl.program_id(0)
        e = jnp.where(i < n_lat, x_ref[...] - t_ref[...], 0.0)
        dx_ref[...] = e * (1.0 / D)

        @pl.when(i == 0)
        def _():
            loss_ref[...] = jnp.zeros_like(loss_ref)

        loss_ref[...] += 0.5 * jnp.sum(jnp.mean(e * e, axis=-1, keepdims=True))

    return pl.pallas_call(
        body, name="loss_bwd", grid=(R // ROW_TILE,),
        in_specs=[row, tgt],
        out_specs=[row, pl.BlockSpec((8, 128), lambda i: (0, 0))],
        out_shape=[SDS((R, D), F32), SDS((8, 128), F32)],
        compiler_params=_params(1),
    )(xf, target)


def _halo_rows(dtype):
    return HALO * (4 // jnp.dtype(dtype).itemsize)


def _halo_specs(n_tiles, width, tile=ROW_TILE, rows=HALO):
    per = tile // rows
    prev = pl.BlockSpec((rows, width), lambda i: (jnp.maximum(i * per - 1, 0), 0))
    nxt = pl.BlockSpec((rows, width), lambda i: (jnp.minimum((i + 1) * per, n_tiles * per - 1), 0))
    return prev, nxt


SHIFT_K = 256


def _shift_matrix(n_out, first_row, deltas):
    half = n_out // 2
    out = []
    for h, start in enumerate((0, 2 * _halo_rows(BF16))):
        r = lax.broadcasted_iota(jnp.int32, (half, SHIFT_K), 0) + (first_row + h * half - start)
        j = lax.broadcasted_iota(jnp.int32, (half, SHIFT_K), 1)
        out.append(jnp.concatenate([(j == r + d).astype(F32) for d in deltas], axis=0).astype(BF16))
    return out


def _shifted_rows(t_ref, p_ref, n_ref, cols, first, last, picks, n_blocks):
    pr = jnp.where(first, jnp.zeros_like(p_ref[:, cols]), p_ref[:, cols])
    nx = jnp.where(last, jnp.zeros_like(n_ref[:, cols]), n_ref[:, cols])
    e = jnp.concatenate([pr, t_ref[:, cols], nx], axis=0)
    start = 2 * pr.shape[0]
    top, bot = _dot(picks[0], e[0:SHIFT_K]), _dot(picks[1], e[start:start + SHIFT_K])
    half = picks[0].shape[0] // n_blocks
    return [jnp.concatenate([top[k * half:(k + 1) * half], bot[k * half:(k + 1) * half]], axis=0)
            for k in range(n_blocks)]


def _edge_flags(i, n_lat, n_tiles):
    first = (i == 0) | (i == n_lat)
    last = (i == n_lat - 1) | (i == n_tiles - 1)
    return first, last


def _conv_gate_fwd(u, conv_w, conv_b, n_lat, name):
    R, F2 = u.shape
    F = F2 // 2
    n_tiles = R // ROW_TILE
    T = ROW_TILE
    cw = _pick(F, 256, 128)
    row = pl.BlockSpec((T, F2), lambda i: (i, 0))
    prev, nxt = _halo_specs(n_tiles, F2, rows=_halo_rows(u.dtype))

    assert u.dtype == BF16

    def body(u_ref, p_ref, n_ref, w_ref, b_ref, o_ref):
        i = pl.program_id(0)
        first, last = _edge_flags(i, n_lat, n_tiles)
        taps = _shift_matrix(T, _halo_rows(BF16), (-1, 0, 1))

        def conv(c0):
            cols = pl.ds(c0, cw)
            up, uv, un = _shifted_rows(u_ref, p_ref, n_ref, cols, first, last, taps, 3)
            return (up * w_ref[pl.ds(0, 1), cols] + uv * w_ref[pl.ds(1, 1), cols]
                    + un * w_ref[pl.ds(2, 1), cols] + b_ref[:, cols])

        for c0 in range(0, F, cw):
            ca, cv = conv(c0), conv(F + c0)
            o_ref[:, pl.ds(c0, cw)] = (ca * _sigmoid(ca) * cv).astype(BF16)

    return pl.pallas_call(
        body, name=name, grid=(n_tiles,),
        in_specs=[row, prev, nxt, pl.BlockSpec((3, F2), lambda i: (0, 0)),
                  pl.BlockSpec((1, F2), lambda i: (0, 0))],
        out_specs=pl.BlockSpec((T, F), lambda i: (i, 0)),
        out_shape=SDS((R, F), BF16), compiler_params=_params(1),
    )(u, u, u, conv_w, conv_b)


def _conv_gate_bwd(u, dgact, conv_w, conv_b, n_lat, name):
    R, F2 = u.shape
    F = F2 // 2
    n_tiles = R // ROW_TILE
    T, N = ROW_TILE, ROW_TILE + 2 * HALO
    cw = _pick(F, 256, 128)
    rowu = pl.BlockSpec((T, F2), lambda i: (i, 0))
    rowg = pl.BlockSpec((T, F), lambda i: (i, 0))
    pu, nu = _halo_specs(n_tiles, F2, rows=_halo_rows(u.dtype))
    pg, ng = _halo_specs(n_tiles, F, rows=_halo_rows(dgact.dtype))

    assert u.dtype == BF16 and dgact.dtype == BF16

    def body(u_ref, pu_ref, nu_ref, g_ref, pg_ref, ng_ref, w_ref, b_ref, du_ref, dw_ref, db_ref):
        i = pl.program_id(0)
        first, last = _edge_flags(i, n_lat, n_tiles)

        @pl.when(i == 0)
        def _():
            dw_ref[...] = jnp.zeros_like(dw_ref)
            db_ref[...] = jnp.zeros_like(db_ref)

        taps = _shift_matrix(N, _halo_rows(BF16) - HALO, (-1, 0, 1))
        same = _shift_matrix(N, _halo_rows(BF16) - HALO, (0,))

        def conv(c0):
            cols = pl.ds(c0, cw)
            up, e, un = _shifted_rows(u_ref, pu_ref, nu_ref, cols, first, last, taps, 3)
            c = (up * w_ref[pl.ds(0, 1), cols] + e * w_ref[pl.ds(1, 1), cols]
                 + un * w_ref[pl.ds(2, 1), cols] + b_ref[:, cols])
            return c, up, e, un

        def back(c0, dc, up, e, un):
            cols = pl.ds(c0, cw)
            du = (pltpu.roll(dc, N - 1, 0) * w_ref[pl.ds(0, 1), cols] + dc * w_ref[pl.ds(1, 1), cols]
                  + pltpu.roll(dc, 1, 0) * w_ref[pl.ds(2, 1), cols])
            du_ref[:, cols] = du[HALO:HALO + T].astype(BF16)
            dct = dc[HALO:HALO + T]
            dw_ref[pl.ds(0, 1), cols] += jnp.sum(dct * up[HALO:HALO + T], axis=0, keepdims=True)
            dw_ref[pl.ds(1, 1), cols] += jnp.sum(dct * e[HALO:HALO + T], axis=0, keepdims=True)
            dw_ref[pl.ds(2, 1), cols] += jnp.sum(dct * un[HALO:HALO + T], axis=0, keepdims=True)
            db_ref[:, cols] += jnp.sum(dct, axis=0, keepdims=True)

        for c0 in range(0, F, cw):
            dg, = _shifted_rows(g_ref, pg_ref, ng_ref, pl.ds(c0, cw), first, last, same, 1)
            ca, upa, ea, una = conv(c0)
            cv, upv, ev, unv = conv(F + c0)
            s = _sigmoid(ca)
            back(F + c0, dg * (ca * s), upv, ev, unv)
            back(c0, dg * cv * (s * (1.0 + ca * (1.0 - s))), upa, ea, una)

    return pl.pallas_call(
        body, name=name, grid=(n_tiles,),
        in_specs=[rowu, pu, nu, rowg, pg, ng, pl.BlockSpec((3, F2), lambda i: (0, 0)),
                  pl.BlockSpec((1, F2), lambda i: (0, 0))],
        out_specs=[rowu, pl.BlockSpec((3, F2), lambda i: (0, 0)), pl.BlockSpec((1, F2), lambda i: (0, 0))],
        out_shape=[SDS((R, F2), BF16), SDS((3, F2), F32), SDS((1, F2), F32)],
        compiler_params=_params(1),
    )(u, u, u, dgact, dgact, dgact, conv_w, conv_b)


def _pool_counts(i, n_lat, s_len, l_len, n_rows, offset):
    ctx = i >= n_lat
    t0 = jnp.where(ctx, i - n_lat, i) * ROW_TILE + offset
    seg = jnp.where(ctx, l_len, s_len)
    t = t0 + lax.broadcasted_iota(jnp.int32, (n_rows, 1), 0)
    out = []
    for win in POOL_WINDOWS:
        cnt = jnp.minimum(t + win // 2, seg) - jnp.maximum(t - win // 2, 0)
        out.append(jnp.maximum(cnt, 1).astype(F32))
    return out


def _window_sum(e, lo, hi, n):
    acc = None
    for j in range(lo, hi + 1):
        term = e if j == 0 else pltpu.roll(e, (-j) % n, 0)
        acc = term if acc is None else acc + term
    return acc


def _pool_fwd(h, w, b, scale, n_lat, s_len, l_len, name):
    R, D = h.shape
    G = D // 4
    n_tiles = R // ROW_TILE
    T, N = ROW_TILE, ROW_TILE + 2 * HALO
    row = pl.BlockSpec((T, D), lambda i: (i, 0))
    prev, nxt = _halo_specs(n_tiles, D)
    vec = pl.BlockSpec((1, D), lambda i: (0, 0))

    def body(h_ref, p_ref, n_ref, w_ref, b_ref, s_ref, y_ref):
        i = pl.program_id(0)
        first, last = _edge_flags(i, n_lat, n_tiles)
        cnts = _pool_counts(i, n_lat, s_len, l_len, T, 0)
        for g, win in enumerate(POOL_WINDOWS):
            cols = pl.ds(g * G, G)
            pr = jnp.where(first, 0.0, p_ref[:, cols])
            nx = jnp.where(last, 0.0, n_ref[:, cols])
            hv = h_ref[:, cols]
            e = jnp.concatenate([pr, hv, nx], axis=0)
            mean = _window_sum(e, -(win // 2), win // 2 - 1, N)[HALO:HALO + T] / cnts[g]
            yg = _dot(_bf(mean - hv), w_ref[g])
            y_ref[:, cols] = (yg + b_ref[:, cols]) * s_ref[:, cols]

    return pl.pallas_call(
        body, name=name, grid=(n_tiles,),
        in_specs=[row, prev, nxt, pl.BlockSpec((4, G, G), lambda i: (0, 0, 0)), vec, vec],
        out_specs=row, out_shape=SDS((R, D), F32), compiler_params=_params(1),
    )(h, h, h, w, b, scale)


def _pool_bwd(h, dy, w, b, scale, n_lat, s_len, l_len, name):
    R, D = h.shape
    G = D // 4
    n_tiles = R // ROW_TILE
    T, N = ROW_TILE, ROW_TILE + 2 * HALO
    row = pl.BlockSpec((T, D), lambda i: (i, 0))
    prev, nxt = _halo_specs(n_tiles, D)
    vec = pl.BlockSpec((1, D), lambda i: (0, 0))
    wspec = pl.BlockSpec((4, G, G), lambda i: (0, 0, 0))

    def body(h_ref, ph_ref, nh_ref, d_ref, pd_ref, nd_ref, w_ref, b_ref, s_ref,
             dh_ref, dw_ref, db_ref, ds_ref):
        i = pl.program_id(0)
        first, last = _edge_flags(i, n_lat, n_tiles)

        @pl.when(i == 0)
        def _():
            dw_ref[...] = jnp.zeros_like(dw_ref)
            db_ref[...] = jnp.zeros_like(db_ref)
            ds_ref[...] = jnp.zeros_like(ds_ref)

        cnts = _pool_counts(i, n_lat, s_len, l_len, T, 0)
        cnts_ext = _pool_counts(i, n_lat, s_len, l_len, N, -HALO)
        for g, win in enumerate(POOL_WINDOWS):
            cols = pl.ds(g * G, G)

            def ext(t_ref, p_ref, n_ref):
                pr = jnp.where(first, 0.0, p_ref[:, cols])
                nx = jnp.where(last, 0.0, n_ref[:, cols])
                return jnp.concatenate([pr, t_ref[:, cols], nx], axis=0)

            hv = h_ref[:, cols]
            mean = _window_sum(ext(h_ref, ph_ref, nh_ref), -(win // 2), win // 2 - 1, N)[HALO:HALO + T] / cnts[g]
            z = _bf(mean - hv)
            sc = s_ref[:, cols]
            dye = ext(d_ref, pd_ref, nd_ref)
            dt = _bf(dye * sc)
            dz = _dot_nt(dt, w_ref[g])
            dm = dz / cnts_ext[g]
            dh = _window_sum(dm, -(win // 2 - 1), win // 2, N) - dz
            dh_ref[:, cols] = dh[HALO:HALO + T]
            dyt = dye[HALO:HALO + T]
            dw_ref[g] += _dot_tn(z, dt[HALO:HALO + T])
            db_ref[:, cols] += jnp.sum(dyt * sc, axis=0, keepdims=True)
            ds_ref[:, cols] += jnp.sum(dyt * (_dot(z, w_ref[g]) + b_ref[:, cols]), axis=0, keepdims=True)

    return pl.pallas_call(
        body, name=name, grid=(n_tiles,),
        in_specs=[row, prev, nxt, row, prev, nxt, wspec, vec, vec],
        out_specs=[row, wspec, vec, vec],
        out_shape=[SDS((R, D), F32), SDS((4, G, G), F32), SDS((1, D), F32), SDS((1, D), F32)],
        compiler_params=_params(1),
    )(h, h, h, dy, dy, dy, w, b, scale)


def _rope_tables(s_len, l_len):
    t = jnp.arange(s_len)
    row = (t // GRID_W).astype(F32)
    col = (t % GRID_W).astype(F32)
    axis_dim = HEAD_DIM // 2
    inv = ROPE_THETA ** (-jnp.arange(0, axis_dim, 2, dtype=F32) / axis_dim)
    ar, ac = row[:, None] * inv, col[:, None] * inv
    cos = jnp.concatenate([jnp.cos(ar), jnp.cos(ar), jnp.cos(ac), jnp.cos(ac)], axis=-1)
    sin = jnp.concatenate([-jnp.sin(ar), jnp.sin(ar), -jnp.sin(ac), jnp.sin(ac)], axis=-1)
    cos = jnp.concatenate([cos, jnp.ones((l_len, HEAD_DIM), F32)], axis=0)
    sin = jnp.concatenate([sin, jnp.zeros((l_len, HEAD_DIM), F32)], axis=0)
    return cos, sin


def _swap_halves(v):
    lane = lax.broadcasted_iota(jnp.int32, v.shape, 1)
    return jnp.where((lane % 64) < 32, pltpu.roll(v, 96, 1), pltpu.roll(v, 32, 1))


def _qk_prep_fwd(qkv, q_gain, k_gain, cos, sin):
    R = qkv.shape[0]
    NQ, NK = N_HEADS * HEAD_DIM, N_KV * HEAD_DIM
    T = ROW_TILE
    vec = pl.BlockSpec((1, HEAD_DIM), lambda i: (0, 0))
    tab = pl.BlockSpec((T, HEAD_DIM), lambda i: (i, 0))

    def body(x_ref, qg_ref, kg_ref, c_ref, s_ref, q_ref, k_ref, v_ref):
        cosv, sinv = c_ref[...], s_ref[...]

        def prep(c0, gain):
            xh = x_ref[:, pl.ds(c0, HEAD_DIM)]
            xn = xh * lax.rsqrt(jnp.mean(xh * xh, axis=-1, keepdims=True) + EPS) * gain
            return _bf(xn * cosv + _swap_halves(xn) * sinv)

        for hd in range(N_HEADS):
            q_ref[:, pl.ds(hd * HEAD_DIM, HEAD_DIM)] = prep(hd * HEAD_DIM, qg_ref[...])
        for hd in range(N_KV):
            k_ref[:, pl.ds(hd * HEAD_DIM, HEAD_DIM)] = prep(NQ + hd * HEAD_DIM, kg_ref[...])
            v_ref[:, pl.ds(2 * hd * HEAD_DIM, HEAD_DIM)] = _bf(x_ref[:, pl.ds(NQ + NK + hd * HEAD_DIM, HEAD_DIM)])
            v_ref[:, pl.ds((2 * hd + 1) * HEAD_DIM, HEAD_DIM)] = jnp.ones((T, HEAD_DIM), BF16)

    return pl.pallas_call(
        body, name="qk_prep_fwd", grid=(R // T,),
        in_specs=[pl.BlockSpec((T, NQ + 2 * NK), lambda i: (i, 0)), vec, vec, tab, tab],
        out_specs=[pl.BlockSpec((T, NQ), lambda i: (i, 0)), pl.BlockSpec((T, NK), lambda i: (i, 0)),
                   pl.BlockSpec((T, 2 * NK), lambda i: (i, 0))],
        out_shape=[SDS((R, NQ), BF16), SDS((R, NK), BF16), SDS((R, 2 * NK), BF16)],
        compiler_params=_params(1),
    )(qkv, q_gain, k_gain, cos, sin)


def _qk_prep_bwd(qkv, dq, dk, dv, q_gain, k_gain, cos, sin):
    R = qkv.shape[0]
    NQ, NK = N_HEADS * HEAD_DIM, N_KV * HEAD_DIM
    T = ROW_TILE
    vec = pl.BlockSpec((1, HEAD_DIM), lambda i: (0, 0))
    tab = pl.BlockSpec((T, HEAD_DIM), lambda i: (i, 0))

    def body(x_ref, dq_ref, dk_ref, dv_ref, qg_ref, kg_ref, c_ref, s_ref, o_ref, dqg_ref, dkg_ref):
        i = pl.program_id(0)
        cosv, sinv = c_ref[...], s_ref[...]

        @pl.when(i == 0)
        def _():
            dqg_ref[...] = jnp.zeros_like(dqg_ref)
            dkg_ref[...] = jnp.zeros_like(dkg_ref)

        def back(c0, dout, gain, dg_ref):
            xh = x_ref[:, pl.ds(c0, HEAD_DIM)]
            r = lax.rsqrt(jnp.mean(xh * xh, axis=-1, keepdims=True) + EPS)
            xhat = xh * r
            dxn = dout * cosv + _swap_halves(dout * sinv)
            dg_ref[...] += jnp.sum(dxn * xhat, axis=0, keepdims=True)
            a = dxn * gain
            o_ref[:, pl.ds(c0, HEAD_DIM)] = _bf(r * (a - xhat * jnp.mean(a * xhat, axis=-1, keepdims=True)))

        for hd in range(N_HEADS):
            back(hd * HEAD_DIM, dq_ref[:, pl.ds(hd * HEAD_DIM, HEAD_DIM)], qg_ref[...], dqg_ref)
        for hd in range(N_KV):
            back(NQ + hd * HEAD_DIM, dk_ref[:, pl.ds(hd * HEAD_DIM, HEAD_DIM)], kg_ref[...], dkg_ref)
        o_ref[:, pl.ds(NQ + NK, NK)] = _bf(dv_ref[...])

    return pl.pallas_call(
        body, name="qk_prep_bwd", grid=(R // T,),
        in_specs=[pl.BlockSpec((T, NQ + 2 * NK), lambda i: (i, 0)), pl.BlockSpec((T, NQ), lambda i: (i, 0)),
                  pl.BlockSpec((T, NK), lambda i: (i, 0)), pl.BlockSpec((T, NK), lambda i: (i, 0)),
                  vec, vec, tab, tab],
        out_specs=[pl.BlockSpec((T, NQ + 2 * NK), lambda i: (i, 0)), vec, vec],
        out_shape=[SDS((R, NQ + 2 * NK), BF16), SDS((1, HEAD_DIM), F32), SDS((1, HEAD_DIM), F32)],
        compiler_params=_params(1),
    )(qkv, dq, dk, dv, q_gain, k_gain, cos, sin)


def _flash_fwd(q, k, v, s_len, l_len):
    R = q.shape[0]
    T = FLASH_FWD_TILE
    n_lat = s_len // T
    ck = _pick(s_len, 1024, 128)
    scale = HEAD_DIM ** -0.5
    group = N_HEADS // N_KV
    GW = group * HEAD_DIM
    M = group * T
    chunks = s_len // ck
    to_log2 = scale * math.log2(math.e)

    def body(q_ref, k_ref, v_ref, o_ref, lse_ref, s_s, sc_s, ml_s, mb_s, acc_s):
        i = pl.program_id(1)
        qv = jnp.concatenate([q_ref[:, pl.ds(hh * HEAD_DIM, HEAD_DIM)] for hh in range(group)], axis=0)

        ml_s[...] = jnp.full_like(ml_s, -jnp.inf)

        def lane_max(s, n):
            m = ml_s[...]
            for t in range(n // HEAD_DIM):
                m = jnp.maximum(m, s[:, t * HEAD_DIM:(t + 1) * HEAD_DIM])
            ml_s[...] = m

        @pl.when(i < n_lat)
        def _():
            def loop(c, carry):
                s = _dot_nt(qv, k_ref[pl.ds(pl.multiple_of(c * ck, ck), ck), :])
                s_s[c] = s
                lane_max(s, ck)
                return carry
            lax.fori_loop(0, chunks, loop, 0, unroll=8 if chunks % 8 == 0 else 1)

        sc = _dot_nt(qv, k_ref[pl.ds(s_len, l_len), :])
        sc_s[...] = sc
        lane_max(sc, l_len)
        m_row = jnp.max(ml_s[...], axis=-1, keepdims=True) * to_log2
        mb_s[...] = jnp.broadcast_to(m_row, (M, ck))

        acc_s[...] = jnp.zeros_like(acc_s)

        @pl.when(i < n_lat)
        def _():
            def loop(c, carry):
                p = jnp.exp2(s_s[c] * to_log2 - mb_s[...])
                acc_s[...] += _dot(_bf(p), v_ref[pl.ds(pl.multiple_of(c * ck, ck), ck), :])
                return carry
            lax.fori_loop(0, chunks, loop, 0, unroll=8 if chunks % 8 == 0 else 1)

        p = jnp.exp2(sc_s[...] * to_log2 - mb_s[:, pl.ds(0, l_len)])
        acc_s[...] += _dot(_bf(p), v_ref[pl.ds(s_len, l_len), :])
        l_rep = acc_s[:, pl.ds(HEAD_DIM, HEAD_DIM)]
        o = acc_s[:, pl.ds(0, HEAD_DIM)] / l_rep
        for hh in range(group):
            o_ref[:, pl.ds(hh * HEAD_DIM, HEAD_DIM)] = o[hh * T:(hh + 1) * T]
        lse = (mb_s[:, pl.ds(0, HEAD_DIM)] + jnp.log2(l_rep)) * math.log(2.0)
        lse_ref[...] = jnp.max(lse, axis=-1, keepdims=True).reshape(group, T, 1)

    return pl.pallas_call(
        body, name="flash_fwd", grid=(N_KV, R // T),
        in_specs=[pl.BlockSpec((T, GW), lambda g, i: (i, g)),
                  pl.BlockSpec((R, HEAD_DIM), lambda g, i: (0, g)),
                  pl.BlockSpec((R, 2 * HEAD_DIM), lambda g, i: (0, g))],
        out_specs=[pl.BlockSpec((T, GW), lambda g, i: (i, g)),
                   pl.BlockSpec((group, T, 1), lambda g, i: (g, i, 0))],
        out_shape=[SDS((R, N_HEADS * HEAD_DIM), F32), SDS((N_HEADS, R, 1), F32)],
        scratch_shapes=[pltpu.VMEM((chunks, M, ck), F32), pltpu.VMEM((M, l_len), F32), pltpu.VMEM((M, HEAD_DIM), F32),
                        pltpu.VMEM((M, ck), F32), pltpu.VMEM((M, 2 * HEAD_DIM), F32)],
        compiler_params=_params(2),
    )(q, k, v)


def _flash_bwd(q, k, v, o, lse, do, s_len, l_len):
    R = q.shape[0]
    T = ROW_TILE
    n_lat = s_len // T
    ck = _pick(s_len, 512, 128)
    scale = HEAD_DIM ** -0.5
    group = N_HEADS // N_KV
    GW = group * HEAD_DIM
    qspec = pl.BlockSpec((T, GW), lambda g, i: (i, g))
    kspec = pl.BlockSpec((R, HEAD_DIM), lambda g, i: (0, g))

    M = group * T
    log2e = math.log2(math.e)

    def body(q_ref, do_ref, o_ref, lse_ref, k_ref, v_ref, dq_ref, dk_ref, dv_ref, dq_s, lse_s, delta_s):
        i = pl.program_id(1)

        @pl.when(i == 0)
        def _():
            dk_ref[...] = jnp.zeros_like(dk_ref)
            dv_ref[...] = jnp.zeros_like(dv_ref)

        def stacked(ref):
            return jnp.concatenate([ref[:, pl.ds(hh * HEAD_DIM, HEAD_DIM)] for hh in range(group)], axis=0)

        qv = stacked(q_ref)
        dov = stacked(do_ref)
        dob = _bf(dov)
        delta_s[...] = jnp.broadcast_to(jnp.sum(dov * stacked(o_ref), axis=-1, keepdims=True), (M, ck))
        lse_s[...] = jnp.broadcast_to(lse_ref[...].reshape(M, 1) * log2e, (M, ck))
        dq_s[...] = jnp.zeros_like(dq_s)

        def step(rows, n):
            kv, vv = k_ref[rows, :], v_ref[rows, :]
            p = jnp.exp2(_dot_nt(qv, kv) * (scale * log2e) - lse_s[:, pl.ds(0, n)])
            dv_ref[rows, :] += _dot_tn(_bf(p), dob)
            ds = _bf(p * (_dot_nt(dob, vv) - delta_s[:, pl.ds(0, n)]) * scale)
            dq_s[...] += _dot(ds, kv)
            dk_ref[rows, :] += _dot_tn(ds, qv)

        @pl.when(i < n_lat)
        def _():
            def loop(c, carry):
                step(pl.ds(pl.multiple_of(c * ck, ck), ck), ck)
                return carry
            lax.fori_loop(0, s_len // ck, loop, 0, unroll=4 if (s_len // ck) % 4 == 0 else 1)

        step(pl.ds(s_len, l_len), l_len)
        for hh in range(group):
            dq_ref[:, pl.ds(hh * HEAD_DIM, HEAD_DIM)] = dq_s[pl.ds(hh * T, T), :]

    return pl.pallas_call(
        body, name="flash_bwd", grid=(N_KV, R // T),
        in_specs=[qspec, qspec, qspec, pl.BlockSpec((group, T, 1), lambda g, i: (g, i, 0)), kspec,
                  pl.BlockSpec((R, HEAD_DIM), lambda g, i: (0, 2 * g))],
        out_specs=[qspec, kspec, kspec],
        out_shape=[SDS((R, N_HEADS * HEAD_DIM), F32), SDS((R, N_KV * HEAD_DIM), F32),
                   SDS((R, N_KV * HEAD_DIM), F32)],
        scratch_shapes=[pltpu.VMEM((M, HEAD_DIM), F32), pltpu.VMEM((M, ck), F32), pltpu.VMEM((M, ck), F32)],
        compiler_params=_params(2),
    )(q, do, o, lse, k, v)


K_SCALE = RET_DK ** -0.5


def _log_sigmoid(v):
    return -(jnp.maximum(-v, 0.0) + jnp.log(1.0 + jnp.exp(-jnp.abs(v))))


def _ret_decays(d, lg):
    C = RET_CHUNK
    ic = lax.broadcasted_iota(jnp.int32, (C, 1), 0)
    ir = lax.broadcasted_iota(jnp.int32, (1, C), 1)
    li = jnp.where(d == 0, ic, C - 1 - ic).astype(F32)
    lj = jnp.where(d == 0, ir, C - 1 - ir).astype(F32)
    diff = li - lj
    mask = jnp.where(diff >= 0, jnp.exp(jnp.maximum(diff, 0.0) * lg), 0.0)
    qd = jnp.exp((li + 1.0) * lg)
    kd = jnp.exp((C - 1.0 - li) * lg)
    cd = jnp.exp(C * lg)
    return li, diff, mask, qd, kd, cd


def _ctx_weights(d, t, lg, l_len):
    C = RET_CHUNK
    j = (t * C + lax.broadcasted_iota(jnp.int32, (C, 1), 0)).astype(F32)
    e = jnp.where(d == 0, (l_len - 1.0) - j, j)
    return e, jnp.exp(e * lg)


def _mirrored(x, i, n_lat):
    return jnp.where(i < n_lat, jnp.concatenate([x[RET_CHUNK:], x[:RET_CHUNK]], axis=0), x)


def _mirror_tile(n_lat):
    return lambda i: jnp.where(i < n_lat, n_lat - 1 - i, i)


def _ret_fwd(proj, lgt, s_len, l_len):
    R = proj.shape[0]
    C, H, DK, DV = RET_CHUNK, RET_HEADS, RET_DK, RET_DV
    nl, nc = s_len // C, l_len // C

    def stored(t):
        return jnp.where(t < nc, nl + t, jnp.maximum(t - nc, 0))

    def actual(d, t):
        n = jnp.maximum(t - nc, 0)
        return jnp.where(t < nc, nl + t, n if d == 0 else nl - 1 - n)

    def body(q0_ref, k0_ref, v0_ref, q1_ref, k1_ref, v1_ref, lg_ref, o_ref, st_ref, r_s):
        t = pl.program_id(0)
        qkv = ((q0_ref, k0_ref, v0_ref), (q1_ref, k1_ref, v1_ref))

        @pl.when(t == 0)
        def _():
            r_s[...] = jnp.zeros_like(r_s)

        def log_gamma(d, hh):
            return jnp.max(_log_sigmoid(lg_ref[d, hh]), axis=-1, keepdims=True)

        @pl.when(t < nc)
        def _():
            for d, (q_ref, k_ref, v_ref) in enumerate(qkv):
                for hh in range(H):
                    qc, vc = pl.ds(hh * DK, DK), pl.ds(hh * DV, DV)
                    _, w = _ctx_weights(d, t, log_gamma(d, hh), l_len)
                    r_s[d, hh] += _dot_tn(_bf(k_ref[:, qc] * K_SCALE * w), _bf(v_ref[:, vc]))
                    o_ref[d, :, vc] = jnp.zeros((C, DV), F32)

        @pl.when(t >= nc)
        def _():
            for d, (q_ref, k_ref, v_ref) in enumerate(qkv):
                for hh in range(H):
                    qc, vc = pl.ds(hh * DK, DK), pl.ds(hh * DV, DV)
                    _, _, mask, qd, kd, cd = _ret_decays(d, log_gamma(d, hh))
                    qb, kv, vb = _bf(q_ref[:, qc]), k_ref[:, qc] * K_SCALE, _bf(v_ref[:, vc])
                    r = r_s[d, hh]
                    st_ref[d, hh, 0] = r
                    att = _dot_nt(qb, _bf(kv)) * mask
                    o_ref[d, :, vc] = _dot(_bf(att), vb) + _dot(qb, _bf(r)) * qd
                    r_s[d, hh] = r * cd + _dot_tn(_bf(kv * kd), vb)

    def rows(d):
        return [pl.BlockSpec((C, H * DK), lambda t: (actual(d, t), 0)),
                pl.BlockSpec((C, H * DK), lambda t: (actual(d, t), 1)),
                pl.BlockSpec((C, H * DV), lambda t: (actual(d, t), 1))]

    return pl.pallas_call(
        body, name="ret_fwd", grid=(nc + nl,),
        in_specs=rows(0) + rows(1) + [pl.BlockSpec((2, H, 1, 128), lambda t: (0, 0, 0, 0))],
        out_specs=[pl.BlockSpec((2, C, H * DV), lambda t: (0, stored(t), 0)),
                   pl.BlockSpec((2, H, 1, DK, DV), lambda t: (0, 0, jnp.maximum(t - nc, 0), 0, 0))],
        out_shape=[SDS((2, R, H * DV), F32), SDS((2, H, nl, DK, DV), F32)],
        scratch_shapes=[pltpu.VMEM((2, H, DK, DV), F32)],
        compiler_params=_params(1),
    )(proj, proj, proj, proj, proj, proj, lgt)


def _ret_bwd(proj, lgt, states, do, s_len, l_len):
    R = proj.shape[0]
    C, H, DK, DV = RET_CHUNK, RET_HEADS, RET_DK, RET_DV
    nl, nc = s_len // C, l_len // C
    last = nl + nc - 1

    def stored(t):
        return jnp.where(t < nl, jnp.maximum(nl - 1 - t, 0), t)

    def actual(d, t):
        return stored(t) if d == 0 else t

    def body(q0_ref, k0_ref, v0_ref, do0_ref, q1_ref, k1_ref, v1_ref, do1_ref, lg_ref, st_ref,
             dq_ref, dk_ref, dv_ref, dlg_ref, dr_s, dl_s):
        t = pl.program_id(0)
        ins = ((q0_ref, k0_ref, v0_ref, do0_ref), (q1_ref, k1_ref, v1_ref, do1_ref))

        def log_gamma(d, hh):
            return jnp.max(_log_sigmoid(lg_ref[d, hh]), axis=-1, keepdims=True)

        @pl.when(t == 0)
        def _():
            dr_s[...] = jnp.zeros_like(dr_s)
            dl_s[...] = jnp.zeros_like(dl_s)

        @pl.when(t < nl)
        def _():
            for d, (q_ref, k_ref, v_ref, do_ref) in enumerate(ins):
                for hh in range(H):
                    qc, vc = pl.ds(hh * DK, DK), pl.ds(hh * DV, DV)
                    li, diff, mask, qd, kd, cd = _ret_decays(d, log_gamma(d, hh))
                    qv, kv, vv, dov = q_ref[:, qc], k_ref[:, qc] * K_SCALE, v_ref[:, vc], do_ref[:, vc]
                    qb, kb, vb, dob = _bf(qv), _bf(kv), _bf(vv), _bf(dov)
                    r, drn = st_ref[d, hh, 0], dr_s[d, hh]
                    rb, drb = _bf(r), _bf(drn)
                    p = _dot_nt(qb, kb)
                    dp = _dot_nt(dob, vb) * mask
                    dpb = _bf(dp)
                    doq = _bf(dov * qd)
                    dq_inter = _dot_nt(doq, rb)
                    dk_state = kd * _dot_nt(vb, drb)
                    dq_ref[d, :, qc] = _dot(dpb, kb) + dq_inter
                    dk_ref[d, :, qc] = (_dot_tn(dpb, qb) + dk_state) * K_SCALE
                    dv_ref[d, :, vc] = _dot_tn(_bf(p * mask), dob) + _dot(_bf(kv * kd), drb)
                    dr_s[d, hh] = cd * drn + _dot_tn(qb, doq)
                    dl_s[d, hh] += (jnp.sum(dp * p * diff) + jnp.sum((li + 1.0) * qv * dq_inter)
                                    + jnp.sum((C - 1.0 - li) * kv * dk_state) + C * jnp.sum(cd * r * drn))

        @pl.when(t >= nl)
        def _():
            for d, (q_ref, k_ref, v_ref, do_ref) in enumerate(ins):
                for hh in range(H):
                    qc, vc = pl.ds(hh * DK, DK), pl.ds(hh * DV, DV)
                    e, w = _ctx_weights(d, t - nl, log_gamma(d, hh), l_len)
                    kv, vb, drb = k_ref[:, qc] * K_SCALE, _bf(v_ref[:, vc]), _bf(dr_s[d, hh])
                    dkc = w * _dot_nt(vb, drb)
                    dq_ref[d, :, qc] = jnp.zeros((C, DK), F32)
                    dk_ref[d, :, qc] = dkc * K_SCALE
                    dv_ref[d, :, vc] = _dot(_bf(kv * w), drb)
                    dl_s[d, hh] += jnp.sum(e * kv * dkc)

        @pl.when(t == last)
        def _():
            for d in range(2):
                for hh in range(H):
                    dlg_ref[d, hh] = dl_s[d, hh] * (1.0 / (1.0 + jnp.exp(lg_ref[d, hh])))

    def rows(d):
        return [pl.BlockSpec((C, H * DK), lambda t: (actual(d, t), 0)),
                pl.BlockSpec((C, H * DK), lambda t: (actual(d, t), 1)),
                pl.BlockSpec((C, H * DV), lambda t: (actual(d, t), 1)),
                pl.BlockSpec((C, H * DV), lambda t: (actual(d, t), 0))]

    return pl.pallas_call(
        body, name="ret_bwd", grid=(nl + nc,),
        in_specs=rows(0) + rows(1) + [
            pl.BlockSpec((2, H, 1, 128), lambda t: (0, 0, 0, 0)),
            pl.BlockSpec((2, H, 1, DK, DV), lambda t: (0, 0, jnp.maximum(nl - 1 - t, 0), 0, 0))],
        out_specs=[pl.BlockSpec((2, C, H * DK), lambda t: (0, stored(t), 0)),
                   pl.BlockSpec((2, C, H * DK), lambda t: (0, stored(t), 0)),
                   pl.BlockSpec((2, C, H * DV), lambda t: (0, stored(t), 0)),
                   pl.BlockSpec((2, H, 1, 128), lambda t: (0, 0, 0, 0))],
        out_shape=[SDS((2, R, H * DK), F32), SDS((2, R, H * DK), F32), SDS((2, R, H * DV), F32),
                   SDS((2, H, 1, 128), F32)],
        scratch_shapes=[pltpu.VMEM((2, H, DK, DV), F32), pltpu.VMEM((2, H, 1, 128), F32)],
        compiler_params=_params(1),
    )(proj, proj, proj, do, proj, proj, proj, do, lgt, states)


def _readout_fwd(o2, proj, gn_w, n_lat):
    R = proj.shape[0]
    H, DV = RET_HEADS, RET_DV
    W = H * DV
    T = ROW_TILE
    assert T == 2 * RET_CHUNK

    def body(o_ref, ob_ref, g_ref, w_ref, out_ref):
        i = pl.program_id(0)
        for hh in range(H):
            cols = pl.ds(hh * DV, DV)
            y = o_ref[0, :, cols] + _mirrored(ob_ref[0, :, cols], i, n_lat)
            yc = y - jnp.mean(y, axis=-1, keepdims=True)
            yn = yc * lax.rsqrt(jnp.mean(yc * yc, axis=-1, keepdims=True) + EPS) * w_ref[:, cols]
            g = g_ref[:, cols]
            out_ref[:, cols] = _bf(g * _sigmoid(g) * yn)

    return pl.pallas_call(
        body, name="readout_fwd", grid=(R // T,),
        in_specs=[pl.BlockSpec((1, T, W), lambda i: (0, i, 0)),
                  pl.BlockSpec((1, T, W), lambda i: (1, _mirror_tile(n_lat)(i), 0)),
                  pl.BlockSpec((T, W), lambda i: (i, 2)), pl.BlockSpec((1, W), lambda i: (0, 0))],
        out_specs=pl.BlockSpec((T, W), lambda i: (i, 0)),
        out_shape=SDS((R, W), BF16), compiler_params=_params(1),
    )(o2, o2, proj, gn_w)


def _readout_bwd(o2, proj, gn_w, dgated, n_lat):
    R = proj.shape[0]
    H, DV = RET_HEADS, RET_DV
    W = H * DV
    T = ROW_TILE

    def body(o_ref, ob_ref, g_ref, w_ref, d_ref, do_ref, dg_ref, dw_ref):
        i = pl.program_id(0)

        @pl.when(i == 0)
        def _():
            dw_ref[...] = jnp.zeros_like(dw_ref)

        for hh in range(H):
            cols = pl.ds(hh * DV, DV)
            y = o_ref[0, :, cols] + _mirrored(ob_ref[0, :, cols], i, n_lat)
            yc = y - jnp.mean(y, axis=-1, keepdims=True)
            rstd = lax.rsqrt(jnp.mean(yc * yc, axis=-1, keepdims=True) + EPS)
            yn0 = yc * rstd
            wv = w_ref[:, cols]
            g = g_ref[:, cols]
            s = _sigmoid(g)
            dgt = d_ref[:, cols]
            dyn = dgt * (g * s)
            dg_ref[:, cols] = _bf(dgt * (yn0 * wv) * (s * (1.0 + g * (1.0 - s))))
            dw_ref[:, cols] += jnp.sum(dyn * yn0, axis=0, keepdims=True)
            a = dyn * wv
            do_ref[:, cols] = rstd * (a - jnp.mean(a, axis=-1, keepdims=True)
                                      - yn0 * jnp.mean(a * yn0, axis=-1, keepdims=True))

    return pl.pallas_call(
        body, name="readout_bwd", grid=(R // T,),
        in_specs=[pl.BlockSpec((1, T, W), lambda i: (0, i, 0)),
                  pl.BlockSpec((1, T, W), lambda i: (1, _mirror_tile(n_lat)(i), 0)),
                  pl.BlockSpec((T, W), lambda i: (i, 2)),
                  pl.BlockSpec((1, W), lambda i: (0, 0)), pl.BlockSpec((T, W), lambda i: (i, 0))],
        out_specs=[pl.BlockSpec((T, W), lambda i: (i, 0)), pl.BlockSpec((T, W), lambda i: (i, 0)),
                   pl.BlockSpec((1, W), lambda i: (0, 0))],
        out_shape=[SDS((R, W), F32), SDS((R, W), BF16), SDS((1, W), F32)],
        compiler_params=_params(1),
    )(o2, o2, proj, gn_w, dgated)


def _ret_dproj(dq2, dk2, dv2, dg, n_lat):
    R = dg.shape[0]
    NQ, NV = RET_HEADS * RET_DK, RET_HEADS * RET_DV
    T = ROW_TILE

    def body(dq_ref, dqb_ref, dk_ref, dkb_ref, dv_ref, dvb_ref, dg_ref, o_ref):
        i = pl.program_id(0)
        o_ref[:, pl.ds(0, NQ)] = _bf(dq_ref[0] + _mirrored(dqb_ref[0], i, n_lat))
        o_ref[:, pl.ds(NQ, NQ)] = _bf(dk_ref[0] + _mirrored(dkb_ref[0], i, n_lat))
        o_ref[:, pl.ds(2 * NQ, NV)] = _bf(dv_ref[0] + _mirrored(dvb_ref[0], i, n_lat))
        o_ref[:, pl.ds(2 * NQ + NV, NV)] = dg_ref[...]

    def both(width):
        return [pl.BlockSpec((1, T, width), lambda i: (0, i, 0)),
                pl.BlockSpec((1, T, width), lambda i: (1, _mirror_tile(n_lat)(i), 0))]

    return pl.pallas_call(
        body, name="ret_dproj", grid=(R // T,),
        in_specs=both(NQ) + both(NQ) + both(NV) + [pl.BlockSpec((T, NV), lambda i: (i, 0))],
        out_specs=pl.BlockSpec((T, 2 * NQ + 2 * NV), lambda i: (i, 0)),
        out_shape=SDS((R, 2 * NQ + 2 * NV), BF16), compiler_params=_params(1),
    )(dq2, dq2, dk2, dk2, dv2, dv2, dg)


def _silu(v):
    return v * _sigmoid(v)


def _ada_fwd(c_rows, ada_w, ada_b_shard):
    depth, D, cols = ada_w.shape

    def body(c_ref, w_ref, b_ref, o_ref):
        o_ref[0] = _dot(_bf(_silu(c_ref[...])), _bf(w_ref[0])) + b_ref[0]

    return pl.pallas_call(
        body, name="ada_fwd", grid=(depth,),
        in_specs=[pl.BlockSpec((16, D), lambda i: (0, 0)), pl.BlockSpec((1, D, cols), lambda i: (i, 0, 0)),
                  pl.BlockSpec((1, 1, cols), lambda i: (i, 0, 0))],
        out_specs=pl.BlockSpec((1, 16, cols), lambda i: (i, 0, 0)),
        out_shape=SDS((depth, 16, cols), F32), compiler_params=_params(1),
    )(c_rows, ada_w, ada_b_shard)


def _ada_bwd(c_rows, ada_w, d_lat, d_ctx):
    depth, D, cols = ada_w.shape

    def body(c_ref, w_ref, dl_ref, dc_ref, dw_ref, pc_ref):
        i = pl.program_id(0)
        cv = c_ref[...]
        a = _silu(cv)
        dcs = jnp.broadcast_to(jnp.sum(dc_ref[0], axis=0, keepdims=True), (8, cols))
        dw_ref[0] = _dot_tn(_bf(a[0:8]), _bf(dl_ref[0])) + _dot_tn(_bf(a[8:16]), _bf(dcs))

        @pl.when(i == 0)
        def _():
            pc_ref[...] = jnp.zeros_like(pc_ref)

        pc_ref[...] += _dot_nt(_bf(dcs), _bf(w_ref[0]))

        @pl.when(i == depth - 1)
        def _():
            cc = c_ref[pl.ds(8, 1), :]
            s = _sigmoid(cc)
            pc_ref[...] = pc_ref[...] * (s * (1.0 + cc * (1.0 - s)))

    return pl.pallas_call(
        body, name="ada_bwd", grid=(depth,),
        in_specs=[pl.BlockSpec((16, D), lambda i: (0, 0)), pl.BlockSpec((1, D, cols), lambda i: (i, 0, 0)),
                  pl.BlockSpec((1, 8, cols), lambda i: (i, 0, 0)), pl.BlockSpec((1, 8, cols), lambda i: (i, 0, 0))],
        out_specs=[pl.BlockSpec((1, D, cols), lambda i: (i, 0, 0)), pl.BlockSpec((8, D), lambda i: (0, 0))],
        out_shape=[SDS((depth, D, cols), F32), SDS((8, D), F32)], compiler_params=_params(1),
    )(c_rows, ada_w, d_lat, d_ctx)


def _adamw(w, g, m, v, name):
    shape = w.shape
    n = g.shape[0]
    cols = shape[-1]
    rows = w.size // cols
    tr = _pick(rows, 512, 8) if rows * cols * 4 > (1 << 20) else rows
    spec = pl.BlockSpec((tr, cols), lambda i: (i, 0))

    def body(w_ref, g_ref, m_ref, v_ref, go_ref, d_ref, mo_ref, vo_ref):
        gs = g_ref[0].astype(F32)
        for k in range(1, n):
            gs = gs + g_ref[k].astype(F32)
        mn = ADAM_B1 * m_ref[...] + (1.0 - ADAM_B1) * gs
        vn = ADAM_B2 * v_ref[...] + (1.0 - ADAM_B2) * jnp.square(gs)
        m_hat = mn / (1.0 - ADAM_B1 ** ADAM_STEP)
        v_hat = vn / (1.0 - ADAM_B2 ** ADAM_STEP)
        go_ref[...] = gs
        d_ref[...] = -ADAM_LR * (m_hat / (jnp.sqrt(v_hat) + ADAM_EPS) + ADAM_WD * w_ref[...])
        mo_ref[...] = mn
        vo_ref[...] = vn

    outs = pl.pallas_call(
        body, name=name, grid=(rows // tr,),
        in_specs=[spec, pl.BlockSpec((n, tr, cols), lambda i: (0, i, 0)), spec, spec],
        out_specs=[spec] * 4, out_shape=[SDS((rows, cols), F32)] * 4, compiler_params=_params(1),
    )(w.reshape(rows, cols), g.reshape(n, rows, cols), m.reshape(rows, cols), v.reshape(rows, cols))
    return tuple(o.reshape(shape) for o in outs)


def _sum_slots(own, recv, name):
    shape, n, cols = own.shape, recv.shape[0], own.shape[-1]
    own, recv = own.reshape(-1, cols), recv.reshape(n, -1, cols)
    rows = own.shape[0]
    tr = _pick(rows, 512, 16)

    def body(own_ref, r_ref, o_ref):
        acc = own_ref[...].astype(F32)
        for k in range(n):
            acc = acc + r_ref[k].astype(F32)
        o_ref[...] = acc

    return pl.pallas_call(
        body, name=name, grid=(rows // tr,),
        in_specs=[pl.BlockSpec((tr, cols), lambda i: (i, 0)), pl.BlockSpec((n, tr, cols), lambda i: (0, i, 0))],
        out_specs=pl.BlockSpec((tr, cols), lambda i: (i, 0)),
        out_shape=SDS((rows, cols), F32), compiler_params=_params(1),
    )(own, recv).reshape(shape)


def _position():
    return lax.axis_index("x"), lax.axis_index("y"), lax.axis_index("c")


def _peer(k, x, y, c):
    return (1 - x if k & 4 else x, 1 - y if k & 2 else y, 1 - c if k & 1 else c)


def _index(pos):
    return 4 * pos[0] + 2 * pos[1] + pos[2]


def _gather_small(v, name):
    rows, lanes = v.shape

    def body(x_ref, out_ref, send_sems, recv_sems, local_sem):
        me = _position()
        mine = pltpu.make_async_copy(x_ref, out_ref.at[_index(me)], local_sem)
        mine.start()

        def copy(k, slot):
            return pltpu.make_async_remote_copy(
                src_ref=x_ref, dst_ref=out_ref.at[slot], send_sem=send_sems.at[k - 1],
                recv_sem=recv_sems.at[k - 1], device_id=_peer(k, *me), device_id_type=MESH)

        sends = [copy(k, _index(me)) for k in range(1, N_DEV)]
        for cp in sends:
            cp.start()
        for k in range(1, N_DEV):
            copy(k, _index(_peer(k, *me))).wait_recv()
        for cp in sends:
            cp.wait_send()
        mine.wait()

    return pl.pallas_call(
        body, name=name, out_shape=SDS((N_DEV, rows, lanes), v.dtype),
        in_specs=[pl.BlockSpec(memory_space=pltpu.VMEM)],
        out_specs=pl.BlockSpec(memory_space=pltpu.VMEM),
        scratch_shapes=[pltpu.SemaphoreType.DMA((N_DEV - 1,)), pltpu.SemaphoreType.DMA((N_DEV - 1,)),
                        pltpu.SemaphoreType.DMA],
        compiler_params=pltpu.CompilerParams(vmem_limit_bytes=VMEM_LIMIT_V7X),
    )(v)


HBM_SPEC = pl.BlockSpec(memory_space=pltpu.HBM)
SEM_SPEC = pl.BlockSpec(memory_space=pltpu.SEMAPHORE)
SPLIT_EFFECT = pltpu.SideEffectType.DATAFLOW_SIDE_EFFECTING


def _split_start(srcs, gather, name):
    n = len(srcs)
    lands = [jnp.zeros(((N_DEV,) + s.shape) if gather else s.shape, s.dtype) for s in srcs]

    def body(*refs):
        src_refs, land_refs, sems, token = refs[:n], refs[n:2 * n], refs[2 * n:4 * n], refs[-1]
        me = _position()
        for a in range(n):
            for k in range(1, N_DEV):
                peer = _peer(k, *me)
                pltpu.make_async_remote_copy(
                    src_ref=src_refs[a] if gather else src_refs[a].at[_index(peer)],
                    dst_ref=land_refs[a].at[_index(me)], send_sem=sems[2 * a], recv_sem=sems[2 * a + 1],
                    device_id=peer, device_id_type=MESH).start()
        token[...] = jnp.zeros_like(token)

    hbm = lambda arrays: tuple(pltpu.HBM(a.shape, a.dtype) for a in arrays)
    outs = pl.pallas_call(
        body, name=name,
        out_shape=(pltpu.SemaphoreType.DMA(()),) * (2 * n) + hbm(srcs) + hbm(lands) + (SDS((8, 128), F32),),
        in_specs=(HBM_SPEC,) * (2 * n),
        out_specs=(SEM_SPEC,) * (2 * n) + (HBM_SPEC,) * (2 * n) + (pl.BlockSpec(memory_space=pltpu.VMEM),),
        input_output_aliases={a: 2 * n + a for a in range(2 * n)},
        compiler_params=pltpu.CompilerParams(has_side_effects=SPLIT_EFFECT),
    )(*[pltpu.with_memory_space_constraint(a, pltpu.HBM) for a in list(srcs) + lands])
    return outs[:2 * n], outs[2 * n:3 * n], outs[3 * n:4 * n], outs[-1]


def _split_wait(flight, after, name):
    sems, srcs, lands, _ = flight
    n = len(srcs)

    def body(*refs):
        land_refs, sem_refs = refs[n:2 * n], refs[2 * n:4 * n]
        me = _position()
        for a in range(n):
            seven = land_refs[a].at[pl.ds(0, N_DEV - 1)]
            copies = pltpu.make_async_remote_copy(
                src_ref=seven, dst_ref=seven, send_sem=sem_refs[2 * a], recv_sem=sem_refs[2 * a + 1],
                device_id=_peer(1, *me), device_id_type=MESH)
            copies.wait_send()
            copies.wait_recv()

    outs = pl.pallas_call(
        body, name=name,
        out_shape=tuple(pltpu.HBM(a.shape, a.dtype) for a in list(srcs) + list(lands)),
        in_specs=(HBM_SPEC,) * (2 * n) + (SEM_SPEC,) * (2 * n) + (pl.BlockSpec(memory_space=pl.ANY),),
        out_specs=(HBM_SPEC,) * (2 * n), input_output_aliases={a: a for a in range(2 * n)},
        compiler_params=pltpu.CompilerParams(has_side_effects=SPLIT_EFFECT),
    )(*srcs, *lands, *sems, after)
    return outs[:n], outs[n:]


def _pack_rows(arrays, lanes, dtype):
    flat = jnp.concatenate([a.astype(dtype).reshape(-1) for a in arrays])
    pad = (-flat.size) % (16 * lanes)
    if pad:
        flat = jnp.concatenate([flat, jnp.zeros((pad,), dtype)])
    return flat.reshape(-1, lanes)


def _unpack_rows(packed, shapes):
    n = packed.shape[0]
    flat = packed.reshape(n, -1)
    out, off = [], 0
    for shp in shapes:
        size = math.prod(shp)
        out.append(flat[:, off:off + size].reshape((n,) + tuple(shp)))
        off += size
    return out


def _unshard(g8, axis):
    moved = jnp.moveaxis(g8, 0, axis)
    shp = list(moved.shape)
    shp[axis:axis + 2] = [shp[axis] * shp[axis + 1]]
    return moved.reshape(shp)


def _split8(full, axis):
    shp = list(full.shape)
    shp[axis:axis + 1] = [N_DEV, shp[axis] // N_DEV]
    return jnp.moveaxis(full.reshape(shp), axis, 0)


def _my_shard(g, axis, me):
    size = g.shape[axis + 1] // N_DEV
    return lax.dynamic_slice_in_dim(g, me * size, size, axis=axis + 1)


BIG_WEIGHTS = ("ffn_w_up", "ffn_w_down", "attn_w_qkv", "attn_w_o", "ret_w_in", "ret_w_out", "pool_w")
LAYER_WEIGHTS = (
    (("ffn_w_up", 0, "cols"), ("ffn_w_down", 0, "rows"), ("pool_w", 0, "pool")),
    (("ffn_w_up", 1, "cols"), ("ffn_w_down", 1, "rows"), ("attn_w_qkv", 0, "cols"), ("attn_w_o", 0, "rows")),
    (("ffn_w_up", 2, "cols"), ("ffn_w_down", 2, "rows"), ("ret_w_in", 0, "cols"), ("ret_w_out", 0, "rows")),
    (("ffn_w_up", 3, "cols"), ("ffn_w_down", 3, "rows"), ("pool_w", 1, "pool")),
)


GATHER_GROUPS = (LAYER_WEIGHTS[0][:2],) + LAYER_WEIGHTS[1:]
GRAD_GROUPS = {"3": LAYER_WEIGHTS[3], "2": LAYER_WEIGHTS[2], "1": LAYER_WEIGHTS[1],
               "0ffn": LAYER_WEIGHTS[0][:2], "0mix": LAYER_WEIGHTS[0][2:]}


def _shard_to_send(w, kind):
    w = w.astype(BF16)
    return w.T if kind == "cols" else w


def _full_from_land(land, kind):
    return _unshard(land, 1) if kind == "pool" else land.reshape(-1, land.shape[-1])


def _grad_to_send(g, kind):
    return _split8(g, 1).astype(BF16) if kind == "pool" else g.astype(BF16).reshape(N_DEV, -1, g.shape[-1])


def _shard_grad(gsum, kind):
    return gsum.T if kind == "cols" else gsum
SMALL_SHARDED = (("norm_w", 2), ("pool_b", 1), ("pool_scale", 1), ("ret_gn_w", 1), ("ffn_conv_w", 2))
REPLICATED = ("ada_b", "attn_q_gain", "attn_k_gain", "ret_decay_logit", "ffn_conv_b")
WEIGHT_ORDER = ("c_ctx", "ada_w", "ada_b", "norm_w", "pool_w", "pool_b", "pool_scale", "attn_w_qkv",
                "attn_q_gain", "attn_k_gain", "attn_w_o", "ret_w_in", "ret_decay_logit", "ret_gn_w",
                "ret_w_out", "ffn_w_up", "ffn_conv_w", "ffn_conv_b", "ffn_w_down")


def _local_step(x0, target, mods, P, get_weights, put_grads, s_len, l_len):
    n_lat = s_len // ROW_TILE
    nw = P["norm_w"]
    lgt = jnp.broadcast_to(P["ret_decay_logit"][0][:, :, None, None], (2, RET_HEADS, 1, 128))
    cos, sin = _rope_tables(s_len, l_len)
    h_dtype = [F32 if i % 3 == 0 else BF16 for i in range(DEPTH)]
    saved = []
    mods = list(mods)
    X = x0
    h = _res_norm(X, None, None, 0, nw[0, 0], mods[0], 0, h_dtype[0], n_lat, "norm_first")
    for i in range(DEPTH):
        kind, j, mod = i % 3, i // 3, mods[i]
        W, zero = get_weights(i, "mix", h)
        W, mod = dict(W), mod + zero
        sv = {"X": X, "h": h, "W": W}
        if kind == 0:
            y = _pool_fwd(h, W["pool_w"], P["pool_b"][j:j + 1], P["pool_scale"][j:j + 1],
                          n_lat, s_len, l_len, f"pool_fwd{i}")
        elif kind == 1:
            qkv = _mm(h, W["attn_w_qkv"], "nt", F32, f"qkv{i}")
            q, k, v = _qk_prep_fwd(qkv, P["attn_q_gain"][j:j + 1], P["attn_k_gain"][j:j + 1], cos, sin)
            o, lse = _flash_fwd(q, k, v, s_len, l_len)
            y = _mm(o, W["attn_w_o"], "nn", F32, f"attn_out{i}")
            sv.update(qkv=qkv, q=q, k=k, v=v, o=o, lse=lse)
        else:
            proj = _mm(h, W["ret_w_in"], "nt", F32, f"ret_in{i}")
            o2, states = _ret_fwd(proj, lgt, s_len, l_len)
            gated = _readout_fwd(o2, proj, P["ret_gn_w"][j:j + 1], n_lat)
            y = _mm(gated, W["ret_w_out"], "nn", F32, f"ret_out{i}")
            sv.update(proj=proj, o2=o2, states=states, gated=gated)
        X1, h2 = _res_norm(X, y, mod, 0, nw[i, 1], mod, 1, BF16, n_lat, f"res_norm_mid{i}")
        W_ffn, zero = get_weights(i, "ffn", h2)
        W.update(W_ffn)
        mod = mod + zero
        u = _mm(h2, W["ffn_w_up"], "nt", FFN_HIDDEN_DTYPE, f"ffn_up{i}")
        gact = _conv_gate_fwd(u, P["ffn_conv_w"][i], P["ffn_conv_b"][i:i + 1], n_lat, f"conv_gate_fwd{i}")
        f = _mm(gact, W["ffn_w_down"], "nn", F32, f"ffn_down{i}")
        sv.update(y=y, X1=X1, h2=h2, u=u, gact=gact, f=f)
        saved.append(sv)
        if i + 1 < DEPTH:
            X, h = _res_norm(X1, f, mod, 1, nw[i + 1, 0], mods[i + 1], 0, h_dtype[i + 1], n_lat,
                             f"res_norm_end{i}")
        else:
            X = _res_norm(X1, f, mod, 1, None, None, 0, None, n_lat, "res_last")

    dX, loss = _loss_bwd(X, target, n_lat)
    G = {name: [None] * P[name].shape[0] for name in
         ("pool_b", "pool_scale", "attn_q_gain", "attn_k_gain", "ret_decay_logit", "ret_gn_w", "ffn_conv_w",
          "ffn_conv_b")}
    dnw = [[None, None] for _ in range(DEPTH)]
    dmods = [None] * DEPTH
    for i in reversed(range(DEPTH)):
        kind, j, mod, sv = i % 3, i // 3, mods[i], saved[i]
        W, gl = sv["W"], {}
        if i == DEPTH - 1:
            df, dg2 = _gate_bwd(dX, sv["f"], mod, 1, BF16, n_lat, f"gate_bwd_ffn{i}")
        dgact = _mm(df, W["ffn_w_down"], "nt", FFN_HIDDEN_DTYPE, f"ffn_down_dx{i}")
        gl["ffn_w_down"] = _mm(sv["gact"], df, "tn", BF16, f"ffn_down_dw{i}")
        du, dcw, dcb = _conv_gate_bwd(sv["u"], dgact, P["ffn_conv_w"][i], P["ffn_conv_b"][i:i + 1], n_lat,
                                      f"conv_gate_bwd{i}")
        G["ffn_conv_w"][i], G["ffn_conv_b"][i] = dcw, dcb[0]
        dh2 = _mm(du, W["ffn_w_up"], "nn", F32, f"ffn_up_dx{i}")
        gl["ffn_w_up"] = _mm(du, sv["h2"], "tn", BF16, f"ffn_up_dw{i}")
        if i == 0:
            mod = mod + put_grads("0ffn", gl)
        dX1, dnw[i][1], dsh2, dsc2, dy, dg1 = _norm_bwd(
            dX, dh2, sv["X1"], nw[i, 1], mod, 1, n_lat, f"norm_bwd_ffn{i}",
            gated=(sv["y"], mod, 0, F32 if kind == 0 else BF16))
        h = sv["h"]
        if kind == 0:
            dh, dpw, dpb, dps = _pool_bwd(h, dy, W["pool_w"], P["pool_b"][j:j + 1], P["pool_scale"][j:j + 1],
                                          n_lat, s_len, l_len, f"pool_bwd{i}")
            gl["pool_w"], G["pool_b"][j], G["pool_scale"][j] = dpw, dpb[0], dps[0]
        elif kind == 1:
            do = _mm(dy, W["attn_w_o"], "nt", F32, f"attn_out_dx{i}")
            gl["attn_w_o"] = _mm(sv["o"], dy, "tn", BF16, f"attn_out_dw{i}")
            dq, dk, dv = _flash_bwd(sv["q"], sv["k"], sv["v"], sv["o"], sv["lse"], do, s_len, l_len)
            dqkv, dqg, dkg = _qk_prep_bwd(sv["qkv"], dq, dk, dv, P["attn_q_gain"][j:j + 1],
                                          P["attn_k_gain"][j:j + 1], cos, sin)
            G["attn_q_gain"][j], G["attn_k_gain"][j] = dqg[0], dkg[0]
            dh = _mm(dqkv, W["attn_w_qkv"], "nn", F32, f"qkv_dx{i}")
            gl["attn_w_qkv"] = _mm(dqkv, h, "tn", BF16, f"qkv_dw{i}")
        else:
            dgated = _mm(dy, W["ret_w_out"], "nt", F32, f"ret_out_dx{i}")
            gl["ret_w_out"] = _mm(sv["gated"], dy, "tn", BF16, f"ret_out_dw{i}")
            do, dg, dgn = _readout_bwd(sv["o2"], sv["proj"], P["ret_gn_w"][j:j + 1], dgated, n_lat)
            dq2, dk2, dv2, dlg = _ret_bwd(sv["proj"], lgt, sv["states"], do, s_len, l_len)
            dproj = _ret_dproj(dq2, dk2, dv2, dg, n_lat)
            G["ret_gn_w"][j], G["ret_decay_logit"][j] = dgn[0], dlg[:, :, 0, 0]
            dh = _mm(dproj, W["ret_w_in"], "nn", F32, f"ret_in_dx{i}")
            gl["ret_w_in"] = _mm(dproj, h, "tn", BF16, f"ret_in_dw{i}")
        zero = put_grads(str(i) if i > 0 else "0mix", gl)
        if i > 0:
            mods[i - 1] = mods[i - 1] + zero
            dX, dnw[i][0], dsh1, dsc1, df_below, dg2_below = _norm_bwd(
                dX1, dh, sv["X"], nw[i, 0], mod, 0, n_lat, f"norm_bwd_mix{i}",
                gated=(saved[i - 1]["f"], mods[i - 1], 1, BF16))
        else:
            dX, dnw[i][0], dsh1, dsc1 = _norm_bwd(dX1, dh, sv["X"], nw[i, 0], mod, 0, n_lat, f"norm_bwd_mix{i}")
        dmods[i] = jnp.concatenate([dsh1, dsc1, dg1, dsh2, dsc2, dg2], axis=1)
        if i > 0:
            df, dg2 = df_below, dg2_below
    grads = {name: jnp.stack(parts) for name, parts in G.items()}
    grads["norm_w"] = jnp.stack([jnp.concatenate(pair, axis=0) for pair in dnw])
    return loss, dX, grads, jnp.stack(dmods)


def kernel(x, c, ctx, c_ctx, ada_w, ada_b, norm_w, pool_w, pool_b, pool_scale, attn_w_qkv, attn_q_gain,
           attn_k_gain, attn_w_o, ret_w_in, ret_decay_logit, ret_gn_w, ret_w_out, ffn_w_up, ffn_conv_w,
           ffn_conv_b, ffn_w_down, loss_target, m_c_ctx, m_ada_w, m_ada_b, m_norm_w, m_pool_w, m_pool_b,
           m_pool_scale, m_attn_w_qkv, m_attn_q_gain, m_attn_k_gain, m_attn_w_o, m_ret_w_in,
           m_ret_decay_logit, m_ret_gn_w, m_ret_w_out, m_ffn_w_up, m_ffn_conv_w, m_ffn_conv_b, m_ffn_w_down,
           v_c_ctx, v_ada_w, v_ada_b, v_norm_w, v_pool_w, v_pool_b, v_pool_scale, v_attn_w_qkv, v_attn_q_gain,
           v_attn_k_gain, v_attn_w_o, v_ret_w_in, v_ret_decay_logit, v_ret_gn_w, v_ret_w_out, v_ffn_w_up,
           v_ffn_conv_w, v_ffn_conv_b, v_ffn_w_down):
    A = dict(locals())
    me = _index(_position())
    s_len, D = x.shape[1], x.shape[2]
    l_len = ctx.shape[1]
    assert s_len % ROW_TILE == 0 and l_len % ROW_TILE == 0 and s_len % GRID_W == 0

    small = [A[n] for n, _ in SMALL_SHARDED]
    first_parts = [c] + small + [pool_w[0]]
    got = _gather_small(_pack_rows(first_parts, 128, F32), "gather_c_small")
    parts = _unpack_rows(got, [a.shape for a in first_parts])
    c_all = parts[0].reshape(N_DEV, D)
    P = {n: _unshard(g8, ax) for (n, ax), g8 in zip(SMALL_SHARDED, parts[1:-1])}
    first_pool_w = _unshard(parts[-1], 1).astype(BF16)

    c_rows = jnp.concatenate([c_all, c_ctx.reshape(1, D), jnp.zeros((7, D), F32)], axis=0)
    cols = ada_w.shape[2]
    ada_b_shard = lax.dynamic_slice_in_dim(ada_b, me * cols, cols, axis=1).reshape(DEPTH, 1, cols)
    mod_shard = _ada_fwd(c_rows, ada_w, ada_b_shard)
    got = _gather_small(mod_shard.reshape(-1, 128), "gather_mod").reshape(N_DEV, DEPTH, 16, cols)
    mod_lat = lax.dynamic_index_in_dim(got, me, axis=2, keepdims=False)
    mod_ctx = got[:, :, 8, :]
    mods = jnp.stack([jnp.moveaxis(mod_lat, 0, 1).reshape(DEPTH, 6, D),
                      jnp.moveaxis(mod_ctx, 0, 1).reshape(DEPTH, 6, D)], axis=1)

    shards = [[_shard_to_send(A[n][j], kind) for n, j, kind in group] for group in GATHER_GROUPS]
    flights, landed = {}, {}

    def start_gather(i, ready):
        mine, _ = lax.optimization_barrier((shards[i], ready))
        flights[i] = _split_start(mine, True, f"gather_start{i}")
        return flights[i][3][0, 0]

    mods = [mods[i] for i in range(DEPTH)]
    mods[0] = mods[0] + start_gather(0, mods[0])
    for n in REPLICATED:
        P[n] = A[n]

    def get_weights(i, part, x_now):
        if i == 0 and part == "mix":
            return {"pool_w": first_pool_w}, 0.0
        if i in landed:
            return landed[i], 0.0
        owns, lands = _split_wait(flights[i], x_now, f"gather_wait{i}")
        landed[i] = {n: _full_from_land(lax.dynamic_update_index_in_dim(land, own, me, axis=0), kind)
                     for (n, j, kind), own, land in zip(GATHER_GROUPS[i], owns, lands)}
        return landed[i], (start_gather(i + 1, lands) if i + 1 < DEPTH else 0.0)

    sent = {}

    def put_grads(group, gl):
        sent[group] = _split_start([_grad_to_send(gl[n], kind) for n, j, kind in GRAD_GROUPS[group]], False,
                                   f"exchange_start_{group}")
        return sent[group][3][0, 0]

    x0 = jnp.concatenate([x[0], ctx[0]], axis=0)
    loss8, dx0, G, dmods = _local_step(x0, loss_target[0], mods, P, get_weights, put_grads, s_len, l_len)
    loss = lax.psum(loss8[0, 0], ("x", "y", "c"))
    grad_x = dx0[:s_len].reshape(x.shape)

    small_names = ["dmods"] + list(REPLICATED[1:]) + [n for n, _ in SMALL_SHARDED]
    small_parts = [dmods] + [G[n] for n in small_names[1:]]
    got = _gather_small(_pack_rows(small_parts, 128, F32), "gather_small_grads")
    S8 = dict(zip(small_names, _unpack_rows(got, [a.shape for a in small_parts])))

    dm = S8["dmods"].reshape(N_DEV, DEPTH, 2, 6 * D)
    dm_mine = lax.dynamic_slice_in_dim(dm, me * cols, cols, axis=3)
    g_ada_w, pc = _ada_bwd(c_rows, ada_w, jnp.moveaxis(dm_mine[:, :, 0], 0, 1), jnp.moveaxis(dm_mine[:, :, 1], 0, 1))
    pc8 = _gather_small(pc.reshape(-1, 128), "gather_c_ctx_grad").reshape(N_DEV, 8, D)

    def owner_sums(group, after):
        sends, lands = _split_wait(sent[group], after, f"exchange_wait_{group}")
        out = {}
        for (n, j, kind), send, land in zip(GRAD_GROUPS[group], sends, lands):
            own = lax.dynamic_index_in_dim(send, me, axis=0, keepdims=False)
            out[(n, j)] = _shard_grad(_sum_slots(own, land, f"sum_slots_{n}{j}"), kind)
        return out

    shard_grads = {}
    for group in ("3", "2", "1", "0mix"):
        shard_grads.update(owner_sums(group, pc8))

    g_in = {"c_ctx": pc8[:, 0, :], "ada_w": g_ada_w[None],
            "ada_b": jnp.moveaxis(dm, 2, 1).reshape(2 * N_DEV, DEPTH, 6 * D)}
    for n in REPLICATED[1:]:
        g_in[n] = S8[n]
    for n, ax in SMALL_SHARDED:
        g_in[n] = _my_shard(S8[n], ax, me)

    def stacked(n):
        return jnp.stack([shard_grads[(n, j)] for j in range(A[n].shape[0])])[None]

    late = [n for n, j, kind in GRAD_GROUPS["0ffn"]]
    for n in BIG_WEIGHTS:
        if n not in late:
            g_in[n] = stacked(n)
    res = {n: _adamw(A[n], g_in[n], A["m_" + n], A["v_" + n], "adamw_" + n) for n in WEIGHT_ORDER if n not in late}
    done = sum(res[n][1].reshape(-1)[0] for n in res)
    shard_grads.update(owner_sums("0ffn", done.reshape(1, 1)))
    for n in late:
        res[n] = _adamw(A[n], stacked(n), A["m_" + n], A["v_" + n], "adamw_" + n)
    outs = [loss, grad_x]
    for slot in range(4):
        outs += [res[n][slot] for n in WEIGHT_ORDER]
    return tuple(outs)
```

```python
import functools
import math

import jax
import jax.numpy as jnp
from jax import lax
from jax.experimental import pallas as pl
from jax.experimental.pallas import tpu as pltpu

F32 = jnp.float32
BF16 = jnp.bfloat16
SDS = jax.ShapeDtypeStruct
MESH = pl.DeviceIdType.MESH

N_DEV = 8
EPS = 1e-6
DEPTH = 4
GRID_W = 64
POOL_WINDOWS = (2, 4, 8, 16)
N_HEADS = 8
N_KV = 2
HEAD_DIM = 128
ROPE_THETA = 10000.0
RET_HEADS = 4
RET_DK = 256
RET_DV = 512
RET_CHUNK = 128
ADAM_LR = 0.001
ADAM_B1 = 0.9
ADAM_B2 = 0.999
ADAM_EPS = 1e-08
ADAM_WD = 0.01
ADAM_STEP = 10

ROW_TILE = 256
FFN_HIDDEN_DTYPE = BF16
FLASH_FWD_TILE = 128
HALO = 8
VMEM_LIMIT_V7X = 56 * 1024 * 1024


def _params(n_axes=0):
    sem = ("arbitrary",) * n_axes if n_axes else None
    return pltpu.CompilerParams(dimension_semantics=sem, vmem_limit_bytes=VMEM_LIMIT_V7X)


def _pick(n, cap, mult):
    best = None
    for d in range(mult, min(n, cap) + 1, mult):
        if n % d == 0:
            best = d
    return best if best is not None else n


def _dot(a, b):
    return jnp.dot(a, b, preferred_element_type=F32)


def _dot_nt(a, b):
    return lax.dot_general(a, b, (((1,), (1,)), ((), ())), preferred_element_type=F32)


def _dot_tn(a, b):
    return lax.dot_general(a, b, (((0,), (0,)), ((), ())), preferred_element_type=F32)


def _bf(v):
    return v.astype(BF16)


def _sigmoid(v):
    return 0.5 * jnp.tanh(0.5 * v) + 0.5


MM_VMEM_BUDGET = 40 * 1024 * 1024
MM_STEP_BYTES = 1 << 20
MM_ACC_PASS_BYTES = 8


def _divisors(n, mult, cap):
    return [d for d in range(mult, min(n, cap) + 1, mult) if n % d == 0] or [n]


def _mm_tiles(mode, M, N, K, a_item, b_item, o_item):
    best = None
    for tm in _divisors(M, 128 if mode == "tn" else 16, 2816):
        for tn in _divisors(N, 128, 2048):
            for tk in _divisors(K, 16 if mode == "tn" else 128, 2816):
                ni, nj, nk = M // tm, N // tn, K // tk
                vmem = 2 * (tm * tk * a_item + tk * tn * b_item + tm * tn * o_item) + tm * tn * 4
                if vmem > MM_VMEM_BUDGET:
                    continue
                a_reads = 1 if nk == 1 else nj
                b_reads = 1 if (nk == 1 and nj == 1) else ni
                cost = (M * K * a_item * a_reads + K * N * b_item * b_reads + M * N * o_item
                        + ni * nj * nk * MM_STEP_BYTES + (nk - 1) * M * N * MM_ACC_PASS_BYTES)
                if best is None or cost < best[0]:
                    best = (cost, tm, tn, tk)
    return best[1:]


def _mm(a, b, mode, out_dtype, name):
    if mode == "nn":
        (M, K), (K2, N) = a.shape, b.shape
    elif mode == "nt":
        (M, K), (N, K2) = a.shape, b.shape
    else:
        (K, M), (K2, N) = a.shape, b.shape
    assert K == K2, (a.shape, b.shape, mode)
    tm, tn, tk = _mm_tiles(mode, M, N, K, a.dtype.itemsize, b.dtype.itemsize, jnp.dtype(out_dtype).itemsize)
    nk = K // tk
    if mode == "nn":
        a_spec = pl.BlockSpec((tm, tk), lambda i, j, k: (i, k))
        b_spec = pl.BlockSpec((tk, tn), lambda i, j, k: (k, j))
    elif mode == "nt":
        a_spec = pl.BlockSpec((tm, tk), lambda i, j, k: (i, k))
        b_spec = pl.BlockSpec((tn, tk), lambda i, j, k: (j, k))
    else:
        a_spec = pl.BlockSpec((tk, tm), lambda i, j, k: (k, i))
        b_spec = pl.BlockSpec((tk, tn), lambda i, j, k: (k, j))
    dot = {"nn": _dot, "nt": _dot_nt, "tn": _dot_tn}[mode]

    def body(a_ref, b_ref, o_ref, acc_ref):
        part = dot(_bf(a_ref[...]), _bf(b_ref[...]))
        if nk == 1:
            o_ref[...] = part.astype(out_dtype)
        else:
            k = pl.program_id(2)

            @pl.when(k == 0)
            def _():
                acc_ref[...] = part

            @pl.when(k > 0)
            def _():
                acc_ref[...] += part

            @pl.when(k == nk - 1)
            def _():
                o_ref[...] = acc_ref[...].astype(out_dtype)

    return pl.pallas_call(
        body, name=name, grid=(M // tm, N // tn, nk),
        in_specs=[a_spec, b_spec],
        out_specs=pl.BlockSpec((tm, tn), lambda i, j, k: (i, j)),
        out_shape=SDS((M, N), out_dtype),
        scratch_shapes=[pltpu.VMEM((tm, tn), F32)],
        compiler_params=_params(3),
    )(a, b)


def _seg_spec(n_lat, d):
    return pl.BlockSpec((1, 6, d), lambda i: ((i >= n_lat).astype(jnp.int32), 0, 0))


def _seg_acc_spec(n_lat, d):
    return pl.BlockSpec((1, 1, d), lambda i: ((i >= n_lat).astype(jnp.int32), 0, 0))


def _res_norm(x, y, gmod, gk, nw, nmod, nk, h_dtype, n_lat, name):
    R, D = x.shape
    has_res, has_norm = y is not None, nw is not None
    row = pl.BlockSpec((ROW_TILE, D), lambda i: (i, 0))
    vec = pl.BlockSpec((1, D), lambda i: (0, 0))
    ins, specs, outs, ospecs = [x], [row], [], []
    if has_res:
        ins += [y, gmod]
        specs += [row, _seg_spec(n_lat, D)]
        outs.append(SDS((R, D), F32))
        ospecs.append(row)
    if has_norm:
        ins += [nw.reshape(1, D), nmod]
        specs += [vec, _seg_spec(n_lat, D)]
        outs.append(SDS((R, D), h_dtype))
        ospecs.append(row)

    def body(*refs):
        refs = list(refs)
        z = refs.pop(0)[...]
        if has_res:
            y_ref, g_ref = refs.pop(0), refs.pop(0)
            z = z + g_ref[0, pl.ds(3 * gk + 2, 1), :] * y_ref[...].astype(F32)
        if has_norm:
            nw_ref, m_ref = refs.pop(0), refs.pop(0)
        if has_res:
            refs.pop(0)[...] = z
        if has_norm:
            r = lax.rsqrt(jnp.mean(z * z, axis=-1, keepdims=True) + EPS)
            h = (z * r) * nw_ref[...]
            h = h * (1.0 + m_ref[0, pl.ds(3 * nk + 1, 1), :]) + m_ref[0, pl.ds(3 * nk, 1), :]
            refs.pop(0)[...] = h.astype(h_dtype)

    res = pl.pallas_call(
        body, name=name, grid=(R // ROW_TILE,), in_specs=specs, out_specs=ospecs,
        out_shape=outs, compiler_params=_params(1),
    )(*ins)
    return res if len(res) > 1 else res[0]


def _gate_bwd(dz, y, mod, k, out_dtype, n_lat, name):
    R, D = dz.shape
    row = pl.BlockSpec((ROW_TILE, D), lambda i: (i, 0))

    def body(dz_ref, y_ref, m_ref, dy_ref, dg_ref):
        i = pl.program_id(0)
        dzv = dz_ref[...]
        dy_ref[...] = (m_ref[0, pl.ds(3 * k + 2, 1), :] * dzv).astype(out_dtype)

        @pl.when((i == 0) | (i == n_lat))
        def _():
            dg_ref[...] = jnp.zeros_like(dg_ref)

        dg_ref[0] += jnp.sum(dzv * y_ref[...].astype(F32), axis=0, keepdims=True)

    return pl.pallas_call(
        body, name=name, grid=(R // ROW_TILE,),
        in_specs=[row, row, _seg_spec(n_lat, D)],
        out_specs=[row, _seg_acc_spec(n_lat, D)],
        out_shape=[SDS((R, D), out_dtype), SDS((2, 1, D), F32)],
        compiler_params=_params(1),
    )(dz, y, mod)


def _norm_bwd(dz, dh, x, nw, mod, k, n_lat, name, gated=None):
    R, D = x.shape
    row = pl.BlockSpec((ROW_TILE, D), lambda i: (i, 0))
    vec = pl.BlockSpec((1, D), lambda i: (0, 0))
    ins, specs = [dz, dh, x, nw.reshape(1, D), mod], [row, row, row, vec, _seg_spec(n_lat, D)]
    outs = [SDS((R, D), F32), SDS((1, D), F32), SDS((2, 1, D), F32), SDS((2, 1, D), F32)]
    ospecs = [row, vec, _seg_acc_spec(n_lat, D), _seg_acc_spec(n_lat, D)]
    if gated is not None:
        y, gmod, gk, dy_dtype = gated
        ins += [y, gmod]
        specs += [row, _seg_spec(n_lat, D)]
        outs += [SDS((R, D), dy_dtype), SDS((2, 1, D), F32)]
        ospecs += [row, _seg_acc_spec(n_lat, D)]

    def body(dz_ref, dh_ref, x_ref, nw_ref, m_ref, *rest):
        if gated is not None:
            y_ref, g_ref, dx_ref, dnw_ref, dsh_ref, dsc_ref, dy_ref, dg_ref = rest
        else:
            dx_ref, dnw_ref, dsh_ref, dsc_ref = rest
        i = pl.program_id(0)
        xv = x_ref[...]
        dhv = dh_ref[...].astype(F32)
        nwv = nw_ref[...]
        sc1 = 1.0 + m_ref[0, pl.ds(3 * k + 1, 1), :]
        r = lax.rsqrt(jnp.mean(xv * xv, axis=-1, keepdims=True) + EPS)
        xhat = xv * r
        a = dhv * (nwv * sc1)
        dx = dz_ref[...] + r * (a - xhat * jnp.mean(a * xhat, axis=-1, keepdims=True))
        dx_ref[...] = dx

        @pl.when(i == 0)
        def _():
            dnw_ref[...] = jnp.zeros_like(dnw_ref)

        @pl.when((i == 0) | (i == n_lat))
        def _():
            dsh_ref[...] = jnp.zeros_like(dsh_ref)
            dsc_ref[...] = jnp.zeros_like(dsc_ref)
            if gated is not None:
                dg_ref[...] = jnp.zeros_like(dg_ref)

        dnw_ref[...] += jnp.sum(dhv * xhat, axis=0, keepdims=True) * sc1
        dsh_ref[0] += jnp.sum(dhv, axis=0, keepdims=True)
        dsc_ref[0] += jnp.sum(dhv * xhat, axis=0, keepdims=True) * nwv
        if gated is not None:
            dy_ref[...] = (g_ref[0, pl.ds(3 * gk + 2, 1), :] * dx).astype(dy_dtype)
            dg_ref[0] += jnp.sum(dx * y_ref[...].astype(F32), axis=0, keepdims=True)

    return pl.pallas_call(
        body, name=name, grid=(R // ROW_TILE,), in_specs=specs, out_specs=ospecs, out_shape=outs,
        compiler_params=_params(1),
    )(*ins)


def _loss_bwd(xf, target, n_lat):
    R, D = xf.shape
    row = pl.BlockSpec((ROW_TILE, D), lambda i: (i, 0))
    tgt = pl.BlockSpec((ROW_TILE, D), lambda i: (jnp.minimum(i, n_lat - 1), 0))

    def body(x_ref, t_ref, dx_ref, loss_ref):
        i = pl.program_id(0)
        e = jnp.where(i < n_lat, x_ref[...] - t_ref[...], 0.0)
        dx_ref[...] = e * (1.0 / D)

        @pl.when(i == 0)
        def _():
            loss_ref[...] = jnp.zeros_like(loss_ref)

        loss_ref[...] += 0.5 * jnp.sum(jnp.mean(e * e, axis=-1, keepdims=True))

    return pl.pallas_call(
        body, name="loss_bwd", grid=(R // ROW_TILE,),
        in_specs=[row, tgt],
        out_specs=[row, pl.BlockSpec((8, 128), lambda i: (0, 0))],
        out_shape=[SDS((R, D), F32), SDS((8, 128), F32)],
        compiler_params=_params(1),
    )(xf, target)


def _halo_rows(dtype):
    return HALO * (4 // jnp.dtype(dtype).itemsize)


def _halo_specs(n_tiles, width, tile=ROW_TILE, rows=HALO):
    per = tile // rows
    prev = pl.BlockSpec((rows, width), lambda i: (jnp.maximum(i * per - 1, 0), 0))
    nxt = pl.BlockSpec((rows, width), lambda i: (jnp.minimum((i + 1) * per, n_tiles * per - 1), 0))
    return prev, nxt


SHIFT_K = 256


def _shift_matrix(n_out, first_row, deltas):
    half = n_out // 2
    out = []
    for h, start in enumerate((0, 2 * _halo_rows(BF16))):
        r = lax.broadcasted_iota(jnp.int32, (half, SHIFT_K), 0) + (first_row + h * half - start)
        j = lax.broadcasted_iota(jnp.int32, (half, SHIFT_K), 1)
        out.append(jnp.concatenate([(j == r + d).astype(F32) for d in deltas], axis=0).astype(BF16))
    return out


def _shifted_rows(t_ref, p_ref, n_ref, cols, first, last, picks, n_blocks, halves=False):
    pr = jnp.where(first, jnp.zeros_like(p_ref[:, cols]), p_ref[:, cols])
    nx = jnp.where(last, jnp.zeros_like(n_ref[:, cols]), n_ref[:, cols])
    e = jnp.concatenate([pr, t_ref[:, cols], nx], axis=0)
    start = 2 * pr.shape[0]
    top, bot = _dot(picks[0], e[0:SHIFT_K]), _dot(picks[1], e[start:start + SHIFT_K])
    half = picks[0].shape[0] // n_blocks
    if halves:
        return [[m[k * half:(k + 1) * half] for k in range(n_blocks)] for m in (top, bot)]
    return [jnp.concatenate([top[k * half:(k + 1) * half], bot[k * half:(k + 1) * half]], axis=0)
            for k in range(n_blocks)]


def _edge_flags(i, n_lat, n_tiles):
    first = (i == 0) | (i == n_lat)
    last = (i == n_lat - 1) | (i == n_tiles - 1)
    return first, last


def _conv_gate_fwd(u, conv_w, conv_b, n_lat, name):
    R, F2 = u.shape
    F = F2 // 2
    n_tiles = R // ROW_TILE
    T = ROW_TILE
    cw = _pick(F, 256, 128)
    row = pl.BlockSpec((T, F2), lambda i: (i, 0))
    prev, nxt = _halo_specs(n_tiles, F2, rows=_halo_rows(u.dtype))

    assert u.dtype == BF16

    def body(u_ref, p_ref, n_ref, w_ref, b_ref, o_ref):
        i = pl.program_id(0)
        first, last = _edge_flags(i, n_lat, n_tiles)
        taps = _shift_matrix(T, _halo_rows(BF16), (-1, 0, 1))

        def conv(c0):
            cols = pl.ds(c0, cw)
            return [up * w_ref[pl.ds(0, 1), cols] + uv * w_ref[pl.ds(1, 1), cols]
                    + un * w_ref[pl.ds(2, 1), cols] + b_ref[:, cols]
                    for up, uv, un in _shifted_rows(u_ref, p_ref, n_ref, cols, first, last, taps, 3, halves=True)]

        for c0 in range(0, F, cw):
            for h, (ca, cv) in enumerate(zip(conv(c0), conv(F + c0))):
                o_ref[pl.ds(h * (T // 2), T // 2), pl.ds(c0, cw)] = (ca * _sigmoid(ca) * cv).astype(BF16)

    return pl.pallas_call(
        body, name=name, grid=(n_tiles,),
        in_specs=[row, prev, nxt, pl.BlockSpec((3, F2), lambda i: (0, 0)),
                  pl.BlockSpec((1, F2), lambda i: (0, 0))],
        out_specs=pl.BlockSpec((T, F), lambda i: (i, 0)),
        out_shape=SDS((R, F), BF16), compiler_params=_params(1),
    )(u, u, u, conv_w, conv_b)


def _conv_gate_bwd(u, dgact, conv_w, conv_b, n_lat, name):
    R, F2 = u.shape
    F = F2 // 2
    n_tiles = R // ROW_TILE
    T, N = ROW_TILE, ROW_TILE + 2 * HALO
    cw = _pick(F, 256, 128)
    rowu = pl.BlockSpec((T, F2), lambda i: (i, 0))
    rowg = pl.BlockSpec((T, F), lambda i: (i, 0))
    pu, nu = _halo_specs(n_tiles, F2, rows=_halo_rows(u.dtype))
    pg, ng = _halo_specs(n_tiles, F, rows=_halo_rows(dgact.dtype))

    assert u.dtype == BF16 and dgact.dtype == BF16

    def body(u_ref, pu_ref, nu_ref, g_ref, pg_ref, ng_ref, w_ref, b_ref, du_ref, dw_ref, db_ref):
        i = pl.program_id(0)
        first, last = _edge_flags(i, n_lat, n_tiles)

        @pl.when(i == 0)
        def _():
            dw_ref[...] = jnp.zeros_like(dw_ref)
            db_ref[...] = jnp.zeros_like(db_ref)

        taps = _shift_matrix(N, _halo_rows(BF16) - HALO, (-1, 0, 1))
        same = _shift_matrix(N, _halo_rows(BF16) - HALO, (0,))

        def conv(c0):
            cols = pl.ds(c0, cw)
            up, e, un = _shifted_rows(u_ref, pu_ref, nu_ref, cols, first, last, taps, 3)
            c = (up * w_ref[pl.ds(0, 1), cols] + e * w_ref[pl.ds(1, 1), cols]
                 + un * w_ref[pl.ds(2, 1), cols] + b_ref[:, cols])
            return c, up, e, un

        def back(c0, dc, up, e, un):
            cols = pl.ds(c0, cw)
            du = (pltpu.roll(dc, N - 1, 0) * w_ref[pl.ds(0, 1), cols] + dc * w_ref[pl.ds(1, 1), cols]
                  + pltpu.roll(dc, 1, 0) * w_ref[pl.ds(2, 1), cols])
            du_ref[:, cols] = du[HALO:HALO + T].astype(BF16)
            dct = dc[HALO:HALO + T]
            dw_ref[pl.ds(0, 1), cols] += jnp.sum(dct * up[HALO:HALO + T], axis=0, keepdims=True)
            dw_ref[pl.ds(1, 1), cols] += jnp.sum(dct * e[HALO:HALO + T], axis=0, keepdims=True)
            dw_ref[pl.ds(2, 1), cols] += jnp.sum(dct * un[HALO:HALO + T], axis=0, keepdims=True)
            db_ref[:, cols] += jnp.sum(dct, axis=0, keepdims=True)

        for c0 in range(0, F, cw):
            dg, = _shifted_rows(g_ref, pg_ref, ng_ref, pl.ds(c0, cw), first, last, same, 1)
            ca, upa, ea, una = conv(c0)
            cv, upv, ev, unv = conv(F + c0)
            s = _sigmoid(ca)
            back(F + c0, dg * (ca * s), upv, ev, unv)
            back(c0, dg * cv * (s * (1.0 + ca * (1.0 - s))), upa, ea, una)

    return pl.pallas_call(
        body, name=name, grid=(n_tiles,),
        in_specs=[rowu, pu, nu, rowg, pg, ng, pl.BlockSpec((3, F2), lambda i: (0, 0)),
                  pl.BlockSpec((1, F2), lambda i: (0, 0))],
        out_specs=[rowu, pl.BlockSpec((3, F2), lambda i: (0, 0)), pl.BlockSpec((1, F2), lambda i: (0, 0))],
        out_shape=[SDS((R, F2), BF16), SDS((3, F2), F32), SDS((1, F2), F32)],
        compiler_params=_params(1),
    )(u, u, u, dgact, dgact, dgact, conv_w, conv_b)


def _pool_counts(i, n_lat, s_len, l_len, n_rows, offset):
    ctx = i >= n_lat
    t0 = jnp.where(ctx, i - n_lat, i) * ROW_TILE + offset
    seg = jnp.where(ctx, l_len, s_len)
    t = t0 + lax.broadcasted_iota(jnp.int32, (n_rows, 1), 0)
    out = []
    for win in POOL_WINDOWS:
        cnt = jnp.minimum(t + win // 2, seg) - jnp.maximum(t - win // 2, 0)
        out.append(jnp.maximum(cnt, 1).astype(F32))
    return out


def _window_sum(e, lo, hi, n):
    acc = None
    for j in range(lo, hi + 1):
        term = e if j == 0 else pltpu.roll(e, (-j) % n, 0)
        acc = term if acc is None else acc + term
    return acc


def _pool_fwd(h, w, b, scale, n_lat, s_len, l_len, name):
    R, D = h.shape
    G = D // 4
    n_tiles = R // ROW_TILE
    T, N = ROW_TILE, ROW_TILE + 2 * HALO
    row = pl.BlockSpec((T, D), lambda i: (i, 0))
    prev, nxt = _halo_specs(n_tiles, D)
    vec = pl.BlockSpec((1, D), lambda i: (0, 0))

    def body(h_ref, p_ref, n_ref, w_ref, b_ref, s_ref, y_ref):
        i = pl.program_id(0)
        first, last = _edge_flags(i, n_lat, n_tiles)
        cnts = _pool_counts(i, n_lat, s_len, l_len, T, 0)
        for g, win in enumerate(POOL_WINDOWS):
            cols = pl.ds(g * G, G)
            pr = jnp.where(first, 0.0, p_ref[:, cols])
            nx = jnp.where(last, 0.0, n_ref[:, cols])
            hv = h_ref[:, cols]
            e = jnp.concatenate([pr, hv, nx], axis=0)
            mean = _window_sum(e, -(win // 2), win // 2 - 1, N)[HALO:HALO + T] / cnts[g]
            yg = _dot(_bf(mean - hv), w_ref[g])
            y_ref[:, cols] = (yg + b_ref[:, cols]) * s_ref[:, cols]

    return pl.pallas_call(
        body, name=name, grid=(n_tiles,),
        in_specs=[row, prev, nxt, pl.BlockSpec((4, G, G), lambda i: (0, 0, 0)), vec, vec],
        out_specs=row, out_shape=SDS((R, D), F32), compiler_params=_params(1),
    )(h, h, h, w, b, scale)


def _pool_bwd(h, dy, w, b, scale, n_lat, s_len, l_len, name):
    R, D = h.shape
    G = D // 4
    n_tiles = R // ROW_TILE
    T, N = ROW_TILE, ROW_TILE + 2 * HALO
    row = pl.BlockSpec((T, D), lambda i: (i, 0))
    prev, nxt = _halo_specs(n_tiles, D)
    vec = pl.BlockSpec((1, D), lambda i: (0, 0))
    wspec = pl.BlockSpec((4, G, G), lambda i: (0, 0, 0))

    def body(h_ref, ph_ref, nh_ref, d_ref, pd_ref, nd_ref, w_ref, b_ref, s_ref,
             dh_ref, dw_ref, db_ref, ds_ref):
        i = pl.program_id(0)
        first, last = _edge_flags(i, n_lat, n_tiles)

        @pl.when(i == 0)
        def _():
            dw_ref[...] = jnp.zeros_like(dw_ref)
            db_ref[...] = jnp.zeros_like(db_ref)
            ds_ref[...] = jnp.zeros_like(ds_ref)

        cnts = _pool_counts(i, n_lat, s_len, l_len, T, 0)
        cnts_ext = _pool_counts(i, n_lat, s_len, l_len, N, -HALO)
        for g, win in enumerate(POOL_WINDOWS):
            cols = pl.ds(g * G, G)

            def ext(t_ref, p_ref, n_ref):
                pr = jnp.where(first, 0.0, p_ref[:, cols])
                nx = jnp.where(last, 0.0, n_ref[:, cols])
                return jnp.concatenate([pr, t_ref[:, cols], nx], axis=0)

            hv = h_ref[:, cols]
            mean = _window_sum(ext(h_ref, ph_ref, nh_ref), -(win // 2), win // 2 - 1, N)[HALO:HALO + T] / cnts[g]
            z = _bf(mean - hv)
            sc = s_ref[:, cols]
            dye = ext(d_ref, pd_ref, nd_ref)
            dt = _bf(dye * sc)
            dz = _dot_nt(dt, w_ref[g])
            dm = dz / cnts_ext[g]
            dh = _window_sum(dm, -(win // 2 - 1), win // 2, N) - dz
            dh_ref[:, cols] = dh[HALO:HALO + T]
            dyt = dye[HALO:HALO + T]
            dw_ref[g] += _dot_tn(z, dt[HALO:HALO + T])
            db_ref[:, cols] += jnp.sum(dyt * sc, axis=0, keepdims=True)
            ds_ref[:, cols] += jnp.sum(dyt * (_dot(z, w_ref[g]) + b_ref[:, cols]), axis=0, keepdims=True)

    return pl.pallas_call(
        body, name=name, grid=(n_tiles,),
        in_specs=[row, prev, nxt, row, prev, nxt, wspec, vec, vec],
        out_specs=[row, wspec, vec, vec],
        out_shape=[SDS((R, D), F32), SDS((4, G, G), F32), SDS((1, D), F32), SDS((1, D), F32)],
        compiler_params=_params(1),
    )(h, h, h, dy, dy, dy, w, b, scale)


def _rope_tables(s_len, l_len):
    t = jnp.arange(s_len)
    row = (t // GRID_W).astype(F32)
    col = (t % GRID_W).astype(F32)
    axis_dim = HEAD_DIM // 2
    inv = ROPE_THETA ** (-jnp.arange(0, axis_dim, 2, dtype=F32) / axis_dim)
    ar, ac = row[:, None] * inv, col[:, None] * inv
    cos = jnp.concatenate([jnp.cos(ar), jnp.cos(ar), jnp.cos(ac), jnp.cos(ac)], axis=-1)
    sin = jnp.concatenate([-jnp.sin(ar), jnp.sin(ar), -jnp.sin(ac), jnp.sin(ac)], axis=-1)
    cos = jnp.concatenate([cos, jnp.ones((l_len, HEAD_DIM), F32)], axis=0)
    sin = jnp.concatenate([sin, jnp.zeros((l_len, HEAD_DIM), F32)], axis=0)
    return cos, sin


def _swap_halves(v):
    lane = lax.broadcasted_iota(jnp.int32, v.shape, 1)
    return jnp.where((lane % 64) < 32, pltpu.roll(v, 96, 1), pltpu.roll(v, 32, 1))


def _qk_prep_fwd(qkv, q_gain, k_gain, cos, sin):
    R = qkv.shape[0]
    NQ, NK = N_HEADS * HEAD_DIM, N_KV * HEAD_DIM
    T = ROW_TILE
    vec = pl.BlockSpec((1, HEAD_DIM), lambda i: (0, 0))
    tab = pl.BlockSpec((T, HEAD_DIM), lambda i: (i, 0))

    def body(x_ref, qg_ref, kg_ref, c_ref, s_ref, q_ref, k_ref, v_ref):
        cosv, sinv = c_ref[...], s_ref[...]

        def prep(c0, gain):
            xh = x_ref[:, pl.ds(c0, HEAD_DIM)]
            xn = xh * lax.rsqrt(jnp.mean(xh * xh, axis=-1, keepdims=True) + EPS) * gain
            return _bf(xn * cosv + _swap_halves(xn) * sinv)

        for hd in range(N_HEADS):
            q_ref[:, pl.ds(hd * HEAD_DIM, HEAD_DIM)] = prep(hd * HEAD_DIM, qg_ref[...])
        for hd in range(N_KV):
            k_ref[:, pl.ds(hd * HEAD_DIM, HEAD_DIM)] = prep(NQ + hd * HEAD_DIM, kg_ref[...])
            v_ref[:, pl.ds(2 * hd * HEAD_DIM, HEAD_DIM)] = _bf(x_ref[:, pl.ds(NQ + NK + hd * HEAD_DIM, HEAD_DIM)])
            v_ref[:, pl.ds((2 * hd + 1) * HEAD_DIM, HEAD_DIM)] = jnp.ones((T, HEAD_DIM), BF16)

    return pl.pallas_call(
        body, name="qk_prep_fwd", grid=(R // T,),
        in_specs=[pl.BlockSpec((T, NQ + 2 * NK), lambda i: (i, 0)), vec, vec, tab, tab],
        out_specs=[pl.BlockSpec((T, NQ), lambda i: (i, 0)), pl.BlockSpec((T, NK), lambda i: (i, 0)),
                   pl.BlockSpec((T, 2 * NK), lambda i: (i, 0))],
        out_shape=[SDS((R, NQ), BF16), SDS((R, NK), BF16), SDS((R, 2 * NK), BF16)],
        compiler_params=_params(1),
    )(qkv, q_gain, k_gain, cos, sin)


def _qk_prep_bwd(qkv, dq, dk, dv, q_gain, k_gain, cos, sin):
    R = qkv.shape[0]
    NQ, NK = N_HEADS * HEAD_DIM, N_KV * HEAD_DIM
    T = ROW_TILE
    vec = pl.BlockSpec((1, HEAD_DIM), lambda i: (0, 0))
    tab = pl.BlockSpec((T, HEAD_DIM), lambda i: (i, 0))

    def body(x_ref, dq_ref, dk_ref, dv_ref, qg_ref, kg_ref, c_ref, s_ref, o_ref, dqg_ref, dkg_ref):
        i = pl.program_id(0)
        cosv, sinv = c_ref[...], s_ref[...]

        @pl.when(i == 0)
        def _():
            dqg_ref[...] = jnp.zeros_like(dqg_ref)
            dkg_ref[...] = jnp.zeros_like(dkg_ref)

        def back(c0, dout, gain, dg_ref):
            xh = x_ref[:, pl.ds(c0, HEAD_DIM)]
            r = lax.rsqrt(jnp.mean(xh * xh, axis=-1, keepdims=True) + EPS)
            xhat = xh * r
            dxn = dout * cosv + _swap_halves(dout * sinv)
            dg_ref[...] += jnp.sum(dxn * xhat, axis=0, keepdims=True)
            a = dxn * gain
            o_ref[:, pl.ds(c0, HEAD_DIM)] = _bf(r * (a - xhat * jnp.mean(a * xhat, axis=-1, keepdims=True)))

        for hd in range(N_HEADS):
            back(hd * HEAD_DIM, dq_ref[:, pl.ds(hd * HEAD_DIM, HEAD_DIM)], qg_ref[...], dqg_ref)
        for hd in range(N_KV):
            back(NQ + hd * HEAD_DIM, dk_ref[:, pl.ds(hd * HEAD_DIM, HEAD_DIM)], kg_ref[...], dkg_ref)
        o_ref[:, pl.ds(NQ + NK, NK)] = _bf(dv_ref[...])

    return pl.pallas_call(
        body, name="qk_prep_bwd", grid=(R // T,),
        in_specs=[pl.BlockSpec((T, NQ + 2 * NK), lambda i: (i, 0)), pl.BlockSpec((T, NQ), lambda i: (i, 0)),
                  pl.BlockSpec((T, NK), lambda i: (i, 0)), pl.BlockSpec((T, NK), lambda i: (i, 0)),
                  vec, vec, tab, tab],
        out_specs=[pl.BlockSpec((T, NQ + 2 * NK), lambda i: (i, 0)), vec, vec],
        out_shape=[SDS((R, NQ + 2 * NK), BF16), SDS((1, HEAD_DIM), F32), SDS((1, HEAD_DIM), F32)],
        compiler_params=_params(1),
    )(qkv, dq, dk, dv, q_gain, k_gain, cos, sin)


def _flash_fwd(q, k, v, s_len, l_len):
    R = q.shape[0]
    T = FLASH_FWD_TILE
    n_lat = s_len // T
    ck = _pick(s_len, 1024, 128)
    scale = HEAD_DIM ** -0.5
    group = N_HEADS // N_KV
    GW = group * HEAD_DIM
    M = group * T
    chunks = s_len // ck
    to_log2 = scale * math.log2(math.e)

    def body(q_ref, k_ref, v_ref, o_ref, lse_ref, s_s, sc_s, ml_s, mb_s, acc_s):
        i = pl.program_id(1)
        qv = jnp.concatenate([q_ref[:, pl.ds(hh * HEAD_DIM, HEAD_DIM)] for hh in range(group)], axis=0)

        ml_s[...] = jnp.full_like(ml_s, -jnp.inf)

        def lane_max(s, n):
            m = ml_s[...]
            for t in range(n // HEAD_DIM):
                m = jnp.maximum(m, s[:, t * HEAD_DIM:(t + 1) * HEAD_DIM])
            ml_s[...] = m

        @pl.when(i < n_lat)
        def _():
            def loop(c, carry):
                s = _dot_nt(qv, k_ref[pl.ds(pl.multiple_of(c * ck, ck), ck), :])
                s_s[c] = s
                lane_max(s, ck)
                return carry
            lax.fori_loop(0, chunks, loop, 0, unroll=8 if chunks % 8 == 0 else 1)

        sc = _dot_nt(qv, k_ref[pl.ds(s_len, l_len), :])
        sc_s[...] = sc
        lane_max(sc, l_len)
        m_row = jnp.max(ml_s[...], axis=-1, keepdims=True) * to_log2
        mb_s[...] = jnp.broadcast_to(m_row, (M, ck))

        acc_s[...] = jnp.zeros_like(acc_s)

        @pl.when(i < n_lat)
        def _():
            def loop(c, carry):
                p = jnp.exp2(s_s[c] * to_log2 - mb_s[...])
                acc_s[...] += _dot(_bf(p), v_ref[pl.ds(pl.multiple_of(c * ck, ck), ck), :])
                return carry
            lax.fori_loop(0, chunks, loop, 0, unroll=8 if chunks % 8 == 0 else 1)

        p = jnp.exp2(sc_s[...] * to_log2 - mb_s[:, pl.ds(0, l_len)])
        acc_s[...] += _dot(_bf(p), v_ref[pl.ds(s_len, l_len), :])
        l_rep = acc_s[:, pl.ds(HEAD_DIM, HEAD_DIM)]
        o = acc_s[:, pl.ds(0, HEAD_DIM)] / l_rep
        for hh in range(group):
            o_ref[:, pl.ds(hh * HEAD_DIM, HEAD_DIM)] = o[hh * T:(hh + 1) * T]
        lse = (mb_s[:, pl.ds(0, HEAD_DIM)] + jnp.log2(l_rep)) * math.log(2.0)
        lse_ref[...] = jnp.max(lse, axis=-1, keepdims=True).reshape(group, T, 1)

    return pl.pallas_call(
        body, name="flash_fwd", grid=(N_KV, R // T),
        in_specs=[pl.BlockSpec((T, GW), lambda g, i: (i, g)),
                  pl.BlockSpec((R, HEAD_DIM), lambda g, i: (0, g)),
                  pl.BlockSpec((R, 2 * HEAD_DIM), lambda g, i: (0, g))],
        out_specs=[pl.BlockSpec((T, GW), lambda g, i: (i, g)),
                   pl.BlockSpec((group, T, 1), lambda g, i: (g, i, 0))],
        out_shape=[SDS((R, N_HEADS * HEAD_DIM), F32), SDS((N_HEADS, R, 1), F32)],
        scratch_shapes=[pltpu.VMEM((chunks, M, ck), F32), pltpu.VMEM((M, l_len), F32), pltpu.VMEM((M, HEAD_DIM), F32),
                        pltpu.VMEM((M, ck), F32), pltpu.VMEM((M, 2 * HEAD_DIM), F32)],
        compiler_params=_params(2),
    )(q, k, v)


def _flash_bwd(q, k, v, o, lse, do, s_len, l_len):
    R = q.shape[0]
    T = ROW_TILE
    n_lat = s_len // T
    ck = _pick(s_len, 512, 128)
    scale = HEAD_DIM ** -0.5
    group = N_HEADS // N_KV
    GW = group * HEAD_DIM
    qspec = pl.BlockSpec((T, GW), lambda g, i: (i, g))
    kspec = pl.BlockSpec((R, HEAD_DIM), lambda g, i: (0, g))

    M = group * T
    log2e = math.log2(math.e)

    def body(q_ref, do_ref, o_ref, lse_ref, k_ref, v_ref, dq_ref, dk_ref, dv_ref, dq_s, lse_s, delta_s):
        i = pl.program_id(1)

        @pl.when(i == 0)
        def _():
            dk_ref[...] = jnp.zeros_like(dk_ref)
            dv_ref[...] = jnp.zeros_like(dv_ref)

        def stacked(ref):
            return jnp.concatenate([ref[:, pl.ds(hh * HEAD_DIM, HEAD_DIM)] for hh in range(group)], axis=0)

        qv = stacked(q_ref)
        dov = stacked(do_ref)
        dob = _bf(dov)
        delta_s[...] = jnp.broadcast_to(jnp.sum(dov * stacked(o_ref), axis=-1, keepdims=True), (M, ck))
        lse_s[...] = jnp.broadcast_to(lse_ref[...].reshape(M, 1) * log2e, (M, ck))
        dq_s[...] = jnp.zeros_like(dq_s)

        def step(rows, n):
            kv, vv = k_ref[rows, :], v_ref[rows, :]
            p = jnp.exp2(_dot_nt(qv, kv) * (scale * log2e) - lse_s[:, pl.ds(0, n)])
            dv_ref[rows, :] += _dot_tn(_bf(p), dob)
            ds = _bf(p * (_dot_nt(dob, vv) - delta_s[:, pl.ds(0, n)]) * scale)
            dq_s[...] += _dot(ds, kv)
            dk_ref[rows, :] += _dot_tn(ds, qv)

        @pl.when(i < n_lat)
        def _():
            def loop(c, carry):
                step(pl.ds(pl.multiple_of(c * ck, ck), ck), ck)
                return carry
            lax.fori_loop(0, s_len // ck, loop, 0, unroll=4 if (s_len // ck) % 4 == 0 else 1)

        step(pl.ds(s_len, l_len), l_len)
        for hh in range(group):
            dq_ref[:, pl.ds(hh * HEAD_DIM, HEAD_DIM)] = dq_s[pl.ds(hh * T, T), :]

    return pl.pallas_call(
        body, name="flash_bwd", grid=(N_KV, R // T),
        in_specs=[qspec, qspec, qspec, pl.BlockSpec((group, T, 1), lambda g, i: (g, i, 0)), kspec,
                  pl.BlockSpec((R, HEAD_DIM), lambda g, i: (0, 2 * g))],
        out_specs=[qspec, kspec, kspec],
        out_shape=[SDS((R, N_HEADS * HEAD_DIM), F32), SDS((R, N_KV * HEAD_DIM), F32),
                   SDS((R, N_KV * HEAD_DIM), F32)],
        scratch_shapes=[pltpu.VMEM((M, HEAD_DIM), F32), pltpu.VMEM((M, ck), F32), pltpu.VMEM((M, ck), F32)],
        compiler_params=_params(2),
    )(q, do, o, lse, k, v)


K_SCALE = RET_DK ** -0.5


def _log_sigmoid(v):
    return -(jnp.maximum(-v, 0.0) + jnp.log(1.0 + jnp.exp(-jnp.abs(v))))


def _ret_decays(d, lg):
    C = RET_CHUNK
    ic = lax.broadcasted_iota(jnp.int32, (C, 1), 0)
    ir = lax.broadcasted_iota(jnp.int32, (1, C), 1)
    li = jnp.where(d == 0, ic, C - 1 - ic).astype(F32)
    lj = jnp.where(d == 0, ir, C - 1 - ir).astype(F32)
    diff = li - lj
    mask = jnp.where(diff >= 0, jnp.exp(jnp.maximum(diff, 0.0) * lg), 0.0)
    qd = jnp.exp((li + 1.0) * lg)
    kd = jnp.exp((C - 1.0 - li) * lg)
    cd = jnp.exp(C * lg)
    return li, diff, mask, qd, kd, cd


def _ctx_weights(d, t, lg, l_len):
    C = RET_CHUNK
    j = (t * C + lax.broadcasted_iota(jnp.int32, (C, 1), 0)).astype(F32)
    e = jnp.where(d == 0, (l_len - 1.0) - j, j)
    return e, jnp.exp(e * lg)


def _mirrored(x, i, n_lat):
    return jnp.where(i < n_lat, jnp.concatenate([x[RET_CHUNK:], x[:RET_CHUNK]], axis=0), x)


def _mirror_tile(n_lat):
    return lambda i: jnp.where(i < n_lat, n_lat - 1 - i, i)


def _ret_fwd(proj, lgt, s_len, l_len):
    R = proj.shape[0]
    C, H, DK, DV = RET_CHUNK, RET_HEADS, RET_DK, RET_DV
    nl, nc = s_len // C, l_len // C

    def stored(t):
        return jnp.where(t < nc, nl + t, jnp.maximum(t - nc, 0))

    def actual(d, t):
        n = jnp.maximum(t - nc, 0)
        return jnp.where(t < nc, nl + t, n if d == 0 else nl - 1 - n)

    def body(q0_ref, k0_ref, v0_ref, q1_ref, k1_ref, v1_ref, lg_ref, o_ref, st_ref, r_s):
        t = pl.program_id(0)
        qkv = ((q0_ref, k0_ref, v0_ref), (q1_ref, k1_ref, v1_ref))

        @pl.when(t == 0)
        def _():
            r_s[...] = jnp.zeros_like(r_s)

        def log_gamma(d, hh):
            return jnp.max(_log_sigmoid(lg_ref[d, hh]), axis=-1, keepdims=True)

        @pl.when(t < nc)
        def _():
            for d, (q_ref, k_ref, v_ref) in enumerate(qkv):
                for hh in range(H):
                    qc, vc = pl.ds(hh * DK, DK), pl.ds(hh * DV, DV)
                    _, w = _ctx_weights(d, t, log_gamma(d, hh), l_len)
                    r_s[d, hh] += _dot_tn(_bf(k_ref[:, qc] * K_SCALE * w), _bf(v_ref[:, vc]))
                    o_ref[d, :, vc] = jnp.zeros((C, DV), F32)

        @pl.when(t >= nc)
        def _():
            for d, (q_ref, k_ref, v_ref) in enumerate(qkv):
                for hh in range(H):
                    qc, vc = pl.ds(hh * DK, DK), pl.ds(hh * DV, DV)
                    _, _, mask, qd, kd, cd = _ret_decays(d, log_gamma(d, hh))
                    qb, kv, vb = _bf(q_ref[:, qc]), k_ref[:, qc] * K_SCALE, _bf(v_ref[:, vc])
                    r = r_s[d, hh]
                    st_ref[d, hh, 0] = r
                    att = _dot_nt(qb, _bf(kv)) * mask
                    o_ref[d, :, vc] = _dot(_bf(att), vb) + _dot(qb, _bf(r)) * qd
                    r_s[d, hh] = r * cd + _dot_tn(_bf(kv * kd), vb)

    def rows(d):
        return [pl.BlockSpec((C, H * DK), lambda t: (actual(d, t), 0)),
                pl.BlockSpec((C, H * DK), lambda t: (actual(d, t), 1)),
                pl.BlockSpec((C, H * DV), lambda t: (actual(d, t), 1))]

    return pl.pallas_call(
        body, name="ret_fwd", grid=(nc + nl,),
        in_specs=rows(0) + rows(1) + [pl.BlockSpec((2, H, 1, 128), lambda t: (0, 0, 0, 0))],
        out_specs=[pl.BlockSpec((2, C, H * DV), lambda t: (0, stored(t), 0)),
                   pl.BlockSpec((2, H, 1, DK, DV), lambda t: (0, 0, jnp.maximum(t - nc, 0), 0, 0))],
        out_shape=[SDS((2, R, H * DV), F32), SDS((2, H, nl, DK, DV), F32)],
        scratch_shapes=[pltpu.VMEM((2, H, DK, DV), F32)],
        compiler_params=_params(1),
    )(proj, proj, proj, proj, proj, proj, lgt)


def _ret_bwd(proj, lgt, states, do, s_len, l_len):
    R = proj.shape[0]
    C, H, DK, DV = RET_CHUNK, RET_HEADS, RET_DK, RET_DV
    nl, nc = s_len // C, l_len // C
    last = nl + nc - 1

    def stored(t):
        return jnp.where(t < nl, jnp.maximum(nl - 1 - t, 0), t)

    def actual(d, t):
        return stored(t) if d == 0 else t

    def body(q0_ref, k0_ref, v0_ref, do0_ref, q1_ref, k1_ref, v1_ref, do1_ref, lg_ref, st_ref,
             dq_ref, dk_ref, dv_ref, dlg_ref, dr_s, dl_s):
        t = pl.program_id(0)
        ins = ((q0_ref, k0_ref, v0_ref, do0_ref), (q1_ref, k1_ref, v1_ref, do1_ref))

        def log_gamma(d, hh):
            return jnp.max(_log_sigmoid(lg_ref[d, hh]), axis=-1, keepdims=True)

        @pl.when(t == 0)
        def _():
            dr_s[...] = jnp.zeros_like(dr_s)
            dl_s[...] = jnp.zeros_like(dl_s)

        @pl.when(t < nl)
        def _():
            for d, (q_ref, k_ref, v_ref, do_ref) in enumerate(ins):
                for hh in range(H):
                    qc, vc = pl.ds(hh * DK, DK), pl.ds(hh * DV, DV)
                    li, diff, mask, qd, kd, cd = _ret_decays(d, log_gamma(d, hh))
                    qv, kv, vv, dov = q_ref[:, qc], k_ref[:, qc] * K_SCALE, v_ref[:, vc], do_ref[:, vc]
                    qb, kb, vb, dob = _bf(qv), _bf(kv), _bf(vv), _bf(dov)
                    r, drn = st_ref[d, hh, 0], dr_s[d, hh]
                    rb, drb = _bf(r), _bf(drn)
                    p = _dot_nt(qb, kb)
                    dp = _dot_nt(dob, vb) * mask
                    dpb = _bf(dp)
                    doq = _bf(dov * qd)
                    dq_inter = _dot_nt(doq, rb)
                    dk_state = kd * _dot_nt(vb, drb)
                    dq_ref[d, :, qc] = _dot(dpb, kb) + dq_inter
                    dk_ref[d, :, qc] = (_dot_tn(dpb, qb) + dk_state) * K_SCALE
                    dv_ref[d, :, vc] = _dot_tn(_bf(p * mask), dob) + _dot(_bf(kv * kd), drb)
                    dr_s[d, hh] = cd * drn + _dot_tn(qb, doq)
                    dl_s[d, hh] += (jnp.sum(dp * p * diff) + jnp.sum((li + 1.0) * qv * dq_inter)
                                    + jnp.sum((C - 1.0 - li) * kv * dk_state) + C * jnp.sum(cd * r * drn))

        @pl.when(t >= nl)
        def _():
            for d, (q_ref, k_ref, v_ref, do_ref) in enumerate(ins):
                for hh in range(H):
                    qc, vc = pl.ds(hh * DK, DK), pl.ds(hh * DV, DV)
                    e, w = _ctx_weights(d, t - nl, log_gamma(d, hh), l_len)
                    kv, vb, drb = k_ref[:, qc] * K_SCALE, _bf(v_ref[:, vc]), _bf(dr_s[d, hh])
                    dkc = w * _dot_nt(vb, drb)
                    dq_ref[d, :, qc] = jnp.zeros((C, DK), F32)
                    dk_ref[d, :, qc] = dkc * K_SCALE
                    dv_ref[d, :, vc] = _dot(_bf(kv * w), drb)
                    dl_s[d, hh] += jnp.sum(e * kv * dkc)

        @pl.when(t == last)
        def _():
            for d in range(2):
                for hh in range(H):
                    dlg_ref[d, hh] = dl_s[d, hh] * (1.0 / (1.0 + jnp.exp(lg_ref[d, hh])))

    def rows(d):
        return [pl.BlockSpec((C, H * DK), lambda t: (actual(d, t), 0)),
                pl.BlockSpec((C, H * DK), lambda t: (actual(d, t), 1)),
                pl.BlockSpec((C, H * DV), lambda t: (actual(d, t), 1)),
                pl.BlockSpec((C, H * DV), lambda t: (actual(d, t), 0))]

    return pl.pallas_call(
        body, name="ret_bwd", grid=(nl + nc,),
        in_specs=rows(0) + rows(1) + [
            pl.BlockSpec((2, H, 1, 128), lambda t: (0, 0, 0, 0)),
            pl.BlockSpec((2, H, 1, DK, DV), lambda t: (0, 0, jnp.maximum(nl - 1 - t, 0), 0, 0))],
        out_specs=[pl.BlockSpec((2, C, H * DK), lambda t: (0, stored(t), 0)),
                   pl.BlockSpec((2, C, H * DK), lambda t: (0, stored(t), 0)),
                   pl.BlockSpec((2, C, H * DV), lambda t: (0, stored(t), 0)),
                   pl.BlockSpec((2, H, 1, 128), lambda t: (0, 0, 0, 0))],
        out_shape=[SDS((2, R, H * DK), F32), SDS((2, R, H * DK), F32), SDS((2, R, H * DV), F32),
                   SDS((2, H, 1, 128), F32)],
        scratch_shapes=[pltpu.VMEM((2, H, DK, DV), F32), pltpu.VMEM((2, H, 1, 128), F32)],
        compiler_params=_params(1),
    )(proj, proj, proj, do, proj, proj, proj, do, lgt, states)


def _readout_fwd(o2, proj, gn_w, n_lat):
    R = proj.shape[0]
    H, DV = RET_HEADS, RET_DV
    W = H * DV
    T = ROW_TILE
    assert T == 2 * RET_CHUNK

    def body(o_ref, ob_ref, g_ref, w_ref, out_ref):
        i = pl.program_id(0)
        for hh in range(H):
            cols = pl.ds(hh * DV, DV)
            y = o_ref[0, :, cols] + _mirrored(ob_ref[0, :, cols], i, n_lat)
            yc = y - jnp.mean(y, axis=-1, keepdims=True)
            yn = yc * lax.rsqrt(jnp.mean(yc * yc, axis=-1, keepdims=True) + EPS) * w_ref[:, cols]
            g = g_ref[:, cols]
            out_ref[:, cols] = _bf(g * _sigmoid(g) * yn)

    return pl.pallas_call(
        body, name="readout_fwd", grid=(R // T,),
        in_specs=[pl.BlockSpec((1, T, W), lambda i: (0, i, 0)),
                  pl.BlockSpec((1, T, W), lambda i: (1, _mirror_tile(n_lat)(i), 0)),
                  pl.BlockSpec((T, W), lambda i: (i, 2)), pl.BlockSpec((1, W), lambda i: (0, 0))],
        out_specs=pl.BlockSpec((T, W), lambda i: (i, 0)),
        out_shape=SDS((R, W), BF16), compiler_params=_params(1),
    )(o2, o2, proj, gn_w)


def _readout_bwd(o2, proj, gn_w, dgated, n_lat):
    R = proj.shape[0]
    H, DV = RET_HEADS, RET_DV
    W = H * DV
    T = ROW_TILE

    def body(o_ref, ob_ref, g_ref, w_ref, d_ref, do_ref, dg_ref, dw_ref):
        i = pl.program_id(0)

        @pl.when(i == 0)
        def _():
            dw_ref[...] = jnp.zeros_like(dw_ref)

        for hh in range(H):
            cols = pl.ds(hh * DV, DV)
            y = o_ref[0, :, cols] + _mirrored(ob_ref[0, :, cols], i, n_lat)
            yc = y - jnp.mean(y, axis=-1, keepdims=True)
            rstd = lax.rsqrt(jnp.mean(yc * yc, axis=-1, keepdims=True) + EPS)
            yn0 = yc * rstd
            wv = w_ref[:, cols]
            g = g_ref[:, cols]
            s = _sigmoid(g)
            dgt = d_ref[:, cols]
            dyn = dgt * (g * s)
            dg_ref[:, cols] = _bf(dgt * (yn0 * wv) * (s * (1.0 + g * (1.0 - s))))
            dw_ref[:, cols] += jnp.sum(dyn * yn0, axis=0, keepdims=True)
            a = dyn * wv
            do_ref[:, cols] = rstd * (a - jnp.mean(a, axis=-1, keepdims=True)
                                      - yn0 * jnp.mean(a * yn0, axis=-1, keepdims=True))

    return pl.pallas_call(
        body, name="readout_bwd", grid=(R // T,),
        in_specs=[pl.BlockSpec((1, T, W), lambda i: (0, i, 0)),
                  pl.BlockSpec((1, T, W), lambda i: (1, _mirror_tile(n_lat)(i), 0)),
                  pl.BlockSpec((T, W), lambda i: (i, 2)),
                  pl.BlockSpec((1, W), lambda i: (0, 0)), pl.BlockSpec((T, W), lambda i: (i, 0))],
        out_specs=[pl.BlockSpec((T, W), lambda i: (i, 0)), pl.BlockSpec((T, W), lambda i: (i, 0)),
                   pl.BlockSpec((1, W), lambda i: (0, 0))],
        out_shape=[SDS((R, W), F32), SDS((R, W), BF16), SDS((1, W), F32)],
        compiler_params=_params(1),
    )(o2, o2, proj, gn_w, dgated)


def _ret_dproj(dq2, dk2, dv2, dg, n_lat):
    R = dg.shape[0]
    NQ, NV = RET_HEADS * RET_DK, RET_HEADS * RET_DV
    T = ROW_TILE

    def body(dq_ref, dqb_ref, dk_ref, dkb_ref, dv_ref, dvb_ref, dg_ref, o_ref):
        i = pl.program_id(0)
        o_ref[:, pl.ds(0, NQ)] = _bf(dq_ref[0] + _mirrored(dqb_ref[0], i, n_lat))
        o_ref[:, pl.ds(NQ, NQ)] = _bf(dk_ref[0] + _mirrored(dkb_ref[0], i, n_lat))
        o_ref[:, pl.ds(2 * NQ, NV)] = _bf(dv_ref[0] + _mirrored(dvb_ref[0], i, n_lat))
        o_ref[:, pl.ds(2 * NQ + NV, NV)] = dg_ref[...]

    def both(width):
        return [pl.BlockSpec((1, T, width), lambda i: (0, i, 0)),
                pl.BlockSpec((1, T, width), lambda i: (1, _mirror_tile(n_lat)(i), 0))]

    return pl.pallas_call(
        body, name="ret_dproj", grid=(R // T,),
        in_specs=both(NQ) + both(NQ) + both(NV) + [pl.BlockSpec((T, NV), lambda i: (i, 0))],
        out_specs=pl.BlockSpec((T, 2 * NQ + 2 * NV), lambda i: (i, 0)),
        out_shape=SDS((R, 2 * NQ + 2 * NV), BF16), compiler_params=_params(1),
    )(dq2, dq2, dk2, dk2, dv2, dv2, dg)


def _silu(v):
    return v * _sigmoid(v)


def _ada_fwd(c_rows, ada_w, ada_b_shard):
    depth, D, cols = ada_w.shape

    def body(c_ref, w_ref, b_ref, o_ref):
        o_ref[0] = _dot(_bf(_silu(c_ref[...])), _bf(w_ref[0])) + b_ref[0]

    return pl.pallas_call(
        body, name="ada_fwd", grid=(depth,),
        in_specs=[pl.BlockSpec((16, D), lambda i: (0, 0)), pl.BlockSpec((1, D, cols), lambda i: (i, 0, 0)),
                  pl.BlockSpec((1, 1, cols), lambda i: (i, 0, 0))],
        out_specs=pl.BlockSpec((1, 16, cols), lambda i: (i, 0, 0)),
        out_shape=SDS((depth, 16, cols), F32), compiler_params=_params(1),
    )(c_rows, ada_w, ada_b_shard)


def _ada_bwd(c_rows, ada_w, d_lat, d_ctx):
    depth, D, cols = ada_w.shape

    def body(c_ref, w_ref, dl_ref, dc_ref, dw_ref, pc_ref):
        i = pl.program_id(0)
        cv = c_ref[...]
        a = _silu(cv)
        dcs = jnp.broadcast_to(jnp.sum(dc_ref[0], axis=0, keepdims=True), (8, cols))
        dw_ref[0] = _dot_tn(_bf(a[0:8]), _bf(dl_ref[0])) + _dot_tn(_bf(a[8:16]), _bf(dcs))

        @pl.when(i == 0)
        def _():
            pc_ref[...] = jnp.zeros_like(pc_ref)

        pc_ref[...] += _dot_nt(_bf(dcs), _bf(w_ref[0]))

        @pl.when(i == depth - 1)
        def _():
            cc = c_ref[pl.ds(8, 1), :]
            s = _sigmoid(cc)
            pc_ref[...] = pc_ref[...] * (s * (1.0 + cc * (1.0 - s)))

    return pl.pallas_call(
        body, name="ada_bwd", grid=(depth,),
        in_specs=[pl.BlockSpec((16, D), lambda i: (0, 0)), pl.BlockSpec((1, D, cols), lambda i: (i, 0, 0)),
                  pl.BlockSpec((1, 8, cols), lambda i: (i, 0, 0)), pl.BlockSpec((1, 8, cols), lambda i: (i, 0, 0))],
        out_specs=[pl.BlockSpec((1, D, cols), lambda i: (i, 0, 0)), pl.BlockSpec((8, D), lambda i: (0, 0))],
        out_shape=[SDS((depth, D, cols), F32), SDS((8, D), F32)], compiler_params=_params(1),
    )(c_rows, ada_w, d_lat, d_ctx)


def _adamw(w, g, m, v, name):
    shape = w.shape
    n = g.shape[0]
    cols = shape[-1]
    rows = w.size // cols
    tr = _pick(rows, 512, 8) if rows * cols * 4 > (1 << 20) else rows
    spec = pl.BlockSpec((tr, cols), lambda i: (i, 0))

    def body(w_ref, g_ref, m_ref, v_ref, go_ref, d_ref, mo_ref, vo_ref):
        gs = g_ref[0].astype(F32)
        for k in range(1, n):
            gs = gs + g_ref[k].astype(F32)
        mn = ADAM_B1 * m_ref[...] + (1.0 - ADAM_B1) * gs
        vn = ADAM_B2 * v_ref[...] + (1.0 - ADAM_B2) * jnp.square(gs)
        m_hat = mn / (1.0 - ADAM_B1 ** ADAM_STEP)
        v_hat = vn / (1.0 - ADAM_B2 ** ADAM_STEP)
        go_ref[...] = gs
        d_ref[...] = -ADAM_LR * (m_hat / (jnp.sqrt(v_hat) + ADAM_EPS) + ADAM_WD * w_ref[...])
        mo_ref[...] = mn
        vo_ref[...] = vn

    outs = pl.pallas_call(
        body, name=name, grid=(rows // tr,),
        in_specs=[spec, pl.BlockSpec((n, tr, cols), lambda i: (0, i, 0)), spec, spec],
        out_specs=[spec] * 4, out_shape=[SDS((rows, cols), F32)] * 4, compiler_params=_params(1),
    )(w.reshape(rows, cols), g.reshape(n, rows, cols), m.reshape(rows, cols), v.reshape(rows, cols))
    return tuple(o.reshape(shape) for o in outs)


def _sum_slots(own, recv, name):
    shape, n, cols = own.shape, recv.shape[0], own.shape[-1]
    own, recv = own.reshape(-1, cols), recv.reshape(n, -1, cols)
    rows = own.shape[0]
    tr = _pick(rows, 512, 16)

    def body(own_ref, r_ref, o_ref):
        acc = own_ref[...].astype(F32)
        for k in range(n):
            acc = acc + r_ref[k].astype(F32)
        o_ref[...] = acc

    return pl.pallas_call(
        body, name=name, grid=(rows // tr,),
        in_specs=[pl.BlockSpec((tr, cols), lambda i: (i, 0)), pl.BlockSpec((n, tr, cols), lambda i: (0, i, 0))],
        out_specs=pl.BlockSpec((tr, cols), lambda i: (i, 0)),
        out_shape=SDS((rows, cols), F32), compiler_params=_params(1),
    )(own, recv).reshape(shape)


def _position():
    return lax.axis_index("x"), lax.axis_index("y"), lax.axis_index("c")


def _peer(k, x, y, c):
    return (1 - x if k & 4 else x, 1 - y if k & 2 else y, 1 - c if k & 1 else c)


def _index(pos):
    return 4 * pos[0] + 2 * pos[1] + pos[2]


def _gather_small(v, name):
    rows, lanes = v.shape

    def body(x_ref, out_ref, send_sems, recv_sems, local_sem):
        me = _position()
        mine = pltpu.make_async_copy(x_ref, out_ref.at[_index(me)], local_sem)
        mine.start()

        def copy(k, slot):
            return pltpu.make_async_remote_copy(
                src_ref=x_ref, dst_ref=out_ref.at[slot], send_sem=send_sems.at[k - 1],
                recv_sem=recv_sems.at[k - 1], device_id=_peer(k, *me), device_id_type=MESH)

        sends = [copy(k, _index(me)) for k in range(1, N_DEV)]
        for cp in sends:
            cp.start()
        for k in range(1, N_DEV):
            copy(k, _index(_peer(k, *me))).wait_recv()
        for cp in sends:
            cp.wait_send()
        mine.wait()

    return pl.pallas_call(
        body, name=name, out_shape=SDS((N_DEV, rows, lanes), v.dtype),
        in_specs=[pl.BlockSpec(memory_space=pltpu.VMEM)],
        out_specs=pl.BlockSpec(memory_space=pltpu.VMEM),
        scratch_shapes=[pltpu.SemaphoreType.DMA((N_DEV - 1,)), pltpu.SemaphoreType.DMA((N_DEV - 1,)),
                        pltpu.SemaphoreType.DMA],
        compiler_params=pltpu.CompilerParams(vmem_limit_bytes=VMEM_LIMIT_V7X),
    )(v)


HBM_SPEC = pl.BlockSpec(memory_space=pltpu.HBM)
SEM_SPEC = pl.BlockSpec(memory_space=pltpu.SEMAPHORE)
SPLIT_EFFECT = pltpu.SideEffectType.DATAFLOW_SIDE_EFFECTING


def _split_start(srcs, gather, name):
    n = len(srcs)
    lands = [jnp.zeros(((N_DEV,) + s.shape) if gather else s.shape, s.dtype) for s in srcs]

    def body(*refs):
        src_refs, land_refs, sems, token = refs[:n], refs[n:2 * n], refs[2 * n:4 * n], refs[-1]
        me = _position()
        for a in range(n):
            for k in range(1, N_DEV):
                peer = _peer(k, *me)
                pltpu.make_async_remote_copy(
                    src_ref=src_refs[a] if gather else src_refs[a].at[_index(peer)],
                    dst_ref=land_refs[a].at[_index(me)], send_sem=sems[2 * a], recv_sem=sems[2 * a + 1],
                    device_id=peer, device_id_type=MESH).start()
        token[...] = jnp.zeros_like(token)

    hbm = lambda arrays: tuple(pltpu.HBM(a.shape, a.dtype) for a in arrays)
    outs = pl.pallas_call(
        body, name=name,
        out_shape=(pltpu.SemaphoreType.DMA(()),) * (2 * n) + hbm(srcs) + hbm(lands) + (SDS((8, 128), F32),),
        in_specs=(HBM_SPEC,) * (2 * n),
        out_specs=(SEM_SPEC,) * (2 * n) + (HBM_SPEC,) * (2 * n) + (pl.BlockSpec(memory_space=pltpu.VMEM),),
        input_output_aliases={a: 2 * n + a for a in range(2 * n)},
        compiler_params=pltpu.CompilerParams(has_side_effects=SPLIT_EFFECT),
    )(*[pltpu.with_memory_space_constraint(a, pltpu.HBM) for a in list(srcs) + lands])
    return outs[:2 * n], outs[2 * n:3 * n], outs[3 * n:4 * n], outs[-1]


def _split_wait(flight, after, name):
    sems, srcs, lands, _ = flight
    n = len(srcs)

    def body(*refs):
        land_refs, sem_refs = refs[n:2 * n], refs[2 * n:4 * n]
        me = _position()
        for a in range(n):
            seven = land_refs[a].at[pl.ds(0, N_DEV - 1)]
            copies = pltpu.make_async_remote_copy(
                src_ref=seven, dst_ref=seven, send_sem=sem_refs[2 * a], recv_sem=sem_refs[2 * a + 1],
                device_id=_peer(1, *me), device_id_type=MESH)
            copies.wait_send()
            copies.wait_recv()

    outs = pl.pallas_call(
        body, name=name,
        out_shape=tuple(pltpu.HBM(a.shape, a.dtype) for a in list(srcs) + list(lands)),
        in_specs=(HBM_SPEC,) * (2 * n) + (SEM_SPEC,) * (2 * n) + (pl.BlockSpec(memory_space=pl.ANY),),
        out_specs=(HBM_SPEC,) * (2 * n), input_output_aliases={a: a for a in range(2 * n)},
        compiler_params=pltpu.CompilerParams(has_side_effects=SPLIT_EFFECT),
    )(*srcs, *lands, *sems, after)
    return outs[:n], outs[n:]


def _pack_rows(arrays, lanes, dtype):
    flat = jnp.concatenate([a.astype(dtype).reshape(-1) for a in arrays])
    pad = (-flat.size) % (16 * lanes)
    if pad:
        flat = jnp.concatenate([flat, jnp.zeros((pad,), dtype)])
    return flat.reshape(-1, lanes)


def _unpack_rows(packed, shapes):
    n = packed.shape[0]
    flat = packed.reshape(n, -1)
    out, off = [], 0
    for shp in shapes:
        size = math.prod(shp)
        out.append(flat[:, off:off + size].reshape((n,) + tuple(shp)))
        off += size
    return out


def _unshard(g8, axis):
    moved = jnp.moveaxis(g8, 0, axis)
    shp = list(moved.shape)
    shp[axis:axis + 2] = [shp[axis] * shp[axis + 1]]
    return moved.reshape(shp)


def _split8(full, axis):
    shp = list(full.shape)
    shp[axis:axis + 1] = [N_DEV, shp[axis] // N_DEV]
    return jnp.moveaxis(full.reshape(shp), axis, 0)


def _my_shard(g, axis, me):
    size = g.shape[axis + 1] // N_DEV
    return lax.dynamic_slice_in_dim(g, me * size, size, axis=axis + 1)


BIG_WEIGHTS = ("ffn_w_up", "ffn_w_down", "attn_w_qkv", "attn_w_o", "ret_w_in", "ret_w_out", "pool_w")
LAYER_WEIGHTS = (
    (("ffn_w_up", 0, "cols"), ("ffn_w_down", 0, "rows"), ("pool_w", 0, "pool")),
    (("ffn_w_up", 1, "cols"), ("ffn_w_down", 1, "rows"), ("attn_w_qkv", 0, "cols"), ("attn_w_o", 0, "rows")),
    (("ffn_w_up", 2, "cols"), ("ffn_w_down", 2, "rows"), ("ret_w_in", 0, "cols"), ("ret_w_out", 0, "rows")),
    (("ffn_w_up", 3, "cols"), ("ffn_w_down", 3, "rows"), ("pool_w", 1, "pool")),
)


GATHER_GROUPS = (LAYER_WEIGHTS[0][:2],) + LAYER_WEIGHTS[1:]
GRAD_GROUPS = {"3": LAYER_WEIGHTS[3], "2": LAYER_WEIGHTS[2], "1": LAYER_WEIGHTS[1],
               "0ffn": LAYER_WEIGHTS[0][:2], "0mix": LAYER_WEIGHTS[0][2:]}


def _shard_to_send(w, kind):
    w = w.astype(BF16)
    return w.T if kind == "cols" else w


def _full_from_land(land, kind):
    return _unshard(land, 1) if kind == "pool" else land.reshape(-1, land.shape[-1])


def _grad_to_send(g, kind):
    return _split8(g, 1).astype(BF16) if kind == "pool" else g.astype(BF16).reshape(N_DEV, -1, g.shape[-1])


def _shard_grad(gsum, kind):
    return gsum.T if kind == "cols" else gsum
SMALL_SHARDED = (("norm_w", 2), ("pool_b", 1), ("pool_scale", 1), ("ret_gn_w", 1), ("ffn_conv_w", 2))
REPLICATED = ("ada_b", "attn_q_gain", "attn_k_gain", "ret_decay_logit", "ffn_conv_b")
WEIGHT_ORDER = ("c_ctx", "ada_w", "ada_b", "norm_w", "pool_w", "pool_b", "pool_scale", "attn_w_qkv",
                "attn_q_gain", "attn_k_gain", "attn_w_o", "ret_w_in", "ret_decay_logit", "ret_gn_w",
                "ret_w_out", "ffn_w_up", "ffn_conv_w", "ffn_conv_b", "ffn_w_down")


def _local_step(x0, target, mods, P, get_weights, put_grads, s_len, l_len):
    n_lat = s_len // ROW_TILE
    nw = P["norm_w"]
    lgt = jnp.broadcast_to(P["ret_decay_logit"][0][:, :, None, None], (2, RET_HEADS, 1, 128))
    cos, sin = _rope_tables(s_len, l_len)
    h_dtype = [F32 if i % 3 == 0 else BF16 for i in range(DEPTH)]
    saved = []
    mods = list(mods)
    X = x0
    h = _res_norm(X, None, None, 0, nw[0, 0], mods[0], 0, h_dtype[0], n_lat, "norm_first")
    for i in range(DEPTH):
        kind, j, mod = i % 3, i // 3, mods[i]
        W, zero = get_weights(i, "mix", h)
        W, mod = dict(W), mod + zero
        sv = {"X": X, "h": h, "W": W}
        if kind == 0:
            y = _pool_fwd(h, W["pool_w"], P["pool_b"][j:j + 1], P["pool_scale"][j:j + 1],
                          n_lat, s_len, l_len, f"pool_fwd{i}")
        elif kind == 1:
            qkv = _mm(h, W["attn_w_qkv"], "nt", F32, f"qkv{i}")
            q, k, v = _qk_prep_fwd(qkv, P["attn_q_gain"][j:j + 1], P["attn_k_gain"][j:j + 1], cos, sin)
            o, lse = _flash_fwd(q, k, v, s_len, l_len)
            y = _mm(o, W["attn_w_o"], "nn", F32, f"attn_out{i}")
            sv.update(qkv=qkv, q=q, k=k, v=v, o=o, lse=lse)
        else:
            proj = _mm(h, W["ret_w_in"], "nt", F32, f"ret_in{i}")
            o2, states = _ret_fwd(proj, lgt, s_len, l_len)
            gated = _readout_fwd(o2, proj, P["ret_gn_w"][j:j + 1], n_lat)
            y = _mm(gated, W["ret_w_out"], "nn", F32, f"ret_out{i}")
            sv.update(proj=proj, o2=o2, states=states, gated=gated)
        X1, h2 = _res_norm(X, y, mod, 0, nw[i, 1], mod, 1, BF16, n_lat, f"res_norm_mid{i}")
        W_ffn, zero = get_weights(i, "ffn", h2)
        W.update(W_ffn)
        mod = mod + zero
        u = _mm(h2, W["ffn_w_up"], "nt", FFN_HIDDEN_DTYPE, f"ffn_up{i}")
        gact = _conv_gate_fwd(u, P["ffn_conv_w"][i], P["ffn_conv_b"][i:i + 1], n_lat, f"conv_gate_fwd{i}")
        f = _mm(gact, W["ffn_w_down"], "nn", F32, f"ffn_down{i}")
        sv.update(y=y, X1=X1, h2=h2, u=u, gact=gact, f=f)
        saved.append(sv)
        if i + 1 < DEPTH:
            X, h = _res_norm(X1, f, mod, 1, nw[i + 1, 0], mods[i + 1], 0, h_dtype[i + 1], n_lat,
                             f"res_norm_end{i}")
        else:
            X = _res_norm(X1, f, mod, 1, None, None, 0, None, n_lat, "res_last")

    dX, loss = _loss_bwd(X, target, n_lat)
    G = {name: [None] * P[name].shape[0] for name in
         ("pool_b", "pool_scale", "attn_q_gain", "attn_k_gain", "ret_decay_logit", "ret_gn_w", "ffn_conv_w",
          "ffn_conv_b")}
    dnw = [[None, None] for _ in range(DEPTH)]
    dmods = [None] * DEPTH
    for i in reversed(range(DEPTH)):
        kind, j, mod, sv = i % 3, i // 3, mods[i], saved[i]
        W, gl = sv["W"], {}
        if i == DEPTH - 1:
            df, dg2 = _gate_bwd(dX, sv["f"], mod, 1, BF16, n_lat, f"gate_bwd_ffn{i}")
        dgact = _mm(df, W["ffn_w_down"], "nt", FFN_HIDDEN_DTYPE, f"ffn_down_dx{i}")
        gl["ffn_w_down"] = _mm(sv["gact"], df, "tn", BF16, f"ffn_down_dw{i}")
        du, dcw, dcb = _conv_gate_bwd(sv["u"], dgact, P["ffn_conv_w"][i], P["ffn_conv_b"][i:i + 1], n_lat,
                                      f"conv_gate_bwd{i}")
        G["ffn_conv_w"][i], G["ffn_conv_b"][i] = dcw, dcb[0]
        dh2 = _mm(du, W["ffn_w_up"], "nn", F32, f"ffn_up_dx{i}")
        gl["ffn_w_up"] = _mm(du, sv["h2"], "tn", BF16, f"ffn_up_dw{i}")
        if i == 0:
            mod = mod + put_grads("0ffn", gl)
        dX1, dnw[i][1], dsh2, dsc2, dy, dg1 = _norm_bwd(
            dX, dh2, sv["X1"], nw[i, 1], mod, 1, n_lat, f"norm_bwd_ffn{i}",
            gated=(sv["y"], mod, 0, F32 if kind == 0 else BF16))
        h = sv["h"]
        if kind == 0:
            dh, dpw, dpb, dps = _pool_bwd(h, dy, W["pool_w"], P["pool_b"][j:j + 1], P["pool_scale"][j:j + 1],
                                          n_lat, s_len, l_len, f"pool_bwd{i}")
            gl["pool_w"], G["pool_b"][j], G["pool_scale"][j] = dpw, dpb[0], dps[0]
        elif kind == 1:
            do = _mm(dy, W["attn_w_o"], "nt", F32, f"attn_out_dx{i}")
            gl["attn_w_o"] = _mm(sv["o"], dy, "tn", BF16, f"attn_out_dw{i}")
            dq, dk, dv = _flash_bwd(sv["q"], sv["k"], sv["v"], sv["o"], sv["lse"], do, s_len, l_len)
            dqkv, dqg, dkg = _qk_prep_bwd(sv["qkv"], dq, dk, dv, P["attn_q_gain"][j:j + 1],
                                          P["attn_k_gain"][j:j + 1], cos, sin)
            G["attn_q_gain"][j], G["attn_k_gain"][j] = dqg[0], dkg[0]
            dh = _mm(dqkv, W["attn_w_qkv"], "nn", F32, f"qkv_dx{i}")
            gl["attn_w_qkv"] = _mm(dqkv, h, "tn", BF16, f"qkv_dw{i}")
        else:
            dgated = _mm(dy, W["ret_w_out"], "nt", F32, f"ret_out_dx{i}")
            gl["ret_w_out"] = _mm(sv["gated"], dy, "tn", BF16, f"ret_out_dw{i}")
            do, dg, dgn = _readout_bwd(sv["o2"], sv["proj"], P["ret_gn_w"][j:j + 1], dgated, n_lat)
            dq2, dk2, dv2, dlg = _ret_bwd(sv["proj"], lgt, sv["states"], do, s_len, l_len)
            dproj = _ret_dproj(dq2, dk2, dv2, dg, n_lat)
            G["ret_gn_w"][j], G["ret_decay_logit"][j] = dgn[0], dlg[:, :, 0, 0]
            dh = _mm(dproj, W["ret_w_in"], "nn", F32, f"ret_in_dx{i}")
            gl["ret_w_in"] = _mm(dproj, h, "tn", BF16, f"ret_in_dw{i}")
        zero = put_grads(str(i) if i > 0 else "0mix", gl)
        if i > 0:
            mods[i - 1] = mods[i - 1] + zero
            dX, dnw[i][0], dsh1, dsc1, df_below, dg2_below = _norm_bwd(
                dX1, dh, sv["X"], nw[i, 0], mod, 0, n_lat, f"norm_bwd_mix{i}",
                gated=(saved[i - 1]["f"], mods[i - 1], 1, BF16))
        else:
            dX, dnw[i][0], dsh1, dsc1 = _norm_bwd(dX1, dh, sv["X"], nw[i, 0], mod, 0, n_lat, f"norm_bwd_mix{i}")
        dmods[i] = jnp.concatenate([dsh1, dsc1, dg1, dsh2, dsc2, dg2], axis=1)
        if i > 0:
            df, dg2 = df_below, dg2_below
    grads = {name: jnp.stack(parts) for name, parts in G.items()}
    grads["norm_w"] = jnp.stack([jnp.concatenate(pair, axis=0) for pair in dnw])
    return loss, dX, grads, jnp.stack(dmods)


def kernel(x, c, ctx, c_ctx, ada_w, ada_b, norm_w, pool_w, pool_b, pool_scale, attn_w_qkv, attn_q_gain,
           attn_k_gain, attn_w_o, ret_w_in, ret_decay_logit, ret_gn_w, ret_w_out, ffn_w_up, ffn_conv_w,
           ffn_conv_b, ffn_w_down, loss_target, m_c_ctx, m_ada_w, m_ada_b, m_norm_w, m_pool_w, m_pool_b,
           m_pool_scale, m_attn_w_qkv, m_attn_q_gain, m_attn_k_gain, m_attn_w_o, m_ret_w_in,
           m_ret_decay_logit, m_ret_gn_w, m_ret_w_out, m_ffn_w_up, m_ffn_conv_w, m_ffn_conv_b, m_ffn_w_down,
           v_c_ctx, v_ada_w, v_ada_b, v_norm_w, v_pool_w, v_pool_b, v_pool_scale, v_attn_w_qkv, v_attn_q_gain,
           v_attn_k_gain, v_attn_w_o, v_ret_w_in, v_ret_decay_logit, v_ret_gn_w, v_ret_w_out, v_ffn_w_up,
           v_ffn_conv_w, v_ffn_conv_b, v_ffn_w_down):
    A = dict(locals())
    me = _index(_position())
    s_len, D = x.shape[1], x.shape[2]
    l_len = ctx.shape[1]
    assert s_len % ROW_TILE == 0 and l_len % ROW_TILE == 0 and s_len % GRID_W == 0

    small = [A[n] for n, _ in SMALL_SHARDED]
    first_parts = [c] + small + [pool_w[0]]
    got = _gather_small(_pack_rows(first_parts, 128, F32), "gather_c_small")
    parts = _unpack_rows(got, [a.shape for a in first_parts])
    c_all = parts[0].reshape(N_DEV, D)
    P = {n: _unshard(g8, ax) for (n, ax), g8 in zip(SMALL_SHARDED, parts[1:-1])}
    first_pool_w = _unshard(parts[-1], 1).astype(BF16)

    c_rows = jnp.concatenate([c_all, c_ctx.reshape(1, D), jnp.zeros((7, D), F32)], axis=0)
    cols = ada_w.shape[2]
    ada_b_shard = lax.dynamic_slice_in_dim(ada_b, me * cols, cols, axis=1).reshape(DEPTH, 1, cols)
    mod_shard = _ada_fwd(c_rows, ada_w, ada_b_shard)
    got = _gather_small(mod_shard.reshape(-1, 128), "gather_mod").reshape(N_DEV, DEPTH, 16, cols)
    mod_lat = lax.dynamic_index_in_dim(got, me, axis=2, keepdims=False)
    mod_ctx = got[:, :, 8, :]
    mods = jnp.stack([jnp.moveaxis(mod_lat, 0, 1).reshape(DEPTH, 6, D),
                      jnp.moveaxis(mod_ctx, 0, 1).reshape(DEPTH, 6, D)], axis=1)

    shards = [[_shard_to_send(A[n][j], kind) for n, j, kind in group] for group in GATHER_GROUPS]
    flights, landed = {}, {}

    def start_gather(i, ready):
        mine, _ = lax.optimization_barrier((shards[i], ready))
        flights[i] = _split_start(mine, True, f"gather_start{i}")
        return flights[i][3][0, 0]

    mods = [mods[i] for i in range(DEPTH)]
    mods[0] = mods[0] + start_gather(0, mods[0])
    for n in REPLICATED:
        P[n] = A[n]

    def get_weights(i, part, x_now):
        if i == 0 and part == "mix":
            return {"pool_w": first_pool_w}, 0.0
        if i in landed:
            return landed[i], 0.0
        owns, lands = _split_wait(flights[i], x_now, f"gather_wait{i}")
        landed[i] = {n: _full_from_land(lax.dynamic_update_index_in_dim(land, own, me, axis=0), kind)
                     for (n, j, kind), own, land in zip(GATHER_GROUPS[i], owns, lands)}
        return landed[i], (start_gather(i + 1, lands) if i + 1 < DEPTH else 0.0)

    sent = {}

    def put_grads(group, gl):
        sent[group] = _split_start([_grad_to_send(gl[n], kind) for n, j, kind in GRAD_GROUPS[group]], False,
                                   f"exchange_start_{group}")
        return sent[group][3][0, 0]

    x0 = jnp.concatenate([x[0], ctx[0]], axis=0)
    loss8, dx0, G, dmods = _local_step(x0, loss_target[0], mods, P, get_weights, put_grads, s_len, l_len)
    loss = lax.psum(loss8[0, 0], ("x", "y", "c"))
    grad_x = dx0[:s_len].reshape(x.shape)

    small_names = ["dmods"] + list(REPLICATED[1:]) + [n for n, _ in SMALL_SHARDED]
    small_parts = [dmods] + [G[n] for n in small_names[1:]]
    got = _gather_small(_pack_rows(small_parts, 128, F32), "gather_small_grads")
    S8 = dict(zip(small_names, _unpack_rows(got, [a.shape for a in small_parts])))

    dm = S8["dmods"].reshape(N_DEV, DEPTH, 2, 6 * D)
    dm_mine = lax.dynamic_slice_in_dim(dm, me * cols, cols, axis=3)
    g_ada_w, pc = _ada_bwd(c_rows, ada_w, jnp.moveaxis(dm_mine[:, :, 0], 0, 1), jnp.moveaxis(dm_mine[:, :, 1], 0, 1))
    pc8 = _gather_small(pc.reshape(-1, 128), "gather_c_ctx_grad").reshape(N_DEV, 8, D)

    def owner_sums(group, after):
        sends, lands = _split_wait(sent[group], after, f"exchange_wait_{group}")
        out = {}
        for (n, j, kind), send, land in zip(GRAD_GROUPS[group], sends, lands):
            own = lax.dynamic_index_in_dim(send, me, axis=0, keepdims=False)
            out[(n, j)] = _shard_grad(_sum_slots(own, land, f"sum_slots_{n}{j}"), kind)
        return out

    shard_grads = {}
    for group in ("3", "2", "1", "0mix"):
        shard_grads.update(owner_sums(group, pc8))

    g_in = {"c_ctx": pc8[:, 0, :], "ada_w": g_ada_w[None],
            "ada_b": jnp.moveaxis(dm, 2, 1).reshape(2 * N_DEV, DEPTH, 6 * D)}
    for n in REPLICATED[1:]:
        g_in[n] = S8[n]
    for n, ax in SMALL_SHARDED:
        g_in[n] = _my_shard(S8[n], ax, me)

    def stacked(n):
        return jnp.stack([shard_grads[(n, j)] for j in range(A[n].shape[0])])[None]

    late = [n for n, j, kind in GRAD_GROUPS["0ffn"]]
    for n in BIG_WEIGHTS:
        if n not in late:
            g_in[n] = stacked(n)
    res = {n: _adamw(A[n], g_in[n], A["m_" + n], A["v_" + n], "adamw_" + n) for n in WEIGHT_ORDER if n not in late}
    done = sum(res[n][1].reshape(-1)[0] for n in res)
    shard_grads.update(owner_sums("0ffn", done.reshape(1, 1)))
    for n in late:
        res[n] = _adamw(A[n], stacked(n), A["m_" + n], A["v_" + n], "adamw_" + n)
    outs = [loss, grad_x]
    for slot in range(4):
        outs += [res[n][slot] for n in WEIGHT_ORDER]
    return tuple(outs)
```
